```python
import jax, jax.numpy as jnp
from jax import lax
import numpy as np

D_MODEL = 1024
BATCH = 16
SEQ = 256
DEPTH = 2
DEC_BATCH = 2
DEC_SEQ = 1024
PAST_LEN = 256

GRID_W = 64
N_EVEN = (DEPTH + 1) // 2
N_ODD = DEPTH // 2
POOL_WINDOWS = (2, 4, 8, 16)
POOL_GROUPS = 4
POOL_GROUP_DIM = D_MODEL // 8
POOL_DIM = POOL_GROUPS * POOL_GROUP_DIM
HEAD_DIM = 64
N_Q_HEADS = (D_MODEL // 2) // HEAD_DIM
N_KV_HEADS = 2
Q_PER_KV = N_Q_HEADS // N_KV_HEADS
Q_DIM = N_Q_HEADS * HEAD_DIM
KV_DIM = N_KV_HEADS * HEAD_DIM
ATTN_WINDOW = 128
ATTN_BLOCK = 128
ATTN_SCALE = HEAD_DIM ** -0.5
ROPE_BASE = 10000.0
ROPE_AXIS_DIM = HEAD_DIM // 2
IN_AB = POOL_DIM + Q_DIM + 2 * KV_DIM
MIX_AB = POOL_DIM + Q_DIM
CONV_DIM = D_MODEL
CONV_WIDTH = 31
N_GROUPS = 4
EXPERTS_PER_GROUP = 4
N_EXPERTS = N_GROUPS * EXPERTS_PER_GROUP
TOP_K_IN_GROUP = 2
EXPERT_HIDDEN = D_MODEL // 2
EPS = 1e-6
NEG_BIG = -1e30

kernel_name = "hybrid_pool_swa_conformer_hmoe_dit_step"


def rmsnorm(x, g):
    xf = x.astype(jnp.float32)
    y = xf * lax.rsqrt(jnp.mean(xf * xf, axis=-1, keepdims=True) + EPS)
    return (y * g.astype(jnp.float32)).astype(x.dtype)


def layernorm(x, g, b):
    xf = x.astype(jnp.float32)
    mu = jnp.mean(xf, axis=-1, keepdims=True)
    var = jnp.mean(jnp.square(xf - mu), axis=-1, keepdims=True)
    y = (xf - mu) * lax.rsqrt(var + EPS) * g.astype(jnp.float32) + b.astype(jnp.float32)
    return y.astype(x.dtype)


def modulation(cond, w_ada, b_ada):
    m = jax.nn.silu(cond) @ w_ada + b_ada
    return jnp.split(m[:, None, :], 6, axis=-1)


def centred_mean(u, window):
    n = u.shape[1]
    lo = window // 2
    hi = window - lo - 1
    uf = u.astype(jnp.float32)
    cs = jnp.concatenate([jnp.zeros_like(uf[:, :1]), jnp.cumsum(uf, axis=1)], axis=1)
    t = jnp.arange(n)
    start = jnp.clip(t - lo, 0, n)
    end = jnp.clip(t + hi + 1, 0, n)
    total = cs[:, end] - cs[:, start]
    count = (end - start).astype(jnp.float32)[None, :, None]
    return (total / count).astype(u.dtype)


def pool_mixer(u, w_pool, pool_scale):
    b, n, _ = u.shape
    ug = u.reshape(b, n, POOL_GROUPS, POOL_GROUP_DIM)
    pooled = jnp.stack([centred_mean(ug[:, :, g], w) - ug[:, :, g]
                        for g, w in enumerate(POOL_WINDOWS)], axis=2)
    mixed = jnp.einsum('bngc,gcd->bngd', pooled, w_pool)
    return mixed.reshape(b, n, POOL_DIM) * pool_scale


def axial_rope_tables(n):
    rows = n // GRID_W
    t = jnp.arange(rows * GRID_W)
    row = (t // GRID_W).astype(jnp.float32)
    col = (t % GRID_W).astype(jnp.float32)
    inv = ROPE_BASE ** (-jnp.arange(0, ROPE_AXIS_DIM, 2, dtype=jnp.float32) / ROPE_AXIS_DIM)

    def table(p):
        ang = p[:, None] * inv[None, :]
        ang = jnp.concatenate([ang, ang], axis=-1)
        return jnp.cos(ang)[None, :, None, :], jnp.sin(ang)[None, :, None, :]

    return table(row), table(col)


def rotate(x, cos, sin):
    half = x.shape[-1] // 2
    rot = jnp.concatenate([-x[..., half:], x[..., :half]], axis=-1)
    return x * cos + rot * sin


def apply_axial_rope(x, tables):
    (cr, sr), (cc, sc) = tables
    xf = x.astype(jnp.float32)
    xr = rotate(xf[..., :ROPE_AXIS_DIM], cr, sr)
    xc = rotate(xf[..., ROPE_AXIS_DIM:], cc, sc)
    return jnp.concatenate([xr, xc], axis=-1).astype(x.dtype)


def split_ab(proj):
    b, n, _ = proj.shape
    a = proj[..., :POOL_DIM]
    q = proj[..., POOL_DIM:POOL_DIM + Q_DIM].reshape(b, n, N_Q_HEADS, HEAD_DIM)
    k = proj[..., POOL_DIM + Q_DIM:POOL_DIM + Q_DIM + KV_DIM].reshape(b, n, N_KV_HEADS, HEAD_DIM)
    v = proj[..., POOL_DIM + Q_DIM + KV_DIM:].reshape(b, n, N_KV_HEADS, HEAD_DIM)
    return a, q, k, v


def sink_logits(sink, shape):
    s = sink.astype(jnp.float32).reshape((1,) * (len(shape) - 4) + (N_KV_HEADS, Q_PER_KV, 1, 1))
    return jnp.broadcast_to(s, shape[:-1] + (1,))


def context_attention(q, k, v, sink):
    b, l = q.shape[0], q.shape[1]
    qg = q.reshape(b, l, N_KV_HEADS, Q_PER_KV, HEAD_DIM)
    s = jnp.einsum('bqhgd,bkhd->bhgqk', qg, k).astype(jnp.float32) * ATTN_SCALE
    p = jax.nn.softmax(jnp.concatenate([s, sink_logits(sink, s.shape)], axis=-1), axis=-1)[..., :-1]
    o = jnp.einsum('bhgqk,bkhd->bqhgd', p.astype(v.dtype), v)
    return o.reshape(b, l, Q_DIM)


def latent_window_attention(q, k, v, ck, cv, sink):
    b, n = q.shape[0], q.shape[1]
    nb = n // ATTN_BLOCK
    lc = ck.shape[1]
    qb = q.reshape(b, nb, ATTN_BLOCK, N_KV_HEADS, Q_PER_KV, HEAD_DIM)

    def band(x):
        xp = jnp.pad(x, ((0, 0), (ATTN_BLOCK, ATTN_BLOCK), (0, 0), (0, 0)))
        xp = xp.reshape(b, nb + 2, ATTN_BLOCK, N_KV_HEADS, HEAD_DIM)
        return jnp.concatenate([xp[:, :-2], xp[:, 1:-1], xp[:, 2:]], axis=2)

    kw, vw = band(k), band(v)
    blk = jnp.arange(nb)[:, None]
    qpos = blk * ATTN_BLOCK + jnp.arange(ATTN_BLOCK)[None, :]
    kpos = (blk - 1) * ATTN_BLOCK + jnp.arange(3 * ATTN_BLOCK)[None, :]
    valid = ((jnp.abs(qpos[:, :, None] - kpos[:, None, :]) <= ATTN_WINDOW)
             & (kpos[:, None, :] >= 0) & (kpos[:, None, :] < n))
    s_loc = jnp.einsum('bnqhgd,bnkhd->bnhgqk', qb, kw).astype(jnp.float32) * ATTN_SCALE
    s_loc = jnp.where(valid[None, :, None, None], s_loc, NEG_BIG)
    s_ctx = jnp.einsum('bnqhgd,bkhd->bnhgqk', qb, ck).astype(jnp.float32) * ATTN_SCALE
    p = jax.nn.softmax(jnp.concatenate([s_loc, s_ctx, sink_logits(sink, s_loc.shape)], axis=-1), axis=-1)
    wl = 3 * ATTN_BLOCK
    p_loc = p[..., :wl].astype(v.dtype)
    p_ctx = p[..., wl:wl + lc].astype(v.dtype)
    o = (jnp.einsum('bnhgqk,bnkhd->bnqhgd', p_loc, vw)
         + jnp.einsum('bnhgqk,bkhd->bnqhgd', p_ctx, cv))
    return o.reshape(b, n, Q_DIM)


def even_mixer_context(h, w_in, w_pool, pool_scale, sink, w_out):
    a, q, k, v = split_ab(h @ w_in)
    y = jnp.concatenate([pool_mixer(a, w_pool, pool_scale), context_attention(q, k, v, sink)], axis=-1)
    return y @ w_out, k, v


def even_mixer_latent(h, ck, cv, w_in, w_pool, pool_scale, sink, w_out):
    a, q, k, v = split_ab(h @ w_in)
    tables = axial_rope_tables(h.shape[1])
    q = apply_axial_rope(q, tables)
    k = apply_axial_rope(k, tables)
    y = jnp.concatenate([pool_mixer(a, w_pool, pool_scale),
                         latent_window_attention(q, k, v, ck, cv, sink)], axis=-1)
    return y @ w_out


def conformer_conv(h, w1, b1, w_dw, b_dw, ln_g, ln_b, w2, b2):
    u = h @ w1 + b1
    u = u[..., :CONV_DIM] * jax.nn.sigmoid(u[..., CONV_DIM:])
    u = lax.conv_general_dilated(u, w_dw[:, None, :], window_strides=(1,),
                                 padding=[(CONV_WIDTH // 2, CONV_WIDTH // 2)],
                                 dimension_numbers=('NWC', 'WIO', 'NWC'),
                                 feature_group_count=CONV_DIM) + b_dw
    u = jax.nn.silu(layernorm(u, ln_g, ln_b))
    return u @ w2 + b2


def hierarchical_moe(h, w_grp, w_exp, w1, w3, w2):
    b, n, d = h.shape
    t = h.reshape(b * n, d)
    g_prob = jax.nn.softmax((t @ w_grp).astype(jnp.float32), axis=-1)
    g_idx = jnp.argmax(g_prob, axis=-1)
    g_w = jnp.take_along_axis(g_prob, g_idx[:, None], axis=-1)
    e_logits = jnp.einsum('td,gde->tge', t, w_exp).astype(jnp.float32)
    e_logits = jnp.take_along_axis(e_logits, g_idx[:, None, None], axis=1)[:, 0]
    top_v, top_i = lax.top_k(e_logits, TOP_K_IN_GROUP)
    e_w = jax.nn.softmax(top_v, axis=-1) * g_w
    expert = g_idx[:, None] * EXPERTS_PER_GROUP + top_i
    gate = jnp.sum(jax.nn.one_hot(expert, N_EXPERTS, dtype=jnp.float32) * e_w[..., None], axis=1)
    hg = jnp.einsum('td,edf->tef', t, w1)
    hu = jnp.einsum('td,edf->tef', t, w3)
    a = jax.nn.silu(hg) * hu * gate[..., None].astype(t.dtype)
    y = jnp.einsum('tef,efd->td', a, w2)
    return y.reshape(b, n, d)


def setup_inputs(seed: int = 0) -> dict:
    key = jax.random.key(seed)
    ks = iter(jax.random.split(key, 40))

    def nrm(shape, scale):
        return jax.random.normal(next(ks), shape, jnp.float32) * scale

    D = D_MODEL
    return {
        "x_prompt": nrm((BATCH, SEQ, D), 1.0),
        "x_sample": nrm((DEC_BATCH, DEC_SEQ, D), 1.0),
        "cache_k": nrm((DEC_BATCH, N_EVEN, PAST_LEN, N_KV_HEADS, HEAD_DIM), 1.0),
        "cache_v": nrm((DEC_BATCH, N_EVEN, PAST_LEN, N_KV_HEADS, HEAD_DIM), 1.0),
        "c": nrm((DEC_BATCH, D), 1.0),
        "c_ctx": nrm((D,), 1.0),
        "w_ada": nrm((DEPTH, D, 6 * D), 0.5 * D ** -0.5),
        "b_ada": nrm((DEPTH, 6 * D), 0.02),
        "norm_mix_g": 1.0 + nrm((DEPTH, D), 0.05),
        "norm_ffn_g": 1.0 + nrm((DEPTH, D), 0.05),
        "w_in_ab": nrm((N_EVEN, D, IN_AB), D ** -0.5),
        "pool_w": nrm((N_EVEN, POOL_GROUPS, POOL_GROUP_DIM, POOL_GROUP_DIM), POOL_GROUP_DIM ** -0.5),
        "pool_scale": 1.0 + nrm((N_EVEN, POOL_DIM), 0.1),
        "attn_sink": nrm((N_EVEN, N_Q_HEADS), 0.5),
        "w_out_ab": nrm((N_EVEN, MIX_AB, D), MIX_AB ** -0.5),
        "conv_w1": nrm((N_ODD, D, 2 * CONV_DIM), D ** -0.5),
        "conv_b1": nrm((N_ODD, 2 * CONV_DIM), 0.02),
        "conv_dw": nrm((N_ODD, CONV_WIDTH, CONV_DIM), CONV_WIDTH ** -0.5),
        "conv_dw_b": nrm((N_ODD, CONV_DIM), 0.02),
        "conv_ln_g": 1.0 + nrm((N_ODD, CONV_DIM), 0.05),
        "conv_ln_b": nrm((N_ODD, CONV_DIM), 0.02),
        "conv_w2": nrm((N_ODD, CONV_DIM, D), CONV_DIM ** -0.5),
        "conv_b2": nrm((N_ODD, D), 0.02),
        "router_grp": nrm((DEPTH, D, N_GROUPS), D ** -0.5),
        "router_exp": nrm((DEPTH, N_GROUPS, D, EXPERTS_PER_GROUP), D ** -0.5),
        "moe_w1": nrm((DEPTH, N_EXPERTS, D, EXPERT_HIDDEN), D ** -0.5),
        "moe_w3": nrm((DEPTH, N_EXPERTS, D, EXPERT_HIDDEN), D ** -0.5),
        "moe_w2": nrm((DEPTH, N_EXPERTS, EXPERT_HIDDEN, D), EXPERT_HIDDEN ** -0.5),
        "final_g": 1.0 + nrm((D,), 0.05),
    }


def reference(x_prompt, x_sample, cache_k, cache_v, c, c_ctx, w_ada, b_ada, norm_mix_g, norm_ffn_g,
              w_in_ab, pool_w, pool_scale, attn_sink, w_out_ab, conv_w1, conv_b1, conv_dw, conv_dw_b,
              conv_ln_g, conv_ln_b, conv_w2, conv_b2, router_grp, router_exp, moe_w1, moe_w3, moe_w2,
              final_g):
    def channel_mixer(x, l, shift, scale, gate):
        h = rmsnorm(x, norm_ffn_g[l]) * (1 + scale) + shift
        return x + gate * hierarchical_moe(h, router_grp[l], router_exp[l], moe_w1[l], moe_w3[l], moe_w2[l])

    def odd_mixer(h, o):
        return conformer_conv(h, conv_w1[o], conv_b1[o], conv_dw[o], conv_dw_b[o],
                              conv_ln_g[o], conv_ln_b[o], conv_w2[o], conv_b2[o])

    x = x_prompt
    ks, vs = [], []
    for l in range(DEPTH):
        sh1, sc1, g1, sh2, sc2, g2 = modulation(c_ctx[None, :], w_ada[l], b_ada[l])
        h = rmsnorm(x, norm_mix_g[l]) * (1 + sc1) + sh1
        if l % 2 == 0:
            e = l // 2
            y, k, v = even_mixer_context(h, w_in_ab[e], pool_w[e], pool_scale[e], attn_sink[e], w_out_ab[e])
            ks.append(k)
            vs.append(v)
        else:
            y = odd_mixer(h, l // 2)
        x = x + g1 * y
        x = channel_mixer(x, l, sh2, sc2, g2)
    y_prompt = rmsnorm(x, final_g)
    state_k = jnp.stack(ks, axis=1)
    state_v = jnp.stack(vs, axis=1)

    x = x_sample
    for l in range(DEPTH):
        sh1, sc1, g1, sh2, sc2, g2 = modulation(c, w_ada[l], b_ada[l])
        h = rmsnorm(x, norm_mix_g[l]) * (1 + sc1) + sh1
        if l % 2 == 0:
            e = l // 2
            y = even_mixer_latent(h, cache_k[:, e], cache_v[:, e], w_in_ab[e], pool_w[e], pool_scale[e],
                                  attn_sink[e], w_out_ab[e])
        else:
            y = odd_mixer(h, l // 2)
        x = x + g1 * y
        x = channel_mixer(x, l, sh2, sc2, g2)
    y_sample = rmsnorm(x, final_g)

    return (y_prompt, y_sample, state_k, state_v)
```

```python
import functools

import jax
import jax.numpy as jnp
from jax import lax
from jax.experimental import pallas as pl
from jax.experimental.pallas import tpu as pltpu

F32 = jnp.float32
BF16 = jnp.bfloat16

D = 1024
N_CTX_SEQ = 16
CTX_LEN = 256
N_LAT_SEQ = 2
LAT_LEN = 1024
T_CTX = N_CTX_SEQ * CTX_LEN
T_LAT = N_LAT_SEQ * LAT_LEN
T_ALL = T_CTX + T_LAT
TM = 256
NT = T_ALL // TM
NT_CTX = T_CTX // TM
LAT_TILES_PER_SEQ = LAT_LEN // TM
GRID_W = 64

POOL_WINDOWS = (2, 4, 8, 16)
POOL_GROUP_DIM = 128
POOL_DIM = 512
HEAD_DIM = 64
N_Q_HEADS = 8
N_KV_HEADS = 2
Q_PER_KV = 4
Q_DIM = 512
KV_DIM = 128
IN_AB = 1280
ATTN_WINDOW = 128
ATTN_BLOCK = 128
ATTN_SCALE = HEAD_DIM ** -0.5
ROPE_BASE = 10000.0
ROPE_AXIS_DIM = 32
CONV_WIDTH = 31
CONV_HALO = 16
POOL_HALO = 8
N_GROUPS = 4
EPG = 4
N_EXPERTS = 16
EXPERT_HIDDEN = 512
EPS = 1e-6
NEG_BIG = -1e30

SUBLANES = 8
LANES = 128
TOK_ROWS = D // LANES

MOE_TM = 256
VMEM_LIMIT = 56 * 1024 * 1024


def _silu(x):
    return x * (1.0 / (1.0 + jnp.exp(-x)))


def _mod_row(i):
    return jnp.where(i < NT_CTX, 0, 1 + (i - NT_CTX) // LAT_TILES_PER_SEQ)


def _seq_info(i):
    is_ctx = i < NT_CTX
    k = (i - NT_CTX) % LAT_TILES_PER_SEQ
    off = jnp.where(is_ctx, 0, k * TM)
    n = jnp.where(is_ctx, CTX_LEN, LAT_LEN)
    first = jnp.logical_or(is_ctx, k == 0)
    last = jnp.logical_or(is_ctx, k == LAT_TILES_PER_SEQ - 1)
    return off, n, first, last


def _rms_mod(x, g, scale, shift):
    ms = jnp.mean(x * x, axis=-1, keepdims=True)
    return (x * lax.rsqrt(ms + EPS) * g) * (1.0 + scale) + shift


def _dot(a, b):
    return jnp.dot(a.astype(BF16), b.astype(BF16), preferred_element_type=F32)


def _dot_nt(a, b):
    return lax.dot_general(a.astype(BF16), b.astype(BF16), (((1,), (1,)), ((), ())),
                           preferred_element_type=F32)


def _from_token_tiles(ref):
    return jnp.concatenate([ref[pl.ds(s, TM, stride=TOK_ROWS), :] for s in range(TOK_ROWS)], axis=1)


MOD_TN = 1536


def _mod_kernel(cond_ref, w_ref, b_ref, o_ref):
    s = _silu(cond_ref[...])
    o_ref[0] = _dot(s, w_ref[0]) + b_ref[0]


def _modulation(cond8, w_ada, b_ada):
    depth = w_ada.shape[0]
    return pl.pallas_call(
        _mod_kernel,
        grid=(depth, 6 * D // MOD_TN),
        in_specs=[
            pl.BlockSpec((SUBLANES, D), lambda l, n: (0, 0)),
            pl.BlockSpec((1, D, MOD_TN), lambda l, n: (l, 0, n)),
            pl.BlockSpec((1, 1, MOD_TN), lambda l, n: (l, 0, n)),
        ],
        out_specs=pl.BlockSpec((1, SUBLANES, MOD_TN), lambda l, n: (l, 0, n)),
        out_shape=jax.ShapeDtypeStruct((depth, SUBLANES, 6 * D), F32),
        compiler_params=pltpu.CompilerParams(vmem_limit_bytes=VMEM_LIMIT),
        name="modulation",
    )(cond8, w_ada, b_ada.reshape(depth, 1, 6 * D))


def _rope_chunk(xc, cos, sin):
    lane = lax.broadcasted_iota(jnp.int32, xc.shape, 1)
    first = (lane % ROPE_AXIS_DIM) < (ROPE_AXIS_DIM // 2)
    rot = jnp.where(first, -pltpu.roll(xc, LANES - ROPE_AXIS_DIM // 2, 1), pltpu.roll(xc, ROPE_AXIS_DIM // 2, 1))
    return xc * cos + rot * sin


def _in_even_kernel(x_ref, mod_ref, g_ref, w_ref, cos_ref, sin_ref, a_ref, q_ref, k_ref, v_ref):
    i = pl.program_id(0)
    r = _mod_row(i)
    shift = mod_ref[pl.ds(r, 1), 0:D]
    scale = mod_ref[pl.ds(r, 1), D:2 * D]
    h = _rms_mod(x_ref[...], g_ref[...], scale, shift)
    proj = _dot(h, w_ref[...])
    a_ref[...] = proj[:, :POOL_DIM]
    v_ref[...] = proj[:, POOL_DIM + Q_DIM + KV_DIM:]

    @pl.when(i < NT_CTX)
    def _():
        q_ref[...] = proj[:, POOL_DIM:POOL_DIM + Q_DIM]
        k_ref[...] = proj[:, POOL_DIM + Q_DIM:POOL_DIM + Q_DIM + KV_DIM]

    @pl.when(i >= NT_CTX)
    def _():
        off, _, _, _ = _seq_info(i)
        off = pl.multiple_of(off, TM)
        cos = cos_ref[pl.ds(off, TM), :]
        sin = sin_ref[pl.ds(off, TM), :]
        for c in range(Q_DIM // LANES):
            lo = POOL_DIM + c * LANES
            q_ref[:, c * LANES:(c + 1) * LANES] = _rope_chunk(proj[:, lo:lo + LANES], cos, sin)
        lo = POOL_DIM + Q_DIM
        k_ref[...] = _rope_chunk(proj[:, lo:lo + LANES], cos, sin)


def _in_even(x, mod_l, g, w_in, cos_t, sin_t):
    tile = lambda w: pl.BlockSpec((TM, w), lambda i: (i, 0))
    full = lambda s: pl.BlockSpec(s, lambda i: (0,) * len(s))
    return pl.pallas_call(
        _in_even_kernel,
        grid=(NT,),
        in_specs=[tile(D), full((SUBLANES, 6 * D)), full((1, D)), full((D, IN_AB)),
                  full((LAT_LEN, LANES)), full((LAT_LEN, LANES))],
        out_specs=[tile(POOL_DIM), tile(Q_DIM), tile(KV_DIM), tile(KV_DIM)],
        out_shape=[jax.ShapeDtypeStruct((T_ALL, POOL_DIM), F32), jax.ShapeDtypeStruct((T_ALL, Q_DIM), F32),
                   jax.ShapeDtypeStruct((T_ALL, KV_DIM), F32), jax.ShapeDtypeStruct((T_ALL, KV_DIM), F32)],
        compiler_params=pltpu.CompilerParams(vmem_limit_bytes=VMEM_LIMIT),
        name="in_even",
    )(x, mod_l, g, w_in, cos_t, sin_t)


def _attn_ctx_kernel(sink_ref, q_ref, k_ref, v_ref, o_ref):
    q = q_ref[...]
    k = k_ref[...]
    v = v_ref[...]
    for h in range(N_KV_HEADS):
        kh = k[:, h * HEAD_DIM:(h + 1) * HEAD_DIM]
        vh = v[:, h * HEAD_DIM:(h + 1) * HEAD_DIM]
        for g in range(Q_PER_KV):
            j = h * Q_PER_KV + g
            s = _dot_nt(q[:, j * HEAD_DIM:(j + 1) * HEAD_DIM], kh) * ATTN_SCALE
            sk = sink_ref[0, j]
            m = jnp.maximum(jnp.max(s, axis=-1, keepdims=True), sk)
            p = jnp.exp(s - m)
            denom = jnp.sum(p, axis=-1, keepdims=True) + jnp.exp(sk - m)
            o_ref[:, j * HEAD_DIM:(j + 1) * HEAD_DIM] = _dot(p, vh) / denom


def _attn_ctx(sink, q, k, v):
    tile = lambda w: pl.BlockSpec((TM, w), lambda b: (b, 0))
    return pl.pallas_call(
        _attn_ctx_kernel,
        grid=(N_CTX_SEQ,),
        in_specs=[pl.BlockSpec(memory_space=pltpu.SMEM), tile(Q_DIM), tile(KV_DIM), tile(KV_DIM)],
        out_specs=tile(Q_DIM),
        out_shape=jax.ShapeDtypeStruct((T_CTX, Q_DIM), F32),
        compiler_params=pltpu.CompilerParams(vmem_limit_bytes=VMEM_LIMIT),
        name="attn_ctx",
    )(sink, q, k, v)


LAT_BLOCKS = LAT_LEN // ATTN_BLOCK
BAND = 3 * ATTN_BLOCK


def _attn_lat_kernel(sink_ref, q_ref, k_ref, v_ref, ck_ref, cv_ref, o_ref):
    n = pl.program_id(1)
    start = jnp.clip((n - 1) * ATTN_BLOCK, 0, LAT_LEN - BAND)
    start = pl.multiple_of(start, ATTN_BLOCK)
    q = q_ref[...]
    kw = k_ref[pl.ds(start, BAND), :]
    vw = v_ref[pl.ds(start, BAND), :]
    ck = ck_ref[0]
    cv = cv_ref[0]
    qpos = n * ATTN_BLOCK + lax.broadcasted_iota(jnp.int32, (ATTN_BLOCK, BAND), 0)
    kpos = start + lax.broadcasted_iota(jnp.int32, (ATTN_BLOCK, BAND), 1)
    valid = jnp.abs(qpos - kpos) <= ATTN_WINDOW
    for h in range(N_KV_HEADS):
        hs = slice(h * HEAD_DIM, (h + 1) * HEAD_DIM)
        for g in range(Q_PER_KV):
            j = h * Q_PER_KV + g
            qj = q[:, j * HEAD_DIM:(j + 1) * HEAD_DIM]
            s_loc = jnp.where(valid, _dot_nt(qj, kw[:, hs]) * ATTN_SCALE, NEG_BIG)
            s_ctx = _dot_nt(qj, ck[:, hs]) * ATTN_SCALE
            sk = sink_ref[0, j]
            m = jnp.maximum(jnp.maximum(jnp.max(s_loc, axis=-1, keepdims=True),
                                        jnp.max(s_ctx, axis=-1, keepdims=True)), sk)
            p_loc = jnp.exp(s_loc - m)
            p_ctx = jnp.exp(s_ctx - m)
            denom = (jnp.sum(p_loc, axis=-1, keepdims=True) + jnp.sum(p_ctx, axis=-1, keepdims=True)
                     + jnp.exp(sk - m))
            o = _dot(p_loc, vw[:, hs]) + _dot(p_ctx, cv[:, hs])
            o_ref[:, j * HEAD_DIM:(j + 1) * HEAD_DIM] = o / denom


def _attn_lat(sink, q, k, v, ck, cv):
    past = ck.shape[1]
    q_blk0 = T_CTX // ATTN_BLOCK
    kv_blk0 = T_CTX // LAT_LEN
    return pl.pallas_call(
        _attn_lat_kernel,
        grid=(N_LAT_SEQ, LAT_BLOCKS),
        in_specs=[pl.BlockSpec(memory_space=pltpu.SMEM),
                  pl.BlockSpec((ATTN_BLOCK, Q_DIM), lambda b, n: (q_blk0 + b * LAT_BLOCKS + n, 0)),
                  pl.BlockSpec((LAT_LEN, KV_DIM), lambda b, n: (kv_blk0 + b, 0)),
                  pl.BlockSpec((LAT_LEN, KV_DIM), lambda b, n: (kv_blk0 + b, 0)),
                  pl.BlockSpec((1, past, KV_DIM), lambda b, n: (b, 0, 0)),
                  pl.BlockSpec((1, past, KV_DIM), lambda b, n: (b, 0, 0))],
        out_specs=pl.BlockSpec((ATTN_BLOCK, Q_DIM), lambda b, n: (b * LAT_BLOCKS + n, 0)),
        out_shape=jax.ShapeDtypeStruct((T_LAT, Q_DIM), F32),
        compiler_params=pltpu.CompilerParams(vmem_limit_bytes=VMEM_LIMIT),
        name="attn_lat",
    )(sink, q, k, v, ck, cv)


def _router(h2, wr):
    hi = h2.astype(BF16)
    lo = (h2 - hi.astype(F32)).astype(BF16)
    whi = wr.astype(BF16)
    wlo = (wr - whi.astype(F32)).astype(BF16)
    dot = functools.partial(jnp.dot, preferred_element_type=F32)
    logits = dot(hi, whi) + (dot(lo, whi) + dot(hi, wlo))
    lane = lax.broadcasted_iota(jnp.int32, logits.shape, 1).astype(F32)
    neg = jnp.float32(-jnp.inf)
    big = jnp.float32(1e9)
    is_grp = lane < N_GROUPS
    gl = jnp.where(is_grp, logits, neg)
    gmax = jnp.max(gl, axis=-1, keepdims=True)
    gsum = jnp.sum(jnp.where(is_grp, jnp.exp(logits - gmax), 0.0), axis=-1, keepdims=True)
    g_w = 1.0 / gsum
    g_idx = jnp.min(jnp.where(gl == gmax, lane, big), axis=-1, keepdims=True)
    base = N_GROUPS + EPG * g_idx
    in_grp = jnp.logical_and(lane >= base, lane < base + EPG)
    el = jnp.where(in_grp, logits, neg)
    t1 = jnp.max(el, axis=-1, keepdims=True)
    i1 = jnp.min(jnp.where(el == t1, lane, big), axis=-1, keepdims=True)
    el2 = jnp.where(lane == i1, neg, el)
    t2 = jnp.max(el2, axis=-1, keepdims=True)
    i2 = jnp.min(jnp.where(el2 == t2, lane, big), axis=-1, keepdims=True)
    d = jnp.exp(t2 - t1)
    w1 = g_w / (1.0 + d)
    w2 = g_w * d / (1.0 + d)
    out = jnp.where(lane == 0, i1 - N_GROUPS,
                    jnp.where(lane == 1, i2 - N_GROUPS,
                              jnp.where(lane == 2, w1, jnp.where(lane == 3, w2, 0.0))))
    return out


def _mixer_tail(x, y, mod_ref, r, gffn, wr, x1_ref, h2_ref, route_ref):
    g1 = mod_ref[pl.ds(r, 1), 2 * D:3 * D]
    shift2 = mod_ref[pl.ds(r, 1), 3 * D:4 * D]
    scale2 = mod_ref[pl.ds(r, 1), 4 * D:5 * D]
    x1 = x + g1 * y
    x1_ref[...] = x1
    h2 = _rms_mod(x1, gffn, scale2, shift2)
    for s in range(TOK_ROWS):
        h2_ref[pl.ds(s, TM, stride=TOK_ROWS), :] = h2[:, s * LANES:(s + 1) * LANES]
    route_ref[...] = _router(h2, wr)


_TAIL_OUT_SHAPES = [jax.ShapeDtypeStruct((T_ALL, D), F32),
                    jax.ShapeDtypeStruct((T_ALL * TOK_ROWS, LANES), F32),
                    jax.ShapeDtypeStruct((T_ALL, LANES), F32)]


def _tail_out_specs():
    return [pl.BlockSpec((TM, D), lambda i: (i, 0)),
            pl.BlockSpec((TM * TOK_ROWS, LANES), lambda i: (i, 0)),
            pl.BlockSpec((TM, LANES), lambda i: (i, 0))]


def _halo_specs(halo, width):
    per = TM // halo
    last = T_ALL // halo - 1
    prev = pl.BlockSpec((halo, width), lambda i: (jnp.maximum(i * per - 1, 0), 0))
    nxt = pl.BlockSpec((halo, width), lambda i: (jnp.minimum((i + 1) * per, last), 0))
    return prev, nxt


def _mid_even_kernel(a_ref, ap_ref, an_ref, yb_ref, x_ref, mod_ref, wp_ref, ps_ref, wo_ref, gffn_ref, wr_ref,
                     x1_ref, h2_ref, route_ref, pad_ref):
    i = pl.program_id(0)
    r = _mod_row(i)
    off, n, first, last = _seq_info(i)
    a = a_ref[...]
    pad_ref[0:POOL_HALO, :] = jnp.where(first, 0.0, ap_ref[...])
    pad_ref[POOL_HALO:POOL_HALO + TM, :] = a
    pad_ref[POOL_HALO + TM:, :] = jnp.where(last, 0.0, an_ref[...])
    t = off + lax.broadcasted_iota(jnp.int32, (TM, 1), 0)
    mixed = []
    for g, w in enumerate(POOL_WINDOWS):
        lo = w // 2
        hi = w - lo - 1
        cols = slice(g * POOL_GROUP_DIM, (g + 1) * POOL_GROUP_DIM)
        total = pad_ref[POOL_HALO - lo:POOL_HALO - lo + TM, cols]
        for j in range(-lo + 1, hi + 1):
            total = total + pad_ref[POOL_HALO + j:POOL_HALO + j + TM, cols]
        count = (jnp.minimum(t + hi, n - 1) - jnp.maximum(t - lo, 0) + 1).astype(F32)
        pooled = total / count - a[:, cols]
        mixed.append(_dot(pooled, wp_ref[g]))
    ya = jnp.concatenate(mixed, axis=1) * ps_ref[...]
    y = _dot(ya, wo_ref[0:POOL_DIM, :]) + _dot(yb_ref[...], wo_ref[POOL_DIM:, :])
    _mixer_tail(x_ref[...], y, mod_ref, r, gffn_ref[...], wr_ref[...], x1_ref, h2_ref, route_ref)


def _mid_even(a, yb, x, mod_l, w_pool, pool_scale, w_out, gffn, wr):
    tile = lambda w: pl.BlockSpec((TM, w), lambda i: (i, 0))
    full = lambda s: pl.BlockSpec(s, lambda i: (0,) * len(s))
    prev, nxt = _halo_specs(POOL_HALO, POOL_DIM)
    return pl.pallas_call(
        _mid_even_kernel,
        grid=(NT,),
        in_specs=[tile(POOL_DIM), prev, nxt, tile(Q_DIM), tile(D), full((SUBLANES, 6 * D)),
                  full((len(POOL_WINDOWS), POOL_GROUP_DIM, POOL_GROUP_DIM)), full((1, POOL_DIM)),
                  full((D, D)), full((1, D)), full((D, LANES))],
        out_specs=_tail_out_specs(),
        out_shape=_TAIL_OUT_SHAPES,
        scratch_shapes=[pltpu.VMEM((TM + 2 * POOL_HALO, POOL_DIM), F32)],
        compiler_params=pltpu.CompilerParams(vmem_limit_bytes=VMEM_LIMIT),
        name="mid_even",
    )(a, a, a, yb, x, mod_l, w_pool, pool_scale, w_out, gffn, wr)


def _in_odd_kernel(x1_ref, moe_ref, modp_ref, mod_ref, g_ref, w1_ref, b1_ref, x2_ref, u_ref):
    i = pl.program_id(0)
    r = _mod_row(i)
    g2 = modp_ref[pl.ds(r, 1), 5 * D:6 * D]
    x2 = x1_ref[...] + g2 * _from_token_tiles(moe_ref)
    x2_ref[...] = x2
    shift = mod_ref[pl.ds(r, 1), 0:D]
    scale = mod_ref[pl.ds(r, 1), D:2 * D]
    h = _rms_mod(x2, g_ref[...], scale, shift)
    u = _dot(h, w1_ref[...]) + b1_ref[...]
    u_ref[...] = u[:, :D] * (1.0 / (1.0 + jnp.exp(-u[:, D:])))


def _in_odd(x1, moe_tt, mod_prev, mod_l, g, w1, b1):
    tile = lambda w: pl.BlockSpec((TM, w), lambda i: (i, 0))
    full = lambda s: pl.BlockSpec(s, lambda i: (0,) * len(s))
    return pl.pallas_call(
        _in_odd_kernel,
        grid=(NT,),
        in_specs=[tile(D), pl.BlockSpec((TM * TOK_ROWS, LANES), lambda i: (i, 0)),
                  full((SUBLANES, 6 * D)), full((SUBLANES, 6 * D)), full((1, D)), full((D, 2 * D)),
                  full((1, 2 * D))],
        out_specs=[tile(D), tile(D)],
        out_shape=[jax.ShapeDtypeStruct((T_ALL, D), F32), jax.ShapeDtypeStruct((T_ALL, D), F32)],
        compiler_params=pltpu.CompilerParams(vmem_limit_bytes=VMEM_LIMIT),
        name="in_odd",
    )(x1, moe_tt, mod_prev, mod_l, g, w1, b1)


def _mid_odd_kernel(u_ref, up_ref, un_ref, x_ref, mod_ref, dw_ref, dwb_ref, lng_ref, lnb_ref, w2_ref, b2_ref,
                    gffn_ref, wr_ref, x1_ref, h2_ref, route_ref, pad_ref):
    i = pl.program_id(0)
    r = _mod_row(i)
    _, _, first, last = _seq_info(i)
    pad_ref[0:CONV_HALO, :] = jnp.where(first, 0.0, up_ref[...])
    pad_ref[CONV_HALO:CONV_HALO + TM, :] = u_ref[...]
    pad_ref[CONV_HALO + TM:, :] = jnp.where(last, 0.0, un_ref[...])
    half = CONV_WIDTH // 2
    acc = jnp.zeros((TM, D), F32) + dwb_ref[...]
    for j in range(CONV_WIDTH):
        s = CONV_HALO - half + j
        acc = acc + pad_ref[s:s + TM, :] * dw_ref[j:j + 1, :]
    mu = jnp.mean(acc, axis=-1, keepdims=True)
    cen = acc - mu
    var = jnp.mean(cen * cen, axis=-1, keepdims=True)
    v = _silu(cen * lax.rsqrt(var + EPS) * lng_ref[...] + lnb_ref[...])
    y = _dot(v, w2_ref[...]) + b2_ref[...]
    _mixer_tail(x_ref[...], y, mod_ref, r, gffn_ref[...], wr_ref[...], x1_ref, h2_ref, route_ref)


def _mid_odd(u, x, mod_l, dw, dwb, lng, lnb, w2, b2, gffn, wr):
    tile = lambda w: pl.BlockSpec((TM, w), lambda i: (i, 0))
    full = lambda s: pl.BlockSpec(s, lambda i: (0,) * len(s))
    prev, nxt = _halo_specs(CONV_HALO, D)
    dw_pad = jnp.zeros((32, D), F32).at[:CONV_WIDTH].set(dw)
    return pl.pallas_call(
        _mid_odd_kernel,
        grid=(NT,),
        in_specs=[tile(D), prev, nxt, tile(D), full((SUBLANES, 6 * D)), full((32, D)), full((1, D)),
                  full((1, D)), full((1, D)), full((D, D)), full((1, D)), full((1, D)), full((D, LANES))],
        out_specs=_tail_out_specs(),
        out_shape=_TAIL_OUT_SHAPES,
        scratch_shapes=[pltpu.VMEM((TM + 2 * CONV_HALO, D), F32)],
        compiler_params=pltpu.CompilerParams(vmem_limit_bytes=VMEM_LIMIT),
        name="mid_odd",
    )(u, u, u, x, mod_l, dw_pad, dwb, lng, lnb, w2, b2, gffn, wr)


SCATTER_BATCH = 8


def _moe_kernel(te_ref, nact_ref, nval_ref, idx_ref, h_ref, gate_ref, w1_ref, w3_ref, w2_ref, o_ref,
                xs_ref, ys_ref):
    j = pl.program_id(0)

    @pl.when(j == 0)
    def _():
        o_ref[...] = jnp.zeros(o_ref.shape, F32)
        xs_ref[...] = jnp.zeros(xs_ref.shape, F32)

    @pl.when(j < nact_ref[0])
    def _():
        base = j * MOE_TM
        nval = nval_ref[j]

        def tok_rows(row):
            return pl.ds(pl.multiple_of(idx_ref[base + row] * TOK_ROWS, TOK_ROWS), TOK_ROWS)

        def tile_rows(row):
            return pl.ds(pl.multiple_of(row * TOK_ROWS, TOK_ROWS), TOK_ROWS)

        def gather(row, c):
            xs_ref[tile_rows(row), :] = h_ref[tok_rows(row), :]
            return c

        lax.fori_loop(0, nval, gather, 0)
        x = jnp.concatenate([xs_ref[pl.ds(s, MOE_TM, stride=TOK_ROWS), :] for s in range(TOK_ROWS)], axis=1)
        x = x.astype(BF16)
        hg = jnp.dot(x, w1_ref[0].astype(BF16), preferred_element_type=F32)
        hu = jnp.dot(x, w3_ref[0].astype(BF16), preferred_element_type=F32)
        act = _silu(hg) * hu * gate_ref[...]
        y = _dot(act, w2_ref[0])
        for s in range(TOK_ROWS):
            ys_ref[pl.ds(s, MOE_TM, stride=TOK_ROWS), :] = y[:, s * LANES:(s + 1) * LANES]

        def scatter_batch(b, c):
            row0 = b * SCATTER_BATCH
            vals = [o_ref[tok_rows(row0 + k), :] + ys_ref[tile_rows(row0 + k), :] for k in range(SCATTER_BATCH)]
            for k in range(SCATTER_BATCH):
                o_ref[tok_rows(row0 + k), :] = vals[k]
            return c

        nfull = nval // SCATTER_BATCH
        lax.fori_loop(0, nfull, scatter_batch, 0)

        def scatter_one(row, c):
            o_ref[tok_rows(row), :] = o_ref[tok_rows(row), :] + ys_ref[tile_rows(row), :]
            return c

        lax.fori_loop(nfull * SCATTER_BATCH, nval, scatter_one, 0)


def _moe(h_tt, blk, tp, route, w1, w3, w2):
    n_tiles = 2 * tp // MOE_TM + N_EXPERTS
    n_rows = n_tiles * MOE_TM
    e = route[:, 0:2].astype(jnp.int32)
    w = route[:, 2:4]
    member = jnp.sum((e[:, :, None] == jnp.arange(N_EXPERTS)[None, None, :]).astype(jnp.int32), axis=1)
    csum = jnp.cumsum(member, axis=0)
    count = csum[-1]
    rank = csum - member
    padded = ((count + MOE_TM - 1) // MOE_TM) * MOE_TM
    ends = jnp.cumsum(padded)
    starts = ends - padded
    pos = (starts[e] + jnp.take_along_axis(rank, e, axis=1)).reshape(-1)
    tok = jnp.repeat(jnp.arange(tp, dtype=jnp.int32), 2)
    idx = jnp.zeros((n_rows,), jnp.int32).at[pos].set(tok)
    gate = jnp.zeros((n_rows,), F32).at[pos].set(w.reshape(-1)).reshape(n_rows, 1)
    tile_start = jnp.arange(n_tiles, dtype=jnp.int32) * MOE_TM
    n_active = (ends[-1] // MOE_TM).astype(jnp.int32)
    tile_e = jnp.sum((tile_start[:, None] >= ends[None, :]).astype(jnp.int32), axis=1)
    last_e = tile_e[jnp.maximum(n_active - 1, 0)]
    tile_e = jnp.minimum(tile_e, last_e).astype(jnp.int32)
    n_valid = jnp.clip(starts[tile_e] + count[tile_e] - tile_start, 0, MOE_TM).astype(jnp.int32)

    grid_spec = pltpu.PrefetchScalarGridSpec(
        num_scalar_prefetch=4,
        grid=(n_tiles,),
        in_specs=[
            pl.BlockSpec((tp * TOK_ROWS, LANES), lambda j, te, na, nv, ix: (blk, 0),
                         pipeline_mode=pl.Buffered(1)),
            pl.BlockSpec((MOE_TM, 1), lambda j, te, na, nv, ix: (j, 0)),
            pl.BlockSpec((1, D, EXPERT_HIDDEN), lambda j, te, na, nv, ix: (te[j], 0, 0)),
            pl.BlockSpec((1, D, EXPERT_HIDDEN), lambda j, te, na, nv, ix: (te[j], 0, 0)),
            pl.BlockSpec((1, EXPERT_HIDDEN, D), lambda j, te, na, nv, ix: (te[j], 0, 0)),
        ],
        out_specs=pl.BlockSpec((tp * TOK_ROWS, LANES), lambda j, te, na, nv, ix: (0, 0),
                               pipeline_mode=pl.Buffered(1)),
        scratch_shapes=[pltpu.VMEM((MOE_TM * TOK_ROWS, LANES), F32),
                        pltpu.VMEM((MOE_TM * TOK_ROWS, LANES), F32)],
    )
    return pl.pallas_call(
        _moe_kernel,
        grid_spec=grid_spec,
        out_shape=jax.ShapeDtypeStruct((tp * TOK_ROWS, LANES), F32),
        compiler_params=pltpu.CompilerParams(vmem_limit_bytes=VMEM_LIMIT),
        name="moe",
    )(tile_e, n_active.reshape(1), n_valid, idx, h_tt, gate, w1, w3, w2)


def _moe_all(h_tt, route, w1, w3, w2):
    out_c = _moe(h_tt, 0, T_CTX, route[:T_CTX], w1, w3, w2)
    out_l = _moe(h_tt, T_CTX // T_LAT, T_LAT, route[T_CTX:], w1, w3, w2)
    return jnp.concatenate([out_c, out_l], axis=0)


def _final_kernel(x1_ref, moe_ref, mod_ref, g_ref, o_ref):
    i = pl.program_id(0)
    r = _mod_row(i)
    g2 = mod_ref[pl.ds(r, 1), 5 * D:6 * D]
    x = x1_ref[...] + g2 * _from_token_tiles(moe_ref)
    ms = jnp.mean(x * x, axis=-1, keepdims=True)
    o_ref[...] = x * lax.rsqrt(ms + EPS) * g_ref[...]


def _final(x1, moe_tt, mod_l, g):
    return pl.pallas_call(
        _final_kernel,
        grid=(NT,),
        in_specs=[pl.BlockSpec((TM, D), lambda i: (i, 0)),
                  pl.BlockSpec((TM * TOK_ROWS, LANES), lambda i: (i, 0)),
                  pl.BlockSpec((SUBLANES, 6 * D), lambda i: (0, 0)),
                  pl.BlockSpec((1, D), lambda i: (0, 0))],
        out_specs=pl.BlockSpec((TM, D), lambda i: (i, 0)),
        out_shape=jax.ShapeDtypeStruct((T_ALL, D), F32),
        compiler_params=pltpu.CompilerParams(vmem_limit_bytes=VMEM_LIMIT),
        name="final_norm",
    )(x1, moe_tt, mod_l, g)


def _rope_tables():
    t = jnp.arange(LAT_LEN)
    row = (t // GRID_W).astype(F32)
    col = (t % GRID_W).astype(F32)
    inv = ROPE_BASE ** (-jnp.arange(0, ROPE_AXIS_DIM, 2, dtype=F32) / ROPE_AXIS_DIM)

    def table(p):
        ang = p[:, None] * inv[None, :]
        ang = jnp.concatenate([ang, ang], axis=-1)
        return jnp.cos(ang), jnp.sin(ang)

    (cr, sr), (cc, sc) = table(row), table(col)
    cos = jnp.concatenate([cr, cc, cr, cc], axis=-1)
    sin = jnp.concatenate([sr, sc, sr, sc], axis=-1)
    return cos, sin


def _router_weights(w_grp, w_exp):
    we = jnp.transpose(w_exp, (1, 0, 2)).reshape(D, N_GROUPS * EPG)
    pad = jnp.zeros((D, LANES - N_GROUPS - N_GROUPS * EPG), F32)
    return jnp.concatenate([w_grp, we, pad], axis=1)


def kernel(x_prompt, x_sample, cache_k, cache_v, c, c_ctx, w_ada, b_ada, norm_mix_g, norm_ffn_g, w_in_ab, pool_w, pool_scale, attn_sink, w_out_ab, conv_w1, conv_b1, conv_dw, conv_dw_b, conv_ln_g, conv_ln_b, conv_w2, conv_b2, router_grp, router_exp, moe_w1, moe_w3, moe_w2, final_g):
    x = jnp.concatenate([x_prompt.reshape(T_CTX, D), x_sample.reshape(T_LAT, D)], axis=0)
    cond8 = jnp.concatenate([c_ctx[None, :], c, jnp.zeros((SUBLANES - 1 - N_LAT_SEQ, D), F32)], axis=0)
    mod = _modulation(cond8, w_ada, b_ada)
    cos_t, sin_t = _rope_tables()
    row = lambda v: v.reshape(1, -1)

    a, q, k, v = _in_even(x, mod[0], row(norm_mix_g[0]), w_in_ab[0], cos_t, sin_t)
    sink = row(attn_sink[0])
    yb_c = _attn_ctx(sink, q, k, v)
    past = cache_k.shape[2]
    yb_l = _attn_lat(sink, q, k, v, cache_k[:, 0].reshape(N_LAT_SEQ, past, KV_DIM),
                     cache_v[:, 0].reshape(N_LAT_SEQ, past, KV_DIM))
    yb = jnp.concatenate([yb_c, yb_l], axis=0)
    x1, h2, route = _mid_even(a, yb, x, mod[0], pool_w[0], row(pool_scale[0]), w_out_ab[0],
                              row(norm_ffn_g[0]), _router_weights(router_grp[0], router_exp[0]))
    moe0 = _moe_all(h2, route, moe_w1[0], moe_w3[0], moe_w2[0])

    x2, u = _in_odd(x1, moe0, mod[0], mod[1], row(norm_mix_g[1]), conv_w1[0], row(conv_b1[0]))
    x3, h2b, route_b = _mid_odd(u, x2, mod[1], conv_dw[0], row(conv_dw_b[0]), row(conv_ln_g[0]),
                                row(conv_ln_b[0]), conv_w2[0], row(conv_b2[0]), row(norm_ffn_g[1]),
                                _router_weights(router_grp[1], router_exp[1]))
    moe1 = _moe_all(h2b, route_b, moe_w1[1], moe_w3[1], moe_w2[1])
    y = _final(x3, moe1, mod[1], row(final_g))

    y_prompt = y[:T_CTX].reshape(N_CTX_SEQ, CTX_LEN, D)
    y_sample = y[T_CTX:].reshape(N_LAT_SEQ, LAT_LEN, D)
    state_k = k[:T_CTX].reshape(N_CTX_SEQ, 1, CTX_LEN, N_KV_HEADS, HEAD_DIM)
    state_v = v[:T_CTX].reshape(N_CTX_SEQ, 1, CTX_LEN, N_KV_HEADS, HEAD_DIM)
    return (y_prompt, y_sample, state_k, state_v)
```

```python
import functools

import jax
import jax.numpy as jnp
from jax import lax
from jax.experimental import pallas as pl
from jax.experimental.pallas import tpu as pltpu

F32 = jnp.float32
BF16 = jnp.bfloat16
I32 = jnp.int32

D = 1024
N_CTX_SEQ = 16
CTX_LEN = 256
N_LAT_SEQ = 2
LAT_LEN = 1024
T_CTX = N_CTX_SEQ * CTX_LEN
T_LAT = N_LAT_SEQ * LAT_LEN
T_ALL = T_CTX + T_LAT
TM = 256
NT = T_ALL // TM
NT_CTX = T_CTX // TM
LAT_TILES_PER_SEQ = LAT_LEN // TM
GRID_W = 64
DEPTH = 2

POOL_WINDOWS = (2, 4, 8, 16)
POOL_GROUP_DIM = 128
POOL_DIM = 512
HEAD_DIM = 64
N_Q_HEADS = 8
N_KV_HEADS = 2
Q_PER_KV = 4
Q_DIM = 512
KV_DIM = 128
IN_AB = 1280
ATTN_WINDOW = 128
ATTN_BLOCK = 128
ATTN_SCALE = HEAD_DIM ** -0.5
ROPE_BASE = 10000.0
ROPE_AXIS_DIM = 32
CONV_WIDTH = 31
CONV_HALO = 16
POOL_HALO = 8
N_GROUPS = 4
EPG = 4
N_EXPERTS = 16
EXPERT_HIDDEN = 512
EPS = 1e-6
NEG_BIG = -1e30

SUBLANES = 8
LANES = 128
TOK_ROWS = D // LANES

MOE_TM = 256
MOE_TILES_CTX = 2 * T_CTX // MOE_TM + N_EXPERTS
MOE_TILES_LAT = 2 * T_LAT // MOE_TM + N_EXPERTS
MOVE_BATCH = 8
VMEM_LIMIT = 56 * 1024 * 1024


def _silu(x):
    return x * (1.0 / (1.0 + jnp.exp(-x)))


def _mod_row(i):
    return jnp.where(i < NT_CTX, 0, 1 + (i - NT_CTX) // LAT_TILES_PER_SEQ)


def _seq_info(i):
    is_ctx = i < NT_CTX
    k = (i - NT_CTX) % LAT_TILES_PER_SEQ
    off = jnp.where(is_ctx, 0, k * TM)
    n = jnp.where(is_ctx, CTX_LEN, LAT_LEN)
    first = jnp.logical_or(is_ctx, k == 0)
    last = jnp.logical_or(is_ctx, k == LAT_TILES_PER_SEQ - 1)
    return off, n, first, last


def _rms_mod(x, g, scale, shift):
    ms = jnp.mean(x * x, axis=-1, keepdims=True)
    return (x * lax.rsqrt(ms + EPS) * g) * (1.0 + scale) + shift


def _dot(a, b):
    return jnp.dot(a.astype(BF16), b.astype(BF16), preferred_element_type=F32)


def _dot_nt(a, b):
    return lax.dot_general(a.astype(BF16), b.astype(BF16), (((1,), (1,)), ((), ())),
                           preferred_element_type=F32)


def _from_token_tiles(ref):
    return jnp.concatenate([ref[pl.ds(s, TM, stride=TOK_ROWS), :] for s in range(TOK_ROWS)], axis=1)


def _tile_spec(width):
    return pl.BlockSpec((TM, width), lambda i: (i, 0))


def _full_spec(shape):
    return pl.BlockSpec(shape, lambda i: (0,) * len(shape))


def _layer_spec(shape, l):
    return pl.BlockSpec((None,) + tuple(shape), lambda i: (l,) + (0,) * len(shape))


_CTX_TILE = pl.BlockSpec((TM, D), lambda i: (jnp.minimum(i, NT_CTX - 1), 0))
_LAT_TILE = pl.BlockSpec((TM, D), lambda i: (jnp.maximum(i - NT_CTX, 0), 0))
_MOD_SPEC = pl.BlockSpec((DEPTH, SUBLANES, 6 * D), lambda i: (0, 0, 0))


def _params():
    return pltpu.CompilerParams(vmem_limit_bytes=VMEM_LIMIT)


MOD_TN = 1536


def _mod_kernel(cond_ref, w_ref, b_ref, o_ref):
    s = _silu(cond_ref[...])
    o_ref[0] = _dot(s, w_ref[0]) + b_ref[0]


def _modulation(cond8, w_ada, b_ada):
    return pl.pallas_call(
        _mod_kernel,
        grid=(DEPTH, 6 * D // MOD_TN),
        in_specs=[
            pl.BlockSpec((SUBLANES, D), lambda l, n: (0, 0)),
            pl.BlockSpec((1, D, MOD_TN), lambda l, n: (l, 0, n)),
            pl.BlockSpec((1, 1, MOD_TN), lambda l, n: (l, 0, n)),
        ],
        out_specs=pl.BlockSpec((1, SUBLANES, MOD_TN), lambda l, n: (l, 0, n)),
        out_shape=jax.ShapeDtypeStruct((DEPTH, SUBLANES, 6 * D), F32),
        compiler_params=_params(),
        name="modulation",
    )(cond8, w_ada, b_ada.reshape(DEPTH, 1, 6 * D))


def _rope_chunk(xc, cos, sin):
    lane = lax.broadcasted_iota(I32, xc.shape, 1)
    first = (lane % ROPE_AXIS_DIM) < (ROPE_AXIS_DIM // 2)
    rot = jnp.where(first, -pltpu.roll(xc, LANES - ROPE_AXIS_DIM // 2, 1), pltpu.roll(xc, ROPE_AXIS_DIM // 2, 1))
    return xc * cos + rot * sin


def _in_even_kernel(l, xc_ref, xl_ref, mod_ref, g_ref, w_ref, cos_ref, sin_ref, a_ref, q_ref, k_ref, v_ref):
    i = pl.program_id(0)
    r = _mod_row(i)
    x = jnp.where(i < NT_CTX, xc_ref[...], xl_ref[...])
    shift = mod_ref[l, pl.ds(r, 1), 0:D]
    scale = mod_ref[l, pl.ds(r, 1), D:2 * D]
    h = _rms_mod(x, g_ref[l:l + 1, :], scale, shift)
    proj = _dot(h, w_ref[...])
    a_ref[...] = proj[:, :POOL_DIM]
    v_ref[...] = proj[:, POOL_DIM + Q_DIM + KV_DIM:]

    @pl.when(i < NT_CTX)
    def _():
        q_ref[...] = proj[:, POOL_DIM:POOL_DIM + Q_DIM]
        k_ref[...] = proj[:, POOL_DIM + Q_DIM:POOL_DIM + Q_DIM + KV_DIM]

    @pl.when(i >= NT_CTX)
    def _():
        off, _, _, _ = _seq_info(i)
        off = pl.multiple_of(off, TM)
        cos = cos_ref[pl.ds(off, TM), :]
        sin = sin_ref[pl.ds(off, TM), :]
        for c in range(Q_DIM // LANES):
            lo = POOL_DIM + c * LANES
            q_ref[:, c * LANES:(c + 1) * LANES] = _rope_chunk(proj[:, lo:lo + LANES], cos, sin)
        lo = POOL_DIM + Q_DIM
        k_ref[...] = _rope_chunk(proj[:, lo:lo + LANES], cos, sin)


def _in_even(l, e, xc, xl, mod, g, w_in, cos_t, sin_t):
    return pl.pallas_call(
        functools.partial(_in_even_kernel, l),
        grid=(NT,),
        in_specs=[_CTX_TILE, _LAT_TILE, _MOD_SPEC, _full_spec((DEPTH, D)), _layer_spec((D, IN_AB), e),
                  _full_spec((LAT_LEN, LANES)), _full_spec((LAT_LEN, LANES))],
        out_specs=[_tile_spec(POOL_DIM), _tile_spec(Q_DIM), _tile_spec(KV_DIM), _tile_spec(KV_DIM)],
        out_shape=[jax.ShapeDtypeStruct((T_ALL, POOL_DIM), F32), jax.ShapeDtypeStruct((T_ALL, Q_DIM), F32),
                   jax.ShapeDtypeStruct((T_ALL, KV_DIM), F32), jax.ShapeDtypeStruct((T_ALL, KV_DIM), F32)],
        compiler_params=_params(),
        name="in_even",
    )(xc, xl, mod, g, w_in, cos_t, sin_t)


def _attn_ctx_kernel(e, sink_ref, q_ref, k_ref, v_ref, o_ref):
    q = q_ref[...]
    k = k_ref[...]
    v = v_ref[...]
    for h in range(N_KV_HEADS):
        kh = k[:, h * HEAD_DIM:(h + 1) * HEAD_DIM]
        vh = v[:, h * HEAD_DIM:(h + 1) * HEAD_DIM]
        for g in range(Q_PER_KV):
            j = h * Q_PER_KV + g
            s = _dot_nt(q[:, j * HEAD_DIM:(j + 1) * HEAD_DIM], kh) * ATTN_SCALE
            sk = sink_ref[e, j]
            m = jnp.maximum(jnp.max(s, axis=-1, keepdims=True), sk)
            p = jnp.exp(s - m)
            denom = jnp.sum(p, axis=-1, keepdims=True) + jnp.exp(sk - m)
            o_ref[:, j * HEAD_DIM:(j + 1) * HEAD_DIM] = _dot(p, vh) / denom


def _attn_ctx(e, sink, q, k, v):
    tile = lambda w: pl.BlockSpec((TM, w), lambda b: (b, 0))
    return pl.pallas_call(
        functools.partial(_attn_ctx_kernel, e),
        grid=(N_CTX_SEQ,),
        in_specs=[pl.BlockSpec(memory_space=pltpu.SMEM), tile(Q_DIM), tile(KV_DIM), tile(KV_DIM)],
        out_specs=tile(Q_DIM),
        out_shape=jax.ShapeDtypeStruct((T_ALL, Q_DIM), F32),
        compiler_params=_params(),
        name="attn_ctx",
    )(sink, q, k, v)


LAT_BLOCKS = LAT_LEN // ATTN_BLOCK
BAND = 3 * ATTN_BLOCK


def _attn_lat_kernel(e, sink_ref, q_ref, k_ref, v_ref, ck_ref, cv_ref, yb_in_ref, o_ref):
    del yb_in_ref
    n = pl.program_id(1)
    start = jnp.clip((n - 1) * ATTN_BLOCK, 0, LAT_LEN - BAND)
    start = pl.multiple_of(start, ATTN_BLOCK)
    q = q_ref[...]
    kw = k_ref[pl.ds(start, BAND), :]
    vw = v_ref[pl.ds(start, BAND), :]
    ck = ck_ref[0]
    cv = cv_ref[0]
    qpos = n * ATTN_BLOCK + lax.broadcasted_iota(I32, (ATTN_BLOCK, BAND), 0)
    kpos = start + lax.broadcasted_iota(I32, (ATTN_BLOCK, BAND), 1)
    valid = jnp.abs(qpos - kpos) <= ATTN_WINDOW
    for h in range(N_KV_HEADS):
        hs = slice(h * HEAD_DIM, (h + 1) * HEAD_DIM)
        for g in range(Q_PER_KV):
            j = h * Q_PER_KV + g
            qj = q[:, j * HEAD_DIM:(j + 1) * HEAD_DIM]
            s_loc = jnp.where(valid, _dot_nt(qj, kw[:, hs]) * ATTN_SCALE, NEG_BIG)
            s_ctx = _dot_nt(qj, ck[:, hs]) * ATTN_SCALE
            sk = sink_ref[e, j]
            m = jnp.maximum(jnp.maximum(jnp.max(s_loc, axis=-1, keepdims=True),
                                        jnp.max(s_ctx, axis=-1, keepdims=True)), sk)
            p_loc = jnp.exp(s_loc - m)
            p_ctx = jnp.exp(s_ctx - m)
            denom = (jnp.sum(p_loc, axis=-1, keepdims=True) + jnp.sum(p_ctx, axis=-1, keepdims=True)
                     + jnp.exp(sk - m))
            o = _dot(p_loc, vw[:, hs]) + _dot(p_ctx, cv[:, hs])
            o_ref[:, j * HEAD_DIM:(j + 1) * HEAD_DIM] = o / denom


def _attn_lat(e, sink, q, k, v, ck, cv, yb):
    past = ck.shape[1]
    q_blk0 = T_CTX // ATTN_BLOCK
    kv_blk0 = T_CTX // LAT_LEN
    q_spec = pl.BlockSpec((ATTN_BLOCK, Q_DIM), lambda b, n: (q_blk0 + b * LAT_BLOCKS + n, 0))
    return pl.pallas_call(
        functools.partial(_attn_lat_kernel, e),
        grid=(N_LAT_SEQ, LAT_BLOCKS),
        in_specs=[pl.BlockSpec(memory_space=pltpu.SMEM),
                  q_spec,
                  pl.BlockSpec((LAT_LEN, KV_DIM), lambda b, n: (kv_blk0 + b, 0)),
                  pl.BlockSpec((LAT_LEN, KV_DIM), lambda b, n: (kv_blk0 + b, 0)),
                  pl.BlockSpec((1, past, KV_DIM), lambda b, n: (b, 0, 0)),
                  pl.BlockSpec((1, past, KV_DIM), lambda b, n: (b, 0, 0)),
                  pl.BlockSpec(memory_space=pl.ANY)],
        out_specs=q_spec,
        out_shape=jax.ShapeDtypeStruct((T_ALL, Q_DIM), F32),
        input_output_aliases={6: 0},
        compiler_params=_params(),
        name="attn_lat",
    )(sink, q, k, v, ck, cv, yb)


def _router(h2, wr):
    hi = h2.astype(BF16)
    lo = (h2 - hi.astype(F32)).astype(BF16)
    whi = wr.astype(BF16)
    wlo = (wr - whi.astype(F32)).astype(BF16)
    dot = functools.partial(jnp.dot, preferred_element_type=F32)
    logits = dot(hi, whi) + (dot(lo, whi) + dot(hi, wlo))
    lane = lax.broadcasted_iota(I32, logits.shape, 1).astype(F32)
    neg = jnp.float32(-jnp.inf)
    big = jnp.float32(1e9)
    is_grp = lane < N_GROUPS
    gl = jnp.where(is_grp, logits, neg)
    gmax = jnp.max(gl, axis=-1, keepdims=True)
    gsum = jnp.sum(jnp.where(is_grp, jnp.exp(logits - gmax), 0.0), axis=-1, keepdims=True)
    g_w = 1.0 / gsum
    g_idx = jnp.min(jnp.where(gl == gmax, lane, big), axis=-1, keepdims=True)
    base = N_GROUPS + EPG * g_idx
    in_grp = jnp.logical_and(lane >= base, lane < base + EPG)
    el = jnp.where(in_grp, logits, neg)
    t1 = jnp.max(el, axis=-1, keepdims=True)
    i1 = jnp.min(jnp.where(el == t1, lane, big), axis=-1, keepdims=True)
    el2 = jnp.where(lane == i1, neg, el)
    t2 = jnp.max(el2, axis=-1, keepdims=True)
    i2 = jnp.min(jnp.where(el2 == t2, lane, big), axis=-1, keepdims=True)
    d = jnp.exp(t2 - t1)
    w1 = g_w / (1.0 + d)
    w2 = g_w * d / (1.0 + d)
    return i1 - N_GROUPS, i2 - N_GROUPS, w1, w2


ROW_E1, ROW_E2, ROW_R1, ROW_R2, ROW_W1, ROW_W2 = range(6)


def _mixer_tail(l, x, y, mod_ref, r, gffn_ref, wr_ref, x1_ref, h2_ref, ri_ref, rf_ref, cnt_ref, run_ref):
    i = pl.program_id(0)
    g1 = mod_ref[l, pl.ds(r, 1), 2 * D:3 * D]
    shift2 = mod_ref[l, pl.ds(r, 1), 3 * D:4 * D]
    scale2 = mod_ref[l, pl.ds(r, 1), 4 * D:5 * D]
    x1 = x + g1 * y
    x1_ref[...] = x1
    h2 = _rms_mod(x1, gffn_ref[l:l + 1, :], scale2, shift2)
    for s in range(TOK_ROWS):
        h2_ref[pl.ds(s, TM, stride=TOK_ROWS), :] = h2[:, s * LANES:(s + 1) * LANES]
    e1, e2, w1, w2 = _router(h2, wr_ref[...])

    @pl.when(jnp.logical_or(i == 0, i == NT_CTX))
    def _():
        run_ref[...] = jnp.zeros(run_ref.shape, F32)

    @pl.when(i == 0)
    def _():
        cnt_ref[...] = jnp.zeros(cnt_ref.shape, I32)

    lane = lax.broadcasted_iota(I32, (TM, LANES), 1).astype(F32)
    member = jnp.where(jnp.logical_or(lane == e1, lane == e2), 1.0, 0.0)
    tri = jnp.where(lax.broadcasted_iota(I32, (TM, TM), 0) >= lax.broadcasted_iota(I32, (TM, TM), 1), 1.0, 0.0)
    csum = jnp.dot(tri.astype(BF16), member.astype(BF16), preferred_element_type=F32)
    run = run_ref[0:1, :]
    before = csum - member + run
    r1 = jnp.sum(jnp.where(lane == e1, before, 0.0), axis=-1, keepdims=True)
    r2 = jnp.sum(jnp.where(lane == e2, before, 0.0), axis=-1, keepdims=True)
    run_new = run + csum[TM - 1:TM, :]
    run_ref[0:1, :] = run_new

    @pl.when(i == NT_CTX - 1)
    def _():
        cnt_ref[0:1, :] = run_new.astype(I32)

    @pl.when(i == NT - 1)
    def _():
        cnt_ref[1:2, :] = run_new.astype(I32)

    rec = jnp.zeros((TM, LANES), F32)
    for row, val in ((ROW_E1, e1), (ROW_E2, e2), (ROW_R1, r1), (ROW_R2, r2), (ROW_W1, w1), (ROW_W2, w2)):
        rec = jnp.where(lane == row, val, rec)
    rec_t = rec.T[0:SUBLANES, :]
    ri_ref[...] = rec_t.astype(I32)
    rf_ref[...] = rec_t


_TAIL_OUT_SHAPES = [jax.ShapeDtypeStruct((T_ALL, D), F32),
                    jax.ShapeDtypeStruct((T_ALL * TOK_ROWS, LANES), F32),
                    jax.ShapeDtypeStruct((SUBLANES, T_ALL), I32),
                    jax.ShapeDtypeStruct((SUBLANES, T_ALL), F32),
                    jax.ShapeDtypeStruct((SUBLANES, LANES), I32)]


def _tail_out_specs():
    return [_tile_spec(D),
            pl.BlockSpec((TM * TOK_ROWS, LANES), lambda i: (i, 0)),
            pl.BlockSpec((SUBLANES, TM), lambda i: (0, i)),
            pl.BlockSpec((SUBLANES, TM), lambda i: (0, i)),
            _full_spec((SUBLANES, LANES))]


def _halo_specs(halo, width):
    per = TM // halo
    last = T_ALL // halo - 1
    prev = pl.BlockSpec((halo, width), lambda i: (jnp.maximum(i * per - 1, 0), 0))
    nxt = pl.BlockSpec((halo, width), lambda i: (jnp.minimum((i + 1) * per, last), 0))
    return prev, nxt


def _mid_even_kernel(l, a_ref, ap_ref, an_ref, yb_ref, xc_ref, xl_ref, mod_ref, wp_ref, ps_ref, wo_ref,
                     gffn_ref, wr_ref, x1_ref, h2_ref, ri_ref, rf_ref, cnt_ref, pad_ref, run_ref):
    i = pl.program_id(0)
    r = _mod_row(i)
    off, n, first, last = _seq_info(i)
    a = a_ref[...]
    pad_ref[0:POOL_HALO, :] = jnp.where(first, 0.0, ap_ref[...])
    pad_ref[POOL_HALO:POOL_HALO + TM, :] = a
    pad_ref[POOL_HALO + TM:, :] = jnp.where(last, 0.0, an_ref[...])
    t = off + lax.broadcasted_iota(I32, (TM, 1), 0)
    mixed = []
    for g, w in enumerate(POOL_WINDOWS):
        lo = w // 2
        hi = w - lo - 1
        cols = slice(g * POOL_GROUP_DIM, (g + 1) * POOL_GROUP_DIM)
        total = pad_ref[POOL_HALO - lo:POOL_HALO - lo + TM, cols]
        for j in range(-lo + 1, hi + 1):
            total = total + pad_ref[POOL_HALO + j:POOL_HALO + j + TM, cols]
        count = (jnp.minimum(t + hi, n - 1) - jnp.maximum(t - lo, 0) + 1).astype(F32)
        pooled = total / count - a[:, cols]
        mixed.append(_dot(pooled, wp_ref[g]))
    ya = jnp.concatenate(mixed, axis=1) * ps_ref[...]
    y = _dot(ya, wo_ref[0:POOL_DIM, :]) + _dot(yb_ref[...], wo_ref[POOL_DIM:, :])
    x = jnp.where(i < NT_CTX, xc_ref[...], xl_ref[...])
    _mixer_tail(l, x, y, mod_ref, r, gffn_ref, wr_ref, x1_ref, h2_ref, ri_ref, rf_ref, cnt_ref, run_ref)


def _mid_even(l, e, a, yb, xc, xl, mod, w_pool, pool_scale, w_out, gffn, wr):
    prev, nxt = _halo_specs(POOL_HALO, POOL_DIM)
    return pl.pallas_call(
        functools.partial(_mid_even_kernel, l),
        grid=(NT,),
        in_specs=[_tile_spec(POOL_DIM), prev, nxt, _tile_spec(Q_DIM), _CTX_TILE, _LAT_TILE, _MOD_SPEC,
                  _layer_spec((len(POOL_WINDOWS), POOL_GROUP_DIM, POOL_GROUP_DIM), e),
                  _layer_spec((1, POOL_DIM), e), _layer_spec((D, D), e), _full_spec((DEPTH, D)),
                  _layer_spec((D, LANES), l)],
        out_specs=_tail_out_specs(),
        out_shape=_TAIL_OUT_SHAPES,
        scratch_shapes=[pltpu.VMEM((TM + 2 * POOL_HALO, POOL_DIM), F32), pltpu.VMEM((SUBLANES, LANES), F32)],
        compiler_params=_params(),
        name="mid_even",
    )(a, a, a, yb, xc, xl, mod, w_pool, pool_scale, w_out, gffn, wr)


def _in_odd_kernel(l, x1_ref, moe_ref, mod_ref, g_ref, w1_ref, b1_ref, x2_ref, u_ref):
    i = pl.program_id(0)
    r = _mod_row(i)
    g2 = mod_ref[l - 1, pl.ds(r, 1), 5 * D:6 * D]
    x2 = x1_ref[...] + g2 * _from_token_tiles(moe_ref)
    x2_ref[...] = x2
    shift = mod_ref[l, pl.ds(r, 1), 0:D]
    scale = mod_ref[l, pl.ds(r, 1), D:2 * D]
    h = _rms_mod(x2, g_ref[l:l + 1, :], scale, shift)
    u = _dot(h, w1_ref[...]) + b1_ref[...]
    u_ref[...] = u[:, :D] * (1.0 / (1.0 + jnp.exp(-u[:, D:])))


def _in_odd(l, o, x1, moe_tt, mod, g, w1, b1):
    return pl.pallas_call(
        functools.partial(_in_odd_kernel, l),
        grid=(NT,),
        in_specs=[_tile_spec(D), pl.BlockSpec((TM * TOK_ROWS, LANES), lambda i: (i, 0)), _MOD_SPEC,
                  _full_spec((DEPTH, D)), _layer_spec((D, 2 * D), o), _layer_spec((1, 2 * D), o)],
        out_specs=[_tile_spec(D), _tile_spec(D)],
        out_shape=[jax.ShapeDtypeStruct((T_ALL, D), F32), jax.ShapeDtypeStruct((T_ALL, D), F32)],
        compiler_params=_params(),
        name="in_odd",
    )(x1, moe_tt, mod, g, w1, b1)


CONV_PAD_ROWS = TM + 2 * CONV_HALO
CONV_SHIFT_ROWS = CONV_PAD_ROWS - SUBLANES


def _mid_odd_kernel(l, u_ref, up_ref, un_ref, x_ref, mod_ref, dw_ref, dwb_ref, lng_ref, lnb_ref, w2_ref, b2_ref,
                    gffn_ref, wr_ref, x1_ref, h2_ref, ri_ref, rf_ref, cnt_ref, pad_ref, shift_ref, run_ref):
    i = pl.program_id(0)
    r = _mod_row(i)
    _, _, first, last = _seq_info(i)
    pad_ref[0:CONV_HALO, :] = jnp.where(first, 0.0, up_ref[...])
    pad_ref[CONV_HALO:CONV_HALO + TM, :] = u_ref[...]
    pad_ref[CONV_HALO + TM:, :] = jnp.where(last, 0.0, un_ref[...])
    for b in range(1, SUBLANES):
        shift_ref[b - 1] = pad_ref[b:b + CONV_SHIFT_ROWS, :]
    half = CONV_WIDTH // 2
    acc = jnp.zeros((TM, D), F32) + dwb_ref[...]
    for j in range(CONV_WIDTH):
        s = CONV_HALO - half + j
        a8, b = (s // SUBLANES) * SUBLANES, s % SUBLANES
        win = pad_ref[a8:a8 + TM, :] if b == 0 else shift_ref[b - 1, a8:a8 + TM, :]
        acc = acc + win * dw_ref[j:j + 1, :]
    mu = jnp.mean(acc, axis=-1, keepdims=True)
    cen = acc - mu
    var = jnp.mean(cen * cen, axis=-1, keepdims=True)
    v = _silu(cen * lax.rsqrt(var + EPS) * lng_ref[...] + lnb_ref[...])
    y = _dot(v, w2_ref[...]) + b2_ref[...]
    _mixer_tail(l, x_ref[...], y, mod_ref, r, gffn_ref, wr_ref, x1_ref, h2_ref, ri_ref, rf_ref, cnt_ref, run_ref)


def _mid_odd(l, o, u, x, mod, dw, dwb, lng, lnb, w2, b2, gffn, wr):
    prev, nxt = _halo_specs(CONV_HALO, D)
    return pl.pallas_call(
        functools.partial(_mid_odd_kernel, l),
        grid=(NT,),
        in_specs=[_tile_spec(D), prev, nxt, _tile_spec(D), _MOD_SPEC, _layer_spec((CONV_WIDTH, D), o),
                  _layer_spec((1, D), o), _layer_spec((1, D), o), _layer_spec((1, D), o),
                  _layer_spec((D, D), o), _layer_spec((1, D), o), _full_spec((DEPTH, D)),
                  _layer_spec((D, LANES), l)],
        out_specs=_tail_out_specs(),
        out_shape=_TAIL_OUT_SHAPES,
        scratch_shapes=[pltpu.VMEM((CONV_PAD_ROWS, D), F32),
                        pltpu.VMEM((SUBLANES - 1, CONV_SHIFT_ROWS, D), F32),
                        pltpu.VMEM((SUBLANES, LANES), F32)],
        compiler_params=_params(),
        name="mid_odd",
    )(u, u, u, x, mod, dw, dwb, lng, lnb, w2, b2, gffn, wr)


def _route_finish_kernel(ri_ref, rf_ref, cnt_ref, te_c, nv_c, idx_c, gate_c, te_l, nv_l, idx_l, gate_l, nact_ref,
                         starts_ref):
    pops = ((0, T_CTX, te_c, nv_c, idx_c, gate_c), (T_CTX, T_LAT, te_l, nv_l, idx_l, gate_l))
    for p, (t0, tp, te_ref, nv_ref, idx_ref, gate_ref) in enumerate(pops):
        n_tiles = te_ref.shape[0]

        def fill(s, c, idx_ref=idx_ref, gate_ref=gate_ref):
            idx_ref[s] = 0
            gate_ref[s] = 0.0
            return c

        run = jnp.int32(0)
        ends = []
        for e in range(N_EXPERTS):
            cnt = cnt_ref[p, e]
            starts_ref[p * N_EXPERTS + e] = run
            padded = ((cnt + MOE_TM - 1) // MOE_TM) * MOE_TM
            lax.fori_loop(run + cnt, run + padded, fill, 0)
            run = run + padded
            ends.append(run)
        lax.fori_loop(run, n_tiles * MOE_TM, fill, 0)
        nact = run // MOE_TM
        nact_ref[p] = nact

        def expert_of(start, ends=ends):
            e = jnp.int32(0)
            for k in range(N_EXPERTS):
                e = e + jnp.where(start >= ends[k], 1, 0)
            return e

        last_e = expert_of((nact - 1) * MOE_TM)

        def tile_body(j, c, te_ref=te_ref, nv_ref=nv_ref, p=p, expert_of=expert_of, last_e=last_e):
            start = j * MOE_TM
            e = jnp.minimum(expert_of(start), last_e)
            te_ref[j] = e
            nv_ref[j] = jnp.clip(starts_ref[p * N_EXPERTS + e] + cnt_ref[p, e] - start, 0, MOE_TM)
            return c

        lax.fori_loop(0, n_tiles, tile_body, 0)

        def tok_body(t, c, idx_ref=idx_ref, gate_ref=gate_ref, p=p, t0=t0):
            g = t0 + t
            p1 = starts_ref[p * N_EXPERTS + ri_ref[ROW_E1, g]] + ri_ref[ROW_R1, g]
            p2 = starts_ref[p * N_EXPERTS + ri_ref[ROW_E2, g]] + ri_ref[ROW_R2, g]
            idx_ref[p1] = t
            idx_ref[p2] = t
            gate_ref[p1] = rf_ref[ROW_W1, g]
            gate_ref[p2] = rf_ref[ROW_W2, g]
            return c

        lax.fori_loop(0, tp, tok_body, 0, unroll=4)


def _route_finish(ri, rf, cnt):
    smem = pl.BlockSpec(memory_space=pltpu.SMEM)
    rows_c, rows_l = MOE_TILES_CTX * MOE_TM, MOE_TILES_LAT * MOE_TM
    shapes = [((MOE_TILES_CTX,), I32), ((MOE_TILES_CTX,), I32), ((rows_c,), I32), ((rows_c,), F32),
              ((MOE_TILES_LAT,), I32), ((MOE_TILES_LAT,), I32), ((rows_l,), I32), ((rows_l,), F32),
              ((2,), I32)]
    return pl.pallas_call(
        _route_finish_kernel,
        in_specs=[smem, smem, smem],
        out_specs=[smem] * len(shapes),
        out_shape=[jax.ShapeDtypeStruct(s, d) for s, d in shapes],
        scratch_shapes=[pltpu.SMEM((2 * N_EXPERTS,), I32)],
        name="route_finish",
    )(ri, rf, cnt)


def _moe_kernel(pop, te_ref, nact_ref, nval_ref, idx_ref, gate_ref, h_ref, w1_ref, w3_ref, w2_ref, *rest):
    o_ref, xs_ref, ys_ref = rest[-3:]
    del te_ref
    j = pl.program_id(0)

    @pl.when(j == 0)
    def _():
        o_ref[...] = jnp.zeros(o_ref.shape, F32)
        xs_ref[...] = jnp.zeros(xs_ref.shape, F32)

    @pl.when(j < nact_ref[pop])
    def _():
        base = j * MOE_TM
        nval = nval_ref[j]
        nfull = nval // MOVE_BATCH

        def rows(r):
            return pl.ds(pl.multiple_of(r * TOK_ROWS, TOK_ROWS), TOK_ROWS)

        def gather_batch(b, c):
            r0 = b * MOVE_BATCH
            for k in range(MOVE_BATCH):
                xs_ref[rows(r0 + k), :] = h_ref[rows(idx_ref[base + r0 + k]), :]
            return c

        def gather_one(r, c):
            xs_ref[rows(r), :] = h_ref[rows(idx_ref[base + r]), :]
            return c

        lax.fori_loop(0, nfull, gather_batch, 0)
        lax.fori_loop(nfull * MOVE_BATCH, nval, gather_one, 0)

        x = jnp.concatenate([xs_ref[pl.ds(s, MOE_TM, stride=TOK_ROWS), :] for s in range(TOK_ROWS)], axis=1)
        x = x.astype(BF16)
        hg = jnp.dot(x, w1_ref[...].astype(BF16), preferred_element_type=F32)
        hu = jnp.dot(x, w3_ref[...].astype(BF16), preferred_element_type=F32)
        y = _dot(_silu(hg) * hu, w2_ref[...])
        for s in range(TOK_ROWS):
            ys_ref[pl.ds(s, MOE_TM, stride=TOK_ROWS), :] = y[:, s * LANES:(s + 1) * LANES]

        def scatter_batch(b, c):
            r0 = b * MOVE_BATCH
            toks = [idx_ref[base + r0 + k] for k in range(MOVE_BATCH)]
            vals = [o_ref[rows(toks[k]), :] + gate_ref[base + r0 + k] * ys_ref[rows(r0 + k), :]
                    for k in range(MOVE_BATCH)]
            for k in range(MOVE_BATCH):
                o_ref[rows(toks[k]), :] = vals[k]
            return c

        def scatter_one(r, c):
            t = idx_ref[base + r]
            o_ref[rows(t), :] = o_ref[rows(t), :] + gate_ref[base + r] * ys_ref[rows(r), :]
            return c

        lax.fori_loop(0, nfull, scatter_batch, 0)
        lax.fori_loop(nfull * MOVE_BATCH, nval, scatter_one, 0)


def _moe(l, pop, h_tt, prev, te, nact, nval, idx, gate, w1, w3, w2):
    tp = (T_CTX, T_LAT)[pop]
    blk = (0, T_CTX // T_LAT)[pop]
    n_tiles = te.shape[0]
    wmap = lambda j, te, *_: (l, te[j], 0, 0)
    in_specs = [
        pl.BlockSpec((tp * TOK_ROWS, LANES), lambda j, *_: (blk, 0), pipeline_mode=pl.Buffered(1)),
        pl.BlockSpec((None, None, D, EXPERT_HIDDEN), wmap),
        pl.BlockSpec((None, None, D, EXPERT_HIDDEN), wmap),
        pl.BlockSpec((None, None, EXPERT_HIDDEN, D), wmap),
    ]
    args = [te, nact, nval, idx, gate, h_tt, w1, w3, w2]
    aliases = {}
    if prev is not None:
        in_specs.append(pl.BlockSpec(memory_space=pl.ANY))
        aliases = {len(args): 0}
        args.append(prev)
    grid_spec = pltpu.PrefetchScalarGridSpec(
        num_scalar_prefetch=5,
        grid=(n_tiles,),
        in_specs=in_specs,
        out_specs=pl.BlockSpec((tp * TOK_ROWS, LANES), lambda j, *_: (blk, 0), pipeline_mode=pl.Buffered(1)),
        scratch_shapes=[pltpu.VMEM((MOE_TM * TOK_ROWS, LANES), F32),
                        pltpu.VMEM((MOE_TM * TOK_ROWS, LANES), F32)],
    )
    return pl.pallas_call(
        functools.partial(_moe_kernel, pop),
        grid_spec=grid_spec,
        out_shape=jax.ShapeDtypeStruct((T_ALL * TOK_ROWS, LANES), F32),
        input_output_aliases=aliases,
        compiler_params=_params(),
        name="moe",
    )(*args)


def _moe_all(l, h_tt, ri, rf, cnt, w1, w3, w2):
    te_c, nv_c, idx_c, gate_c, te_l, nv_l, idx_l, gate_l, nact = _route_finish(ri, rf, cnt)
    out = _moe(l, 0, h_tt, None, te_c, nact, nv_c, idx_c, gate_c, w1, w3, w2)
    return _moe(l, 1, h_tt, out, te_l, nact, nv_l, idx_l, gate_l, w1, w3, w2)


def _final_kernel(l, x1_ref, moe_ref, mod_ref, g_ref, oc_ref, ol_ref):
    i = pl.program_id(0)
    r = _mod_row(i)
    g2 = mod_ref[l, pl.ds(r, 1), 5 * D:6 * D]
    x = x1_ref[...] + g2 * _from_token_tiles(moe_ref)
    ms = jnp.mean(x * x, axis=-1, keepdims=True)
    y = x * lax.rsqrt(ms + EPS) * g_ref[...]

    @pl.when(i < NT_CTX)
    def _():
        oc_ref[...] = y

    @pl.when(i >= NT_CTX)
    def _():
        ol_ref[...] = y


def _final(l, x1, moe_tt, mod, g):
    return pl.pallas_call(
        functools.partial(_final_kernel, l),
        grid=(NT,),
        in_specs=[_tile_spec(D), pl.BlockSpec((TM * TOK_ROWS, LANES), lambda i: (i, 0)), _MOD_SPEC,
                  _full_spec((1, D))],
        out_specs=[_CTX_TILE, _LAT_TILE],
        out_shape=[jax.ShapeDtypeStruct((T_CTX, D), F32), jax.ShapeDtypeStruct((T_LAT, D), F32)],
        compiler_params=_params(),
        name="final_norm",
    )(x1, moe_tt, mod, g)


def _rope_tables():
    t = jnp.arange(LAT_LEN)
    row = (t // GRID_W).astype(F32)
    col = (t % GRID_W).astype(F32)
    inv = ROPE_BASE ** (-jnp.arange(0, ROPE_AXIS_DIM, 2, dtype=F32) / ROPE_AXIS_DIM)

    def table(p):
        ang = p[:, None] * inv[None, :]
        ang = jnp.concatenate([ang, ang], axis=-1)
        return jnp.cos(ang), jnp.sin(ang)

    (cr, sr), (cc, sc) = table(row), table(col)
    cos = jnp.concatenate([cr, cc, cr, cc], axis=-1)
    sin = jnp.concatenate([sr, sc, sr, sc], axis=-1)
    return cos, sin


def _router_weights(w_grp, w_exp):
    we = jnp.transpose(w_exp, (0, 2, 1, 3)).reshape(DEPTH, D, N_GROUPS * EPG)
    pad = jnp.zeros((DEPTH, D, LANES - N_GROUPS - N_GROUPS * EPG), F32)
    return jnp.concatenate([w_grp, we, pad], axis=-1)


def kernel(x_prompt, x_sample, cache_k, cache_v, c, c_ctx, w_ada, b_ada, norm_mix_g, norm_ffn_g, w_in_ab, pool_w, pool_scale, attn_sink, w_out_ab, conv_w1, conv_b1, conv_dw, conv_dw_b, conv_ln_g, conv_ln_b, conv_w2, conv_b2, router_grp, router_exp, moe_w1, moe_w3, moe_w2, final_g):
    xc = x_prompt.reshape(T_CTX, D)
    xl = x_sample.reshape(T_LAT, D)
    cond8 = jnp.concatenate([c_ctx[None, :], c, jnp.zeros((SUBLANES - 1 - N_LAT_SEQ, D), F32)], axis=0)
    mod = _modulation(cond8, w_ada, b_ada)
    cos_t, sin_t = _rope_tables()
    wr = _router_weights(router_grp, router_exp)
    n_even, n_odd = w_in_ab.shape[0], conv_w1.shape[0]
    past = cache_k.shape[2]

    a, q, k, v = _in_even(0, 0, xc, xl, mod, norm_mix_g, w_in_ab, cos_t, sin_t)
    yb = _attn_ctx(0, attn_sink, q, k, v)
    yb = _attn_lat(0, attn_sink, q, k, v, cache_k[:, 0].reshape(N_LAT_SEQ, past, KV_DIM),
                   cache_v[:, 0].reshape(N_LAT_SEQ, past, KV_DIM), yb)
    x1, h2, ri, rf, cnt = _mid_even(0, 0, a, yb, xc, xl, mod, pool_w, pool_scale.reshape(n_even, 1, POOL_DIM),
                                    w_out_ab, norm_ffn_g, wr)
    moe0 = _moe_all(0, h2, ri, rf, cnt, moe_w1, moe_w3, moe_w2)

    vec = lambda p: p.reshape(n_odd, 1, -1)
    x2, u = _in_odd(1, 0, x1, moe0, mod, norm_mix_g, conv_w1, vec(conv_b1))
    x3, h2b, ri_b, rf_b, cnt_b = _mid_odd(1, 0, u, x2, mod, conv_dw, vec(conv_dw_b), vec(conv_ln_g),
                                          vec(conv_ln_b), conv_w2, vec(conv_b2), norm_ffn_g, wr)
    moe1 = _moe_all(1, h2b, ri_b, rf_b, cnt_b, moe_w1, moe_w3, moe_w2)
    y_c, y_l = _final(1, x3, moe1, mod, final_g.reshape(1, D))

    y_prompt = y_c.reshape(N_CTX_SEQ, CTX_LEN, D)
    y_sample = y_l.reshape(N_LAT_SEQ, LAT_LEN, D)
    state_k = k[:T_CTX].reshape(N_CTX_SEQ, 1, CTX_LEN, N_KV_HEADS, HEAD_DIM)
    state_v = v[:T_CTX].reshape(N_CTX_SEQ, 1, CTX_LEN, N_KV_HEADS, HEAD_DIM)
    return (y_prompt, y_sample, state_k, state_v)
```

```python
import functools

import jax
import jax.numpy as jnp
from jax import lax
from jax.experimental import pallas as pl
from jax.experimental.pallas import tpu as pltpu

F32 = jnp.float32
BF16 = jnp.bfloat16
I32 = jnp.int32

D = 1024
N_CTX_SEQ = 16
CTX_LEN = 256
N_LAT_SEQ = 2
LAT_LEN = 1024
T_CTX = N_CTX_SEQ * CTX_LEN
T_LAT = N_LAT_SEQ * LAT_LEN
T_ALL = T_CTX + T_LAT
TM = 256
NT = T_ALL // TM
NT_CTX = T_CTX // TM
LAT_TILES_PER_SEQ = LAT_LEN // TM
GRID_W = 64
DEPTH = 2

POOL_WINDOWS = (2, 4, 8, 16)
POOL_GROUP_DIM = 128
POOL_DIM = 512
HEAD_DIM = 64
N_Q_HEADS = 8
N_KV_HEADS = 2
Q_PER_KV = 4
Q_DIM = 512
KV_DIM = 128
IN_AB = 1280
ATTN_WINDOW = 128
ATTN_BLOCK = 128
ATTN_SCALE = HEAD_DIM ** -0.5
ROPE_BASE = 10000.0
ROPE_AXIS_DIM = 32
CONV_WIDTH = 31
CONV_HALO = 16
POOL_HALO = 8
N_GROUPS = 4
EPG = 4
N_EXPERTS = 16
EXPERT_HIDDEN = 512
EPS = 1e-6
NEG_BIG = -1e30

SUBLANES = 8
LANES = 128
TOK_ROWS = D // LANES

MOE_TM = 256
MOE_TILES_CTX = 2 * T_CTX // MOE_TM + N_EXPERTS
MOE_TILES_LAT = 2 * T_LAT // MOE_TM + N_EXPERTS
MOVE_BATCH = 8
VMEM_LIMIT = 56 * 1024 * 1024


def _silu(x):
    return x * (1.0 / (1.0 + jnp.exp(-x)))


def _mod_row(i):
    return jnp.where(i < NT_CTX, 0, 1 + (i - NT_CTX) // LAT_TILES_PER_SEQ)


def _seq_info(i):
    is_ctx = i < NT_CTX
    k = (i - NT_CTX) % LAT_TILES_PER_SEQ
    off = jnp.where(is_ctx, 0, k * TM)
    n = jnp.where(is_ctx, CTX_LEN, LAT_LEN)
    first = jnp.logical_or(is_ctx, k == 0)
    last = jnp.logical_or(is_ctx, k == LAT_TILES_PER_SEQ - 1)
    return off, n, first, last


def _rms_mod(x, g, scale, shift):
    ms = jnp.mean(x * x, axis=-1, keepdims=True)
    return (x * lax.rsqrt(ms + EPS) * g) * (1.0 + scale) + shift


def _dot(a, b):
    return jnp.dot(a.astype(BF16), b.astype(BF16), preferred_element_type=F32)


def _dot_nt(a, b):
    return lax.dot_general(a.astype(BF16), b.astype(BF16), (((1,), (1,)), ((), ())),
                           preferred_element_type=F32)


def _from_token_tiles(ref):
    return jnp.concatenate([ref[pl.ds(s, TM, stride=TOK_ROWS), :] for s in range(TOK_ROWS)], axis=1)


def _tile_spec(width):
    return pl.BlockSpec((TM, width), lambda i: (i, 0))


def _full_spec(shape):
    return pl.BlockSpec(shape, lambda i: (0,) * len(shape))


def _layer_spec(shape, l):
    return pl.BlockSpec((None,) + tuple(shape), lambda i: (l,) + (0,) * len(shape))


_CTX_TILE = pl.BlockSpec((TM, D), lambda i: (jnp.minimum(i, NT_CTX - 1), 0))
_LAT_TILE = pl.BlockSpec((TM, D), lambda i: (jnp.maximum(i - NT_CTX, 0), 0))
_MOD_SPEC = pl.BlockSpec((DEPTH, SUBLANES, 6 * D), lambda i: (0, 0, 0))


def _params():
    return pltpu.CompilerParams(vmem_limit_bytes=VMEM_LIMIT)


MOD_TN = 1536


def _mod_kernel(cond_ref, w_ref, b_ref, o_ref):
    s = _silu(cond_ref[...])
    o_ref[0] = _dot(s, w_ref[0]) + b_ref[0]


def _modulation(cond8, w_ada, b_ada):
    return pl.pallas_call(
        _mod_kernel,
        grid=(DEPTH, 6 * D // MOD_TN),
        in_specs=[
            pl.BlockSpec((SUBLANES, D), lambda l, n: (0, 0)),
            pl.BlockSpec((1, D, MOD_TN), lambda l, n: (l, 0, n)),
            pl.BlockSpec((1, 1, MOD_TN), lambda l, n: (l, 0, n)),
        ],
        out_specs=pl.BlockSpec((1, SUBLANES, MOD_TN), lambda l, n: (l, 0, n)),
        out_shape=jax.ShapeDtypeStruct((DEPTH, SUBLANES, 6 * D), F32),
        compiler_params=_params(),
        name="modulation",
    )(cond8, w_ada, b_ada.reshape(DEPTH, 1, 6 * D))


def _rope_chunk(xc, cos, sin):
    lane = lax.broadcasted_iota(I32, xc.shape, 1)
    first = (lane % ROPE_AXIS_DIM) < (ROPE_AXIS_DIM // 2)
    rot = jnp.where(first, -pltpu.roll(xc, LANES - ROPE_AXIS_DIM // 2, 1), pltpu.roll(xc, ROPE_AXIS_DIM // 2, 1))
    return xc * cos + rot * sin


def _in_even_kernel(l, xc_ref, xl_ref, mod_ref, g_ref, w_ref, cos_ref, sin_ref, a_ref, q_ref, k_ref, v_ref):
    i = pl.program_id(0)
    r = _mod_row(i)
    x = jnp.where(i < NT_CTX, xc_ref[...], xl_ref[...])
    shift = mod_ref[l, pl.ds(r, 1), 0:D]
    scale = mod_ref[l, pl.ds(r, 1), D:2 * D]
    h = _rms_mod(x, g_ref[l:l + 1, :], scale, shift)
    proj = _dot(h, w_ref[...])
    a_ref[...] = proj[:, :POOL_DIM]
    v_ref[...] = proj[:, POOL_DIM + Q_DIM + KV_DIM:]

    @pl.when(i < NT_CTX)
    def _():
        q_ref[...] = proj[:, POOL_DIM:POOL_DIM + Q_DIM]
        k_ref[...] = proj[:, POOL_DIM + Q_DIM:POOL_DIM + Q_DIM + KV_DIM]

    @pl.when(i >= NT_CTX)
    def _():
        off, _, _, _ = _seq_info(i)
        off = pl.multiple_of(off, TM)
        cos = cos_ref[pl.ds(off, TM), :]
        sin = sin_ref[pl.ds(off, TM), :]
        for c in range(Q_DIM // LANES):
            lo = POOL_DIM + c * LANES
            q_ref[:, c * LANES:(c + 1) * LANES] = _rope_chunk(proj[:, lo:lo + LANES], cos, sin)
        lo = POOL_DIM + Q_DIM
        k_ref[...] = _rope_chunk(proj[:, lo:lo + LANES], cos, sin)


def _in_even(l, e, xc, xl, mod, g, w_in, cos_t, sin_t):
    return pl.pallas_call(
        functools.partial(_in_even_kernel, l),
        grid=(NT,),
        in_specs=[_CTX_TILE, _LAT_TILE, _MOD_SPEC, _full_spec((DEPTH, D)), _layer_spec((D, IN_AB), e),
                  _full_spec((LAT_LEN, LANES)), _full_spec((LAT_LEN, LANES))],
        out_specs=[_tile_spec(POOL_DIM), _tile_spec(Q_DIM), _tile_spec(KV_DIM), _tile_spec(KV_DIM)],
        out_shape=[jax.ShapeDtypeStruct((T_ALL, POOL_DIM), F32), jax.ShapeDtypeStruct((T_ALL, Q_DIM), F32),
                   jax.ShapeDtypeStruct((T_ALL, KV_DIM), F32), jax.ShapeDtypeStruct((T_ALL, KV_DIM), F32)],
        compiler_params=_params(),
        name="in_even",
    )(xc, xl, mod, g, w_in, cos_t, sin_t)


def _attn_ctx_kernel(e, sink_ref, q_ref, k_ref, v_ref, o_ref):
    q = q_ref[...]
    k = k_ref[...]
    v = v_ref[...]
    for h in range(N_KV_HEADS):
        kh = k[:, h * HEAD_DIM:(h + 1) * HEAD_DIM]
        vh = v[:, h * HEAD_DIM:(h + 1) * HEAD_DIM]
        for g in range(Q_PER_KV):
            j = h * Q_PER_KV + g
            s = _dot_nt(q[:, j * HEAD_DIM:(j + 1) * HEAD_DIM], kh) * ATTN_SCALE
            sk = sink_ref[e, j]
            m = jnp.maximum(jnp.max(s, axis=-1, keepdims=True), sk)
            p = jnp.exp(s - m)
            denom = jnp.sum(p, axis=-1, keepdims=True) + jnp.exp(sk - m)
            o_ref[:, j * HEAD_DIM:(j + 1) * HEAD_DIM] = _dot(p, vh) / denom


def _attn_ctx(e, sink, q, k, v):
    tile = lambda w: pl.BlockSpec((TM, w), lambda b: (b, 0))
    return pl.pallas_call(
        functools.partial(_attn_ctx_kernel, e),
        grid=(N_CTX_SEQ,),
        in_specs=[pl.BlockSpec(memory_space=pltpu.SMEM), tile(Q_DIM), tile(KV_DIM), tile(KV_DIM)],
        out_specs=tile(Q_DIM),
        out_shape=jax.ShapeDtypeStruct((T_ALL, Q_DIM), F32),
        compiler_params=_params(),
        name="attn_ctx",
    )(sink, q, k, v)


LAT_BLOCKS = LAT_LEN // ATTN_BLOCK
BAND = 3 * ATTN_BLOCK


def _attn_lat_kernel(e, sink_ref, q_ref, k_ref, v_ref, ck_ref, cv_ref, yb_in_ref, o_ref):
    del yb_in_ref
    n = pl.program_id(1)
    start = jnp.clip((n - 1) * ATTN_BLOCK, 0, LAT_LEN - BAND)
    start = pl.multiple_of(start, ATTN_BLOCK)
    q = q_ref[...]
    kw = k_ref[pl.ds(start, BAND), :]
    vw = v_ref[pl.ds(start, BAND), :]
    ck = ck_ref[0]
    cv = cv_ref[0]
    qpos = n * ATTN_BLOCK + lax.broadcasted_iota(I32, (ATTN_BLOCK, BAND), 0)
    kpos = start + lax.broadcasted_iota(I32, (ATTN_BLOCK, BAND), 1)
    valid = jnp.abs(qpos - kpos) <= ATTN_WINDOW
    for h in range(N_KV_HEADS):
        hs = slice(h * HEAD_DIM, (h + 1) * HEAD_DIM)
        for g in range(Q_PER_KV):
            j = h * Q_PER_KV + g
            qj = q[:, j * HEAD_DIM:(j + 1) * HEAD_DIM]
            s_loc = jnp.where(valid, _dot_nt(qj, kw[:, hs]) * ATTN_SCALE, NEG_BIG)
            s_ctx = _dot_nt(qj, ck[:, hs]) * ATTN_SCALE
            sk = sink_ref[e, j]
            m = jnp.maximum(jnp.maximum(jnp.max(s_loc, axis=-1, keepdims=True),
                                        jnp.max(s_ctx, axis=-1, keepdims=True)), sk)
            p_loc = jnp.exp(s_loc - m)
            p_ctx = jnp.exp(s_ctx - m)
            denom = (jnp.sum(p_loc, axis=-1, keepdims=True) + jnp.sum(p_ctx, axis=-1, keepdims=True)
                     + jnp.exp(sk - m))
            o = _dot(p_loc, vw[:, hs]) + _dot(p_ctx, cv[:, hs])
            o_ref[:, j * HEAD_DIM:(j + 1) * HEAD_DIM] = o / denom


def _attn_lat(e, sink, q, k, v, ck, cv, yb):
    past = ck.shape[1]
    q_blk0 = T_CTX // ATTN_BLOCK
    kv_blk0 = T_CTX // LAT_LEN
    q_spec = pl.BlockSpec((ATTN_BLOCK, Q_DIM), lambda b, n: (q_blk0 + b * LAT_BLOCKS + n, 0))
    return pl.pallas_call(
        functools.partial(_attn_lat_kernel, e),
        grid=(N_LAT_SEQ, LAT_BLOCKS),
        in_specs=[pl.BlockSpec(memory_space=pltpu.SMEM),
                  q_spec,
                  pl.BlockSpec((LAT_LEN, KV_DIM), lambda b, n: (kv_blk0 + b, 0)),
                  pl.BlockSpec((LAT_LEN, KV_DIM), lambda b, n: (kv_blk0 + b, 0)),
                  pl.BlockSpec((1, past, KV_DIM), lambda b, n: (b, 0, 0)),
                  pl.BlockSpec((1, past, KV_DIM), lambda b, n: (b, 0, 0)),
                  pl.BlockSpec(memory_space=pl.ANY)],
        out_specs=q_spec,
        out_shape=jax.ShapeDtypeStruct((T_ALL, Q_DIM), F32),
        input_output_aliases={6: 0},
        compiler_params=_params(),
        name="attn_lat",
    )(sink, q, k, v, ck, cv, yb)


def _router(h2, wr):
    hi = h2.astype(BF16)
    lo = (h2 - hi.astype(F32)).astype(BF16)
    whi = wr.astype(BF16)
    wlo = (wr - whi.astype(F32)).astype(BF16)
    dot = functools.partial(jnp.dot, preferred_element_type=F32)
    logits = dot(hi, whi) + (dot(lo, whi) + dot(hi, wlo))
    lane = lax.broadcasted_iota(I32, logits.shape, 1).astype(F32)
    neg = jnp.float32(-jnp.inf)
    big = jnp.float32(1e9)
    is_grp = lane < N_GROUPS
    gl = jnp.where(is_grp, logits, neg)
    gmax = jnp.max(gl, axis=-1, keepdims=True)
    gsum = jnp.sum(jnp.where(is_grp, jnp.exp(logits - gmax), 0.0), axis=-1, keepdims=True)
    g_w = 1.0 / gsum
    g_idx = jnp.min(jnp.where(gl == gmax, lane, big), axis=-1, keepdims=True)
    base = N_GROUPS + EPG * g_idx
    in_grp = jnp.logical_and(lane >= base, lane < base + EPG)
    el = jnp.where(in_grp, logits, neg)
    t1 = jnp.max(el, axis=-1, keepdims=True)
    i1 = jnp.min(jnp.where(el == t1, lane, big), axis=-1, keepdims=True)
    el2 = jnp.where(lane == i1, neg, el)
    t2 = jnp.max(el2, axis=-1, keepdims=True)
    i2 = jnp.min(jnp.where(el2 == t2, lane, big), axis=-1, keepdims=True)
    d = jnp.exp(t2 - t1)
    w1 = g_w / (1.0 + d)
    w2 = g_w * d / (1.0 + d)
    return i1 - N_GROUPS, i2 - N_GROUPS, w1, w2


LIST_SPAN = 32
LIST_ROWS = T_CTX // LIST_SPAN
LIST_COLS = N_EXPERTS * LIST_SPAN


N_TAIL_OUT = 7
N_TAIL_SCRATCH = 3


def _mixer_tail(l, x, y, mod_ref, r, gffn_ref, wr_ref, outs, scratch):
    x1_ref, h2_ref, cnt_ref, lc_ref, gc_ref, ll_ref, gl_ref = outs
    run_ref, acc_ref, gacc_ref = scratch
    i = pl.program_id(0)
    g1 = mod_ref[l, pl.ds(r, 1), 2 * D:3 * D]
    shift2 = mod_ref[l, pl.ds(r, 1), 3 * D:4 * D]
    scale2 = mod_ref[l, pl.ds(r, 1), 4 * D:5 * D]
    x1 = x + g1 * y
    x1_ref[...] = x1
    h2 = _rms_mod(x1, gffn_ref[l:l + 1, :], scale2, shift2)
    for s in range(TOK_ROWS):
        h2_ref[pl.ds(s, TM, stride=TOK_ROWS), :] = h2[:, s * LANES:(s + 1) * LANES]
    e1, e2, w1, w2 = _router(h2, wr_ref[...])

    @pl.when(jnp.logical_or(i == 0, i == NT_CTX))
    def _():
        run_ref[...] = jnp.zeros(run_ref.shape, F32)
        acc_ref[...] = jnp.zeros(acc_ref.shape, F32)
        gacc_ref[...] = jnp.zeros(gacc_ref.shape, F32)

    @pl.when(i == 0)
    def _():
        cnt_ref[...] = jnp.zeros(cnt_ref.shape, I32)

    lane_i = lax.broadcasted_iota(I32, (TM, LANES), 1)
    lane = lane_i.astype(F32)
    member = jnp.where(jnp.logical_or(lane == e1, lane == e2), 1.0, 0.0)
    tri = jnp.where(lax.broadcasted_iota(I32, (TM, TM), 0) >= lax.broadcasted_iota(I32, (TM, TM), 1), 1.0, 0.0)
    csum = jnp.dot(tri.astype(BF16), member.astype(BF16), preferred_element_type=F32)
    run = run_ref[0:1, :]
    before = csum - member + run
    r1 = jnp.sum(jnp.where(lane == e1, before, 0.0), axis=-1, keepdims=True)
    r2 = jnp.sum(jnp.where(lane == e2, before, 0.0), axis=-1, keepdims=True)
    run_new = run + csum[TM - 1:TM, :]
    run_ref[0:1, :] = run_new

    col_i = lax.broadcasted_iota(I32, (TM, LIST_COLS), 1)
    rows_oh, cols_oh = [], []
    for e, rank in ((e1, r1), (e2, r2)):
        rank_i = rank.astype(I32)
        rows_oh.append(jnp.where(lane_i == rank_i // LIST_SPAN, 1.0, 0.0))
        cols_oh.append(jnp.where(col_i == e.astype(I32) * LIST_SPAN + rank_i % LIST_SPAN, 1.0, 0.0).astype(BF16))
    rows_t = jnp.concatenate(rows_oh, axis=0).T
    cols = jnp.concatenate(cols_oh, axis=0)
    tok = (lax.broadcasted_iota(I32, (1, 2 * TM), 1) % TM).astype(F32)
    dot = functools.partial(jnp.dot, preferred_element_type=F32)
    hits = dot(rows_t.astype(BF16), cols)
    tok_sum = dot((rows_t * tok).astype(BF16), cols)
    tile_in_pop = jnp.where(i < NT_CTX, i, i - NT_CTX).astype(F32)
    acc = acc_ref[...] + TOK_ROWS * (tok_sum + TM * tile_in_pop * hits)
    acc_ref[...] = acc

    rec = jnp.where(lane_i == 0, w1, jnp.where(lane_i == 1, w2, 0.0)).T
    rest = jnp.concatenate([rec[0:1, :], rec[1:2, :]], axis=1)
    gacc = gacc_ref[...]
    for _ in range(3):
        piece = rest.astype(BF16)
        gacc = gacc + dot((rows_t * piece.astype(F32)).astype(BF16), cols)
        rest = rest - piece.astype(F32)
    gacc_ref[...] = gacc

    @pl.when(i == NT_CTX - 1)
    def _():
        cnt_ref[0:1, :] = run_new.astype(I32)
        lc_ref[...] = acc.astype(I32)
        gc_ref[...] = gacc

    @pl.when(i == NT - 1)
    def _():
        cnt_ref[1:2, :] = run_new.astype(I32)
        ll_ref[...] = acc.astype(I32)
        gl_ref[...] = gacc


_TAIL_OUT_SHAPES = [jax.ShapeDtypeStruct((T_ALL, D), F32),
                    jax.ShapeDtypeStruct((T_ALL * TOK_ROWS, LANES), F32),
                    jax.ShapeDtypeStruct((SUBLANES, LANES), I32)] + [
                        jax.ShapeDtypeStruct((LIST_ROWS, LIST_COLS), dt) for dt in (I32, F32, I32, F32)]


def _tail_scratch():
    return [pltpu.VMEM((SUBLANES, LANES), F32), pltpu.VMEM((LIST_ROWS, LIST_COLS), F32),
            pltpu.VMEM((LIST_ROWS, LIST_COLS), F32)]


def _tail_out_specs():
    return [_tile_spec(D),
            pl.BlockSpec((TM * TOK_ROWS, LANES), lambda i: (i, 0)),
            _full_spec((SUBLANES, LANES))] + [_full_spec((LIST_ROWS, LIST_COLS)) for _ in range(4)]


def _halo_specs(halo, width):
    per = TM // halo
    last = T_ALL // halo - 1
    prev = pl.BlockSpec((halo, width), lambda i: (jnp.maximum(i * per - 1, 0), 0))
    nxt = pl.BlockSpec((halo, width), lambda i: (jnp.minimum((i + 1) * per, last), 0))
    return prev, nxt


def _mid_even_kernel(l, a_ref, ap_ref, an_ref, yb_ref, xc_ref, xl_ref, mod_ref, wp_ref, ps_ref, wo_ref,
                     gffn_ref, wr_ref, *refs):
    outs, (pad_ref,), scratch = refs[:N_TAIL_OUT], refs[N_TAIL_OUT:-N_TAIL_SCRATCH], refs[-N_TAIL_SCRATCH:]
    i = pl.program_id(0)
    r = _mod_row(i)
    off, n, first, last = _seq_info(i)
    a = a_ref[...]
    pad_ref[0:POOL_HALO, :] = jnp.where(first, 0.0, ap_ref[...])
    pad_ref[POOL_HALO:POOL_HALO + TM, :] = a
    pad_ref[POOL_HALO + TM:, :] = jnp.where(last, 0.0, an_ref[...])
    t = off + lax.broadcasted_iota(I32, (TM, 1), 0)
    mixed = []
    for g, w in enumerate(POOL_WINDOWS):
        lo = w // 2
        hi = w - lo - 1
        cols = slice(g * POOL_GROUP_DIM, (g + 1) * POOL_GROUP_DIM)
        total = pad_ref[POOL_HALO - lo:POOL_HALO - lo + TM, cols]
        for j in range(-lo + 1, hi + 1):
            total = total + pad_ref[POOL_HALO + j:POOL_HALO + j + TM, cols]
        count = (jnp.minimum(t + hi, n - 1) - jnp.maximum(t - lo, 0) + 1).astype(F32)
        pooled = total / count - a[:, cols]
        mixed.append(_dot(pooled, wp_ref[g]))
    ya = jnp.concatenate(mixed, axis=1) * ps_ref[...]
    y = _dot(ya, wo_ref[0:POOL_DIM, :]) + _dot(yb_ref[...], wo_ref[POOL_DIM:, :])
    x = jnp.where(i < NT_CTX, xc_ref[...], xl_ref[...])
    _mixer_tail(l, x, y, mod_ref, r, gffn_ref, wr_ref, outs, scratch)


def _mid_even(l, e, a, yb, xc, xl, mod, w_pool, pool_scale, w_out, gffn, wr):
    prev, nxt = _halo_specs(POOL_HALO, POOL_DIM)
    return pl.pallas_call(
        functools.partial(_mid_even_kernel, l),
        grid=(NT,),
        in_specs=[_tile_spec(POOL_DIM), prev, nxt, _tile_spec(Q_DIM), _CTX_TILE, _LAT_TILE, _MOD_SPEC,
                  _layer_spec((len(POOL_WINDOWS), POOL_GROUP_DIM, POOL_GROUP_DIM), e),
                  _layer_spec((1, POOL_DIM), e), _layer_spec((D, D), e), _full_spec((DEPTH, D)),
                  _layer_spec((D, LANES), l)],
        out_specs=_tail_out_specs(),
        out_shape=_TAIL_OUT_SHAPES,
        scratch_shapes=[pltpu.VMEM((TM + 2 * POOL_HALO, POOL_DIM), F32)] + _tail_scratch(),
        compiler_params=_params(),
        name="mid_even",
    )(a, a, a, yb, xc, xl, mod, w_pool, pool_scale, w_out, gffn, wr)


def _in_odd_kernel(l, x1_ref, moe_ref, mod_ref, g_ref, w1_ref, b1_ref, x2_ref, u_ref):
    i = pl.program_id(0)
    r = _mod_row(i)
    g2 = mod_ref[l - 1, pl.ds(r, 1), 5 * D:6 * D]
    x2 = x1_ref[...] + g2 * _from_token_tiles(moe_ref)
    x2_ref[...] = x2
    shift = mod_ref[l, pl.ds(r, 1), 0:D]
    scale = mod_ref[l, pl.ds(r, 1), D:2 * D]
    h = _rms_mod(x2, g_ref[l:l + 1, :], scale, shift)
    u = _dot(h, w1_ref[...]) + b1_ref[...]
    u_ref[...] = u[:, :D] * (1.0 / (1.0 + jnp.exp(-u[:, D:])))


def _in_odd(l, o, x1, moe_tt, mod, g, w1, b1):
    return pl.pallas_call(
        functools.partial(_in_odd_kernel, l),
        grid=(NT,),
        in_specs=[_tile_spec(D), pl.BlockSpec((TM * TOK_ROWS, LANES), lambda i: (i, 0)), _MOD_SPEC,
                  _full_spec((DEPTH, D)), _layer_spec((D, 2 * D), o), _layer_spec((1, 2 * D), o)],
        out_specs=[_tile_spec(D), _tile_spec(D)],
        out_shape=[jax.ShapeDtypeStruct((T_ALL, D), F32), jax.ShapeDtypeStruct((T_ALL, D), F32)],
        compiler_params=_params(),
        name="in_odd",
    )(x1, moe_tt, mod, g, w1, b1)


CONV_PAD_ROWS = TM + 2 * CONV_HALO
CONV_CHUNKS = D // LANES
CONV_BLOCK = 64


def _mid_odd_kernel(l, u_ref, up_ref, un_ref, x_ref, mod_ref, dw_ref, dwb_ref, lng_ref, lnb_ref, w2_ref, b2_ref,
                    gffn_ref, wr_ref, *refs):
    outs, scratch = refs[:N_TAIL_OUT], refs[-N_TAIL_SCRATCH:]
    pad_ref, conv_ref = refs[N_TAIL_OUT:-N_TAIL_SCRATCH]
    i = pl.program_id(0)
    r = _mod_row(i)
    _, _, first, last = _seq_info(i)
    u = u_ref[...]
    up = jnp.where(first, 0.0, up_ref[...])
    un = jnp.where(last, 0.0, un_ref[...])
    half = CONV_WIDTH // 2
    first_row = CONV_HALO - half
    for c in range(CONV_CHUNKS):
        cols = slice(c * LANES, (c + 1) * LANES)
        pad_ref[c, 0:CONV_HALO, :] = up[:, cols]
        pad_ref[c, CONV_HALO:CONV_HALO + TM, :] = u[:, cols]
        pad_ref[c, CONV_HALO + TM:, :] = un[:, cols]
        bias = jnp.broadcast_to(dwb_ref[:, cols], (CONV_BLOCK, LANES))
        for r0 in range(0, TM, 2 * CONV_BLOCK):
            acc_even, acc_odd = bias, bias
            for s in range(first_row, first_row + CONV_WIDTH + 1):
                win = pad_ref[c, pl.ds(r0 + s, CONV_BLOCK, stride=2), :]
                j = s - first_row
                if j < CONV_WIDTH:
                    acc_even = acc_even + win * dw_ref[j:j + 1, cols]
                if j >= 1:
                    acc_odd = acc_odd + win * dw_ref[j - 1:j, cols]
            conv_ref[c, pl.ds(r0, CONV_BLOCK, stride=2), :] = acc_even
            conv_ref[c, pl.ds(r0 + 1, CONV_BLOCK, stride=2), :] = acc_odd
    acc = jnp.concatenate([conv_ref[c] for c in range(CONV_CHUNKS)], axis=1)
    mu = jnp.mean(acc, axis=-1, keepdims=True)
    cen = acc - mu
    var = jnp.mean(cen * cen, axis=-1, keepdims=True)
    v = _silu(cen * lax.rsqrt(var + EPS) * lng_ref[...] + lnb_ref[...])
    y = _dot(v, w2_ref[...]) + b2_ref[...]
    _mixer_tail(l, x_ref[...], y, mod_ref, r, gffn_ref, wr_ref, outs, scratch)


def _mid_odd(l, o, u, x, mod, dw, dwb, lng, lnb, w2, b2, gffn, wr):
    prev, nxt = _halo_specs(CONV_HALO, D)
    return pl.pallas_call(
        functools.partial(_mid_odd_kernel, l),
        grid=(NT,),
        in_specs=[_tile_spec(D), prev, nxt, _tile_spec(D), _MOD_SPEC, _layer_spec((CONV_WIDTH, D), o),
                  _layer_spec((1, D), o), _layer_spec((1, D), o), _layer_spec((1, D), o),
                  _layer_spec((D, D), o), _layer_spec((1, D), o), _full_spec((DEPTH, D)),
                  _layer_spec((D, LANES), l)],
        out_specs=_tail_out_specs(),
        out_shape=_TAIL_OUT_SHAPES,
        scratch_shapes=[pltpu.VMEM((CONV_CHUNKS, CONV_PAD_ROWS, LANES), F32),
                        pltpu.VMEM((CONV_CHUNKS, TM, LANES), F32)] + _tail_scratch(),
        compiler_params=_params(),
        name="mid_odd",
    )(u, u, u, x, mod, dw, dwb, lng, lnb, w2, b2, gffn, wr)


def _tile_tables_kernel(cnt_ref, te_c, tc_c, nv_c, te_l, tc_l, nv_l, nact_ref, first_ref):
    for p, (te_ref, tc_ref, nv_ref) in enumerate(((te_c, tc_c, nv_c), (te_l, tc_l, nv_l))):
        run = jnp.int32(0)
        ends = []
        for e in range(N_EXPERTS):
            first_ref[p * N_EXPERTS + e] = run
            run = run + (cnt_ref[p, e] + MOE_TM - 1) // MOE_TM
            ends.append(run)
        nact_ref[p] = run

        def expert_of(tile, ends=ends):
            e = jnp.int32(0)
            for k in range(N_EXPERTS):
                e = e + jnp.where(tile >= ends[k], 1, 0)
            return e

        last_e = expert_of(run - 1)

        def tile_body(j, c, te_ref=te_ref, tc_ref=tc_ref, nv_ref=nv_ref, p=p, expert_of=expert_of, last_e=last_e):
            e = jnp.minimum(expert_of(j), last_e)
            chunk = j - first_ref[p * N_EXPERTS + e]
            te_ref[j] = e
            tc_ref[j] = chunk
            nv_ref[j] = jnp.clip(cnt_ref[p, e] - chunk * MOE_TM, 0, MOE_TM)
            return c

        lax.fori_loop(0, te_ref.shape[0], tile_body, 0)


def _tile_tables(cnt):
    smem = pl.BlockSpec(memory_space=pltpu.SMEM)
    shapes = [(MOE_TILES_CTX,)] * 3 + [(MOE_TILES_LAT,)] * 3 + [(2,)]
    return pl.pallas_call(
        _tile_tables_kernel,
        in_specs=[smem],
        out_specs=[smem] * len(shapes),
        out_shape=[jax.ShapeDtypeStruct(s, I32) for s in shapes],
        scratch_shapes=[pltpu.SMEM((2 * N_EXPERTS,), I32)],
        name="tile_tables",
    )(cnt)


def _moe_kernel(pop, te_ref, tc_ref, nval_ref, nact_ref, list_ref, gate_ref, h_ref, w1_ref, w3_ref, w2_ref, *rest):
    o_ref, xs_ref, ys_ref = rest[-3:]
    j = pl.program_id(0)

    @pl.when(j == 0)
    def _():
        o_ref[...] = jnp.zeros(o_ref.shape, F32)
        xs_ref[...] = jnp.zeros(xs_ref.shape, F32)

    @pl.when(j < nact_ref[pop])
    def _():
        nval = nval_ref[j]
        nfull = nval // MOVE_BATCH
        list_base = tc_ref[j] * (MOE_TM // LIST_SPAN) * LIST_COLS + te_ref[j] * LIST_SPAN

        def entry(r):
            return list_base + (r // LIST_SPAN) * LIST_COLS + r % LIST_SPAN

        def tok_rows(i):
            return pl.ds(pl.multiple_of(list_ref[i], TOK_ROWS), TOK_ROWS)

        def rows(r):
            return pl.ds(pl.multiple_of(r * TOK_ROWS, TOK_ROWS), TOK_ROWS)

        def gather_batch(b, c):
            r0 = b * MOVE_BATCH
            i0 = entry(r0)
            for k in range(MOVE_BATCH):
                xs_ref[rows(r0 + k), :] = h_ref[tok_rows(i0 + k), :]
            return c

        def gather_one(r, c):
            xs_ref[rows(r), :] = h_ref[tok_rows(entry(r)), :]
            return c

        lax.fori_loop(0, nfull, gather_batch, 0)
        lax.fori_loop(nfull * MOVE_BATCH, nval, gather_one, 0)

        x = jnp.concatenate([xs_ref[pl.ds(s, MOE_TM, stride=TOK_ROWS), :] for s in range(TOK_ROWS)], axis=1)
        x = x.astype(BF16)
        hg = jnp.dot(x, w1_ref[...].astype(BF16), preferred_element_type=F32)
        hu = jnp.dot(x, w3_ref[...].astype(BF16), preferred_element_type=F32)
        y = _dot(_silu(hg) * hu, w2_ref[...])
        for s in range(TOK_ROWS):
            ys_ref[pl.ds(s, MOE_TM, stride=TOK_ROWS), :] = y[:, s * LANES:(s + 1) * LANES]

        def scatter_batch(b, c):
            r0 = b * MOVE_BATCH
            i0 = entry(r0)
            dst = [tok_rows(i0 + k) for k in range(MOVE_BATCH)]
            vals = [o_ref[dst[k], :] + gate_ref[i0 + k] * ys_ref[rows(r0 + k), :] for k in range(MOVE_BATCH)]
            for k in range(MOVE_BATCH):
                o_ref[dst[k], :] = vals[k]
            return c

        def scatter_one(r, c):
            i = entry(r)
            o_ref[tok_rows(i), :] = o_ref[tok_rows(i), :] + gate_ref[i] * ys_ref[rows(r), :]
            return c

        lax.fori_loop(0, nfull, scatter_batch, 0)
        lax.fori_loop(nfull * MOVE_BATCH, nval, scatter_one, 0)


def _moe(l, pop, h_tt, prev, te, tc, nval, nact, rows, gate, w1, w3, w2):
    tp = (T_CTX, T_LAT)[pop]
    blk = (0, T_CTX // T_LAT)[pop]
    n_tiles = te.shape[0]
    wmap = lambda j, te, *_: (l, te[j], 0, 0)
    in_specs = [
        pl.BlockSpec((tp * TOK_ROWS, LANES), lambda j, *_: (blk, 0), pipeline_mode=pl.Buffered(1)),
        pl.BlockSpec((None, None, D, EXPERT_HIDDEN), wmap),
        pl.BlockSpec((None, None, D, EXPERT_HIDDEN), wmap),
        pl.BlockSpec((None, None, EXPERT_HIDDEN, D), wmap),
    ]
    args = [te, tc, nval, nact, rows, gate, h_tt, w1, w3, w2]
    aliases = {}
    if prev is not None:
        in_specs.append(pl.BlockSpec(memory_space=pl.ANY))
        aliases = {len(args): 0}
        args.append(prev)
    grid_spec = pltpu.PrefetchScalarGridSpec(
        num_scalar_prefetch=6,
        grid=(n_tiles,),
        in_specs=in_specs,
        out_specs=pl.BlockSpec((tp * TOK_ROWS, LANES), lambda j, *_: (blk, 0), pipeline_mode=pl.Buffered(1)),
        scratch_shapes=[pltpu.VMEM((MOE_TM * TOK_ROWS, LANES), F32),
                        pltpu.VMEM((MOE_TM * TOK_ROWS, LANES), F32)],
    )
    return pl.pallas_call(
        functools.partial(_moe_kernel, pop),
        grid_spec=grid_spec,
        out_shape=jax.ShapeDtypeStruct((T_ALL * TOK_ROWS, LANES), F32),
        input_output_aliases=aliases,
        compiler_params=_params(),
        name="moe",
    )(*args)


def _moe_all(l, h_tt, cnt, rows_c, gate_c, rows_l, gate_l, w1, w3, w2):
    te_c, tc_c, nv_c, te_l, tc_l, nv_l, nact = _tile_tables(cnt)
    out = _moe(l, 0, h_tt, None, te_c, tc_c, nv_c, nact, rows_c.reshape(-1), gate_c.reshape(-1), w1, w3, w2)
    return _moe(l, 1, h_tt, out, te_l, tc_l, nv_l, nact, rows_l.reshape(-1), gate_l.reshape(-1), w1, w3, w2)


def _final_kernel(l, x1_ref, moe_ref, mod_ref, g_ref, oc_ref, ol_ref):
    i = pl.program_id(0)
    r = _mod_row(i)
    g2 = mod_ref[l, pl.ds(r, 1), 5 * D:6 * D]
    x = x1_ref[...] + g2 * _from_token_tiles(moe_ref)
    ms = jnp.mean(x * x, axis=-1, keepdims=True)
    y = x * lax.rsqrt(ms + EPS) * g_ref[...]

    @pl.when(i < NT_CTX)
    def _():
        oc_ref[...] = y

    @pl.when(i >= NT_CTX)
    def _():
        ol_ref[...] = y


def _final(l, x1, moe_tt, mod, g):
    return pl.pallas_call(
        functools.partial(_final_kernel, l),
        grid=(NT,),
        in_specs=[_tile_spec(D), pl.BlockSpec((TM * TOK_ROWS, LANES), lambda i: (i, 0)), _MOD_SPEC,
                  _full_spec((1, D))],
        out_specs=[_CTX_TILE, _LAT_TILE],
        out_shape=[jax.ShapeDtypeStruct((T_CTX, D), F32), jax.ShapeDtypeStruct((T_LAT, D), F32)],
        compiler_params=_params(),
        name="final_norm",
    )(x1, moe_tt, mod, g)


def _rope_tables():
    t = jnp.arange(LAT_LEN)
    row = (t // GRID_W).astype(F32)
    col = (t % GRID_W).astype(F32)
    inv = ROPE_BASE ** (-jnp.arange(0, ROPE_AXIS_DIM, 2, dtype=F32) / ROPE_AXIS_DIM)

    def table(p):
        ang = p[:, None] * inv[None, :]
        ang = jnp.concatenate([ang, ang], axis=-1)
        return jnp.cos(ang), jnp.sin(ang)

    (cr, sr), (cc, sc) = table(row), table(col)
    cos = jnp.concatenate([cr, cc, cr, cc], axis=-1)
    sin = jnp.concatenate([sr, sc, sr, sc], axis=-1)
    return cos, sin


def _router_weights(w_grp, w_exp):
    we = jnp.transpose(w_exp, (0, 2, 1, 3)).reshape(DEPTH, D, N_GROUPS * EPG)
    pad = jnp.zeros((DEPTH, D, LANES - N_GROUPS - N_GROUPS * EPG), F32)
    return jnp.concatenate([w_grp, we, pad], axis=-1)


def kernel(x_prompt, x_sample, cache_k, cache_v, c, c_ctx, w_ada, b_ada, norm_mix_g, norm_ffn_g, w_in_ab, pool_w, pool_scale, attn_sink, w_out_ab, conv_w1, conv_b1, conv_dw, conv_dw_b, conv_ln_g, conv_ln_b, conv_w2, conv_b2, router_grp, router_exp, moe_w1, moe_w3, moe_w2, final_g):
    xc = x_prompt.reshape(T_CTX, D)
    xl = x_sample.reshape(T_LAT, D)
    cond8 = jnp.concatenate([c_ctx[None, :], c, jnp.zeros((SUBLANES - 1 - N_LAT_SEQ, D), F32)], axis=0)
    mod = _modulation(cond8, w_ada, b_ada)
    cos_t, sin_t = _rope_tables()
    wr = _router_weights(router_grp, router_exp)
    n_even, n_odd = w_in_ab.shape[0], conv_w1.shape[0]
    past = cache_k.shape[2]

    a, q, k, v = _in_even(0, 0, xc, xl, mod, norm_mix_g, w_in_ab, cos_t, sin_t)
    yb = _attn_ctx(0, attn_sink, q, k, v)
    yb = _attn_lat(0, attn_sink, q, k, v, cache_k[:, 0].reshape(N_LAT_SEQ, past, KV_DIM),
                   cache_v[:, 0].reshape(N_LAT_SEQ, past, KV_DIM), yb)
    x1, h2, *routing = _mid_even(0, 0, a, yb, xc, xl, mod, pool_w, pool_scale.reshape(n_even, 1, POOL_DIM),
                                 w_out_ab, norm_ffn_g, wr)
    moe0 = _moe_all(0, h2, *routing, moe_w1, moe_w3, moe_w2)

    vec = lambda p: p.reshape(n_odd, 1, -1)
    x2, u = _in_odd(1, 0, x1, moe0, mod, norm_mix_g, conv_w1, vec(conv_b1))
    x3, h2b, *routing = _mid_odd(1, 0, u, x2, mod, conv_dw, vec(conv_dw_b), vec(conv_ln_g),
                                 vec(conv_ln_b), conv_w2, vec(conv_b2), norm_ffn_g, wr)
    moe1 = _moe_all(1, h2b, *routing, moe_w1, moe_w3, moe_w2)
    y_c, y_l = _final(1, x3, moe1, mod, final_g.reshape(1, D))

    y_prompt = y_c.reshape(N_CTX_SEQ, CTX_LEN, D)
    y_sample = y_l.reshape(N_LAT_SEQ, LAT_LEN, D)
    state_k = k[:T_CTX].reshape(N_CTX_SEQ, 1, CTX_LEN, N_KV_HEADS, HEAD_DIM)
    state_v = v[:T_CTX].reshape(N_CTX_SEQ, 1, CTX_LEN, N_KV_HEADS, HEAD_DIM)
    return (y_prompt, y_sample, state_k, state_v)
```

```python
import functools

import jax
import jax.numpy as jnp
from jax import lax
from jax.experimental import pallas as pl
from jax.experimental.pallas import tpu as pltpu

F32 = jnp.float32
BF16 = jnp.bfloat16
I32 = jnp.int32

D = 1024
N_CTX_SEQ = 16
CTX_LEN = 256
N_LAT_SEQ = 2
LAT_LEN = 1024
T_CTX = N_CTX_SEQ * CTX_LEN
T_LAT = N_LAT_SEQ * LAT_LEN
T_ALL = T_CTX + T_LAT
TM = 256
NT = T_ALL // TM
NT_CTX = T_CTX // TM
LAT_TILES_PER_SEQ = LAT_LEN // TM
GRID_W = 64
DEPTH = 2

POOL_WINDOWS = (2, 4, 8, 16)
POOL_GROUP_DIM = 128
POOL_DIM = 512
HEAD_DIM = 64
N_Q_HEADS = 8
N_KV_HEADS = 2
Q_PER_KV = 4
Q_DIM = 512
KV_DIM = 128
IN_AB = 1280
ATTN_WINDOW = 128
ATTN_BLOCK = 128
ATTN_SCALE = HEAD_DIM ** -0.5
ROPE_BASE = 10000.0
ROPE_AXIS_DIM = 32
CONV_WIDTH = 31
CONV_HALO = 16
POOL_HALO = 8
N_GROUPS = 4
EPG = 4
N_EXPERTS = 16
EXPERT_HIDDEN = 512
EPS = 1e-6
NEG_BIG = -1e30

SUBLANES = 8
LANES = 128
TOK_ROWS = D // LANES

MOE_TM = 256
MOE_TILES_CTX = 2 * T_CTX // MOE_TM + N_EXPERTS
MOE_TILES_LAT = 2 * T_LAT // MOE_TM + N_EXPERTS
MOVE_BATCH_LOG2 = 4
MOVE_BATCH = 1 << MOVE_BATCH_LOG2
VMEM_LIMIT = 56 * 1024 * 1024


def _silu(x):
    return x * (1.0 / (1.0 + jnp.exp(-x)))


def _mod_row(i):
    return jnp.where(i < NT_CTX, 0, 1 + (i - NT_CTX) // LAT_TILES_PER_SEQ)


def _seq_info(i):
    is_ctx = i < NT_CTX
    k = (i - NT_CTX) % LAT_TILES_PER_SEQ
    off = jnp.where(is_ctx, 0, k * TM)
    n = jnp.where(is_ctx, CTX_LEN, LAT_LEN)
    first = jnp.logical_or(is_ctx, k == 0)
    last = jnp.logical_or(is_ctx, k == LAT_TILES_PER_SEQ - 1)
    return off, n, first, last


def _rms_mod(x, g, scale, shift):
    ms = jnp.mean(x * x, axis=-1, keepdims=True)
    return (x * lax.rsqrt(ms + EPS) * g) * (1.0 + scale) + shift


def _dot(a, b):
    return jnp.dot(a.astype(BF16), b.astype(BF16), preferred_element_type=F32)


def _dot_nt(a, b):
    return lax.dot_general(a.astype(BF16), b.astype(BF16), (((1,), (1,)), ((), ())),
                           preferred_element_type=F32)


def _from_token_tiles(ref):
    return jnp.concatenate([ref[pl.ds(s, TM, stride=TOK_ROWS), :] for s in range(TOK_ROWS)], axis=1)


def _tile_spec(width):
    return pl.BlockSpec((TM, width), lambda i: (i, 0))


def _full_spec(shape):
    return pl.BlockSpec(shape, lambda i: (0,) * len(shape))


def _layer_spec(shape, l):
    return pl.BlockSpec((None,) + tuple(shape), lambda i: (l,) + (0,) * len(shape))


def _pop_specs(rows, width):
    return [pl.BlockSpec((rows, width), lambda i: (jnp.minimum(i, NT_CTX - 1), 0)),
            pl.BlockSpec((rows, width), lambda i: (jnp.maximum(i - NT_CTX, 0), 0))]


def _pick(i, read, ctx_ref, lat_ref):
    return lax.cond(i < NT_CTX, lambda: read(ctx_ref), lambda: read(lat_ref))


def _whole(ref):
    return ref[...]
_MOD_SPEC = pl.BlockSpec((DEPTH, SUBLANES, 6 * D), lambda i: (0, 0, 0))


def _params():
    return pltpu.CompilerParams(vmem_limit_bytes=VMEM_LIMIT)


MOD_TN = 1536


def _mod_kernel(cond_ref, w_ref, b_ref, o_ref):
    s = _silu(cond_ref[...])
    o_ref[0] = _dot(s, w_ref[0]) + b_ref[0]


def _modulation(cond8, w_ada, b_ada):
    return pl.pallas_call(
        _mod_kernel,
        grid=(DEPTH, 6 * D // MOD_TN),
        in_specs=[
            pl.BlockSpec((SUBLANES, D), lambda l, n: (0, 0)),
            pl.BlockSpec((1, D, MOD_TN), lambda l, n: (l, 0, n)),
            pl.BlockSpec((1, 1, MOD_TN), lambda l, n: (l, 0, n)),
        ],
        out_specs=pl.BlockSpec((1, SUBLANES, MOD_TN), lambda l, n: (l, 0, n)),
        out_shape=jax.ShapeDtypeStruct((DEPTH, SUBLANES, 6 * D), F32),
        compiler_params=_params(),
        name="modulation",
    )(cond8, w_ada, b_ada.reshape(DEPTH, 1, 6 * D))


def _rope_chunk(xc, cos, sin):
    lane = lax.broadcasted_iota(I32, xc.shape, 1)
    first = (lane % ROPE_AXIS_DIM) < (ROPE_AXIS_DIM // 2)
    rot = jnp.where(first, -pltpu.roll(xc, LANES - ROPE_AXIS_DIM // 2, 1), pltpu.roll(xc, ROPE_AXIS_DIM // 2, 1))
    return xc * cos + rot * sin


def _in_even_kernel(l, xc_ref, xl_ref, mod_ref, g_ref, w_ref, cos_ref, sin_ref, a_ref, q_ref, k_ref, v_ref):
    i = pl.program_id(0)
    r = _mod_row(i)
    x = _pick(i, _whole, xc_ref, xl_ref)
    shift = mod_ref[l, pl.ds(r, 1), 0:D]
    scale = mod_ref[l, pl.ds(r, 1), D:2 * D]
    h = _rms_mod(x, g_ref[l:l + 1, :], scale, shift)
    proj = _dot(h, w_ref[...])
    a_ref[...] = proj[:, :POOL_DIM]
    v_ref[...] = proj[:, POOL_DIM + Q_DIM + KV_DIM:]

    @pl.when(i < NT_CTX)
    def _():
        q_ref[...] = proj[:, POOL_DIM:POOL_DIM + Q_DIM]
        k_ref[...] = proj[:, POOL_DIM + Q_DIM:POOL_DIM + Q_DIM + KV_DIM]

    @pl.when(i >= NT_CTX)
    def _():
        off, _, _, _ = _seq_info(i)
        off = pl.multiple_of(off, TM)
        cos = cos_ref[pl.ds(off, TM), :]
        sin = sin_ref[pl.ds(off, TM), :]
        for c in range(Q_DIM // LANES):
            lo = POOL_DIM + c * LANES
            q_ref[:, c * LANES:(c + 1) * LANES] = _rope_chunk(proj[:, lo:lo + LANES], cos, sin)
        lo = POOL_DIM + Q_DIM
        k_ref[...] = _rope_chunk(proj[:, lo:lo + LANES], cos, sin)


def _in_even(l, e, xc, xl, mod, g, w_in, cos_t, sin_t):
    return pl.pallas_call(
        functools.partial(_in_even_kernel, l),
        grid=(NT,),
        in_specs=_pop_specs(TM, D) + [_MOD_SPEC, _full_spec((DEPTH, D)), _layer_spec((D, IN_AB), e),
                  _full_spec((LAT_LEN, LANES)), _full_spec((LAT_LEN, LANES))],
        out_specs=[_tile_spec(POOL_DIM), _tile_spec(Q_DIM), _tile_spec(KV_DIM), _tile_spec(KV_DIM)],
        out_shape=[jax.ShapeDtypeStruct((T_ALL, POOL_DIM), F32), jax.ShapeDtypeStruct((T_ALL, Q_DIM), F32),
                   jax.ShapeDtypeStruct((T_ALL, KV_DIM), F32), jax.ShapeDtypeStruct((T_ALL, KV_DIM), F32)],
        compiler_params=_params(),
        name="in_even",
    )(xc, xl, mod, g, w_in, cos_t, sin_t)


def _attn_ctx_kernel(e, sink_ref, q_ref, k_ref, v_ref, o_ref):
    q = q_ref[...]
    k = k_ref[...]
    v = v_ref[...]
    for h in range(N_KV_HEADS):
        kh = k[:, h * HEAD_DIM:(h + 1) * HEAD_DIM]
        vh = v[:, h * HEAD_DIM:(h + 1) * HEAD_DIM]
        for g in range(Q_PER_KV):
            j = h * Q_PER_KV + g
            s = _dot_nt(q[:, j * HEAD_DIM:(j + 1) * HEAD_DIM], kh) * ATTN_SCALE
            sk = sink_ref[e, j]
            m = jnp.maximum(jnp.max(s, axis=-1, keepdims=True), sk)
            p = jnp.exp(s - m)
            denom = jnp.sum(p, axis=-1, keepdims=True) + jnp.exp(sk - m)
            o_ref[:, j * HEAD_DIM:(j + 1) * HEAD_DIM] = _dot(p, vh) / denom


def _attn_ctx(e, sink, q, k, v):
    tile = lambda w: pl.BlockSpec((TM, w), lambda b: (b, 0))
    return pl.pallas_call(
        functools.partial(_attn_ctx_kernel, e),
        grid=(N_CTX_SEQ,),
        in_specs=[pl.BlockSpec(memory_space=pltpu.SMEM), tile(Q_DIM), tile(KV_DIM), tile(KV_DIM)],
        out_specs=tile(Q_DIM),
        out_shape=jax.ShapeDtypeStruct((T_CTX, Q_DIM), F32),
        compiler_params=_params(),
        name="attn_ctx",
    )(sink, q, k, v)


LAT_BLOCKS = LAT_LEN // ATTN_BLOCK
BAND = 3 * ATTN_BLOCK


def _attn_lat_kernel(e, sink_ref, q_ref, k_ref, v_ref, ck_ref, cv_ref, o_ref):
    n = pl.program_id(1)
    start = jnp.clip((n - 1) * ATTN_BLOCK, 0, LAT_LEN - BAND)
    start = pl.multiple_of(start, ATTN_BLOCK)
    q = q_ref[...]
    kw = k_ref[pl.ds(start, BAND), :]
    vw = v_ref[pl.ds(start, BAND), :]
    ck = ck_ref[0]
    cv = cv_ref[0]
    qpos = n * ATTN_BLOCK + lax.broadcasted_iota(I32, (ATTN_BLOCK, BAND), 0)
    kpos = start + lax.broadcasted_iota(I32, (ATTN_BLOCK, BAND), 1)
    valid = jnp.abs(qpos - kpos) <= ATTN_WINDOW
    for h in range(N_KV_HEADS):
        hs = slice(h * HEAD_DIM, (h + 1) * HEAD_DIM)
        for g in range(Q_PER_KV):
            j = h * Q_PER_KV + g
            qj = q[:, j * HEAD_DIM:(j + 1) * HEAD_DIM]
            s_loc = jnp.where(valid, _dot_nt(qj, kw[:, hs]) * ATTN_SCALE, NEG_BIG)
            s_ctx = _dot_nt(qj, ck[:, hs]) * ATTN_SCALE
            sk = sink_ref[e, j]
            m = jnp.maximum(jnp.maximum(jnp.max(s_loc, axis=-1, keepdims=True),
                                        jnp.max(s_ctx, axis=-1, keepdims=True)), sk)
            p_loc = jnp.exp(s_loc - m)
            p_ctx = jnp.exp(s_ctx - m)
            denom = (jnp.sum(p_loc, axis=-1, keepdims=True) + jnp.sum(p_ctx, axis=-1, keepdims=True)
                     + jnp.exp(sk - m))
            o = _dot(p_loc, vw[:, hs]) + _dot(p_ctx, cv[:, hs])
            o_ref[:, j * HEAD_DIM:(j + 1) * HEAD_DIM] = o / denom


def _attn_lat(e, sink, q, k, v, ck, cv):
    past = ck.shape[1]
    q_blk0 = T_CTX // ATTN_BLOCK
    kv_blk0 = T_CTX // LAT_LEN
    q_spec = pl.BlockSpec((ATTN_BLOCK, Q_DIM), lambda b, n: (q_blk0 + b * LAT_BLOCKS + n, 0))
    return pl.pallas_call(
        functools.partial(_attn_lat_kernel, e),
        grid=(N_LAT_SEQ, LAT_BLOCKS),
        in_specs=[pl.BlockSpec(memory_space=pltpu.SMEM),
                  q_spec,
                  pl.BlockSpec((LAT_LEN, KV_DIM), lambda b, n: (kv_blk0 + b, 0)),
                  pl.BlockSpec((LAT_LEN, KV_DIM), lambda b, n: (kv_blk0 + b, 0)),
                  pl.BlockSpec((1, past, KV_DIM), lambda b, n: (b, 0, 0)),
                  pl.BlockSpec((1, past, KV_DIM), lambda b, n: (b, 0, 0))],
        out_specs=pl.BlockSpec((ATTN_BLOCK, Q_DIM), lambda b, n: (b * LAT_BLOCKS + n, 0)),
        out_shape=jax.ShapeDtypeStruct((T_LAT, Q_DIM), F32),
        compiler_params=_params(),
        name="attn_lat",
    )(sink, q, k, v, ck, cv)


def _router(h2, wr_ref):
    hi = h2.astype(BF16)
    lo = (h2 - hi.astype(F32)).astype(BF16)
    whi = wr_ref[0]
    wlo = wr_ref[1]
    dot = functools.partial(jnp.dot, preferred_element_type=F32)
    logits = dot(hi, whi) + (dot(lo, whi) + dot(hi, wlo))
    lane = lax.broadcasted_iota(I32, logits.shape, 1).astype(F32)
    neg = jnp.float32(-jnp.inf)
    big = jnp.float32(1e9)
    is_grp = lane < N_GROUPS
    gl = jnp.where(is_grp, logits, neg)
    gmax = jnp.max(gl, axis=-1, keepdims=True)
    gsum = jnp.sum(jnp.where(is_grp, jnp.exp(logits - gmax), 0.0), axis=-1, keepdims=True)
    g_w = 1.0 / gsum
    g_idx = jnp.min(jnp.where(gl == gmax, lane, big), axis=-1, keepdims=True)
    base = N_GROUPS + EPG * g_idx
    in_grp = jnp.logical_and(lane >= base, lane < base + EPG)
    el = jnp.where(in_grp, logits, neg)
    t1 = jnp.max(el, axis=-1, keepdims=True)
    i1 = jnp.min(jnp.where(el == t1, lane, big), axis=-1, keepdims=True)
    el2 = jnp.where(lane == i1, neg, el)
    t2 = jnp.max(el2, axis=-1, keepdims=True)
    i2 = jnp.min(jnp.where(el2 == t2, lane, big), axis=-1, keepdims=True)
    d = jnp.exp(t2 - t1)
    w1 = g_w / (1.0 + d)
    w2 = g_w * d / (1.0 + d)
    return i1 - N_GROUPS, i2 - N_GROUPS, w1, w2


LIST_SPAN_LOG2 = 5
LIST_SPAN = 1 << LIST_SPAN_LOG2
LIST_ROWS = T_CTX // LIST_SPAN
LIST_COLS = N_EXPERTS * LIST_SPAN


N_TAIL_OUT = 7
N_TAIL_SCRATCH = 3


def _mixer_tail(l, x, y, mod_ref, r, gffn_ref, wr_ref, outs, scratch):
    x1_ref, h2_ref, cnt_ref, lc_ref, gc_ref, ll_ref, gl_ref = outs
    run_ref, acc_ref, gacc_ref = scratch
    i = pl.program_id(0)
    g1 = mod_ref[l, pl.ds(r, 1), 2 * D:3 * D]
    shift2 = mod_ref[l, pl.ds(r, 1), 3 * D:4 * D]
    scale2 = mod_ref[l, pl.ds(r, 1), 4 * D:5 * D]
    x1 = x + g1 * y
    x1_ref[...] = x1
    h2 = _rms_mod(x1, gffn_ref[l:l + 1, :], scale2, shift2)
    for s in range(TOK_ROWS):
        h2_ref[pl.ds(s, TM, stride=TOK_ROWS), :] = h2[:, s * LANES:(s + 1) * LANES]
    e1, e2, w1, w2 = _router(h2, wr_ref)

    @pl.when(jnp.logical_or(i == 0, i == NT_CTX))
    def _():
        run_ref[...] = jnp.zeros(run_ref.shape, F32)
        acc_ref[...] = jnp.zeros(acc_ref.shape, F32)
        gacc_ref[...] = jnp.zeros(gacc_ref.shape, F32)

    @pl.when(i == 0)
    def _():
        cnt_ref[...] = jnp.zeros(cnt_ref.shape, I32)

    lane_i = lax.broadcasted_iota(I32, (TM, LANES), 1)
    lane = lane_i.astype(F32)
    member = jnp.where(jnp.logical_or(lane == e1, lane == e2), 1.0, 0.0)
    tri = jnp.where(lax.broadcasted_iota(I32, (TM, TM), 0) >= lax.broadcasted_iota(I32, (TM, TM), 1), 1.0, 0.0)
    csum = jnp.dot(tri.astype(BF16), member.astype(BF16), preferred_element_type=F32)
    run = run_ref[0:1, :]
    before = csum - member + run
    r1 = jnp.sum(jnp.where(lane == e1, before, 0.0), axis=-1, keepdims=True)
    r2 = jnp.sum(jnp.where(lane == e2, before, 0.0), axis=-1, keepdims=True)
    run_new = run + csum[TM - 1:TM, :]
    run_ref[0:1, :] = run_new

    col_i = lax.broadcasted_iota(I32, (TM, LIST_COLS), 1)
    rows_oh, cols_oh = [], []
    for e, rank in ((e1, r1), (e2, r2)):
        rank_i = rank.astype(I32)
        rows_oh.append(jnp.where(lane_i == rank_i // LIST_SPAN, 1.0, 0.0))
        cols_oh.append(jnp.where(col_i == e.astype(I32) * LIST_SPAN + rank_i % LIST_SPAN, 1.0, 0.0).astype(BF16))
    rows_t = jnp.concatenate(rows_oh, axis=0).T
    cols = jnp.concatenate(cols_oh, axis=0)
    tok = (lax.broadcasted_iota(I32, (1, 2 * TM), 1) % TM).astype(F32)
    dot = functools.partial(jnp.dot, preferred_element_type=F32)
    hits = dot(rows_t.astype(BF16), cols)
    tok_sum = dot((rows_t * tok).astype(BF16), cols)
    tile_in_pop = jnp.where(i < NT_CTX, i, i - NT_CTX).astype(F32)
    acc = acc_ref[...] + TOK_ROWS * (tok_sum + TM * tile_in_pop * hits)
    acc_ref[...] = acc

    rec = jnp.where(lane_i == 0, w1, jnp.where(lane_i == 1, w2, 0.0)).T
    rest = jnp.concatenate([rec[0:1, :], rec[1:2, :]], axis=1)
    gacc = gacc_ref[...]
    for _ in range(3):
        piece = rest.astype(BF16)
        gacc = gacc + dot((rows_t * piece.astype(F32)).astype(BF16), cols)
        rest = rest - piece.astype(F32)
    gacc_ref[...] = gacc

    @pl.when(i == NT_CTX - 1)
    def _():
        cnt_ref[0:1, :] = run_new.astype(I32)
        lc_ref[...] = acc.astype(I32)
        gc_ref[...] = gacc

    @pl.when(i == NT - 1)
    def _():
        cnt_ref[1:2, :] = run_new.astype(I32)
        ll_ref[...] = acc.astype(I32)
        gl_ref[...] = gacc


_TAIL_OUT_SHAPES = [jax.ShapeDtypeStruct((T_ALL, D), F32),
                    jax.ShapeDtypeStruct((T_ALL * TOK_ROWS, LANES), F32),
                    jax.ShapeDtypeStruct((SUBLANES, LANES), I32)] + [
                        jax.ShapeDtypeStruct((LIST_ROWS, LIST_COLS), dt) for dt in (I32, F32, I32, F32)]


def _tail_scratch():
    return [pltpu.VMEM((SUBLANES, LANES), F32), pltpu.VMEM((LIST_ROWS, LIST_COLS), F32),
            pltpu.VMEM((LIST_ROWS, LIST_COLS), F32)]


def _tail_out_specs():
    return [_tile_spec(D),
            pl.BlockSpec((TM * TOK_ROWS, LANES), lambda i: (i, 0)),
            _full_spec((SUBLANES, LANES))] + [_full_spec((LIST_ROWS, LIST_COLS)) for _ in range(4)]


def _halo_specs(halo, width):
    per = TM // halo
    last = T_ALL // halo - 1
    prev = pl.BlockSpec((halo, width), lambda i: (jnp.maximum(i * per - 1, 0), 0))
    nxt = pl.BlockSpec((halo, width), lambda i: (jnp.minimum((i + 1) * per, last), 0))
    return prev, nxt


def _mid_even_kernel(l, a_ref, ap_ref, an_ref, ybc_ref, ybl_ref, xc_ref, xl_ref, mod_ref, wp_ref, ps_ref, wo_ref,
                     gffn_ref, wr_ref, *refs):
    outs, (pad_ref,), scratch = refs[:N_TAIL_OUT], refs[N_TAIL_OUT:-N_TAIL_SCRATCH], refs[-N_TAIL_SCRATCH:]
    i = pl.program_id(0)
    r = _mod_row(i)
    off, n, first, last = _seq_info(i)
    a = a_ref[...]
    pad_ref[0:POOL_HALO, :] = jnp.where(first, 0.0, ap_ref[...])
    pad_ref[POOL_HALO:POOL_HALO + TM, :] = a
    pad_ref[POOL_HALO + TM:, :] = jnp.where(last, 0.0, an_ref[...])
    t = off + lax.broadcasted_iota(I32, (TM, 1), 0)
    mixed = []
    for g, w in enumerate(POOL_WINDOWS):
        lo = w // 2
        hi = w - lo - 1
        cols = slice(g * POOL_GROUP_DIM, (g + 1) * POOL_GROUP_DIM)
        total = pad_ref[POOL_HALO - lo:POOL_HALO - lo + TM, cols]
        for j in range(-lo + 1, hi + 1):
            total = total + pad_ref[POOL_HALO + j:POOL_HALO + j + TM, cols]
        count = (jnp.minimum(t + hi, n - 1) - jnp.maximum(t - lo, 0) + 1).astype(F32)
        pooled = total / count - a[:, cols]
        mixed.append(_dot(pooled, wp_ref[g]))
    ya = jnp.concatenate(mixed, axis=1) * ps_ref[...]
    y = _dot(ya, wo_ref[0:POOL_DIM, :]) + _dot(_pick(i, _whole, ybc_ref, ybl_ref), wo_ref[POOL_DIM:, :])
    x = _pick(i, _whole, xc_ref, xl_ref)
    _mixer_tail(l, x, y, mod_ref, r, gffn_ref, wr_ref, outs, scratch)


def _mid_even(l, e, a, yb_c, yb_l, xc, xl, mod, w_pool, pool_scale, w_out, gffn, wr):
    prev, nxt = _halo_specs(POOL_HALO, POOL_DIM)
    return pl.pallas_call(
        functools.partial(_mid_even_kernel, l),
        grid=(NT,),
        in_specs=[_tile_spec(POOL_DIM), prev, nxt] + _pop_specs(TM, Q_DIM) + _pop_specs(TM, D) + [
                  _MOD_SPEC, _layer_spec((len(POOL_WINDOWS), POOL_GROUP_DIM, POOL_GROUP_DIM), e),
                  _layer_spec((1, POOL_DIM), e), _layer_spec((D, D), e), _full_spec((DEPTH, D)),
                  _layer_spec((2, D, LANES), l)],
        out_specs=_tail_out_specs(),
        out_shape=_TAIL_OUT_SHAPES,
        scratch_shapes=[pltpu.VMEM((TM + 2 * POOL_HALO, POOL_DIM), F32)] + _tail_scratch(),
        compiler_params=_params(),
        name="mid_even",
    )(a, a, a, yb_c, yb_l, xc, xl, mod, w_pool, pool_scale, w_out, gffn, wr)


def _in_odd_kernel(l, x1_ref, moec_ref, moel_ref, mod_ref, g_ref, w1_ref, b1_ref, x2_ref, u_ref):
    i = pl.program_id(0)
    r = _mod_row(i)
    g2 = mod_ref[l - 1, pl.ds(r, 1), 5 * D:6 * D]
    x2 = x1_ref[...] + g2 * _pick(i, _from_token_tiles, moec_ref, moel_ref)
    x2_ref[...] = x2
    shift = mod_ref[l, pl.ds(r, 1), 0:D]
    scale = mod_ref[l, pl.ds(r, 1), D:2 * D]
    h = _rms_mod(x2, g_ref[l:l + 1, :], scale, shift)
    u = _dot(h, w1_ref[...]) + b1_ref[...]
    u_ref[...] = u[:, :D] * (1.0 / (1.0 + jnp.exp(-u[:, D:])))


def _in_odd(l, o, x1, moe_c, moe_l, mod, g, w1, b1):
    return pl.pallas_call(
        functools.partial(_in_odd_kernel, l),
        grid=(NT,),
        in_specs=[_tile_spec(D)] + _pop_specs(TM * TOK_ROWS, LANES) + [
                  _MOD_SPEC, _full_spec((DEPTH, D)), _layer_spec((D, 2 * D), o), _layer_spec((1, 2 * D), o)],
        out_specs=[_tile_spec(D), _tile_spec(D)],
        out_shape=[jax.ShapeDtypeStruct((T_ALL, D), F32), jax.ShapeDtypeStruct((T_ALL, D), F32)],
        compiler_params=_params(),
        name="in_odd",
    )(x1, moe_c, moe_l, mod, g, w1, b1)


CONV_PAD_ROWS = TM + 2 * CONV_HALO
CONV_CHUNKS = D // LANES
CONV_BLOCK = 64


def _mid_odd_kernel(l, u_ref, up_ref, un_ref, x_ref, mod_ref, dw_ref, dwb_ref, lng_ref, lnb_ref, w2_ref, b2_ref,
                    gffn_ref, wr_ref, *refs):
    outs, scratch = refs[:N_TAIL_OUT], refs[-N_TAIL_SCRATCH:]
    pad_ref, conv_ref = refs[N_TAIL_OUT:-N_TAIL_SCRATCH]
    i = pl.program_id(0)
    r = _mod_row(i)
    _, _, first, last = _seq_info(i)
    u = u_ref[...]
    up = jnp.where(first, 0.0, up_ref[...])
    un = jnp.where(last, 0.0, un_ref[...])
    half = CONV_WIDTH // 2
    first_row = CONV_HALO - half
    for c in range(CONV_CHUNKS):
        cols = slice(c * LANES, (c + 1) * LANES)
        pad_ref[c, 0:CONV_HALO, :] = up[:, cols]
        pad_ref[c, CONV_HALO:CONV_HALO + TM, :] = u[:, cols]
        pad_ref[c, CONV_HALO + TM:, :] = un[:, cols]
        bias = jnp.broadcast_to(dwb_ref[:, cols], (CONV_BLOCK, LANES))
        for r0 in range(0, TM, 2 * CONV_BLOCK):
            acc_even, acc_odd = bias, bias
            for s in range(first_row, first_row + CONV_WIDTH + 1):
                win = pad_ref[c, pl.ds(r0 + s, CONV_BLOCK, stride=2), :]
                j = s - first_row
                if j < CONV_WIDTH:
                    acc_even = acc_even + win * dw_ref[j:j + 1, cols]
                if j >= 1:
                    acc_odd = acc_odd + win * dw_ref[j - 1:j, cols]
            conv_ref[c, pl.ds(r0, CONV_BLOCK, stride=2), :] = acc_even
            conv_ref[c, pl.ds(r0 + 1, CONV_BLOCK, stride=2), :] = acc_odd
    acc = jnp.concatenate([conv_ref[c] for c in range(CONV_CHUNKS)], axis=1)
    mu = jnp.mean(acc, axis=-1, keepdims=True)
    cen = acc - mu
    var = jnp.mean(cen * cen, axis=-1, keepdims=True)
    v = _silu(cen * lax.rsqrt(var + EPS) * lng_ref[...] + lnb_ref[...])
    y = _dot(v, w2_ref[...]) + b2_ref[...]
    _mixer_tail(l, x_ref[...], y, mod_ref, r, gffn_ref, wr_ref, outs, scratch)


def _mid_odd(l, o, u, x, mod, dw, dwb, lng, lnb, w2, b2, gffn, wr):
    prev, nxt = _halo_specs(CONV_HALO, D)
    return pl.pallas_call(
        functools.partial(_mid_odd_kernel, l),
        grid=(NT,),
        in_specs=[_tile_spec(D), prev, nxt, _tile_spec(D), _MOD_SPEC, _layer_spec((CONV_WIDTH, D), o),
                  _layer_spec((1, D), o), _layer_spec((1, D), o), _layer_spec((1, D), o),
                  _layer_spec((D, D), o), _layer_spec((1, D), o), _full_spec((DEPTH, D)),
                  _layer_spec((2, D, LANES), l)],
        out_specs=_tail_out_specs(),
        out_shape=_TAIL_OUT_SHAPES,
        scratch_shapes=[pltpu.VMEM((CONV_CHUNKS, CONV_PAD_ROWS, LANES), F32),
                        pltpu.VMEM((CONV_CHUNKS, TM, LANES), F32)] + _tail_scratch(),
        compiler_params=_params(),
        name="mid_odd",
    )(u, u, u, x, mod, dw, dwb, lng, lnb, w2, b2, gffn, wr)


def _tile_tables_kernel(cnt_ref, te_c, tc_c, nv_c, te_l, tc_l, nv_l, nact_ref, first_ref):
    for p, (te_ref, tc_ref, nv_ref) in enumerate(((te_c, tc_c, nv_c), (te_l, tc_l, nv_l))):
        run = jnp.int32(0)
        ends = []
        for e in range(N_EXPERTS):
            first_ref[p * N_EXPERTS + e] = run
            run = run + (cnt_ref[p, e] + MOE_TM - 1) // MOE_TM
            ends.append(run)
        nact_ref[p] = run

        def expert_of(tile, ends=ends):
            e = jnp.int32(0)
            for k in range(N_EXPERTS):
                e = e + jnp.where(tile >= ends[k], 1, 0)
            return e

        last_e = expert_of(run - 1)

        def tile_body(j, c, te_ref=te_ref, tc_ref=tc_ref, nv_ref=nv_ref, p=p, expert_of=expert_of, last_e=last_e):
            e = jnp.minimum(expert_of(j), last_e)
            chunk = j - first_ref[p * N_EXPERTS + e]
            te_ref[j] = e
            tc_ref[j] = chunk
            nv_ref[j] = jnp.clip(cnt_ref[p, e] - chunk * MOE_TM, 0, MOE_TM)
            return c

        lax.fori_loop(0, te_ref.shape[0], tile_body, 0)


def _tile_tables(cnt):
    smem = pl.BlockSpec(memory_space=pltpu.SMEM)
    shapes = [(MOE_TILES_CTX,)] * 3 + [(MOE_TILES_LAT,)] * 3 + [(2,)]
    return pl.pallas_call(
        _tile_tables_kernel,
        in_specs=[smem],
        out_specs=[smem] * len(shapes),
        out_shape=[jax.ShapeDtypeStruct(s, I32) for s in shapes],
        scratch_shapes=[pltpu.SMEM((2 * N_EXPERTS,), I32)],
        name="tile_tables",
    )(cnt)


def _moe_kernel(pop, te_ref, tc_ref, nval_ref, nact_ref, list_ref, gate_ref, h_ref, w1_ref, w3_ref, w2_ref,
                o_ref, xs_ref, ys_ref):
    j = pl.program_id(0)

    @pl.when(j == 0)
    def _():
        o_ref[...] = jnp.zeros(o_ref.shape, F32)
        xs_ref[...] = jnp.zeros(xs_ref.shape, F32)

    @pl.when(j < nact_ref[pop])
    def _():
        nval = nval_ref[j]
        nfull = nval >> MOVE_BATCH_LOG2
        list_base = tc_ref[j] * (MOE_TM // LIST_SPAN) * LIST_COLS + te_ref[j] * LIST_SPAN

        def entry(r):
            return list_base + (r >> LIST_SPAN_LOG2) * LIST_COLS + (r & (LIST_SPAN - 1))

        def tok_rows(i):
            return pl.ds(pl.multiple_of(list_ref[i], TOK_ROWS), TOK_ROWS)

        def rows(r):
            return pl.ds(pl.multiple_of(r * TOK_ROWS, TOK_ROWS), TOK_ROWS)

        def gather_batch(b, c):
            r0 = b * MOVE_BATCH
            i0 = entry(r0)
            for k in range(MOVE_BATCH):
                xs_ref[rows(r0 + k), :] = h_ref[tok_rows(i0 + k), :]
            return c

        def gather_one(r, c):
            xs_ref[rows(r), :] = h_ref[tok_rows(entry(r)), :]
            return c

        lax.fori_loop(0, nfull, gather_batch, 0)
        lax.fori_loop(nfull * MOVE_BATCH, nval, gather_one, 0)

        x = jnp.concatenate([xs_ref[pl.ds(s, MOE_TM, stride=TOK_ROWS), :] for s in range(TOK_ROWS)], axis=1)
        x = x.astype(BF16)
        hg = jnp.dot(x, w1_ref[...].astype(BF16), preferred_element_type=F32)
        hu = jnp.dot(x, w3_ref[...].astype(BF16), preferred_element_type=F32)
        y = _dot(_silu(hg) * hu, w2_ref[...])
        for s in range(TOK_ROWS):
            ys_ref[pl.ds(s, MOE_TM, stride=TOK_ROWS), :] = y[:, s * LANES:(s + 1) * LANES]

        def scatter_batch(b, c):
            r0 = b * MOVE_BATCH
            i0 = entry(r0)
            dst = [tok_rows(i0 + k) for k in range(MOVE_BATCH)]
            vals = [o_ref[dst[k], :] + gate_ref[i0 + k] * ys_ref[rows(r0 + k), :] for k in range(MOVE_BATCH)]
            for k in range(MOVE_BATCH):
                o_ref[dst[k], :] = vals[k]
            return c

        def scatter_one(r, c):
            i = entry(r)
            o_ref[tok_rows(i), :] = o_ref[tok_rows(i), :] + gate_ref[i] * ys_ref[rows(r), :]
            return c

        lax.fori_loop(0, nfull, scatter_batch, 0)
        lax.fori_loop(nfull * MOVE_BATCH, nval, scatter_one, 0)


def _moe(l, pop, h_tt, te, tc, nval, nact, rows, gate, w1, w3, w2):
    tp = (T_CTX, T_LAT)[pop]
    blk = (0, T_CTX // T_LAT)[pop]
    wmap = lambda j, te, *_: (l, te[j], 0, 0)
    grid_spec = pltpu.PrefetchScalarGridSpec(
        num_scalar_prefetch=6,
        grid=(te.shape[0],),
        in_specs=[
            pl.BlockSpec((tp * TOK_ROWS, LANES), lambda j, *_: (blk, 0), pipeline_mode=pl.Buffered(1)),
            pl.BlockSpec((None, None, D, EXPERT_HIDDEN), wmap),
            pl.BlockSpec((None, None, D, EXPERT_HIDDEN), wmap),
            pl.BlockSpec((None, None, EXPERT_HIDDEN, D), wmap),
        ],
        out_specs=pl.BlockSpec((tp * TOK_ROWS, LANES), lambda j, *_: (0, 0), pipeline_mode=pl.Buffered(1)),
        scratch_shapes=[pltpu.VMEM((MOE_TM * TOK_ROWS, LANES), F32),
                        pltpu.VMEM((MOE_TM * TOK_ROWS, LANES), F32)],
    )
    return pl.pallas_call(
        functools.partial(_moe_kernel, pop),
        grid_spec=grid_spec,
        out_shape=jax.ShapeDtypeStruct((tp * TOK_ROWS, LANES), F32),
        compiler_params=_params(),
        name="moe",
    )(te, tc, nval, nact, rows, gate, h_tt, w1, w3, w2)


def _moe_all(l, h_tt, cnt, rows_c, gate_c, rows_l, gate_l, w1, w3, w2):
    te_c, tc_c, nv_c, te_l, tc_l, nv_l, nact = _tile_tables(cnt)
    out_c = _moe(l, 0, h_tt, te_c, tc_c, nv_c, nact, rows_c.reshape(-1), gate_c.reshape(-1), w1, w3, w2)
    out_l = _moe(l, 1, h_tt, te_l, tc_l, nv_l, nact, rows_l.reshape(-1), gate_l.reshape(-1), w1, w3, w2)
    return out_c, out_l


def _final_kernel(l, x1_ref, moec_ref, moel_ref, mod_ref, g_ref, oc_ref, ol_ref):
    i = pl.program_id(0)
    r = _mod_row(i)
    g2 = mod_ref[l, pl.ds(r, 1), 5 * D:6 * D]
    x = x1_ref[...] + g2 * _pick(i, _from_token_tiles, moec_ref, moel_ref)
    ms = jnp.mean(x * x, axis=-1, keepdims=True)
    y = x * lax.rsqrt(ms + EPS) * g_ref[...]

    @pl.when(i < NT_CTX)
    def _():
        oc_ref[...] = y

    @pl.when(i >= NT_CTX)
    def _():
        ol_ref[...] = y


def _final(l, x1, moe_c, moe_l, mod, g):
    return pl.pallas_call(
        functools.partial(_final_kernel, l),
        grid=(NT,),
        in_specs=[_tile_spec(D)] + _pop_specs(TM * TOK_ROWS, LANES) + [_MOD_SPEC, _full_spec((1, D))],
        out_specs=_pop_specs(TM, D),
        out_shape=[jax.ShapeDtypeStruct((T_CTX, D), F32), jax.ShapeDtypeStruct((T_LAT, D), F32)],
        compiler_params=_params(),
        name="final_norm",
    )(x1, moe_c, moe_l, mod, g)


def _rope_tables():
    t = jnp.arange(LAT_LEN)
    row = (t // GRID_W).astype(F32)
    col = (t % GRID_W).astype(F32)
    inv = ROPE_BASE ** (-jnp.arange(0, ROPE_AXIS_DIM, 2, dtype=F32) / ROPE_AXIS_DIM)

    def table(p):
        ang = p[:, None] * inv[None, :]
        ang = jnp.concatenate([ang, ang], axis=-1)
        return jnp.cos(ang), jnp.sin(ang)

    (cr, sr), (cc, sc) = table(row), table(col)
    cos = jnp.concatenate([cr, cc, cr, cc], axis=-1)
    sin = jnp.concatenate([sr, sc, sr, sc], axis=-1)
    return cos, sin


def _router_weights(w_grp, w_exp):
    we = jnp.transpose(w_exp, (0, 2, 1, 3)).reshape(DEPTH, D, N_GROUPS * EPG)
    pad = jnp.zeros((DEPTH, D, LANES - N_GROUPS - N_GROUPS * EPG), F32)
    wr = jnp.concatenate([w_grp, we, pad], axis=-1)
    hi = lax.bitcast_convert_type(lax.bitcast_convert_type(wr, jnp.uint32) & jnp.uint32(0xFFFF0000), F32)
    return jnp.stack([hi.astype(BF16), (wr - hi).astype(BF16)], axis=1)


def kernel(x_prompt, x_sample, cache_k, cache_v, c, c_ctx, w_ada, b_ada, norm_mix_g, norm_ffn_g, w_in_ab, pool_w, pool_scale, attn_sink, w_out_ab, conv_w1, conv_b1, conv_dw, conv_dw_b, conv_ln_g, conv_ln_b, conv_w2, conv_b2, router_grp, router_exp, moe_w1, moe_w3, moe_w2, final_g):
    xc = x_prompt.reshape(T_CTX, D)
    xl = x_sample.reshape(T_LAT, D)
    cond8 = jnp.concatenate([c_ctx[None, :], c, jnp.zeros((SUBLANES - 1 - N_LAT_SEQ, D), F32)], axis=0)
    mod = _modulation(cond8, w_ada, b_ada)
    cos_t, sin_t = _rope_tables()
    wr = _router_weights(router_grp, router_exp)
    n_even, n_odd = w_in_ab.shape[0], conv_w1.shape[0]
    past = cache_k.shape[2]

    a, q, k, v = _in_even(0, 0, xc, xl, mod, norm_mix_g, w_in_ab, cos_t, sin_t)
    yb_c = _attn_ctx(0, attn_sink, q, k, v)
    yb_l = _attn_lat(0, attn_sink, q, k, v, cache_k[:, 0].reshape(N_LAT_SEQ, past, KV_DIM),
                     cache_v[:, 0].reshape(N_LAT_SEQ, past, KV_DIM))
    x1, h2, *routing = _mid_even(0, 0, a, yb_c, yb_l, xc, xl, mod, pool_w,
                                 pool_scale.reshape(n_even, 1, POOL_DIM), w_out_ab, norm_ffn_g, wr)
    moe0 = _moe_all(0, h2, *routing, moe_w1, moe_w3, moe_w2)

    vec = lambda p: p.reshape(n_odd, 1, -1)
    x2, u = _in_odd(1, 0, x1, *moe0, mod, norm_mix_g, conv_w1, vec(conv_b1))
    x3, h2b, *routing = _mid_odd(1, 0, u, x2, mod, conv_dw, vec(conv_dw_b), vec(conv_ln_g),
                                 vec(conv_ln_b), conv_w2, vec(conv_b2), norm_ffn_g, wr)
    moe1 = _moe_all(1, h2b, *routing, moe_w1, moe_w3, moe_w2)
    y_c, y_l = _final(1, x3, *moe1, mod, final_g.reshape(1, D))

    y_prompt = y_c.reshape(N_CTX_SEQ, CTX_LEN, D)
    y_sample = y_l.reshape(N_LAT_SEQ, LAT_LEN, D)
    state_k = k[:T_CTX].reshape(N_CTX_SEQ, 1, CTX_LEN, N_KV_HEADS, HEAD_DIM)
    state_v = v[:T_CTX].reshape(N_CTX_SEQ, 1, CTX_LEN, N_KV_HEADS, HEAD_DIM)
    return (y_prompt, y_sample, state_k, state_v)
```

```python
import functools

import jax
import jax.numpy as jnp
from jax import lax
from jax.experimental import pallas as pl
from jax.experimental.pallas import tpu as pltpu

F32 = jnp.float32
BF16 = jnp.bfloat16
I32 = jnp.int32

D = 1024
N_CTX_SEQ = 16
CTX_LEN = 256
N_LAT_SEQ = 2
LAT_LEN = 1024
T_CTX = N_CTX_SEQ * CTX_LEN
T_LAT = N_LAT_SEQ * LAT_LEN
T_ALL = T_CTX + T_LAT
TM = 256
NT = T_ALL // TM
NT_CTX = T_CTX // TM
LAT_TILES_PER_SEQ = LAT_LEN // TM
GRID_W = 64
DEPTH = 2

POOL_WINDOWS = (2, 4, 8, 16)
POOL_GROUP_DIM = 128
POOL_DIM = 512
HEAD_DIM = 64
N_Q_HEADS = 8
N_KV_HEADS = 2
Q_PER_KV = 4
Q_DIM = 512
KV_DIM = 128
IN_AB = 1280
ATTN_WINDOW = 128
ATTN_BLOCK = 128
ATTN_SCALE = HEAD_DIM ** -0.5
ROPE_BASE = 10000.0
ROPE_AXIS_DIM = 32
CONV_WIDTH = 31
CONV_HALO = 16
POOL_HALO = 8
N_GROUPS = 4
EPG = 4
N_EXPERTS = 16
EXPERT_HIDDEN = 512
EPS = 1e-6
NEG_BIG = -1e30

SUBLANES = 8
LANES = 128
TOK_ROWS = D // LANES

MOE_TM = 256
MOE_TILES_CTX = 2 * T_CTX // MOE_TM + N_EXPERTS
MOE_TILES_LAT = 2 * T_LAT // MOE_TM + N_EXPERTS
MOVE_BATCH_LOG2 = 4
MOVE_BATCH = 1 << MOVE_BATCH_LOG2
VMEM_LIMIT = 56 * 1024 * 1024


def _silu(x):
    return x * (1.0 / (1.0 + jnp.exp(-x)))


def _mod_row(i):
    return jnp.where(i < NT_CTX, 0, 1 + (i - NT_CTX) // LAT_TILES_PER_SEQ)


def _seq_info(i):
    is_ctx = i < NT_CTX
    k = (i - NT_CTX) % LAT_TILES_PER_SEQ
    off = jnp.where(is_ctx, 0, k * TM)
    n = jnp.where(is_ctx, CTX_LEN, LAT_LEN)
    first = jnp.logical_or(is_ctx, k == 0)
    last = jnp.logical_or(is_ctx, k == LAT_TILES_PER_SEQ - 1)
    return off, n, first, last


def _rms_mod(x, g, scale, shift):
    ms = jnp.mean(x * x, axis=-1, keepdims=True)
    return (x * lax.rsqrt(ms + EPS) * g) * (1.0 + scale) + shift


def _dot(a, b):
    return jnp.dot(a.astype(BF16), b.astype(BF16), preferred_element_type=F32)


def _dot_nt(a, b):
    return lax.dot_general(a.astype(BF16), b.astype(BF16), (((1,), (1,)), ((), ())),
                           preferred_element_type=F32)


def _from_token_tiles(ref):
    return jnp.concatenate([ref[pl.ds(s, TM, stride=TOK_ROWS), :] for s in range(TOK_ROWS)], axis=1)


def _tile_spec(width):
    return pl.BlockSpec((TM, width), lambda i: (i, 0))


def _full_spec(shape):
    return pl.BlockSpec(shape, lambda i: (0,) * len(shape))


def _layer_spec(shape, l):
    return pl.BlockSpec((None,) + tuple(shape), lambda i: (l,) + (0,) * len(shape))


def _pop_specs(rows, width):
    return [pl.BlockSpec((rows, width), lambda i: (jnp.minimum(i, NT_CTX - 1), 0)),
            pl.BlockSpec((rows, width), lambda i: (jnp.maximum(i - NT_CTX, 0), 0))]


def _per_population(i, body):
    pl.when(i < NT_CTX)(functools.partial(body, 0))
    pl.when(i >= NT_CTX)(functools.partial(body, 1))
_MOD_SPEC = pl.BlockSpec((DEPTH, SUBLANES, 6 * D), lambda i: (0, 0, 0))


def _params():
    return pltpu.CompilerParams(vmem_limit_bytes=VMEM_LIMIT)


MOD_TN = 1536


def _mod_kernel(cond_ref, w_ref, b_ref, o_ref):
    s = _silu(cond_ref[...])
    o_ref[0] = _dot(s, w_ref[0]) + b_ref[0]


def _modulation(cond8, w_ada, b_ada):
    return pl.pallas_call(
        _mod_kernel,
        grid=(DEPTH, 6 * D // MOD_TN),
        in_specs=[
            pl.BlockSpec((SUBLANES, D), lambda l, n: (0, 0)),
            pl.BlockSpec((1, D, MOD_TN), lambda l, n: (l, 0, n)),
            pl.BlockSpec((1, 1, MOD_TN), lambda l, n: (l, 0, n)),
        ],
        out_specs=pl.BlockSpec((1, SUBLANES, MOD_TN), lambda l, n: (l, 0, n)),
        out_shape=jax.ShapeDtypeStruct((DEPTH, SUBLANES, 6 * D), F32),
        compiler_params=_params(),
        name="modulation",
    )(cond8, w_ada, b_ada.reshape(DEPTH, 1, 6 * D))


def _rope_chunk(xc, cos, sin):
    lane = lax.broadcasted_iota(I32, xc.shape, 1)
    first = (lane % ROPE_AXIS_DIM) < (ROPE_AXIS_DIM // 2)
    rot = jnp.where(first, -pltpu.roll(xc, LANES - ROPE_AXIS_DIM // 2, 1), pltpu.roll(xc, ROPE_AXIS_DIM // 2, 1))
    return xc * cos + rot * sin


def _in_even_kernel(l, xc_ref, xl_ref, mod_ref, g_ref, w_ref, cos_ref, sin_ref, a_ref, q_ref, k_ref, v_ref):
    i = pl.program_id(0)
    r = _mod_row(i)

    def body(pop):
        x = (xc_ref, xl_ref)[pop][...]
        shift = mod_ref[l, pl.ds(r, 1), 0:D]
        scale = mod_ref[l, pl.ds(r, 1), D:2 * D]
        h = _rms_mod(x, g_ref[l:l + 1, :], scale, shift)
        proj = _dot(h, w_ref[...])
        a_ref[...] = proj[:, :POOL_DIM]
        v_ref[...] = proj[:, POOL_DIM + Q_DIM + KV_DIM:]
        if pop == 0:
            q_ref[...] = proj[:, POOL_DIM:POOL_DIM + Q_DIM]
            k_ref[...] = proj[:, POOL_DIM + Q_DIM:POOL_DIM + Q_DIM + KV_DIM]
        else:
            off, _, _, _ = _seq_info(i)
            off = pl.multiple_of(off, TM)
            cos = cos_ref[pl.ds(off, TM), :]
            sin = sin_ref[pl.ds(off, TM), :]
            for c in range(Q_DIM // LANES):
                lo = POOL_DIM + c * LANES
                q_ref[:, c * LANES:(c + 1) * LANES] = _rope_chunk(proj[:, lo:lo + LANES], cos, sin)
            lo = POOL_DIM + Q_DIM
            k_ref[...] = _rope_chunk(proj[:, lo:lo + LANES], cos, sin)

    _per_population(i, body)


def _in_even(l, e, xc, xl, mod, g, w_in, cos_t, sin_t):
    return pl.pallas_call(
        functools.partial(_in_even_kernel, l),
        grid=(NT,),
        in_specs=_pop_specs(TM, D) + [_MOD_SPEC, _full_spec((DEPTH, D)), _layer_spec((D, IN_AB), e),
                  _full_spec((LAT_LEN, LANES)), _full_spec((LAT_LEN, LANES))],
        out_specs=[_tile_spec(POOL_DIM), _tile_spec(Q_DIM), _tile_spec(KV_DIM), _tile_spec(KV_DIM)],
        out_shape=[jax.ShapeDtypeStruct((T_ALL, POOL_DIM), F32), jax.ShapeDtypeStruct((T_ALL, Q_DIM), F32),
                   jax.ShapeDtypeStruct((T_ALL, KV_DIM), F32), jax.ShapeDtypeStruct((T_ALL, KV_DIM), F32)],
        compiler_params=_params(),
        name="in_even",
    )(xc, xl, mod, g, w_in, cos_t, sin_t)


def _attn_ctx_kernel(e, sink_ref, q_ref, k_ref, v_ref, o_ref):
    q = q_ref[...]
    k = k_ref[...]
    v = v_ref[...]
    for h in range(N_KV_HEADS):
        kh = k[:, h * HEAD_DIM:(h + 1) * HEAD_DIM]
        vh = v[:, h * HEAD_DIM:(h + 1) * HEAD_DIM]
        for g in range(Q_PER_KV):
            j = h * Q_PER_KV + g
            s = _dot_nt(q[:, j * HEAD_DIM:(j + 1) * HEAD_DIM], kh) * ATTN_SCALE
            sk = sink_ref[e, j]
            m = jnp.maximum(jnp.max(s, axis=-1, keepdims=True), sk)
            p = jnp.exp(s - m)
            denom = jnp.sum(p, axis=-1, keepdims=True) + jnp.exp(sk - m)
            o_ref[:, j * HEAD_DIM:(j + 1) * HEAD_DIM] = _dot(p, vh) / denom


def _attn_ctx(e, sink, q, k, v):
    tile = lambda w: pl.BlockSpec((TM, w), lambda b: (b, 0))
    return pl.pallas_call(
        functools.partial(_attn_ctx_kernel, e),
        grid=(N_CTX_SEQ,),
        in_specs=[pl.BlockSpec(memory_space=pltpu.SMEM), tile(Q_DIM), tile(KV_DIM), tile(KV_DIM)],
        out_specs=tile(Q_DIM),
        out_shape=jax.ShapeDtypeStruct((T_CTX, Q_DIM), F32),
        compiler_params=_params(),
        name="attn_ctx",
    )(sink, q, k, v)


LAT_BLOCKS = LAT_LEN // ATTN_BLOCK
BAND = 3 * ATTN_BLOCK


def _attn_lat_kernel(e, sink_ref, q_ref, k_ref, v_ref, ck_ref, cv_ref, o_ref):
    n = pl.program_id(1)
    start = jnp.clip((n - 1) * ATTN_BLOCK, 0, LAT_LEN - BAND)
    start = pl.multiple_of(start, ATTN_BLOCK)
    q = q_ref[...]
    kw = k_ref[pl.ds(start, BAND), :]
    vw = v_ref[pl.ds(start, BAND), :]
    ck = ck_ref[0]
    cv = cv_ref[0]
    qpos = n * ATTN_BLOCK + lax.broadcasted_iota(I32, (ATTN_BLOCK, BAND), 0)
    kpos = start + lax.broadcasted_iota(I32, (ATTN_BLOCK, BAND), 1)
    valid = jnp.abs(qpos - kpos) <= ATTN_WINDOW
    for h in range(N_KV_HEADS):
        hs = slice(h * HEAD_DIM, (h + 1) * HEAD_DIM)
        for g in range(Q_PER_KV):
            j = h * Q_PER_KV + g
            qj = q[:, j * HEAD_DIM:(j + 1) * HEAD_DIM]
            s_loc = jnp.where(valid, _dot_nt(qj, kw[:, hs]) * ATTN_SCALE, NEG_BIG)
            s_ctx = _dot_nt(qj, ck[:, hs]) * ATTN_SCALE
            sk = sink_ref[e, j]
            m = jnp.maximum(jnp.maximum(jnp.max(s_loc, axis=-1, keepdims=True),
                                        jnp.max(s_ctx, axis=-1, keepdims=True)), sk)
            p_loc = jnp.exp(s_loc - m)
            p_ctx = jnp.exp(s_ctx - m)
            denom = (jnp.sum(p_loc, axis=-1, keepdims=True) + jnp.sum(p_ctx, axis=-1, keepdims=True)
                     + jnp.exp(sk - m))
            o = _dot(p_loc, vw[:, hs]) + _dot(p_ctx, cv[:, hs])
            o_ref[:, j * HEAD_DIM:(j + 1) * HEAD_DIM] = o / denom


def _attn_lat(e, sink, q, k, v, ck, cv):
    past = ck.shape[1]
    q_blk0 = T_CTX // ATTN_BLOCK
    kv_blk0 = T_CTX // LAT_LEN
    q_spec = pl.BlockSpec((ATTN_BLOCK, Q_DIM), lambda b, n: (q_blk0 + b * LAT_BLOCKS + n, 0))
    return pl.pallas_call(
        functools.partial(_attn_lat_kernel, e),
        grid=(N_LAT_SEQ, LAT_BLOCKS),
        in_specs=[pl.BlockSpec(memory_space=pltpu.SMEM),
                  q_spec,
                  pl.BlockSpec((LAT_LEN, KV_DIM), lambda b, n: (kv_blk0 + b, 0)),
                  pl.BlockSpec((LAT_LEN, KV_DIM), lambda b, n: (kv_blk0 + b, 0)),
                  pl.BlockSpec((1, past, KV_DIM), lambda b, n: (b, 0, 0)),
                  pl.BlockSpec((1, past, KV_DIM), lambda b, n: (b, 0, 0))],
        out_specs=pl.BlockSpec((ATTN_BLOCK, Q_DIM), lambda b, n: (b * LAT_BLOCKS + n, 0)),
        out_shape=jax.ShapeDtypeStruct((T_LAT, Q_DIM), F32),
        compiler_params=_params(),
        name="attn_lat",
    )(sink, q, k, v, ck, cv)


def _router(h2, wr_ref):
    hi = h2.astype(BF16)
    lo = (h2 - hi.astype(F32)).astype(BF16)
    whi = wr_ref[0]
    wlo = wr_ref[1]
    dot = functools.partial(jnp.dot, preferred_element_type=F32)
    logits = dot(hi, whi) + (dot(lo, whi) + dot(hi, wlo))
    lane = lax.broadcasted_iota(I32, logits.shape, 1).astype(F32)
    neg = jnp.float32(-jnp.inf)
    big = jnp.float32(1e9)
    is_grp = lane < N_GROUPS
    gl = jnp.where(is_grp, logits, neg)
    gmax = jnp.max(gl, axis=-1, keepdims=True)
    gsum = jnp.sum(jnp.where(is_grp, jnp.exp(logits - gmax), 0.0), axis=-1, keepdims=True)
    g_w = 1.0 / gsum
    g_idx = jnp.min(jnp.where(gl == gmax, lane, big), axis=-1, keepdims=True)
    base = N_GROUPS + EPG * g_idx
    in_grp = jnp.logical_and(lane >= base, lane < base + EPG)
    el = jnp.where(in_grp, logits, neg)
    t1 = jnp.max(el, axis=-1, keepdims=True)
    i1 = jnp.min(jnp.where(el == t1, lane, big), axis=-1, keepdims=True)
    el2 = jnp.where(lane == i1, neg, el)
    t2 = jnp.max(el2, axis=-1, keepdims=True)
    i2 = jnp.min(jnp.where(el2 == t2, lane, big), axis=-1, keepdims=True)
    d = jnp.exp(t2 - t1)
    w1 = g_w / (1.0 + d)
    w2 = g_w * d / (1.0 + d)
    return i1 - N_GROUPS, i2 - N_GROUPS, w1, w2


LIST_SPAN_LOG2 = 5
LIST_SPAN = 1 << LIST_SPAN_LOG2
LIST_ROWS = T_CTX // LIST_SPAN
LIST_COLS = N_EXPERTS * LIST_SPAN


N_TAIL_OUT = 7
N_TAIL_SCRATCH = 3


def _mixer_tail(l, x, y, mod_ref, r, gffn_ref, wr_ref, outs, scratch):
    x1_ref, h2_ref, cnt_ref, lc_ref, gc_ref, ll_ref, gl_ref = outs
    run_ref, acc_ref, gacc_ref = scratch
    i = pl.program_id(0)
    g1 = mod_ref[l, pl.ds(r, 1), 2 * D:3 * D]
    shift2 = mod_ref[l, pl.ds(r, 1), 3 * D:4 * D]
    scale2 = mod_ref[l, pl.ds(r, 1), 4 * D:5 * D]
    x1 = x + g1 * y
    x1_ref[...] = x1
    h2 = _rms_mod(x1, gffn_ref[l:l + 1, :], scale2, shift2)
    for s in range(TOK_ROWS):
        h2_ref[pl.ds(s, TM, stride=TOK_ROWS), :] = h2[:, s * LANES:(s + 1) * LANES]
    e1, e2, w1, w2 = _router(h2, wr_ref)

    @pl.when(jnp.logical_or(i == 0, i == NT_CTX))
    def _():
        run_ref[...] = jnp.zeros(run_ref.shape, F32)
        acc_ref[...] = jnp.zeros(acc_ref.shape, F32)
        gacc_ref[...] = jnp.zeros(gacc_ref.shape, F32)

    @pl.when(i == 0)
    def _():
        cnt_ref[...] = jnp.zeros(cnt_ref.shape, I32)

    lane_i = lax.broadcasted_iota(I32, (TM, LANES), 1)
    lane = lane_i.astype(F32)
    member = jnp.where(jnp.logical_or(lane == e1, lane == e2), 1.0, 0.0)
    tri = jnp.where(lax.broadcasted_iota(I32, (TM, TM), 0) >= lax.broadcasted_iota(I32, (TM, TM), 1), 1.0, 0.0)
    csum = jnp.dot(tri.astype(BF16), member.astype(BF16), preferred_element_type=F32)
    run = run_ref[0:1, :]
    before = csum - member + run
    r1 = jnp.sum(jnp.where(lane == e1, before, 0.0), axis=-1, keepdims=True)
    r2 = jnp.sum(jnp.where(lane == e2, before, 0.0), axis=-1, keepdims=True)
    run_new = run + csum[TM - 1:TM, :]
    run_ref[0:1, :] = run_new

    col_i = lax.broadcasted_iota(I32, (TM, LIST_COLS), 1)
    rows_oh, cols_oh = [], []
    for e, rank in ((e1, r1), (e2, r2)):
        rank_i = rank.astype(I32)
        rows_oh.append(jnp.where(lane_i == rank_i // LIST_SPAN, 1.0, 0.0))
        cols_oh.append(jnp.where(col_i == e.astype(I32) * LIST_SPAN + rank_i % LIST_SPAN, 1.0, 0.0).astype(BF16))
    rows_t = jnp.concatenate(rows_oh, axis=0).T
    cols = jnp.concatenate(cols_oh, axis=0)
    tok = (lax.broadcasted_iota(I32, (1, 2 * TM), 1) % TM).astype(F32)
    dot = functools.partial(jnp.dot, preferred_element_type=F32)
    hits = dot(rows_t.astype(BF16), cols)
    tok_sum = dot((rows_t * tok).astype(BF16), cols)
    tile_in_pop = jnp.where(i < NT_CTX, i, i - NT_CTX).astype(F32)
    acc = acc_ref[...] + TOK_ROWS * (tok_sum + TM * tile_in_pop * hits)
    acc_ref[...] = acc

    rec = jnp.where(lane_i == 0, w1, jnp.where(lane_i == 1, w2, 0.0)).T
    rest = jnp.concatenate([rec[0:1, :], rec[1:2, :]], axis=1)
    gacc = gacc_ref[...]
    for _ in range(3):
        piece = rest.astype(BF16)
        gacc = gacc + dot((rows_t * piece.astype(F32)).astype(BF16), cols)
        rest = rest - piece.astype(F32)
    gacc_ref[...] = gacc

    @pl.when(i == NT_CTX - 1)
    def _():
        cnt_ref[0:1, :] = run_new.astype(I32)
        lc_ref[...] = acc.astype(I32)
        gc_ref[...] = gacc

    @pl.when(i == NT - 1)
    def _():
        cnt_ref[1:2, :] = run_new.astype(I32)
        ll_ref[...] = acc.astype(I32)
        gl_ref[...] = gacc


_TAIL_OUT_SHAPES = [jax.ShapeDtypeStruct((T_ALL, D), F32),
                    jax.ShapeDtypeStruct((T_ALL * TOK_ROWS, LANES), F32),
                    jax.ShapeDtypeStruct((SUBLANES, LANES), I32)] + [
                        jax.ShapeDtypeStruct((LIST_ROWS, LIST_COLS), dt) for dt in (I32, F32, I32, F32)]


def _tail_scratch():
    return [pltpu.VMEM((SUBLANES, LANES), F32), pltpu.VMEM((LIST_ROWS, LIST_COLS), F32),
            pltpu.VMEM((LIST_ROWS, LIST_COLS), F32)]


def _tail_out_specs():
    return [_tile_spec(D),
            pl.BlockSpec((TM * TOK_ROWS, LANES), lambda i: (i, 0)),
            _full_spec((SUBLANES, LANES))] + [_full_spec((LIST_ROWS, LIST_COLS)) for _ in range(4)]


def _halo_specs(halo, width):
    per = TM // halo
    last = T_ALL // halo - 1
    prev = pl.BlockSpec((halo, width), lambda i: (jnp.maximum(i * per - 1, 0), 0))
    nxt = pl.BlockSpec((halo, width), lambda i: (jnp.minimum((i + 1) * per, last), 0))
    return prev, nxt


def _mid_even_kernel(l, a_ref, ap_ref, an_ref, ybc_ref, ybl_ref, xc_ref, xl_ref, mod_ref, wp_ref, ps_ref, wo_ref,
                     gffn_ref, wr_ref, *refs):
    outs, (pad_ref,), scratch = refs[:N_TAIL_OUT], refs[N_TAIL_OUT:-N_TAIL_SCRATCH], refs[-N_TAIL_SCRATCH:]
    i = pl.program_id(0)
    r = _mod_row(i)
    off, n, first, last = _seq_info(i)
    a = a_ref[...]
    pad_ref[0:POOL_HALO, :] = jnp.where(first, 0.0, ap_ref[...])
    pad_ref[POOL_HALO:POOL_HALO + TM, :] = a
    pad_ref[POOL_HALO + TM:, :] = jnp.where(last, 0.0, an_ref[...])
    t = off + lax.broadcasted_iota(I32, (TM, 1), 0)
    mixed = []
    for g, w in enumerate(POOL_WINDOWS):
        lo = w // 2
        hi = w - lo - 1
        cols = slice(g * POOL_GROUP_DIM, (g + 1) * POOL_GROUP_DIM)
        total = pad_ref[POOL_HALO - lo:POOL_HALO - lo + TM, cols]
        for j in range(-lo + 1, hi + 1):
            total = total + pad_ref[POOL_HALO + j:POOL_HALO + j + TM, cols]
        count = (jnp.minimum(t + hi, n - 1) - jnp.maximum(t - lo, 0) + 1).astype(F32)
        pooled = total / count - a[:, cols]
        mixed.append(_dot(pooled, wp_ref[g]))
    ya = jnp.concatenate(mixed, axis=1) * ps_ref[...]
    y_pool = _dot(ya, wo_ref[0:POOL_DIM, :])

    def body(pop):
        y = y_pool + _dot((ybc_ref, ybl_ref)[pop][...], wo_ref[POOL_DIM:, :])
        _mixer_tail(l, (xc_ref, xl_ref)[pop][...], y, mod_ref, r, gffn_ref, wr_ref, outs, scratch)

    _per_population(i, body)


def _mid_even(l, e, a, yb_c, yb_l, xc, xl, mod, w_pool, pool_scale, w_out, gffn, wr):
    prev, nxt = _halo_specs(POOL_HALO, POOL_DIM)
    return pl.pallas_call(
        functools.partial(_mid_even_kernel, l),
        grid=(NT,),
        in_specs=[_tile_spec(POOL_DIM), prev, nxt] + _pop_specs(TM, Q_DIM) + _pop_specs(TM, D) + [
                  _MOD_SPEC, _layer_spec((len(POOL_WINDOWS), POOL_GROUP_DIM, POOL_GROUP_DIM), e),
                  _layer_spec((1, POOL_DIM), e), _layer_spec((D, D), e), _full_spec((DEPTH, D)),
                  _layer_spec((2, D, LANES), l)],
        out_specs=_tail_out_specs(),
        out_shape=_TAIL_OUT_SHAPES,
        scratch_shapes=[pltpu.VMEM((TM + 2 * POOL_HALO, POOL_DIM), F32)] + _tail_scratch(),
        compiler_params=_params(),
        name="mid_even",
    )(a, a, a, yb_c, yb_l, xc, xl, mod, w_pool, pool_scale, w_out, gffn, wr)


def _in_odd_kernel(l, x1_ref, moec_ref, moel_ref, mod_ref, g_ref, w1_ref, b1_ref, x2_ref, u_ref):
    i = pl.program_id(0)
    r = _mod_row(i)

    def body(pop):
        g2 = mod_ref[l - 1, pl.ds(r, 1), 5 * D:6 * D]
        x2 = x1_ref[...] + g2 * _from_token_tiles((moec_ref, moel_ref)[pop])
        x2_ref[...] = x2
        shift = mod_ref[l, pl.ds(r, 1), 0:D]
        scale = mod_ref[l, pl.ds(r, 1), D:2 * D]
        h = _rms_mod(x2, g_ref[l:l + 1, :], scale, shift)
        u = _dot(h, w1_ref[...]) + b1_ref[...]
        u_ref[...] = u[:, :D] * (1.0 / (1.0 + jnp.exp(-u[:, D:])))

    _per_population(i, body)


def _in_odd(l, o, x1, moe_c, moe_l, mod, g, w1, b1):
    return pl.pallas_call(
        functools.partial(_in_odd_kernel, l),
        grid=(NT,),
        in_specs=[_tile_spec(D)] + _pop_specs(TM * TOK_ROWS, LANES) + [
                  _MOD_SPEC, _full_spec((DEPTH, D)), _layer_spec((D, 2 * D), o), _layer_spec((1, 2 * D), o)],
        out_specs=[_tile_spec(D), _tile_spec(D)],
        out_shape=[jax.ShapeDtypeStruct((T_ALL, D), F32), jax.ShapeDtypeStruct((T_ALL, D), F32)],
        compiler_params=_params(),
        name="in_odd",
    )(x1, moe_c, moe_l, mod, g, w1, b1)


CONV_PAD_ROWS = TM + 2 * CONV_HALO
CONV_CHUNKS = D // LANES
CONV_BLOCK = 64


def _mid_odd_kernel(l, u_ref, up_ref, un_ref, x_ref, mod_ref, dw_ref, dwb_ref, lng_ref, lnb_ref, w2_ref, b2_ref,
                    gffn_ref, wr_ref, *refs):
    outs, scratch = refs[:N_TAIL_OUT], refs[-N_TAIL_SCRATCH:]
    pad_ref, conv_ref = refs[N_TAIL_OUT:-N_TAIL_SCRATCH]
    i = pl.program_id(0)
    r = _mod_row(i)
    _, _, first, last = _seq_info(i)
    u = u_ref[...]
    up = jnp.where(first, 0.0, up_ref[...])
    un = jnp.where(last, 0.0, un_ref[...])
    half = CONV_WIDTH // 2
    first_row = CONV_HALO - half
    for c in range(CONV_CHUNKS):
        cols = slice(c * LANES, (c + 1) * LANES)
        pad_ref[c, 0:CONV_HALO, :] = up[:, cols]
        pad_ref[c, CONV_HALO:CONV_HALO + TM, :] = u[:, cols]
        pad_ref[c, CONV_HALO + TM:, :] = un[:, cols]
        bias = jnp.broadcast_to(dwb_ref[:, cols], (CONV_BLOCK, LANES))
        for r0 in range(0, TM, 2 * CONV_BLOCK):
            acc_even, acc_odd = bias, bias
            for s in range(first_row, first_row + CONV_WIDTH + 1):
                win = pad_ref[c, pl.ds(r0 + s, CONV_BLOCK, stride=2), :]
                j = s - first_row
                if j < CONV_WIDTH:
                    acc_even = acc_even + win * dw_ref[j:j + 1, cols]
                if j >= 1:
                    acc_odd = acc_odd + win * dw_ref[j - 1:j, cols]
            conv_ref[c, pl.ds(r0, CONV_BLOCK, stride=2), :] = acc_even
            conv_ref[c, pl.ds(r0 + 1, CONV_BLOCK, stride=2), :] = acc_odd
    acc = jnp.concatenate([conv_ref[c] for c in range(CONV_CHUNKS)], axis=1)
    mu = jnp.mean(acc, axis=-1, keepdims=True)
    cen = acc - mu
    var = jnp.mean(cen * cen, axis=-1, keepdims=True)
    v = _silu(cen * lax.rsqrt(var + EPS) * lng_ref[...] + lnb_ref[...])
    y = _dot(v, w2_ref[...]) + b2_ref[...]
    _mixer_tail(l, x_ref[...], y, mod_ref, r, gffn_ref, wr_ref, outs, scratch)


def _mid_odd(l, o, u, x, mod, dw, dwb, lng, lnb, w2, b2, gffn, wr):
    prev, nxt = _halo_specs(CONV_HALO, D)
    return pl.pallas_call(
        functools.partial(_mid_odd_kernel, l),
        grid=(NT,),
        in_specs=[_tile_spec(D), prev, nxt, _tile_spec(D), _MOD_SPEC, _layer_spec((CONV_WIDTH, D), o),
                  _layer_spec((1, D), o), _layer_spec((1, D), o), _layer_spec((1, D), o),
                  _layer_spec((D, D), o), _layer_spec((1, D), o), _full_spec((DEPTH, D)),
                  _layer_spec((2, D, LANES), l)],
        out_specs=_tail_out_specs(),
        out_shape=_TAIL_OUT_SHAPES,
        scratch_shapes=[pltpu.VMEM((CONV_CHUNKS, CONV_PAD_ROWS, LANES), F32),
                        pltpu.VMEM((CONV_CHUNKS, TM, LANES), F32)] + _tail_scratch(),
        compiler_params=_params(),
        name="mid_odd",
    )(u, u, u, x, mod, dw, dwb, lng, lnb, w2, b2, gffn, wr)


def _tile_tables_kernel(cnt_ref, te_c, tc_c, nv_c, te_l, tc_l, nv_l, nact_ref, first_ref):
    for p, (te_ref, tc_ref, nv_ref) in enumerate(((te_c, tc_c, nv_c), (te_l, tc_l, nv_l))):
        run = jnp.int32(0)
        ends = []
        for e in range(N_EXPERTS):
            first_ref[p * N_EXPERTS + e] = run
            run = run + (cnt_ref[p, e] + MOE_TM - 1) // MOE_TM
            ends.append(run)
        nact_ref[p] = run

        def expert_of(tile, ends=ends):
            e = jnp.int32(0)
            for k in range(N_EXPERTS):
                e = e + jnp.where(tile >= ends[k], 1, 0)
            return e

        last_e = expert_of(run - 1)

        def tile_body(j, c, te_ref=te_ref, tc_ref=tc_ref, nv_ref=nv_ref, p=p, expert_of=expert_of, last_e=last_e):
            e = jnp.minimum(expert_of(j), last_e)
            chunk = j - first_ref[p * N_EXPERTS + e]
            te_ref[j] = e
            tc_ref[j] = chunk
            nv_ref[j] = jnp.clip(cnt_ref[p, e] - chunk * MOE_TM, 0, MOE_TM)
            return c

        lax.fori_loop(0, te_ref.shape[0], tile_body, 0)


def _tile_tables(cnt):
    smem = pl.BlockSpec(memory_space=pltpu.SMEM)
    shapes = [(MOE_TILES_CTX,)] * 3 + [(MOE_TILES_LAT,)] * 3 + [(2,)]
    return pl.pallas_call(
        _tile_tables_kernel,
        in_specs=[smem],
        out_specs=[smem] * len(shapes),
        out_shape=[jax.ShapeDtypeStruct(s, I32) for s in shapes],
        scratch_shapes=[pltpu.SMEM((2 * N_EXPERTS,), I32)],
        name="tile_tables",
    )(cnt)


def _moe_kernel(pop, te_ref, tc_ref, nval_ref, nact_ref, list_ref, h_ref, gate_ref, w1_ref, w3_ref, w2_ref,
                o_ref, xs_ref, ys_ref):
    j = pl.program_id(0)

    @pl.when(j == 0)
    def _():
        o_ref[...] = jnp.zeros(o_ref.shape, F32)
        xs_ref[...] = jnp.zeros(xs_ref.shape, F32)

    @pl.when(j < nact_ref[pop])
    def _():
        nval = nval_ref[j]
        nfull = nval >> MOVE_BATCH_LOG2
        list_base = tc_ref[j] * (MOE_TM // LIST_SPAN) * LIST_COLS + te_ref[j] * LIST_SPAN

        def entry(r):
            return list_base + (r >> LIST_SPAN_LOG2) * LIST_COLS + (r & (LIST_SPAN - 1))

        def tok_rows(i):
            return pl.ds(pl.multiple_of(list_ref[i], TOK_ROWS), TOK_ROWS)

        def rows(r):
            return pl.ds(pl.multiple_of(r * TOK_ROWS, TOK_ROWS), TOK_ROWS)

        def gather_batch(b, c):
            r0 = b * MOVE_BATCH
            i0 = entry(r0)
            for k in range(MOVE_BATCH):
                xs_ref[rows(r0 + k), :] = h_ref[tok_rows(i0 + k), :]
            return c

        def gather_one(r, c):
            xs_ref[rows(r), :] = h_ref[tok_rows(entry(r)), :]
            return c

        lax.fori_loop(0, nfull, gather_batch, 0)
        lax.fori_loop(nfull * MOVE_BATCH, nval, gather_one, 0)

        x = jnp.concatenate([xs_ref[pl.ds(s, MOE_TM, stride=TOK_ROWS), :] for s in range(TOK_ROWS)], axis=1)
        x = x.astype(BF16)
        hg = jnp.dot(x, w1_ref[...].astype(BF16), preferred_element_type=F32)
        hu = jnp.dot(x, w3_ref[...].astype(BF16), preferred_element_type=F32)
        y = _dot(_silu(hg) * hu, w2_ref[...])

        first_row = pl.multiple_of(tc_ref[j] * (MOE_TM // LIST_SPAN), MOE_TM // LIST_SPAN)
        spread = jnp.concatenate(
            [jnp.broadcast_to(gate_ref[pl.ds(first_row + k, 1), :], (LIST_SPAN, LIST_COLS))
             for k in range(MOE_TM // LIST_SPAN)], axis=0)
        want = te_ref[j] * LIST_SPAN + (lax.broadcasted_iota(I32, (MOE_TM, 1), 0) & (LIST_SPAN - 1))
        hit = lax.broadcasted_iota(I32, (MOE_TM, LIST_COLS), 1) == want
        y = y * jnp.sum(jnp.where(hit, spread, 0.0), axis=1, keepdims=True)
        for s in range(TOK_ROWS):
            ys_ref[pl.ds(s, MOE_TM, stride=TOK_ROWS), :] = y[:, s * LANES:(s + 1) * LANES]

        def scatter_batch(b, c):
            r0 = b * MOVE_BATCH
            i0 = entry(r0)
            dst = [tok_rows(i0 + k) for k in range(MOVE_BATCH)]
            vals = [o_ref[dst[k], :] + ys_ref[rows(r0 + k), :] for k in range(MOVE_BATCH)]
            for k in range(MOVE_BATCH):
                o_ref[dst[k], :] = vals[k]
            return c

        def scatter_one(r, c):
            i = entry(r)
            o_ref[tok_rows(i), :] = o_ref[tok_rows(i), :] + ys_ref[rows(r), :]
            return c

        lax.fori_loop(0, nfull, scatter_batch, 0)
        lax.fori_loop(nfull * MOVE_BATCH, nval, scatter_one, 0)


def _moe(l, pop, h_tt, te, tc, nval, nact, rows, gate, w1, w3, w2):
    tp = (T_CTX, T_LAT)[pop]
    blk = (0, T_CTX // T_LAT)[pop]
    wmap = lambda j, te, *_: (l, te[j], 0, 0)
    wspec = lambda shape: pl.BlockSpec((None, None) + shape, wmap)
    grid_spec = pltpu.PrefetchScalarGridSpec(
        num_scalar_prefetch=5,
        grid=(te.shape[0],),
        in_specs=[
            pl.BlockSpec((tp * TOK_ROWS, LANES), lambda j, *_: (blk, 0), pipeline_mode=pl.Buffered(1)),
            pl.BlockSpec((LIST_ROWS, LIST_COLS), lambda j, *_: (0, 0), pipeline_mode=pl.Buffered(1)),
            wspec((D, EXPERT_HIDDEN)), wspec((D, EXPERT_HIDDEN)), wspec((EXPERT_HIDDEN, D)),
        ],
        out_specs=pl.BlockSpec((tp * TOK_ROWS, LANES), lambda j, *_: (0, 0), pipeline_mode=pl.Buffered(1)),
        scratch_shapes=[pltpu.VMEM((MOE_TM * TOK_ROWS, LANES), F32),
                        pltpu.VMEM((MOE_TM * TOK_ROWS, LANES), F32)],
    )
    return pl.pallas_call(
        functools.partial(_moe_kernel, pop),
        grid_spec=grid_spec,
        out_shape=jax.ShapeDtypeStruct((tp * TOK_ROWS, LANES), F32),
        compiler_params=_params(),
        name="moe",
    )(te, tc, nval, nact, rows, h_tt, gate, w1, w3, w2)


def _moe_all(l, h_tt, cnt, rows_c, gate_c, rows_l, gate_l, w1, w3, w2):
    te_c, tc_c, nv_c, te_l, tc_l, nv_l, nact = _tile_tables(cnt)
    out_c = _moe(l, 0, h_tt, te_c, tc_c, nv_c, nact, rows_c.reshape(-1), gate_c, w1, w3, w2)
    out_l = _moe(l, 1, h_tt, te_l, tc_l, nv_l, nact, rows_l.reshape(-1), gate_l, w1, w3, w2)
    return out_c, out_l


def _final_kernel(l, x1_ref, moec_ref, moel_ref, mod_ref, g_ref, oc_ref, ol_ref):
    i = pl.program_id(0)
    r = _mod_row(i)

    def body(pop):
        g2 = mod_ref[l, pl.ds(r, 1), 5 * D:6 * D]
        x = x1_ref[...] + g2 * _from_token_tiles((moec_ref, moel_ref)[pop])
        ms = jnp.mean(x * x, axis=-1, keepdims=True)
        (oc_ref, ol_ref)[pop][...] = x * lax.rsqrt(ms + EPS) * g_ref[...]

    _per_population(i, body)


def _final(l, x1, moe_c, moe_l, mod, g):
    return pl.pallas_call(
        functools.partial(_final_kernel, l),
        grid=(NT,),
        in_specs=[_tile_spec(D)] + _pop_specs(TM * TOK_ROWS, LANES) + [_MOD_SPEC, _full_spec((1, D))],
        out_specs=_pop_specs(TM, D),
        out_shape=[jax.ShapeDtypeStruct((T_CTX, D), F32), jax.ShapeDtypeStruct((T_LAT, D), F32)],
        compiler_params=_params(),
        name="final_norm",
    )(x1, moe_c, moe_l, mod, g)


def _rope_tables():
    t = jnp.arange(LAT_LEN)
    row = (t // GRID_W).astype(F32)
    col = (t % GRID_W).astype(F32)
    inv = ROPE_BASE ** (-jnp.arange(0, ROPE_AXIS_DIM, 2, dtype=F32) / ROPE_AXIS_DIM)

    def table(p):
        ang = p[:, None] * inv[None, :]
        ang = jnp.concatenate([ang, ang], axis=-1)
        return jnp.cos(ang), jnp.sin(ang)

    (cr, sr), (cc, sc) = table(row), table(col)
    cos = jnp.concatenate([cr, cc, cr, cc], axis=-1)
    sin = jnp.concatenate([sr, sc, sr, sc], axis=-1)
    return cos, sin


def _router_weights(w_grp, w_exp):
    we = jnp.transpose(w_exp, (0, 2, 1, 3)).reshape(DEPTH, D, N_GROUPS * EPG)
    pad = jnp.zeros((DEPTH, D, LANES - N_GROUPS - N_GROUPS * EPG), F32)
    wr = jnp.concatenate([w_grp, we, pad], axis=-1)
    hi = lax.bitcast_convert_type(lax.bitcast_convert_type(wr, jnp.uint32) & jnp.uint32(0xFFFF0000), F32)
    return jnp.stack([hi.astype(BF16), (wr - hi).astype(BF16)], axis=1)


def kernel(x_prompt, x_sample, cache_k, cache_v, c, c_ctx, w_ada, b_ada, norm_mix_g, norm_ffn_g, w_in_ab, pool_w, pool_scale, attn_sink, w_out_ab, conv_w1, conv_b1, conv_dw, conv_dw_b, conv_ln_g, conv_ln_b, conv_w2, conv_b2, router_grp, router_exp, moe_w1, moe_w3, moe_w2, final_g):
    xc = x_prompt.reshape(T_CTX, D)
    xl = x_sample.reshape(T_LAT, D)
    cond8 = jnp.concatenate([c_ctx[None, :], c, jnp.zeros((SUBLANES - 1 - N_LAT_SEQ, D), F32)], axis=0)
    mod = _modulation(cond8, w_ada, b_ada)
    cos_t, sin_t = _rope_tables()
    wr = _router_weights(router_grp, router_exp)
    n_even, n_odd = w_in_ab.shape[0], conv_w1.shape[0]
    past = cache_k.shape[2]

    a, q, k, v = _in_even(0, 0, xc, xl, mod, norm_mix_g, w_in_ab, cos_t, sin_t)
    yb_c = _attn_ctx(0, attn_sink, q, k, v)
    yb_l = _attn_lat(0, attn_sink, q, k, v, cache_k[:, 0].reshape(N_LAT_SEQ, past, KV_DIM),
                     cache_v[:, 0].reshape(N_LAT_SEQ, past, KV_DIM))
    x1, h2, *routing = _mid_even(0, 0, a, yb_c, yb_l, xc, xl, mod, pool_w,
                                 pool_scale.reshape(n_even, 1, POOL_DIM), w_out_ab, norm_ffn_g, wr)
    moe0 = _moe_all(0, h2, *routing, moe_w1, moe_w3, moe_w2)

    vec = lambda p: p.reshape(n_odd, 1, -1)
    x2, u = _in_odd(1, 0, x1, *moe0, mod, norm_mix_g, conv_w1, vec(conv_b1))
    x3, h2b, *routing = _mid_odd(1, 0, u, x2, mod, conv_dw, vec(conv_dw_b), vec(conv_ln_g),
                                 vec(conv_ln_b), conv_w2, vec(conv_b2), norm_ffn_g, wr)
    moe1 = _moe_all(1, h2b, *routing, moe_w1, moe_w3, moe_w2)
    y_c, y_l = _final(1, x3, *moe1, mod, final_g.reshape(1, D))

    y_prompt = y_c.reshape(N_CTX_SEQ, CTX_LEN, D)
    y_sample = y_l.reshape(N_LAT_SEQ, LAT_LEN, D)
    state_k = k[:T_CTX].reshape(N_CTX_SEQ, 1, CTX_LEN, N_KV_HEADS, HEAD_DIM)
    state_v = v[:T_CTX].reshape(N_CTX_SEQ, 1, CTX_LEN, N_KV_HEADS, HEAD_DIM)
    return (y_prompt, y_sample, state_k, state_v)
```

```python
import functools

import jax
import jax.numpy as jnp
from jax import lax
from jax.experimental import pallas as pl
from jax.experimental.pallas import tpu as pltpu

F32 = jnp.float32
BF16 = jnp.bfloat16
I32 = jnp.int32

D = 1024
N_CTX_SEQ = 16
CTX_LEN = 256
N_LAT_SEQ = 2
LAT_LEN = 1024
T_CTX = N_CTX_SEQ * CTX_LEN
T_LAT = N_LAT_SEQ * LAT_LEN
T_ALL = T_CTX + T_LAT
TM = 256
NT = T_ALL // TM
NT_CTX = T_CTX // TM
LAT_TILES_PER_SEQ = LAT_LEN // TM
GRID_W = 64
DEPTH = 2

POOL_WINDOWS = (2, 4, 8, 16)
POOL_GROUP_DIM = 128
POOL_DIM = 512
HEAD_DIM = 64
N_Q_HEADS = 8
N_KV_HEADS = 2
Q_PER_KV = 4
Q_DIM = 512
KV_DIM = 128
IN_AB = 1280
ATTN_WINDOW = 128
ATTN_BLOCK = 128
ATTN_SCALE = HEAD_DIM ** -0.5
ROPE_BASE = 10000.0
ROPE_AXIS_DIM = 32
CONV_WIDTH = 31
CONV_HALO = 16
POOL_HALO = 8
N_GROUPS = 4
EPG = 4
N_EXPERTS = 16
EXPERT_HIDDEN = 512
EPS = 1e-6
NEG_BIG = -1e30

SUBLANES = 8
LANES = 128
TOK_ROWS = D // LANES

MOE_TM = 256
MOE_TILES_CTX = 2 * T_CTX // MOE_TM + N_EXPERTS
MOE_TILES_LAT = 2 * T_LAT // MOE_TM + N_EXPERTS
MOVE_BATCH_LOG2 = 4
MOVE_BATCH = 1 << MOVE_BATCH_LOG2
VMEM_LIMIT = 56 * 1024 * 1024


def _silu(x):
    return x * (1.0 / (1.0 + jnp.exp(-x)))


def _mod_row(i):
    return jnp.where(i < NT_CTX, 0, 1 + (i - NT_CTX) // LAT_TILES_PER_SEQ)


def _seq_info(i):
    is_ctx = i < NT_CTX
    k = (i - NT_CTX) % LAT_TILES_PER_SEQ
    off = jnp.where(is_ctx, 0, k * TM)
    n = jnp.where(is_ctx, CTX_LEN, LAT_LEN)
    first = jnp.logical_or(is_ctx, k == 0)
    last = jnp.logical_or(is_ctx, k == LAT_TILES_PER_SEQ - 1)
    return off, n, first, last


def _rms_mod(x, g, scale, shift):
    ms = jnp.mean(x * x, axis=-1, keepdims=True)
    return (x * lax.rsqrt(ms + EPS) * g) * (1.0 + scale) + shift


def _dot(a, b):
    return jnp.dot(a.astype(BF16), b.astype(BF16), preferred_element_type=F32)


def _dot_nt(a, b):
    return lax.dot_general(a.astype(BF16), b.astype(BF16), (((1,), (1,)), ((), ())),
                           preferred_element_type=F32)


def _from_token_tiles(ref):
    return jnp.concatenate([ref[pl.ds(s, TM, stride=TOK_ROWS), :] for s in range(TOK_ROWS)], axis=1)


def _tile_spec(width):
    return pl.BlockSpec((TM, width), lambda i: (i, 0))


def _full_spec(shape):
    return pl.BlockSpec(shape, lambda i: (0,) * len(shape))


def _layer_spec(shape, l):
    return pl.BlockSpec((None,) + tuple(shape), lambda i: (l,) + (0,) * len(shape))


def _pop_specs(rows, width):
    return [pl.BlockSpec((rows, width), lambda i: (jnp.minimum(i, NT_CTX - 1), 0)),
            pl.BlockSpec((rows, width), lambda i: (jnp.maximum(i - NT_CTX, 0), 0))]


def _per_population(i, body):
    pl.when(i < NT_CTX)(functools.partial(body, 0))
    pl.when(i >= NT_CTX)(functools.partial(body, 1))
_MOD_SPEC = pl.BlockSpec((DEPTH, SUBLANES, 6 * D), lambda i: (0, 0, 0))


def _params():
    return pltpu.CompilerParams(vmem_limit_bytes=VMEM_LIMIT)


MOD_TN = 1536


def _mod_kernel(cond_ref, w_ref, b_ref, o_ref):
    s = _silu(cond_ref[...])
    o_ref[0] = _dot(s, w_ref[0]) + b_ref[0]


def _modulation(cond8, w_ada, b_ada):
    return pl.pallas_call(
        _mod_kernel,
        grid=(DEPTH, 6 * D // MOD_TN),
        in_specs=[
            pl.BlockSpec((SUBLANES, D), lambda l, n: (0, 0)),
            pl.BlockSpec((1, D, MOD_TN), lambda l, n: (l, 0, n)),
            pl.BlockSpec((1, 1, MOD_TN), lambda l, n: (l, 0, n)),
        ],
        out_specs=pl.BlockSpec((1, SUBLANES, MOD_TN), lambda l, n: (l, 0, n)),
        out_shape=jax.ShapeDtypeStruct((DEPTH, SUBLANES, 6 * D), F32),
        compiler_params=_params(),
        name="modulation",
    )(cond8, w_ada, b_ada.reshape(DEPTH, 1, 6 * D))


def _rope_chunk(xc, cos, sin):
    lane = lax.broadcasted_iota(I32, xc.shape, 1)
    first = (lane % ROPE_AXIS_DIM) < (ROPE_AXIS_DIM // 2)
    rot = jnp.where(first, -pltpu.roll(xc, LANES - ROPE_AXIS_DIM // 2, 1), pltpu.roll(xc, ROPE_AXIS_DIM // 2, 1))
    return xc * cos + rot * sin


def _in_even_kernel(l, xc_ref, xl_ref, mod_ref, g_ref, w_ref, cos_ref, sin_ref, a_ref, q_ref, k_ref, v_ref):
    i = pl.program_id(0)
    r = _mod_row(i)

    def body(pop):
        x = (xc_ref, xl_ref)[pop][...]
        shift = mod_ref[l, pl.ds(r, 1), 0:D]
        scale = mod_ref[l, pl.ds(r, 1), D:2 * D]
        h = _rms_mod(x, g_ref[l:l + 1, :], scale, shift)
        proj = _dot(h, w_ref[...])
        a_ref[...] = proj[:, :POOL_DIM]
        v_ref[...] = proj[:, POOL_DIM + Q_DIM + KV_DIM:]
        if pop == 0:
            q_ref[...] = proj[:, POOL_DIM:POOL_DIM + Q_DIM]
            k_ref[...] = proj[:, POOL_DIM + Q_DIM:POOL_DIM + Q_DIM + KV_DIM]
        else:
            off, _, _, _ = _seq_info(i)
            off = pl.multiple_of(off, TM)
            cos = cos_ref[pl.ds(off, TM), :]
            sin = sin_ref[pl.ds(off, TM), :]
            for c in range(Q_DIM // LANES):
                lo = POOL_DIM + c * LANES
                q_ref[:, c * LANES:(c + 1) * LANES] = _rope_chunk(proj[:, lo:lo + LANES], cos, sin)
            lo = POOL_DIM + Q_DIM
            k_ref[...] = _rope_chunk(proj[:, lo:lo + LANES], cos, sin)

    _per_population(i, body)


def _in_even(l, e, xc, xl, mod, g, w_in, cos_t, sin_t):
    return pl.pallas_call(
        functools.partial(_in_even_kernel, l),
        grid=(NT,),
        in_specs=_pop_specs(TM, D) + [_MOD_SPEC, _full_spec((DEPTH, D)), _layer_spec((D, IN_AB), e),
                  _full_spec((LAT_LEN, LANES)), _full_spec((LAT_LEN, LANES))],
        out_specs=[_tile_spec(POOL_DIM), _tile_spec(Q_DIM), _tile_spec(KV_DIM), _tile_spec(KV_DIM)],
        out_shape=[jax.ShapeDtypeStruct((T_ALL, POOL_DIM), F32), jax.ShapeDtypeStruct((T_ALL, Q_DIM), F32),
                   jax.ShapeDtypeStruct((T_ALL, KV_DIM), F32), jax.ShapeDtypeStruct((T_ALL, KV_DIM), F32)],
        compiler_params=_params(),
        name="in_even",
    )(xc, xl, mod, g, w_in, cos_t, sin_t)


def _attn_ctx_kernel(e, sink_ref, q_ref, k_ref, v_ref, o_ref):
    q = q_ref[...]
    k = k_ref[...]
    v = v_ref[...]
    for h in range(N_KV_HEADS):
        kh = k[:, h * HEAD_DIM:(h + 1) * HEAD_DIM]
        vh = v[:, h * HEAD_DIM:(h + 1) * HEAD_DIM]
        for g in range(Q_PER_KV):
            j = h * Q_PER_KV + g
            s = _dot_nt(q[:, j * HEAD_DIM:(j + 1) * HEAD_DIM], kh) * ATTN_SCALE
            sk = sink_ref[e, j]
            m = jnp.maximum(jnp.max(s, axis=-1, keepdims=True), sk)
            p = jnp.exp(s - m)
            denom = jnp.sum(p, axis=-1, keepdims=True) + jnp.exp(sk - m)
            o_ref[:, j * HEAD_DIM:(j + 1) * HEAD_DIM] = _dot(p, vh) / denom


def _attn_ctx(e, sink, q, k, v):
    tile = lambda w: pl.BlockSpec((TM, w), lambda b: (b, 0))
    return pl.pallas_call(
        functools.partial(_attn_ctx_kernel, e),
        grid=(N_CTX_SEQ,),
        in_specs=[pl.BlockSpec(memory_space=pltpu.SMEM), tile(Q_DIM), tile(KV_DIM), tile(KV_DIM)],
        out_specs=tile(Q_DIM),
        out_shape=jax.ShapeDtypeStruct((T_CTX, Q_DIM), F32),
        compiler_params=_params(),
        name="attn_ctx",
    )(sink, q, k, v)


LAT_BLOCKS = LAT_LEN // ATTN_BLOCK
BAND = 3 * ATTN_BLOCK


def _attn_lat_kernel(e, sink_ref, q_ref, k_ref, v_ref, ck_ref, cv_ref, o_ref):
    n = pl.program_id(1)
    start = jnp.clip((n - 1) * ATTN_BLOCK, 0, LAT_LEN - BAND)
    start = pl.multiple_of(start, ATTN_BLOCK)
    q = q_ref[...]
    kw = k_ref[pl.ds(start, BAND), :]
    vw = v_ref[pl.ds(start, BAND), :]
    ck = ck_ref[0]
    cv = cv_ref[0]
    qpos = n * ATTN_BLOCK + lax.broadcasted_iota(I32, (ATTN_BLOCK, BAND), 0)
    kpos = start + lax.broadcasted_iota(I32, (ATTN_BLOCK, BAND), 1)
    valid = jnp.abs(qpos - kpos) <= ATTN_WINDOW
    for h in range(N_KV_HEADS):
        hs = slice(h * HEAD_DIM, (h + 1) * HEAD_DIM)
        for g in range(Q_PER_KV):
            j = h * Q_PER_KV + g
            qj = q[:, j * HEAD_DIM:(j + 1) * HEAD_DIM]
            s_loc = jnp.where(valid, _dot_nt(qj, kw[:, hs]) * ATTN_SCALE, NEG_BIG)
            s_ctx = _dot_nt(qj, ck[:, hs]) * ATTN_SCALE
            sk = sink_ref[e, j]
            m = jnp.maximum(jnp.maximum(jnp.max(s_loc, axis=-1, keepdims=True),
                                        jnp.max(s_ctx, axis=-1, keepdims=True)), sk)
            p_loc = jnp.exp(s_loc - m)
            p_ctx = jnp.exp(s_ctx - m)
            denom = (jnp.sum(p_loc, axis=-1, keepdims=True) + jnp.sum(p_ctx, axis=-1, keepdims=True)
                     + jnp.exp(sk - m))
            o = _dot(p_loc, vw[:, hs]) + _dot(p_ctx, cv[:, hs])
            o_ref[:, j * HEAD_DIM:(j + 1) * HEAD_DIM] = o / denom


def _attn_lat(e, sink, q, k, v, ck, cv):
    past = ck.shape[1]
    q_blk0 = T_CTX // ATTN_BLOCK
    kv_blk0 = T_CTX // LAT_LEN
    q_spec = pl.BlockSpec((ATTN_BLOCK, Q_DIM), lambda b, n: (q_blk0 + b * LAT_BLOCKS + n, 0))
    return pl.pallas_call(
        functools.partial(_attn_lat_kernel, e),
        grid=(N_LAT_SEQ, LAT_BLOCKS),
        in_specs=[pl.BlockSpec(memory_space=pltpu.SMEM),
                  q_spec,
                  pl.BlockSpec((LAT_LEN, KV_DIM), lambda b, n: (kv_blk0 + b, 0)),
                  pl.BlockSpec((LAT_LEN, KV_DIM), lambda b, n: (kv_blk0 + b, 0)),
                  pl.BlockSpec((1, past, KV_DIM), lambda b, n: (b, 0, 0)),
                  pl.BlockSpec((1, past, KV_DIM), lambda b, n: (b, 0, 0))],
        out_specs=pl.BlockSpec((ATTN_BLOCK, Q_DIM), lambda b, n: (b * LAT_BLOCKS + n, 0)),
        out_shape=jax.ShapeDtypeStruct((T_LAT, Q_DIM), F32),
        compiler_params=_params(),
        name="attn_lat",
    )(sink, q, k, v, ck, cv)


def _router(h2, wr_ref):
    hi = h2.astype(BF16)
    lo = (h2 - hi.astype(F32)).astype(BF16)
    whi = wr_ref[0]
    wlo = wr_ref[1]
    dot = functools.partial(jnp.dot, preferred_element_type=F32)
    logits = dot(hi, whi) + (dot(lo, whi) + dot(hi, wlo))
    lane = lax.broadcasted_iota(I32, logits.shape, 1).astype(F32)
    neg = jnp.float32(-jnp.inf)
    big = jnp.float32(1e9)
    is_grp = lane < N_GROUPS
    gl = jnp.where(is_grp, logits, neg)
    gmax = jnp.max(gl, axis=-1, keepdims=True)
    gsum = jnp.sum(jnp.where(is_grp, jnp.exp(logits - gmax), 0.0), axis=-1, keepdims=True)
    g_w = 1.0 / gsum
    g_idx = jnp.min(jnp.where(gl == gmax, lane, big), axis=-1, keepdims=True)
    base = N_GROUPS + EPG * g_idx
    in_grp = jnp.logical_and(lane >= base, lane < base + EPG)
    el = jnp.where(in_grp, logits, neg)
    t1 = jnp.max(el, axis=-1, keepdims=True)
    i1 = jnp.min(jnp.where(el == t1, lane, big), axis=-1, keepdims=True)
    el2 = jnp.where(lane == i1, neg, el)
    t2 = jnp.max(el2, axis=-1, keepdims=True)
    i2 = jnp.min(jnp.where(el2 == t2, lane, big), axis=-1, keepdims=True)
    d = jnp.exp(t2 - t1)
    w1 = g_w / (1.0 + d)
    w2 = g_w * d / (1.0 + d)
    return i1 - N_GROUPS, i2 - N_GROUPS, w1, w2


LIST_SPAN_LOG2 = 5
LIST_SPAN = 1 << LIST_SPAN_LOG2
LIST_ROWS = T_CTX // LIST_SPAN
LIST_COLS = N_EXPERTS * LIST_SPAN


N_TAIL_OUT = 7
N_TAIL_SCRATCH = 3


def _mixer_tail(l, x, y, mod_ref, r, gffn_ref, wr_ref, outs, scratch):
    x1_ref, h2_ref, cnt_ref, lc_ref, gc_ref, ll_ref, gl_ref = outs
    run_ref, acc_ref, gacc_ref = scratch
    i = pl.program_id(0)
    g1 = mod_ref[l, pl.ds(r, 1), 2 * D:3 * D]
    shift2 = mod_ref[l, pl.ds(r, 1), 3 * D:4 * D]
    scale2 = mod_ref[l, pl.ds(r, 1), 4 * D:5 * D]
    x1 = x + g1 * y
    x1_ref[...] = x1
    h2 = _rms_mod(x1, gffn_ref[l:l + 1, :], scale2, shift2)
    for s in range(TOK_ROWS):
        h2_ref[pl.ds(s, TM, stride=TOK_ROWS), :] = h2[:, s * LANES:(s + 1) * LANES]
    e1, e2, w1, w2 = _router(h2, wr_ref)

    @pl.when(jnp.logical_or(i == 0, i == NT_CTX))
    def _():
        run_ref[...] = jnp.zeros(run_ref.shape, F32)
        acc_ref[...] = jnp.zeros(acc_ref.shape, F32)
        gacc_ref[...] = jnp.zeros(gacc_ref.shape, F32)

    @pl.when(i == 0)
    def _():
        cnt_ref[...] = jnp.zeros(cnt_ref.shape, I32)

    lane_i = lax.broadcasted_iota(I32, (TM, LANES), 1)
    lane = lane_i.astype(F32)
    member = jnp.where(jnp.logical_or(lane == e1, lane == e2), 1.0, 0.0)
    tri = jnp.where(lax.broadcasted_iota(I32, (TM, TM), 0) >= lax.broadcasted_iota(I32, (TM, TM), 1), 1.0, 0.0)
    csum = jnp.dot(tri.astype(BF16), member.astype(BF16), preferred_element_type=F32)
    run = run_ref[0:1, :]
    before = csum - member + run
    r1 = jnp.sum(jnp.where(lane == e1, before, 0.0), axis=-1, keepdims=True)
    r2 = jnp.sum(jnp.where(lane == e2, before, 0.0), axis=-1, keepdims=True)
    run_new = run + csum[TM - 1:TM, :]
    run_ref[0:1, :] = run_new

    col_i = lax.broadcasted_iota(I32, (TM, LIST_COLS), 1)
    rows_oh, cols_oh = [], []
    for e, rank in ((e1, r1), (e2, r2)):
        rank_i = rank.astype(I32)
        rows_oh.append(jnp.where(lane_i == rank_i // LIST_SPAN, 1.0, 0.0))
        cols_oh.append(jnp.where(col_i == e.astype(I32) * LIST_SPAN + rank_i % LIST_SPAN, 1.0, 0.0).astype(BF16))
    rows_t = jnp.concatenate(rows_oh, axis=0).T
    cols = jnp.concatenate(cols_oh, axis=0)
    tok = (lax.broadcasted_iota(I32, (1, 2 * TM), 1) % TM).astype(F32)
    dot = functools.partial(jnp.dot, preferred_element_type=F32)
    hits = dot(rows_t.astype(BF16), cols)
    tok_sum = dot((rows_t * tok).astype(BF16), cols)
    tile_in_pop = jnp.where(i < NT_CTX, i, i - NT_CTX).astype(F32)
    acc = acc_ref[...] + TOK_ROWS * (tok_sum + TM * tile_in_pop * hits)
    acc_ref[...] = acc

    rec = jnp.where(lane_i == 0, w1, jnp.where(lane_i == 1, w2, 0.0)).T
    rest = jnp.concatenate([rec[0:1, :], rec[1:2, :]], axis=1)
    gacc = gacc_ref[...]
    for _ in range(3):
        piece = rest.astype(BF16)
        gacc = gacc + dot((rows_t * piece.astype(F32)).astype(BF16), cols)
        rest = rest - piece.astype(F32)
    gacc_ref[...] = gacc

    @pl.when(i == NT_CTX - 1)
    def _():
        cnt_ref[0:1, :] = run_new.astype(I32)
        lc_ref[...] = acc.astype(I32)
        gc_ref[...] = gacc

    @pl.when(i == NT - 1)
    def _():
        cnt_ref[1:2, :] = run_new.astype(I32)
        ll_ref[...] = acc.astype(I32)
        gl_ref[...] = gacc


_TAIL_OUT_SHAPES = [jax.ShapeDtypeStruct((T_ALL, D), F32),
                    jax.ShapeDtypeStruct((T_ALL * TOK_ROWS, LANES), F32),
                    jax.ShapeDtypeStruct((SUBLANES, LANES), I32)] + [
                        jax.ShapeDtypeStruct((LIST_ROWS, LIST_COLS), dt) for dt in (I32, F32, I32, F32)]


def _tail_scratch():
    return [pltpu.VMEM((SUBLANES, LANES), F32), pltpu.VMEM((LIST_ROWS, LIST_COLS), F32),
            pltpu.VMEM((LIST_ROWS, LIST_COLS), F32)]


def _tail_out_specs():
    return [_tile_spec(D),
            pl.BlockSpec((TM * TOK_ROWS, LANES), lambda i: (i, 0)),
            _full_spec((SUBLANES, LANES))] + [_full_spec((LIST_ROWS, LIST_COLS)) for _ in range(4)]


def _halo_specs(halo, width):
    per = TM // halo
    last = T_ALL // halo - 1
    prev = pl.BlockSpec((halo, width), lambda i: (jnp.maximum(i * per - 1, 0), 0))
    nxt = pl.BlockSpec((halo, width), lambda i: (jnp.minimum((i + 1) * per, last), 0))
    return prev, nxt


def _mid_even_kernel(l, a_ref, ap_ref, an_ref, ybc_ref, ybl_ref, xc_ref, xl_ref, mod_ref, wp_ref, ps_ref, wo_ref,
                     gffn_ref, wr_ref, *refs):
    outs, (pad_ref,), scratch = refs[:N_TAIL_OUT], refs[N_TAIL_OUT:-N_TAIL_SCRATCH], refs[-N_TAIL_SCRATCH:]
    i = pl.program_id(0)
    r = _mod_row(i)
    off, n, first, last = _seq_info(i)
    a = a_ref[...]
    pad_ref[0:POOL_HALO, :] = jnp.where(first, 0.0, ap_ref[...])
    pad_ref[POOL_HALO:POOL_HALO + TM, :] = a
    pad_ref[POOL_HALO + TM:, :] = jnp.where(last, 0.0, an_ref[...])
    t = off + lax.broadcasted_iota(I32, (TM, 1), 0)
    mixed = []
    for g, w in enumerate(POOL_WINDOWS):
        lo = w // 2
        hi = w - lo - 1
        cols = slice(g * POOL_GROUP_DIM, (g + 1) * POOL_GROUP_DIM)
        total = pad_ref[POOL_HALO - lo:POOL_HALO - lo + TM, cols]
        for j in range(-lo + 1, hi + 1):
            total = total + pad_ref[POOL_HALO + j:POOL_HALO + j + TM, cols]
        count = (jnp.minimum(t + hi, n - 1) - jnp.maximum(t - lo, 0) + 1).astype(F32)
        pooled = total / count - a[:, cols]
        mixed.append(_dot(pooled, wp_ref[g]))
    ya = jnp.concatenate(mixed, axis=1) * ps_ref[...]
    y_pool = _dot(ya, wo_ref[0:POOL_DIM, :])

    def body(pop):
        y = y_pool + _dot((ybc_ref, ybl_ref)[pop][...], wo_ref[POOL_DIM:, :])
        _mixer_tail(l, (xc_ref, xl_ref)[pop][...], y, mod_ref, r, gffn_ref, wr_ref, outs, scratch)

    _per_population(i, body)


def _mid_even(l, e, a, yb_c, yb_l, xc, xl, mod, w_pool, pool_scale, w_out, gffn, wr):
    prev, nxt = _halo_specs(POOL_HALO, POOL_DIM)
    return pl.pallas_call(
        functools.partial(_mid_even_kernel, l),
        grid=(NT,),
        in_specs=[_tile_spec(POOL_DIM), prev, nxt] + _pop_specs(TM, Q_DIM) + _pop_specs(TM, D) + [
                  _MOD_SPEC, _layer_spec((len(POOL_WINDOWS), POOL_GROUP_DIM, POOL_GROUP_DIM), e),
                  _layer_spec((1, POOL_DIM), e), _layer_spec((D, D), e), _full_spec((DEPTH, D)),
                  _layer_spec((2, D, LANES), l)],
        out_specs=_tail_out_specs(),
        out_shape=_TAIL_OUT_SHAPES,
        scratch_shapes=[pltpu.VMEM((TM + 2 * POOL_HALO, POOL_DIM), F32)] + _tail_scratch(),
        compiler_params=_params(),
        name="mid_even",
    )(a, a, a, yb_c, yb_l, xc, xl, mod, w_pool, pool_scale, w_out, gffn, wr)


def _in_odd_kernel(l, x1_ref, moec_ref, moel_ref, mod_ref, g_ref, w1_ref, b1_ref, x2_ref, u_ref):
    i = pl.program_id(0)
    r = _mod_row(i)

    def body(pop):
        g2 = mod_ref[l - 1, pl.ds(r, 1), 5 * D:6 * D]
        x2 = x1_ref[...] + g2 * _from_token_tiles((moec_ref, moel_ref)[pop])
        x2_ref[...] = x2
        shift = mod_ref[l, pl.ds(r, 1), 0:D]
        scale = mod_ref[l, pl.ds(r, 1), D:2 * D]
        h = _rms_mod(x2, g_ref[l:l + 1, :], scale, shift)
        u = _dot(h, w1_ref[...]) + b1_ref[...]
        u_ref[...] = u[:, :D] * (1.0 / (1.0 + jnp.exp(-u[:, D:])))

    _per_population(i, body)


def _in_odd(l, o, x1, moe_c, moe_l, mod, g, w1, b1):
    return pl.pallas_call(
        functools.partial(_in_odd_kernel, l),
        grid=(NT,),
        in_specs=[_tile_spec(D)] + _pop_specs(TM * TOK_ROWS, LANES) + [
                  _MOD_SPEC, _full_spec((DEPTH, D)), _layer_spec((D, 2 * D), o), _layer_spec((1, 2 * D), o)],
        out_specs=[_tile_spec(D), _tile_spec(D)],
        out_shape=[jax.ShapeDtypeStruct((T_ALL, D), F32), jax.ShapeDtypeStruct((T_ALL, D), F32)],
        compiler_params=_params(),
        name="in_odd",
    )(x1, moe_c, moe_l, mod, g, w1, b1)


CONV_PAD_ROWS = TM + 2 * CONV_HALO
CONV_CHUNKS = D // LANES
CONV_BLOCK = 64


def _mid_odd_kernel(l, u_ref, up_ref, un_ref, x_ref, mod_ref, dw_ref, dwb_ref, lng_ref, lnb_ref, w2_ref, b2_ref,
                    gffn_ref, wr_ref, *refs):
    outs, scratch = refs[:N_TAIL_OUT], refs[-N_TAIL_SCRATCH:]
    pad_ref, conv_ref = refs[N_TAIL_OUT:-N_TAIL_SCRATCH]
    i = pl.program_id(0)
    r = _mod_row(i)
    _, _, first, last = _seq_info(i)
    u = u_ref[...]
    up = jnp.where(first, 0.0, up_ref[...])
    un = jnp.where(last, 0.0, un_ref[...])
    half = CONV_WIDTH // 2
    first_row = CONV_HALO - half
    for c in range(CONV_CHUNKS):
        cols = slice(c * LANES, (c + 1) * LANES)
        pad_ref[c, 0:CONV_HALO, :] = up[:, cols]
        pad_ref[c, CONV_HALO:CONV_HALO + TM, :] = u[:, cols]
        pad_ref[c, CONV_HALO + TM:, :] = un[:, cols]
        bias = jnp.broadcast_to(dwb_ref[:, cols], (CONV_BLOCK, LANES))
        for r0 in range(0, TM, 2 * CONV_BLOCK):
            acc_even, acc_odd = bias, bias
            for s in range(first_row, first_row + CONV_WIDTH + 1):
                win = pad_ref[c, pl.ds(r0 + s, CONV_BLOCK, stride=2), :]
                j = s - first_row
                if j < CONV_WIDTH:
                    acc_even = acc_even + win * dw_ref[j:j + 1, cols]
                if j >= 1:
                    acc_odd = acc_odd + win * dw_ref[j - 1:j, cols]
            conv_ref[c, pl.ds(r0, CONV_BLOCK, stride=2), :] = acc_even
            conv_ref[c, pl.ds(r0 + 1, CONV_BLOCK, stride=2), :] = acc_odd
    acc = jnp.concatenate([conv_ref[c] for c in range(CONV_CHUNKS)], axis=1)
    mu = jnp.mean(acc, axis=-1, keepdims=True)
    cen = acc - mu
    var = jnp.mean(cen * cen, axis=-1, keepdims=True)
    v = _silu(cen * lax.rsqrt(var + EPS) * lng_ref[...] + lnb_ref[...])
    y = _dot(v, w2_ref[...]) + b2_ref[...]
    _mixer_tail(l, x_ref[...], y, mod_ref, r, gffn_ref, wr_ref, outs, scratch)


def _mid_odd(l, o, u, x, mod, dw, dwb, lng, lnb, w2, b2, gffn, wr):
    prev, nxt = _halo_specs(CONV_HALO, D)
    return pl.pallas_call(
        functools.partial(_mid_odd_kernel, l),
        grid=(NT,),
        in_specs=[_tile_spec(D), prev, nxt, _tile_spec(D), _MOD_SPEC, _layer_spec((CONV_WIDTH, D), o),
                  _layer_spec((1, D), o), _layer_spec((1, D), o), _layer_spec((1, D), o),
                  _layer_spec((D, D), o), _layer_spec((1, D), o), _full_spec((DEPTH, D)),
                  _layer_spec((2, D, LANES), l)],
        out_specs=_tail_out_specs(),
        out_shape=_TAIL_OUT_SHAPES,
        scratch_shapes=[pltpu.VMEM((CONV_CHUNKS, CONV_PAD_ROWS, LANES), F32),
                        pltpu.VMEM((CONV_CHUNKS, TM, LANES), F32)] + _tail_scratch(),
        compiler_params=_params(),
        name="mid_odd",
    )(u, u, u, x, mod, dw, dwb, lng, lnb, w2, b2, gffn, wr)


def _tile_tables_kernel(cnt_ref, te_c, tc_c, te_l, tc_l, nact_ref, first_ref):
    for p, (te_ref, tc_ref) in enumerate(((te_c, tc_c), (te_l, tc_l))):
        run = jnp.int32(0)
        ends = []
        for e in range(N_EXPERTS):
            first_ref[p * N_EXPERTS + e] = run
            run = run + (cnt_ref[p, e] + MOE_TM - 1) // MOE_TM
            ends.append(run)
        nact_ref[p] = run

        def expert_of(tile, ends=ends):
            e = jnp.int32(0)
            for k in range(N_EXPERTS):
                e = e + jnp.where(tile >= ends[k], 1, 0)
            return e

        last_e = expert_of(run - 1)

        def tile_body(j, c, te_ref=te_ref, tc_ref=tc_ref, p=p, run=run, expert_of=expert_of, last_e=last_e):
            e = jnp.minimum(expert_of(j), last_e)
            te_ref[j] = e
            tc_ref[j] = jnp.where(j < run, j - first_ref[p * N_EXPERTS + e], 0)
            return c

        lax.fori_loop(0, te_ref.shape[0], tile_body, 0)


def _tile_tables(cnt):
    smem = pl.BlockSpec(memory_space=pltpu.SMEM)
    shapes = [(MOE_TILES_CTX,)] * 2 + [(MOE_TILES_LAT,)] * 2 + [(2,)]
    return pl.pallas_call(
        _tile_tables_kernel,
        in_specs=[smem],
        out_specs=[smem] * len(shapes),
        out_shape=[jax.ShapeDtypeStruct(s, I32) for s in shapes],
        scratch_shapes=[pltpu.SMEM((2 * N_EXPERTS,), I32)],
        name="tile_tables",
    )(cnt)


def _moe_kernel(pop, te_ref, tc_ref, nact_ref, list_ref, h_ref, gate_ref, w1_ref, w3_ref, w2_ref,
                o_ref, xs_ref, ys_ref):
    j = pl.program_id(0)
    nact = nact_ref[pop]
    last_tile = te_ref.shape[0] - 1
    tile_rows = MOE_TM * TOK_ROWS

    def list_base(tile):
        return tc_ref[tile] * (MOE_TM // LIST_SPAN) * LIST_COLS + te_ref[tile] * LIST_SPAN

    def list_offset(r):
        return (r // LIST_SPAN) * LIST_COLS + r % LIST_SPAN

    def tok_rows(i):
        return pl.ds(pl.multiple_of(list_ref[i], TOK_ROWS), TOK_ROWS)

    def buf_rows(buf, r):
        return pl.ds(pl.multiple_of(buf * tile_rows + r * TOK_ROWS, TOK_ROWS), TOK_ROWS)

    def gather(tile, buf):
        base = list_base(tile)
        for r in range(MOE_TM):
            xs_ref[buf_rows(buf, r), :] = h_ref[tok_rows(base + list_offset(r)), :]

    @pl.when(j == 0)
    def _():
        o_ref[...] = jnp.zeros(o_ref.shape, F32)
        ys_ref[...] = jnp.zeros(ys_ref.shape, F32)
        gather(0, 0)

    @pl.when(j <= nact)
    def _():
        cur = j & 1
        x = jnp.concatenate([xs_ref[pl.ds(cur * tile_rows + s, MOE_TM, stride=TOK_ROWS), :]
                             for s in range(TOK_ROWS)], axis=1).astype(BF16)
        gather(jnp.minimum(j + 1, last_tile), 1 - cur)

        prev_base = list_base(jnp.maximum(j - 1, 0))
        for r0 in range(0, MOE_TM, MOVE_BATCH):
            dst = [tok_rows(prev_base + list_offset(r0 + k)) for k in range(MOVE_BATCH)]
            vals = [o_ref[dst[k], :] + ys_ref[buf_rows(1 - cur, r0 + k), :] for k in range(MOVE_BATCH)]
            for k in reversed(range(MOVE_BATCH)):
                o_ref[dst[k], :] = vals[k]

        hg = jnp.dot(x, w1_ref[...].astype(BF16), preferred_element_type=F32)
        hu = jnp.dot(x, w3_ref[...].astype(BF16), preferred_element_type=F32)
        y = _dot(_silu(hg) * hu, w2_ref[...])

        tile = jnp.minimum(j, last_tile)
        first_row = pl.multiple_of(tc_ref[tile] * (MOE_TM // LIST_SPAN), MOE_TM // LIST_SPAN)
        spread = jnp.concatenate(
            [jnp.broadcast_to(gate_ref[pl.ds(first_row + k, 1), :], (LIST_SPAN, LIST_COLS))
             for k in range(MOE_TM // LIST_SPAN)], axis=0)
        want = te_ref[tile] * LIST_SPAN + (lax.broadcasted_iota(I32, (MOE_TM, 1), 0) & (LIST_SPAN - 1))
        hit = jnp.logical_and(lax.broadcasted_iota(I32, (MOE_TM, LIST_COLS), 1) == want, j < nact)
        y = y * jnp.sum(jnp.where(hit, spread, 0.0), axis=1, keepdims=True)
        for s in range(TOK_ROWS):
            ys_ref[pl.ds(cur * tile_rows + s, MOE_TM, stride=TOK_ROWS), :] = y[:, s * LANES:(s + 1) * LANES]


def _moe(l, pop, h_tt, te, tc, nact, rows, gate, w1, w3, w2):
    tp = (T_CTX, T_LAT)[pop]
    blk = (0, T_CTX // T_LAT)[pop]
    n_tiles = te.shape[0]
    wmap = lambda j, te, *_: (l, te[jnp.minimum(j, n_tiles - 1)], 0, 0)
    wspec = lambda shape: pl.BlockSpec((None, None) + shape, wmap)
    grid_spec = pltpu.PrefetchScalarGridSpec(
        num_scalar_prefetch=4,
        grid=(n_tiles + 1,),
        in_specs=[
            pl.BlockSpec((tp * TOK_ROWS, LANES), lambda j, *_: (blk, 0), pipeline_mode=pl.Buffered(1)),
            pl.BlockSpec((LIST_ROWS, LIST_COLS), lambda j, *_: (0, 0), pipeline_mode=pl.Buffered(1)),
            wspec((D, EXPERT_HIDDEN)), wspec((D, EXPERT_HIDDEN)), wspec((EXPERT_HIDDEN, D)),
        ],
        out_specs=pl.BlockSpec((tp * TOK_ROWS, LANES), lambda j, *_: (0, 0), pipeline_mode=pl.Buffered(1)),
        scratch_shapes=[pltpu.VMEM((2 * MOE_TM * TOK_ROWS, LANES), F32),
                        pltpu.VMEM((2 * MOE_TM * TOK_ROWS, LANES), F32)],
    )
    return pl.pallas_call(
        functools.partial(_moe_kernel, pop),
        grid_spec=grid_spec,
        out_shape=jax.ShapeDtypeStruct((tp * TOK_ROWS, LANES), F32),
        compiler_params=_params(),
        name="moe",
    )(te, tc, nact, rows, h_tt, gate, w1, w3, w2)


def _moe_all(l, h_tt, cnt, rows_c, gate_c, rows_l, gate_l, w1, w3, w2):
    te_c, tc_c, te_l, tc_l, nact = _tile_tables(cnt)
    out_c = _moe(l, 0, h_tt, te_c, tc_c, nact, rows_c.reshape(-1), gate_c, w1, w3, w2)
    out_l = _moe(l, 1, h_tt, te_l, tc_l, nact, rows_l.reshape(-1), gate_l, w1, w3, w2)
    return out_c, out_l


def _final_kernel(l, x1_ref, moec_ref, moel_ref, mod_ref, g_ref, oc_ref, ol_ref):
    i = pl.program_id(0)
    r = _mod_row(i)

    def body(pop):
        g2 = mod_ref[l, pl.ds(r, 1), 5 * D:6 * D]
        x = x1_ref[...] + g2 * _from_token_tiles((moec_ref, moel_ref)[pop])
        ms = jnp.mean(x * x, axis=-1, keepdims=True)
        (oc_ref, ol_ref)[pop][...] = x * lax.rsqrt(ms + EPS) * g_ref[...]

    _per_population(i, body)


def _final(l, x1, moe_c, moe_l, mod, g):
    return pl.pallas_call(
        functools.partial(_final_kernel, l),
        grid=(NT,),
        in_specs=[_tile_spec(D)] + _pop_specs(TM * TOK_ROWS, LANES) + [_MOD_SPEC, _full_spec((1, D))],
        out_specs=_pop_specs(TM, D),
        out_shape=[jax.ShapeDtypeStruct((T_CTX, D), F32), jax.ShapeDtypeStruct((T_LAT, D), F32)],
        compiler_params=_params(),
        name="final_norm",
    )(x1, moe_c, moe_l, mod, g)


def _rope_tables():
    t = jnp.arange(LAT_LEN)
    row = (t // GRID_W).astype(F32)
    col = (t % GRID_W).astype(F32)
    inv = ROPE_BASE ** (-jnp.arange(0, ROPE_AXIS_DIM, 2, dtype=F32) / ROPE_AXIS_DIM)

    def table(p):
        ang = p[:, None] * inv[None, :]
        ang = jnp.concatenate([ang, ang], axis=-1)
        return jnp.cos(ang), jnp.sin(ang)

    (cr, sr), (cc, sc) = table(row), table(col)
    cos = jnp.concatenate([cr, cc, cr, cc], axis=-1)
    sin = jnp.concatenate([sr, sc, sr, sc], axis=-1)
    return cos, sin


def _router_weights(w_grp, w_exp):
    we = jnp.transpose(w_exp, (0, 2, 1, 3)).reshape(DEPTH, D, N_GROUPS * EPG)
    pad = jnp.zeros((DEPTH, D, LANES - N_GROUPS - N_GROUPS * EPG), F32)
    wr = jnp.concatenate([w_grp, we, pad], axis=-1)
    hi = lax.bitcast_convert_type(lax.bitcast_convert_type(wr, jnp.uint32) & jnp.uint32(0xFFFF0000), F32)
    return jnp.stack([hi.astype(BF16), (wr - hi).astype(BF16)], axis=1)


def kernel(x_prompt, x_sample, cache_k, cache_v, c, c_ctx, w_ada, b_ada, norm_mix_g, norm_ffn_g, w_in_ab, pool_w, pool_scale, attn_sink, w_out_ab, conv_w1, conv_b1, conv_dw, conv_dw_b, conv_ln_g, conv_ln_b, conv_w2, conv_b2, router_grp, router_exp, moe_w1, moe_w3, moe_w2, final_g):
    xc = x_prompt.reshape(T_CTX, D)
    xl = x_sample.reshape(T_LAT, D)
    cond8 = jnp.concatenate([c_ctx[None, :], c, jnp.zeros((SUBLANES - 1 - N_LAT_SEQ, D), F32)], axis=0)
    mod = _modulation(cond8, w_ada, b_ada)
    cos_t, sin_t = _rope_tables()
    wr = _router_weights(router_grp, router_exp)
    n_even, n_odd = w_in_ab.shape[0], conv_w1.shape[0]
    past = cache_k.shape[2]

    a, q, k, v = _in_even(0, 0, xc, xl, mod, norm_mix_g, w_in_ab, cos_t, sin_t)
    yb_c = _attn_ctx(0, attn_sink, q, k, v)
    yb_l = _attn_lat(0, attn_sink, q, k, v, cache_k[:, 0].reshape(N_LAT_SEQ, past, KV_DIM),
                     cache_v[:, 0].reshape(N_LAT_SEQ, past, KV_DIM))
    x1, h2, *routing = _mid_even(0, 0, a, yb_c, yb_l, xc, xl, mod, pool_w,
                                 pool_scale.reshape(n_even, 1, POOL_DIM), w_out_ab, norm_ffn_g, wr)
    moe0 = _moe_all(0, h2, *routing, moe_w1, moe_w3, moe_w2)

    vec = lambda p: p.reshape(n_odd, 1, -1)
    x2, u = _in_odd(1, 0, x1, *moe0, mod, norm_mix_g, conv_w1, vec(conv_b1))
    x3, h2b, *routing = _mid_odd(1, 0, u, x2, mod, conv_dw, vec(conv_dw_b), vec(conv_ln_g),
                                 vec(conv_ln_b), conv_w2, vec(conv_b2), norm_ffn_g, wr)
    moe1 = _moe_all(1, h2b, *routing, moe_w1, moe_w3, moe_w2)
    y_c, y_l = _final(1, x3, *moe1, mod, final_g.reshape(1, D))

    y_prompt = y_c.reshape(N_CTX_SEQ, CTX_LEN, D)
    y_sample = y_l.reshape(N_LAT_SEQ, LAT_LEN, D)
    state_k = k[:T_CTX].reshape(N_CTX_SEQ, 1, CTX_LEN, N_KV_HEADS, HEAD_DIM)
    state_v = v[:T_CTX].reshape(N_CTX_SEQ, 1, CTX_LEN, N_KV_HEADS, HEAD_DIM)
    return (y_prompt, y_sample, state_k, state_v)
```

```python
import functools

import jax
import jax.numpy as jnp
from jax import lax
from jax.experimental import pallas as pl
from jax.experimental.pallas import tpu as pltpu

F32 = jnp.float32
BF16 = jnp.bfloat16
I32 = jnp.int32

D = 1024
N_CTX_SEQ = 16
CTX_LEN = 256
N_LAT_SEQ = 2
LAT_LEN = 1024
T_CTX = N_CTX_SEQ * CTX_LEN
T_LAT = N_LAT_SEQ * LAT_LEN
T_ALL = T_CTX + T_LAT
TM = 256
NT = T_ALL // TM
NT_CTX = T_CTX // TM
LAT_TILES_PER_SEQ = LAT_LEN // TM
GRID_W = 64
DEPTH = 2

POOL_WINDOWS = (2, 4, 8, 16)
POOL_GROUP_DIM = 128
POOL_DIM = 512
HEAD_DIM = 64
N_Q_HEADS = 8
N_KV_HEADS = 2
Q_PER_KV = 4
Q_DIM = 512
KV_DIM = 128
IN_AB = 1280
ATTN_WINDOW = 128
ATTN_BLOCK = 128
ATTN_SCALE = HEAD_DIM ** -0.5
ROPE_BASE = 10000.0
ROPE_AXIS_DIM = 32
CONV_WIDTH = 31
CONV_HALO = 16
POOL_HALO = 8
N_GROUPS = 4
EPG = 4
N_EXPERTS = 16
EXPERT_HIDDEN = 512
EPS = 1e-6
NEG_BIG = -1e30

SUBLANES = 8
LANES = 128
TOK_ROWS = D // LANES

MOE_TM = 256
MOE_TILES_CTX = 2 * T_CTX // MOE_TM + N_EXPERTS
MOE_TILES_LAT = 2 * T_LAT // MOE_TM + N_EXPERTS
W_FIRST_UNIT = 2
W_NEXT_UNIT = 4
MOVE_BATCH_LOG2 = 4
MOVE_BATCH = 1 << MOVE_BATCH_LOG2
VMEM_LIMIT = 56 * 1024 * 1024


def _silu(x):
    return x * (1.0 / (1.0 + jnp.exp(-x)))


def _mod_row(i):
    return jnp.where(i < NT_CTX, 0, 1 + (i - NT_CTX) // LAT_TILES_PER_SEQ)


def _seq_info(i):
    is_ctx = i < NT_CTX
    k = (i - NT_CTX) % LAT_TILES_PER_SEQ
    off = jnp.where(is_ctx, 0, k * TM)
    n = jnp.where(is_ctx, CTX_LEN, LAT_LEN)
    first = jnp.logical_or(is_ctx, k == 0)
    last = jnp.logical_or(is_ctx, k == LAT_TILES_PER_SEQ - 1)
    return off, n, first, last


def _rms_mod(x, g, scale, shift):
    ms = jnp.mean(x * x, axis=-1, keepdims=True)
    return (x * lax.rsqrt(ms + EPS) * g) * (1.0 + scale) + shift


def _dot(a, b):
    return jnp.dot(a.astype(BF16), b.astype(BF16), preferred_element_type=F32)


def _dot_nt(a, b):
    return lax.dot_general(a.astype(BF16), b.astype(BF16), (((1,), (1,)), ((), ())),
                           preferred_element_type=F32)


def _from_token_tiles(ref):
    return jnp.concatenate([ref[pl.ds(s, TM, stride=TOK_ROWS), :] for s in range(TOK_ROWS)], axis=1)


def _tile_spec(width):
    return pl.BlockSpec((TM, width), lambda i: (i, 0))


def _full_spec(shape):
    return pl.BlockSpec(shape, lambda i: (0,) * len(shape))


def _layer_spec(shape, l):
    return pl.BlockSpec((None,) + tuple(shape), lambda i: (l,) + (0,) * len(shape))


def _pop_specs(rows, width):
    return [pl.BlockSpec((rows, width), lambda i: (jnp.minimum(i, NT_CTX - 1), 0)),
            pl.BlockSpec((rows, width), lambda i: (jnp.maximum(i - NT_CTX, 0), 0))]


def _per_population(i, body):
    pl.when(i < NT_CTX)(functools.partial(body, 0))
    pl.when(i >= NT_CTX)(functools.partial(body, 1))
_MOD_SPEC = pl.BlockSpec((DEPTH, SUBLANES, 6 * D), lambda i: (0, 0, 0))


def _params():
    return pltpu.CompilerParams(vmem_limit_bytes=VMEM_LIMIT)


MOD_TN = 1536


def _mod_kernel(cond_ref, w_ref, b_ref, o_ref):
    s = _silu(cond_ref[...])
    o_ref[0] = _dot(s, w_ref[0]) + b_ref[0]


def _modulation(cond8, w_ada, b_ada):
    return pl.pallas_call(
        _mod_kernel,
        grid=(DEPTH, 6 * D // MOD_TN),
        in_specs=[
            pl.BlockSpec((SUBLANES, D), lambda l, n: (0, 0)),
            pl.BlockSpec((1, D, MOD_TN), lambda l, n: (l, 0, n)),
            pl.BlockSpec((1, 1, MOD_TN), lambda l, n: (l, 0, n)),
        ],
        out_specs=pl.BlockSpec((1, SUBLANES, MOD_TN), lambda l, n: (l, 0, n)),
        out_shape=jax.ShapeDtypeStruct((DEPTH, SUBLANES, 6 * D), F32),
        compiler_params=_params(),
        name="modulation",
    )(cond8, w_ada, b_ada.reshape(DEPTH, 1, 6 * D))


def _rope_chunk(xc, cos, sin):
    lane = lax.broadcasted_iota(I32, xc.shape, 1)
    first = (lane % ROPE_AXIS_DIM) < (ROPE_AXIS_DIM // 2)
    rot = jnp.where(first, -pltpu.roll(xc, LANES - ROPE_AXIS_DIM // 2, 1), pltpu.roll(xc, ROPE_AXIS_DIM // 2, 1))
    return xc * cos + rot * sin


def _in_even_kernel(l, xc_ref, xl_ref, mod_ref, g_ref, w_ref, cos_ref, sin_ref, a_ref, q_ref, k_ref, v_ref):
    i = pl.program_id(0)
    r = _mod_row(i)

    def body(pop):
        x = (xc_ref, xl_ref)[pop][...]
        shift = mod_ref[l, pl.ds(r, 1), 0:D]
        scale = mod_ref[l, pl.ds(r, 1), D:2 * D]
        h = _rms_mod(x, g_ref[l:l + 1, :], scale, shift)
        proj = _dot(h, w_ref[...])
        a_ref[...] = proj[:, :POOL_DIM]
        v_ref[...] = proj[:, POOL_DIM + Q_DIM + KV_DIM:]
        if pop == 0:
            q_ref[...] = proj[:, POOL_DIM:POOL_DIM + Q_DIM]
            k_ref[...] = proj[:, POOL_DIM + Q_DIM:POOL_DIM + Q_DIM + KV_DIM]
        else:
            off, _, _, _ = _seq_info(i)
            off = pl.multiple_of(off, TM)
            cos = cos_ref[pl.ds(off, TM), :]
            sin = sin_ref[pl.ds(off, TM), :]
            for c in range(Q_DIM // LANES):
                lo = POOL_DIM + c * LANES
                q_ref[:, c * LANES:(c + 1) * LANES] = _rope_chunk(proj[:, lo:lo + LANES], cos, sin)
            lo = POOL_DIM + Q_DIM
            k_ref[...] = _rope_chunk(proj[:, lo:lo + LANES], cos, sin)

    _per_population(i, body)


def _in_even(l, e, xc, xl, mod, g, w_in, cos_t, sin_t):
    return pl.pallas_call(
        functools.partial(_in_even_kernel, l),
        grid=(NT,),
        in_specs=_pop_specs(TM, D) + [_MOD_SPEC, _full_spec((DEPTH, D)), _layer_spec((D, IN_AB), e),
                  _full_spec((LAT_LEN, LANES)), _full_spec((LAT_LEN, LANES))],
        out_specs=[_tile_spec(POOL_DIM), _tile_spec(Q_DIM), _tile_spec(KV_DIM), _tile_spec(KV_DIM)],
        out_shape=[jax.ShapeDtypeStruct((T_ALL, POOL_DIM), F32), jax.ShapeDtypeStruct((T_ALL, Q_DIM), F32),
                   jax.ShapeDtypeStruct((T_ALL, KV_DIM), F32), jax.ShapeDtypeStruct((T_ALL, KV_DIM), F32)],
        compiler_params=_params(),
        name="in_even",
    )(xc, xl, mod, g, w_in, cos_t, sin_t)


def _attn_ctx_kernel(e, sink_ref, q_ref, k_ref, v_ref, o_ref):
    q = q_ref[...]
    k = k_ref[...]
    v = v_ref[...]
    for h in range(N_KV_HEADS):
        kh = k[:, h * HEAD_DIM:(h + 1) * HEAD_DIM]
        vh = v[:, h * HEAD_DIM:(h + 1) * HEAD_DIM]
        for g in range(Q_PER_KV):
            j = h * Q_PER_KV + g
            s = _dot_nt(q[:, j * HEAD_DIM:(j + 1) * HEAD_DIM], kh) * ATTN_SCALE
            sk = sink_ref[e, j]
            m = jnp.maximum(jnp.max(s, axis=-1, keepdims=True), sk)
            p = jnp.exp(s - m)
            denom = jnp.sum(p, axis=-1, keepdims=True) + jnp.exp(sk - m)
            o_ref[:, j * HEAD_DIM:(j + 1) * HEAD_DIM] = _dot(p, vh) / denom


def _attn_ctx(e, sink, q, k, v):
    tile = lambda w: pl.BlockSpec((TM, w), lambda b: (b, 0))
    return pl.pallas_call(
        functools.partial(_attn_ctx_kernel, e),
        grid=(N_CTX_SEQ,),
        in_specs=[pl.BlockSpec(memory_space=pltpu.SMEM), tile(Q_DIM), tile(KV_DIM), tile(KV_DIM)],
        out_specs=tile(Q_DIM),
        out_shape=jax.ShapeDtypeStruct((T_CTX, Q_DIM), F32),
        compiler_params=_params(),
        name="attn_ctx",
    )(sink, q, k, v)


LAT_BLOCKS = LAT_LEN // ATTN_BLOCK
BAND = 3 * ATTN_BLOCK


def _attn_lat_kernel(e, sink_ref, q_ref, k_ref, v_ref, ck_ref, cv_ref, o_ref):
    n = pl.program_id(1)
    start = jnp.clip((n - 1) * ATTN_BLOCK, 0, LAT_LEN - BAND)
    start = pl.multiple_of(start, ATTN_BLOCK)
    q = q_ref[...]
    kw = k_ref[pl.ds(start, BAND), :]
    vw = v_ref[pl.ds(start, BAND), :]
    ck = ck_ref[0]
    cv = cv_ref[0]
    qpos = n * ATTN_BLOCK + lax.broadcasted_iota(I32, (ATTN_BLOCK, BAND), 0)
    kpos = start + lax.broadcasted_iota(I32, (ATTN_BLOCK, BAND), 1)
    valid = jnp.abs(qpos - kpos) <= ATTN_WINDOW
    for h in range(N_KV_HEADS):
        hs = slice(h * HEAD_DIM, (h + 1) * HEAD_DIM)
        for g in range(Q_PER_KV):
            j = h * Q_PER_KV + g
            qj = q[:, j * HEAD_DIM:(j + 1) * HEAD_DIM]
            s_loc = jnp.where(valid, _dot_nt(qj, kw[:, hs]) * ATTN_SCALE, NEG_BIG)
            s_ctx = _dot_nt(qj, ck[:, hs]) * ATTN_SCALE
            sk = sink_ref[e, j]
            m = jnp.maximum(jnp.maximum(jnp.max(s_loc, axis=-1, keepdims=True),
                                        jnp.max(s_ctx, axis=-1, keepdims=True)), sk)
            p_loc = jnp.exp(s_loc - m)
            p_ctx = jnp.exp(s_ctx - m)
            denom = (jnp.sum(p_loc, axis=-1, keepdims=True) + jnp.sum(p_ctx, axis=-1, keepdims=True)
                     + jnp.exp(sk - m))
            o = _dot(p_loc, vw[:, hs]) + _dot(p_ctx, cv[:, hs])
            o_ref[:, j * HEAD_DIM:(j + 1) * HEAD_DIM] = o / denom


def _attn_lat(e, sink, q, k, v, ck, cv):
    past = ck.shape[1]
    q_blk0 = T_CTX // ATTN_BLOCK
    kv_blk0 = T_CTX // LAT_LEN
    q_spec = pl.BlockSpec((ATTN_BLOCK, Q_DIM), lambda b, n: (q_blk0 + b * LAT_BLOCKS + n, 0))
    return pl.pallas_call(
        functools.partial(_attn_lat_kernel, e),
        grid=(N_LAT_SEQ, LAT_BLOCKS),
        in_specs=[pl.BlockSpec(memory_space=pltpu.SMEM),
                  q_spec,
                  pl.BlockSpec((LAT_LEN, KV_DIM), lambda b, n: (kv_blk0 + b, 0)),
                  pl.BlockSpec((LAT_LEN, KV_DIM), lambda b, n: (kv_blk0 + b, 0)),
                  pl.BlockSpec((1, past, KV_DIM), lambda b, n: (b, 0, 0)),
                  pl.BlockSpec((1, past, KV_DIM), lambda b, n: (b, 0, 0))],
        out_specs=pl.BlockSpec((ATTN_BLOCK, Q_DIM), lambda b, n: (b * LAT_BLOCKS + n, 0)),
        out_shape=jax.ShapeDtypeStruct((T_LAT, Q_DIM), F32),
        compiler_params=_params(),
        name="attn_lat",
    )(sink, q, k, v, ck, cv)


def _router(h2, wr_ref):
    hi = h2.astype(BF16)
    lo = (h2 - hi.astype(F32)).astype(BF16)
    whi = wr_ref[0]
    wlo = wr_ref[1]
    dot = functools.partial(jnp.dot, preferred_element_type=F32)
    logits = dot(hi, whi) + (dot(lo, whi) + dot(hi, wlo))
    lane = lax.broadcasted_iota(I32, logits.shape, 1).astype(F32)
    neg = jnp.float32(-jnp.inf)
    big = jnp.float32(1e9)
    is_grp = lane < N_GROUPS
    gl = jnp.where(is_grp, logits, neg)
    gmax = jnp.max(gl, axis=-1, keepdims=True)
    gsum = jnp.sum(jnp.where(is_grp, jnp.exp(logits - gmax), 0.0), axis=-1, keepdims=True)
    g_w = 1.0 / gsum
    g_idx = jnp.min(jnp.where(gl == gmax, lane, big), axis=-1, keepdims=True)
    base = N_GROUPS + EPG * g_idx
    in_grp = jnp.logical_and(lane >= base, lane < base + EPG)
    el = jnp.where(in_grp, logits, neg)
    t1 = jnp.max(el, axis=-1, keepdims=True)
    i1 = jnp.min(jnp.where(el == t1, lane, big), axis=-1, keepdims=True)
    el2 = jnp.where(lane == i1, neg, el)
    t2 = jnp.max(el2, axis=-1, keepdims=True)
    i2 = jnp.min(jnp.where(el2 == t2, lane, big), axis=-1, keepdims=True)
    d = jnp.exp(t2 - t1)
    w1 = g_w / (1.0 + d)
    w2 = g_w * d / (1.0 + d)
    return i1 - N_GROUPS, i2 - N_GROUPS, w1, w2


LIST_SPAN_LOG2 = 5
LIST_SPAN = 1 << LIST_SPAN_LOG2
LIST_ROWS = T_CTX // LIST_SPAN
LIST_COLS = N_EXPERTS * LIST_SPAN


N_TAIL_OUT = 7
N_TAIL_SCRATCH = 3


def _mixer_tail(l, x, y, mod_ref, r, gffn_ref, wr_ref, outs, scratch):
    x1_ref, h2_ref, cnt_ref, lc_ref, gc_ref, ll_ref, gl_ref = outs
    run_ref, acc_ref, gacc_ref = scratch
    i = pl.program_id(0)
    g1 = mod_ref[l, pl.ds(r, 1), 2 * D:3 * D]
    shift2 = mod_ref[l, pl.ds(r, 1), 3 * D:4 * D]
    scale2 = mod_ref[l, pl.ds(r, 1), 4 * D:5 * D]
    x1 = x + g1 * y
    x1_ref[...] = x1
    h2 = _rms_mod(x1, gffn_ref[l:l + 1, :], scale2, shift2)
    for s in range(TOK_ROWS):
        h2_ref[pl.ds(s, TM, stride=TOK_ROWS), :] = h2[:, s * LANES:(s + 1) * LANES]
    e1, e2, w1, w2 = _router(h2, wr_ref)

    @pl.when(jnp.logical_or(i == 0, i == NT_CTX))
    def _():
        run_ref[...] = jnp.zeros(run_ref.shape, F32)
        acc_ref[...] = jnp.zeros(acc_ref.shape, F32)
        gacc_ref[...] = jnp.zeros(gacc_ref.shape, F32)

    @pl.when(i == 0)
    def _():
        cnt_ref[...] = jnp.zeros(cnt_ref.shape, I32)

    lane_i = lax.broadcasted_iota(I32, (TM, LANES), 1)
    lane = lane_i.astype(F32)
    member = jnp.where(jnp.logical_or(lane == e1, lane == e2), 1.0, 0.0)
    tri = jnp.where(lax.broadcasted_iota(I32, (TM, TM), 0) >= lax.broadcasted_iota(I32, (TM, TM), 1), 1.0, 0.0)
    csum = jnp.dot(tri.astype(BF16), member.astype(BF16), preferred_element_type=F32)
    run = run_ref[0:1, :]
    before = csum - member + run
    r1 = jnp.sum(jnp.where(lane == e1, before, 0.0), axis=-1, keepdims=True)
    r2 = jnp.sum(jnp.where(lane == e2, before, 0.0), axis=-1, keepdims=True)
    run_new = run + csum[TM - 1:TM, :]
    run_ref[0:1, :] = run_new

    col_i = lax.broadcasted_iota(I32, (TM, LIST_COLS), 1)
    rows_oh, cols_oh = [], []
    for e, rank in ((e1, r1), (e2, r2)):
        rank_i = rank.astype(I32)
        rows_oh.append(jnp.where(lane_i == rank_i // LIST_SPAN, 1.0, 0.0))
        cols_oh.append(jnp.where(col_i == e.astype(I32) * LIST_SPAN + rank_i % LIST_SPAN, 1.0, 0.0).astype(BF16))
    rows_t = jnp.concatenate(rows_oh, axis=0).T
    cols = jnp.concatenate(cols_oh, axis=0)
    tok = (lax.broadcasted_iota(I32, (1, 2 * TM), 1) % TM).astype(F32)
    dot = functools.partial(jnp.dot, preferred_element_type=F32)
    hits = dot(rows_t.astype(BF16), cols)
    tok_sum = dot((rows_t * tok).astype(BF16), cols)
    tile_in_pop = jnp.where(i < NT_CTX, i, i - NT_CTX).astype(F32)
    acc = acc_ref[...] + TOK_ROWS * (tok_sum + TM * tile_in_pop * hits)
    acc_ref[...] = acc

    rec = jnp.where(lane_i == 0, w1, jnp.where(lane_i == 1, w2, 0.0)).T
    rest = jnp.concatenate([rec[0:1, :], rec[1:2, :]], axis=1)
    gacc = gacc_ref[...]
    for _ in range(3):
        piece = rest.astype(BF16)
        gacc = gacc + dot((rows_t * piece.astype(F32)).astype(BF16), cols)
        rest = rest - piece.astype(F32)
    gacc_ref[...] = gacc

    @pl.when(i == NT_CTX - 1)
    def _():
        cnt_ref[0:1, :] = run_new.astype(I32)
        lc_ref[...] = acc.astype(I32)
        gc_ref[...] = gacc

    @pl.when(i == NT - 1)
    def _():
        cnt_ref[1:2, :] = run_new.astype(I32)
        ll_ref[...] = acc.astype(I32)
        gl_ref[...] = gacc


_TAIL_OUT_SHAPES = [jax.ShapeDtypeStruct((T_ALL, D), F32),
                    jax.ShapeDtypeStruct((T_ALL * TOK_ROWS, LANES), F32),
                    jax.ShapeDtypeStruct((SUBLANES, LANES), I32)] + [
                        jax.ShapeDtypeStruct((LIST_ROWS, LIST_COLS), dt) for dt in (I32, F32, I32, F32)]


def _tail_scratch():
    return [pltpu.VMEM((SUBLANES, LANES), F32), pltpu.VMEM((LIST_ROWS, LIST_COLS), F32),
            pltpu.VMEM((LIST_ROWS, LIST_COLS), F32)]


def _tail_out_specs():
    return [_tile_spec(D),
            pl.BlockSpec((TM * TOK_ROWS, LANES), lambda i: (i, 0)),
            _full_spec((SUBLANES, LANES))] + [_full_spec((LIST_ROWS, LIST_COLS)) for _ in range(4)]


def _halo_specs(halo, width):
    per = TM // halo
    last = T_ALL // halo - 1
    prev = pl.BlockSpec((halo, width), lambda i: (jnp.maximum(i * per - 1, 0), 0))
    nxt = pl.BlockSpec((halo, width), lambda i: (jnp.minimum((i + 1) * per, last), 0))
    return prev, nxt


def _mid_even_kernel(l, a_ref, ap_ref, an_ref, ybc_ref, ybl_ref, xc_ref, xl_ref, mod_ref, wp_ref, ps_ref, wo_ref,
                     gffn_ref, wr_ref, *refs):
    outs, (pad_ref,), scratch = refs[:N_TAIL_OUT], refs[N_TAIL_OUT:-N_TAIL_SCRATCH], refs[-N_TAIL_SCRATCH:]
    i = pl.program_id(0)
    r = _mod_row(i)
    off, n, first, last = _seq_info(i)
    a = a_ref[...]
    pad_ref[0:POOL_HALO, :] = jnp.where(first, 0.0, ap_ref[...])
    pad_ref[POOL_HALO:POOL_HALO + TM, :] = a
    pad_ref[POOL_HALO + TM:, :] = jnp.where(last, 0.0, an_ref[...])
    t = off + lax.broadcasted_iota(I32, (TM, 1), 0)
    mixed = []
    for g, w in enumerate(POOL_WINDOWS):
        lo = w // 2
        hi = w - lo - 1
        cols = slice(g * POOL_GROUP_DIM, (g + 1) * POOL_GROUP_DIM)
        total = pad_ref[POOL_HALO - lo:POOL_HALO - lo + TM, cols]
        for j in range(-lo + 1, hi + 1):
            total = total + pad_ref[POOL_HALO + j:POOL_HALO + j + TM, cols]
        count = (jnp.minimum(t + hi, n - 1) - jnp.maximum(t - lo, 0) + 1).astype(F32)
        pooled = total / count - a[:, cols]
        mixed.append(_dot(pooled, wp_ref[g]))
    ya = jnp.concatenate(mixed, axis=1) * ps_ref[...]
    y_pool = _dot(ya, wo_ref[0:POOL_DIM, :])

    def body(pop):
        y = y_pool + _dot((ybc_ref, ybl_ref)[pop][...], wo_ref[POOL_DIM:, :])
        _mixer_tail(l, (xc_ref, xl_ref)[pop][...], y, mod_ref, r, gffn_ref, wr_ref, outs, scratch)

    _per_population(i, body)


def _mid_even(l, e, a, yb_c, yb_l, xc, xl, mod, w_pool, pool_scale, w_out, gffn, wr):
    prev, nxt = _halo_specs(POOL_HALO, POOL_DIM)
    return pl.pallas_call(
        functools.partial(_mid_even_kernel, l),
        grid=(NT,),
        in_specs=[_tile_spec(POOL_DIM), prev, nxt] + _pop_specs(TM, Q_DIM) + _pop_specs(TM, D) + [
                  _MOD_SPEC, _layer_spec((len(POOL_WINDOWS), POOL_GROUP_DIM, POOL_GROUP_DIM), e),
                  _layer_spec((1, POOL_DIM), e), _layer_spec((D, D), e), _full_spec((DEPTH, D)),
                  _layer_spec((2, D, LANES), l)],
        out_specs=_tail_out_specs(),
        out_shape=_TAIL_OUT_SHAPES,
        scratch_shapes=[pltpu.VMEM((TM + 2 * POOL_HALO, POOL_DIM), F32)] + _tail_scratch(),
        compiler_params=_params(),
        name="mid_even",
    )(a, a, a, yb_c, yb_l, xc, xl, mod, w_pool, pool_scale, w_out, gffn, wr)


def _in_odd_kernel(l, x1_ref, moec_ref, moel_ref, mod_ref, g_ref, w1_ref, b1_ref, x2_ref, u_ref):
    i = pl.program_id(0)
    r = _mod_row(i)

    def body(pop):
        g2 = mod_ref[l - 1, pl.ds(r, 1), 5 * D:6 * D]
        x2 = x1_ref[...] + g2 * _from_token_tiles((moec_ref, moel_ref)[pop])
        x2_ref[...] = x2
        shift = mod_ref[l, pl.ds(r, 1), 0:D]
        scale = mod_ref[l, pl.ds(r, 1), D:2 * D]
        h = _rms_mod(x2, g_ref[l:l + 1, :], scale, shift)
        u = _dot(h, w1_ref[...]) + b1_ref[...]
        u_ref[...] = u[:, :D] * (1.0 / (1.0 + jnp.exp(-u[:, D:])))

    _per_population(i, body)


def _in_odd(l, o, x1, moe_c, moe_l, mod, g, w1, b1):
    return pl.pallas_call(
        functools.partial(_in_odd_kernel, l),
        grid=(NT,),
        in_specs=[_tile_spec(D)] + _pop_specs(TM * TOK_ROWS, LANES) + [
                  _MOD_SPEC, _full_spec((DEPTH, D)), _layer_spec((D, 2 * D), o), _layer_spec((1, 2 * D), o)],
        out_specs=[_tile_spec(D), _tile_spec(D)],
        out_shape=[jax.ShapeDtypeStruct((T_ALL, D), F32), jax.ShapeDtypeStruct((T_ALL, D), F32)],
        compiler_params=_params(),
        name="in_odd",
    )(x1, moe_c, moe_l, mod, g, w1, b1)


CONV_PAD_ROWS = TM + 2 * CONV_HALO
CONV_CHUNKS = D // LANES
CONV_BLOCK = 64


def _mid_odd_kernel(l, u_ref, up_ref, un_ref, x_ref, mod_ref, dw_ref, dwb_ref, lng_ref, lnb_ref, w2_ref, b2_ref,
                    gffn_ref, wr_ref, *refs):
    outs, scratch = refs[:N_TAIL_OUT], refs[-N_TAIL_SCRATCH:]
    pad_ref, conv_ref = refs[N_TAIL_OUT:-N_TAIL_SCRATCH]
    i = pl.program_id(0)
    r = _mod_row(i)
    _, _, first, last = _seq_info(i)
    u = u_ref[...]
    up = jnp.where(first, 0.0, up_ref[...])
    un = jnp.where(last, 0.0, un_ref[...])
    half = CONV_WIDTH // 2
    first_row = CONV_HALO - half
    for c in range(CONV_CHUNKS):
        cols = slice(c * LANES, (c + 1) * LANES)
        pad_ref[c, 0:CONV_HALO, :] = up[:, cols]
        pad_ref[c, CONV_HALO:CONV_HALO + TM, :] = u[:, cols]
        pad_ref[c, CONV_HALO + TM:, :] = un[:, cols]
        bias = jnp.broadcast_to(dwb_ref[:, cols], (CONV_BLOCK, LANES))
        for r0 in range(0, TM, 2 * CONV_BLOCK):
            acc_even, acc_odd = bias, bias
            for s in range(first_row, first_row + CONV_WIDTH + 1):
                win = pad_ref[c, pl.ds(r0 + s, CONV_BLOCK, stride=2), :]
                j = s - first_row
                if j < CONV_WIDTH:
                    acc_even = acc_even + win * dw_ref[j:j + 1, cols]
                if j >= 1:
                    acc_odd = acc_odd + win * dw_ref[j - 1:j, cols]
            conv_ref[c, pl.ds(r0, CONV_BLOCK, stride=2), :] = acc_even
            conv_ref[c, pl.ds(r0 + 1, CONV_BLOCK, stride=2), :] = acc_odd
    acc = jnp.concatenate([conv_ref[c] for c in range(CONV_CHUNKS)], axis=1)
    mu = jnp.mean(acc, axis=-1, keepdims=True)
    cen = acc - mu
    var = jnp.mean(cen * cen, axis=-1, keepdims=True)
    v = _silu(cen * lax.rsqrt(var + EPS) * lng_ref[...] + lnb_ref[...])
    y = _dot(v, w2_ref[...]) + b2_ref[...]
    _mixer_tail(l, x_ref[...], y, mod_ref, r, gffn_ref, wr_ref, outs, scratch)


def _mid_odd(l, o, u, x, mod, dw, dwb, lng, lnb, w2, b2, gffn, wr):
    prev, nxt = _halo_specs(CONV_HALO, D)
    return pl.pallas_call(
        functools.partial(_mid_odd_kernel, l),
        grid=(NT,),
        in_specs=[_tile_spec(D), prev, nxt, _tile_spec(D), _MOD_SPEC, _layer_spec((CONV_WIDTH, D), o),
                  _layer_spec((1, D), o), _layer_spec((1, D), o), _layer_spec((1, D), o),
                  _layer_spec((D, D), o), _layer_spec((1, D), o), _full_spec((DEPTH, D)),
                  _layer_spec((2, D, LANES), l)],
        out_specs=_tail_out_specs(),
        out_shape=_TAIL_OUT_SHAPES,
        scratch_shapes=[pltpu.VMEM((CONV_CHUNKS, CONV_PAD_ROWS, LANES), F32),
                        pltpu.VMEM((CONV_CHUNKS, TM, LANES), F32)] + _tail_scratch(),
        compiler_params=_params(),
        name="mid_odd",
    )(u, u, u, x, mod, dw, dwb, lng, lnb, w2, b2, gffn, wr)


def _tile_tables_kernel(cnt_ref, te_c, tc_c, tw_c, te_l, tc_l, tw_l, nact_ref, first_ref, wplan_ref):
    for p, (te_ref, tc_ref, tw_ref) in enumerate(((te_c, tc_c, tw_c), (te_l, tc_l, tw_l))):
        run = jnp.int32(0)
        ends = []
        for e in range(N_EXPERTS):
            first_ref[p * N_EXPERTS + e] = run
            run = run + (cnt_ref[p, e] + MOE_TM - 1) // MOE_TM
            ends.append(run)
        nact_ref[p] = run

        nxt = jnp.int32(0)
        plans = []
        for e in reversed(range(N_EXPERTS)):
            plans.append(nxt)
            nxt = jnp.where(cnt_ref[p, e] > 0, e + 1, nxt)
        plans.reverse()
        order = jnp.int32(0)
        for e in range(N_EXPERTS):
            wplan_ref[p * N_EXPERTS + e] = (order & 1) + W_NEXT_UNIT * plans[e]
            order = order + jnp.where(cnt_ref[p, e] > 0, 1, 0)

        def expert_of(tile, ends=ends):
            e = jnp.int32(0)
            for k in range(N_EXPERTS):
                e = e + jnp.where(tile >= ends[k], 1, 0)
            return e

        last_e = expert_of(run - 1)

        def tile_body(j, c, te_ref=te_ref, tc_ref=tc_ref, tw_ref=tw_ref, p=p, run=run, expert_of=expert_of,
                      last_e=last_e):
            e = jnp.minimum(expert_of(j), last_e)
            te_ref[j] = e
            chunk = jnp.where(j < run, j - first_ref[p * N_EXPERTS + e], 0)
            tc_ref[j] = chunk
            starts_expert = jnp.logical_and(j < run, chunk == 0)
            tw_ref[j] = wplan_ref[p * N_EXPERTS + e] + jnp.where(starts_expert, W_FIRST_UNIT, 0)
            return c

        lax.fori_loop(0, te_ref.shape[0], tile_body, 0)


def _tile_tables(cnt):
    smem = pl.BlockSpec(memory_space=pltpu.SMEM)
    shapes = [(MOE_TILES_CTX,)] * 3 + [(MOE_TILES_LAT,)] * 3 + [(2,)]
    return pl.pallas_call(
        _tile_tables_kernel,
        in_specs=[smem],
        out_specs=[smem] * len(shapes),
        out_shape=[jax.ShapeDtypeStruct(s, I32) for s in shapes],
        scratch_shapes=[pltpu.SMEM((2 * N_EXPERTS,), I32), pltpu.SMEM((2 * N_EXPERTS,), I32)],
        name="tile_tables",
    )(cnt)


def _moe_kernel(l, pop, te_ref, tc_ref, tw_ref, nact_ref, list_ref, h_ref, gate_ref, w1_hbm, w3_hbm, w2_hbm,
                o_ref, xs_ref, ys_ref, w1_buf, w3_buf, w2_buf, w_sem):
    j = pl.program_id(0)
    nact = nact_ref[pop]
    last_tile = te_ref.shape[0] - 1
    tile_rows = MOE_TM * TOK_ROWS

    def weight_copies(expert, buf):
        return [pltpu.make_async_copy(hbm.at[l, expert], vmem.at[buf], w_sem.at[buf, k])
                for k, (hbm, vmem) in enumerate(((w1_hbm, w1_buf), (w3_hbm, w3_buf), (w2_hbm, w2_buf)))]

    def list_base(tile):
        return tc_ref[tile] * (MOE_TM // LIST_SPAN) * LIST_COLS + te_ref[tile] * LIST_SPAN

    def list_offset(r):
        return (r // LIST_SPAN) * LIST_COLS + r % LIST_SPAN

    def tok_rows(i):
        return pl.ds(pl.multiple_of(list_ref[i], TOK_ROWS), TOK_ROWS)

    def buf_rows(buf, r):
        return pl.ds(pl.multiple_of(buf * tile_rows + r * TOK_ROWS, TOK_ROWS), TOK_ROWS)

    def gather(tile, buf):
        base = list_base(tile)
        for r in range(MOE_TM):
            xs_ref[buf_rows(buf, r), :] = h_ref[tok_rows(base + list_offset(r)), :]

    tile = jnp.minimum(j, last_tile)
    plan = tw_ref[tile]
    wbuf = plan & 1
    starts_expert = jnp.logical_and(plan & W_FIRST_UNIT != 0, j <= last_tile)
    next_expert = (plan >> 2) - 1

    @pl.when(j == 0)
    def _():
        for cp in weight_copies(te_ref[0], 0):
            cp.start()
        o_ref[...] = jnp.zeros(o_ref.shape, F32)
        ys_ref[...] = jnp.zeros(ys_ref.shape, F32)
        gather(0, 0)

    @pl.when(starts_expert)
    def _():
        for cp in weight_copies(te_ref[tile], wbuf):
            cp.wait()

        @pl.when(next_expert >= 0)
        def _():
            for cp in weight_copies(next_expert, 1 - wbuf):
                cp.start()

    @pl.when(j <= nact)
    def _():
        cur = j & 1
        w1_ref, w3_ref, w2_ref = w1_buf.at[wbuf], w3_buf.at[wbuf], w2_buf.at[wbuf]
        x = jnp.concatenate([xs_ref[pl.ds(cur * tile_rows + s, MOE_TM, stride=TOK_ROWS), :]
                             for s in range(TOK_ROWS)], axis=1).astype(BF16)
        gather(jnp.minimum(j + 1, last_tile), 1 - cur)

        prev_base = list_base(jnp.maximum(j - 1, 0))
        for r0 in range(0, MOE_TM, MOVE_BATCH):
            dst = [tok_rows(prev_base + list_offset(r0 + k)) for k in range(MOVE_BATCH)]
            vals = [o_ref[dst[k], :] + ys_ref[buf_rows(1 - cur, r0 + k), :] for k in range(MOVE_BATCH)]
            for k in reversed(range(MOVE_BATCH)):
                o_ref[dst[k], :] = vals[k]

        hg = jnp.dot(x, w1_ref[...].astype(BF16), preferred_element_type=F32)
        hu = jnp.dot(x, w3_ref[...].astype(BF16), preferred_element_type=F32)
        y = _dot(_silu(hg) * hu, w2_ref[...])

        first_row = pl.multiple_of(tc_ref[tile] * (MOE_TM // LIST_SPAN), MOE_TM // LIST_SPAN)
        spread = jnp.concatenate(
            [jnp.broadcast_to(gate_ref[pl.ds(first_row + k, 1), :], (LIST_SPAN, LIST_COLS))
             for k in range(MOE_TM // LIST_SPAN)], axis=0)
        want = te_ref[tile] * LIST_SPAN + (lax.broadcasted_iota(I32, (MOE_TM, 1), 0) & (LIST_SPAN - 1))
        hit = jnp.logical_and(lax.broadcasted_iota(I32, (MOE_TM, LIST_COLS), 1) == want, j < nact)
        y = y * jnp.sum(jnp.where(hit, spread, 0.0), axis=1, keepdims=True)
        for s in range(TOK_ROWS):
            ys_ref[pl.ds(cur * tile_rows + s, MOE_TM, stride=TOK_ROWS), :] = y[:, s * LANES:(s + 1) * LANES]


def _moe(l, pop, h_tt, te, tc, tw, nact, rows, gate, w1, w3, w2):
    tp = (T_CTX, T_LAT)[pop]
    blk = (0, T_CTX // T_LAT)[pop]
    n_tiles = te.shape[0]
    hbm = pl.BlockSpec(memory_space=pl.ANY)
    grid_spec = pltpu.PrefetchScalarGridSpec(
        num_scalar_prefetch=5,
        grid=(n_tiles + 1,),
        in_specs=[
            pl.BlockSpec((tp * TOK_ROWS, LANES), lambda j, *_: (blk, 0), pipeline_mode=pl.Buffered(1)),
            pl.BlockSpec((LIST_ROWS, LIST_COLS), lambda j, *_: (0, 0), pipeline_mode=pl.Buffered(1)),
            hbm, hbm, hbm,
        ],
        out_specs=pl.BlockSpec((tp * TOK_ROWS, LANES), lambda j, *_: (0, 0), pipeline_mode=pl.Buffered(1)),
        scratch_shapes=[pltpu.VMEM((2 * MOE_TM * TOK_ROWS, LANES), F32),
                        pltpu.VMEM((2 * MOE_TM * TOK_ROWS, LANES), F32),
                        pltpu.VMEM((2, D, EXPERT_HIDDEN), F32),
                        pltpu.VMEM((2, D, EXPERT_HIDDEN), F32),
                        pltpu.VMEM((2, EXPERT_HIDDEN, D), F32),
                        pltpu.SemaphoreType.DMA((2, 3))],
    )
    return pl.pallas_call(
        functools.partial(_moe_kernel, l, pop),
        grid_spec=grid_spec,
        out_shape=jax.ShapeDtypeStruct((tp * TOK_ROWS, LANES), F32),
        compiler_params=_params(),
        name="moe",
    )(te, tc, tw, nact, rows, h_tt, gate, w1, w3, w2)


def _moe_all(l, h_tt, cnt, rows_c, gate_c, rows_l, gate_l, w1, w3, w2):
    te_c, tc_c, tw_c, te_l, tc_l, tw_l, nact = _tile_tables(cnt)
    out_c = _moe(l, 0, h_tt, te_c, tc_c, tw_c, nact, rows_c.reshape(-1), gate_c, w1, w3, w2)
    out_l = _moe(l, 1, h_tt, te_l, tc_l, tw_l, nact, rows_l.reshape(-1), gate_l, w1, w3, w2)
    return out_c, out_l


def _final_kernel(l, x1_ref, moec_ref, moel_ref, mod_ref, g_ref, oc_ref, ol_ref):
    i = pl.program_id(0)
    r = _mod_row(i)

    def body(pop):
        g2 = mod_ref[l, pl.ds(r, 1), 5 * D:6 * D]
        x = x1_ref[...] + g2 * _from_token_tiles((moec_ref, moel_ref)[pop])
        ms = jnp.mean(x * x, axis=-1, keepdims=True)
        (oc_ref, ol_ref)[pop][...] = x * lax.rsqrt(ms + EPS) * g_ref[...]

    _per_population(i, body)


def _final(l, x1, moe_c, moe_l, mod, g):
    return pl.pallas_call(
        functools.partial(_final_kernel, l),
        grid=(NT,),
        in_specs=[_tile_spec(D)] + _pop_specs(TM * TOK_ROWS, LANES) + [_MOD_SPEC, _full_spec((1, D))],
        out_specs=_pop_specs(TM, D),
        out_shape=[jax.ShapeDtypeStruct((T_CTX, D), F32), jax.ShapeDtypeStruct((T_LAT, D), F32)],
        compiler_params=_params(),
        name="final_norm",
    )(x1, moe_c, moe_l, mod, g)


def _rope_tables():
    t = jnp.arange(LAT_LEN)
    row = (t // GRID_W).astype(F32)
    col = (t % GRID_W).astype(F32)
    inv = ROPE_BASE ** (-jnp.arange(0, ROPE_AXIS_DIM, 2, dtype=F32) / ROPE_AXIS_DIM)

    def table(p):
        ang = p[:, None] * inv[None, :]
        ang = jnp.concatenate([ang, ang], axis=-1)
        return jnp.cos(ang), jnp.sin(ang)

    (cr, sr), (cc, sc) = table(row), table(col)
    cos = jnp.concatenate([cr, cc, cr, cc], axis=-1)
    sin = jnp.concatenate([sr, sc, sr, sc], axis=-1)
    return cos, sin


def _router_weights(w_grp, w_exp):
    we = jnp.transpose(w_exp, (0, 2, 1, 3)).reshape(DEPTH, D, N_GROUPS * EPG)
    pad = jnp.zeros((DEPTH, D, LANES - N_GROUPS - N_GROUPS * EPG), F32)
    wr = jnp.concatenate([w_grp, we, pad], axis=-1)
    hi = lax.bitcast_convert_type(lax.bitcast_convert_type(wr, jnp.uint32) & jnp.uint32(0xFFFF0000), F32)
    return jnp.stack([hi.astype(BF16), (wr - hi).astype(BF16)], axis=1)


def kernel(x_prompt, x_sample, cache_k, cache_v, c, c_ctx, w_ada, b_ada, norm_mix_g, norm_ffn_g, w_in_ab, pool_w, pool_scale, attn_sink, w_out_ab, conv_w1, conv_b1, conv_dw, conv_dw_b, conv_ln_g, conv_ln_b, conv_w2, conv_b2, router_grp, router_exp, moe_w1, moe_w3, moe_w2, final_g):
    xc = x_prompt.reshape(T_CTX, D)
    xl = x_sample.reshape(T_LAT, D)
    cond8 = jnp.concatenate([c_ctx[None, :], c, jnp.zeros((SUBLANES - 1 - N_LAT_SEQ, D), F32)], axis=0)
    mod = _modulation(cond8, w_ada, b_ada)
    cos_t, sin_t = _rope_tables()
    wr = _router_weights(router_grp, router_exp)
    n_even, n_odd = w_in_ab.shape[0], conv_w1.shape[0]
    past = cache_k.shape[2]

    a, q, k, v = _in_even(0, 0, xc, xl, mod, norm_mix_g, w_in_ab, cos_t, sin_t)
    yb_c = _attn_ctx(0, attn_sink, q, k, v)
    yb_l = _attn_lat(0, attn_sink, q, k, v, cache_k[:, 0].reshape(N_LAT_SEQ, past, KV_DIM),
                     cache_v[:, 0].reshape(N_LAT_SEQ, past, KV_DIM))
    x1, h2, *routing = _mid_even(0, 0, a, yb_c, yb_l, xc, xl, mod, pool_w,
                                 pool_scale.reshape(n_even, 1, POOL_DIM), w_out_ab, norm_ffn_g, wr)
    moe0 = _moe_all(0, h2, *routing, moe_w1, moe_w3, moe_w2)

    vec = lambda p: p.reshape(n_odd, 1, -1)
    x2, u = _in_odd(1, 0, x1, *moe0, mod, norm_mix_g, conv_w1, vec(conv_b1))
    x3, h2b, *routing = _mid_odd(1, 0, u, x2, mod, conv_dw, vec(conv_dw_b), vec(conv_ln_g),
                                 vec(conv_ln_b), conv_w2, vec(conv_b2), norm_ffn_g, wr)
    moe1 = _moe_all(1, h2b, *routing, moe_w1, moe_w3, moe_w2)
    y_c, y_l = _final(1, x3, *moe1, mod, final_g.reshape(1, D))

    y_prompt = y_c.reshape(N_CTX_SEQ, CTX_LEN, D)
    y_sample = y_l.reshape(N_LAT_SEQ, LAT_LEN, D)
    state_k = k[:T_CTX].reshape(N_CTX_SEQ, 1, CTX_LEN, N_KV_HEADS, HEAD_DIM)
    state_v = v[:T_CTX].reshape(N_CTX_SEQ, 1, CTX_LEN, N_KV_HEADS, HEAD_DIM)
    return (y_prompt, y_sample, state_k, state_v)
```

```python
import functools

import jax
import jax.numpy as jnp
from jax import lax
from jax.experimental import pallas as pl
from jax.experimental.pallas import tpu as pltpu

F32 = jnp.float32
BF16 = jnp.bfloat16
I32 = jnp.int32

D = 1024
N_CTX_SEQ = 16
CTX_LEN = 256
N_LAT_SEQ = 2
LAT_LEN = 1024
T_CTX = N_CTX_SEQ * CTX_LEN
T_LAT = N_LAT_SEQ * LAT_LEN
T_ALL = T_CTX + T_LAT
TM = 256
NT = T_ALL // TM
NT_CTX = T_CTX // TM
LAT_TILES_PER_SEQ = LAT_LEN // TM
GRID_W = 64
DEPTH = 2

POOL_WINDOWS = (2, 4, 8, 16)
POOL_GROUP_DIM = 128
POOL_DIM = 512
HEAD_DIM = 64
N_Q_HEADS = 8
N_KV_HEADS = 2
Q_PER_KV = 4
Q_DIM = 512
KV_DIM = 128
IN_AB = 1280
ATTN_WINDOW = 128
ATTN_BLOCK = 128
ATTN_SCALE = HEAD_DIM ** -0.5
ROPE_BASE = 10000.0
ROPE_AXIS_DIM = 32
CONV_WIDTH = 31
CONV_HALO = 16
POOL_HALO = 8
N_GROUPS = 4
EPG = 4
N_EXPERTS = 16
EXPERT_HIDDEN = 512
EPS = 1e-6
NEG_BIG = -1e30

SUBLANES = 8
LANES = 128
TOK_ROWS = D // LANES

MOE_TM = 256
MOE_TILES_CTX = 2 * T_CTX // MOE_TM + N_EXPERTS
MOE_TILES_LAT = 2 * T_LAT // MOE_TM + N_EXPERTS
W_FIRST_UNIT = 2
W_NEXT_UNIT = 4
MOVE_BATCH_LOG2 = 4
MOVE_BATCH = 1 << MOVE_BATCH_LOG2
VMEM_LIMIT = 56 * 1024 * 1024


def _silu(x):
    return x * (1.0 / (1.0 + jnp.exp(-x)))


def _mod_row(i):
    return jnp.where(i < NT_CTX, 0, 1 + (i - NT_CTX) // LAT_TILES_PER_SEQ)


def _seq_info(i):
    is_ctx = i < NT_CTX
    k = (i - NT_CTX) % LAT_TILES_PER_SEQ
    off = jnp.where(is_ctx, 0, k * TM)
    n = jnp.where(is_ctx, CTX_LEN, LAT_LEN)
    first = jnp.logical_or(is_ctx, k == 0)
    last = jnp.logical_or(is_ctx, k == LAT_TILES_PER_SEQ - 1)
    return off, n, first, last


def _rms_mod(x, g, scale, shift):
    ms = jnp.mean(x * x, axis=-1, keepdims=True)
    return (x * lax.rsqrt(ms + EPS) * g) * (1.0 + scale) + shift


def _dot(a, b):
    return jnp.dot(a.astype(BF16), b.astype(BF16), preferred_element_type=F32)


def _dot_nt(a, b):
    return lax.dot_general(a.astype(BF16), b.astype(BF16), (((1,), (1,)), ((), ())),
                           preferred_element_type=F32)


def _from_token_tiles(ref):
    return jnp.concatenate([ref[pl.ds(s, TM, stride=TOK_ROWS), :] for s in range(TOK_ROWS)], axis=1)


def _tile_spec(width):
    return pl.BlockSpec((TM, width), lambda i: (i, 0))


def _full_spec(shape):
    return pl.BlockSpec(shape, lambda i: (0,) * len(shape))


def _layer_spec(shape, l):
    return pl.BlockSpec((None,) + tuple(shape), lambda i: (l,) + (0,) * len(shape))


def _pop_specs(rows, width):
    return [pl.BlockSpec((rows, width), lambda i: (jnp.minimum(i, NT_CTX - 1), 0)),
            pl.BlockSpec((rows, width), lambda i: (jnp.maximum(i - NT_CTX, 0), 0))]


def _per_population(i, body):
    pl.when(i < NT_CTX)(functools.partial(body, 0))
    pl.when(i >= NT_CTX)(functools.partial(body, 1))
_MOD_SPEC = pl.BlockSpec((DEPTH, SUBLANES, 6 * D), lambda i: (0, 0, 0))


def _params():
    return pltpu.CompilerParams(vmem_limit_bytes=VMEM_LIMIT)


MOD_TN = 1536


def _mod_kernel(cond_ref, w_ref, b_ref, o_ref):
    s = _silu(cond_ref[...])
    o_ref[0] = _dot(s, w_ref[0]) + b_ref[0]


def _modulation(cond8, w_ada, b_ada):
    return pl.pallas_call(
        _mod_kernel,
        grid=(DEPTH, 6 * D // MOD_TN),
        in_specs=[
            pl.BlockSpec((SUBLANES, D), lambda l, n: (0, 0)),
            pl.BlockSpec((1, D, MOD_TN), lambda l, n: (l, 0, n)),
            pl.BlockSpec((1, 1, MOD_TN), lambda l, n: (l, 0, n)),
        ],
        out_specs=pl.BlockSpec((1, SUBLANES, MOD_TN), lambda l, n: (l, 0, n)),
        out_shape=jax.ShapeDtypeStruct((DEPTH, SUBLANES, 6 * D), F32),
        compiler_params=_params(),
        name="modulation",
    )(cond8, w_ada, b_ada.reshape(DEPTH, 1, 6 * D))


def _rope_chunk(xc, cos, sin):
    lane = lax.broadcasted_iota(I32, xc.shape, 1)
    first = (lane % ROPE_AXIS_DIM) < (ROPE_AXIS_DIM // 2)
    rot = jnp.where(first, -pltpu.roll(xc, LANES - ROPE_AXIS_DIM // 2, 1), pltpu.roll(xc, ROPE_AXIS_DIM // 2, 1))
    return xc * cos + rot * sin


def _in_even_kernel(l, xc_ref, xl_ref, mod_ref, g_ref, w_ref, cos_ref, sin_ref, a_ref, q_ref, k_ref, v_ref):
    i = pl.program_id(0)
    r = _mod_row(i)

    def body(pop):
        x = (xc_ref, xl_ref)[pop][...]
        shift = mod_ref[l, pl.ds(r, 1), 0:D]
        scale = mod_ref[l, pl.ds(r, 1), D:2 * D]
        h = _rms_mod(x, g_ref[l:l + 1, :], scale, shift)
        proj = _dot(h, w_ref[...])
        a_ref[...] = proj[:, :POOL_DIM]
        v_ref[...] = proj[:, POOL_DIM + Q_DIM + KV_DIM:]
        if pop == 0:
            q_ref[...] = proj[:, POOL_DIM:POOL_DIM + Q_DIM]
            k_ref[...] = proj[:, POOL_DIM + Q_DIM:POOL_DIM + Q_DIM + KV_DIM]
        else:
            off, _, _, _ = _seq_info(i)
            off = pl.multiple_of(off, TM)
            cos = cos_ref[pl.ds(off, TM), :]
            sin = sin_ref[pl.ds(off, TM), :]
            for c in range(Q_DIM // LANES):
                lo = POOL_DIM + c * LANES
                q_ref[:, c * LANES:(c + 1) * LANES] = _rope_chunk(proj[:, lo:lo + LANES], cos, sin)
            lo = POOL_DIM + Q_DIM
            k_ref[...] = _rope_chunk(proj[:, lo:lo + LANES], cos, sin)

    _per_population(i, body)


def _in_even(l, e, xc, xl, mod, g, w_in, cos_t, sin_t):
    return pl.pallas_call(
        functools.partial(_in_even_kernel, l),
        grid=(NT,),
        in_specs=_pop_specs(TM, D) + [_MOD_SPEC, _full_spec((DEPTH, D)), _layer_spec((D, IN_AB), e),
                  _full_spec((LAT_LEN, LANES)), _full_spec((LAT_LEN, LANES))],
        out_specs=[_tile_spec(POOL_DIM), _tile_spec(Q_DIM), _tile_spec(KV_DIM), _tile_spec(KV_DIM)],
        out_shape=[jax.ShapeDtypeStruct((T_ALL, POOL_DIM), F32), jax.ShapeDtypeStruct((T_ALL, Q_DIM), F32),
                   jax.ShapeDtypeStruct((T_ALL, KV_DIM), F32), jax.ShapeDtypeStruct((T_ALL, KV_DIM), F32)],
        compiler_params=_params(),
        name="in_even",
    )(xc, xl, mod, g, w_in, cos_t, sin_t)


def _attn_ctx_kernel(e, sink_ref, q_ref, k_ref, v_ref, o_ref):
    q = q_ref[...]
    k = k_ref[...]
    v = v_ref[...]
    for h in range(N_KV_HEADS):
        kh = k[:, h * HEAD_DIM:(h + 1) * HEAD_DIM]
        vh = v[:, h * HEAD_DIM:(h + 1) * HEAD_DIM]
        for g in range(Q_PER_KV):
            j = h * Q_PER_KV + g
            s = _dot_nt(q[:, j * HEAD_DIM:(j + 1) * HEAD_DIM], kh) * ATTN_SCALE
            sk = sink_ref[e, j]
            m = jnp.maximum(jnp.max(s, axis=-1, keepdims=True), sk)
            p = jnp.exp(s - m)
            denom = jnp.sum(p, axis=-1, keepdims=True) + jnp.exp(sk - m)
            o_ref[:, j * HEAD_DIM:(j + 1) * HEAD_DIM] = _dot(p, vh) / denom


def _attn_ctx(e, sink, q, k, v):
    tile = lambda w: pl.BlockSpec((TM, w), lambda b: (b, 0))
    return pl.pallas_call(
        functools.partial(_attn_ctx_kernel, e),
        grid=(N_CTX_SEQ,),
        in_specs=[pl.BlockSpec(memory_space=pltpu.SMEM), tile(Q_DIM), tile(KV_DIM), tile(KV_DIM)],
        out_specs=tile(Q_DIM),
        out_shape=jax.ShapeDtypeStruct((T_CTX, Q_DIM), F32),
        compiler_params=_params(),
        name="attn_ctx",
    )(sink, q, k, v)


LAT_BLOCKS = LAT_LEN // ATTN_BLOCK
BAND = 3 * ATTN_BLOCK


def _attn_lat_kernel(e, sink_ref, q_ref, k_ref, v_ref, ck_ref, cv_ref, o_ref):
    n = pl.program_id(1)
    start = jnp.clip((n - 1) * ATTN_BLOCK, 0, LAT_LEN - BAND)
    start = pl.multiple_of(start, ATTN_BLOCK)
    q = q_ref[...]
    kw = k_ref[pl.ds(start, BAND), :]
    vw = v_ref[pl.ds(start, BAND), :]
    ck = ck_ref[0]
    cv = cv_ref[0]
    qpos = n * ATTN_BLOCK + lax.broadcasted_iota(I32, (ATTN_BLOCK, BAND), 0)
    kpos = start + lax.broadcasted_iota(I32, (ATTN_BLOCK, BAND), 1)
    valid = jnp.abs(qpos - kpos) <= ATTN_WINDOW
    for h in range(N_KV_HEADS):
        hs = slice(h * HEAD_DIM, (h + 1) * HEAD_DIM)
        for g in range(Q_PER_KV):
            j = h * Q_PER_KV + g
            qj = q[:, j * HEAD_DIM:(j + 1) * HEAD_DIM]
            s_loc = jnp.where(valid, _dot_nt(qj, kw[:, hs]) * ATTN_SCALE, NEG_BIG)
            s_ctx = _dot_nt(qj, ck[:, hs]) * ATTN_SCALE
            sk = sink_ref[e, j]
            m = jnp.maximum(jnp.maximum(jnp.max(s_loc, axis=-1, keepdims=True),
                                        jnp.max(s_ctx, axis=-1, keepdims=True)), sk)
            p_loc = jnp.exp(s_loc - m)
            p_ctx = jnp.exp(s_ctx - m)
            denom = (jnp.sum(p_loc, axis=-1, keepdims=True) + jnp.sum(p_ctx, axis=-1, keepdims=True)
                     + jnp.exp(sk - m))
            o = _dot(p_loc, vw[:, hs]) + _dot(p_ctx, cv[:, hs])
            o_ref[:, j * HEAD_DIM:(j + 1) * HEAD_DIM] = o / denom


def _attn_lat(e, sink, q, k, v, ck, cv):
    past = ck.shape[1]
    q_blk0 = T_CTX // ATTN_BLOCK
    kv_blk0 = T_CTX // LAT_LEN
    q_spec = pl.BlockSpec((ATTN_BLOCK, Q_DIM), lambda b, n: (q_blk0 + b * LAT_BLOCKS + n, 0))
    return pl.pallas_call(
        functools.partial(_attn_lat_kernel, e),
        grid=(N_LAT_SEQ, LAT_BLOCKS),
        in_specs=[pl.BlockSpec(memory_space=pltpu.SMEM),
                  q_spec,
                  pl.BlockSpec((LAT_LEN, KV_DIM), lambda b, n: (kv_blk0 + b, 0)),
                  pl.BlockSpec((LAT_LEN, KV_DIM), lambda b, n: (kv_blk0 + b, 0)),
                  pl.BlockSpec((1, past, KV_DIM), lambda b, n: (b, 0, 0)),
                  pl.BlockSpec((1, past, KV_DIM), lambda b, n: (b, 0, 0))],
        out_specs=pl.BlockSpec((ATTN_BLOCK, Q_DIM), lambda b, n: (b * LAT_BLOCKS + n, 0)),
        out_shape=jax.ShapeDtypeStruct((T_LAT, Q_DIM), F32),
        compiler_params=_params(),
        name="attn_lat",
    )(sink, q, k, v, ck, cv)


def _router(h2, wr_ref):
    hi = h2.astype(BF16)
    lo = (h2 - hi.astype(F32)).astype(BF16)
    whi = wr_ref[0]
    wlo = wr_ref[1]
    dot = functools.partial(jnp.dot, preferred_element_type=F32)
    logits = dot(hi, whi) + (dot(lo, whi) + dot(hi, wlo))
    lane = lax.broadcasted_iota(I32, logits.shape, 1).astype(F32)
    neg = jnp.float32(-jnp.inf)
    big = jnp.float32(1e9)
    is_grp = lane < N_GROUPS
    gl = jnp.where(is_grp, logits, neg)
    gmax = jnp.max(gl, axis=-1, keepdims=True)
    gsum = jnp.sum(jnp.where(is_grp, jnp.exp(logits - gmax), 0.0), axis=-1, keepdims=True)
    g_w = 1.0 / gsum
    g_idx = jnp.min(jnp.where(gl == gmax, lane, big), axis=-1, keepdims=True)
    base = N_GROUPS + EPG * g_idx
    in_grp = jnp.logical_and(lane >= base, lane < base + EPG)
    el = jnp.where(in_grp, logits, neg)
    t1 = jnp.max(el, axis=-1, keepdims=True)
    i1 = jnp.min(jnp.where(el == t1, lane, big), axis=-1, keepdims=True)
    el2 = jnp.where(lane == i1, neg, el)
    t2 = jnp.max(el2, axis=-1, keepdims=True)
    i2 = jnp.min(jnp.where(el2 == t2, lane, big), axis=-1, keepdims=True)
    d = jnp.exp(t2 - t1)
    w1 = g_w / (1.0 + d)
    w2 = g_w * d / (1.0 + d)
    return i1 - N_GROUPS, i2 - N_GROUPS, w1, w2


LIST_SPAN_LOG2 = 5
LIST_SPAN = 1 << LIST_SPAN_LOG2
LIST_ROWS = T_CTX // LIST_SPAN
LIST_COLS = N_EXPERTS * LIST_SPAN


N_TAIL_OUT = 7
N_TAIL_SCRATCH = 4


def _mixer_tail(l, x, y, mod_ref, r, gffn_ref, wr_refs, outs, scratch):
    x1_ref, h2_ref, cnt_ref, lc_ref, gc_ref, ll_ref, gl_ref = outs
    run_ref, acc_ref, gacc_ref, wr_ref = scratch
    i = pl.program_id(0)

    @pl.when(i == 0)
    def _():
        wg_ref, we_ref = wr_refs
        pad = jnp.zeros((D, LANES - N_GROUPS - N_GROUPS * EPG), F32)
        wr = jnp.concatenate([wg_ref[...]] + [we_ref[g] for g in range(N_GROUPS)] + [pad], axis=1)
        hi = wr.astype(BF16)
        wr_ref[0] = hi
        wr_ref[1] = (wr - hi.astype(F32)).astype(BF16)

    g1 = mod_ref[l, pl.ds(r, 1), 2 * D:3 * D]
    shift2 = mod_ref[l, pl.ds(r, 1), 3 * D:4 * D]
    scale2 = mod_ref[l, pl.ds(r, 1), 4 * D:5 * D]
    x1 = x + g1 * y
    x1_ref[...] = x1
    h2 = _rms_mod(x1, gffn_ref[l:l + 1, :], scale2, shift2)
    for s in range(TOK_ROWS):
        h2_ref[pl.ds(s, TM, stride=TOK_ROWS), :] = h2[:, s * LANES:(s + 1) * LANES]
    e1, e2, w1, w2 = _router(h2, wr_ref)

    @pl.when(jnp.logical_or(i == 0, i == NT_CTX))
    def _():
        run_ref[...] = jnp.zeros(run_ref.shape, F32)
        acc_ref[...] = jnp.zeros(acc_ref.shape, F32)
        gacc_ref[...] = jnp.zeros(gacc_ref.shape, F32)

    @pl.when(i == 0)
    def _():
        cnt_ref[...] = jnp.zeros(cnt_ref.shape, I32)

    lane_i = lax.broadcasted_iota(I32, (TM, LANES), 1)
    lane = lane_i.astype(F32)
    member = jnp.where(jnp.logical_or(lane == e1, lane == e2), 1.0, 0.0)
    tri = jnp.where(lax.broadcasted_iota(I32, (TM, TM), 0) >= lax.broadcasted_iota(I32, (TM, TM), 1), 1.0, 0.0)
    csum = jnp.dot(tri.astype(BF16), member.astype(BF16), preferred_element_type=F32)
    run = run_ref[0:1, :]
    before = csum - member + run
    r1 = jnp.sum(jnp.where(lane == e1, before, 0.0), axis=-1, keepdims=True)
    r2 = jnp.sum(jnp.where(lane == e2, before, 0.0), axis=-1, keepdims=True)
    run_new = run + csum[TM - 1:TM, :]
    run_ref[0:1, :] = run_new

    col_i = lax.broadcasted_iota(I32, (TM, LIST_COLS), 1)
    rows_oh, cols_oh = [], []
    for e, rank in ((e1, r1), (e2, r2)):
        rank_i = rank.astype(I32)
        rows_oh.append(jnp.where(lane_i == rank_i // LIST_SPAN, 1.0, 0.0))
        cols_oh.append(jnp.where(col_i == e.astype(I32) * LIST_SPAN + rank_i % LIST_SPAN, 1.0, 0.0).astype(BF16))
    rows_t = jnp.concatenate(rows_oh, axis=0).T
    cols = jnp.concatenate(cols_oh, axis=0)
    tok = (lax.broadcasted_iota(I32, (1, 2 * TM), 1) % TM).astype(F32)
    dot = functools.partial(jnp.dot, preferred_element_type=F32)
    hits = dot(rows_t.astype(BF16), cols)
    tok_sum = dot((rows_t * tok).astype(BF16), cols)
    tile_in_pop = jnp.where(i < NT_CTX, i, i - NT_CTX).astype(F32)
    acc = acc_ref[...] + TOK_ROWS * (tok_sum + TM * tile_in_pop * hits)
    acc_ref[...] = acc

    rec = jnp.where(lane_i == 0, w1, jnp.where(lane_i == 1, w2, 0.0)).T
    rest = jnp.concatenate([rec[0:1, :], rec[1:2, :]], axis=1)
    gacc = gacc_ref[...]
    for _ in range(3):
        piece = rest.astype(BF16)
        gacc = gacc + dot((rows_t * piece.astype(F32)).astype(BF16), cols)
        rest = rest - piece.astype(F32)
    gacc_ref[...] = gacc

    @pl.when(i == NT_CTX - 1)
    def _():
        cnt_ref[0:1, :] = run_new.astype(I32)
        lc_ref[...] = acc.astype(I32)
        gc_ref[...] = gacc

    @pl.when(i == NT - 1)
    def _():
        cnt_ref[1:2, :] = run_new.astype(I32)
        ll_ref[...] = acc.astype(I32)
        gl_ref[...] = gacc


_TAIL_OUT_SHAPES = [jax.ShapeDtypeStruct((T_ALL, D), F32),
                    jax.ShapeDtypeStruct((T_ALL * TOK_ROWS, LANES), F32),
                    jax.ShapeDtypeStruct((SUBLANES, LANES), I32)] + [
                        jax.ShapeDtypeStruct((LIST_ROWS, LIST_COLS), dt) for dt in (I32, F32, I32, F32)]


def _tail_scratch():
    return [pltpu.VMEM((SUBLANES, LANES), F32), pltpu.VMEM((LIST_ROWS, LIST_COLS), F32),
            pltpu.VMEM((LIST_ROWS, LIST_COLS), F32), pltpu.VMEM((2, D, LANES), BF16)]


def _router_specs(l):
    return [_layer_spec((D, N_GROUPS), l), _layer_spec((N_GROUPS, D, EPG), l)]


def _tail_out_specs():
    return [_tile_spec(D),
            pl.BlockSpec((TM * TOK_ROWS, LANES), lambda i: (i, 0)),
            _full_spec((SUBLANES, LANES))] + [_full_spec((LIST_ROWS, LIST_COLS)) for _ in range(4)]


def _halo_specs(halo, width):
    per = TM // halo
    last = T_ALL // halo - 1
    prev = pl.BlockSpec((halo, width), lambda i: (jnp.maximum(i * per - 1, 0), 0))
    nxt = pl.BlockSpec((halo, width), lambda i: (jnp.minimum((i + 1) * per, last), 0))
    return prev, nxt


def _mid_even_kernel(l, a_ref, ap_ref, an_ref, ybc_ref, ybl_ref, xc_ref, xl_ref, mod_ref, wp_ref, ps_ref, wo_ref,
                     gffn_ref, wg_ref, we_ref, *refs):
    outs, (pad_ref,), scratch = refs[:N_TAIL_OUT], refs[N_TAIL_OUT:-N_TAIL_SCRATCH], refs[-N_TAIL_SCRATCH:]
    i = pl.program_id(0)
    r = _mod_row(i)
    off, n, first, last = _seq_info(i)
    a = a_ref[...]
    pad_ref[0:POOL_HALO, :] = jnp.where(first, 0.0, ap_ref[...])
    pad_ref[POOL_HALO:POOL_HALO + TM, :] = a
    pad_ref[POOL_HALO + TM:, :] = jnp.where(last, 0.0, an_ref[...])
    t = off + lax.broadcasted_iota(I32, (TM, 1), 0)
    mixed = []
    for g, w in enumerate(POOL_WINDOWS):
        lo = w // 2
        hi = w - lo - 1
        cols = slice(g * POOL_GROUP_DIM, (g + 1) * POOL_GROUP_DIM)
        total = pad_ref[POOL_HALO - lo:POOL_HALO - lo + TM, cols]
        for j in range(-lo + 1, hi + 1):
            total = total + pad_ref[POOL_HALO + j:POOL_HALO + j + TM, cols]
        count = (jnp.minimum(t + hi, n - 1) - jnp.maximum(t - lo, 0) + 1).astype(F32)
        pooled = total / count - a[:, cols]
        mixed.append(_dot(pooled, wp_ref[g]))
    ya = jnp.concatenate(mixed, axis=1) * ps_ref[...]
    y_pool = _dot(ya, wo_ref[0:POOL_DIM, :])

    def body(pop):
        y = y_pool + _dot((ybc_ref, ybl_ref)[pop][...], wo_ref[POOL_DIM:, :])
        _mixer_tail(l, (xc_ref, xl_ref)[pop][...], y, mod_ref, r, gffn_ref, (wg_ref, we_ref), outs, scratch)

    _per_population(i, body)


def _mid_even(l, e, a, yb_c, yb_l, xc, xl, mod, w_pool, pool_scale, w_out, gffn, wr):
    prev, nxt = _halo_specs(POOL_HALO, POOL_DIM)
    return pl.pallas_call(
        functools.partial(_mid_even_kernel, l),
        grid=(NT,),
        in_specs=[_tile_spec(POOL_DIM), prev, nxt] + _pop_specs(TM, Q_DIM) + _pop_specs(TM, D) + [
                  _MOD_SPEC, _layer_spec((len(POOL_WINDOWS), POOL_GROUP_DIM, POOL_GROUP_DIM), e),
                  _layer_spec((1, POOL_DIM), e), _layer_spec((D, D), e), _full_spec((DEPTH, D))] + _router_specs(l),
        out_specs=_tail_out_specs(),
        out_shape=_TAIL_OUT_SHAPES,
        scratch_shapes=[pltpu.VMEM((TM + 2 * POOL_HALO, POOL_DIM), F32)] + _tail_scratch(),
        compiler_params=_params(),
        name="mid_even",
    )(a, a, a, yb_c, yb_l, xc, xl, mod, w_pool, pool_scale, w_out, gffn, *wr)


def _after_moe(l, r, x1_ref, moe_ref, mod_ref):
    return x1_ref[...] + mod_ref[l, pl.ds(r, 1), 5 * D:6 * D] * _from_token_tiles(moe_ref)


def _in_odd_kernel(l, x1_ref, moec_ref, moel_ref, mod_ref, g_ref, w1_ref, b1_ref, u_ref):
    i = pl.program_id(0)
    r = _mod_row(i)

    def body(pop):
        x2 = _after_moe(l - 1, r, x1_ref, (moec_ref, moel_ref)[pop], mod_ref)
        shift = mod_ref[l, pl.ds(r, 1), 0:D]
        scale = mod_ref[l, pl.ds(r, 1), D:2 * D]
        h = _rms_mod(x2, g_ref[l:l + 1, :], scale, shift)
        u = _dot(h, w1_ref[...]) + b1_ref[...]
        u_ref[...] = u[:, :D] * (1.0 / (1.0 + jnp.exp(-u[:, D:])))

    _per_population(i, body)


def _in_odd(l, o, x1, moe_c, moe_l, mod, g, w1, b1):
    return pl.pallas_call(
        functools.partial(_in_odd_kernel, l),
        grid=(NT,),
        in_specs=[_tile_spec(D)] + _pop_specs(TM * TOK_ROWS, LANES) + [
                  _MOD_SPEC, _full_spec((DEPTH, D)), _layer_spec((D, 2 * D), o), _layer_spec((1, 2 * D), o)],
        out_specs=_tile_spec(D),
        out_shape=jax.ShapeDtypeStruct((T_ALL, D), F32),
        compiler_params=_params(),
        name="in_odd",
    )(x1, moe_c, moe_l, mod, g, w1, b1)


CONV_PAD_ROWS = TM + 2 * CONV_HALO
CONV_CHUNKS = D // LANES
CONV_BLOCK = 64


def _mid_odd_kernel(l, u_ref, up_ref, un_ref, x1_ref, moec_ref, moel_ref, mod_ref, dw_ref, dwb_ref, lng_ref, lnb_ref,
                    w2_ref, b2_ref, gffn_ref, wg_ref, we_ref, *refs):
    outs, scratch = refs[:N_TAIL_OUT], refs[-N_TAIL_SCRATCH:]
    pad_ref, conv_ref, x_ref = refs[N_TAIL_OUT:-N_TAIL_SCRATCH]
    i = pl.program_id(0)

    def residual(pop):
        x_ref[...] = _after_moe(l - 1, _mod_row(i), x1_ref, (moec_ref, moel_ref)[pop], mod_ref)

    _per_population(i, residual)
    r = _mod_row(i)
    _, _, first, last = _seq_info(i)
    u = u_ref[...]
    up = jnp.where(first, 0.0, up_ref[...])
    un = jnp.where(last, 0.0, un_ref[...])
    half = CONV_WIDTH // 2
    first_row = CONV_HALO - half
    for c in range(CONV_CHUNKS):
        cols = slice(c * LANES, (c + 1) * LANES)
        pad_ref[c, 0:CONV_HALO, :] = up[:, cols]
        pad_ref[c, CONV_HALO:CONV_HALO + TM, :] = u[:, cols]
        pad_ref[c, CONV_HALO + TM:, :] = un[:, cols]
        bias = jnp.broadcast_to(dwb_ref[:, cols], (CONV_BLOCK, LANES))
        for r0 in range(0, TM, 2 * CONV_BLOCK):
            acc_even, acc_odd = bias, bias
            for s in range(first_row, first_row + CONV_WIDTH + 1):
                win = pad_ref[c, pl.ds(r0 + s, CONV_BLOCK, stride=2), :]
                j = s - first_row
                if j < CONV_WIDTH:
                    acc_even = acc_even + win * dw_ref[j:j + 1, cols]
                if j >= 1:
                    acc_odd = acc_odd + win * dw_ref[j - 1:j, cols]
            conv_ref[c, pl.ds(r0, CONV_BLOCK, stride=2), :] = acc_even
            conv_ref[c, pl.ds(r0 + 1, CONV_BLOCK, stride=2), :] = acc_odd
    acc = jnp.concatenate([conv_ref[c] for c in range(CONV_CHUNKS)], axis=1)
    mu = jnp.mean(acc, axis=-1, keepdims=True)
    cen = acc - mu
    var = jnp.mean(cen * cen, axis=-1, keepdims=True)
    v = _silu(cen * lax.rsqrt(var + EPS) * lng_ref[...] + lnb_ref[...])
    y = _dot(v, w2_ref[...]) + b2_ref[...]
    _mixer_tail(l, x_ref[...], y, mod_ref, r, gffn_ref, (wg_ref, we_ref), outs, scratch)


def _mid_odd(l, o, u, x1, moe_c, moe_l, mod, dw, dwb, lng, lnb, w2, b2, gffn, wr):
    prev, nxt = _halo_specs(CONV_HALO, D)
    return pl.pallas_call(
        functools.partial(_mid_odd_kernel, l),
        grid=(NT,),
        in_specs=[_tile_spec(D), prev, nxt, _tile_spec(D)] + _pop_specs(TM * TOK_ROWS, LANES) + [
                  _MOD_SPEC, _layer_spec((CONV_WIDTH, D), o),
                  _layer_spec((1, D), o), _layer_spec((1, D), o), _layer_spec((1, D), o),
                  _layer_spec((D, D), o), _layer_spec((1, D), o), _full_spec((DEPTH, D))] + _router_specs(l),
        out_specs=_tail_out_specs(),
        out_shape=_TAIL_OUT_SHAPES,
        scratch_shapes=[pltpu.VMEM((CONV_CHUNKS, CONV_PAD_ROWS, LANES), F32),
                        pltpu.VMEM((CONV_CHUNKS, TM, LANES), F32), pltpu.VMEM((TM, D), F32)] + _tail_scratch(),
        compiler_params=_params(),
        name="mid_odd",
    )(u, u, u, x1, moe_c, moe_l, mod, dw, dwb, lng, lnb, w2, b2, gffn, *wr)


def _tile_tables_kernel(cnt_ref, te_c, tc_c, tw_c, te_l, tc_l, tw_l, nact_ref, first_ref, wplan_ref):
    for p, (te_ref, tc_ref, tw_ref) in enumerate(((te_c, tc_c, tw_c), (te_l, tc_l, tw_l))):
        run = jnp.int32(0)
        ends = []
        for e in range(N_EXPERTS):
            first_ref[p * N_EXPERTS + e] = run
            run = run + (cnt_ref[p, e] + MOE_TM - 1) // MOE_TM
            ends.append(run)
        nact_ref[p] = run

        nxt = jnp.int32(0)
        plans = []
        for e in reversed(range(N_EXPERTS)):
            plans.append(nxt)
            nxt = jnp.where(cnt_ref[p, e] > 0, e + 1, nxt)
        plans.reverse()
        order = jnp.int32(0)
        for e in range(N_EXPERTS):
            wplan_ref[p * N_EXPERTS + e] = (order & 1) + W_NEXT_UNIT * plans[e]
            order = order + jnp.where(cnt_ref[p, e] > 0, 1, 0)

        def expert_of(tile, ends=ends):
            e = jnp.int32(0)
            for k in range(N_EXPERTS):
                e = e + jnp.where(tile >= ends[k], 1, 0)
            return e

        last_e = expert_of(run - 1)

        def tile_body(j, c, te_ref=te_ref, tc_ref=tc_ref, tw_ref=tw_ref, p=p, run=run, expert_of=expert_of,
                      last_e=last_e):
            e = jnp.minimum(expert_of(j), last_e)
            te_ref[j] = e
            chunk = jnp.where(j < run, j - first_ref[p * N_EXPERTS + e], 0)
            tc_ref[j] = chunk
            starts_expert = jnp.logical_and(j < run, chunk == 0)
            tw_ref[j] = wplan_ref[p * N_EXPERTS + e] + jnp.where(starts_expert, W_FIRST_UNIT, 0)
            return c

        lax.fori_loop(0, te_ref.shape[0], tile_body, 0)


def _tile_tables(cnt):
    smem = pl.BlockSpec(memory_space=pltpu.SMEM)
    shapes = [(MOE_TILES_CTX,)] * 3 + [(MOE_TILES_LAT,)] * 3 + [(2,)]
    return pl.pallas_call(
        _tile_tables_kernel,
        in_specs=[smem],
        out_specs=[smem] * len(shapes),
        out_shape=[jax.ShapeDtypeStruct(s, I32) for s in shapes],
        scratch_shapes=[pltpu.SMEM((2 * N_EXPERTS,), I32), pltpu.SMEM((2 * N_EXPERTS,), I32)],
        name="tile_tables",
    )(cnt)


def _moe_kernel(l, pop, te_ref, tc_ref, tw_ref, nact_ref, list_ref, h_ref, gate_ref, w1_hbm, w3_hbm, w2_hbm,
                o_ref, xs_ref, ys_ref, w1_buf, w3_buf, w2_buf, w_sem):
    j = pl.program_id(0)
    nact = nact_ref[pop]
    last_tile = te_ref.shape[0] - 1
    tile_rows = MOE_TM * TOK_ROWS

    def weight_copies(expert, buf):
        return [pltpu.make_async_copy(hbm.at[l, expert], vmem.at[buf], w_sem.at[buf, k])
                for k, (hbm, vmem) in enumerate(((w1_hbm, w1_buf), (w3_hbm, w3_buf), (w2_hbm, w2_buf)))]

    def list_base(tile):
        return tc_ref[tile] * (MOE_TM // LIST_SPAN) * LIST_COLS + te_ref[tile] * LIST_SPAN

    def tok_rows(base, r):
        first = list_ref[base + (r // LIST_SPAN) * LIST_COLS + r % LIST_SPAN]
        return pl.ds(pl.multiple_of(first, TOK_ROWS), TOK_ROWS)

    def buf_rows(buf, r):
        return pl.ds(pl.multiple_of(buf * tile_rows + r * TOK_ROWS, TOK_ROWS), TOK_ROWS)

    def gather(tile, buf):
        base = list_base(tile)
        for r in range(MOE_TM):
            xs_ref[buf_rows(buf, r), :] = h_ref[tok_rows(base, r), :]

    tile = jnp.minimum(j, last_tile)
    plan = tw_ref[tile]
    wbuf = plan & 1
    starts_expert = jnp.logical_and(plan & W_FIRST_UNIT != 0, j <= last_tile)
    next_expert = (plan >> 2) - 1

    @pl.when(j == 0)
    def _():
        for cp in weight_copies(te_ref[0], 0):
            cp.start()
        o_ref[...] = jnp.zeros(o_ref.shape, F32)
        ys_ref[...] = jnp.zeros(ys_ref.shape, F32)
        gather(0, 0)

    @pl.when(starts_expert)
    def _():
        for cp in weight_copies(te_ref[tile], wbuf):
            cp.wait()

        @pl.when(next_expert >= 0)
        def _():
            for cp in weight_copies(next_expert, 1 - wbuf):
                cp.start()

    @pl.when(j <= nact)
    def _():
        cur = j & 1
        w1_ref, w3_ref, w2_ref = w1_buf.at[wbuf], w3_buf.at[wbuf], w2_buf.at[wbuf]
        x = jnp.concatenate([xs_ref[pl.ds(cur * tile_rows + s, MOE_TM, stride=TOK_ROWS), :]
                             for s in range(TOK_ROWS)], axis=1).astype(BF16)
        gather(jnp.minimum(j + 1, last_tile), 1 - cur)

        prev_base = list_base(jnp.maximum(j - 1, 0))
        for r0 in range(0, MOE_TM, MOVE_BATCH):
            dst = [tok_rows(prev_base, r0 + k) for k in range(MOVE_BATCH)]
            vals = [o_ref[dst[k], :] + ys_ref[buf_rows(1 - cur, r0 + k), :] for k in range(MOVE_BATCH)]
            for k in reversed(range(MOVE_BATCH)):
                o_ref[dst[k], :] = vals[k]

        hg = jnp.dot(x, w1_ref[...].astype(BF16), preferred_element_type=F32)
        hu = jnp.dot(x, w3_ref[...].astype(BF16), preferred_element_type=F32)
        y = _dot(_silu(hg) * hu, w2_ref[...])

        first_row = pl.multiple_of(tc_ref[tile] * (MOE_TM // LIST_SPAN), MOE_TM // LIST_SPAN)
        spread = jnp.concatenate(
            [jnp.broadcast_to(gate_ref[pl.ds(first_row + k, 1), :], (LIST_SPAN, LIST_COLS))
             for k in range(MOE_TM // LIST_SPAN)], axis=0)
        want = te_ref[tile] * LIST_SPAN + (lax.broadcasted_iota(I32, (MOE_TM, 1), 0) & (LIST_SPAN - 1))
        hit = jnp.logical_and(lax.broadcasted_iota(I32, (MOE_TM, LIST_COLS), 1) == want, j < nact)
        y = y * jnp.sum(jnp.where(hit, spread, 0.0), axis=1, keepdims=True)
        for s in range(TOK_ROWS):
            ys_ref[pl.ds(cur * tile_rows + s, MOE_TM, stride=TOK_ROWS), :] = y[:, s * LANES:(s + 1) * LANES]


def _moe(l, pop, h_tt, te, tc, tw, nact, rows, gate, w1, w3, w2):
    tp = (T_CTX, T_LAT)[pop]
    blk = (0, T_CTX // T_LAT)[pop]
    n_tiles = te.shape[0]
    hbm = pl.BlockSpec(memory_space=pl.ANY)
    grid_spec = pltpu.PrefetchScalarGridSpec(
        num_scalar_prefetch=5,
        grid=(n_tiles + 1,),
        in_specs=[
            pl.BlockSpec((tp * TOK_ROWS, LANES), lambda j, *_: (blk, 0), pipeline_mode=pl.Buffered(1)),
            pl.BlockSpec((LIST_ROWS, LIST_COLS), lambda j, *_: (0, 0), pipeline_mode=pl.Buffered(1)),
            hbm, hbm, hbm,
        ],
        out_specs=pl.BlockSpec((tp * TOK_ROWS, LANES), lambda j, *_: (0, 0), pipeline_mode=pl.Buffered(1)),
        scratch_shapes=[pltpu.VMEM((2 * MOE_TM * TOK_ROWS, LANES), F32),
                        pltpu.VMEM((2 * MOE_TM * TOK_ROWS, LANES), F32),
                        pltpu.VMEM((2, D, EXPERT_HIDDEN), F32),
                        pltpu.VMEM((2, D, EXPERT_HIDDEN), F32),
                        pltpu.VMEM((2, EXPERT_HIDDEN, D), F32),
                        pltpu.SemaphoreType.DMA((2, 3))],
    )
    return pl.pallas_call(
        functools.partial(_moe_kernel, l, pop),
        grid_spec=grid_spec,
        out_shape=jax.ShapeDtypeStruct((tp * TOK_ROWS, LANES), F32),
        compiler_params=_params(),
        name="moe",
    )(te, tc, tw, nact, rows, h_tt, gate, w1, w3, w2)


def _moe_all(l, h_tt, cnt, rows_c, gate_c, rows_l, gate_l, w1, w3, w2):
    te_c, tc_c, tw_c, te_l, tc_l, tw_l, nact = _tile_tables(cnt)
    out_c = _moe(l, 0, h_tt, te_c, tc_c, tw_c, nact, rows_c.reshape(-1), gate_c, w1, w3, w2)
    out_l = _moe(l, 1, h_tt, te_l, tc_l, tw_l, nact, rows_l.reshape(-1), gate_l, w1, w3, w2)
    return out_c, out_l


def _final_kernel(l, x1_ref, moec_ref, moel_ref, mod_ref, g_ref, oc_ref, ol_ref):
    i = pl.program_id(0)
    r = _mod_row(i)

    def body(pop):
        g2 = mod_ref[l, pl.ds(r, 1), 5 * D:6 * D]
        x = x1_ref[...] + g2 * _from_token_tiles((moec_ref, moel_ref)[pop])
        ms = jnp.mean(x * x, axis=-1, keepdims=True)
        (oc_ref, ol_ref)[pop][...] = x * lax.rsqrt(ms + EPS) * g_ref[...]

    _per_population(i, body)


def _final(l, x1, moe_c, moe_l, mod, g):
    return pl.pallas_call(
        functools.partial(_final_kernel, l),
        grid=(NT,),
        in_specs=[_tile_spec(D)] + _pop_specs(TM * TOK_ROWS, LANES) + [_MOD_SPEC, _full_spec((1, D))],
        out_specs=_pop_specs(TM, D),
        out_shape=[jax.ShapeDtypeStruct((T_CTX, D), F32), jax.ShapeDtypeStruct((T_LAT, D), F32)],
        compiler_params=_params(),
        name="final_norm",
    )(x1, moe_c, moe_l, mod, g)


def _rope_tables():
    t = jnp.arange(LAT_LEN)
    row = (t // GRID_W).astype(F32)
    col = (t % GRID_W).astype(F32)
    inv = ROPE_BASE ** (-jnp.arange(0, ROPE_AXIS_DIM, 2, dtype=F32) / ROPE_AXIS_DIM)

    def table(p):
        ang = p[:, None] * inv[None, :]
        ang = jnp.concatenate([ang, ang], axis=-1)
        return jnp.cos(ang), jnp.sin(ang)

    (cr, sr), (cc, sc) = table(row), table(col)
    cos = jnp.concatenate([cr, cc, cr, cc], axis=-1)
    sin = jnp.concatenate([sr, sc, sr, sc], axis=-1)
    return cos, sin


def kernel(x_prompt, x_sample, cache_k, cache_v, c, c_ctx, w_ada, b_ada, norm_mix_g, norm_ffn_g, w_in_ab, pool_w, pool_scale, attn_sink, w_out_ab, conv_w1, conv_b1, conv_dw, conv_dw_b, conv_ln_g, conv_ln_b, conv_w2, conv_b2, router_grp, router_exp, moe_w1, moe_w3, moe_w2, final_g):
    xc = x_prompt.reshape(T_CTX, D)
    xl = x_sample.reshape(T_LAT, D)
    cond8 = jnp.concatenate([c_ctx[None, :], c, jnp.zeros((SUBLANES - 1 - N_LAT_SEQ, D), F32)], axis=0)
    mod = _modulation(cond8, w_ada, b_ada)
    cos_t, sin_t = _rope_tables()
    wr = (router_grp, router_exp)
    n_even, n_odd = w_in_ab.shape[0], conv_w1.shape[0]
    past = cache_k.shape[2]

    a, q, k, v = _in_even(0, 0, xc, xl, mod, norm_mix_g, w_in_ab, cos_t, sin_t)
    yb_c = _attn_ctx(0, attn_sink, q, k, v)
    yb_l = _attn_lat(0, attn_sink, q, k, v, cache_k[:, 0].reshape(N_LAT_SEQ, past, KV_DIM),
                     cache_v[:, 0].reshape(N_LAT_SEQ, past, KV_DIM))
    x1, h2, *routing = _mid_even(0, 0, a, yb_c, yb_l, xc, xl, mod, pool_w,
                                 pool_scale.reshape(n_even, 1, POOL_DIM), w_out_ab, norm_ffn_g, wr)
    moe0 = _moe_all(0, h2, *routing, moe_w1, moe_w3, moe_w2)

    vec = lambda p: p.reshape(n_odd, 1, -1)
    u = _in_odd(1, 0, x1, *moe0, mod, norm_mix_g, conv_w1, vec(conv_b1))
    x3, h2b, *routing = _mid_odd(1, 0, u, x1, *moe0, mod, conv_dw, vec(conv_dw_b), vec(conv_ln_g),
                                 vec(conv_ln_b), conv_w2, vec(conv_b2), norm_ffn_g, wr)
    moe1 = _moe_all(1, h2b, *routing, moe_w1, moe_w3, moe_w2)
    y_c, y_l = _final(1, x3, *moe1, mod, final_g.reshape(1, D))

    y_prompt = y_c.reshape(N_CTX_SEQ, CTX_LEN, D)
    y_sample = y_l.reshape(N_LAT_SEQ, LAT_LEN, D)
    state_k = k[:T_CTX].reshape(N_CTX_SEQ, 1, CTX_LEN, N_KV_HEADS, HEAD_DIM)
    state_v = v[:T_CTX].reshape(N_CTX_SEQ, 1, CTX_LEN, N_KV_HEADS, HEAD_DIM)
    return (y_prompt, y_sample, state_k, state_v)
```

```python
import functools

import jax
import jax.numpy as jnp
from jax import lax
from jax.experimental import pallas as pl
from jax.experimental.pallas import tpu as pltpu

F32 = jnp.float32
BF16 = jnp.bfloat16
I32 = jnp.int32

D = 1024
N_CTX_SEQ = 16
CTX_LEN = 256
N_LAT_SEQ = 2
LAT_LEN = 1024
T_CTX = N_CTX_SEQ * CTX_LEN
T_LAT = N_LAT_SEQ * LAT_LEN
T_ALL = T_CTX + T_LAT
TM = 256
NT = T_ALL // TM
NT_CTX = T_CTX // TM
LAT_TILES_PER_SEQ = LAT_LEN // TM
GRID_W = 64
DEPTH = 2

POOL_WINDOWS = (2, 4, 8, 16)
POOL_GROUP_DIM = 128
POOL_DIM = 512
HEAD_DIM = 64
N_Q_HEADS = 8
N_KV_HEADS = 2
Q_PER_KV = 4
Q_DIM = 512
KV_DIM = 128
IN_AB = 1280
ATTN_WINDOW = 128
ATTN_BLOCK = 128
ATTN_SCALE = HEAD_DIM ** -0.5
ROPE_BASE = 10000.0
ROPE_AXIS_DIM = 32
CONV_WIDTH = 31
CONV_HALO = 16
POOL_HALO = 8
N_GROUPS = 4
EPG = 4
N_EXPERTS = 16
EXPERT_HIDDEN = 512
EPS = 1e-6
NEG_BIG = -1e30

SUBLANES = 8
LANES = 128
TOK_ROWS = D // LANES

MOE_TM = 256
MOE_SPLIT = NT // 2
T_HALF = MOE_SPLIT * TM
MOE_TILES = 2 * T_HALF // MOE_TM + N_EXPERTS
W_FIRST_UNIT = 2
W_NEXT_UNIT = 4
MOVE_BATCH_LOG2 = 4
MOVE_BATCH = 1 << MOVE_BATCH_LOG2
VMEM_LIMIT = 56 * 1024 * 1024


def _silu(x):
    return x * (1.0 / (1.0 + jnp.exp(-x)))


def _mod_row(i):
    return jnp.where(i < NT_CTX, 0, 1 + (i - NT_CTX) // LAT_TILES_PER_SEQ)


def _seq_info(i):
    is_ctx = i < NT_CTX
    k = (i - NT_CTX) % LAT_TILES_PER_SEQ
    off = jnp.where(is_ctx, 0, k * TM)
    n = jnp.where(is_ctx, CTX_LEN, LAT_LEN)
    first = jnp.logical_or(is_ctx, k == 0)
    last = jnp.logical_or(is_ctx, k == LAT_TILES_PER_SEQ - 1)
    return off, n, first, last


def _rms_mod(x, g, scale, shift):
    ms = jnp.mean(x * x, axis=-1, keepdims=True)
    return (x * lax.rsqrt(ms + EPS) * g) * (1.0 + scale) + shift


def _dot(a, b):
    return jnp.dot(a.astype(BF16), b.astype(BF16), preferred_element_type=F32)


def _dot_nt(a, b):
    return lax.dot_general(a.astype(BF16), b.astype(BF16), (((1,), (1,)), ((), ())),
                           preferred_element_type=F32)


def _from_token_tiles(ref):
    return jnp.concatenate([ref[pl.ds(s, TM, stride=TOK_ROWS), :] for s in range(TOK_ROWS)], axis=1)


def _tile_spec(width):
    return pl.BlockSpec((TM, width), lambda i: (i, 0))


def _full_spec(shape):
    return pl.BlockSpec(shape, lambda i: (0,) * len(shape))


def _layer_spec(shape, l):
    return pl.BlockSpec((None,) + tuple(shape), lambda i: (l,) + (0,) * len(shape))


def _pop_specs(rows, width):
    return [pl.BlockSpec((rows, width), lambda i: (jnp.minimum(i, NT_CTX - 1), 0)),
            pl.BlockSpec((rows, width), lambda i: (jnp.maximum(i - NT_CTX, 0), 0))]


def _per_population(i, body):
    pl.when(i < NT_CTX)(functools.partial(body, 0))
    pl.when(i >= NT_CTX)(functools.partial(body, 1))


def _half_specs(rows, width):
    return [pl.BlockSpec((rows, width), lambda i: (jnp.minimum(i, MOE_SPLIT - 1), 0)),
            pl.BlockSpec((rows, width), lambda i: (jnp.maximum(i - MOE_SPLIT, 0), 0))]


def _per_half(i, body):
    pl.when(i < MOE_SPLIT)(functools.partial(body, 0))
    pl.when(i >= MOE_SPLIT)(functools.partial(body, 1))


_MOD_SPEC = pl.BlockSpec((DEPTH, SUBLANES, 6 * D), lambda i: (0, 0, 0))


def _params():
    return pltpu.CompilerParams(vmem_limit_bytes=VMEM_LIMIT)


MOD_TN = 1536


def _mod_kernel(cond_ref, w_ref, b_ref, o_ref):
    s = _silu(cond_ref[...])
    o_ref[0] = _dot(s, w_ref[0]) + b_ref[0]


def _modulation(cond8, w_ada, b_ada):
    return pl.pallas_call(
        _mod_kernel,
        grid=(DEPTH, 6 * D // MOD_TN),
        in_specs=[
            pl.BlockSpec((SUBLANES, D), lambda l, n: (0, 0)),
            pl.BlockSpec((1, D, MOD_TN), lambda l, n: (l, 0, n)),
            pl.BlockSpec((1, 1, MOD_TN), lambda l, n: (l, 0, n)),
        ],
        out_specs=pl.BlockSpec((1, SUBLANES, MOD_TN), lambda l, n: (l, 0, n)),
        out_shape=jax.ShapeDtypeStruct((DEPTH, SUBLANES, 6 * D), F32),
        compiler_params=_params(),
        name="modulation",
    )(cond8, w_ada, b_ada.reshape(DEPTH, 1, 6 * D))


def _rope_chunk(xc, cos, sin):
    lane = lax.broadcasted_iota(I32, xc.shape, 1)
    first = (lane % ROPE_AXIS_DIM) < (ROPE_AXIS_DIM // 2)
    rot = jnp.where(first, -pltpu.roll(xc, LANES - ROPE_AXIS_DIM // 2, 1), pltpu.roll(xc, ROPE_AXIS_DIM // 2, 1))
    return xc * cos + rot * sin


def _in_even_kernel(l, xc_ref, xl_ref, mod_ref, g_ref, w_ref, cos_ref, sin_ref, a_ref, q_ref, k_ref, v_ref):
    i = pl.program_id(0)
    r = _mod_row(i)

    def body(pop):
        x = (xc_ref, xl_ref)[pop][...]
        shift = mod_ref[l, pl.ds(r, 1), 0:D]
        scale = mod_ref[l, pl.ds(r, 1), D:2 * D]
        h = _rms_mod(x, g_ref[l:l + 1, :], scale, shift)
        proj = _dot(h, w_ref[...])
        a_ref[...] = proj[:, :POOL_DIM]
        v_ref[...] = proj[:, POOL_DIM + Q_DIM + KV_DIM:]
        if pop == 0:
            q_ref[...] = proj[:, POOL_DIM:POOL_DIM + Q_DIM]
            k_ref[...] = proj[:, POOL_DIM + Q_DIM:POOL_DIM + Q_DIM + KV_DIM]
        else:
            off, _, _, _ = _seq_info(i)
            off = pl.multiple_of(off, TM)
            cos = cos_ref[pl.ds(off, TM), :]
            sin = sin_ref[pl.ds(off, TM), :]
            for c in range(Q_DIM // LANES):
                lo = POOL_DIM + c * LANES
                q_ref[:, c * LANES:(c + 1) * LANES] = _rope_chunk(proj[:, lo:lo + LANES], cos, sin)
            lo = POOL_DIM + Q_DIM
            k_ref[...] = _rope_chunk(proj[:, lo:lo + LANES], cos, sin)

    _per_population(i, body)


def _in_even(l, e, xc, xl, mod, g, w_in, cos_t, sin_t):
    return pl.pallas_call(
        functools.partial(_in_even_kernel, l),
        grid=(NT,),
        in_specs=_pop_specs(TM, D) + [_MOD_SPEC, _full_spec((DEPTH, D)), _layer_spec((D, IN_AB), e),
                  _full_spec((LAT_LEN, LANES)), _full_spec((LAT_LEN, LANES))],
        out_specs=[_tile_spec(POOL_DIM), _tile_spec(Q_DIM), _tile_spec(KV_DIM), _tile_spec(KV_DIM)],
        out_shape=[jax.ShapeDtypeStruct((T_ALL, POOL_DIM), F32), jax.ShapeDtypeStruct((T_ALL, Q_DIM), F32),
                   jax.ShapeDtypeStruct((T_ALL, KV_DIM), F32), jax.ShapeDtypeStruct((T_ALL, KV_DIM), F32)],
        compiler_params=_params(),
        name="in_even",
    )(xc, xl, mod, g, w_in, cos_t, sin_t)


def _attn_ctx_kernel(e, sink_ref, q_ref, k_ref, v_ref, o_ref):
    q = q_ref[...]
    k = k_ref[...]
    v = v_ref[...]
    for h in range(N_KV_HEADS):
        kh = k[:, h * HEAD_DIM:(h + 1) * HEAD_DIM]
        vh = v[:, h * HEAD_DIM:(h + 1) * HEAD_DIM]
        for g in range(Q_PER_KV):
            j = h * Q_PER_KV + g
            s = _dot_nt(q[:, j * HEAD_DIM:(j + 1) * HEAD_DIM], kh) * ATTN_SCALE
            sk = sink_ref[e, j]
            m = jnp.maximum(jnp.max(s, axis=-1, keepdims=True), sk)
            p = jnp.exp(s - m)
            denom = jnp.sum(p, axis=-1, keepdims=True) + jnp.exp(sk - m)
            o_ref[:, j * HEAD_DIM:(j + 1) * HEAD_DIM] = _dot(p, vh) / denom


def _attn_ctx(e, sink, q, k, v):
    tile = lambda w: pl.BlockSpec((TM, w), lambda b: (b, 0))
    return pl.pallas_call(
        functools.partial(_attn_ctx_kernel, e),
        grid=(N_CTX_SEQ,),
        in_specs=[pl.BlockSpec(memory_space=pltpu.SMEM), tile(Q_DIM), tile(KV_DIM), tile(KV_DIM)],
        out_specs=tile(Q_DIM),
        out_shape=jax.ShapeDtypeStruct((T_CTX, Q_DIM), F32),
        compiler_params=_params(),
        name="attn_ctx",
    )(sink, q, k, v)


LAT_BLOCKS = LAT_LEN // ATTN_BLOCK
BAND = 3 * ATTN_BLOCK


def _attn_lat_kernel(e, sink_ref, q_ref, k_ref, v_ref, ck_ref, cv_ref, o_ref):
    n = pl.program_id(1)
    start = jnp.clip((n - 1) * ATTN_BLOCK, 0, LAT_LEN - BAND)
    start = pl.multiple_of(start, ATTN_BLOCK)
    q = q_ref[...]
    kw = k_ref[pl.ds(start, BAND), :]
    vw = v_ref[pl.ds(start, BAND), :]
    ck = ck_ref[0]
    cv = cv_ref[0]
    qpos = n * ATTN_BLOCK + lax.broadcasted_iota(I32, (ATTN_BLOCK, BAND), 0)
    kpos = start + lax.broadcasted_iota(I32, (ATTN_BLOCK, BAND), 1)
    valid = jnp.abs(qpos - kpos) <= ATTN_WINDOW
    for h in range(N_KV_HEADS):
        hs = slice(h * HEAD_DIM, (h + 1) * HEAD_DIM)
        for g in range(Q_PER_KV):
            j = h * Q_PER_KV + g
            qj = q[:, j * HEAD_DIM:(j + 1) * HEAD_DIM]
            s_loc = jnp.where(valid, _dot_nt(qj, kw[:, hs]) * ATTN_SCALE, NEG_BIG)
            s_ctx = _dot_nt(qj, ck[:, hs]) * ATTN_SCALE
            sk = sink_ref[e, j]
            m = jnp.maximum(jnp.maximum(jnp.max(s_loc, axis=-1, keepdims=True),
                                        jnp.max(s_ctx, axis=-1, keepdims=True)), sk)
            p_loc = jnp.exp(s_loc - m)
            p_ctx = jnp.exp(s_ctx - m)
            denom = (jnp.sum(p_loc, axis=-1, keepdims=True) + jnp.sum(p_ctx, axis=-1, keepdims=True)
                     + jnp.exp(sk - m))
            o = _dot(p_loc, vw[:, hs]) + _dot(p_ctx, cv[:, hs])
            o_ref[:, j * HEAD_DIM:(j + 1) * HEAD_DIM] = o / denom


def _attn_lat(e, sink, q, k, v, ck, cv):
    past = ck.shape[1]
    q_blk0 = T_CTX // ATTN_BLOCK
    kv_blk0 = T_CTX // LAT_LEN
    q_spec = pl.BlockSpec((ATTN_BLOCK, Q_DIM), lambda b, n: (q_blk0 + b * LAT_BLOCKS + n, 0))
    return pl.pallas_call(
        functools.partial(_attn_lat_kernel, e),
        grid=(N_LAT_SEQ, LAT_BLOCKS),
        in_specs=[pl.BlockSpec(memory_space=pltpu.SMEM),
                  q_spec,
                  pl.BlockSpec((LAT_LEN, KV_DIM), lambda b, n: (kv_blk0 + b, 0)),
                  pl.BlockSpec((LAT_LEN, KV_DIM), lambda b, n: (kv_blk0 + b, 0)),
                  pl.BlockSpec((1, past, KV_DIM), lambda b, n: (b, 0, 0)),
                  pl.BlockSpec((1, past, KV_DIM), lambda b, n: (b, 0, 0))],
        out_specs=pl.BlockSpec((ATTN_BLOCK, Q_DIM), lambda b, n: (b * LAT_BLOCKS + n, 0)),
        out_shape=jax.ShapeDtypeStruct((T_LAT, Q_DIM), F32),
        compiler_params=_params(),
        name="attn_lat",
    )(sink, q, k, v, ck, cv)


def _router(h2, wr_ref):
    hi = h2.astype(BF16)
    lo = (h2 - hi.astype(F32)).astype(BF16)
    whi = wr_ref[0]
    wlo = wr_ref[1]
    dot = functools.partial(jnp.dot, preferred_element_type=F32)
    logits = dot(hi, whi) + (dot(lo, whi) + dot(hi, wlo))
    lane = lax.broadcasted_iota(I32, logits.shape, 1).astype(F32)
    neg = jnp.float32(-jnp.inf)
    big = jnp.float32(1e9)
    is_grp = lane < N_GROUPS
    gl = jnp.where(is_grp, logits, neg)
    gmax = jnp.max(gl, axis=-1, keepdims=True)
    gsum = jnp.sum(jnp.where(is_grp, jnp.exp(logits - gmax), 0.0), axis=-1, keepdims=True)
    g_w = 1.0 / gsum
    g_idx = jnp.min(jnp.where(gl == gmax, lane, big), axis=-1, keepdims=True)
    base = N_GROUPS + EPG * g_idx
    in_grp = jnp.logical_and(lane >= base, lane < base + EPG)
    el = jnp.where(in_grp, logits, neg)
    t1 = jnp.max(el, axis=-1, keepdims=True)
    i1 = jnp.min(jnp.where(el == t1, lane, big), axis=-1, keepdims=True)
    el2 = jnp.where(lane == i1, neg, el)
    t2 = jnp.max(el2, axis=-1, keepdims=True)
    i2 = jnp.min(jnp.where(el2 == t2, lane, big), axis=-1, keepdims=True)
    d = jnp.exp(t2 - t1)
    w1 = g_w / (1.0 + d)
    w2 = g_w * d / (1.0 + d)
    return i1 - N_GROUPS, i2 - N_GROUPS, w1, w2


LIST_SPAN_LOG2 = 5
LIST_SPAN = 1 << LIST_SPAN_LOG2
LIST_ROWS = LANES
LIST_COLS = N_EXPERTS * LIST_SPAN


N_TAIL_OUT = 7
N_TAIL_SCRATCH = 3


def _mixer_tail(l, x, y, mod_ref, r, gffn_ref, wr_ref, outs, scratch):
    x1_ref, h2_ref, cnt_ref, lc_ref, gc_ref, ll_ref, gl_ref = outs
    run_ref, acc_ref, gacc_ref = scratch
    i = pl.program_id(0)
    g1 = mod_ref[l, pl.ds(r, 1), 2 * D:3 * D]
    shift2 = mod_ref[l, pl.ds(r, 1), 3 * D:4 * D]
    scale2 = mod_ref[l, pl.ds(r, 1), 4 * D:5 * D]
    x1 = x + g1 * y
    x1_ref[...] = x1
    h2 = _rms_mod(x1, gffn_ref[l:l + 1, :], scale2, shift2)
    for s in range(TOK_ROWS):
        h2_ref[pl.ds(s, TM, stride=TOK_ROWS), :] = h2[:, s * LANES:(s + 1) * LANES]
    e1, e2, w1, w2 = _router(h2, wr_ref)

    @pl.when(jnp.logical_or(i == 0, i == MOE_SPLIT))
    def _():
        run_ref[...] = jnp.zeros(run_ref.shape, F32)
        acc_ref[...] = jnp.zeros(acc_ref.shape, F32)
        gacc_ref[...] = jnp.zeros(gacc_ref.shape, F32)

    @pl.when(i == 0)
    def _():
        cnt_ref[...] = jnp.zeros(cnt_ref.shape, I32)

    lane_i = lax.broadcasted_iota(I32, (TM, LANES), 1)
    lane = lane_i.astype(F32)
    member = jnp.where(jnp.logical_or(lane == e1, lane == e2), 1.0, 0.0)
    tri = jnp.where(lax.broadcasted_iota(I32, (TM, TM), 0) >= lax.broadcasted_iota(I32, (TM, TM), 1), 1.0, 0.0)
    csum = jnp.dot(tri.astype(BF16), member.astype(BF16), preferred_element_type=F32)
    run = run_ref[0:1, :]
    before = csum - member + run
    r1 = jnp.sum(jnp.where(lane == e1, before, 0.0), axis=-1, keepdims=True)
    r2 = jnp.sum(jnp.where(lane == e2, before, 0.0), axis=-1, keepdims=True)
    run_new = run + csum[TM - 1:TM, :]
    run_ref[0:1, :] = run_new

    col_i = lax.broadcasted_iota(I32, (TM, LIST_COLS), 1)
    rows_oh, cols_oh = [], []
    for e, rank in ((e1, r1), (e2, r2)):
        rank_i = rank.astype(I32)
        rows_oh.append(jnp.where(lane_i == rank_i // LIST_SPAN, 1.0, 0.0))
        cols_oh.append(jnp.where(col_i == e.astype(I32) * LIST_SPAN + rank_i % LIST_SPAN, 1.0, 0.0).astype(BF16))
    rows_t = jnp.concatenate(rows_oh, axis=0).T
    cols = jnp.concatenate(cols_oh, axis=0)
    tok = (lax.broadcasted_iota(I32, (1, 2 * TM), 1) % TM).astype(F32)
    dot = functools.partial(jnp.dot, preferred_element_type=F32)
    hits = dot(rows_t.astype(BF16), cols)
    tok_sum = dot((rows_t * tok).astype(BF16), cols)
    tile_in_pop = jnp.where(i < MOE_SPLIT, i, i - MOE_SPLIT).astype(F32)
    acc = acc_ref[...] + TOK_ROWS * (tok_sum + TM * tile_in_pop * hits)
    acc_ref[...] = acc

    rec = jnp.where(lane_i == 0, w1, jnp.where(lane_i == 1, w2, 0.0)).T
    rest = jnp.concatenate([rec[0:1, :], rec[1:2, :]], axis=1)
    gacc = gacc_ref[...]
    for _ in range(3):
        piece = rest.astype(BF16)
        gacc = gacc + dot((rows_t * piece.astype(F32)).astype(BF16), cols)
        rest = rest - piece.astype(F32)
    gacc_ref[...] = gacc

    @pl.when(i == MOE_SPLIT - 1)
    def _():
        cnt_ref[0:1, :] = run_new.astype(I32)
        lc_ref[...] = acc.astype(I32)
        gc_ref[...] = gacc

    @pl.when(i == NT - 1)
    def _():
        cnt_ref[1:2, :] = run_new.astype(I32)
        ll_ref[...] = acc.astype(I32)
        gl_ref[...] = gacc


_TAIL_OUT_SHAPES = [jax.ShapeDtypeStruct((T_ALL, D), F32),
                    jax.ShapeDtypeStruct((T_ALL * TOK_ROWS, LANES), F32),
                    jax.ShapeDtypeStruct((SUBLANES, LANES), I32)] + [
                        jax.ShapeDtypeStruct((LIST_ROWS, LIST_COLS), dt) for dt in (I32, F32, I32, F32)]


def _tail_scratch():
    return [pltpu.VMEM((SUBLANES, LANES), F32), pltpu.VMEM((LIST_ROWS, LIST_COLS), F32),
            pltpu.VMEM((LIST_ROWS, LIST_COLS), F32)]


def _tail_out_specs():
    return [_tile_spec(D),
            pl.BlockSpec((TM * TOK_ROWS, LANES), lambda i: (i, 0)),
            _full_spec((SUBLANES, LANES))] + [_full_spec((LIST_ROWS, LIST_COLS)) for _ in range(4)]


def _halo_specs(halo, width):
    per = TM // halo
    last = T_ALL // halo - 1
    prev = pl.BlockSpec((halo, width), lambda i: (jnp.maximum(i * per - 1, 0), 0))
    nxt = pl.BlockSpec((halo, width), lambda i: (jnp.minimum((i + 1) * per, last), 0))
    return prev, nxt


def _mid_even_kernel(l, a_ref, ap_ref, an_ref, ybc_ref, ybl_ref, xc_ref, xl_ref, mod_ref, wp_ref, ps_ref, wo_ref,
                     gffn_ref, wr_ref, *refs):
    outs, (pad_ref,), scratch = refs[:N_TAIL_OUT], refs[N_TAIL_OUT:-N_TAIL_SCRATCH], refs[-N_TAIL_SCRATCH:]
    i = pl.program_id(0)
    r = _mod_row(i)
    off, n, first, last = _seq_info(i)
    a = a_ref[...]
    pad_ref[0:POOL_HALO, :] = jnp.where(first, 0.0, ap_ref[...])
    pad_ref[POOL_HALO:POOL_HALO + TM, :] = a
    pad_ref[POOL_HALO + TM:, :] = jnp.where(last, 0.0, an_ref[...])
    t = off + lax.broadcasted_iota(I32, (TM, 1), 0)
    mixed = []
    for g, w in enumerate(POOL_WINDOWS):
        lo = w // 2
        hi = w - lo - 1
        cols = slice(g * POOL_GROUP_DIM, (g + 1) * POOL_GROUP_DIM)
        total = pad_ref[POOL_HALO - lo:POOL_HALO - lo + TM, cols]
        for j in range(-lo + 1, hi + 1):
            total = total + pad_ref[POOL_HALO + j:POOL_HALO + j + TM, cols]
        count = (jnp.minimum(t + hi, n - 1) - jnp.maximum(t - lo, 0) + 1).astype(F32)
        pooled = total / count - a[:, cols]
        mixed.append(_dot(pooled, wp_ref[g]))
    ya = jnp.concatenate(mixed, axis=1) * ps_ref[...]
    y_pool = _dot(ya, wo_ref[0:POOL_DIM, :])

    def body(pop):
        y = y_pool + _dot((ybc_ref, ybl_ref)[pop][...], wo_ref[POOL_DIM:, :])
        _mixer_tail(l, (xc_ref, xl_ref)[pop][...], y, mod_ref, r, gffn_ref, wr_ref, outs, scratch)

    _per_population(i, body)


def _mid_even(l, e, a, yb_c, yb_l, xc, xl, mod, w_pool, pool_scale, w_out, gffn, wr):
    prev, nxt = _halo_specs(POOL_HALO, POOL_DIM)
    return pl.pallas_call(
        functools.partial(_mid_even_kernel, l),
        grid=(NT,),
        in_specs=[_tile_spec(POOL_DIM), prev, nxt] + _pop_specs(TM, Q_DIM) + _pop_specs(TM, D) + [
                  _MOD_SPEC, _layer_spec((len(POOL_WINDOWS), POOL_GROUP_DIM, POOL_GROUP_DIM), e),
                  _layer_spec((1, POOL_DIM), e), _layer_spec((D, D), e), _full_spec((DEPTH, D)),
                  _layer_spec((2, D, LANES), l)],
        out_specs=_tail_out_specs(),
        out_shape=_TAIL_OUT_SHAPES,
        scratch_shapes=[pltpu.VMEM((TM + 2 * POOL_HALO, POOL_DIM), F32)] + _tail_scratch(),
        compiler_params=_params(),
        name="mid_even",
    )(a, a, a, yb_c, yb_l, xc, xl, mod, w_pool, pool_scale, w_out, gffn, wr)


def _after_moe(l, r, x1_ref, moe_ref, mod_ref):
    return x1_ref[...] + mod_ref[l, pl.ds(r, 1), 5 * D:6 * D] * _from_token_tiles(moe_ref)


def _in_odd_kernel(l, x1_ref, moec_ref, moel_ref, mod_ref, g_ref, w1_ref, b1_ref, u_ref):
    i = pl.program_id(0)
    r = _mod_row(i)

    def body(pop):
        x2 = _after_moe(l - 1, r, x1_ref, (moec_ref, moel_ref)[pop], mod_ref)
        shift = mod_ref[l, pl.ds(r, 1), 0:D]
        scale = mod_ref[l, pl.ds(r, 1), D:2 * D]
        h = _rms_mod(x2, g_ref[l:l + 1, :], scale, shift)
        u = _dot(h, w1_ref[...]) + b1_ref[...]
        u_ref[...] = u[:, :D] * (1.0 / (1.0 + jnp.exp(-u[:, D:])))

    _per_half(i, body)


def _in_odd(l, o, x1, moe_c, moe_l, mod, g, w1, b1):
    return pl.pallas_call(
        functools.partial(_in_odd_kernel, l),
        grid=(NT,),
        in_specs=[_tile_spec(D)] + _half_specs(TM * TOK_ROWS, LANES) + [
                  _MOD_SPEC, _full_spec((DEPTH, D)), _layer_spec((D, 2 * D), o), _layer_spec((1, 2 * D), o)],
        out_specs=_tile_spec(D),
        out_shape=jax.ShapeDtypeStruct((T_ALL, D), F32),
        compiler_params=_params(),
        name="in_odd",
    )(x1, moe_c, moe_l, mod, g, w1, b1)


CONV_PAD_ROWS = TM + 2 * CONV_HALO
CONV_CHUNKS = D // LANES
CONV_BLOCK = 64


def _mid_odd_kernel(l, u_ref, up_ref, un_ref, x1_ref, moec_ref, moel_ref, mod_ref, dw_ref, dwb_ref, lng_ref, lnb_ref,
                    w2_ref, b2_ref, gffn_ref, wr_ref, *refs):
    outs, scratch = refs[:N_TAIL_OUT], refs[-N_TAIL_SCRATCH:]
    pad_ref, conv_ref, x_ref = refs[N_TAIL_OUT:-N_TAIL_SCRATCH]
    i = pl.program_id(0)

    def residual(pop):
        x_ref[...] = _after_moe(l - 1, _mod_row(i), x1_ref, (moec_ref, moel_ref)[pop], mod_ref)

    _per_half(i, residual)
    r = _mod_row(i)
    _, _, first, last = _seq_info(i)
    u = u_ref[...]
    up = jnp.where(first, 0.0, up_ref[...])
    un = jnp.where(last, 0.0, un_ref[...])
    half = CONV_WIDTH // 2
    first_row = CONV_HALO - half
    for c in range(CONV_CHUNKS):
        cols = slice(c * LANES, (c + 1) * LANES)
        pad_ref[c, 0:CONV_HALO, :] = up[:, cols]
        pad_ref[c, CONV_HALO:CONV_HALO + TM, :] = u[:, cols]
        pad_ref[c, CONV_HALO + TM:, :] = un[:, cols]
        bias = jnp.broadcast_to(dwb_ref[:, cols], (CONV_BLOCK, LANES))
        for r0 in range(0, TM, 2 * CONV_BLOCK):
            acc_even, acc_odd = bias, bias
            for s in range(first_row, first_row + CONV_WIDTH + 1):
                win = pad_ref[c, pl.ds(r0 + s, CONV_BLOCK, stride=2), :]
                j = s - first_row
                if j < CONV_WIDTH:
                    acc_even = acc_even + win * dw_ref[j:j + 1, cols]
                if j >= 1:
                    acc_odd = acc_odd + win * dw_ref[j - 1:j, cols]
            conv_ref[c, pl.ds(r0, CONV_BLOCK, stride=2), :] = acc_even
            conv_ref[c, pl.ds(r0 + 1, CONV_BLOCK, stride=2), :] = acc_odd
    acc = jnp.concatenate([conv_ref[c] for c in range(CONV_CHUNKS)], axis=1)
    mu = jnp.mean(acc, axis=-1, keepdims=True)
    cen = acc - mu
    var = jnp.mean(cen * cen, axis=-1, keepdims=True)
    v = _silu(cen * lax.rsqrt(var + EPS) * lng_ref[...] + lnb_ref[...])
    y = _dot(v, w2_ref[...]) + b2_ref[...]
    _mixer_tail(l, x_ref[...], y, mod_ref, r, gffn_ref, wr_ref, outs, scratch)


def _mid_odd(l, o, u, x1, moe_c, moe_l, mod, dw, dwb, lng, lnb, w2, b2, gffn, wr):
    prev, nxt = _halo_specs(CONV_HALO, D)
    return pl.pallas_call(
        functools.partial(_mid_odd_kernel, l),
        grid=(NT,),
        in_specs=[_tile_spec(D), prev, nxt, _tile_spec(D)] + _half_specs(TM * TOK_ROWS, LANES) + [
                  _MOD_SPEC, _layer_spec((CONV_WIDTH, D), o),
                  _layer_spec((1, D), o), _layer_spec((1, D), o), _layer_spec((1, D), o),
                  _layer_spec((D, D), o), _layer_spec((1, D), o), _full_spec((DEPTH, D)),
                  _layer_spec((2, D, LANES), l)],
        out_specs=_tail_out_specs(),
        out_shape=_TAIL_OUT_SHAPES,
        scratch_shapes=[pltpu.VMEM((CONV_CHUNKS, CONV_PAD_ROWS, LANES), F32),
                        pltpu.VMEM((CONV_CHUNKS, TM, LANES), F32), pltpu.VMEM((TM, D), F32)] + _tail_scratch(),
        compiler_params=_params(),
        name="mid_odd",
    )(u, u, u, x1, moe_c, moe_l, mod, dw, dwb, lng, lnb, w2, b2, gffn, wr)


def _tile_tables_kernel(cnt_ref, te_c, tc_c, tw_c, te_l, tc_l, tw_l, nact_ref, first_ref, wplan_ref):
    for p, (te_ref, tc_ref, tw_ref) in enumerate(((te_c, tc_c, tw_c), (te_l, tc_l, tw_l))):
        run = jnp.int32(0)
        ends = []
        for e in range(N_EXPERTS):
            first_ref[p * N_EXPERTS + e] = run
            run = run + (cnt_ref[p, e] + MOE_TM - 1) // MOE_TM
            ends.append(run)
        nact_ref[p] = run

        nxt = jnp.int32(0)
        plans = []
        for e in reversed(range(N_EXPERTS)):
            plans.append(nxt)
            nxt = jnp.where(cnt_ref[p, e] > 0, e + 1, nxt)
        plans.reverse()
        order = jnp.int32(0)
        for e in range(N_EXPERTS):
            wplan_ref[p * N_EXPERTS + e] = (order & 1) + W_NEXT_UNIT * plans[e]
            order = order + jnp.where(cnt_ref[p, e] > 0, 1, 0)

        def expert_of(tile, ends=ends):
            e = jnp.int32(0)
            for k in range(N_EXPERTS):
                e = e + jnp.where(tile >= ends[k], 1, 0)
            return e

        last_e = expert_of(run - 1)

        def tile_body(j, c, te_ref=te_ref, tc_ref=tc_ref, tw_ref=tw_ref, p=p, run=run, expert_of=expert_of,
                      last_e=last_e):
            e = jnp.minimum(expert_of(j), last_e)
            te_ref[j] = e
            chunk = jnp.where(j < run, j - first_ref[p * N_EXPERTS + e], 0)
            tc_ref[j] = chunk
            starts_expert = jnp.logical_and(j < run, chunk == 0)
            tw_ref[j] = wplan_ref[p * N_EXPERTS + e] + jnp.where(starts_expert, W_FIRST_UNIT, 0)
            return c

        lax.fori_loop(0, te_ref.shape[0], tile_body, 0)


def _tile_tables(cnt):
    smem = pl.BlockSpec(memory_space=pltpu.SMEM)
    shapes = [(MOE_TILES,)] * 6 + [(2,)]
    return pl.pallas_call(
        _tile_tables_kernel,
        in_specs=[smem],
        out_specs=[smem] * len(shapes),
        out_shape=[jax.ShapeDtypeStruct(s, I32) for s in shapes],
        scratch_shapes=[pltpu.SMEM((2 * N_EXPERTS,), I32), pltpu.SMEM((2 * N_EXPERTS,), I32)],
        name="tile_tables",
    )(cnt)


def _moe_kernel(l, pop, te_ref, tc_ref, tw_ref, nact_ref, list_ref, h_ref, gate_ref, w1_hbm, w3_hbm, w2_hbm,
                o_ref, xs_ref, ys_ref, w1_buf, w3_buf, w2_buf, w_sem):
    j = pl.program_id(0)
    nact = nact_ref[pop]
    last_tile = te_ref.shape[0] - 1
    tile_rows = MOE_TM * TOK_ROWS

    def weight_copies(expert, buf):
        return [pltpu.make_async_copy(hbm.at[l, expert], vmem.at[buf], w_sem.at[buf, k])
                for k, (hbm, vmem) in enumerate(((w1_hbm, w1_buf), (w3_hbm, w3_buf), (w2_hbm, w2_buf)))]

    def list_base(tile):
        return tc_ref[tile] * (MOE_TM // LIST_SPAN) * LIST_COLS + te_ref[tile] * LIST_SPAN

    def tok_rows(base, r):
        first = list_ref[base + (r // LIST_SPAN) * LIST_COLS + r % LIST_SPAN]
        return pl.ds(pl.multiple_of(first, TOK_ROWS), TOK_ROWS)

    def buf_rows(buf, r):
        return pl.ds(pl.multiple_of(buf * tile_rows + r * TOK_ROWS, TOK_ROWS), TOK_ROWS)

    def gather(tile, buf):
        base = list_base(tile)
        for r in range(MOE_TM):
            xs_ref[buf_rows(buf, r), :] = h_ref[tok_rows(base, r), :]

    tile = jnp.minimum(j, last_tile)
    plan = tw_ref[tile]
    wbuf = plan & 1
    starts_expert = jnp.logical_and(plan & W_FIRST_UNIT != 0, j <= last_tile)
    next_expert = (plan >> 2) - 1

    @pl.when(j == 0)
    def _():
        for cp in weight_copies(te_ref[0], 0):
            cp.start()
        o_ref[...] = jnp.zeros(o_ref.shape, F32)
        ys_ref[...] = jnp.zeros(ys_ref.shape, F32)
        gather(0, 0)

    @pl.when(starts_expert)
    def _():
        for cp in weight_copies(te_ref[tile], wbuf):
            cp.wait()

        @pl.when(next_expert >= 0)
        def _():
            for cp in weight_copies(next_expert, 1 - wbuf):
                cp.start()

    @pl.when(j <= nact)
    def _():
        cur = j & 1
        w1_ref, w3_ref, w2_ref = w1_buf.at[wbuf], w3_buf.at[wbuf], w2_buf.at[wbuf]
        x = jnp.concatenate([xs_ref[pl.ds(cur * tile_rows + s, MOE_TM, stride=TOK_ROWS), :]
                             for s in range(TOK_ROWS)], axis=1).astype(BF16)
        gather(jnp.minimum(j + 1, last_tile), 1 - cur)

        prev_base = list_base(jnp.maximum(j - 1, 0))
        for r0 in range(0, MOE_TM, MOVE_BATCH):
            dst = [tok_rows(prev_base, r0 + k) for k in range(MOVE_BATCH)]
            vals = [o_ref[dst[k], :] + ys_ref[buf_rows(1 - cur, r0 + k), :] for k in range(MOVE_BATCH)]
            for k in reversed(range(MOVE_BATCH)):
                o_ref[dst[k], :] = vals[k]

        hg = jnp.dot(x, w1_ref[...].astype(BF16), preferred_element_type=F32)
        hu = jnp.dot(x, w3_ref[...].astype(BF16), preferred_element_type=F32)
        y = _dot(_silu(hg) * hu, w2_ref[...])

        first_row = pl.multiple_of(tc_ref[tile] * (MOE_TM // LIST_SPAN), MOE_TM // LIST_SPAN)
        spread = jnp.concatenate(
            [jnp.broadcast_to(gate_ref[pl.ds(first_row + k, 1), :], (LIST_SPAN, LIST_COLS))
             for k in range(MOE_TM // LIST_SPAN)], axis=0)
        want = te_ref[tile] * LIST_SPAN + (lax.broadcasted_iota(I32, (MOE_TM, 1), 0) & (LIST_SPAN - 1))
        hit = jnp.logical_and(lax.broadcasted_iota(I32, (MOE_TM, LIST_COLS), 1) == want, j < nact)
        y = y * jnp.sum(jnp.where(hit, spread, 0.0), axis=1, keepdims=True)
        for s in range(TOK_ROWS):
            ys_ref[pl.ds(cur * tile_rows + s, MOE_TM, stride=TOK_ROWS), :] = y[:, s * LANES:(s + 1) * LANES]


def _moe(l, pop, h_tt, te, tc, tw, nact, rows, gate, w1, w3, w2):
    tp = T_HALF
    blk = pop
    n_tiles = te.shape[0]
    hbm = pl.BlockSpec(memory_space=pl.ANY)
    grid_spec = pltpu.PrefetchScalarGridSpec(
        num_scalar_prefetch=5,
        grid=(n_tiles + 1,),
        in_specs=[
            pl.BlockSpec((tp * TOK_ROWS, LANES), lambda j, *_: (blk, 0), pipeline_mode=pl.Buffered(1)),
            pl.BlockSpec((LIST_ROWS, LIST_COLS), lambda j, *_: (0, 0), pipeline_mode=pl.Buffered(1)),
            hbm, hbm, hbm,
        ],
        out_specs=pl.BlockSpec((tp * TOK_ROWS, LANES), lambda j, *_: (0, 0), pipeline_mode=pl.Buffered(1)),
        scratch_shapes=[pltpu.VMEM((2 * MOE_TM * TOK_ROWS, LANES), F32),
                        pltpu.VMEM((2 * MOE_TM * TOK_ROWS, LANES), F32),
                        pltpu.VMEM((2, D, EXPERT_HIDDEN), F32),
                        pltpu.VMEM((2, D, EXPERT_HIDDEN), F32),
                        pltpu.VMEM((2, EXPERT_HIDDEN, D), F32),
                        pltpu.SemaphoreType.DMA((2, 3))],
    )
    return pl.pallas_call(
        functools.partial(_moe_kernel, l, pop),
        grid_spec=grid_spec,
        out_shape=jax.ShapeDtypeStruct((tp * TOK_ROWS, LANES), F32),
        compiler_params=_params(),
        name="moe",
    )(te, tc, tw, nact, rows, h_tt, gate, w1, w3, w2)


def _moe_all(l, h_tt, cnt, rows_c, gate_c, rows_l, gate_l, w1, w3, w2):
    te_c, tc_c, tw_c, te_l, tc_l, tw_l, nact = _tile_tables(cnt)
    out_c = _moe(l, 0, h_tt, te_c, tc_c, tw_c, nact, rows_c.reshape(-1), gate_c, w1, w3, w2)
    out_l = _moe(l, 1, h_tt, te_l, tc_l, tw_l, nact, rows_l.reshape(-1), gate_l, w1, w3, w2)
    return out_c, out_l


def _final_kernel(l, x1_ref, moec_ref, moel_ref, mod_ref, g_ref, oc_ref, ol_ref):
    i = pl.program_id(0)
    r = _mod_row(i)

    def body(half, pop):
        x = _after_moe(l, r, x1_ref, (moec_ref, moel_ref)[half], mod_ref)
        ms = jnp.mean(x * x, axis=-1, keepdims=True)
        (oc_ref, ol_ref)[pop][...] = x * lax.rsqrt(ms + EPS) * g_ref[...]

    pl.when(i < MOE_SPLIT)(functools.partial(body, 0, 0))
    pl.when(jnp.logical_and(i >= MOE_SPLIT, i < NT_CTX))(functools.partial(body, 1, 0))
    pl.when(i >= NT_CTX)(functools.partial(body, 1, 1))


def _final(l, x1, moe_c, moe_l, mod, g):
    return pl.pallas_call(
        functools.partial(_final_kernel, l),
        grid=(NT,),
        in_specs=[_tile_spec(D)] + _half_specs(TM * TOK_ROWS, LANES) + [_MOD_SPEC, _full_spec((1, D))],
        out_specs=_pop_specs(TM, D),
        out_shape=[jax.ShapeDtypeStruct((T_CTX, D), F32), jax.ShapeDtypeStruct((T_LAT, D), F32)],
        compiler_params=_params(),
        name="final_norm",
    )(x1, moe_c, moe_l, mod, g)


def _rope_tables():
    t = jnp.arange(LAT_LEN)
    row = (t // GRID_W).astype(F32)
    col = (t % GRID_W).astype(F32)
    inv = ROPE_BASE ** (-jnp.arange(0, ROPE_AXIS_DIM, 2, dtype=F32) / ROPE_AXIS_DIM)

    def table(p):
        ang = p[:, None] * inv[None, :]
        ang = jnp.concatenate([ang, ang], axis=-1)
        return jnp.cos(ang), jnp.sin(ang)

    (cr, sr), (cc, sc) = table(row), table(col)
    cos = jnp.concatenate([cr, cc, cr, cc], axis=-1)
    sin = jnp.concatenate([sr, sc, sr, sc], axis=-1)
    return cos, sin


def _router_weights(w_grp, w_exp):
    we = jnp.transpose(w_exp, (0, 2, 1, 3)).reshape(DEPTH, D, N_GROUPS * EPG)
    pad = jnp.zeros((DEPTH, D, LANES - N_GROUPS - N_GROUPS * EPG), F32)
    wr = jnp.concatenate([w_grp, we, pad], axis=-1)
    hi = lax.bitcast_convert_type(lax.bitcast_convert_type(wr, jnp.uint32) & jnp.uint32(0xFFFF0000), F32)
    return jnp.stack([hi.astype(BF16), (wr - hi).astype(BF16)], axis=1)


def kernel(x_prompt, x_sample, cache_k, cache_v, c, c_ctx, w_ada, b_ada, norm_mix_g, norm_ffn_g, w_in_ab, pool_w, pool_scale, attn_sink, w_out_ab, conv_w1, conv_b1, conv_dw, conv_dw_b, conv_ln_g, conv_ln_b, conv_w2, conv_b2, router_grp, router_exp, moe_w1, moe_w3, moe_w2, final_g):
    xc = x_prompt.reshape(T_CTX, D)
    xl = x_sample.reshape(T_LAT, D)
    cond8 = jnp.concatenate([c_ctx[None, :], c, jnp.zeros((SUBLANES - 1 - N_LAT_SEQ, D), F32)], axis=0)
    mod = _modulation(cond8, w_ada, b_ada)
    cos_t, sin_t = _rope_tables()
    wr = _router_weights(router_grp, router_exp)
    n_even, n_odd = w_in_ab.shape[0], conv_w1.shape[0]
    past = cache_k.shape[2]

    a, q, k, v = _in_even(0, 0, xc, xl, mod, norm_mix_g, w_in_ab, cos_t, sin_t)
    yb_c = _attn_ctx(0, attn_sink, q, k, v)
    yb_l = _attn_lat(0, attn_sink, q, k, v, cache_k[:, 0].reshape(N_LAT_SEQ, past, KV_DIM),
                     cache_v[:, 0].reshape(N_LAT_SEQ, past, KV_DIM))
    x1, h2, *routing = _mid_even(0, 0, a, yb_c, yb_l, xc, xl, mod, pool_w,
                                 pool_scale.reshape(n_even, 1, POOL_DIM), w_out_ab, norm_ffn_g, wr)
    moe0 = _moe_all(0, h2, *routing, moe_w1, moe_w3, moe_w2)

    vec = lambda p: p.reshape(n_odd, 1, -1)
    u = _in_odd(1, 0, x1, *moe0, mod, norm_mix_g, conv_w1, vec(conv_b1))
    x3, h2b, *routing = _mid_odd(1, 0, u, x1, *moe0, mod, conv_dw, vec(conv_dw_b), vec(conv_ln_g),
                                 vec(conv_ln_b), conv_w2, vec(conv_b2), norm_ffn_g, wr)
    moe1 = _moe_all(1, h2b, *routing, moe_w1, moe_w3, moe_w2)
    y_c, y_l = _final(1, x3, *moe1, mod, final_g.reshape(1, D))

    y_prompt = y_c.reshape(N_CTX_SEQ, CTX_LEN, D)
    y_sample = y_l.reshape(N_LAT_SEQ, LAT_LEN, D)
    state_k = k[:T_CTX].reshape(N_CTX_SEQ, 1, CTX_LEN, N_KV_HEADS, HEAD_DIM)
    state_v = v[:T_CTX].reshape(N_CTX_SEQ, 1, CTX_LEN, N_KV_HEADS, HEAD_DIM)
    return (y_prompt, y_sample, state_k, state_v)
```

```python
import functools

import jax
import jax.numpy as jnp
from jax import lax
from jax.experimental import pallas as pl
from jax.experimental.pallas import tpu as pltpu

F32 = jnp.float32
BF16 = jnp.bfloat16
I32 = jnp.int32

D = 1024
N_CTX_SEQ = 16
CTX_LEN = 256
N_LAT_SEQ = 2
LAT_LEN = 1024
T_CTX = N_CTX_SEQ * CTX_LEN
T_LAT = N_LAT_SEQ * LAT_LEN
T_ALL = T_CTX + T_LAT
TM = 256
NT = T_ALL // TM
NT_CTX = T_CTX // TM
LAT_TILES_PER_SEQ = LAT_LEN // TM
GRID_W = 64
DEPTH = 2

POOL_WINDOWS = (2, 4, 8, 16)
POOL_GROUP_DIM = 128
POOL_DIM = 512
HEAD_DIM = 64
N_Q_HEADS = 8
N_KV_HEADS = 2
Q_PER_KV = 4
Q_DIM = 512
KV_DIM = 128
IN_AB = 1280
ATTN_WINDOW = 128
ATTN_BLOCK = 128
ATTN_SCALE = HEAD_DIM ** -0.5
ROPE_BASE = 10000.0
ROPE_AXIS_DIM = 32
CONV_WIDTH = 31
CONV_HALO = 16
POOL_HALO = 8
N_GROUPS = 4
EPG = 4
N_EXPERTS = 16
EXPERT_HIDDEN = 512
EPS = 1e-6
NEG_BIG = -1e30

SUBLANES = 8
LANES = 128
TOK_ROWS = D // LANES

MOE_TM = 256
MOE_SPLIT = NT // 2
T_HALF = MOE_SPLIT * TM
MOE_TILES = 2 * (2 * T_HALF // MOE_TM + N_EXPERTS)
HALF_SHIFT = 8
HALF_UNIT = 1 << HALF_SHIFT
W_FIRST_UNIT = 2
W_NEXT_UNIT = 4
MOVE_BATCH_LOG2 = 4
MOVE_BATCH = 1 << MOVE_BATCH_LOG2
VMEM_LIMIT = 56 * 1024 * 1024


def _silu(x):
    return x * (1.0 / (1.0 + jnp.exp(-x)))


def _mod_row(i):
    return jnp.where(i < NT_CTX, 0, 1 + (i - NT_CTX) // LAT_TILES_PER_SEQ)


def _seq_info(i):
    is_ctx = i < NT_CTX
    k = (i - NT_CTX) % LAT_TILES_PER_SEQ
    off = jnp.where(is_ctx, 0, k * TM)
    n = jnp.where(is_ctx, CTX_LEN, LAT_LEN)
    first = jnp.logical_or(is_ctx, k == 0)
    last = jnp.logical_or(is_ctx, k == LAT_TILES_PER_SEQ - 1)
    return off, n, first, last


def _rms_mod(x, g, scale, shift):
    ms = jnp.mean(x * x, axis=-1, keepdims=True)
    return (x * lax.rsqrt(ms + EPS) * g) * (1.0 + scale) + shift


def _dot(a, b):
    return jnp.dot(a.astype(BF16), b.astype(BF16), preferred_element_type=F32)


def _dot_nt(a, b):
    return lax.dot_general(a.astype(BF16), b.astype(BF16), (((1,), (1,)), ((), ())),
                           preferred_element_type=F32)


def _from_token_tiles(ref):
    return jnp.concatenate([ref[pl.ds(s, TM, stride=TOK_ROWS), :] for s in range(TOK_ROWS)], axis=1)


def _tile_spec(width):
    return pl.BlockSpec((TM, width), lambda i: (i, 0))


def _full_spec(shape):
    return pl.BlockSpec(shape, lambda i: (0,) * len(shape))


def _layer_spec(shape, l):
    return pl.BlockSpec((None,) + tuple(shape), lambda i: (l,) + (0,) * len(shape))


def _pop_specs(rows, width):
    return [pl.BlockSpec((rows, width), lambda i: (jnp.minimum(i, NT_CTX - 1), 0)),
            pl.BlockSpec((rows, width), lambda i: (jnp.maximum(i - NT_CTX, 0), 0))]


def _per_population(i, body):
    pl.when(i < NT_CTX)(functools.partial(body, 0))
    pl.when(i >= NT_CTX)(functools.partial(body, 1))


_TOKEN_TILES_SPEC = pl.BlockSpec((TM * TOK_ROWS, LANES), lambda i: (i, 0))


_MOD_SPEC = pl.BlockSpec((DEPTH, SUBLANES, 6 * D), lambda i: (0, 0, 0))


def _params():
    return pltpu.CompilerParams(vmem_limit_bytes=VMEM_LIMIT)


MOD_TN = 1536


def _mod_kernel(cond_ref, w_ref, b_ref, o_ref):
    s = _silu(cond_ref[...])
    o_ref[0] = _dot(s, w_ref[0]) + b_ref[0]


def _modulation(cond8, w_ada, b_ada):
    return pl.pallas_call(
        _mod_kernel,
        grid=(DEPTH, 6 * D // MOD_TN),
        in_specs=[
            pl.BlockSpec((SUBLANES, D), lambda l, n: (0, 0)),
            pl.BlockSpec((1, D, MOD_TN), lambda l, n: (l, 0, n)),
            pl.BlockSpec((1, 1, MOD_TN), lambda l, n: (l, 0, n)),
        ],
        out_specs=pl.BlockSpec((1, SUBLANES, MOD_TN), lambda l, n: (l, 0, n)),
        out_shape=jax.ShapeDtypeStruct((DEPTH, SUBLANES, 6 * D), F32),
        compiler_params=_params(),
        name="modulation",
    )(cond8, w_ada, b_ada.reshape(DEPTH, 1, 6 * D))


def _rope_chunk(xc, cos, sin):
    lane = lax.broadcasted_iota(I32, xc.shape, 1)
    first = (lane % ROPE_AXIS_DIM) < (ROPE_AXIS_DIM // 2)
    rot = jnp.where(first, -pltpu.roll(xc, LANES - ROPE_AXIS_DIM // 2, 1), pltpu.roll(xc, ROPE_AXIS_DIM // 2, 1))
    return xc * cos + rot * sin


def _in_even_kernel(l, xc_ref, xl_ref, mod_ref, g_ref, w_ref, cos_ref, sin_ref, a_ref, q_ref, k_ref, v_ref):
    i = pl.program_id(0)
    r = _mod_row(i)

    def body(pop):
        x = (xc_ref, xl_ref)[pop][...]
        shift = mod_ref[l, pl.ds(r, 1), 0:D]
        scale = mod_ref[l, pl.ds(r, 1), D:2 * D]
        h = _rms_mod(x, g_ref[l:l + 1, :], scale, shift)
        proj = _dot(h, w_ref[...])
        a_ref[...] = proj[:, :POOL_DIM]
        v_ref[...] = proj[:, POOL_DIM + Q_DIM + KV_DIM:]
        if pop == 0:
            q_ref[...] = proj[:, POOL_DIM:POOL_DIM + Q_DIM]
            k_ref[...] = proj[:, POOL_DIM + Q_DIM:POOL_DIM + Q_DIM + KV_DIM]
        else:
            off, _, _, _ = _seq_info(i)
            off = pl.multiple_of(off, TM)
            cos = cos_ref[pl.ds(off, TM), :]
            sin = sin_ref[pl.ds(off, TM), :]
            for c in range(Q_DIM // LANES):
                lo = POOL_DIM + c * LANES
                q_ref[:, c * LANES:(c + 1) * LANES] = _rope_chunk(proj[:, lo:lo + LANES], cos, sin)
            lo = POOL_DIM + Q_DIM
            k_ref[...] = _rope_chunk(proj[:, lo:lo + LANES], cos, sin)

    _per_population(i, body)


def _in_even(l, e, xc, xl, mod, g, w_in, cos_t, sin_t):
    return pl.pallas_call(
        functools.partial(_in_even_kernel, l),
        grid=(NT,),
        in_specs=_pop_specs(TM, D) + [_MOD_SPEC, _full_spec((DEPTH, D)), _layer_spec((D, IN_AB), e),
                  _full_spec((LAT_LEN, LANES)), _full_spec((LAT_LEN, LANES))],
        out_specs=[_tile_spec(POOL_DIM), _tile_spec(Q_DIM), _tile_spec(KV_DIM), _tile_spec(KV_DIM)],
        out_shape=[jax.ShapeDtypeStruct((T_ALL, POOL_DIM), F32), jax.ShapeDtypeStruct((T_ALL, Q_DIM), F32),
                   jax.ShapeDtypeStruct((T_ALL, KV_DIM), F32), jax.ShapeDtypeStruct((T_ALL, KV_DIM), F32)],
        compiler_params=_params(),
        name="in_even",
    )(xc, xl, mod, g, w_in, cos_t, sin_t)


def _attn_ctx_kernel(e, sink_ref, q_ref, k_ref, v_ref, o_ref):
    q = q_ref[...]
    k = k_ref[...]
    v = v_ref[...]
    for h in range(N_KV_HEADS):
        kh = k[:, h * HEAD_DIM:(h + 1) * HEAD_DIM]
        vh = v[:, h * HEAD_DIM:(h + 1) * HEAD_DIM]
        for g in range(Q_PER_KV):
            j = h * Q_PER_KV + g
            s = _dot_nt(q[:, j * HEAD_DIM:(j + 1) * HEAD_DIM], kh) * ATTN_SCALE
            sk = sink_ref[e, j]
            m = jnp.maximum(jnp.max(s, axis=-1, keepdims=True), sk)
            p = jnp.exp(s - m)
            denom = jnp.sum(p, axis=-1, keepdims=True) + jnp.exp(sk - m)
            o_ref[:, j * HEAD_DIM:(j + 1) * HEAD_DIM] = _dot(p, vh) / denom


def _attn_ctx(e, sink, q, k, v):
    tile = lambda w: pl.BlockSpec((TM, w), lambda b: (b, 0))
    return pl.pallas_call(
        functools.partial(_attn_ctx_kernel, e),
        grid=(N_CTX_SEQ,),
        in_specs=[pl.BlockSpec(memory_space=pltpu.SMEM), tile(Q_DIM), tile(KV_DIM), tile(KV_DIM)],
        out_specs=tile(Q_DIM),
        out_shape=jax.ShapeDtypeStruct((T_CTX, Q_DIM), F32),
        compiler_params=_params(),
        name="attn_ctx",
    )(sink, q, k, v)


LAT_BLOCKS = LAT_LEN // ATTN_BLOCK
BAND = 3 * ATTN_BLOCK


def _attn_lat_kernel(e, sink_ref, q_ref, k_ref, v_ref, ck_ref, cv_ref, o_ref):
    n = pl.program_id(1)
    start = jnp.clip((n - 1) * ATTN_BLOCK, 0, LAT_LEN - BAND)
    start = pl.multiple_of(start, ATTN_BLOCK)
    q = q_ref[...]
    kw = k_ref[pl.ds(start, BAND), :]
    vw = v_ref[pl.ds(start, BAND), :]
    ck = ck_ref[0]
    cv = cv_ref[0]
    qpos = n * ATTN_BLOCK + lax.broadcasted_iota(I32, (ATTN_BLOCK, BAND), 0)
    kpos = start + lax.broadcasted_iota(I32, (ATTN_BLOCK, BAND), 1)
    valid = jnp.abs(qpos - kpos) <= ATTN_WINDOW
    for h in range(N_KV_HEADS):
        hs = slice(h * HEAD_DIM, (h + 1) * HEAD_DIM)
        for g in range(Q_PER_KV):
            j = h * Q_PER_KV + g
            qj = q[:, j * HEAD_DIM:(j + 1) * HEAD_DIM]
            s_loc = jnp.where(valid, _dot_nt(qj, kw[:, hs]) * ATTN_SCALE, NEG_BIG)
            s_ctx = _dot_nt(qj, ck[:, hs]) * ATTN_SCALE
            sk = sink_ref[e, j]
            m = jnp.maximum(jnp.maximum(jnp.max(s_loc, axis=-1, keepdims=True),
                                        jnp.max(s_ctx, axis=-1, keepdims=True)), sk)
            p_loc = jnp.exp(s_loc - m)
            p_ctx = jnp.exp(s_ctx - m)
            denom = (jnp.sum(p_loc, axis=-1, keepdims=True) + jnp.sum(p_ctx, axis=-1, keepdims=True)
                     + jnp.exp(sk - m))
            o = _dot(p_loc, vw[:, hs]) + _dot(p_ctx, cv[:, hs])
            o_ref[:, j * HEAD_DIM:(j + 1) * HEAD_DIM] = o / denom


def _attn_lat(e, sink, q, k, v, ck, cv):
    past = ck.shape[1]
    q_blk0 = T_CTX // ATTN_BLOCK
    kv_blk0 = T_CTX // LAT_LEN
    q_spec = pl.BlockSpec((ATTN_BLOCK, Q_DIM), lambda b, n: (q_blk0 + b * LAT_BLOCKS + n, 0))
    return pl.pallas_call(
        functools.partial(_attn_lat_kernel, e),
        grid=(N_LAT_SEQ, LAT_BLOCKS),
        in_specs=[pl.BlockSpec(memory_space=pltpu.SMEM),
                  q_spec,
                  pl.BlockSpec((LAT_LEN, KV_DIM), lambda b, n: (kv_blk0 + b, 0)),
                  pl.BlockSpec((LAT_LEN, KV_DIM), lambda b, n: (kv_blk0 + b, 0)),
                  pl.BlockSpec((1, past, KV_DIM), lambda b, n: (b, 0, 0)),
                  pl.BlockSpec((1, past, KV_DIM), lambda b, n: (b, 0, 0))],
        out_specs=pl.BlockSpec((ATTN_BLOCK, Q_DIM), lambda b, n: (b * LAT_BLOCKS + n, 0)),
        out_shape=jax.ShapeDtypeStruct((T_LAT, Q_DIM), F32),
        compiler_params=_params(),
        name="attn_lat",
    )(sink, q, k, v, ck, cv)


def _router(h2, wr_ref):
    hi = h2.astype(BF16)
    lo = (h2 - hi.astype(F32)).astype(BF16)
    whi = wr_ref[0]
    wlo = wr_ref[1]
    dot = functools.partial(jnp.dot, preferred_element_type=F32)
    logits = dot(hi, whi) + (dot(lo, whi) + dot(hi, wlo))
    lane = lax.broadcasted_iota(I32, logits.shape, 1).astype(F32)
    neg = jnp.float32(-jnp.inf)
    big = jnp.float32(1e9)
    is_grp = lane < N_GROUPS
    gl = jnp.where(is_grp, logits, neg)
    gmax = jnp.max(gl, axis=-1, keepdims=True)
    gsum = jnp.sum(jnp.where(is_grp, jnp.exp(logits - gmax), 0.0), axis=-1, keepdims=True)
    g_w = 1.0 / gsum
    g_idx = jnp.min(jnp.where(gl == gmax, lane, big), axis=-1, keepdims=True)
    base = N_GROUPS + EPG * g_idx
    in_grp = jnp.logical_and(lane >= base, lane < base + EPG)
    el = jnp.where(in_grp, logits, neg)
    t1 = jnp.max(el, axis=-1, keepdims=True)
    i1 = jnp.min(jnp.where(el == t1, lane, big), axis=-1, keepdims=True)
    el2 = jnp.where(lane == i1, neg, el)
    t2 = jnp.max(el2, axis=-1, keepdims=True)
    i2 = jnp.min(jnp.where(el2 == t2, lane, big), axis=-1, keepdims=True)
    d = jnp.exp(t2 - t1)
    w1 = g_w / (1.0 + d)
    w2 = g_w * d / (1.0 + d)
    return i1 - N_GROUPS, i2 - N_GROUPS, w1, w2


LIST_SPAN_LOG2 = 5
LIST_SPAN = 1 << LIST_SPAN_LOG2
LIST_ROWS = LANES
LIST_COLS = N_EXPERTS * LIST_SPAN


N_TAIL_OUT = 7
N_TAIL_SCRATCH = 3


def _mixer_tail(l, x, y, mod_ref, r, gffn_ref, wr_ref, outs, scratch):
    x1_ref, h2_ref, cnt_ref, lc_ref, gc_ref, ll_ref, gl_ref = outs
    run_ref, acc_ref, gacc_ref = scratch
    i = pl.program_id(0)
    g1 = mod_ref[l, pl.ds(r, 1), 2 * D:3 * D]
    shift2 = mod_ref[l, pl.ds(r, 1), 3 * D:4 * D]
    scale2 = mod_ref[l, pl.ds(r, 1), 4 * D:5 * D]
    x1 = x + g1 * y
    x1_ref[...] = x1
    h2 = _rms_mod(x1, gffn_ref[l:l + 1, :], scale2, shift2)
    for s in range(TOK_ROWS):
        h2_ref[pl.ds(s, TM, stride=TOK_ROWS), :] = h2[:, s * LANES:(s + 1) * LANES]
    e1, e2, w1, w2 = _router(h2, wr_ref)

    @pl.when(jnp.logical_or(i == 0, i == MOE_SPLIT))
    def _():
        run_ref[...] = jnp.zeros(run_ref.shape, F32)
        acc_ref[...] = jnp.zeros(acc_ref.shape, F32)
        gacc_ref[...] = jnp.zeros(gacc_ref.shape, F32)

    @pl.when(i == 0)
    def _():
        cnt_ref[...] = jnp.zeros(cnt_ref.shape, I32)

    lane_i = lax.broadcasted_iota(I32, (TM, LANES), 1)
    lane = lane_i.astype(F32)
    member = jnp.where(jnp.logical_or(lane == e1, lane == e2), 1.0, 0.0)
    tri = jnp.where(lax.broadcasted_iota(I32, (TM, TM), 0) >= lax.broadcasted_iota(I32, (TM, TM), 1), 1.0, 0.0)
    csum = jnp.dot(tri.astype(BF16), member.astype(BF16), preferred_element_type=F32)
    run = run_ref[0:1, :]
    before = csum - member + run
    r1 = jnp.sum(jnp.where(lane == e1, before, 0.0), axis=-1, keepdims=True)
    r2 = jnp.sum(jnp.where(lane == e2, before, 0.0), axis=-1, keepdims=True)
    run_new = run + csum[TM - 1:TM, :]
    run_ref[0:1, :] = run_new

    col_i = lax.broadcasted_iota(I32, (TM, LIST_COLS), 1)
    rows_oh, cols_oh = [], []
    for e, rank in ((e1, r1), (e2, r2)):
        rank_i = rank.astype(I32)
        rows_oh.append(jnp.where(lane_i == rank_i // LIST_SPAN, 1.0, 0.0))
        cols_oh.append(jnp.where(col_i == e.astype(I32) * LIST_SPAN + rank_i % LIST_SPAN, 1.0, 0.0).astype(BF16))
    rows_t = jnp.concatenate(rows_oh, axis=0).T
    cols = jnp.concatenate(cols_oh, axis=0)
    tok = (lax.broadcasted_iota(I32, (1, 2 * TM), 1) % TM).astype(F32)
    dot = functools.partial(jnp.dot, preferred_element_type=F32)
    hits = dot(rows_t.astype(BF16), cols)
    tok_sum = dot((rows_t * tok).astype(BF16), cols)
    tile_in_pop = jnp.where(i < MOE_SPLIT, i, i - MOE_SPLIT).astype(F32)
    acc = acc_ref[...] + TOK_ROWS * (tok_sum + TM * tile_in_pop * hits)
    acc_ref[...] = acc

    rec = jnp.where(lane_i == 0, w1, jnp.where(lane_i == 1, w2, 0.0)).T
    rest = jnp.concatenate([rec[0:1, :], rec[1:2, :]], axis=1)
    gacc = gacc_ref[...]
    for _ in range(3):
        piece = rest.astype(BF16)
        gacc = gacc + dot((rows_t * piece.astype(F32)).astype(BF16), cols)
        rest = rest - piece.astype(F32)
    gacc_ref[...] = gacc

    @pl.when(i == MOE_SPLIT - 1)
    def _():
        cnt_ref[0:1, :] = run_new.astype(I32)
        lc_ref[...] = acc.astype(I32)
        gc_ref[...] = gacc

    @pl.when(i == NT - 1)
    def _():
        cnt_ref[1:2, :] = run_new.astype(I32)
        ll_ref[...] = acc.astype(I32)
        gl_ref[...] = gacc


_TAIL_OUT_SHAPES = [jax.ShapeDtypeStruct((T_ALL, D), F32),
                    jax.ShapeDtypeStruct((T_ALL * TOK_ROWS, LANES), F32),
                    jax.ShapeDtypeStruct((SUBLANES, LANES), I32)] + [
                        jax.ShapeDtypeStruct((LIST_ROWS, LIST_COLS), dt) for dt in (I32, F32, I32, F32)]


def _tail_scratch():
    return [pltpu.VMEM((SUBLANES, LANES), F32), pltpu.VMEM((LIST_ROWS, LIST_COLS), F32),
            pltpu.VMEM((LIST_ROWS, LIST_COLS), F32)]


def _tail_out_specs():
    return [_tile_spec(D),
            pl.BlockSpec((TM * TOK_ROWS, LANES), lambda i: (i, 0)),
            _full_spec((SUBLANES, LANES))] + [_full_spec((LIST_ROWS, LIST_COLS)) for _ in range(4)]


def _halo_specs(halo, width):
    per = TM // halo
    last = T_ALL // halo - 1
    prev = pl.BlockSpec((halo, width), lambda i: (jnp.maximum(i * per - 1, 0), 0))
    nxt = pl.BlockSpec((halo, width), lambda i: (jnp.minimum((i + 1) * per, last), 0))
    return prev, nxt


def _mid_even_kernel(l, a_ref, ap_ref, an_ref, ybc_ref, ybl_ref, xc_ref, xl_ref, mod_ref, wp_ref, ps_ref, wo_ref,
                     gffn_ref, wr_ref, *refs):
    outs, (pad_ref,), scratch = refs[:N_TAIL_OUT], refs[N_TAIL_OUT:-N_TAIL_SCRATCH], refs[-N_TAIL_SCRATCH:]
    i = pl.program_id(0)
    r = _mod_row(i)
    off, n, first, last = _seq_info(i)
    a = a_ref[...]
    pad_ref[0:POOL_HALO, :] = jnp.where(first, 0.0, ap_ref[...])
    pad_ref[POOL_HALO:POOL_HALO + TM, :] = a
    pad_ref[POOL_HALO + TM:, :] = jnp.where(last, 0.0, an_ref[...])
    t = off + lax.broadcasted_iota(I32, (TM, 1), 0)
    mixed = []
    for g, w in enumerate(POOL_WINDOWS):
        lo = w // 2
        hi = w - lo - 1
        cols = slice(g * POOL_GROUP_DIM, (g + 1) * POOL_GROUP_DIM)
        total = pad_ref[POOL_HALO - lo:POOL_HALO - lo + TM, cols]
        for j in range(-lo + 1, hi + 1):
            total = total + pad_ref[POOL_HALO + j:POOL_HALO + j + TM, cols]
        count = (jnp.minimum(t + hi, n - 1) - jnp.maximum(t - lo, 0) + 1).astype(F32)
        pooled = total / count - a[:, cols]
        mixed.append(_dot(pooled, wp_ref[g]))
    ya = jnp.concatenate(mixed, axis=1) * ps_ref[...]
    y_pool = _dot(ya, wo_ref[0:POOL_DIM, :])

    def body(pop):
        y = y_pool + _dot((ybc_ref, ybl_ref)[pop][...], wo_ref[POOL_DIM:, :])
        _mixer_tail(l, (xc_ref, xl_ref)[pop][...], y, mod_ref, r, gffn_ref, wr_ref, outs, scratch)

    _per_population(i, body)


def _mid_even(l, e, a, yb_c, yb_l, xc, xl, mod, w_pool, pool_scale, w_out, gffn, wr):
    prev, nxt = _halo_specs(POOL_HALO, POOL_DIM)
    return pl.pallas_call(
        functools.partial(_mid_even_kernel, l),
        grid=(NT,),
        in_specs=[_tile_spec(POOL_DIM), prev, nxt] + _pop_specs(TM, Q_DIM) + _pop_specs(TM, D) + [
                  _MOD_SPEC, _layer_spec((len(POOL_WINDOWS), POOL_GROUP_DIM, POOL_GROUP_DIM), e),
                  _layer_spec((1, POOL_DIM), e), _layer_spec((D, D), e), _full_spec((DEPTH, D)),
                  _layer_spec((2, D, LANES), l)],
        out_specs=_tail_out_specs(),
        out_shape=_TAIL_OUT_SHAPES,
        scratch_shapes=[pltpu.VMEM((TM + 2 * POOL_HALO, POOL_DIM), F32)] + _tail_scratch(),
        compiler_params=_params(),
        name="mid_even",
    )(a, a, a, yb_c, yb_l, xc, xl, mod, w_pool, pool_scale, w_out, gffn, wr)


def _after_moe(l, r, x1_ref, moe_ref, mod_ref):
    return x1_ref[...] + mod_ref[l, pl.ds(r, 1), 5 * D:6 * D] * _from_token_tiles(moe_ref)


def _in_odd_kernel(l, x1_ref, moe_ref, mod_ref, g_ref, w1_ref, b1_ref, u_ref):
    r = _mod_row(pl.program_id(0))
    x2 = _after_moe(l - 1, r, x1_ref, moe_ref, mod_ref)
    shift = mod_ref[l, pl.ds(r, 1), 0:D]
    scale = mod_ref[l, pl.ds(r, 1), D:2 * D]
    h = _rms_mod(x2, g_ref[l:l + 1, :], scale, shift)
    u = _dot(h, w1_ref[...]) + b1_ref[...]
    u_ref[...] = u[:, :D] * (1.0 / (1.0 + jnp.exp(-u[:, D:])))


def _in_odd(l, o, x1, moe, mod, g, w1, b1):
    return pl.pallas_call(
        functools.partial(_in_odd_kernel, l),
        grid=(NT,),
        in_specs=[_tile_spec(D), _TOKEN_TILES_SPEC,
                  _MOD_SPEC, _full_spec((DEPTH, D)), _layer_spec((D, 2 * D), o), _layer_spec((1, 2 * D), o)],
        out_specs=_tile_spec(D),
        out_shape=jax.ShapeDtypeStruct((T_ALL, D), F32),
        compiler_params=_params(),
        name="in_odd",
    )(x1, moe, mod, g, w1, b1)


CONV_PAD_ROWS = TM + 2 * CONV_HALO
CONV_CHUNKS = D // LANES
CONV_BLOCK = 64


def _mid_odd_kernel(l, u_ref, up_ref, un_ref, x1_ref, moe_ref, mod_ref, dw_ref, dwb_ref, lng_ref, lnb_ref,
                    w2_ref, b2_ref, gffn_ref, wr_ref, *refs):
    outs, scratch = refs[:N_TAIL_OUT], refs[-N_TAIL_SCRATCH:]
    pad_ref, conv_ref = refs[N_TAIL_OUT:-N_TAIL_SCRATCH]
    i = pl.program_id(0)
    r = _mod_row(i)
    _, _, first, last = _seq_info(i)
    u = u_ref[...]
    up = jnp.where(first, 0.0, up_ref[...])
    un = jnp.where(last, 0.0, un_ref[...])
    half = CONV_WIDTH // 2
    first_row = CONV_HALO - half
    for c in range(CONV_CHUNKS):
        cols = slice(c * LANES, (c + 1) * LANES)
        pad_ref[c, 0:CONV_HALO, :] = up[:, cols]
        pad_ref[c, CONV_HALO:CONV_HALO + TM, :] = u[:, cols]
        pad_ref[c, CONV_HALO + TM:, :] = un[:, cols]
        bias = jnp.broadcast_to(dwb_ref[:, cols], (CONV_BLOCK, LANES))
        for r0 in range(0, TM, 2 * CONV_BLOCK):
            acc_even, acc_odd = bias, bias
            for s in range(first_row, first_row + CONV_WIDTH + 1):
                win = pad_ref[c, pl.ds(r0 + s, CONV_BLOCK, stride=2), :]
                j = s - first_row
                if j < CONV_WIDTH:
                    acc_even = acc_even + win * dw_ref[j:j + 1, cols]
                if j >= 1:
                    acc_odd = acc_odd + win * dw_ref[j - 1:j, cols]
            conv_ref[c, pl.ds(r0, CONV_BLOCK, stride=2), :] = acc_even
            conv_ref[c, pl.ds(r0 + 1, CONV_BLOCK, stride=2), :] = acc_odd
    acc = jnp.concatenate([conv_ref[c] for c in range(CONV_CHUNKS)], axis=1)
    mu = jnp.mean(acc, axis=-1, keepdims=True)
    cen = acc - mu
    var = jnp.mean(cen * cen, axis=-1, keepdims=True)
    v = _silu(cen * lax.rsqrt(var + EPS) * lng_ref[...] + lnb_ref[...])
    y = _dot(v, w2_ref[...]) + b2_ref[...]
    x = _after_moe(l - 1, r, x1_ref, moe_ref, mod_ref)
    _mixer_tail(l, x, y, mod_ref, r, gffn_ref, wr_ref, outs, scratch)


def _mid_odd(l, o, u, x1, moe, mod, dw, dwb, lng, lnb, w2, b2, gffn, wr):
    prev, nxt = _halo_specs(CONV_HALO, D)
    return pl.pallas_call(
        functools.partial(_mid_odd_kernel, l),
        grid=(NT,),
        in_specs=[_tile_spec(D), prev, nxt, _tile_spec(D), _TOKEN_TILES_SPEC,
                  _MOD_SPEC, _layer_spec((CONV_WIDTH, D), o),
                  _layer_spec((1, D), o), _layer_spec((1, D), o), _layer_spec((1, D), o),
                  _layer_spec((D, D), o), _layer_spec((1, D), o), _full_spec((DEPTH, D)),
                  _layer_spec((2, D, LANES), l)],
        out_specs=_tail_out_specs(),
        out_shape=_TAIL_OUT_SHAPES,
        scratch_shapes=[pltpu.VMEM((CONV_CHUNKS, CONV_PAD_ROWS, LANES), F32),
                        pltpu.VMEM((CONV_CHUNKS, TM, LANES), F32)] + _tail_scratch(),
        compiler_params=_params(),
        name="mid_odd",
    )(u, u, u, x1, moe, mod, dw, dwb, lng, lnb, w2, b2, gffn, wr)


def _tile_tables_kernel(cnt_ref, te_ref, tp_ref, tw_ref, nact_ref, first_ref, wplan_ref):
    n_groups = 2 * N_EXPERTS
    run = jnp.int32(0)
    ends = []
    for g in range(n_groups):
        first_ref[g] = run
        run = run + (cnt_ref[g % 2, g // 2] + MOE_TM - 1) // MOE_TM
        ends.append(run)
    nact_ref[0] = run

    used = [cnt_ref[0, e] + cnt_ref[1, e] > 0 for e in range(N_EXPERTS)]
    nxt = jnp.int32(0)
    plans = []
    for e in reversed(range(N_EXPERTS)):
        plans.append(nxt)
        nxt = jnp.where(used[e], e + 1, nxt)
    plans.reverse()
    order = jnp.int32(0)
    for e in range(N_EXPERTS):
        wplan_ref[e] = (order & 1) + W_NEXT_UNIT * plans[e]
        order = order + jnp.where(used[e], 1, 0)

    def group_of(tile):
        g = jnp.int32(0)
        for k in range(n_groups):
            g = g + jnp.where(tile >= ends[k], 1, 0)
        return g

    last_g = group_of(run - 1)

    def tile_body(j, c):
        g = jnp.minimum(group_of(j), last_g)
        e = g >> 1
        chunk = jnp.where(j < run, j - first_ref[g], 0)
        te_ref[j] = e
        tp_ref[j] = (g & 1) * HALF_UNIT + chunk
        first_group = 2 * e + jnp.where(cnt_ref[0, e] > 0, 0, 1)
        starts_expert = jnp.logical_and(jnp.logical_and(j < run, chunk == 0), g == first_group)
        tw_ref[j] = wplan_ref[e] + jnp.where(starts_expert, W_FIRST_UNIT, 0)
        return c

    lax.fori_loop(0, te_ref.shape[0], tile_body, 0)


def _tile_tables(cnt):
    smem = pl.BlockSpec(memory_space=pltpu.SMEM)
    shapes = [(MOE_TILES,)] * 3 + [(1,)]
    return pl.pallas_call(
        _tile_tables_kernel,
        in_specs=[smem],
        out_specs=[smem] * len(shapes),
        out_shape=[jax.ShapeDtypeStruct(s, I32) for s in shapes],
        scratch_shapes=[pltpu.SMEM((2 * N_EXPERTS,), I32), pltpu.SMEM((N_EXPERTS,), I32)],
        name="tile_tables",
    )(cnt)


def _moe_kernel(l, te_ref, tp_ref, tw_ref, nact_ref, list_ref, h_hbm, gate_ref, w1_hbm, w3_hbm, w2_hbm,
                o_ref, xs_ref, ys_ref, w1_buf, w3_buf, w2_buf, w_sem, x_sem):
    j = pl.program_id(0)
    nact = nact_ref[0]
    last_tile = te_ref.shape[0] - 1
    tile_rows = MOE_TM * TOK_ROWS
    chunk_rows = MOE_TM // LIST_SPAN

    def weight_copies(expert, buf):
        return [pltpu.make_async_copy(hbm.at[l, expert], vmem.at[buf], w_sem.at[buf, k])
                for k, (hbm, vmem) in enumerate(((w1_hbm, w1_buf), (w3_hbm, w3_buf), (w2_hbm, w2_buf)))]

    def tile_pos(tile):
        half = tp_ref[tile] >> HALF_SHIFT
        chunk = tp_ref[tile] & (HALF_UNIT - 1)
        gate_row = half * LIST_ROWS + chunk * chunk_rows
        return gate_row * LIST_COLS + te_ref[tile] * LIST_SPAN, half * (T_HALF * TOK_ROWS), gate_row

    def tok_rows(pos, r):
        first = pos[1] + list_ref[pos[0] + (r // LIST_SPAN) * LIST_COLS + r % LIST_SPAN]
        return pl.ds(pl.multiple_of(first, TOK_ROWS), TOK_ROWS)

    def buf_rows(buf, r):
        return pl.ds(pl.multiple_of(buf * tile_rows + r * TOK_ROWS, TOK_ROWS), TOK_ROWS)

    def start_gather(tile, buf):
        pos = tile_pos(tile)
        for r in range(MOE_TM):
            pltpu.make_async_copy(h_hbm.at[tok_rows(pos, r), :], xs_ref.at[buf_rows(buf, r), :],
                                  x_sem.at[buf]).start()

    def wait_gather(buf):
        whole = pl.ds(pl.multiple_of(buf * tile_rows, TOK_ROWS), tile_rows)
        pltpu.make_async_copy(h_hbm.at[pl.ds(0, tile_rows), :], xs_ref.at[whole, :], x_sem.at[buf]).wait()

    tile = jnp.minimum(j, last_tile)
    plan = tw_ref[tile]
    wbuf = plan & 1
    starts_expert = jnp.logical_and(plan & W_FIRST_UNIT != 0, j <= last_tile)
    next_expert = (plan >> 2) - 1

    @pl.when(j == 0)
    def _():
        for cp in weight_copies(te_ref[0], 0):
            cp.start()
        start_gather(0, 0)
        o_ref[...] = jnp.zeros(o_ref.shape, F32)
        ys_ref[...] = jnp.zeros(ys_ref.shape, F32)

    @pl.when(starts_expert)
    def _():
        for cp in weight_copies(te_ref[tile], wbuf):
            cp.wait()

        @pl.when(next_expert >= 0)
        def _():
            for cp in weight_copies(next_expert, 1 - wbuf):
                cp.start()

    @pl.when(j <= nact)
    def _():
        cur = j & 1
        w1_ref, w3_ref, w2_ref = w1_buf.at[wbuf], w3_buf.at[wbuf], w2_buf.at[wbuf]
        wait_gather(cur)
        x = jnp.concatenate([xs_ref[pl.ds(cur * tile_rows + s, MOE_TM, stride=TOK_ROWS), :]
                             for s in range(TOK_ROWS)], axis=1).astype(BF16)
        start_gather(jnp.minimum(j + 1, last_tile), 1 - cur)

        prev_pos = tile_pos(jnp.maximum(j - 1, 0))
        for r0 in range(0, MOE_TM, MOVE_BATCH):
            dst = [tok_rows(prev_pos, r0 + k) for k in range(MOVE_BATCH)]
            vals = [o_ref[dst[k], :] + ys_ref[buf_rows(1 - cur, r0 + k), :] for k in range(MOVE_BATCH)]
            for k in reversed(range(MOVE_BATCH)):
                o_ref[dst[k], :] = vals[k]

        hg = jnp.dot(x, w1_ref[...].astype(BF16), preferred_element_type=F32)
        hu = jnp.dot(x, w3_ref[...].astype(BF16), preferred_element_type=F32)
        y = _dot(_silu(hg) * hu, w2_ref[...])

        first_row = pl.multiple_of(tile_pos(tile)[2], chunk_rows)
        spread = jnp.concatenate(
            [jnp.broadcast_to(gate_ref[pl.ds(first_row + k, 1), :], (LIST_SPAN, LIST_COLS))
             for k in range(chunk_rows)], axis=0)
        want = te_ref[tile] * LIST_SPAN + (lax.broadcasted_iota(I32, (MOE_TM, 1), 0) & (LIST_SPAN - 1))
        hit = jnp.logical_and(lax.broadcasted_iota(I32, (MOE_TM, LIST_COLS), 1) == want, j < nact)
        y = y * jnp.sum(jnp.where(hit, spread, 0.0), axis=1, keepdims=True)
        for s in range(TOK_ROWS):
            ys_ref[pl.ds(cur * tile_rows + s, MOE_TM, stride=TOK_ROWS), :] = y[:, s * LANES:(s + 1) * LANES]

    @pl.when(j == nact)
    def _():
        wait_gather(1 - (j & 1))


def _moe(l, h_tt, cnt, rows_a, gate_a, rows_b, gate_b, w1, w3, w2):
    te, tp, tw, nact = _tile_tables(cnt)
    rows = jnp.concatenate([rows_a.reshape(-1), rows_b.reshape(-1)])
    gate = jnp.concatenate([gate_a, gate_b], axis=0)
    hbm = pl.BlockSpec(memory_space=pl.ANY)
    grid_spec = pltpu.PrefetchScalarGridSpec(
        num_scalar_prefetch=5,
        grid=(MOE_TILES + 1,),
        in_specs=[
            hbm,
            pl.BlockSpec((2 * LIST_ROWS, LIST_COLS), lambda j, *_: (0, 0), pipeline_mode=pl.Buffered(1)),
            hbm, hbm, hbm,
        ],
        out_specs=pl.BlockSpec((T_ALL * TOK_ROWS, LANES), lambda j, *_: (0, 0), pipeline_mode=pl.Buffered(1)),
        scratch_shapes=[pltpu.VMEM((2 * MOE_TM * TOK_ROWS, LANES), F32),
                        pltpu.VMEM((2 * MOE_TM * TOK_ROWS, LANES), F32),
                        pltpu.VMEM((2, D, EXPERT_HIDDEN), F32),
                        pltpu.VMEM((2, D, EXPERT_HIDDEN), F32),
                        pltpu.VMEM((2, EXPERT_HIDDEN, D), F32),
                        pltpu.SemaphoreType.DMA((2, 3)),
                        pltpu.SemaphoreType.DMA((2,))],
    )
    return pl.pallas_call(
        functools.partial(_moe_kernel, l),
        grid_spec=grid_spec,
        out_shape=jax.ShapeDtypeStruct((T_ALL * TOK_ROWS, LANES), F32),
        compiler_params=_params(),
        name="moe",
    )(te, tp, tw, nact, rows, h_tt, gate, w1, w3, w2)


def _final_kernel(l, x1_ref, moe_ref, mod_ref, g_ref, oc_ref, ol_ref):
    i = pl.program_id(0)
    r = _mod_row(i)

    def body(pop):
        x = _after_moe(l, r, x1_ref, moe_ref, mod_ref)
        ms = jnp.mean(x * x, axis=-1, keepdims=True)
        (oc_ref, ol_ref)[pop][...] = x * lax.rsqrt(ms + EPS) * g_ref[...]

    _per_population(i, body)


def _final(l, x1, moe, mod, g):
    return pl.pallas_call(
        functools.partial(_final_kernel, l),
        grid=(NT,),
        in_specs=[_tile_spec(D), _TOKEN_TILES_SPEC, _MOD_SPEC, _full_spec((1, D))],
        out_specs=_pop_specs(TM, D),
        out_shape=[jax.ShapeDtypeStruct((T_CTX, D), F32), jax.ShapeDtypeStruct((T_LAT, D), F32)],
        compiler_params=_params(),
        name="final_norm",
    )(x1, moe, mod, g)


def _rope_tables():
    t = jnp.arange(LAT_LEN)
    row = (t // GRID_W).astype(F32)
    col = (t % GRID_W).astype(F32)
    inv = ROPE_BASE ** (-jnp.arange(0, ROPE_AXIS_DIM, 2, dtype=F32) / ROPE_AXIS_DIM)

    def table(p):
        ang = p[:, None] * inv[None, :]
        ang = jnp.concatenate([ang, ang], axis=-1)
        return jnp.cos(ang), jnp.sin(ang)

    (cr, sr), (cc, sc) = table(row), table(col)
    cos = jnp.concatenate([cr, cc, cr, cc], axis=-1)
    sin = jnp.concatenate([sr, sc, sr, sc], axis=-1)
    return cos, sin


def _router_weights(w_grp, w_exp):
    we = jnp.transpose(w_exp, (0, 2, 1, 3)).reshape(DEPTH, D, N_GROUPS * EPG)
    pad = jnp.zeros((DEPTH, D, LANES - N_GROUPS - N_GROUPS * EPG), F32)
    wr = jnp.concatenate([w_grp, we, pad], axis=-1)
    hi = lax.bitcast_convert_type(lax.bitcast_convert_type(wr, jnp.uint32) & jnp.uint32(0xFFFF0000), F32)
    return jnp.stack([hi.astype(BF16), (wr - hi).astype(BF16)], axis=1)


def kernel(x_prompt, x_sample, cache_k, cache_v, c, c_ctx, w_ada, b_ada, norm_mix_g, norm_ffn_g, w_in_ab, pool_w, pool_scale, attn_sink, w_out_ab, conv_w1, conv_b1, conv_dw, conv_dw_b, conv_ln_g, conv_ln_b, conv_w2, conv_b2, router_grp, router_exp, moe_w1, moe_w3, moe_w2, final_g):
    xc = x_prompt.reshape(T_CTX, D)
    xl = x_sample.reshape(T_LAT, D)
    cond8 = jnp.concatenate([c_ctx[None, :], c, jnp.zeros((SUBLANES - 1 - N_LAT_SEQ, D), F32)], axis=0)
    mod = _modulation(cond8, w_ada, b_ada)
    cos_t, sin_t = _rope_tables()
    wr = _router_weights(router_grp, router_exp)
    n_even, n_odd = w_in_ab.shape[0], conv_w1.shape[0]
    past = cache_k.shape[2]

    a, q, k, v = _in_even(0, 0, xc, xl, mod, norm_mix_g, w_in_ab, cos_t, sin_t)
    yb_c = _attn_ctx(0, attn_sink, q, k, v)
    yb_l = _attn_lat(0, attn_sink, q, k, v, cache_k[:, 0].reshape(N_LAT_SEQ, past, KV_DIM),
                     cache_v[:, 0].reshape(N_LAT_SEQ, past, KV_DIM))
    x1, h2, *routing = _mid_even(0, 0, a, yb_c, yb_l, xc, xl, mod, pool_w,
                                 pool_scale.reshape(n_even, 1, POOL_DIM), w_out_ab, norm_ffn_g, wr)
    moe0 = _moe(0, h2, *routing, moe_w1, moe_w3, moe_w2)

    vec = lambda p: p.reshape(n_odd, 1, -1)
    u = _in_odd(1, 0, x1, moe0, mod, norm_mix_g, conv_w1, vec(conv_b1))
    x3, h2b, *routing = _mid_odd(1, 0, u, x1, moe0, mod, conv_dw, vec(conv_dw_b), vec(conv_ln_g),
                                 vec(conv_ln_b), conv_w2, vec(conv_b2), norm_ffn_g, wr)
    moe1 = _moe(1, h2b, *routing, moe_w1, moe_w3, moe_w2)
    y_c, y_l = _final(1, x3, moe1, mod, final_g.reshape(1, D))

    y_prompt = y_c.reshape(N_CTX_SEQ, CTX_LEN, D)
    y_sample = y_l.reshape(N_LAT_SEQ, LAT_LEN, D)
    state_k = k[:T_CTX].reshape(N_CTX_SEQ, 1, CTX_LEN, N_KV_HEADS, HEAD_DIM)
    state_v = v[:T_CTX].reshape(N_CTX_SEQ, 1, CTX_LEN, N_KV_HEADS, HEAD_DIM)
    return (y_prompt, y_sample, state_k, state_v)
```

```python
import functools

import jax
import jax.numpy as jnp
from jax import lax
from jax.experimental import pallas as pl
from jax.experimental.pallas import tpu as pltpu

F32 = jnp.float32
BF16 = jnp.bfloat16
I32 = jnp.int32

D = 1024
N_CTX_SEQ = 16
CTX_LEN = 256
N_LAT_SEQ = 2
LAT_LEN = 1024
T_CTX = N_CTX_SEQ * CTX_LEN
T_LAT = N_LAT_SEQ * LAT_LEN
T_ALL = T_CTX + T_LAT
TM = 256
NT = T_ALL // TM
NT_CTX = T_CTX // TM
LAT_TILES_PER_SEQ = LAT_LEN // TM
GRID_W = 64
DEPTH = 2

POOL_WINDOWS = (2, 4, 8, 16)
POOL_GROUP_DIM = 128
POOL_DIM = 512
HEAD_DIM = 64
N_Q_HEADS = 8
N_KV_HEADS = 2
Q_PER_KV = 4
Q_DIM = 512
KV_DIM = 128
IN_AB = 1280
ATTN_WINDOW = 128
ATTN_BLOCK = 128
ATTN_SCALE = HEAD_DIM ** -0.5
ROPE_BASE = 10000.0
ROPE_AXIS_DIM = 32
CONV_WIDTH = 31
CONV_HALO = 16
POOL_HALO = 8
N_GROUPS = 4
EPG = 4
N_EXPERTS = 16
EXPERT_HIDDEN = 512
EPS = 1e-6
NEG_BIG = -1e30

SUBLANES = 8
LANES = 128
TOK_ROWS = D // LANES

MOE_TM = 256
MOE_SPLIT = NT // 2
T_HALF = MOE_SPLIT * TM
MOE_TILES = 2 * T_HALF // MOE_TM + N_EXPERTS
W_BUFFERS = 3
W_FIRST_UNIT = 4
W_AHEAD_SHIFT = 3
MOVE_BATCH_LOG2 = 4
MOVE_BATCH = 1 << MOVE_BATCH_LOG2
VMEM_LIMIT = 56 * 1024 * 1024


def _silu(x):
    return x * (1.0 / (1.0 + jnp.exp(-x)))


def _mod_row(i):
    return jnp.where(i < NT_CTX, 0, 1 + (i - NT_CTX) // LAT_TILES_PER_SEQ)


def _seq_info(i):
    is_ctx = i < NT_CTX
    k = (i - NT_CTX) % LAT_TILES_PER_SEQ
    off = jnp.where(is_ctx, 0, k * TM)
    n = jnp.where(is_ctx, CTX_LEN, LAT_LEN)
    first = jnp.logical_or(is_ctx, k == 0)
    last = jnp.logical_or(is_ctx, k == LAT_TILES_PER_SEQ - 1)
    return off, n, first, last


def _rms_mod(x, g, scale, shift):
    ms = jnp.mean(x * x, axis=-1, keepdims=True)
    return (x * lax.rsqrt(ms + EPS) * g) * (1.0 + scale) + shift


def _dot(a, b):
    return jnp.dot(a.astype(BF16), b.astype(BF16), preferred_element_type=F32)


def _dot_nt(a, b):
    return lax.dot_general(a.astype(BF16), b.astype(BF16), (((1,), (1,)), ((), ())),
                           preferred_element_type=F32)


def _from_token_tiles(ref):
    return jnp.concatenate([ref[pl.ds(s, TM, stride=TOK_ROWS), :] for s in range(TOK_ROWS)], axis=1)


def _tile_spec(width):
    return pl.BlockSpec((TM, width), lambda i: (i, 0))


def _full_spec(shape):
    return pl.BlockSpec(shape, lambda i: (0,) * len(shape))


def _layer_spec(shape, l):
    return pl.BlockSpec((None,) + tuple(shape), lambda i: (l,) + (0,) * len(shape))


def _pop_specs(rows, width):
    return [pl.BlockSpec((rows, width), lambda i: (jnp.minimum(i, NT_CTX - 1), 0)),
            pl.BlockSpec((rows, width), lambda i: (jnp.maximum(i - NT_CTX, 0), 0))]


def _per_population(i, body):
    pl.when(i < NT_CTX)(functools.partial(body, 0))
    pl.when(i >= NT_CTX)(functools.partial(body, 1))


def _half_specs(rows, width):
    return [pl.BlockSpec((rows, width), lambda i: (jnp.minimum(i, MOE_SPLIT - 1), 0)),
            pl.BlockSpec((rows, width), lambda i: (jnp.maximum(i - MOE_SPLIT, 0), 0))]


def _per_half(i, body):
    pl.when(i < MOE_SPLIT)(functools.partial(body, 0))
    pl.when(i >= MOE_SPLIT)(functools.partial(body, 1))


_MOD_SPEC = pl.BlockSpec((DEPTH, SUBLANES, 6 * D), lambda i: (0, 0, 0))


def _params():
    return pltpu.CompilerParams(vmem_limit_bytes=VMEM_LIMIT)


MOD_TN = 1536


def _mod_kernel(cond_ref, w_ref, b_ref, o_ref):
    s = _silu(cond_ref[...])
    o_ref[0] = _dot(s, w_ref[0]) + b_ref[0]


def _modulation(cond8, w_ada, b_ada):
    return pl.pallas_call(
        _mod_kernel,
        grid=(DEPTH, 6 * D // MOD_TN),
        in_specs=[
            pl.BlockSpec((SUBLANES, D), lambda l, n: (0, 0)),
            pl.BlockSpec((1, D, MOD_TN), lambda l, n: (l, 0, n)),
            pl.BlockSpec((1, 1, MOD_TN), lambda l, n: (l, 0, n)),
        ],
        out_specs=pl.BlockSpec((1, SUBLANES, MOD_TN), lambda l, n: (l, 0, n)),
        out_shape=jax.ShapeDtypeStruct((DEPTH, SUBLANES, 6 * D), F32),
        compiler_params=_params(),
        name="modulation",
    )(cond8, w_ada, b_ada.reshape(DEPTH, 1, 6 * D))


def _rope_chunk(xc, cos, sin):
    lane = lax.broadcasted_iota(I32, xc.shape, 1)
    first = (lane % ROPE_AXIS_DIM) < (ROPE_AXIS_DIM // 2)
    rot = jnp.where(first, -pltpu.roll(xc, LANES - ROPE_AXIS_DIM // 2, 1), pltpu.roll(xc, ROPE_AXIS_DIM // 2, 1))
    return xc * cos + rot * sin


def _in_even_kernel(l, xc_ref, xl_ref, mod_ref, g_ref, w_ref, cos_ref, sin_ref, a_ref, q_ref, k_ref, v_ref):
    i = pl.program_id(0)
    r = _mod_row(i)

    def body(pop):
        x = (xc_ref, xl_ref)[pop][...]
        shift = mod_ref[l, pl.ds(r, 1), 0:D]
        scale = mod_ref[l, pl.ds(r, 1), D:2 * D]
        h = _rms_mod(x, g_ref[l:l + 1, :], scale, shift)
        proj = _dot(h, w_ref[...])
        a_ref[...] = proj[:, :POOL_DIM]
        v_ref[...] = proj[:, POOL_DIM + Q_DIM + KV_DIM:]
        if pop == 0:
            q_ref[...] = proj[:, POOL_DIM:POOL_DIM + Q_DIM]
            k_ref[...] = proj[:, POOL_DIM + Q_DIM:POOL_DIM + Q_DIM + KV_DIM]
        else:
            off, _, _, _ = _seq_info(i)
            off = pl.multiple_of(off, TM)
            cos = cos_ref[pl.ds(off, TM), :]
            sin = sin_ref[pl.ds(off, TM), :]
            for c in range(Q_DIM // LANES):
                lo = POOL_DIM + c * LANES
                q_ref[:, c * LANES:(c + 1) * LANES] = _rope_chunk(proj[:, lo:lo + LANES], cos, sin)
            lo = POOL_DIM + Q_DIM
            k_ref[...] = _rope_chunk(proj[:, lo:lo + LANES], cos, sin)

    _per_population(i, body)


def _in_even(l, e, xc, xl, mod, g, w_in, cos_t, sin_t):
    return pl.pallas_call(
        functools.partial(_in_even_kernel, l),
        grid=(NT,),
        in_specs=_pop_specs(TM, D) + [_MOD_SPEC, _full_spec((DEPTH, D)), _layer_spec((D, IN_AB), e),
                  _full_spec((LAT_LEN, LANES)), _full_spec((LAT_LEN, LANES))],
        out_specs=[_tile_spec(POOL_DIM), _tile_spec(Q_DIM), _tile_spec(KV_DIM), _tile_spec(KV_DIM)],
        out_shape=[jax.ShapeDtypeStruct((T_ALL, POOL_DIM), F32), jax.ShapeDtypeStruct((T_ALL, Q_DIM), F32),
                   jax.ShapeDtypeStruct((T_ALL, KV_DIM), F32), jax.ShapeDtypeStruct((T_ALL, KV_DIM), F32)],
        compiler_params=_params(),
        name="in_even",
    )(xc, xl, mod, g, w_in, cos_t, sin_t)


def _attn_ctx_kernel(e, sink_ref, q_ref, k_ref, v_ref, o_ref):
    q = q_ref[...]
    k = k_ref[...]
    v = v_ref[...]
    for h in range(N_KV_HEADS):
        kh = k[:, h * HEAD_DIM:(h + 1) * HEAD_DIM]
        vh = v[:, h * HEAD_DIM:(h + 1) * HEAD_DIM]
        for g in range(Q_PER_KV):
            j = h * Q_PER_KV + g
            s = _dot_nt(q[:, j * HEAD_DIM:(j + 1) * HEAD_DIM], kh) * ATTN_SCALE
            sk = sink_ref[e, j]
            m = jnp.maximum(jnp.max(s, axis=-1, keepdims=True), sk)
            p = jnp.exp(s - m)
            denom = jnp.sum(p, axis=-1, keepdims=True) + jnp.exp(sk - m)
            o_ref[:, j * HEAD_DIM:(j + 1) * HEAD_DIM] = _dot(p, vh) / denom


def _attn_ctx(e, sink, q, k, v):
    tile = lambda w: pl.BlockSpec((TM, w), lambda b: (b, 0))
    return pl.pallas_call(
        functools.partial(_attn_ctx_kernel, e),
        grid=(N_CTX_SEQ,),
        in_specs=[pl.BlockSpec(memory_space=pltpu.SMEM), tile(Q_DIM), tile(KV_DIM), tile(KV_DIM)],
        out_specs=tile(Q_DIM),
        out_shape=jax.ShapeDtypeStruct((T_CTX, Q_DIM), F32),
        compiler_params=_params(),
        name="attn_ctx",
    )(sink, q, k, v)


LAT_BLOCKS = LAT_LEN // ATTN_BLOCK
BAND = 3 * ATTN_BLOCK


def _attn_lat_kernel(e, sink_ref, q_ref, k_ref, v_ref, ck_ref, cv_ref, o_ref):
    n = pl.program_id(1)
    start = jnp.clip((n - 1) * ATTN_BLOCK, 0, LAT_LEN - BAND)
    start = pl.multiple_of(start, ATTN_BLOCK)
    q = q_ref[...]
    kw = k_ref[pl.ds(start, BAND), :]
    vw = v_ref[pl.ds(start, BAND), :]
    ck = ck_ref[0]
    cv = cv_ref[0]
    qpos = n * ATTN_BLOCK + lax.broadcasted_iota(I32, (ATTN_BLOCK, BAND), 0)
    kpos = start + lax.broadcasted_iota(I32, (ATTN_BLOCK, BAND), 1)
    valid = jnp.abs(qpos - kpos) <= ATTN_WINDOW
    for h in range(N_KV_HEADS):
        hs = slice(h * HEAD_DIM, (h + 1) * HEAD_DIM)
        for g in range(Q_PER_KV):
            j = h * Q_PER_KV + g
            qj = q[:, j * HEAD_DIM:(j + 1) * HEAD_DIM]
            s_loc = jnp.where(valid, _dot_nt(qj, kw[:, hs]) * ATTN_SCALE, NEG_BIG)
            s_ctx = _dot_nt(qj, ck[:, hs]) * ATTN_SCALE
            sk = sink_ref[e, j]
            m = jnp.maximum(jnp.maximum(jnp.max(s_loc, axis=-1, keepdims=True),
                                        jnp.max(s_ctx, axis=-1, keepdims=True)), sk)
            p_loc = jnp.exp(s_loc - m)
            p_ctx = jnp.exp(s_ctx - m)
            denom = (jnp.sum(p_loc, axis=-1, keepdims=True) + jnp.sum(p_ctx, axis=-1, keepdims=True)
                     + jnp.exp(sk - m))
            o = _dot(p_loc, vw[:, hs]) + _dot(p_ctx, cv[:, hs])
            o_ref[:, j * HEAD_DIM:(j + 1) * HEAD_DIM] = o / denom


def _attn_lat(e, sink, q, k, v, ck, cv):
    past = ck.shape[1]
    q_blk0 = T_CTX // ATTN_BLOCK
    kv_blk0 = T_CTX // LAT_LEN
    q_spec = pl.BlockSpec((ATTN_BLOCK, Q_DIM), lambda b, n: (q_blk0 + b * LAT_BLOCKS + n, 0))
    return pl.pallas_call(
        functools.partial(_attn_lat_kernel, e),
        grid=(N_LAT_SEQ, LAT_BLOCKS),
        in_specs=[pl.BlockSpec(memory_space=pltpu.SMEM),
                  q_spec,
                  pl.BlockSpec((LAT_LEN, KV_DIM), lambda b, n: (kv_blk0 + b, 0)),
                  pl.BlockSpec((LAT_LEN, KV_DIM), lambda b, n: (kv_blk0 + b, 0)),
                  pl.BlockSpec((1, past, KV_DIM), lambda b, n: (b, 0, 0)),
                  pl.BlockSpec((1, past, KV_DIM), lambda b, n: (b, 0, 0))],
        out_specs=pl.BlockSpec((ATTN_BLOCK, Q_DIM), lambda b, n: (b * LAT_BLOCKS + n, 0)),
        out_shape=jax.ShapeDtypeStruct((T_LAT, Q_DIM), F32),
        compiler_params=_params(),
        name="attn_lat",
    )(sink, q, k, v, ck, cv)


def _router(h2, wr_ref):
    hi = h2.astype(BF16)
    lo = (h2 - hi.astype(F32)).astype(BF16)
    whi = wr_ref[0]
    wlo = wr_ref[1]
    dot = functools.partial(jnp.dot, preferred_element_type=F32)
    logits = dot(hi, whi) + (dot(lo, whi) + dot(hi, wlo))
    lane = lax.broadcasted_iota(I32, logits.shape, 1).astype(F32)
    neg = jnp.float32(-jnp.inf)
    big = jnp.float32(1e9)
    is_grp = lane < N_GROUPS
    gl = jnp.where(is_grp, logits, neg)
    gmax = jnp.max(gl, axis=-1, keepdims=True)
    gsum = jnp.sum(jnp.where(is_grp, jnp.exp(logits - gmax), 0.0), axis=-1, keepdims=True)
    g_w = 1.0 / gsum
    g_idx = jnp.min(jnp.where(gl == gmax, lane, big), axis=-1, keepdims=True)
    base = N_GROUPS + EPG * g_idx
    in_grp = jnp.logical_and(lane >= base, lane < base + EPG)
    el = jnp.where(in_grp, logits, neg)
    t1 = jnp.max(el, axis=-1, keepdims=True)
    i1 = jnp.min(jnp.where(el == t1, lane, big), axis=-1, keepdims=True)
    el2 = jnp.where(lane == i1, neg, el)
    t2 = jnp.max(el2, axis=-1, keepdims=True)
    i2 = jnp.min(jnp.where(el2 == t2, lane, big), axis=-1, keepdims=True)
    d = jnp.exp(t2 - t1)
    w1 = g_w / (1.0 + d)
    w2 = g_w * d / (1.0 + d)
    return i1 - N_GROUPS, i2 - N_GROUPS, w1, w2


LIST_SPAN_LOG2 = 5
LIST_SPAN = 1 << LIST_SPAN_LOG2
LIST_ROWS = LANES
LIST_COLS = N_EXPERTS * LIST_SPAN


N_TAIL_OUT = 7
N_TAIL_SCRATCH = 3


def _mixer_tail(l, x, y, mod_ref, r, gffn_ref, wr_ref, outs, scratch):
    x1_ref, h2_ref, cnt_ref, lc_ref, gc_ref, ll_ref, gl_ref = outs
    run_ref, acc_ref, gacc_ref = scratch
    i = pl.program_id(0)
    g1 = mod_ref[l, pl.ds(r, 1), 2 * D:3 * D]
    shift2 = mod_ref[l, pl.ds(r, 1), 3 * D:4 * D]
    scale2 = mod_ref[l, pl.ds(r, 1), 4 * D:5 * D]
    x1 = x + g1 * y
    x1_ref[...] = x1
    h2 = _rms_mod(x1, gffn_ref[l:l + 1, :], scale2, shift2)
    for s in range(TOK_ROWS):
        h2_ref[pl.ds(s, TM, stride=TOK_ROWS), :] = h2[:, s * LANES:(s + 1) * LANES]
    e1, e2, w1, w2 = _router(h2, wr_ref)

    @pl.when(jnp.logical_or(i == 0, i == MOE_SPLIT))
    def _():
        run_ref[...] = jnp.zeros(run_ref.shape, F32)
        acc_ref[...] = jnp.zeros(acc_ref.shape, F32)
        gacc_ref[...] = jnp.zeros(gacc_ref.shape, F32)

    @pl.when(i == 0)
    def _():
        cnt_ref[...] = jnp.zeros(cnt_ref.shape, I32)

    lane_i = lax.broadcasted_iota(I32, (TM, LANES), 1)
    lane = lane_i.astype(F32)
    member = jnp.where(jnp.logical_or(lane == e1, lane == e2), 1.0, 0.0)
    tri = jnp.where(lax.broadcasted_iota(I32, (TM, TM), 0) >= lax.broadcasted_iota(I32, (TM, TM), 1), 1.0, 0.0)
    csum = jnp.dot(tri.astype(BF16), member.astype(BF16), preferred_element_type=F32)
    run = run_ref[0:1, :]
    before = csum - member + run
    r1 = jnp.sum(jnp.where(lane == e1, before, 0.0), axis=-1, keepdims=True)
    r2 = jnp.sum(jnp.where(lane == e2, before, 0.0), axis=-1, keepdims=True)
    run_new = run + csum[TM - 1:TM, :]
    run_ref[0:1, :] = run_new

    col_i = lax.broadcasted_iota(I32, (TM, LIST_COLS), 1)
    rows_oh, cols_oh = [], []
    for e, rank in ((e1, r1), (e2, r2)):
        rank_i = rank.astype(I32)
        rows_oh.append(jnp.where(lane_i == rank_i // LIST_SPAN, 1.0, 0.0))
        cols_oh.append(jnp.where(col_i == e.astype(I32) * LIST_SPAN + rank_i % LIST_SPAN, 1.0, 0.0).astype(BF16))
    rows_t = jnp.concatenate(rows_oh, axis=0).T
    cols = jnp.concatenate(cols_oh, axis=0)
    tok = (lax.broadcasted_iota(I32, (1, 2 * TM), 1) % TM).astype(F32)
    rec = jnp.where(lane_i == 0, w1, jnp.where(lane_i == 1, w2, 0.0)).T
    rest = jnp.concatenate([rec[0:1, :], rec[1:2, :]], axis=1)
    values = [None, tok]
    for _ in range(3):
        piece = rest.astype(BF16).astype(F32)
        values.append(piece)
        rest = rest - piece
    planes = jnp.concatenate([rows_t if v is None else rows_t * v for v in values], axis=0).astype(BF16)
    out = jnp.dot(planes, cols, preferred_element_type=F32)
    hits, tok_sum, gp0, gp1, gp2 = [out[k * LANES:(k + 1) * LANES] for k in range(len(values))]
    tile_in_pop = jnp.where(i < MOE_SPLIT, i, i - MOE_SPLIT).astype(F32)
    acc = acc_ref[...] + TOK_ROWS * (tok_sum + TM * tile_in_pop * hits)
    acc_ref[...] = acc
    gacc = gacc_ref[...] + gp0 + gp1 + gp2
    gacc_ref[...] = gacc

    @pl.when(i == MOE_SPLIT - 1)
    def _():
        cnt_ref[0:1, :] = run_new.astype(I32)
        lc_ref[...] = acc.astype(I32)
        gc_ref[...] = gacc

    @pl.when(i == NT - 1)
    def _():
        cnt_ref[1:2, :] = run_new.astype(I32)
        ll_ref[...] = acc.astype(I32)
        gl_ref[...] = gacc


_TAIL_OUT_SHAPES = [jax.ShapeDtypeStruct((T_ALL, D), F32),
                    jax.ShapeDtypeStruct((T_ALL * TOK_ROWS, LANES), F32),
                    jax.ShapeDtypeStruct((SUBLANES, LANES), I32)] + [
                        jax.ShapeDtypeStruct((LIST_ROWS, LIST_COLS), dt) for dt in (I32, F32, I32, F32)]


def _tail_scratch():
    return [pltpu.VMEM((SUBLANES, LANES), F32), pltpu.VMEM((LIST_ROWS, LIST_COLS), F32),
            pltpu.VMEM((LIST_ROWS, LIST_COLS), F32)]


def _tail_out_specs():
    return [_tile_spec(D),
            pl.BlockSpec((TM * TOK_ROWS, LANES), lambda i: (i, 0)),
            _full_spec((SUBLANES, LANES))] + [_full_spec((LIST_ROWS, LIST_COLS)) for _ in range(4)]


def _halo_specs(halo, width):
    per = TM // halo
    last = T_ALL // halo - 1
    prev = pl.BlockSpec((halo, width), lambda i: (jnp.maximum(i * per - 1, 0), 0))
    nxt = pl.BlockSpec((halo, width), lambda i: (jnp.minimum((i + 1) * per, last), 0))
    return prev, nxt


def _mid_even_kernel(l, a_ref, ap_ref, an_ref, ybc_ref, ybl_ref, xc_ref, xl_ref, mod_ref, wp_ref, ps_ref, wo_ref,
                     gffn_ref, wr_ref, *refs):
    outs, (pad_ref,), scratch = refs[:N_TAIL_OUT], refs[N_TAIL_OUT:-N_TAIL_SCRATCH], refs[-N_TAIL_SCRATCH:]
    i = pl.program_id(0)
    r = _mod_row(i)
    off, n, first, last = _seq_info(i)
    a = a_ref[...]
    pad_ref[0:POOL_HALO, :] = jnp.where(first, 0.0, ap_ref[...])
    pad_ref[POOL_HALO:POOL_HALO + TM, :] = a
    pad_ref[POOL_HALO + TM:, :] = jnp.where(last, 0.0, an_ref[...])
    t = off + lax.broadcasted_iota(I32, (TM, 1), 0)
    mixed = []
    for g, w in enumerate(POOL_WINDOWS):
        lo = w // 2
        hi = w - lo - 1
        cols = slice(g * POOL_GROUP_DIM, (g + 1) * POOL_GROUP_DIM)
        total = pad_ref[POOL_HALO - lo:POOL_HALO - lo + TM, cols]
        for j in range(-lo + 1, hi + 1):
            total = total + pad_ref[POOL_HALO + j:POOL_HALO + j + TM, cols]
        count = (jnp.minimum(t + hi, n - 1) - jnp.maximum(t - lo, 0) + 1).astype(F32)
        pooled = total / count - a[:, cols]
        mixed.append(_dot(pooled, wp_ref[g]))
    ya = jnp.concatenate(mixed, axis=1) * ps_ref[...]
    y_pool = _dot(ya, wo_ref[0:POOL_DIM, :])

    def body(pop):
        y = y_pool + _dot((ybc_ref, ybl_ref)[pop][...], wo_ref[POOL_DIM:, :])
        _mixer_tail(l, (xc_ref, xl_ref)[pop][...], y, mod_ref, r, gffn_ref, wr_ref, outs, scratch)

    _per_population(i, body)


def _mid_even(l, e, a, yb_c, yb_l, xc, xl, mod, w_pool, pool_scale, w_out, gffn, wr):
    prev, nxt = _halo_specs(POOL_HALO, POOL_DIM)
    return pl.pallas_call(
        functools.partial(_mid_even_kernel, l),
        grid=(NT,),
        in_specs=[_tile_spec(POOL_DIM), prev, nxt] + _pop_specs(TM, Q_DIM) + _pop_specs(TM, D) + [
                  _MOD_SPEC, _layer_spec((len(POOL_WINDOWS), POOL_GROUP_DIM, POOL_GROUP_DIM), e),
                  _layer_spec((1, POOL_DIM), e), _layer_spec((D, D), e), _full_spec((DEPTH, D)),
                  _layer_spec((2, D, LANES), l)],
        out_specs=_tail_out_specs(),
        out_shape=_TAIL_OUT_SHAPES,
        scratch_shapes=[pltpu.VMEM((TM + 2 * POOL_HALO, POOL_DIM), F32)] + _tail_scratch(),
        compiler_params=_params(),
        name="mid_even",
    )(a, a, a, yb_c, yb_l, xc, xl, mod, w_pool, pool_scale, w_out, gffn, wr)


def _after_moe(l, r, x1_ref, moe_ref, mod_ref):
    return x1_ref[...] + mod_ref[l, pl.ds(r, 1), 5 * D:6 * D] * _from_token_tiles(moe_ref)


def _in_odd_kernel(l, x1_ref, moec_ref, moel_ref, mod_ref, g_ref, w1_ref, b1_ref, u_ref):
    i = pl.program_id(0)
    r = _mod_row(i)

    def body(pop):
        x2 = _after_moe(l - 1, r, x1_ref, (moec_ref, moel_ref)[pop], mod_ref)
        shift = mod_ref[l, pl.ds(r, 1), 0:D]
        scale = mod_ref[l, pl.ds(r, 1), D:2 * D]
        h = _rms_mod(x2, g_ref[l:l + 1, :], scale, shift)
        u = _dot(h, w1_ref[...]) + b1_ref[...]
        u_ref[...] = u[:, :D] * (1.0 / (1.0 + jnp.exp(-u[:, D:])))

    _per_half(i, body)


def _in_odd(l, o, x1, moe_c, moe_l, mod, g, w1, b1):
    return pl.pallas_call(
        functools.partial(_in_odd_kernel, l),
        grid=(NT,),
        in_specs=[_tile_spec(D)] + _half_specs(TM * TOK_ROWS, LANES) + [
                  _MOD_SPEC, _full_spec((DEPTH, D)), _layer_spec((D, 2 * D), o), _layer_spec((1, 2 * D), o)],
        out_specs=_tile_spec(D),
        out_shape=jax.ShapeDtypeStruct((T_ALL, D), F32),
        compiler_params=_params(),
        name="in_odd",
    )(x1, moe_c, moe_l, mod, g, w1, b1)


CONV_PAD_ROWS = TM + 2 * CONV_HALO
CONV_CHUNKS = D // LANES
CONV_BLOCK = 64


def _mid_odd_kernel(l, u_ref, up_ref, un_ref, x1_ref, moec_ref, moel_ref, mod_ref, dw_ref, dwb_ref, lng_ref, lnb_ref,
                    w2_ref, b2_ref, gffn_ref, wr_ref, *refs):
    outs, scratch = refs[:N_TAIL_OUT], refs[-N_TAIL_SCRATCH:]
    pad_ref, conv_ref, x_ref = refs[N_TAIL_OUT:-N_TAIL_SCRATCH]
    i = pl.program_id(0)

    def residual(pop):
        x_ref[...] = _after_moe(l - 1, _mod_row(i), x1_ref, (moec_ref, moel_ref)[pop], mod_ref)

    _per_half(i, residual)
    r = _mod_row(i)
    _, _, first, last = _seq_info(i)
    u = u_ref[...]
    up = jnp.where(first, 0.0, up_ref[...])
    un = jnp.where(last, 0.0, un_ref[...])
    half = CONV_WIDTH // 2
    first_row = CONV_HALO - half
    for c in range(CONV_CHUNKS):
        cols = slice(c * LANES, (c + 1) * LANES)
        pad_ref[c, 0:CONV_HALO, :] = up[:, cols]
        pad_ref[c, CONV_HALO:CONV_HALO + TM, :] = u[:, cols]
        pad_ref[c, CONV_HALO + TM:, :] = un[:, cols]
        bias = jnp.broadcast_to(dwb_ref[:, cols], (CONV_BLOCK, LANES))
        for r0 in range(0, TM, 2 * CONV_BLOCK):
            acc_even, acc_odd = bias, bias
            for s in range(first_row, first_row + CONV_WIDTH + 1):
                win = pad_ref[c, pl.ds(r0 + s, CONV_BLOCK, stride=2), :]
                j = s - first_row
                if j < CONV_WIDTH:
                    acc_even = acc_even + win * dw_ref[j:j + 1, cols]
                if j >= 1:
                    acc_odd = acc_odd + win * dw_ref[j - 1:j, cols]
            conv_ref[c, pl.ds(r0, CONV_BLOCK, stride=2), :] = acc_even
            conv_ref[c, pl.ds(r0 + 1, CONV_BLOCK, stride=2), :] = acc_odd
    acc = jnp.concatenate([conv_ref[c] for c in range(CONV_CHUNKS)], axis=1)
    mu = jnp.mean(acc, axis=-1, keepdims=True)
    cen = acc - mu
    var = jnp.mean(cen * cen, axis=-1, keepdims=True)
    v = _silu(cen * lax.rsqrt(var + EPS) * lng_ref[...] + lnb_ref[...])
    y = _dot(v, w2_ref[...]) + b2_ref[...]
    _mixer_tail(l, x_ref[...], y, mod_ref, r, gffn_ref, wr_ref, outs, scratch)


def _mid_odd(l, o, u, x1, moe_c, moe_l, mod, dw, dwb, lng, lnb, w2, b2, gffn, wr):
    prev, nxt = _halo_specs(CONV_HALO, D)
    return pl.pallas_call(
        functools.partial(_mid_odd_kernel, l),
        grid=(NT,),
        in_specs=[_tile_spec(D), prev, nxt, _tile_spec(D)] + _half_specs(TM * TOK_ROWS, LANES) + [
                  _MOD_SPEC, _layer_spec((CONV_WIDTH, D), o),
                  _layer_spec((1, D), o), _layer_spec((1, D), o), _layer_spec((1, D), o),
                  _layer_spec((D, D), o), _layer_spec((1, D), o), _full_spec((DEPTH, D)),
                  _layer_spec((2, D, LANES), l)],
        out_specs=_tail_out_specs(),
        out_shape=_TAIL_OUT_SHAPES,
        scratch_shapes=[pltpu.VMEM((CONV_CHUNKS, CONV_PAD_ROWS, LANES), F32),
                        pltpu.VMEM((CONV_CHUNKS, TM, LANES), F32), pltpu.VMEM((TM, D), F32)] + _tail_scratch(),
        compiler_params=_params(),
        name="mid_odd",
    )(u, u, u, x1, moe_c, moe_l, mod, dw, dwb, lng, lnb, w2, b2, gffn, wr)


def _tile_tables_kernel(cnt_ref, te_c, tc_c, tw_c, te_l, tc_l, tw_l, nact_ref, second_ref, first_ref, wplan_ref):
    for p, (te_ref, tc_ref, tw_ref) in enumerate(((te_c, tc_c, tw_c), (te_l, tc_l, tw_l))):
        run = jnp.int32(0)
        ends = []
        for e in range(N_EXPERTS):
            first_ref[p * N_EXPERTS + e] = run
            run = run + (cnt_ref[p, e] + MOE_TM - 1) // MOE_TM
            ends.append(run)
        nact_ref[p] = run

        nxt1, nxt2 = jnp.int32(0), jnp.int32(0)
        plans = []
        for e in reversed(range(N_EXPERTS)):
            plans.append(nxt2)
            used = cnt_ref[p, e] > 0
            nxt2 = jnp.where(used, nxt1, nxt2)
            nxt1 = jnp.where(used, e + 1, nxt1)
        plans.reverse()
        second_ref[p] = nxt2
        order = jnp.int32(0)
        for e in range(N_EXPERTS):
            wplan_ref[p * N_EXPERTS + e] = order + (plans[e] << W_AHEAD_SHIFT)
            order = jnp.where(cnt_ref[p, e] > 0, jnp.where(order == W_BUFFERS - 1, 0, order + 1), order)

        def expert_of(tile, ends=ends):
            e = jnp.int32(0)
            for k in range(N_EXPERTS):
                e = e + jnp.where(tile >= ends[k], 1, 0)
            return e

        last_e = expert_of(run - 1)

        def tile_body(j, c, te_ref=te_ref, tc_ref=tc_ref, tw_ref=tw_ref, p=p, run=run, expert_of=expert_of,
                      last_e=last_e):
            e = jnp.minimum(expert_of(j), last_e)
            te_ref[j] = e
            chunk = jnp.where(j < run, j - first_ref[p * N_EXPERTS + e], 0)
            tc_ref[j] = chunk
            starts_expert = jnp.logical_and(j < run, chunk == 0)
            tw_ref[j] = wplan_ref[p * N_EXPERTS + e] + jnp.where(starts_expert, W_FIRST_UNIT, 0)
            return c

        lax.fori_loop(0, te_ref.shape[0], tile_body, 0)


def _tile_tables(cnt):
    smem = pl.BlockSpec(memory_space=pltpu.SMEM)
    shapes = [(MOE_TILES,)] * 6 + [(2,), (2,)]
    return pl.pallas_call(
        _tile_tables_kernel,
        in_specs=[smem],
        out_specs=[smem] * len(shapes),
        out_shape=[jax.ShapeDtypeStruct(s, I32) for s in shapes],
        scratch_shapes=[pltpu.SMEM((2 * N_EXPERTS,), I32), pltpu.SMEM((2 * N_EXPERTS,), I32)],
        name="tile_tables",
    )(cnt)


def _moe_kernel(l, pop, te_ref, tc_ref, tw_ref, nact_ref, second_ref, list_ref, h_ref, gate_ref,
                w1_hbm, w3_hbm, w2_hbm, o_ref, xs_ref, ys_ref, w1_buf, w3_buf, w2_buf, w_sem):
    j = pl.program_id(0)
    nact = nact_ref[pop]
    last_tile = te_ref.shape[0] - 1
    tile_rows = MOE_TM * TOK_ROWS

    def weight_copies(expert, buf):
        return [pltpu.make_async_copy(hbm.at[l, expert], vmem.at[buf], w_sem.at[buf, k])
                for k, (hbm, vmem) in enumerate(((w1_hbm, w1_buf), (w3_hbm, w3_buf), (w2_hbm, w2_buf)))]

    def list_base(tile):
        return tc_ref[tile] * (MOE_TM // LIST_SPAN) * LIST_COLS + te_ref[tile] * LIST_SPAN

    def tok_rows(base, r):
        first = list_ref[base + (r // LIST_SPAN) * LIST_COLS + r % LIST_SPAN]
        return pl.ds(pl.multiple_of(first, TOK_ROWS), TOK_ROWS)

    def buf_rows(buf, r):
        return pl.ds(pl.multiple_of(buf * tile_rows + r * TOK_ROWS, TOK_ROWS), TOK_ROWS)

    def gather(tile, buf):
        base = list_base(tile)
        for r in range(MOE_TM):
            xs_ref[buf_rows(buf, r), :] = h_ref[tok_rows(base, r), :]

    tile = jnp.minimum(j, last_tile)
    plan = tw_ref[tile]
    wbuf = plan & (W_FIRST_UNIT - 1)
    starts_expert = jnp.logical_and(plan & W_FIRST_UNIT != 0, j <= last_tile)
    ahead_expert = (plan >> W_AHEAD_SHIFT) - 1
    ahead_buf = jnp.where(wbuf == 0, W_BUFFERS - 1, wbuf - 1)

    @pl.when(j == 0)
    def _():
        for cp in weight_copies(te_ref[0], 0):
            cp.start()

        @pl.when(second_ref[pop] > 0)
        def _():
            for cp in weight_copies(second_ref[pop] - 1, 1):
                cp.start()

        o_ref[...] = jnp.zeros(o_ref.shape, F32)
        ys_ref[...] = jnp.zeros(ys_ref.shape, F32)
        gather(0, 0)

    @pl.when(starts_expert)
    def _():
        for cp in weight_copies(te_ref[tile], wbuf):
            cp.wait()

        @pl.when(ahead_expert >= 0)
        def _():
            for cp in weight_copies(ahead_expert, ahead_buf):
                cp.start()

    @pl.when(j <= nact)
    def _():
        cur = j & 1
        w1_ref, w3_ref, w2_ref = w1_buf.at[wbuf], w3_buf.at[wbuf], w2_buf.at[wbuf]
        x = jnp.concatenate([xs_ref[pl.ds(cur * tile_rows + s, MOE_TM, stride=TOK_ROWS), :]
                             for s in range(TOK_ROWS)], axis=1).astype(BF16)
        gather(jnp.minimum(j + 1, last_tile), 1 - cur)

        prev_base = list_base(jnp.maximum(j - 1, 0))
        for r0 in range(0, MOE_TM, MOVE_BATCH):
            dst = [tok_rows(prev_base, r0 + k) for k in range(MOVE_BATCH)]
            vals = [o_ref[dst[k], :] + ys_ref[buf_rows(1 - cur, r0 + k), :] for k in range(MOVE_BATCH)]
            for k in reversed(range(MOVE_BATCH)):
                o_ref[dst[k], :] = vals[k]

        hg = jnp.dot(x, w1_ref[...].astype(BF16), preferred_element_type=F32)
        hu = jnp.dot(x, w3_ref[...].astype(BF16), preferred_element_type=F32)
        y = _dot(_silu(hg) * hu, w2_ref[...])

        first_row = pl.multiple_of(tc_ref[tile] * (MOE_TM // LIST_SPAN), MOE_TM // LIST_SPAN)
        spread = jnp.concatenate(
            [jnp.broadcast_to(gate_ref[pl.ds(first_row + k, 1), :], (LIST_SPAN, LIST_COLS))
             for k in range(MOE_TM // LIST_SPAN)], axis=0)
        want = te_ref[tile] * LIST_SPAN + (lax.broadcasted_iota(I32, (MOE_TM, 1), 0) & (LIST_SPAN - 1))
        hit = jnp.logical_and(lax.broadcasted_iota(I32, (MOE_TM, LIST_COLS), 1) == want, j < nact)
        y = y * jnp.sum(jnp.where(hit, spread, 0.0), axis=1, keepdims=True)
        for s in range(TOK_ROWS):
            ys_ref[pl.ds(cur * tile_rows + s, MOE_TM, stride=TOK_ROWS), :] = y[:, s * LANES:(s + 1) * LANES]


def _moe(l, pop, h_tt, te, tc, tw, nact, second, rows, gate, w1, w3, w2):
    tp = T_HALF
    blk = pop
    n_tiles = te.shape[0]
    hbm = pl.BlockSpec(memory_space=pl.ANY)
    grid_spec = pltpu.PrefetchScalarGridSpec(
        num_scalar_prefetch=6,
        grid=(n_tiles + 1,),
        in_specs=[
            pl.BlockSpec((tp * TOK_ROWS, LANES), lambda j, *_: (blk, 0), pipeline_mode=pl.Buffered(1)),
            pl.BlockSpec((LIST_ROWS, LIST_COLS), lambda j, *_: (0, 0), pipeline_mode=pl.Buffered(1)),
            hbm, hbm, hbm,
        ],
        out_specs=pl.BlockSpec((tp * TOK_ROWS, LANES), lambda j, *_: (0, 0), pipeline_mode=pl.Buffered(1)),
        scratch_shapes=[pltpu.VMEM((2 * MOE_TM * TOK_ROWS, LANES), F32),
                        pltpu.VMEM((2 * MOE_TM * TOK_ROWS, LANES), F32),
                        pltpu.VMEM((W_BUFFERS, D, EXPERT_HIDDEN), F32),
                        pltpu.VMEM((W_BUFFERS, D, EXPERT_HIDDEN), F32),
                        pltpu.VMEM((W_BUFFERS, EXPERT_HIDDEN, D), F32),
                        pltpu.SemaphoreType.DMA((W_BUFFERS, 3))],
    )
    return pl.pallas_call(
        functools.partial(_moe_kernel, l, pop),
        grid_spec=grid_spec,
        out_shape=jax.ShapeDtypeStruct((tp * TOK_ROWS, LANES), F32),
        compiler_params=_params(),
        name="moe",
    )(te, tc, tw, nact, second, rows, h_tt, gate, w1, w3, w2)


def _moe_all(l, h_tt, cnt, rows_c, gate_c, rows_l, gate_l, w1, w3, w2):
    te_c, tc_c, tw_c, te_l, tc_l, tw_l, nact, second = _tile_tables(cnt)
    out_c = _moe(l, 0, h_tt, te_c, tc_c, tw_c, nact, second, rows_c.reshape(-1), gate_c, w1, w3, w2)
    out_l = _moe(l, 1, h_tt, te_l, tc_l, tw_l, nact, second, rows_l.reshape(-1), gate_l, w1, w3, w2)
    return out_c, out_l


def _final_kernel(l, x1_ref, moec_ref, moel_ref, mod_ref, g_ref, oc_ref, ol_ref):
    i = pl.program_id(0)
    r = _mod_row(i)

    def body(half, pop):
        x = _after_moe(l, r, x1_ref, (moec_ref, moel_ref)[half], mod_ref)
        ms = jnp.mean(x * x, axis=-1, keepdims=True)
        (oc_ref, ol_ref)[pop][...] = x * lax.rsqrt(ms + EPS) * g_ref[...]

    pl.when(i < MOE_SPLIT)(functools.partial(body, 0, 0))
    pl.when(jnp.logical_and(i >= MOE_SPLIT, i < NT_CTX))(functools.partial(body, 1, 0))
    pl.when(i >= NT_CTX)(functools.partial(body, 1, 1))


def _final(l, x1, moe_c, moe_l, mod, g):
    return pl.pallas_call(
        functools.partial(_final_kernel, l),
        grid=(NT,),
        in_specs=[_tile_spec(D)] + _half_specs(TM * TOK_ROWS, LANES) + [_MOD_SPEC, _full_spec((1, D))],
        out_specs=_pop_specs(TM, D),
        out_shape=[jax.ShapeDtypeStruct((T_CTX, D), F32), jax.ShapeDtypeStruct((T_LAT, D), F32)],
        compiler_params=_params(),
        name="final_norm",
    )(x1, moe_c, moe_l, mod, g)


def _rope_tables():
    t = jnp.arange(LAT_LEN)
    row = (t // GRID_W).astype(F32)
    col = (t % GRID_W).astype(F32)
    inv = ROPE_BASE ** (-jnp.arange(0, ROPE_AXIS_DIM, 2, dtype=F32) / ROPE_AXIS_DIM)

    def table(p):
        ang = p[:, None] * inv[None, :]
        ang = jnp.concatenate([ang, ang], axis=-1)
        return jnp.cos(ang), jnp.sin(ang)

    (cr, sr), (cc, sc) = table(row), table(col)
    cos = jnp.concatenate([cr, cc, cr, cc], axis=-1)
    sin = jnp.concatenate([sr, sc, sr, sc], axis=-1)
    return cos, sin


def _router_weights(w_grp, w_exp):
    we = jnp.transpose(w_exp, (0, 2, 1, 3)).reshape(DEPTH, D, N_GROUPS * EPG)
    pad = jnp.zeros((DEPTH, D, LANES - N_GROUPS - N_GROUPS * EPG), F32)
    wr = jnp.concatenate([w_grp, we, pad], axis=-1)
    hi = lax.bitcast_convert_type(lax.bitcast_convert_type(wr, jnp.uint32) & jnp.uint32(0xFFFF0000), F32)
    return jnp.stack([hi.astype(BF16), (wr - hi).astype(BF16)], axis=1)


def kernel(x_prompt, x_sample, cache_k, cache_v, c, c_ctx, w_ada, b_ada, norm_mix_g, norm_ffn_g, w_in_ab, pool_w, pool_scale, attn_sink, w_out_ab, conv_w1, conv_b1, conv_dw, conv_dw_b, conv_ln_g, conv_ln_b, conv_w2, conv_b2, router_grp, router_exp, moe_w1, moe_w3, moe_w2, final_g):
    xc = x_prompt.reshape(T_CTX, D)
    xl = x_sample.reshape(T_LAT, D)
    cond8 = jnp.concatenate([c_ctx[None, :], c, jnp.zeros((SUBLANES - 1 - N_LAT_SEQ, D), F32)], axis=0)
    mod = _modulation(cond8, w_ada, b_ada)
    cos_t, sin_t = _rope_tables()
    wr = _router_weights(router_grp, router_exp)
    n_even, n_odd = w_in_ab.shape[0], conv_w1.shape[0]
    past = cache_k.shape[2]

    a, q, k, v = _in_even(0, 0, xc, xl, mod, norm_mix_g, w_in_ab, cos_t, sin_t)
    yb_c = _attn_ctx(0, attn_sink, q, k, v)
    yb_l = _attn_lat(0, attn_sink, q, k, v, cache_k[:, 0].reshape(N_LAT_SEQ, past, KV_DIM),
                     cache_v[:, 0].reshape(N_LAT_SEQ, past, KV_DIM))
    x1, h2, *routing = _mid_even(0, 0, a, yb_c, yb_l, xc, xl, mod, pool_w,
                                 pool_scale.reshape(n_even, 1, POOL_DIM), w_out_ab, norm_ffn_g, wr)
    moe0 = _moe_all(0, h2, *routing, moe_w1, moe_w3, moe_w2)

    vec = lambda p: p.reshape(n_odd, 1, -1)
    u = _in_odd(1, 0, x1, *moe0, mod, norm_mix_g, conv_w1, vec(conv_b1))
    x3, h2b, *routing = _mid_odd(1, 0, u, x1, *moe0, mod, conv_dw, vec(conv_dw_b), vec(conv_ln_g),
                                 vec(conv_ln_b), conv_w2, vec(conv_b2), norm_ffn_g, wr)
    moe1 = _moe_all(1, h2b, *routing, moe_w1, moe_w3, moe_w2)
    y_c, y_l = _final(1, x3, *moe1, mod, final_g.reshape(1, D))

    y_prompt = y_c.reshape(N_CTX_SEQ, CTX_LEN, D)
    y_sample = y_l.reshape(N_LAT_SEQ, LAT_LEN, D)
    state_k = k[:T_CTX].reshape(N_CTX_SEQ, 1, CTX_LEN, N_KV_HEADS, HEAD_DIM)
    state_v = v[:T_CTX].reshape(N_CTX_SEQ, 1, CTX_LEN, N_KV_HEADS, HEAD_DIM)
    return (y_prompt, y_sample, state_k, state_v)
```

```python
import functools

import jax
import jax.numpy as jnp
from jax import lax
from jax.experimental import pallas as pl
from jax.experimental.pallas import tpu as pltpu

F32 = jnp.float32
BF16 = jnp.bfloat16
I32 = jnp.int32

D = 1024
N_CTX_SEQ = 16
CTX_LEN = 256
N_LAT_SEQ = 2
LAT_LEN = 1024
T_CTX = N_CTX_SEQ * CTX_LEN
T_LAT = N_LAT_SEQ * LAT_LEN
T_ALL = T_CTX + T_LAT
TM = 256
NT = T_ALL // TM
NT_CTX = T_CTX // TM
LAT_TILES_PER_SEQ = LAT_LEN // TM
GRID_W = 64
DEPTH = 2

POOL_WINDOWS = (2, 4, 8, 16)
POOL_GROUP_DIM = 128
POOL_DIM = 512
HEAD_DIM = 64
N_Q_HEADS = 8
N_KV_HEADS = 2
Q_PER_KV = 4
Q_DIM = 512
KV_DIM = 128
IN_AB = 1280
ATTN_WINDOW = 128
ATTN_BLOCK = 128
ATTN_SCALE = HEAD_DIM ** -0.5
ROPE_BASE = 10000.0
ROPE_AXIS_DIM = 32
CONV_WIDTH = 31
CONV_HALO = 16
POOL_HALO = 8
N_GROUPS = 4
EPG = 4
N_EXPERTS = 16
EXPERT_HIDDEN = 512
EPS = 1e-6
NEG_BIG = -1e30

SUBLANES = 8
LANES = 128
TOK_ROWS = D // LANES

MOE_TM = 256
MOE_SPLIT = NT // 2
T_HALF = MOE_SPLIT * TM
MOE_TILES = 2 * T_HALF // MOE_TM + N_EXPERTS
W_BUFFERS = 3
W_FIRST_UNIT = 4
W_AHEAD_SHIFT = 3
MOVE_BATCH_LOG2 = 2
MOVE_BATCH = 1 << MOVE_BATCH_LOG2
VMEM_LIMIT = 56 * 1024 * 1024


def _silu(x):
    return x * (1.0 / (1.0 + jnp.exp(-x)))


def _mod_row(i):
    return jnp.where(i < NT_CTX, 0, 1 + (i - NT_CTX) // LAT_TILES_PER_SEQ)


def _seq_info(i):
    is_ctx = i < NT_CTX
    k = (i - NT_CTX) % LAT_TILES_PER_SEQ
    off = jnp.where(is_ctx, 0, k * TM)
    n = jnp.where(is_ctx, CTX_LEN, LAT_LEN)
    first = jnp.logical_or(is_ctx, k == 0)
    last = jnp.logical_or(is_ctx, k == LAT_TILES_PER_SEQ - 1)
    return off, n, first, last


def _rms_mod(x, g, scale, shift):
    ms = jnp.mean(x * x, axis=-1, keepdims=True)
    return (x * lax.rsqrt(ms + EPS) * g) * (1.0 + scale) + shift


def _dot(a, b):
    return jnp.dot(a.astype(BF16), b.astype(BF16), preferred_element_type=F32)


def _dot_nt(a, b):
    return lax.dot_general(a.astype(BF16), b.astype(BF16), (((1,), (1,)), ((), ())),
                           preferred_element_type=F32)


def _from_token_tiles(ref):
    return jnp.concatenate([ref[pl.ds(s, TM, stride=TOK_ROWS), :] for s in range(TOK_ROWS)], axis=1)


def _tile_spec(width):
    return pl.BlockSpec((TM, width), lambda i: (i, 0))


def _full_spec(shape):
    return pl.BlockSpec(shape, lambda i: (0,) * len(shape))


def _layer_spec(shape, l):
    return pl.BlockSpec((None,) + tuple(shape), lambda i: (l,) + (0,) * len(shape))


def _pop_specs(rows, width):
    return [pl.BlockSpec((rows, width), lambda i: (jnp.minimum(i, NT_CTX - 1), 0)),
            pl.BlockSpec((rows, width), lambda i: (jnp.maximum(i - NT_CTX, 0), 0))]


def _per_population(i, body):
    pl.when(i < NT_CTX)(functools.partial(body, 0))
    pl.when(i >= NT_CTX)(functools.partial(body, 1))


def _half_specs(rows, width):
    return [pl.BlockSpec((rows, width), lambda i: (jnp.minimum(i, MOE_SPLIT - 1), 0)),
            pl.BlockSpec((rows, width), lambda i: (jnp.maximum(i - MOE_SPLIT, 0), 0))]


def _per_half(i, body):
    pl.when(i < MOE_SPLIT)(functools.partial(body, 0))
    pl.when(i >= MOE_SPLIT)(functools.partial(body, 1))


_MOD_SPEC = pl.BlockSpec((DEPTH, SUBLANES, 6 * D), lambda i: (0, 0, 0))


def _params():
    return pltpu.CompilerParams(vmem_limit_bytes=VMEM_LIMIT)


MOD_TN = 1536


def _mod_kernel(cond_ref, w_ref, b_ref, o_ref):
    s = _silu(cond_ref[...])
    o_ref[0] = _dot(s, w_ref[0]) + b_ref[0]


def _modulation(cond8, w_ada, b_ada):
    return pl.pallas_call(
        _mod_kernel,
        grid=(DEPTH, 6 * D // MOD_TN),
        in_specs=[
            pl.BlockSpec((SUBLANES, D), lambda l, n: (0, 0)),
            pl.BlockSpec((1, D, MOD_TN), lambda l, n: (l, 0, n)),
            pl.BlockSpec((1, 1, MOD_TN), lambda l, n: (l, 0, n)),
        ],
        out_specs=pl.BlockSpec((1, SUBLANES, MOD_TN), lambda l, n: (l, 0, n)),
        out_shape=jax.ShapeDtypeStruct((DEPTH, SUBLANES, 6 * D), F32),
        compiler_params=_params(),
        name="modulation",
    )(cond8, w_ada, b_ada.reshape(DEPTH, 1, 6 * D))


def _rope_chunk(xc, cos, sin):
    lane = lax.broadcasted_iota(I32, xc.shape, 1)
    first = (lane % ROPE_AXIS_DIM) < (ROPE_AXIS_DIM // 2)
    rot = jnp.where(first, -pltpu.roll(xc, LANES - ROPE_AXIS_DIM // 2, 1), pltpu.roll(xc, ROPE_AXIS_DIM // 2, 1))
    return xc * cos + rot * sin


def _in_even_kernel(l, xc_ref, xl_ref, mod_ref, g_ref, w_ref, cos_ref, sin_ref, a_ref, q_ref, k_ref, v_ref):
    i = pl.program_id(0)
    r = _mod_row(i)

    def body(pop):
        x = (xc_ref, xl_ref)[pop][...]
        shift = mod_ref[l, pl.ds(r, 1), 0:D]
        scale = mod_ref[l, pl.ds(r, 1), D:2 * D]
        h = _rms_mod(x, g_ref[l:l + 1, :], scale, shift)
        proj = _dot(h, w_ref[...])
        a_ref[...] = proj[:, :POOL_DIM]
        v_ref[...] = proj[:, POOL_DIM + Q_DIM + KV_DIM:]
        if pop == 0:
            q_ref[...] = proj[:, POOL_DIM:POOL_DIM + Q_DIM].astype(BF16)
            k_ref[...] = proj[:, POOL_DIM + Q_DIM:POOL_DIM + Q_DIM + KV_DIM]
        else:
            off, _, _, _ = _seq_info(i)
            off = pl.multiple_of(off, TM)
            cos = cos_ref[pl.ds(off, TM), :]
            sin = sin_ref[pl.ds(off, TM), :]
            for c in range(Q_DIM // LANES):
                lo = POOL_DIM + c * LANES
                q_ref[:, c * LANES:(c + 1) * LANES] = _rope_chunk(proj[:, lo:lo + LANES], cos, sin).astype(BF16)
            lo = POOL_DIM + Q_DIM
            k_ref[...] = _rope_chunk(proj[:, lo:lo + LANES], cos, sin)

    _per_population(i, body)


def _in_even(l, e, xc, xl, mod, g, w_in, cos_t, sin_t):
    return pl.pallas_call(
        functools.partial(_in_even_kernel, l),
        grid=(NT,),
        in_specs=_pop_specs(TM, D) + [_MOD_SPEC, _full_spec((DEPTH, D)), _layer_spec((D, IN_AB), e),
                  _full_spec((LAT_LEN, LANES)), _full_spec((LAT_LEN, LANES))],
        out_specs=[_tile_spec(POOL_DIM), _tile_spec(Q_DIM), _tile_spec(KV_DIM), _tile_spec(KV_DIM)],
        out_shape=[jax.ShapeDtypeStruct((T_ALL, POOL_DIM), F32), jax.ShapeDtypeStruct((T_ALL, Q_DIM), BF16),
                   jax.ShapeDtypeStruct((T_ALL, KV_DIM), F32), jax.ShapeDtypeStruct((T_ALL, KV_DIM), F32)],
        compiler_params=_params(),
        name="in_even",
    )(xc, xl, mod, g, w_in, cos_t, sin_t)


def _attn_ctx_kernel(e, sink_ref, q_ref, k_ref, v_ref, o_ref):
    q = q_ref[...]
    k = k_ref[...]
    v = v_ref[...]
    for h in range(N_KV_HEADS):
        kh = k[:, h * HEAD_DIM:(h + 1) * HEAD_DIM]
        vh = v[:, h * HEAD_DIM:(h + 1) * HEAD_DIM]
        for g in range(Q_PER_KV):
            j = h * Q_PER_KV + g
            s = _dot_nt(q[:, j * HEAD_DIM:(j + 1) * HEAD_DIM], kh) * ATTN_SCALE
            sk = sink_ref[e, j]
            m = jnp.maximum(jnp.max(s, axis=-1, keepdims=True), sk)
            p = jnp.exp(s - m)
            denom = jnp.sum(p, axis=-1, keepdims=True) + jnp.exp(sk - m)
            o_ref[:, j * HEAD_DIM:(j + 1) * HEAD_DIM] = (_dot(p, vh) / denom).astype(o_ref.dtype)


def _attn_ctx(e, sink, q, k, v):
    tile = lambda w: pl.BlockSpec((TM, w), lambda b: (b, 0))
    return pl.pallas_call(
        functools.partial(_attn_ctx_kernel, e),
        grid=(N_CTX_SEQ,),
        in_specs=[pl.BlockSpec(memory_space=pltpu.SMEM), tile(Q_DIM), tile(KV_DIM), tile(KV_DIM)],
        out_specs=tile(Q_DIM),
        out_shape=jax.ShapeDtypeStruct((T_CTX, Q_DIM), BF16),
        compiler_params=_params(),
        name="attn_ctx",
    )(sink, q, k, v)


LAT_BLOCKS = LAT_LEN // ATTN_BLOCK
BAND = 3 * ATTN_BLOCK


def _attn_lat_kernel(e, sink_ref, q_ref, k_ref, v_ref, ck_ref, cv_ref, o_ref):
    n = pl.program_id(1)
    start = jnp.clip((n - 1) * ATTN_BLOCK, 0, LAT_LEN - BAND)
    start = pl.multiple_of(start, ATTN_BLOCK)
    q = q_ref[...]
    kw = k_ref[pl.ds(start, BAND), :]
    vw = v_ref[pl.ds(start, BAND), :]
    ck = ck_ref[0]
    cv = cv_ref[0]
    qpos = n * ATTN_BLOCK + lax.broadcasted_iota(I32, (ATTN_BLOCK, BAND), 0)
    kpos = start + lax.broadcasted_iota(I32, (ATTN_BLOCK, BAND), 1)
    valid = jnp.abs(qpos - kpos) <= ATTN_WINDOW
    for h in range(N_KV_HEADS):
        hs = slice(h * HEAD_DIM, (h + 1) * HEAD_DIM)
        for g in range(Q_PER_KV):
            j = h * Q_PER_KV + g
            qj = q[:, j * HEAD_DIM:(j + 1) * HEAD_DIM]
            s_loc = jnp.where(valid, _dot_nt(qj, kw[:, hs]) * ATTN_SCALE, NEG_BIG)
            s_ctx = _dot_nt(qj, ck[:, hs]) * ATTN_SCALE
            sk = sink_ref[e, j]
            m = jnp.maximum(jnp.maximum(jnp.max(s_loc, axis=-1, keepdims=True),
                                        jnp.max(s_ctx, axis=-1, keepdims=True)), sk)
            p_loc = jnp.exp(s_loc - m)
            p_ctx = jnp.exp(s_ctx - m)
            denom = (jnp.sum(p_loc, axis=-1, keepdims=True) + jnp.sum(p_ctx, axis=-1, keepdims=True)
                     + jnp.exp(sk - m))
            o = _dot(p_loc, vw[:, hs]) + _dot(p_ctx, cv[:, hs])
            o_ref[:, j * HEAD_DIM:(j + 1) * HEAD_DIM] = (o / denom).astype(o_ref.dtype)


def _attn_lat(e, sink, q, k, v, ck, cv):
    past = ck.shape[1]
    q_blk0 = T_CTX // ATTN_BLOCK
    kv_blk0 = T_CTX // LAT_LEN
    q_spec = pl.BlockSpec((ATTN_BLOCK, Q_DIM), lambda b, n: (q_blk0 + b * LAT_BLOCKS + n, 0))
    return pl.pallas_call(
        functools.partial(_attn_lat_kernel, e),
        grid=(N_LAT_SEQ, LAT_BLOCKS),
        in_specs=[pl.BlockSpec(memory_space=pltpu.SMEM),
                  q_spec,
                  pl.BlockSpec((LAT_LEN, KV_DIM), lambda b, n: (kv_blk0 + b, 0)),
                  pl.BlockSpec((LAT_LEN, KV_DIM), lambda b, n: (kv_blk0 + b, 0)),
                  pl.BlockSpec((1, past, KV_DIM), lambda b, n: (b, 0, 0)),
                  pl.BlockSpec((1, past, KV_DIM), lambda b, n: (b, 0, 0))],
        out_specs=pl.BlockSpec((ATTN_BLOCK, Q_DIM), lambda b, n: (b * LAT_BLOCKS + n, 0)),
        out_shape=jax.ShapeDtypeStruct((T_LAT, Q_DIM), BF16),
        compiler_params=_params(),
        name="attn_lat",
    )(sink, q, k, v, ck, cv)


def _router(h2, wr_ref):
    hi = h2.astype(BF16)
    lo = (h2 - hi.astype(F32)).astype(BF16)
    whi = wr_ref[0]
    wlo = wr_ref[1]
    dot = functools.partial(jnp.dot, preferred_element_type=F32)
    logits = dot(hi, whi) + (dot(lo, whi) + dot(hi, wlo))
    lane = lax.broadcasted_iota(I32, logits.shape, 1).astype(F32)
    neg = jnp.float32(-jnp.inf)
    big = jnp.float32(1e9)
    is_grp = lane < N_GROUPS
    gl = jnp.where(is_grp, logits, neg)
    gmax = jnp.max(gl, axis=-1, keepdims=True)
    gsum = jnp.sum(jnp.where(is_grp, jnp.exp(logits - gmax), 0.0), axis=-1, keepdims=True)
    g_w = 1.0 / gsum
    g_idx = jnp.min(jnp.where(gl == gmax, lane, big), axis=-1, keepdims=True)
    base = N_GROUPS + EPG * g_idx
    in_grp = jnp.logical_and(lane >= base, lane < base + EPG)
    el = jnp.where(in_grp, logits, neg)
    t1 = jnp.max(el, axis=-1, keepdims=True)
    i1 = jnp.min(jnp.where(el == t1, lane, big), axis=-1, keepdims=True)
    el2 = jnp.where(lane == i1, neg, el)
    t2 = jnp.max(el2, axis=-1, keepdims=True)
    i2 = jnp.min(jnp.where(el2 == t2, lane, big), axis=-1, keepdims=True)
    d = jnp.exp(t2 - t1)
    w1 = g_w / (1.0 + d)
    w2 = g_w * d / (1.0 + d)
    return i1 - N_GROUPS, i2 - N_GROUPS, w1, w2


LIST_SPAN_LOG2 = 5
LIST_SPAN = 1 << LIST_SPAN_LOG2
LIST_ROWS = LANES
LIST_COLS = N_EXPERTS * LIST_SPAN


N_TAIL_OUT = 7
N_TAIL_SCRATCH = 3


def _mixer_tail(l, x, y, mod_ref, r, gffn_ref, wr_ref, outs, scratch):
    x1_ref, h2_ref, cnt_ref, lc_ref, gc_ref, ll_ref, gl_ref = outs
    run_ref, acc_ref, gacc_ref = scratch
    i = pl.program_id(0)
    g1 = mod_ref[l, pl.ds(r, 1), 2 * D:3 * D]
    shift2 = mod_ref[l, pl.ds(r, 1), 3 * D:4 * D]
    scale2 = mod_ref[l, pl.ds(r, 1), 4 * D:5 * D]
    x1 = x + g1 * y
    x1_ref[...] = x1
    h2 = _rms_mod(x1, gffn_ref[l:l + 1, :], scale2, shift2)
    for s in range(TOK_ROWS):
        h2_ref[pl.ds(s, TM, stride=TOK_ROWS), :] = h2[:, s * LANES:(s + 1) * LANES]
    e1, e2, w1, w2 = _router(h2, wr_ref)

    @pl.when(jnp.logical_or(i == 0, i == MOE_SPLIT))
    def _():
        run_ref[...] = jnp.zeros(run_ref.shape, F32)
        acc_ref[...] = jnp.zeros(acc_ref.shape, F32)
        gacc_ref[...] = jnp.zeros(gacc_ref.shape, F32)

    @pl.when(i == 0)
    def _():
        cnt_ref[...] = jnp.zeros(cnt_ref.shape, I32)

    lane_i = lax.broadcasted_iota(I32, (TM, LANES), 1)
    lane = lane_i.astype(F32)
    member = jnp.where(jnp.logical_or(lane == e1, lane == e2), 1.0, 0.0)
    tri = jnp.where(lax.broadcasted_iota(I32, (TM, TM), 0) >= lax.broadcasted_iota(I32, (TM, TM), 1), 1.0, 0.0)
    csum = jnp.dot(tri.astype(BF16), member.astype(BF16), preferred_element_type=F32)
    run = run_ref[0:1, :]
    before = csum - member + run
    r1 = jnp.sum(jnp.where(lane == e1, before, 0.0), axis=-1, keepdims=True)
    r2 = jnp.sum(jnp.where(lane == e2, before, 0.0), axis=-1, keepdims=True)
    run_new = run + csum[TM - 1:TM, :]
    run_ref[0:1, :] = run_new

    col_i = lax.broadcasted_iota(I32, (TM, LIST_COLS), 1)
    rows_oh, cols_oh = [], []
    for e, rank in ((e1, r1), (e2, r2)):
        rank_i = rank.astype(I32)
        rows_oh.append(jnp.where(lane_i == rank_i // LIST_SPAN, 1.0, 0.0))
        cols_oh.append(jnp.where(col_i == e.astype(I32) * LIST_SPAN + rank_i % LIST_SPAN, 1.0, 0.0).astype(BF16))
    rows_t = jnp.concatenate(rows_oh, axis=0).T
    cols = jnp.concatenate(cols_oh, axis=0)
    tok = (lax.broadcasted_iota(I32, (1, 2 * TM), 1) % TM).astype(F32)
    rec = jnp.where(lane_i == 0, w1, jnp.where(lane_i == 1, w2, 0.0)).T
    rest = jnp.concatenate([rec[0:1, :], rec[1:2, :]], axis=1)
    values = [None, tok]
    for _ in range(3):
        piece = rest.astype(BF16).astype(F32)
        values.append(piece)
        rest = rest - piece
    planes = jnp.concatenate([rows_t if v is None else rows_t * v for v in values], axis=0).astype(BF16)
    out = jnp.dot(planes, cols, preferred_element_type=F32)
    hits, tok_sum, gp0, gp1, gp2 = [out[k * LANES:(k + 1) * LANES] for k in range(len(values))]
    tile_in_pop = jnp.where(i < MOE_SPLIT, i, i - MOE_SPLIT).astype(F32)
    acc = acc_ref[...] + TOK_ROWS * (tok_sum + TM * tile_in_pop * hits)
    acc_ref[...] = acc
    gacc = gacc_ref[...] + gp0 + gp1 + gp2
    gacc_ref[...] = gacc

    @pl.when(i == MOE_SPLIT - 1)
    def _():
        cnt_ref[0:1, :] = run_new.astype(I32)
        lc_ref[...] = acc.astype(I32)
        gc_ref[...] = gacc

    @pl.when(i == NT - 1)
    def _():
        cnt_ref[1:2, :] = run_new.astype(I32)
        ll_ref[...] = acc.astype(I32)
        gl_ref[...] = gacc


_TAIL_OUT_SHAPES = [jax.ShapeDtypeStruct((T_ALL, D), F32),
                    jax.ShapeDtypeStruct((T_ALL * TOK_ROWS, LANES), F32),
                    jax.ShapeDtypeStruct((SUBLANES, LANES), I32)] + [
                        jax.ShapeDtypeStruct((LIST_ROWS, LIST_COLS), dt) for dt in (I32, F32, I32, F32)]


def _tail_scratch():
    return [pltpu.VMEM((SUBLANES, LANES), F32), pltpu.VMEM((LIST_ROWS, LIST_COLS), F32),
            pltpu.VMEM((LIST_ROWS, LIST_COLS), F32)]


def _tail_out_specs():
    return [_tile_spec(D),
            pl.BlockSpec((TM * TOK_ROWS, LANES), lambda i: (i, 0)),
            _full_spec((SUBLANES, LANES))] + [_full_spec((LIST_ROWS, LIST_COLS)) for _ in range(4)]


def _halo_specs(halo, width):
    per = TM // halo
    last = T_ALL // halo - 1
    prev = pl.BlockSpec((halo, width), lambda i: (jnp.maximum(i * per - 1, 0), 0))
    nxt = pl.BlockSpec((halo, width), lambda i: (jnp.minimum((i + 1) * per, last), 0))
    return prev, nxt


def _mid_even_kernel(l, a_ref, ap_ref, an_ref, ybc_ref, ybl_ref, xc_ref, xl_ref, mod_ref, wp_ref, ps_ref, wo_ref,
                     gffn_ref, wr_ref, *refs):
    outs, (pad_ref,), scratch = refs[:N_TAIL_OUT], refs[N_TAIL_OUT:-N_TAIL_SCRATCH], refs[-N_TAIL_SCRATCH:]
    i = pl.program_id(0)
    r = _mod_row(i)
    off, n, first, last = _seq_info(i)
    a = a_ref[...]
    pad_ref[0:POOL_HALO, :] = jnp.where(first, 0.0, ap_ref[...])
    pad_ref[POOL_HALO:POOL_HALO + TM, :] = a
    pad_ref[POOL_HALO + TM:, :] = jnp.where(last, 0.0, an_ref[...])
    t = off + lax.broadcasted_iota(I32, (TM, 1), 0)
    mixed = []
    for g, w in enumerate(POOL_WINDOWS):
        lo = w // 2
        hi = w - lo - 1
        cols = slice(g * POOL_GROUP_DIM, (g + 1) * POOL_GROUP_DIM)
        total = pad_ref[POOL_HALO - lo:POOL_HALO - lo + TM, cols]
        for j in range(-lo + 1, hi + 1):
            total = total + pad_ref[POOL_HALO + j:POOL_HALO + j + TM, cols]
        count = (jnp.minimum(t + hi, n - 1) - jnp.maximum(t - lo, 0) + 1).astype(F32)
        pooled = total / count - a[:, cols]
        mixed.append(_dot(pooled, wp_ref[g]))
    ya = jnp.concatenate(mixed, axis=1) * ps_ref[...]
    y_pool = _dot(ya, wo_ref[0:POOL_DIM, :])

    def body(pop):
        y = y_pool + _dot((ybc_ref, ybl_ref)[pop][...], wo_ref[POOL_DIM:, :])
        _mixer_tail(l, (xc_ref, xl_ref)[pop][...], y, mod_ref, r, gffn_ref, wr_ref, outs, scratch)

    _per_population(i, body)


def _mid_even(l, e, a, yb_c, yb_l, xc, xl, mod, w_pool, pool_scale, w_out, gffn, wr):
    prev, nxt = _halo_specs(POOL_HALO, POOL_DIM)
    return pl.pallas_call(
        functools.partial(_mid_even_kernel, l),
        grid=(NT,),
        in_specs=[_tile_spec(POOL_DIM), prev, nxt] + _pop_specs(TM, Q_DIM) + _pop_specs(TM, D) + [
                  _MOD_SPEC, _layer_spec((len(POOL_WINDOWS), POOL_GROUP_DIM, POOL_GROUP_DIM), e),
                  _layer_spec((1, POOL_DIM), e), _layer_spec((D, D), e), _full_spec((DEPTH, D)),
                  _layer_spec((2, D, LANES), l)],
        out_specs=_tail_out_specs(),
        out_shape=_TAIL_OUT_SHAPES,
        scratch_shapes=[pltpu.VMEM((TM + 2 * POOL_HALO, POOL_DIM), F32)] + _tail_scratch(),
        compiler_params=_params(),
        name="mid_even",
    )(a, a, a, yb_c, yb_l, xc, xl, mod, w_pool, pool_scale, w_out, gffn, wr)


def _after_moe(l, r, x1_ref, moe_ref, mod_ref):
    return x1_ref[...] + mod_ref[l, pl.ds(r, 1), 5 * D:6 * D] * _from_token_tiles(moe_ref)


def _in_odd_kernel(l, x1_ref, moec_ref, moel_ref, mod_ref, g_ref, w1_ref, b1_ref, u_ref):
    i = pl.program_id(0)
    r = _mod_row(i)

    def body(pop):
        x2 = _after_moe(l - 1, r, x1_ref, (moec_ref, moel_ref)[pop], mod_ref)
        shift = mod_ref[l, pl.ds(r, 1), 0:D]
        scale = mod_ref[l, pl.ds(r, 1), D:2 * D]
        h = _rms_mod(x2, g_ref[l:l + 1, :], scale, shift)
        u = _dot(h, w1_ref[...]) + b1_ref[...]
        u_ref[...] = u[:, :D] * (1.0 / (1.0 + jnp.exp(-u[:, D:])))

    _per_half(i, body)


def _in_odd(l, o, x1, moe_c, moe_l, mod, g, w1, b1):
    return pl.pallas_call(
        functools.partial(_in_odd_kernel, l),
        grid=(NT,),
        in_specs=[_tile_spec(D)] + _half_specs(TM * TOK_ROWS, LANES) + [
                  _MOD_SPEC, _full_spec((DEPTH, D)), _layer_spec((D, 2 * D), o), _layer_spec((1, 2 * D), o)],
        out_specs=_tile_spec(D),
        out_shape=jax.ShapeDtypeStruct((T_ALL, D), F32),
        compiler_params=_params(),
        name="in_odd",
    )(x1, moe_c, moe_l, mod, g, w1, b1)


CONV_PAD_ROWS = TM + 2 * CONV_HALO
CONV_CHUNKS = D // LANES
CONV_BLOCK = 64


def _mid_odd_kernel(l, u_ref, up_ref, un_ref, x1_ref, moec_ref, moel_ref, mod_ref, dw_ref, dwb_ref, lng_ref, lnb_ref,
                    w2_ref, b2_ref, gffn_ref, wr_ref, *refs):
    outs, scratch = refs[:N_TAIL_OUT], refs[-N_TAIL_SCRATCH:]
    pad_ref, conv_ref, x_ref = refs[N_TAIL_OUT:-N_TAIL_SCRATCH]
    i = pl.program_id(0)

    def residual(pop):
        x_ref[...] = _after_moe(l - 1, _mod_row(i), x1_ref, (moec_ref, moel_ref)[pop], mod_ref)

    _per_half(i, residual)
    r = _mod_row(i)
    _, _, first, last = _seq_info(i)
    u = u_ref[...]
    up = jnp.where(first, 0.0, up_ref[...])
    un = jnp.where(last, 0.0, un_ref[...])
    half = CONV_WIDTH // 2
    first_row = CONV_HALO - half
    for c in range(CONV_CHUNKS):
        cols = slice(c * LANES, (c + 1) * LANES)
        pad_ref[c, 0:CONV_HALO, :] = up[:, cols]
        pad_ref[c, CONV_HALO:CONV_HALO + TM, :] = u[:, cols]
        pad_ref[c, CONV_HALO + TM:, :] = un[:, cols]
        bias = jnp.broadcast_to(dwb_ref[:, cols], (CONV_BLOCK, LANES))
        for r0 in range(0, TM, 2 * CONV_BLOCK):
            acc_even, acc_odd = bias, bias
            for s in range(first_row, first_row + CONV_WIDTH + 1):
                win = pad_ref[c, pl.ds(r0 + s, CONV_BLOCK, stride=2), :]
                j = s - first_row
                if j < CONV_WIDTH:
                    acc_even = acc_even + win * dw_ref[j:j + 1, cols]
                if j >= 1:
                    acc_odd = acc_odd + win * dw_ref[j - 1:j, cols]
            conv_ref[c, pl.ds(r0, CONV_BLOCK, stride=2), :] = acc_even
            conv_ref[c, pl.ds(r0 + 1, CONV_BLOCK, stride=2), :] = acc_odd
    acc = jnp.concatenate([conv_ref[c] for c in range(CONV_CHUNKS)], axis=1)
    mu = jnp.mean(acc, axis=-1, keepdims=True)
    cen = acc - mu
    var = jnp.mean(cen * cen, axis=-1, keepdims=True)
    v = _silu(cen * lax.rsqrt(var + EPS) * lng_ref[...] + lnb_ref[...])
    y = _dot(v, w2_ref[...]) + b2_ref[...]
    _mixer_tail(l, x_ref[...], y, mod_ref, r, gffn_ref, wr_ref, outs, scratch)


def _mid_odd(l, o, u, x1, moe_c, moe_l, mod, dw, dwb, lng, lnb, w2, b2, gffn, wr):
    prev, nxt = _halo_specs(CONV_HALO, D)
    return pl.pallas_call(
        functools.partial(_mid_odd_kernel, l),
        grid=(NT,),
        in_specs=[_tile_spec(D), prev, nxt, _tile_spec(D)] + _half_specs(TM * TOK_ROWS, LANES) + [
                  _MOD_SPEC, _layer_spec((CONV_WIDTH, D), o),
                  _layer_spec((1, D), o), _layer_spec((1, D), o), _layer_spec((1, D), o),
                  _layer_spec((D, D), o), _layer_spec((1, D), o), _full_spec((DEPTH, D)),
                  _layer_spec((2, D, LANES), l)],
        out_specs=_tail_out_specs(),
        out_shape=_TAIL_OUT_SHAPES,
        scratch_shapes=[pltpu.VMEM((CONV_CHUNKS, CONV_PAD_ROWS, LANES), F32),
                        pltpu.VMEM((CONV_CHUNKS, TM, LANES), F32), pltpu.VMEM((TM, D), F32)] + _tail_scratch(),
        compiler_params=_params(),
        name="mid_odd",
    )(u, u, u, x1, moe_c, moe_l, mod, dw, dwb, lng, lnb, w2, b2, gffn, wr)


def _tile_tables_kernel(cnt_ref, te_c, tc_c, tw_c, te_l, tc_l, tw_l, nact_ref, second_ref, first_ref, wplan_ref):
    for p, (te_ref, tc_ref, tw_ref) in enumerate(((te_c, tc_c, tw_c), (te_l, tc_l, tw_l))):
        run = jnp.int32(0)
        ends = []
        for e in range(N_EXPERTS):
            first_ref[p * N_EXPERTS + e] = run
            run = run + (cnt_ref[p, e] + MOE_TM - 1) // MOE_TM
            ends.append(run)
        nact_ref[p] = run

        nxt1, nxt2 = jnp.int32(0), jnp.int32(0)
        plans = []
        for e in reversed(range(N_EXPERTS)):
            plans.append(nxt2)
            used = cnt_ref[p, e] > 0
            nxt2 = jnp.where(used, nxt1, nxt2)
            nxt1 = jnp.where(used, e + 1, nxt1)
        plans.reverse()
        second_ref[p] = nxt2
        order = jnp.int32(0)
        for e in range(N_EXPERTS):
            wplan_ref[p * N_EXPERTS + e] = order + (plans[e] << W_AHEAD_SHIFT)
            order = jnp.where(cnt_ref[p, e] > 0, jnp.where(order == W_BUFFERS - 1, 0, order + 1), order)

        def expert_of(tile, ends=ends):
            e = jnp.int32(0)
            for k in range(N_EXPERTS):
                e = e + jnp.where(tile >= ends[k], 1, 0)
            return e

        last_e = expert_of(run - 1)

        def tile_body(j, c, te_ref=te_ref, tc_ref=tc_ref, tw_ref=tw_ref, p=p, run=run, expert_of=expert_of,
                      last_e=last_e):
            e = jnp.minimum(expert_of(j), last_e)
            te_ref[j] = e
            chunk = jnp.where(j < run, j - first_ref[p * N_EXPERTS + e], 0)
            tc_ref[j] = chunk
            starts_expert = jnp.logical_and(j < run, chunk == 0)
            tw_ref[j] = wplan_ref[p * N_EXPERTS + e] + jnp.where(starts_expert, W_FIRST_UNIT, 0)
            return c

        lax.fori_loop(0, te_ref.shape[0], tile_body, 0)


def _tile_tables(cnt):
    smem = pl.BlockSpec(memory_space=pltpu.SMEM)
    shapes = [(MOE_TILES,)] * 6 + [(2,), (2,)]
    return pl.pallas_call(
        _tile_tables_kernel,
        in_specs=[smem],
        out_specs=[smem] * len(shapes),
        out_shape=[jax.ShapeDtypeStruct(s, I32) for s in shapes],
        scratch_shapes=[pltpu.SMEM((2 * N_EXPERTS,), I32), pltpu.SMEM((2 * N_EXPERTS,), I32)],
        name="tile_tables",
    )(cnt)


def _moe_kernel(l, pop, te_ref, tc_ref, tw_ref, nact_ref, second_ref, list_ref, h_ref, gate_ref,
                w1_hbm, w3_hbm, w2_hbm, o_ref, xs_ref, ys_ref, w1_buf, w3_buf, w2_buf, w_sem):
    j = pl.program_id(0)
    nact = nact_ref[pop]
    last_tile = te_ref.shape[0] - 1
    tile_rows = MOE_TM * TOK_ROWS

    def weight_copies(expert, buf):
        return [pltpu.make_async_copy(hbm.at[l, expert], vmem.at[buf], w_sem.at[buf, k])
                for k, (hbm, vmem) in enumerate(((w1_hbm, w1_buf), (w3_hbm, w3_buf), (w2_hbm, w2_buf)))]

    def list_base(tile):
        return tc_ref[tile] * (MOE_TM // LIST_SPAN) * LIST_COLS + te_ref[tile] * LIST_SPAN

    def tok_rows(base, r):
        first = list_ref[base + (r // LIST_SPAN) * LIST_COLS + r % LIST_SPAN]
        return pl.ds(pl.multiple_of(first, TOK_ROWS), TOK_ROWS)

    def buf_rows(buf, r):
        return pl.ds(pl.multiple_of(buf * tile_rows + r * TOK_ROWS, TOK_ROWS), TOK_ROWS)

    def gather(tile, buf):
        base = list_base(tile)
        for r in range(MOE_TM):
            xs_ref[buf_rows(buf, r), :] = h_ref[tok_rows(base, r), :]

    tile = jnp.minimum(j, last_tile)
    plan = tw_ref[tile]
    wbuf = plan & (W_FIRST_UNIT - 1)
    starts_expert = jnp.logical_and(plan & W_FIRST_UNIT != 0, j <= last_tile)
    ahead_expert = (plan >> W_AHEAD_SHIFT) - 1
    ahead_buf = jnp.where(wbuf == 0, W_BUFFERS - 1, wbuf - 1)

    @pl.when(j == 0)
    def _():
        for cp in weight_copies(te_ref[0], 0):
            cp.start()

        @pl.when(second_ref[pop] > 0)
        def _():
            for cp in weight_copies(second_ref[pop] - 1, 1):
                cp.start()

        o_ref[...] = jnp.zeros(o_ref.shape, F32)
        ys_ref[...] = jnp.zeros(ys_ref.shape, F32)
        gather(0, 0)

    @pl.when(starts_expert)
    def _():
        for cp in weight_copies(te_ref[tile], wbuf):
            cp.wait()

        @pl.when(ahead_expert >= 0)
        def _():
            for cp in weight_copies(ahead_expert, ahead_buf):
                cp.start()

    @pl.when(j <= nact)
    def _():
        cur = j & 1
        w1_ref, w3_ref, w2_ref = w1_buf.at[wbuf], w3_buf.at[wbuf], w2_buf.at[wbuf]
        x = jnp.concatenate([xs_ref[pl.ds(cur * tile_rows + s, MOE_TM, stride=TOK_ROWS), :]
                             for s in range(TOK_ROWS)], axis=1).astype(BF16)
        gather(jnp.minimum(j + 1, last_tile), 1 - cur)

        prev_base = list_base(jnp.maximum(j - 1, 0))
        for r0 in range(0, MOE_TM, MOVE_BATCH):
            dst = [tok_rows(prev_base, r0 + k) for k in range(MOVE_BATCH)]
            vals = [o_ref[dst[k], :] + ys_ref[buf_rows(1 - cur, r0 + k), :] for k in range(MOVE_BATCH)]
            for k in reversed(range(MOVE_BATCH)):
                o_ref[dst[k], :] = vals[k]

        hg = jnp.dot(x, w1_ref[...].astype(BF16), preferred_element_type=F32)
        hu = jnp.dot(x, w3_ref[...].astype(BF16), preferred_element_type=F32)
        y = _dot(_silu(hg) * hu, w2_ref[...])

        first_row = pl.multiple_of(tc_ref[tile] * (MOE_TM // LIST_SPAN), MOE_TM // LIST_SPAN)
        spread = jnp.concatenate(
            [jnp.broadcast_to(gate_ref[pl.ds(first_row + k, 1), :], (LIST_SPAN, LIST_COLS))
             for k in range(MOE_TM // LIST_SPAN)], axis=0)
        want = te_ref[tile] * LIST_SPAN + (lax.broadcasted_iota(I32, (MOE_TM, 1), 0) & (LIST_SPAN - 1))
        hit = jnp.logical_and(lax.broadcasted_iota(I32, (MOE_TM, LIST_COLS), 1) == want, j < nact)
        y = y * jnp.sum(jnp.where(hit, spread, 0.0), axis=1, keepdims=True)
        for s in range(TOK_ROWS):
            ys_ref[pl.ds(cur * tile_rows + s, MOE_TM, stride=TOK_ROWS), :] = y[:, s * LANES:(s + 1) * LANES]


def _moe(l, pop, h_tt, te, tc, tw, nact, second, rows, gate, w1, w3, w2):
    tp = T_HALF
    blk = pop
    n_tiles = te.shape[0]
    hbm = pl.BlockSpec(memory_space=pl.ANY)
    grid_spec = pltpu.PrefetchScalarGridSpec(
        num_scalar_prefetch=6,
        grid=(n_tiles + 1,),
        in_specs=[
            pl.BlockSpec((tp * TOK_ROWS, LANES), lambda j, *_: (blk, 0), pipeline_mode=pl.Buffered(1)),
            pl.BlockSpec((LIST_ROWS, LIST_COLS), lambda j, *_: (0, 0), pipeline_mode=pl.Buffered(1)),
            hbm, hbm, hbm,
        ],
        out_specs=pl.BlockSpec((tp * TOK_ROWS, LANES), lambda j, *_: (0, 0), pipeline_mode=pl.Buffered(1)),
        scratch_shapes=[pltpu.VMEM((2 * MOE_TM * TOK_ROWS, LANES), F32),
                        pltpu.VMEM((2 * MOE_TM * TOK_ROWS, LANES), F32),
                        pltpu.VMEM((W_BUFFERS, D, EXPERT_HIDDEN), F32),
                        pltpu.VMEM((W_BUFFERS, D, EXPERT_HIDDEN), F32),
                        pltpu.VMEM((W_BUFFERS, EXPERT_HIDDEN, D), F32),
                        pltpu.SemaphoreType.DMA((W_BUFFERS, 3))],
    )
    return pl.pallas_call(
        functools.partial(_moe_kernel, l, pop),
        grid_spec=grid_spec,
        out_shape=jax.ShapeDtypeStruct((tp * TOK_ROWS, LANES), F32),
        compiler_params=_params(),
        name="moe",
    )(te, tc, tw, nact, second, rows, h_tt, gate, w1, w3, w2)


def _moe_all(l, h_tt, cnt, rows_c, gate_c, rows_l, gate_l, w1, w3, w2):
    te_c, tc_c, tw_c, te_l, tc_l, tw_l, nact, second = _tile_tables(cnt)
    out_c = _moe(l, 0, h_tt, te_c, tc_c, tw_c, nact, second, rows_c.reshape(-1), gate_c, w1, w3, w2)
    out_l = _moe(l, 1, h_tt, te_l, tc_l, tw_l, nact, second, rows_l.reshape(-1), gate_l, w1, w3, w2)
    return out_c, out_l


def _final_kernel(l, x1_ref, moec_ref, moel_ref, mod_ref, g_ref, oc_ref, ol_ref):
    i = pl.program_id(0)
    r = _mod_row(i)

    def body(half, pop):
        x = _after_moe(l, r, x1_ref, (moec_ref, moel_ref)[half], mod_ref)
        ms = jnp.mean(x * x, axis=-1, keepdims=True)
        (oc_ref, ol_ref)[pop][...] = x * lax.rsqrt(ms + EPS) * g_ref[...]

    pl.when(i < MOE_SPLIT)(functools.partial(body, 0, 0))
    pl.when(jnp.logical_and(i >= MOE_SPLIT, i < NT_CTX))(functools.partial(body, 1, 0))
    pl.when(i >= NT_CTX)(functools.partial(body, 1, 1))


def _final(l, x1, moe_c, moe_l, mod, g):
    return pl.pallas_call(
        functools.partial(_final_kernel, l),
        grid=(NT,),
        in_specs=[_tile_spec(D)] + _half_specs(TM * TOK_ROWS, LANES) + [_MOD_SPEC, _full_spec((1, D))],
        out_specs=_pop_specs(TM, D),
        out_shape=[jax.ShapeDtypeStruct((T_CTX, D), F32), jax.ShapeDtypeStruct((T_LAT, D), F32)],
        compiler_params=_params(),
        name="final_norm",
    )(x1, moe_c, moe_l, mod, g)


def _rope_tables():
    t = jnp.arange(LAT_LEN)
    row = (t // GRID_W).astype(F32)
    col = (t % GRID_W).astype(F32)
    inv = ROPE_BASE ** (-jnp.arange(0, ROPE_AXIS_DIM, 2, dtype=F32) / ROPE_AXIS_DIM)

    def table(p):
        ang = p[:, None] * inv[None, :]
        ang = jnp.concatenate([ang, ang], axis=-1)
        return jnp.cos(ang), jnp.sin(ang)

    (cr, sr), (cc, sc) = table(row), table(col)
    cos = jnp.concatenate([cr, cc, cr, cc], axis=-1)
    sin = jnp.concatenate([sr, sc, sr, sc], axis=-1)
    return cos, sin


def _router_weights(w_grp, w_exp):
    we = jnp.transpose(w_exp, (0, 2, 1, 3)).reshape(DEPTH, D, N_GROUPS * EPG)
    pad = jnp.zeros((DEPTH, D, LANES - N_GROUPS - N_GROUPS * EPG), F32)
    wr = jnp.concatenate([w_grp, we, pad], axis=-1)
    hi = lax.bitcast_convert_type(lax.bitcast_convert_type(wr, jnp.uint32) & jnp.uint32(0xFFFF0000), F32)
    return jnp.stack([hi.astype(BF16), (wr - hi).astype(BF16)], axis=1)


def kernel(x_prompt, x_sample, cache_k, cache_v, c, c_ctx, w_ada, b_ada, norm_mix_g, norm_ffn_g, w_in_ab, pool_w, pool_scale, attn_sink, w_out_ab, conv_w1, conv_b1, conv_dw, conv_dw_b, conv_ln_g, conv_ln_b, conv_w2, conv_b2, router_grp, router_exp, moe_w1, moe_w3, moe_w2, final_g):
    xc = x_prompt.reshape(T_CTX, D)
    xl = x_sample.reshape(T_LAT, D)
    cond8 = jnp.concatenate([c_ctx[None, :], c, jnp.zeros((SUBLANES - 1 - N_LAT_SEQ, D), F32)], axis=0)
    mod = _modulation(cond8, w_ada, b_ada)
    cos_t, sin_t = _rope_tables()
    wr = _router_weights(router_grp, router_exp)
    n_even, n_odd = w_in_ab.shape[0], conv_w1.shape[0]
    past = cache_k.shape[2]

    a, q, k, v = _in_even(0, 0, xc, xl, mod, norm_mix_g, w_in_ab, cos_t, sin_t)
    yb_c = _attn_ctx(0, attn_sink, q, k, v)
    yb_l = _attn_lat(0, attn_sink, q, k, v, cache_k[:, 0].reshape(N_LAT_SEQ, past, KV_DIM),
                     cache_v[:, 0].reshape(N_LAT_SEQ, past, KV_DIM))
    x1, h2, *routing = _mid_even(0, 0, a, yb_c, yb_l, xc, xl, mod, pool_w,
                                 pool_scale.reshape(n_even, 1, POOL_DIM), w_out_ab, norm_ffn_g, wr)
    moe0 = _moe_all(0, h2, *routing, moe_w1, moe_w3, moe_w2)

    vec = lambda p: p.reshape(n_odd, 1, -1)
    u = _in_odd(1, 0, x1, *moe0, mod, norm_mix_g, conv_w1, vec(conv_b1))
    x3, h2b, *routing = _mid_odd(1, 0, u, x1, *moe0, mod, conv_dw, vec(conv_dw_b), vec(conv_ln_g),
                                 vec(conv_ln_b), conv_w2, vec(conv_b2), norm_ffn_g, wr)
    moe1 = _moe_all(1, h2b, *routing, moe_w1, moe_w3, moe_w2)
    y_c, y_l = _final(1, x3, *moe1, mod, final_g.reshape(1, D))

    y_prompt = y_c.reshape(N_CTX_SEQ, CTX_LEN, D)
    y_sample = y_l.reshape(N_LAT_SEQ, LAT_LEN, D)
    state_k = k[:T_CTX].reshape(N_CTX_SEQ, 1, CTX_LEN, N_KV_HEADS, HEAD_DIM)
    state_v = v[:T_CTX].reshape(N_CTX_SEQ, 1, CTX_LEN, N_KV_HEADS, HEAD_DIM)
    return (y_prompt, y_sample, state_k, state_v)
```

```python
import functools

import jax
import jax.numpy as jnp
from jax import lax
from jax.experimental import pallas as pl
from jax.experimental.pallas import tpu as pltpu

F32 = jnp.float32
BF16 = jnp.bfloat16
I32 = jnp.int32

D = 1024
N_CTX_SEQ = 16
CTX_LEN = 256
N_LAT_SEQ = 2
LAT_LEN = 1024
T_CTX = N_CTX_SEQ * CTX_LEN
T_LAT = N_LAT_SEQ * LAT_LEN
T_ALL = T_CTX + T_LAT
TM = 256
NT = T_ALL // TM
NT_CTX = T_CTX // TM
LAT_TILES_PER_SEQ = LAT_LEN // TM
GRID_W = 64
DEPTH = 2

POOL_WINDOWS = (2, 4, 8, 16)
POOL_GROUP_DIM = 128
POOL_DIM = 512
HEAD_DIM = 64
N_Q_HEADS = 8
N_KV_HEADS = 2
Q_PER_KV = 4
Q_DIM = 512
KV_DIM = 128
IN_AB = 1280
ATTN_WINDOW = 128
ATTN_BLOCK = 128
ATTN_SCALE = HEAD_DIM ** -0.5
ROPE_BASE = 10000.0
ROPE_AXIS_DIM = 32
CONV_WIDTH = 31
CONV_HALO = 16
POOL_HALO = 8
N_GROUPS = 4
EPG = 4
N_EXPERTS = 16
EXPERT_HIDDEN = 512
EPS = 1e-6
NEG_BIG = -1e30

SUBLANES = 8
LANES = 128
TOK_ROWS = D // LANES

MOE_TM = 256
MOE_SPLIT = NT // 2
T_HALF = MOE_SPLIT * TM
MOE_TILES = 2 * T_HALF // MOE_TM + N_EXPERTS
W_BUFFERS = 3
W_FIRST_UNIT = 4
W_AHEAD_SHIFT = 3
MOVE_BATCH_LOG2 = 2
MOVE_BATCH = 1 << MOVE_BATCH_LOG2
VMEM_LIMIT = 56 * 1024 * 1024


def _silu(x):
    return x * (1.0 / (1.0 + jnp.exp(-x)))


def _mod_row(i):
    return jnp.where(i < NT_CTX, 0, 1 + (i - NT_CTX) // LAT_TILES_PER_SEQ)


def _seq_info(i):
    is_ctx = i < NT_CTX
    k = (i - NT_CTX) % LAT_TILES_PER_SEQ
    off = jnp.where(is_ctx, 0, k * TM)
    n = jnp.where(is_ctx, CTX_LEN, LAT_LEN)
    first = jnp.logical_or(is_ctx, k == 0)
    last = jnp.logical_or(is_ctx, k == LAT_TILES_PER_SEQ - 1)
    return off, n, first, last


def _rms_mod(x, g, scale, shift):
    ms = jnp.mean(x * x, axis=-1, keepdims=True)
    return (x * lax.rsqrt(ms + EPS) * g) * (1.0 + scale) + shift


def _dot(a, b):
    return jnp.dot(a.astype(BF16), b.astype(BF16), preferred_element_type=F32)


def _dot_nt(a, b):
    return lax.dot_general(a.astype(BF16), b.astype(BF16), (((1,), (1,)), ((), ())),
                           preferred_element_type=F32)


def _from_token_tiles(ref):
    return jnp.concatenate([ref[pl.ds(s, TM, stride=TOK_ROWS), :] for s in range(TOK_ROWS)], axis=1)


def _tile_spec(width):
    return pl.BlockSpec((TM, width), lambda i: (i, 0))


def _full_spec(shape):
    return pl.BlockSpec(shape, lambda i: (0,) * len(shape))


def _layer_spec(shape, l):
    return pl.BlockSpec((None,) + tuple(shape), lambda i: (l,) + (0,) * len(shape))


def _pop_specs(rows, width):
    return [pl.BlockSpec((rows, width), lambda i: (jnp.minimum(i, NT_CTX - 1), 0)),
            pl.BlockSpec((rows, width), lambda i: (jnp.maximum(i - NT_CTX, 0), 0))]


def _per_population(i, body):
    pl.when(i < NT_CTX)(functools.partial(body, 0))
    pl.when(i >= NT_CTX)(functools.partial(body, 1))


def _half_specs(rows, width):
    return [pl.BlockSpec((rows, width), lambda i: (jnp.minimum(i, MOE_SPLIT - 1), 0)),
            pl.BlockSpec((rows, width), lambda i: (jnp.maximum(i - MOE_SPLIT, 0), 0))]


def _per_half(i, body):
    pl.when(i < MOE_SPLIT)(functools.partial(body, 0))
    pl.when(i >= MOE_SPLIT)(functools.partial(body, 1))


_MOD_SPEC = pl.BlockSpec((DEPTH, SUBLANES, 6 * D), lambda i: (0, 0, 0))


def _params():
    return pltpu.CompilerParams(vmem_limit_bytes=VMEM_LIMIT)


MOD_TN = 1536


def _mod_kernel(cond_ref, w_ref, b_ref, o_ref):
    s = _silu(cond_ref[...])
    o_ref[0] = _dot(s, w_ref[0]) + b_ref[0]


def _modulation(cond8, w_ada, b_ada):
    return pl.pallas_call(
        _mod_kernel,
        grid=(DEPTH, 6 * D // MOD_TN),
        in_specs=[
            pl.BlockSpec((SUBLANES, D), lambda l, n: (0, 0)),
            pl.BlockSpec((1, D, MOD_TN), lambda l, n: (l, 0, n)),
            pl.BlockSpec((1, 1, MOD_TN), lambda l, n: (l, 0, n)),
        ],
        out_specs=pl.BlockSpec((1, SUBLANES, MOD_TN), lambda l, n: (l, 0, n)),
        out_shape=jax.ShapeDtypeStruct((DEPTH, SUBLANES, 6 * D), F32),
        compiler_params=_params(),
        name="modulation",
    )(cond8, w_ada, b_ada.reshape(DEPTH, 1, 6 * D))


def _rope_chunk(xc, cos, sin):
    lane = lax.broadcasted_iota(I32, xc.shape, 1)
    first = (lane % ROPE_AXIS_DIM) < (ROPE_AXIS_DIM // 2)
    rot = jnp.where(first, -pltpu.roll(xc, LANES - ROPE_AXIS_DIM // 2, 1), pltpu.roll(xc, ROPE_AXIS_DIM // 2, 1))
    return xc * cos + rot * sin


def _in_even_kernel(l, xc_ref, xl_ref, mod_ref, g_ref, w_ref, cos_ref, sin_ref, a_ref, q_ref, k_ref, v_ref):
    i = pl.program_id(0)
    r = _mod_row(i)

    def body(pop):
        x = (xc_ref, xl_ref)[pop][...]
        shift = mod_ref[l, pl.ds(r, 1), 0:D]
        scale = mod_ref[l, pl.ds(r, 1), D:2 * D]
        h = _rms_mod(x, g_ref[l:l + 1, :], scale, shift)
        proj = _dot(h, w_ref[...])
        a_ref[...] = proj[:, :POOL_DIM]
        v_ref[...] = proj[:, POOL_DIM + Q_DIM + KV_DIM:]
        if pop == 0:
            q_ref[...] = proj[:, POOL_DIM:POOL_DIM + Q_DIM].astype(BF16)
            k_ref[...] = proj[:, POOL_DIM + Q_DIM:POOL_DIM + Q_DIM + KV_DIM]
        else:
            off, _, _, _ = _seq_info(i)
            off = pl.multiple_of(off, TM)
            cos = cos_ref[pl.ds(off, TM), :]
            sin = sin_ref[pl.ds(off, TM), :]
            for c in range(Q_DIM // LANES):
                lo = POOL_DIM + c * LANES
                q_ref[:, c * LANES:(c + 1) * LANES] = _rope_chunk(proj[:, lo:lo + LANES], cos, sin).astype(BF16)
            lo = POOL_DIM + Q_DIM
            k_ref[...] = _rope_chunk(proj[:, lo:lo + LANES], cos, sin)

    _per_population(i, body)


def _in_even(l, e, xc, xl, mod, g, w_in, cos_t, sin_t):
    return pl.pallas_call(
        functools.partial(_in_even_kernel, l),
        grid=(NT,),
        in_specs=_pop_specs(TM, D) + [_MOD_SPEC, _full_spec((DEPTH, D)), _layer_spec((D, IN_AB), e),
                  _full_spec((LAT_LEN, LANES)), _full_spec((LAT_LEN, LANES))],
        out_specs=[_tile_spec(POOL_DIM), _tile_spec(Q_DIM), _tile_spec(KV_DIM), _tile_spec(KV_DIM)],
        out_shape=[jax.ShapeDtypeStruct((T_ALL, POOL_DIM), F32), jax.ShapeDtypeStruct((T_ALL, Q_DIM), BF16),
                   jax.ShapeDtypeStruct((T_ALL, KV_DIM), F32), jax.ShapeDtypeStruct((T_ALL, KV_DIM), F32)],
        compiler_params=_params(),
        name="in_even",
    )(xc, xl, mod, g, w_in, cos_t, sin_t)


def _attn_ctx_kernel(e, sink_ref, q_ref, k_ref, v_ref, o_ref):
    q = q_ref[...]
    k = k_ref[...]
    v = v_ref[...]
    for h in range(N_KV_HEADS):
        kh = k[:, h * HEAD_DIM:(h + 1) * HEAD_DIM]
        vh = v[:, h * HEAD_DIM:(h + 1) * HEAD_DIM]
        for g in range(Q_PER_KV):
            j = h * Q_PER_KV + g
            s = _dot_nt(q[:, j * HEAD_DIM:(j + 1) * HEAD_DIM], kh) * ATTN_SCALE
            sk = sink_ref[e, j]
            m = jnp.maximum(jnp.max(s, axis=-1, keepdims=True), sk)
            p = jnp.exp(s - m)
            denom = jnp.sum(p, axis=-1, keepdims=True) + jnp.exp(sk - m)
            o_ref[:, j * HEAD_DIM:(j + 1) * HEAD_DIM] = (_dot(p, vh) / denom).astype(o_ref.dtype)


def _attn_ctx(e, sink, q, k, v):
    tile = lambda w: pl.BlockSpec((TM, w), lambda b: (b, 0))
    return pl.pallas_call(
        functools.partial(_attn_ctx_kernel, e),
        grid=(N_CTX_SEQ,),
        in_specs=[pl.BlockSpec(memory_space=pltpu.SMEM), tile(Q_DIM), tile(KV_DIM), tile(KV_DIM)],
        out_specs=tile(Q_DIM),
        out_shape=jax.ShapeDtypeStruct((T_CTX, Q_DIM), BF16),
        compiler_params=_params(),
        name="attn_ctx",
    )(sink, q, k, v)


LAT_BLOCKS = LAT_LEN // ATTN_BLOCK
BAND = 3 * ATTN_BLOCK


def _attn_lat_kernel(e, sink_ref, q_ref, k_ref, v_ref, ck_ref, cv_ref, o_ref):
    n = pl.program_id(1)
    start = jnp.clip((n - 1) * ATTN_BLOCK, 0, LAT_LEN - BAND)
    start = pl.multiple_of(start, ATTN_BLOCK)
    q = q_ref[...]
    kw = k_ref[pl.ds(start, BAND), :]
    vw = v_ref[pl.ds(start, BAND), :]
    ck = ck_ref[0]
    cv = cv_ref[0]
    qpos = n * ATTN_BLOCK + lax.broadcasted_iota(I32, (ATTN_BLOCK, BAND), 0)
    kpos = start + lax.broadcasted_iota(I32, (ATTN_BLOCK, BAND), 1)
    valid = jnp.abs(qpos - kpos) <= ATTN_WINDOW
    for h in range(N_KV_HEADS):
        hs = slice(h * HEAD_DIM, (h + 1) * HEAD_DIM)
        for g in range(Q_PER_KV):
            j = h * Q_PER_KV + g
            qj = q[:, j * HEAD_DIM:(j + 1) * HEAD_DIM]
            s_loc = jnp.where(valid, _dot_nt(qj, kw[:, hs]) * ATTN_SCALE, NEG_BIG)
            s_ctx = _dot_nt(qj, ck[:, hs]) * ATTN_SCALE
            sk = sink_ref[e, j]
            m = jnp.maximum(jnp.maximum(jnp.max(s_loc, axis=-1, keepdims=True),
                                        jnp.max(s_ctx, axis=-1, keepdims=True)), sk)
            p_loc = jnp.exp(s_loc - m)
            p_ctx = jnp.exp(s_ctx - m)
            denom = (jnp.sum(p_loc, axis=-1, keepdims=True) + jnp.sum(p_ctx, axis=-1, keepdims=True)
                     + jnp.exp(sk - m))
            o = _dot(p_loc, vw[:, hs]) + _dot(p_ctx, cv[:, hs])
            o_ref[:, j * HEAD_DIM:(j + 1) * HEAD_DIM] = (o / denom).astype(o_ref.dtype)


def _attn_lat(e, sink, q, k, v, ck, cv):
    past = ck.shape[1]
    q_blk0 = T_CTX // ATTN_BLOCK
    kv_blk0 = T_CTX // LAT_LEN
    q_spec = pl.BlockSpec((ATTN_BLOCK, Q_DIM), lambda b, n: (q_blk0 + b * LAT_BLOCKS + n, 0))
    return pl.pallas_call(
        functools.partial(_attn_lat_kernel, e),
        grid=(N_LAT_SEQ, LAT_BLOCKS),
        in_specs=[pl.BlockSpec(memory_space=pltpu.SMEM),
                  q_spec,
                  pl.BlockSpec((LAT_LEN, KV_DIM), lambda b, n: (kv_blk0 + b, 0)),
                  pl.BlockSpec((LAT_LEN, KV_DIM), lambda b, n: (kv_blk0 + b, 0)),
                  pl.BlockSpec((1, past, KV_DIM), lambda b, n: (b, 0, 0)),
                  pl.BlockSpec((1, past, KV_DIM), lambda b, n: (b, 0, 0))],
        out_specs=pl.BlockSpec((ATTN_BLOCK, Q_DIM), lambda b, n: (b * LAT_BLOCKS + n, 0)),
        out_shape=jax.ShapeDtypeStruct((T_LAT, Q_DIM), BF16),
        compiler_params=_params(),
        name="attn_lat",
    )(sink, q, k, v, ck, cv)


def _router(h2, wr_ref):
    hi = h2.astype(BF16)
    lo = (h2 - hi.astype(F32)).astype(BF16)
    whi = wr_ref[0]
    wlo = wr_ref[1]
    dot = functools.partial(jnp.dot, preferred_element_type=F32)
    logits_t = (dot(hi, whi) + (dot(lo, whi) + dot(hi, wlo))).T
    row = lambda k: logits_t[k:k + 1, :]

    def first_max(vals):
        best = functools.reduce(jnp.maximum, vals)
        idx = jnp.full(best.shape, len(vals) - 1, F32)
        for k in reversed(range(len(vals) - 1)):
            idx = jnp.where(vals[k] == best, float(k), idx)
        return best, idx

    grp = [row(k) for k in range(N_GROUPS)]
    gmax, g_idx = first_max(grp)
    g_w = 1.0 / sum(jnp.exp(gk - gmax) for gk in grp)
    el = []
    for e in range(EPG):
        v = row(N_GROUPS + (N_GROUPS - 1) * EPG + e)
        for k in reversed(range(N_GROUPS - 1)):
            v = jnp.where(g_idx == k, row(N_GROUPS + k * EPG + e), v)
        el.append(v)
    t1, i1 = first_max(el)
    t2, i2 = first_max([jnp.where(i1 == e, -jnp.inf, el[e]) for e in range(EPG)])
    d = jnp.exp(t2 - t1)
    w1 = g_w / (1.0 + d)
    w2 = g_w * d / (1.0 + d)
    return EPG * g_idx + i1, EPG * g_idx + i2, w1, w2


LIST_SPAN_LOG2 = 5
LIST_SPAN = 1 << LIST_SPAN_LOG2
LIST_ROWS = LANES
LIST_COLS = N_EXPERTS * LIST_SPAN


N_TAIL_OUT = 7
N_TAIL_SCRATCH = 3


def _mixer_tail(l, x, y, mod_ref, r, gffn_ref, wr_ref, outs, scratch):
    x1_ref, h2_ref, cnt_ref, lc_ref, gc_ref, ll_ref, gl_ref = outs
    run_ref, acc_ref, gacc_ref = scratch
    i = pl.program_id(0)
    g1 = mod_ref[l, pl.ds(r, 1), 2 * D:3 * D]
    shift2 = mod_ref[l, pl.ds(r, 1), 3 * D:4 * D]
    scale2 = mod_ref[l, pl.ds(r, 1), 4 * D:5 * D]
    x1 = x + g1 * y
    x1_ref[...] = x1
    h2 = _rms_mod(x1, gffn_ref[l:l + 1, :], scale2, shift2)
    for s in range(TOK_ROWS):
        h2_ref[pl.ds(s, TM, stride=TOK_ROWS), :] = h2[:, s * LANES:(s + 1) * LANES]
    e1, e2, w1_row, w2_row = _router(h2, wr_ref)

    @pl.when(jnp.logical_or(i == 0, i == MOE_SPLIT))
    def _():
        run_ref[...] = jnp.zeros(run_ref.shape, F32)
        acc_ref[...] = jnp.zeros(acc_ref.shape, F32)
        gacc_ref[...] = jnp.zeros(gacc_ref.shape, F32)

    @pl.when(i == 0)
    def _():
        cnt_ref[...] = jnp.zeros(cnt_ref.shape, I32)

    sub_i = lax.broadcasted_iota(I32, (LANES, TM), 0)
    sub = sub_i.astype(F32)
    member = jnp.where(jnp.logical_or(sub == e1, sub == e2), 1.0, 0.0)
    tri = jnp.where(lax.broadcasted_iota(I32, (TM, TM), 0) <= lax.broadcasted_iota(I32, (TM, TM), 1), 1.0, 0.0)
    csum = jnp.dot(member.astype(BF16), tri.astype(BF16), preferred_element_type=F32)
    run = run_ref[:, 0:1]
    before = csum - member + run
    run_new = run + csum[:, TM - 1:TM]
    run_ref[...] = jnp.broadcast_to(run_new, run_ref.shape)

    col_i = lax.broadcasted_iota(I32, (TM, LIST_COLS), 1)
    rows_oh, col_rows = [], []
    for e in (e1, e2):
        rank = jnp.zeros((1, TM), F32)
        for k in range(N_EXPERTS):
            rank = rank + jnp.where(e == k, before[k:k + 1, :], 0.0)
        rank_i = rank.astype(I32)
        rows_oh.append(jnp.where(sub_i == (rank_i >> LIST_SPAN_LOG2), 1.0, 0.0))
        col_rows.append(e.astype(I32) * LIST_SPAN + (rank_i & (LIST_SPAN - 1)))
    rows_t = jnp.concatenate(rows_oh, axis=1)
    col_cols = jnp.concatenate(col_rows + [jnp.zeros((LANES - 2, TM), I32)], axis=0).T
    cols = jnp.concatenate([jnp.where(col_i == col_cols[:, k:k + 1], 1.0, 0.0).astype(BF16) for k in range(2)],
                           axis=0)
    tok = (lax.broadcasted_iota(I32, (1, 2 * TM), 1) % TM).astype(F32)
    rest = jnp.concatenate([w1_row, w2_row], axis=1)
    values = [None, tok]
    for _ in range(3):
        piece = rest.astype(BF16).astype(F32)
        values.append(piece)
        rest = rest - piece
    planes = jnp.concatenate([rows_t if v is None else rows_t * v for v in values], axis=0).astype(BF16)
    out = jnp.dot(planes, cols, preferred_element_type=F32)
    hits, tok_sum, gp0, gp1, gp2 = [out[k * LANES:(k + 1) * LANES] for k in range(len(values))]
    tile_in_pop = jnp.where(i < MOE_SPLIT, i, i - MOE_SPLIT).astype(F32)
    acc = acc_ref[...] + TOK_ROWS * (tok_sum + TM * tile_in_pop * hits)
    acc_ref[...] = acc
    gacc = gacc_ref[...] + gp0 + gp1 + gp2
    gacc_ref[...] = gacc

    def counts_row():
        return jnp.broadcast_to(run_new, (LANES, LANES)).T[0:1, :].astype(I32)

    @pl.when(i == MOE_SPLIT - 1)
    def _():
        cnt_ref[0:1, :] = counts_row()
        lc_ref[...] = acc.astype(I32)
        gc_ref[...] = gacc

    @pl.when(i == NT - 1)
    def _():
        cnt_ref[1:2, :] = counts_row()
        ll_ref[...] = acc.astype(I32)
        gl_ref[...] = gacc


_TAIL_OUT_SHAPES = [jax.ShapeDtypeStruct((T_ALL, D), F32),
                    jax.ShapeDtypeStruct((T_ALL * TOK_ROWS, LANES), F32),
                    jax.ShapeDtypeStruct((SUBLANES, LANES), I32)] + [
                        jax.ShapeDtypeStruct((LIST_ROWS, LIST_COLS), dt) for dt in (I32, F32, I32, F32)]


def _tail_scratch():
    return [pltpu.VMEM((LANES, LANES), F32), pltpu.VMEM((LIST_ROWS, LIST_COLS), F32),
            pltpu.VMEM((LIST_ROWS, LIST_COLS), F32)]


def _tail_out_specs():
    return [_tile_spec(D),
            pl.BlockSpec((TM * TOK_ROWS, LANES), lambda i: (i, 0)),
            _full_spec((SUBLANES, LANES))] + [_full_spec((LIST_ROWS, LIST_COLS)) for _ in range(4)]


def _halo_specs(halo, width):
    per = TM // halo
    last = T_ALL // halo - 1
    prev = pl.BlockSpec((halo, width), lambda i: (jnp.maximum(i * per - 1, 0), 0))
    nxt = pl.BlockSpec((halo, width), lambda i: (jnp.minimum((i + 1) * per, last), 0))
    return prev, nxt


def _mid_even_kernel(l, a_ref, ap_ref, an_ref, ybc_ref, ybl_ref, xc_ref, xl_ref, mod_ref, wp_ref, ps_ref, wo_ref,
                     gffn_ref, wr_ref, *refs):
    outs, (pad_ref,), scratch = refs[:N_TAIL_OUT], refs[N_TAIL_OUT:-N_TAIL_SCRATCH], refs[-N_TAIL_SCRATCH:]
    i = pl.program_id(0)
    r = _mod_row(i)
    off, n, first, last = _seq_info(i)
    a = a_ref[...]
    pad_ref[0:POOL_HALO, :] = jnp.where(first, 0.0, ap_ref[...])
    pad_ref[POOL_HALO:POOL_HALO + TM, :] = a
    pad_ref[POOL_HALO + TM:, :] = jnp.where(last, 0.0, an_ref[...])
    t = off + lax.broadcasted_iota(I32, (TM, 1), 0)
    mixed = []
    for g, w in enumerate(POOL_WINDOWS):
        lo = w // 2
        hi = w - lo - 1
        cols = slice(g * POOL_GROUP_DIM, (g + 1) * POOL_GROUP_DIM)
        total = pad_ref[POOL_HALO - lo:POOL_HALO - lo + TM, cols]
        for j in range(-lo + 1, hi + 1):
            total = total + pad_ref[POOL_HALO + j:POOL_HALO + j + TM, cols]
        count = (jnp.minimum(t + hi, n - 1) - jnp.maximum(t - lo, 0) + 1).astype(F32)
        pooled = total / count - a[:, cols]
        mixed.append(_dot(pooled, wp_ref[g]))
    ya = jnp.concatenate(mixed, axis=1) * ps_ref[...]
    y_pool = _dot(ya, wo_ref[0:POOL_DIM, :])

    def body(pop):
        y = y_pool + _dot((ybc_ref, ybl_ref)[pop][...], wo_ref[POOL_DIM:, :])
        _mixer_tail(l, (xc_ref, xl_ref)[pop][...], y, mod_ref, r, gffn_ref, wr_ref, outs, scratch)

    _per_population(i, body)


def _mid_even(l, e, a, yb_c, yb_l, xc, xl, mod, w_pool, pool_scale, w_out, gffn, wr):
    prev, nxt = _halo_specs(POOL_HALO, POOL_DIM)
    return pl.pallas_call(
        functools.partial(_mid_even_kernel, l),
        grid=(NT,),
        in_specs=[_tile_spec(POOL_DIM), prev, nxt] + _pop_specs(TM, Q_DIM) + _pop_specs(TM, D) + [
                  _MOD_SPEC, _layer_spec((len(POOL_WINDOWS), POOL_GROUP_DIM, POOL_GROUP_DIM), e),
                  _layer_spec((1, POOL_DIM), e), _layer_spec((D, D), e), _full_spec((DEPTH, D)),
                  _layer_spec((2, D, LANES), l)],
        out_specs=_tail_out_specs(),
        out_shape=_TAIL_OUT_SHAPES,
        scratch_shapes=[pltpu.VMEM((TM + 2 * POOL_HALO, POOL_DIM), F32)] + _tail_scratch(),
        compiler_params=_params(),
        name="mid_even",
    )(a, a, a, yb_c, yb_l, xc, xl, mod, w_pool, pool_scale, w_out, gffn, wr)


def _after_moe(l, r, x1_ref, moe_ref, mod_ref):
    return x1_ref[...] + mod_ref[l, pl.ds(r, 1), 5 * D:6 * D] * _from_token_tiles(moe_ref)


def _in_odd_kernel(l, x1_ref, moec_ref, moel_ref, mod_ref, g_ref, w1_ref, b1_ref, u_ref):
    i = pl.program_id(0)
    r = _mod_row(i)

    def body(pop):
        x2 = _after_moe(l - 1, r, x1_ref, (moec_ref, moel_ref)[pop], mod_ref)
        shift = mod_ref[l, pl.ds(r, 1), 0:D]
        scale = mod_ref[l, pl.ds(r, 1), D:2 * D]
        h = _rms_mod(x2, g_ref[l:l + 1, :], scale, shift)
        u = _dot(h, w1_ref[...]) + b1_ref[...]
        u_ref[...] = u[:, :D] * (1.0 / (1.0 + jnp.exp(-u[:, D:])))

    _per_half(i, body)


def _in_odd(l, o, x1, moe_c, moe_l, mod, g, w1, b1):
    return pl.pallas_call(
        functools.partial(_in_odd_kernel, l),
        grid=(NT,),
        in_specs=[_tile_spec(D)] + _half_specs(TM * TOK_ROWS, LANES) + [
                  _MOD_SPEC, _full_spec((DEPTH, D)), _layer_spec((D, 2 * D), o), _layer_spec((1, 2 * D), o)],
        out_specs=_tile_spec(D),
        out_shape=jax.ShapeDtypeStruct((T_ALL, D), F32),
        compiler_params=_params(),
        name="in_odd",
    )(x1, moe_c, moe_l, mod, g, w1, b1)


CONV_PAD_ROWS = TM + 2 * CONV_HALO
CONV_CHUNKS = D // LANES
CONV_BLOCK = 64


def _mid_odd_kernel(l, u_ref, up_ref, un_ref, x1_ref, moec_ref, moel_ref, mod_ref, dw_ref, dwb_ref, lng_ref, lnb_ref,
                    w2_ref, b2_ref, gffn_ref, wr_ref, *refs):
    outs, scratch = refs[:N_TAIL_OUT], refs[-N_TAIL_SCRATCH:]
    pad_ref, conv_ref, x_ref = refs[N_TAIL_OUT:-N_TAIL_SCRATCH]
    i = pl.program_id(0)

    def residual(pop):
        x_ref[...] = _after_moe(l - 1, _mod_row(i), x1_ref, (moec_ref, moel_ref)[pop], mod_ref)

    _per_half(i, residual)
    r = _mod_row(i)
    _, _, first, last = _seq_info(i)
    u = u_ref[...]
    up = jnp.where(first, 0.0, up_ref[...])
    un = jnp.where(last, 0.0, un_ref[...])
    half = CONV_WIDTH // 2
    first_row = CONV_HALO - half
    for c in range(CONV_CHUNKS):
        cols = slice(c * LANES, (c + 1) * LANES)
        pad_ref[c, 0:CONV_HALO, :] = up[:, cols]
        pad_ref[c, CONV_HALO:CONV_HALO + TM, :] = u[:, cols]
        pad_ref[c, CONV_HALO + TM:, :] = un[:, cols]
        bias = jnp.broadcast_to(dwb_ref[:, cols], (CONV_BLOCK, LANES))
        for r0 in range(0, TM, 2 * CONV_BLOCK):
            acc_even, acc_odd = bias, bias
            for s in range(first_row, first_row + CONV_WIDTH + 1):
                win = pad_ref[c, pl.ds(r0 + s, CONV_BLOCK, stride=2), :]
                j = s - first_row
                if j < CONV_WIDTH:
                    acc_even = acc_even + win * dw_ref[j:j + 1, cols]
                if j >= 1:
                    acc_odd = acc_odd + win * dw_ref[j - 1:j, cols]
            conv_ref[c, pl.ds(r0, CONV_BLOCK, stride=2), :] = acc_even
            conv_ref[c, pl.ds(r0 + 1, CONV_BLOCK, stride=2), :] = acc_odd
    acc = jnp.concatenate([conv_ref[c] for c in range(CONV_CHUNKS)], axis=1)
    mu = jnp.mean(acc, axis=-1, keepdims=True)
    cen = acc - mu
    var = jnp.mean(cen * cen, axis=-1, keepdims=True)
    v = _silu(cen * lax.rsqrt(var + EPS) * lng_ref[...] + lnb_ref[...])
    y = _dot(v, w2_ref[...]) + b2_ref[...]
    _mixer_tail(l, x_ref[...], y, mod_ref, r, gffn_ref, wr_ref, outs, scratch)


def _mid_odd(l, o, u, x1, moe_c, moe_l, mod, dw, dwb, lng, lnb, w2, b2, gffn, wr):
    prev, nxt = _halo_specs(CONV_HALO, D)
    return pl.pallas_call(
        functools.partial(_mid_odd_kernel, l),
        grid=(NT,),
        in_specs=[_tile_spec(D), prev, nxt, _tile_spec(D)] + _half_specs(TM * TOK_ROWS, LANES) + [
                  _MOD_SPEC, _layer_spec((CONV_WIDTH, D), o),
                  _layer_spec((1, D), o), _layer_spec((1, D), o), _layer_spec((1, D), o),
                  _layer_spec((D, D), o), _layer_spec((1, D), o), _full_spec((DEPTH, D)),
                  _layer_spec((2, D, LANES), l)],
        out_specs=_tail_out_specs(),
        out_shape=_TAIL_OUT_SHAPES,
        scratch_shapes=[pltpu.VMEM((CONV_CHUNKS, CONV_PAD_ROWS, LANES), F32),
                        pltpu.VMEM((CONV_CHUNKS, TM, LANES), F32), pltpu.VMEM((TM, D), F32)] + _tail_scratch(),
        compiler_params=_params(),
        name="mid_odd",
    )(u, u, u, x1, moe_c, moe_l, mod, dw, dwb, lng, lnb, w2, b2, gffn, wr)


def _tile_tables_kernel(cnt_ref, te_c, tc_c, tw_c, te_l, tc_l, tw_l, nact_ref, second_ref, first_ref, wplan_ref):
    for p, (te_ref, tc_ref, tw_ref) in enumerate(((te_c, tc_c, tw_c), (te_l, tc_l, tw_l))):
        run = jnp.int32(0)
        ends = []
        for e in range(N_EXPERTS):
            first_ref[p * N_EXPERTS + e] = run
            run = run + (cnt_ref[p, e] + MOE_TM - 1) // MOE_TM
            ends.append(run)
        nact_ref[p] = run

        nxt1, nxt2 = jnp.int32(0), jnp.int32(0)
        plans = []
        for e in reversed(range(N_EXPERTS)):
            plans.append(nxt2)
            used = cnt_ref[p, e] > 0
            nxt2 = jnp.where(used, nxt1, nxt2)
            nxt1 = jnp.where(used, e + 1, nxt1)
        plans.reverse()
        second_ref[p] = nxt2
        order = jnp.int32(0)
        for e in range(N_EXPERTS):
            wplan_ref[p * N_EXPERTS + e] = order + (plans[e] << W_AHEAD_SHIFT)
            order = jnp.where(cnt_ref[p, e] > 0, jnp.where(order == W_BUFFERS - 1, 0, order + 1), order)

        def expert_of(tile, ends=ends):
            e = jnp.int32(0)
            for k in range(N_EXPERTS):
                e = e + jnp.where(tile >= ends[k], 1, 0)
            return e

        last_e = expert_of(run - 1)

        def tile_body(j, c, te_ref=te_ref, tc_ref=tc_ref, tw_ref=tw_ref, p=p, run=run, expert_of=expert_of,
                      last_e=last_e):
            e = jnp.minimum(expert_of(j), last_e)
            te_ref[j] = e
            chunk = jnp.where(j < run, j - first_ref[p * N_EXPERTS + e], 0)
            tc_ref[j] = chunk
            starts_expert = jnp.logical_and(j < run, chunk == 0)
            tw_ref[j] = wplan_ref[p * N_EXPERTS + e] + jnp.where(starts_expert, W_FIRST_UNIT, 0)
            return c

        lax.fori_loop(0, te_ref.shape[0], tile_body, 0)


def _tile_tables(cnt):
    smem = pl.BlockSpec(memory_space=pltpu.SMEM)
    shapes = [(MOE_TILES,)] * 6 + [(2,), (2,)]
    return pl.pallas_call(
        _tile_tables_kernel,
        in_specs=[smem],
        out_specs=[smem] * len(shapes),
        out_shape=[jax.ShapeDtypeStruct(s, I32) for s in shapes],
        scratch_shapes=[pltpu.SMEM((2 * N_EXPERTS,), I32), pltpu.SMEM((2 * N_EXPERTS,), I32)],
        name="tile_tables",
    )(cnt)


def _moe_kernel(l, pop, te_ref, tc_ref, tw_ref, nact_ref, second_ref, list_ref, h_ref, gate_ref,
                w1_hbm, w3_hbm, w2_hbm, o_ref, xs_ref, ys_ref, w1_buf, w3_buf, w2_buf, w_sem):
    j = pl.program_id(0)
    nact = nact_ref[pop]
    last_tile = te_ref.shape[0] - 1
    tile_rows = MOE_TM * TOK_ROWS

    def weight_copies(expert, buf):
        return [pltpu.make_async_copy(hbm.at[l, expert], vmem.at[buf], w_sem.at[buf, k])
                for k, (hbm, vmem) in enumerate(((w1_hbm, w1_buf), (w3_hbm, w3_buf), (w2_hbm, w2_buf)))]

    def list_base(tile):
        return tc_ref[tile] * (MOE_TM // LIST_SPAN) * LIST_COLS + te_ref[tile] * LIST_SPAN

    def tok_rows(base, r):
        first = list_ref[base + (r // LIST_SPAN) * LIST_COLS + r % LIST_SPAN]
        return pl.ds(pl.multiple_of(first, TOK_ROWS), TOK_ROWS)

    def buf_rows(buf, r):
        return pl.ds(pl.multiple_of(buf * tile_rows + r * TOK_ROWS, TOK_ROWS), TOK_ROWS)

    def gather(tile, buf):
        base = list_base(tile)
        for r in range(MOE_TM):
            xs_ref[buf_rows(buf, r), :] = h_ref[tok_rows(base, r), :]

    tile = jnp.minimum(j, last_tile)
    plan = tw_ref[tile]
    wbuf = plan & (W_FIRST_UNIT - 1)
    starts_expert = jnp.logical_and(plan & W_FIRST_UNIT != 0, j <= last_tile)
    ahead_expert = (plan >> W_AHEAD_SHIFT) - 1
    ahead_buf = jnp.where(wbuf == 0, W_BUFFERS - 1, wbuf - 1)

    @pl.when(j == 0)
    def _():
        for cp in weight_copies(te_ref[0], 0):
            cp.start()

        @pl.when(second_ref[pop] > 0)
        def _():
            for cp in weight_copies(second_ref[pop] - 1, 1):
                cp.start()

        o_ref[...] = jnp.zeros(o_ref.shape, F32)
        ys_ref[...] = jnp.zeros(ys_ref.shape, F32)
        gather(0, 0)

    @pl.when(starts_expert)
    def _():
        for cp in weight_copies(te_ref[tile], wbuf):
            cp.wait()

        @pl.when(ahead_expert >= 0)
        def _():
            for cp in weight_copies(ahead_expert, ahead_buf):
                cp.start()

    @pl.when(j <= nact)
    def _():
        cur = j & 1
        w1_ref, w3_ref, w2_ref = w1_buf.at[wbuf], w3_buf.at[wbuf], w2_buf.at[wbuf]
        x = jnp.concatenate([xs_ref[pl.ds(cur * tile_rows + s, MOE_TM, stride=TOK_ROWS), :]
                             for s in range(TOK_ROWS)], axis=1).astype(BF16)
        gather(jnp.minimum(j + 1, last_tile), 1 - cur)

        prev_base = list_base(jnp.maximum(j - 1, 0))
        for r0 in range(0, MOE_TM, MOVE_BATCH):
            dst = [tok_rows(prev_base, r0 + k) for k in range(MOVE_BATCH)]
            vals = [o_ref[dst[k], :] + ys_ref[buf_rows(1 - cur, r0 + k), :] for k in range(MOVE_BATCH)]
            for k in reversed(range(MOVE_BATCH)):
                o_ref[dst[k], :] = vals[k]

        hg = jnp.dot(x, w1_ref[...].astype(BF16), preferred_element_type=F32)
        hu = jnp.dot(x, w3_ref[...].astype(BF16), preferred_element_type=F32)
        y = _dot(_silu(hg) * hu, w2_ref[...])

        first_row = pl.multiple_of(tc_ref[tile] * (MOE_TM // LIST_SPAN), MOE_TM // LIST_SPAN)
        spread = jnp.concatenate(
            [jnp.broadcast_to(gate_ref[pl.ds(first_row + k, 1), :], (LIST_SPAN, LIST_COLS))
             for k in range(MOE_TM // LIST_SPAN)], axis=0)
        want = te_ref[tile] * LIST_SPAN + (lax.broadcasted_iota(I32, (MOE_TM, 1), 0) & (LIST_SPAN - 1))
        hit = jnp.logical_and(lax.broadcasted_iota(I32, (MOE_TM, LIST_COLS), 1) == want, j < nact)
        y = y * jnp.sum(jnp.where(hit, spread, 0.0), axis=1, keepdims=True)
        for s in range(TOK_ROWS):
            ys_ref[pl.ds(cur * tile_rows + s, MOE_TM, stride=TOK_ROWS), :] = y[:, s * LANES:(s + 1) * LANES]


def _moe(l, pop, h_tt, te, tc, tw, nact, second, rows, gate, w1, w3, w2):
    tp = T_HALF
    blk = pop
    n_tiles = te.shape[0]
    hbm = pl.BlockSpec(memory_space=pl.ANY)
    grid_spec = pltpu.PrefetchScalarGridSpec(
        num_scalar_prefetch=6,
        grid=(n_tiles + 1,),
        in_specs=[
            pl.BlockSpec((tp * TOK_ROWS, LANES), lambda j, *_: (blk, 0), pipeline_mode=pl.Buffered(1)),
            pl.BlockSpec((LIST_ROWS, LIST_COLS), lambda j, *_: (0, 0), pipeline_mode=pl.Buffered(1)),
            hbm, hbm, hbm,
        ],
        out_specs=pl.BlockSpec((tp * TOK_ROWS, LANES), lambda j, *_: (0, 0), pipeline_mode=pl.Buffered(1)),
        scratch_shapes=[pltpu.VMEM((2 * MOE_TM * TOK_ROWS, LANES), F32),
                        pltpu.VMEM((2 * MOE_TM * TOK_ROWS, LANES), F32),
                        pltpu.VMEM((W_BUFFERS, D, EXPERT_HIDDEN), F32),
                        pltpu.VMEM((W_BUFFERS, D, EXPERT_HIDDEN), F32),
                        pltpu.VMEM((W_BUFFERS, EXPERT_HIDDEN, D), F32),
                        pltpu.SemaphoreType.DMA((W_BUFFERS, 3))],
    )
    return pl.pallas_call(
        functools.partial(_moe_kernel, l, pop),
        grid_spec=grid_spec,
        out_shape=jax.ShapeDtypeStruct((tp * TOK_ROWS, LANES), F32),
        compiler_params=_params(),
        name="moe",
    )(te, tc, tw, nact, second, rows, h_tt, gate, w1, w3, w2)


def _moe_all(l, h_tt, cnt, rows_c, gate_c, rows_l, gate_l, w1, w3, w2):
    te_c, tc_c, tw_c, te_l, tc_l, tw_l, nact, second = _tile_tables(cnt)
    out_c = _moe(l, 0, h_tt, te_c, tc_c, tw_c, nact, second, rows_c.reshape(-1), gate_c, w1, w3, w2)
    out_l = _moe(l, 1, h_tt, te_l, tc_l, tw_l, nact, second, rows_l.reshape(-1), gate_l, w1, w3, w2)
    return out_c, out_l


def _final_kernel(l, x1_ref, moec_ref, moel_ref, mod_ref, g_ref, oc_ref, ol_ref):
    i = pl.program_id(0)
    r = _mod_row(i)

    def body(half, pop):
        x = _after_moe(l, r, x1_ref, (moec_ref, moel_ref)[half], mod_ref)
        ms = jnp.mean(x * x, axis=-1, keepdims=True)
        (oc_ref, ol_ref)[pop][...] = x * lax.rsqrt(ms + EPS) * g_ref[...]

    pl.when(i < MOE_SPLIT)(functools.partial(body, 0, 0))
    pl.when(jnp.logical_and(i >= MOE_SPLIT, i < NT_CTX))(functools.partial(body, 1, 0))
    pl.when(i >= NT_CTX)(functools.partial(body, 1, 1))


def _final(l, x1, moe_c, moe_l, mod, g):
    return pl.pallas_call(
        functools.partial(_final_kernel, l),
        grid=(NT,),
        in_specs=[_tile_spec(D)] + _half_specs(TM * TOK_ROWS, LANES) + [_MOD_SPEC, _full_spec((1, D))],
        out_specs=_pop_specs(TM, D),
        out_shape=[jax.ShapeDtypeStruct((T_CTX, D), F32), jax.ShapeDtypeStruct((T_LAT, D), F32)],
        compiler_params=_params(),
        name="final_norm",
    )(x1, moe_c, moe_l, mod, g)


def _rope_tables():
    t = jnp.arange(LAT_LEN)
    row = (t // GRID_W).astype(F32)
    col = (t % GRID_W).astype(F32)
    inv = ROPE_BASE ** (-jnp.arange(0, ROPE_AXIS_DIM, 2, dtype=F32) / ROPE_AXIS_DIM)

    def table(p):
        ang = p[:, None] * inv[None, :]
        ang = jnp.concatenate([ang, ang], axis=-1)
        return jnp.cos(ang), jnp.sin(ang)

    (cr, sr), (cc, sc) = table(row), table(col)
    cos = jnp.concatenate([cr, cc, cr, cc], axis=-1)
    sin = jnp.concatenate([sr, sc, sr, sc], axis=-1)
    return cos, sin


def _router_weights(w_grp, w_exp):
    we = jnp.transpose(w_exp, (0, 2, 1, 3)).reshape(DEPTH, D, N_GROUPS * EPG)
    pad = jnp.zeros((DEPTH, D, LANES - N_GROUPS - N_GROUPS * EPG), F32)
    wr = jnp.concatenate([w_grp, we, pad], axis=-1)
    hi = lax.bitcast_convert_type(lax.bitcast_convert_type(wr, jnp.uint32) & jnp.uint32(0xFFFF0000), F32)
    return jnp.stack([hi.astype(BF16), (wr - hi).astype(BF16)], axis=1)


def kernel(x_prompt, x_sample, cache_k, cache_v, c, c_ctx, w_ada, b_ada, norm_mix_g, norm_ffn_g, w_in_ab, pool_w, pool_scale, attn_sink, w_out_ab, conv_w1, conv_b1, conv_dw, conv_dw_b, conv_ln_g, conv_ln_b, conv_w2, conv_b2, router_grp, router_exp, moe_w1, moe_w3, moe_w2, final_g):
    xc = x_prompt.reshape(T_CTX, D)
    xl = x_sample.reshape(T_LAT, D)
    cond8 = jnp.concatenate([c_ctx[None, :], c, jnp.zeros((SUBLANES - 1 - N_LAT_SEQ, D), F32)], axis=0)
    mod = _modulation(cond8, w_ada, b_ada)
    cos_t, sin_t = _rope_tables()
    wr = _router_weights(router_grp, router_exp)
    n_even, n_odd = w_in_ab.shape[0], conv_w1.shape[0]
    past = cache_k.shape[2]

    a, q, k, v = _in_even(0, 0, xc, xl, mod, norm_mix_g, w_in_ab, cos_t, sin_t)
    yb_c = _attn_ctx(0, attn_sink, q, k, v)
    yb_l = _attn_lat(0, attn_sink, q, k, v, cache_k[:, 0].reshape(N_LAT_SEQ, past, KV_DIM),
                     cache_v[:, 0].reshape(N_LAT_SEQ, past, KV_DIM))
    x1, h2, *routing = _mid_even(0, 0, a, yb_c, yb_l, xc, xl, mod, pool_w,
                                 pool_scale.reshape(n_even, 1, POOL_DIM), w_out_ab, norm_ffn_g, wr)
    moe0 = _moe_all(0, h2, *routing, moe_w1, moe_w3, moe_w2)

    vec = lambda p: p.reshape(n_odd, 1, -1)
    u = _in_odd(1, 0, x1, *moe0, mod, norm_mix_g, conv_w1, vec(conv_b1))
    x3, h2b, *routing = _mid_odd(1, 0, u, x1, *moe0, mod, conv_dw, vec(conv_dw_b), vec(conv_ln_g),
                                 vec(conv_ln_b), conv_w2, vec(conv_b2), norm_ffn_g, wr)
    moe1 = _moe_all(1, h2b, *routing, moe_w1, moe_w3, moe_w2)
    y_c, y_l = _final(1, x3, *moe1, mod, final_g.reshape(1, D))

    y_prompt = y_c.reshape(N_CTX_SEQ, CTX_LEN, D)
    y_sample = y_l.reshape(N_LAT_SEQ, LAT_LEN, D)
    state_k = k[:T_CTX].reshape(N_CTX_SEQ, 1, CTX_LEN, N_KV_HEADS, HEAD_DIM)
    state_v = v[:T_CTX].reshape(N_CTX_SEQ, 1, CTX_LEN, N_KV_HEADS, HEAD_DIM)
    return (y_prompt, y_sample, state_k, state_v)
```

```python
import functools

import jax
import jax.numpy as jnp
from jax import lax
from jax.experimental import pallas as pl
from jax.experimental.pallas import tpu as pltpu

F32 = jnp.float32
BF16 = jnp.bfloat16
I32 = jnp.int32

D = 1024
N_CTX_SEQ = 16
CTX_LEN = 256
N_LAT_SEQ = 2
LAT_LEN = 1024
T_CTX = N_CTX_SEQ * CTX_LEN
T_LAT = N_LAT_SEQ * LAT_LEN
T_ALL = T_CTX + T_LAT
TM = 256
NT = T_ALL // TM
NT_CTX = T_CTX // TM
LAT_TILES_PER_SEQ = LAT_LEN // TM
GRID_W = 64
DEPTH = 2

POOL_WINDOWS = (2, 4, 8, 16)
POOL_GROUP_DIM = 128
POOL_DIM = 512
HEAD_DIM = 64
N_Q_HEADS = 8
N_KV_HEADS = 2
Q_PER_KV = 4
Q_DIM = 512
KV_DIM = 128
IN_AB = 1280
ATTN_WINDOW = 128
ATTN_BLOCK = 128
ATTN_SCALE = HEAD_DIM ** -0.5
ROPE_BASE = 10000.0
ROPE_AXIS_DIM = 32
CONV_WIDTH = 31
CONV_HALO = 16
POOL_HALO = 8
N_GROUPS = 4
EPG = 4
N_EXPERTS = 16
EXPERT_HIDDEN = 512
EPS = 1e-6
NEG_BIG = -1e30

SUBLANES = 8
LANES = 128
TOK_ROWS = D // LANES

MOE_TM = 256
MOE_SPLIT = NT // 2
T_HALF = MOE_SPLIT * TM
MOE_TILES = 2 * T_HALF // MOE_TM + N_EXPERTS
W_BUFFERS = 3
W_FIRST_UNIT = 4
W_AHEAD_SHIFT = 3
MOVE_BATCH_LOG2 = 2
MOVE_BATCH = 1 << MOVE_BATCH_LOG2
VMEM_LIMIT = 56 * 1024 * 1024


def _silu(x):
    return x * (1.0 / (1.0 + jnp.exp(-x)))


def _mod_row(i):
    return jnp.where(i < NT_CTX, 0, 1 + (i - NT_CTX) // LAT_TILES_PER_SEQ)


def _seq_info(i):
    is_ctx = i < NT_CTX
    k = (i - NT_CTX) % LAT_TILES_PER_SEQ
    off = jnp.where(is_ctx, 0, k * TM)
    n = jnp.where(is_ctx, CTX_LEN, LAT_LEN)
    first = jnp.logical_or(is_ctx, k == 0)
    last = jnp.logical_or(is_ctx, k == LAT_TILES_PER_SEQ - 1)
    return off, n, first, last


def _rms_mod(x, g, scale, shift):
    ms = jnp.mean(x * x, axis=-1, keepdims=True)
    return (x * lax.rsqrt(ms + EPS) * g) * (1.0 + scale) + shift


def _dot(a, b):
    return jnp.dot(a.astype(BF16), b.astype(BF16), preferred_element_type=F32)


def _dot_nt(a, b):
    return lax.dot_general(a.astype(BF16), b.astype(BF16), (((1,), (1,)), ((), ())),
                           preferred_element_type=F32)


def _from_token_tiles(ref):
    return jnp.concatenate([ref[pl.ds(s, TM, stride=TOK_ROWS), :] for s in range(TOK_ROWS)], axis=1)


def _tile_spec(width):
    return pl.BlockSpec((TM, width), lambda i: (i, 0))


def _full_spec(shape):
    return pl.BlockSpec(shape, lambda i: (0,) * len(shape))


def _layer_spec(shape, l):
    return pl.BlockSpec((None,) + tuple(shape), lambda i: (l,) + (0,) * len(shape))


def _pop_specs(rows, width):
    return [pl.BlockSpec((rows, width), lambda i: (jnp.minimum(i, NT_CTX - 1), 0)),
            pl.BlockSpec((rows, width), lambda i: (jnp.maximum(i - NT_CTX, 0), 0))]


def _per_population(i, body):
    pl.when(i < NT_CTX)(functools.partial(body, 0))
    pl.when(i >= NT_CTX)(functools.partial(body, 1))


def _half_specs(rows, width):
    return [pl.BlockSpec((rows, width), lambda i: (jnp.minimum(i, MOE_SPLIT - 1), 0)),
            pl.BlockSpec((rows, width), lambda i: (jnp.maximum(i - MOE_SPLIT, 0), 0))]


def _per_half(i, body):
    pl.when(i < MOE_SPLIT)(functools.partial(body, 0))
    pl.when(i >= MOE_SPLIT)(functools.partial(body, 1))


_MOD_SPEC = pl.BlockSpec((DEPTH, SUBLANES, 6 * D), lambda i: (0, 0, 0))


def _params():
    return pltpu.CompilerParams(vmem_limit_bytes=VMEM_LIMIT)


MOD_TN = 1536


def _mod_kernel(cond_ref, w_ref, b_ref, o_ref):
    s = _silu(cond_ref[...])
    o_ref[0] = _dot(s, w_ref[0]) + b_ref[0]


def _modulation(cond8, w_ada, b_ada):
    return pl.pallas_call(
        _mod_kernel,
        grid=(DEPTH, 6 * D // MOD_TN),
        in_specs=[
            pl.BlockSpec((SUBLANES, D), lambda l, n: (0, 0)),
            pl.BlockSpec((1, D, MOD_TN), lambda l, n: (l, 0, n)),
            pl.BlockSpec((1, 1, MOD_TN), lambda l, n: (l, 0, n)),
        ],
        out_specs=pl.BlockSpec((1, SUBLANES, MOD_TN), lambda l, n: (l, 0, n)),
        out_shape=jax.ShapeDtypeStruct((DEPTH, SUBLANES, 6 * D), F32),
        compiler_params=_params(),
        name="modulation",
    )(cond8, w_ada, b_ada.reshape(DEPTH, 1, 6 * D))


def _rope_chunk(xc, cos, sin):
    lane = lax.broadcasted_iota(I32, xc.shape, 1)
    first = (lane % ROPE_AXIS_DIM) < (ROPE_AXIS_DIM // 2)
    rot = jnp.where(first, -pltpu.roll(xc, LANES - ROPE_AXIS_DIM // 2, 1), pltpu.roll(xc, ROPE_AXIS_DIM // 2, 1))
    return xc * cos + rot * sin


def _in_even_kernel(l, xc_ref, xl_ref, mod_ref, g_ref, w_ref, cos_ref, sin_ref, a_ref, q_ref, k_ref, v_ref):
    i = pl.program_id(0)
    r = _mod_row(i)

    def body(pop):
        x = (xc_ref, xl_ref)[pop][...]
        shift = mod_ref[l, pl.ds(r, 1), 0:D]
        scale = mod_ref[l, pl.ds(r, 1), D:2 * D]
        h = _rms_mod(x, g_ref[l:l + 1, :], scale, shift)
        proj = _dot(h, w_ref[...])
        a_ref[...] = proj[:, :POOL_DIM]
        v_ref[...] = proj[:, POOL_DIM + Q_DIM + KV_DIM:]
        if pop == 0:
            q_ref[...] = proj[:, POOL_DIM:POOL_DIM + Q_DIM].astype(BF16)
            k_ref[...] = proj[:, POOL_DIM + Q_DIM:POOL_DIM + Q_DIM + KV_DIM]
        else:
            off, _, _, _ = _seq_info(i)
            off = pl.multiple_of(off, TM)
            cos = cos_ref[pl.ds(off, TM), :]
            sin = sin_ref[pl.ds(off, TM), :]
            for c in range(Q_DIM // LANES):
                lo = POOL_DIM + c * LANES
                q_ref[:, c * LANES:(c + 1) * LANES] = _rope_chunk(proj[:, lo:lo + LANES], cos, sin).astype(BF16)
            lo = POOL_DIM + Q_DIM
            k_ref[...] = _rope_chunk(proj[:, lo:lo + LANES], cos, sin)

    _per_population(i, body)


def _in_even(l, e, xc, xl, mod, g, w_in, cos_t, sin_t):
    return pl.pallas_call(
        functools.partial(_in_even_kernel, l),
        grid=(NT,),
        in_specs=_pop_specs(TM, D) + [_MOD_SPEC, _full_spec((DEPTH, D)), _layer_spec((D, IN_AB), e),
                  _full_spec((LAT_LEN, LANES)), _full_spec((LAT_LEN, LANES))],
        out_specs=[_tile_spec(POOL_DIM), _tile_spec(Q_DIM), _tile_spec(KV_DIM), _tile_spec(KV_DIM)],
        out_shape=[jax.ShapeDtypeStruct((T_ALL, POOL_DIM), F32), jax.ShapeDtypeStruct((T_ALL, Q_DIM), BF16),
                   jax.ShapeDtypeStruct((T_ALL, KV_DIM), F32), jax.ShapeDtypeStruct((T_ALL, KV_DIM), F32)],
        compiler_params=_params(),
        name="in_even",
    )(xc, xl, mod, g, w_in, cos_t, sin_t)


def _attn_ctx_kernel(e, sink_ref, q_ref, k_ref, v_ref, o_ref):
    q = q_ref[...]
    k = k_ref[...]
    v = v_ref[...]
    for h in range(N_KV_HEADS):
        kh = k[:, h * HEAD_DIM:(h + 1) * HEAD_DIM]
        vh = v[:, h * HEAD_DIM:(h + 1) * HEAD_DIM]
        for g in range(Q_PER_KV):
            j = h * Q_PER_KV + g
            s = _dot_nt(q[:, j * HEAD_DIM:(j + 1) * HEAD_DIM], kh) * ATTN_SCALE
            sk = sink_ref[e, j]
            m = jnp.maximum(jnp.max(s, axis=-1, keepdims=True), sk)
            p = jnp.exp(s - m)
            denom = jnp.sum(p, axis=-1, keepdims=True) + jnp.exp(sk - m)
            o_ref[:, j * HEAD_DIM:(j + 1) * HEAD_DIM] = (_dot(p, vh) / denom).astype(o_ref.dtype)


def _attn_ctx(e, sink, q, k, v):
    tile = lambda w: pl.BlockSpec((TM, w), lambda b: (b, 0))
    return pl.pallas_call(
        functools.partial(_attn_ctx_kernel, e),
        grid=(N_CTX_SEQ,),
        in_specs=[pl.BlockSpec(memory_space=pltpu.SMEM), tile(Q_DIM), tile(KV_DIM), tile(KV_DIM)],
        out_specs=tile(Q_DIM),
        out_shape=jax.ShapeDtypeStruct((T_CTX, Q_DIM), BF16),
        compiler_params=_params(),
        name="attn_ctx",
    )(sink, q, k, v)


LAT_BLOCKS = LAT_LEN // ATTN_BLOCK
BAND = 3 * ATTN_BLOCK


def _attn_lat_kernel(e, sink_ref, q_ref, k_ref, v_ref, ck_ref, cv_ref, o_ref):
    n = pl.program_id(1)
    start = jnp.clip((n - 1) * ATTN_BLOCK, 0, LAT_LEN - BAND)
    start = pl.multiple_of(start, ATTN_BLOCK)
    q = q_ref[...]
    kw = k_ref[pl.ds(start, BAND), :]
    vw = v_ref[pl.ds(start, BAND), :]
    ck = ck_ref[0]
    cv = cv_ref[0]
    qpos = n * ATTN_BLOCK + lax.broadcasted_iota(I32, (ATTN_BLOCK, BAND), 0)
    kpos = start + lax.broadcasted_iota(I32, (ATTN_BLOCK, BAND), 1)
    valid = jnp.abs(qpos - kpos) <= ATTN_WINDOW
    for h in range(N_KV_HEADS):
        hs = slice(h * HEAD_DIM, (h + 1) * HEAD_DIM)
        for g in range(Q_PER_KV):
            j = h * Q_PER_KV + g
            qj = q[:, j * HEAD_DIM:(j + 1) * HEAD_DIM]
            s_loc = jnp.where(valid, _dot_nt(qj, kw[:, hs]) * ATTN_SCALE, NEG_BIG)
            s_ctx = _dot_nt(qj, ck[:, hs]) * ATTN_SCALE
            sk = sink_ref[e, j]
            m = jnp.maximum(jnp.maximum(jnp.max(s_loc, axis=-1, keepdims=True),
                                        jnp.max(s_ctx, axis=-1, keepdims=True)), sk)
            p_loc = jnp.exp(s_loc - m)
            p_ctx = jnp.exp(s_ctx - m)
            denom = (jnp.sum(p_loc, axis=-1, keepdims=True) + jnp.sum(p_ctx, axis=-1, keepdims=True)
                     + jnp.exp(sk - m))
            o = _dot(p_loc, vw[:, hs]) + _dot(p_ctx, cv[:, hs])
            o_ref[:, j * HEAD_DIM:(j + 1) * HEAD_DIM] = (o / denom).astype(o_ref.dtype)


def _attn_lat(e, sink, q, k, v, ck, cv):
    past = ck.shape[1]
    q_blk0 = T_CTX // ATTN_BLOCK
    kv_blk0 = T_CTX // LAT_LEN
    q_spec = pl.BlockSpec((ATTN_BLOCK, Q_DIM), lambda b, n: (q_blk0 + b * LAT_BLOCKS + n, 0))
    return pl.pallas_call(
        functools.partial(_attn_lat_kernel, e),
        grid=(N_LAT_SEQ, LAT_BLOCKS),
        in_specs=[pl.BlockSpec(memory_space=pltpu.SMEM),
                  q_spec,
                  pl.BlockSpec((LAT_LEN, KV_DIM), lambda b, n: (kv_blk0 + b, 0)),
                  pl.BlockSpec((LAT_LEN, KV_DIM), lambda b, n: (kv_blk0 + b, 0)),
                  pl.BlockSpec((1, past, KV_DIM), lambda b, n: (b, 0, 0)),
                  pl.BlockSpec((1, past, KV_DIM), lambda b, n: (b, 0, 0))],
        out_specs=pl.BlockSpec((ATTN_BLOCK, Q_DIM), lambda b, n: (b * LAT_BLOCKS + n, 0)),
        out_shape=jax.ShapeDtypeStruct((T_LAT, Q_DIM), BF16),
        compiler_params=_params(),
        name="attn_lat",
    )(sink, q, k, v, ck, cv)


def _router(h2, wr_ref):
    hi = h2.astype(BF16)
    lo = (h2 - hi.astype(F32)).astype(BF16)
    whi = wr_ref[0]
    wlo = wr_ref[1]
    dot = functools.partial(jnp.dot, preferred_element_type=F32)
    logits_t = (dot(hi, whi) + (dot(lo, whi) + dot(hi, wlo))).T
    row = lambda k: logits_t[k:k + 1, :]

    def first_max(vals):
        best = functools.reduce(jnp.maximum, vals)
        idx = jnp.full(best.shape, len(vals) - 1, F32)
        for k in reversed(range(len(vals) - 1)):
            idx = jnp.where(vals[k] == best, float(k), idx)
        return best, idx

    grp = [row(k) for k in range(N_GROUPS)]
    gmax, g_idx = first_max(grp)
    g_w = 1.0 / sum(jnp.exp(gk - gmax) for gk in grp)
    el = []
    for e in range(EPG):
        v = row(N_GROUPS + (N_GROUPS - 1) * EPG + e)
        for k in reversed(range(N_GROUPS - 1)):
            v = jnp.where(g_idx == k, row(N_GROUPS + k * EPG + e), v)
        el.append(v)
    t1, i1 = first_max(el)
    t2, i2 = first_max([jnp.where(i1 == e, -jnp.inf, el[e]) for e in range(EPG)])
    d = jnp.exp(t2 - t1)
    w1 = g_w / (1.0 + d)
    w2 = g_w * d / (1.0 + d)
    return EPG * g_idx + i1, EPG * g_idx + i2, w1, w2


LIST_SPAN_LOG2 = 5
LIST_SPAN = 1 << LIST_SPAN_LOG2
LIST_ROWS = LANES
LIST_COLS = N_EXPERTS * LIST_SPAN


N_TAIL_OUT = 7
N_TAIL_SCRATCH = 3


def _mixer_tail(l, x, y, mod_ref, r, gffn_ref, wr_ref, outs, scratch):
    x1_ref, h2_ref, cnt_ref, lc_ref, gc_ref, ll_ref, gl_ref = outs
    run_ref, acc_ref, gacc_ref = scratch
    i = pl.program_id(0)
    g1 = mod_ref[l, pl.ds(r, 1), 2 * D:3 * D]
    shift2 = mod_ref[l, pl.ds(r, 1), 3 * D:4 * D]
    scale2 = mod_ref[l, pl.ds(r, 1), 4 * D:5 * D]
    x1 = x + g1 * y
    x1_ref[...] = x1
    h2 = _rms_mod(x1, gffn_ref[l:l + 1, :], scale2, shift2)
    for s in range(TOK_ROWS):
        h2_ref[pl.ds(s, TM, stride=TOK_ROWS), :] = h2[:, s * LANES:(s + 1) * LANES]
    e1, e2, w1_row, w2_row = _router(h2, wr_ref)

    @pl.when(jnp.logical_or(i == 0, i == MOE_SPLIT))
    def _():
        run_ref[...] = jnp.zeros(run_ref.shape, F32)
        acc_ref[...] = jnp.zeros(acc_ref.shape, F32)
        gacc_ref[...] = jnp.zeros(gacc_ref.shape, F32)

    @pl.when(i == 0)
    def _():
        cnt_ref[...] = jnp.zeros(cnt_ref.shape, I32)

    sub_i = lax.broadcasted_iota(I32, (LANES, TM), 0)
    sub = sub_i.astype(F32)
    member = jnp.where(jnp.logical_or(sub == e1, sub == e2), 1.0, 0.0)
    tri = jnp.where(lax.broadcasted_iota(I32, (TM, TM), 0) <= lax.broadcasted_iota(I32, (TM, TM), 1), 1.0, 0.0)
    csum = jnp.dot(member.astype(BF16), tri.astype(BF16), preferred_element_type=F32)
    run = run_ref[:, 0:1]
    before = csum - member + run
    run_new = run + csum[:, TM - 1:TM]
    run_ref[...] = jnp.broadcast_to(run_new, run_ref.shape)

    col_i = lax.broadcasted_iota(I32, (TM, LIST_COLS), 1)
    rows_oh, col_rows = [], []
    for e in (e1, e2):
        rank = jnp.zeros((1, TM), F32)
        for k in range(N_EXPERTS):
            rank = rank + jnp.where(e == k, before[k:k + 1, :], 0.0)
        rank_i = rank.astype(I32)
        rows_oh.append(jnp.where(sub_i == (rank_i >> LIST_SPAN_LOG2), 1.0, 0.0))
        col_rows.append(e.astype(I32) * LIST_SPAN + (rank_i & (LIST_SPAN - 1)))
    rows_t = jnp.concatenate(rows_oh, axis=1)
    col_cols = jnp.concatenate(col_rows + [jnp.zeros((LANES - 2, TM), I32)], axis=0).T
    cols = jnp.concatenate([jnp.where(col_i == col_cols[:, k:k + 1], 1.0, 0.0).astype(BF16) for k in range(2)],
                           axis=0)
    tok = (lax.broadcasted_iota(I32, (1, 2 * TM), 1) % TM).astype(F32)
    rest = jnp.concatenate([w1_row, w2_row], axis=1)
    values = [None, tok]
    for _ in range(3):
        piece = rest.astype(BF16).astype(F32)
        values.append(piece)
        rest = rest - piece
    planes = jnp.concatenate([rows_t if v is None else rows_t * v for v in values], axis=0).astype(BF16)
    out = jnp.dot(planes, cols, preferred_element_type=F32)
    hits, tok_sum, gp0, gp1, gp2 = [out[k * LANES:(k + 1) * LANES] for k in range(len(values))]
    tile_in_pop = jnp.where(i < MOE_SPLIT, i, i - MOE_SPLIT).astype(F32)
    acc = acc_ref[...] + TOK_ROWS * (tok_sum + TM * tile_in_pop * hits)
    acc_ref[...] = acc
    gacc = gacc_ref[...] + gp0 + gp1 + gp2
    gacc_ref[...] = gacc

    def counts_row():
        return jnp.broadcast_to(run_new, (LANES, LANES)).T[0:1, :].astype(I32)

    @pl.when(i == MOE_SPLIT - 1)
    def _():
        cnt_ref[0:1, :] = counts_row()
        lc_ref[...] = acc.astype(I32)
        gc_ref[...] = gacc

    @pl.when(i == NT - 1)
    def _():
        cnt_ref[1:2, :] = counts_row()
        ll_ref[...] = acc.astype(I32)
        gl_ref[...] = gacc


_TAIL_OUT_SHAPES = [jax.ShapeDtypeStruct((T_ALL, D), F32),
                    jax.ShapeDtypeStruct((T_ALL * TOK_ROWS, LANES), F32),
                    jax.ShapeDtypeStruct((SUBLANES, LANES), I32)] + [
                        jax.ShapeDtypeStruct((LIST_ROWS, LIST_COLS), dt) for dt in (I32, F32, I32, F32)]


def _tail_scratch():
    return [pltpu.VMEM((LANES, LANES), F32), pltpu.VMEM((LIST_ROWS, LIST_COLS), F32),
            pltpu.VMEM((LIST_ROWS, LIST_COLS), F32)]


def _tail_out_specs():
    return [_tile_spec(D),
            pl.BlockSpec((TM * TOK_ROWS, LANES), lambda i: (i, 0)),
            _full_spec((SUBLANES, LANES))] + [_full_spec((LIST_ROWS, LIST_COLS)) for _ in range(4)]


def _halo_specs(halo, width):
    per = TM // halo
    last = T_ALL // halo - 1
    prev = pl.BlockSpec((halo, width), lambda i: (jnp.maximum(i * per - 1, 0), 0))
    nxt = pl.BlockSpec((halo, width), lambda i: (jnp.minimum((i + 1) * per, last), 0))
    return prev, nxt


def _mid_even_kernel(l, a_ref, ap_ref, an_ref, ybc_ref, ybl_ref, xc_ref, xl_ref, mod_ref, wp_ref, ps_ref, wo_ref,
                     gffn_ref, wr_ref, *refs):
    outs, (pad_ref, pool_ref), scratch = refs[:N_TAIL_OUT], refs[N_TAIL_OUT:-N_TAIL_SCRATCH], refs[-N_TAIL_SCRATCH:]
    i = pl.program_id(0)
    r = _mod_row(i)
    off, n, first, last = _seq_info(i)
    a = a_ref[...]
    ap = jnp.where(first, 0.0, ap_ref[...])
    an = jnp.where(last, 0.0, an_ref[...])
    half_rows = TM // 2
    pos = [(off + 2 * lax.broadcasted_iota(I32, (half_rows, 1), 0) + parity).astype(F32) for parity in (0, 1)]
    last_pos = (n - 1).astype(F32)
    mixed = []
    for g, w in enumerate(POOL_WINDOWS):
        lo = w // 2
        hi = w - lo - 1
        cols = slice(g * POOL_GROUP_DIM, (g + 1) * POOL_GROUP_DIM)
        pad_ref[g, 0:POOL_HALO, :] = ap[:, cols]
        pad_ref[g, POOL_HALO:POOL_HALO + TM, :] = a[:, cols]
        pad_ref[g, POOL_HALO + TM:, :] = an[:, cols]
        loads = {s: pad_ref[g, pl.ds(s, half_rows, stride=2), :]
                 for s in range(POOL_HALO - lo, POOL_HALO + hi + 2)}
        for parity in (0, 1):
            total = functools.reduce(lambda x, y: x + y,
                                     [loads[POOL_HALO + parity + j] for j in range(-lo, hi + 1)])
            count = jnp.minimum(pos[parity] + hi, last_pos) - jnp.maximum(pos[parity] - lo, 0.0) + 1.0
            pool_ref[g, pl.ds(parity, half_rows, stride=2), :] = total / count - loads[POOL_HALO + parity]
        mixed.append(_dot(pool_ref[g], wp_ref[g]))
    ya = jnp.concatenate(mixed, axis=1) * ps_ref[...]
    y_pool = _dot(ya, wo_ref[0:POOL_DIM, :])

    def body(pop):
        y = y_pool + _dot((ybc_ref, ybl_ref)[pop][...], wo_ref[POOL_DIM:, :])
        _mixer_tail(l, (xc_ref, xl_ref)[pop][...], y, mod_ref, r, gffn_ref, wr_ref, outs, scratch)

    _per_population(i, body)


def _mid_even(l, e, a, yb_c, yb_l, xc, xl, mod, w_pool, pool_scale, w_out, gffn, wr):
    prev, nxt = _halo_specs(POOL_HALO, POOL_DIM)
    return pl.pallas_call(
        functools.partial(_mid_even_kernel, l),
        grid=(NT,),
        in_specs=[_tile_spec(POOL_DIM), prev, nxt] + _pop_specs(TM, Q_DIM) + _pop_specs(TM, D) + [
                  _MOD_SPEC, _layer_spec((len(POOL_WINDOWS), POOL_GROUP_DIM, POOL_GROUP_DIM), e),
                  _layer_spec((1, POOL_DIM), e), _layer_spec((D, D), e), _full_spec((DEPTH, D)),
                  _layer_spec((2, D, LANES), l)],
        out_specs=_tail_out_specs(),
        out_shape=_TAIL_OUT_SHAPES,
        scratch_shapes=[pltpu.VMEM((len(POOL_WINDOWS), TM + 2 * POOL_HALO, POOL_GROUP_DIM), F32),
                        pltpu.VMEM((len(POOL_WINDOWS), TM, POOL_GROUP_DIM), F32)] + _tail_scratch(),
        compiler_params=_params(),
        name="mid_even",
    )(a, a, a, yb_c, yb_l, xc, xl, mod, w_pool, pool_scale, w_out, gffn, wr)


def _after_moe(l, r, x1_ref, moe_ref, mod_ref):
    return x1_ref[...] + mod_ref[l, pl.ds(r, 1), 5 * D:6 * D] * _from_token_tiles(moe_ref)


def _in_odd_kernel(l, x1_ref, moec_ref, moel_ref, mod_ref, g_ref, w1_ref, b1_ref, u_ref):
    i = pl.program_id(0)
    r = _mod_row(i)

    def body(pop):
        x2 = _after_moe(l - 1, r, x1_ref, (moec_ref, moel_ref)[pop], mod_ref)
        shift = mod_ref[l, pl.ds(r, 1), 0:D]
        scale = mod_ref[l, pl.ds(r, 1), D:2 * D]
        h = _rms_mod(x2, g_ref[l:l + 1, :], scale, shift)
        u = _dot(h, w1_ref[...]) + b1_ref[...]
        u_ref[...] = u[:, :D] * (1.0 / (1.0 + jnp.exp(-u[:, D:])))

    _per_half(i, body)


def _in_odd(l, o, x1, moe_c, moe_l, mod, g, w1, b1):
    return pl.pallas_call(
        functools.partial(_in_odd_kernel, l),
        grid=(NT,),
        in_specs=[_tile_spec(D)] + _half_specs(TM * TOK_ROWS, LANES) + [
                  _MOD_SPEC, _full_spec((DEPTH, D)), _layer_spec((D, 2 * D), o), _layer_spec((1, 2 * D), o)],
        out_specs=_tile_spec(D),
        out_shape=jax.ShapeDtypeStruct((T_ALL, D), F32),
        compiler_params=_params(),
        name="in_odd",
    )(x1, moe_c, moe_l, mod, g, w1, b1)


CONV_PAD_ROWS = TM + 2 * CONV_HALO
CONV_CHUNKS = D // LANES
CONV_BLOCK = 64


def _mid_odd_kernel(l, u_ref, up_ref, un_ref, x1_ref, moec_ref, moel_ref, mod_ref, dw_ref, dwb_ref, lng_ref, lnb_ref,
                    w2_ref, b2_ref, gffn_ref, wr_ref, *refs):
    outs, scratch = refs[:N_TAIL_OUT], refs[-N_TAIL_SCRATCH:]
    pad_ref, conv_ref, x_ref = refs[N_TAIL_OUT:-N_TAIL_SCRATCH]
    i = pl.program_id(0)

    def residual(pop):
        x_ref[...] = _after_moe(l - 1, _mod_row(i), x1_ref, (moec_ref, moel_ref)[pop], mod_ref)

    _per_half(i, residual)
    r = _mod_row(i)
    _, _, first, last = _seq_info(i)
    u = u_ref[...]
    up = jnp.where(first, 0.0, up_ref[...])
    un = jnp.where(last, 0.0, un_ref[...])
    half = CONV_WIDTH // 2
    first_row = CONV_HALO - half
    for c in range(CONV_CHUNKS):
        cols = slice(c * LANES, (c + 1) * LANES)
        pad_ref[c, 0:CONV_HALO, :] = up[:, cols]
        pad_ref[c, CONV_HALO:CONV_HALO + TM, :] = u[:, cols]
        pad_ref[c, CONV_HALO + TM:, :] = un[:, cols]
        bias = jnp.broadcast_to(dwb_ref[:, cols], (CONV_BLOCK, LANES))
        for r0 in range(0, TM, 2 * CONV_BLOCK):
            acc_even, acc_odd = bias, bias
            for s in range(first_row, first_row + CONV_WIDTH + 1):
                win = pad_ref[c, pl.ds(r0 + s, CONV_BLOCK, stride=2), :]
                j = s - first_row
                if j < CONV_WIDTH:
                    acc_even = acc_even + win * dw_ref[j:j + 1, cols]
                if j >= 1:
                    acc_odd = acc_odd + win * dw_ref[j - 1:j, cols]
            conv_ref[c, pl.ds(r0, CONV_BLOCK, stride=2), :] = acc_even
            conv_ref[c, pl.ds(r0 + 1, CONV_BLOCK, stride=2), :] = acc_odd
    acc = jnp.concatenate([conv_ref[c] for c in range(CONV_CHUNKS)], axis=1)
    mu = jnp.mean(acc, axis=-1, keepdims=True)
    cen = acc - mu
    var = jnp.mean(cen * cen, axis=-1, keepdims=True)
    v = _silu(cen * lax.rsqrt(var + EPS) * lng_ref[...] + lnb_ref[...])
    y = _dot(v, w2_ref[...]) + b2_ref[...]
    _mixer_tail(l, x_ref[...], y, mod_ref, r, gffn_ref, wr_ref, outs, scratch)


def _mid_odd(l, o, u, x1, moe_c, moe_l, mod, dw, dwb, lng, lnb, w2, b2, gffn, wr):
    prev, nxt = _halo_specs(CONV_HALO, D)
    return pl.pallas_call(
        functools.partial(_mid_odd_kernel, l),
        grid=(NT,),
        in_specs=[_tile_spec(D), prev, nxt, _tile_spec(D)] + _half_specs(TM * TOK_ROWS, LANES) + [
                  _MOD_SPEC, _layer_spec((CONV_WIDTH, D), o),
                  _layer_spec((1, D), o), _layer_spec((1, D), o), _layer_spec((1, D), o),
                  _layer_spec((D, D), o), _layer_spec((1, D), o), _full_spec((DEPTH, D)),
                  _layer_spec((2, D, LANES), l)],
        out_specs=_tail_out_specs(),
        out_shape=_TAIL_OUT_SHAPES,
        scratch_shapes=[pltpu.VMEM((CONV_CHUNKS, CONV_PAD_ROWS, LANES), F32),
                        pltpu.VMEM((CONV_CHUNKS, TM, LANES), F32), pltpu.VMEM((TM, D), F32)] + _tail_scratch(),
        compiler_params=_params(),
        name="mid_odd",
    )(u, u, u, x1, moe_c, moe_l, mod, dw, dwb, lng, lnb, w2, b2, gffn, wr)


def _tile_tables_kernel(cnt_ref, te_c, tc_c, tw_c, te_l, tc_l, tw_l, nact_ref, second_ref, first_ref, wplan_ref):
    for p, (te_ref, tc_ref, tw_ref) in enumerate(((te_c, tc_c, tw_c), (te_l, tc_l, tw_l))):
        run = jnp.int32(0)
        ends = []
        for e in range(N_EXPERTS):
            first_ref[p * N_EXPERTS + e] = run
            run = run + (cnt_ref[p, e] + MOE_TM - 1) // MOE_TM
            ends.append(run)
        nact_ref[p] = run

        nxt1, nxt2 = jnp.int32(0), jnp.int32(0)
        plans = []
        for e in reversed(range(N_EXPERTS)):
            plans.append(nxt2)
            used = cnt_ref[p, e] > 0
            nxt2 = jnp.where(used, nxt1, nxt2)
            nxt1 = jnp.where(used, e + 1, nxt1)
        plans.reverse()
        second_ref[p] = nxt2
        order = jnp.int32(0)
        for e in range(N_EXPERTS):
            wplan_ref[p * N_EXPERTS + e] = order + (plans[e] << W_AHEAD_SHIFT)
            order = jnp.where(cnt_ref[p, e] > 0, jnp.where(order == W_BUFFERS - 1, 0, order + 1), order)

        def expert_of(tile, ends=ends):
            e = jnp.int32(0)
            for k in range(N_EXPERTS):
                e = e + jnp.where(tile >= ends[k], 1, 0)
            return e

        last_e = expert_of(run - 1)

        def tile_body(j, c, te_ref=te_ref, tc_ref=tc_ref, tw_ref=tw_ref, p=p, run=run, expert_of=expert_of,
                      last_e=last_e):
            e = jnp.minimum(expert_of(j), last_e)
            te_ref[j] = e
            chunk = jnp.where(j < run, j - first_ref[p * N_EXPERTS + e], 0)
            tc_ref[j] = chunk
            starts_expert = jnp.logical_and(j < run, chunk == 0)
            tw_ref[j] = wplan_ref[p * N_EXPERTS + e] + jnp.where(starts_expert, W_FIRST_UNIT, 0)
            return c

        lax.fori_loop(0, te_ref.shape[0], tile_body, 0)


def _tile_tables(cnt):
    smem = pl.BlockSpec(memory_space=pltpu.SMEM)
    shapes = [(MOE_TILES,)] * 6 + [(2,), (2,)]
    return pl.pallas_call(
        _tile_tables_kernel,
        in_specs=[smem],
        out_specs=[smem] * len(shapes),
        out_shape=[jax.ShapeDtypeStruct(s, I32) for s in shapes],
        scratch_shapes=[pltpu.SMEM((2 * N_EXPERTS,), I32), pltpu.SMEM((2 * N_EXPERTS,), I32)],
        name="tile_tables",
    )(cnt)


def _moe_kernel(l, pop, te_ref, tc_ref, tw_ref, nact_ref, second_ref, list_ref, h_ref, gate_ref,
                w1_hbm, w3_hbm, w2_hbm, o_ref, xs_ref, ys_ref, w1_buf, w3_buf, w2_buf, w_sem):
    j = pl.program_id(0)
    nact = nact_ref[pop]
    last_tile = te_ref.shape[0] - 1
    tile_rows = MOE_TM * TOK_ROWS

    def weight_copies(expert, buf):
        return [pltpu.make_async_copy(hbm.at[l, expert], vmem.at[buf], w_sem.at[buf, k])
                for k, (hbm, vmem) in enumerate(((w1_hbm, w1_buf), (w3_hbm, w3_buf), (w2_hbm, w2_buf)))]

    def list_base(tile):
        return tc_ref[tile] * (MOE_TM // LIST_SPAN) * LIST_COLS + te_ref[tile] * LIST_SPAN

    def tok_rows(base, r):
        first = list_ref[base + (r // LIST_SPAN) * LIST_COLS + r % LIST_SPAN]
        return pl.ds(pl.multiple_of(first, TOK_ROWS), TOK_ROWS)

    def buf_rows(buf, r):
        return pl.ds(pl.multiple_of(buf * tile_rows + r * TOK_ROWS, TOK_ROWS), TOK_ROWS)

    def gather(tile, buf):
        base = list_base(tile)
        for r in range(MOE_TM):
            xs_ref[buf_rows(buf, r), :] = h_ref[tok_rows(base, r), :]

    tile = jnp.minimum(j, last_tile)
    plan = tw_ref[tile]
    wbuf = plan & (W_FIRST_UNIT - 1)
    starts_expert = jnp.logical_and(plan & W_FIRST_UNIT != 0, j <= last_tile)
    ahead_expert = (plan >> W_AHEAD_SHIFT) - 1
    ahead_buf = jnp.where(wbuf == 0, W_BUFFERS - 1, wbuf - 1)

    @pl.when(j == 0)
    def _():
        for cp in weight_copies(te_ref[0], 0):
            cp.start()

        @pl.when(second_ref[pop] > 0)
        def _():
            for cp in weight_copies(second_ref[pop] - 1, 1):
                cp.start()

        o_ref[...] = jnp.zeros(o_ref.shape, F32)
        ys_ref[...] = jnp.zeros(ys_ref.shape, F32)
        gather(0, 0)

    @pl.when(starts_expert)
    def _():
        for cp in weight_copies(te_ref[tile], wbuf):
            cp.wait()

        @pl.when(ahead_expert >= 0)
        def _():
            for cp in weight_copies(ahead_expert, ahead_buf):
                cp.start()

    @pl.when(j <= nact)
    def _():
        cur = j & 1
        w1_ref, w3_ref, w2_ref = w1_buf.at[wbuf], w3_buf.at[wbuf], w2_buf.at[wbuf]
        x = jnp.concatenate([xs_ref[pl.ds(cur * tile_rows + s, MOE_TM, stride=TOK_ROWS), :]
                             for s in range(TOK_ROWS)], axis=1).astype(BF16)
        gather(jnp.minimum(j + 1, last_tile), 1 - cur)

        prev_base = list_base(jnp.maximum(j - 1, 0))
        for r0 in range(0, MOE_TM, MOVE_BATCH):
            dst = [tok_rows(prev_base, r0 + k) for k in range(MOVE_BATCH)]
            vals = [o_ref[dst[k], :] + ys_ref[buf_rows(1 - cur, r0 + k), :] for k in range(MOVE_BATCH)]
            for k in reversed(range(MOVE_BATCH)):
                o_ref[dst[k], :] = vals[k]

        hg = jnp.dot(x, w1_ref[...].astype(BF16), preferred_element_type=F32)
        hu = jnp.dot(x, w3_ref[...].astype(BF16), preferred_element_type=F32)
        y = _dot(_silu(hg) * hu, w2_ref[...])

        first_row = pl.multiple_of(tc_ref[tile] * (MOE_TM // LIST_SPAN), MOE_TM // LIST_SPAN)
        spread = jnp.concatenate(
            [jnp.broadcast_to(gate_ref[pl.ds(first_row + k, 1), :], (LIST_SPAN, LIST_COLS))
             for k in range(MOE_TM // LIST_SPAN)], axis=0)
        want = te_ref[tile] * LIST_SPAN + (lax.broadcasted_iota(I32, (MOE_TM, 1), 0) & (LIST_SPAN - 1))
        hit = jnp.logical_and(lax.broadcasted_iota(I32, (MOE_TM, LIST_COLS), 1) == want, j < nact)
        y = y * jnp.sum(jnp.where(hit, spread, 0.0), axis=1, keepdims=True)
        for s in range(TOK_ROWS):
            ys_ref[pl.ds(cur * tile_rows + s, MOE_TM, stride=TOK_ROWS), :] = y[:, s * LANES:(s + 1) * LANES]


def _moe(l, pop, h_tt, te, tc, tw, nact, second, rows, gate, w1, w3, w2):
    tp = T_HALF
    blk = pop
    n_tiles = te.shape[0]
    hbm = pl.BlockSpec(memory_space=pl.ANY)
    grid_spec = pltpu.PrefetchScalarGridSpec(
        num_scalar_prefetch=6,
        grid=(n_tiles + 1,),
        in_specs=[
            pl.BlockSpec((tp * TOK_ROWS, LANES), lambda j, *_: (blk, 0), pipeline_mode=pl.Buffered(1)),
            pl.BlockSpec((LIST_ROWS, LIST_COLS), lambda j, *_: (0, 0), pipeline_mode=pl.Buffered(1)),
            hbm, hbm, hbm,
        ],
        out_specs=pl.BlockSpec((tp * TOK_ROWS, LANES), lambda j, *_: (0, 0), pipeline_mode=pl.Buffered(1)),
        scratch_shapes=[pltpu.VMEM((2 * MOE_TM * TOK_ROWS, LANES), F32),
                        pltpu.VMEM((2 * MOE_TM * TOK_ROWS, LANES), F32),
                        pltpu.VMEM((W_BUFFERS, D, EXPERT_HIDDEN), F32),
                        pltpu.VMEM((W_BUFFERS, D, EXPERT_HIDDEN), F32),
                        pltpu.VMEM((W_BUFFERS, EXPERT_HIDDEN, D), F32),
                        pltpu.SemaphoreType.DMA((W_BUFFERS, 3))],
    )
    return pl.pallas_call(
        functools.partial(_moe_kernel, l, pop),
        grid_spec=grid_spec,
        out_shape=jax.ShapeDtypeStruct((tp * TOK_ROWS, LANES), F32),
        compiler_params=_params(),
        name="moe",
    )(te, tc, tw, nact, second, rows, h_tt, gate, w1, w3, w2)


def _moe_all(l, h_tt, cnt, rows_c, gate_c, rows_l, gate_l, w1, w3, w2):
    te_c, tc_c, tw_c, te_l, tc_l, tw_l, nact, second = _tile_tables(cnt)
    out_c = _moe(l, 0, h_tt, te_c, tc_c, tw_c, nact, second, rows_c.reshape(-1), gate_c, w1, w3, w2)
    out_l = _moe(l, 1, h_tt, te_l, tc_l, tw_l, nact, second, rows_l.reshape(-1), gate_l, w1, w3, w2)
    return out_c, out_l


def _final_kernel(l, x1_ref, moec_ref, moel_ref, mod_ref, g_ref, oc_ref, ol_ref):
    i = pl.program_id(0)
    r = _mod_row(i)

    def body(half, pop):
        x = _after_moe(l, r, x1_ref, (moec_ref, moel_ref)[half], mod_ref)
        ms = jnp.mean(x * x, axis=-1, keepdims=True)
        (oc_ref, ol_ref)[pop][...] = x * lax.rsqrt(ms + EPS) * g_ref[...]

    pl.when(i < MOE_SPLIT)(functools.partial(body, 0, 0))
    pl.when(jnp.logical_and(i >= MOE_SPLIT, i < NT_CTX))(functools.partial(body, 1, 0))
    pl.when(i >= NT_CTX)(functools.partial(body, 1, 1))


def _final(l, x1, moe_c, moe_l, mod, g):
    return pl.pallas_call(
        functools.partial(_final_kernel, l),
        grid=(NT,),
        in_specs=[_tile_spec(D)] + _half_specs(TM * TOK_ROWS, LANES) + [_MOD_SPEC, _full_spec((1, D))],
        out_specs=_pop_specs(TM, D),
        out_shape=[jax.ShapeDtypeStruct((T_CTX, D), F32), jax.ShapeDtypeStruct((T_LAT, D), F32)],
        compiler_params=_params(),
        name="final_norm",
    )(x1, moe_c, moe_l, mod, g)


def _rope_tables():
    t = jnp.arange(LAT_LEN)
    row = (t // GRID_W).astype(F32)
    col = (t % GRID_W).astype(F32)
    inv = ROPE_BASE ** (-jnp.arange(0, ROPE_AXIS_DIM, 2, dtype=F32) / ROPE_AXIS_DIM)

    def table(p):
        ang = p[:, None] * inv[None, :]
        ang = jnp.concatenate([ang, ang], axis=-1)
        return jnp.cos(ang), jnp.sin(ang)

    (cr, sr), (cc, sc) = table(row), table(col)
    cos = jnp.concatenate([cr, cc, cr, cc], axis=-1)
    sin = jnp.concatenate([sr, sc, sr, sc], axis=-1)
    return cos, sin


def _router_weights(w_grp, w_exp):
    we = jnp.transpose(w_exp, (0, 2, 1, 3)).reshape(DEPTH, D, N_GROUPS * EPG)
    pad = jnp.zeros((DEPTH, D, LANES - N_GROUPS - N_GROUPS * EPG), F32)
    wr = jnp.concatenate([w_grp, we, pad], axis=-1)
    hi = lax.bitcast_convert_type(lax.bitcast_convert_type(wr, jnp.uint32) & jnp.uint32(0xFFFF0000), F32)
    return jnp.stack([hi.astype(BF16), (wr - hi).astype(BF16)], axis=1)


def kernel(x_prompt, x_sample, cache_k, cache_v, c, c_ctx, w_ada, b_ada, norm_mix_g, norm_ffn_g, w_in_ab, pool_w, pool_scale, attn_sink, w_out_ab, conv_w1, conv_b1, conv_dw, conv_dw_b, conv_ln_g, conv_ln_b, conv_w2, conv_b2, router_grp, router_exp, moe_w1, moe_w3, moe_w2, final_g):
    xc = x_prompt.reshape(T_CTX, D)
    xl = x_sample.reshape(T_LAT, D)
    cond8 = jnp.concatenate([c_ctx[None, :], c, jnp.zeros((SUBLANES - 1 - N_LAT_SEQ, D), F32)], axis=0)
    mod = _modulation(cond8, w_ada, b_ada)
    cos_t, sin_t = _rope_tables()
    wr = _router_weights(router_grp, router_exp)
    n_even, n_odd = w_in_ab.shape[0], conv_w1.shape[0]
    past = cache_k.shape[2]

    a, q, k, v = _in_even(0, 0, xc, xl, mod, norm_mix_g, w_in_ab, cos_t, sin_t)
    yb_c = _attn_ctx(0, attn_sink, q, k, v)
    yb_l = _attn_lat(0, attn_sink, q, k, v, cache_k[:, 0].reshape(N_LAT_SEQ, past, KV_DIM),
                     cache_v[:, 0].reshape(N_LAT_SEQ, past, KV_DIM))
    x1, h2, *routing = _mid_even(0, 0, a, yb_c, yb_l, xc, xl, mod, pool_w,
                                 pool_scale.reshape(n_even, 1, POOL_DIM), w_out_ab, norm_ffn_g, wr)
    moe0 = _moe_all(0, h2, *routing, moe_w1, moe_w3, moe_w2)

    vec = lambda p: p.reshape(n_odd, 1, -1)
    u = _in_odd(1, 0, x1, *moe0, mod, norm_mix_g, conv_w1, vec(conv_b1))
    x3, h2b, *routing = _mid_odd(1, 0, u, x1, *moe0, mod, conv_dw, vec(conv_dw_b), vec(conv_ln_g),
                                 vec(conv_ln_b), conv_w2, vec(conv_b2), norm_ffn_g, wr)
    moe1 = _moe_all(1, h2b, *routing, moe_w1, moe_w3, moe_w2)
    y_c, y_l = _final(1, x3, *moe1, mod, final_g.reshape(1, D))

    y_prompt = y_c.reshape(N_CTX_SEQ, CTX_LEN, D)
    y_sample = y_l.reshape(N_LAT_SEQ, LAT_LEN, D)
    state_k = k[:T_CTX].reshape(N_CTX_SEQ, 1, CTX_LEN, N_KV_HEADS, HEAD_DIM)
    state_v = v[:T_CTX].reshape(N_CTX_SEQ, 1, CTX_LEN, N_KV_HEADS, HEAD_DIM)
    return (y_prompt, y_sample, state_k, state_v)
```

```python
import functools

import jax
import jax.numpy as jnp
from jax import lax
from jax.experimental import pallas as pl
from jax.experimental.pallas import tpu as pltpu

F32 = jnp.float32
BF16 = jnp.bfloat16
I32 = jnp.int32

D = 1024
N_CTX_SEQ = 16
CTX_LEN = 256
N_LAT_SEQ = 2
LAT_LEN = 1024
T_CTX = N_CTX_SEQ * CTX_LEN
T_LAT = N_LAT_SEQ * LAT_LEN
T_ALL = T_CTX + T_LAT
TM = 256
NT = T_ALL // TM
NT_CTX = T_CTX // TM
LAT_TILES_PER_SEQ = LAT_LEN // TM
GRID_W = 64
DEPTH = 2

POOL_WINDOWS = (2, 4, 8, 16)
POOL_GROUP_DIM = 128
POOL_DIM = 512
HEAD_DIM = 64
N_Q_HEADS = 8
N_KV_HEADS = 2
Q_PER_KV = 4
Q_DIM = 512
KV_DIM = 128
IN_AB = 1280
ATTN_WINDOW = 128
ATTN_BLOCK = 128
ATTN_SCALE = HEAD_DIM ** -0.5
ROPE_BASE = 10000.0
ROPE_AXIS_DIM = 32
CONV_WIDTH = 31
CONV_HALO = 16
POOL_HALO = 8
N_GROUPS = 4
EPG = 4
N_EXPERTS = 16
EXPERT_HIDDEN = 512
EPS = 1e-6
NEG_BIG = -1e30

SUBLANES = 8
LANES = 128
TOK_ROWS = D // LANES

MOE_TM = 256
MOE_SPLIT = NT // 2
T_HALF = MOE_SPLIT * TM
MOE_TILES = 2 * T_HALF // MOE_TM + N_EXPERTS
W_BUFFERS = 3
W_FIRST_UNIT = 4
W_AHEAD_SHIFT = 3
MOVE_BATCH_LOG2 = 2
MOVE_BATCH = 1 << MOVE_BATCH_LOG2
VMEM_LIMIT = 56 * 1024 * 1024


def _silu(x):
    return x * (1.0 / (1.0 + jnp.exp(-x)))


def _mod_row(i):
    return jnp.where(i < NT_CTX, 0, 1 + (i - NT_CTX) // LAT_TILES_PER_SEQ)


def _seq_info(i):
    is_ctx = i < NT_CTX
    k = (i - NT_CTX) % LAT_TILES_PER_SEQ
    off = jnp.where(is_ctx, 0, k * TM)
    n = jnp.where(is_ctx, CTX_LEN, LAT_LEN)
    first = jnp.logical_or(is_ctx, k == 0)
    last = jnp.logical_or(is_ctx, k == LAT_TILES_PER_SEQ - 1)
    return off, n, first, last


def _rms_mod(x, g, scale, shift):
    ms = jnp.mean(x * x, axis=-1, keepdims=True)
    return (x * lax.rsqrt(ms + EPS) * g) * (1.0 + scale) + shift


def _dot(a, b):
    return jnp.dot(a.astype(BF16), b.astype(BF16), preferred_element_type=F32)


def _dot_nt(a, b):
    return lax.dot_general(a.astype(BF16), b.astype(BF16), (((1,), (1,)), ((), ())),
                           preferred_element_type=F32)


def _from_token_tiles(ref):
    return jnp.concatenate([ref[pl.ds(s, TM, stride=TOK_ROWS), :] for s in range(TOK_ROWS)], axis=1)


def _tile_spec(width):
    return pl.BlockSpec((TM, width), lambda i: (i, 0))


def _full_spec(shape):
    return pl.BlockSpec(shape, lambda i: (0,) * len(shape))


def _layer_spec(shape, l):
    return pl.BlockSpec((None,) + tuple(shape), lambda i: (l,) + (0,) * len(shape))


def _pop_specs(rows, width):
    return [pl.BlockSpec((rows, width), lambda i: (jnp.minimum(i, NT_CTX - 1), 0)),
            pl.BlockSpec((rows, width), lambda i: (jnp.maximum(i - NT_CTX, 0), 0))]


def _per_population(i, body):
    pl.when(i < NT_CTX)(functools.partial(body, 0))
    pl.when(i >= NT_CTX)(functools.partial(body, 1))


def _half_specs(rows, width):
    return [pl.BlockSpec((rows, width), lambda i: (jnp.minimum(i, MOE_SPLIT - 1), 0)),
            pl.BlockSpec((rows, width), lambda i: (jnp.maximum(i - MOE_SPLIT, 0), 0))]


def _per_half(i, body):
    pl.when(i < MOE_SPLIT)(functools.partial(body, 0))
    pl.when(i >= MOE_SPLIT)(functools.partial(body, 1))


_MOD_SPEC = pl.BlockSpec((DEPTH, SUBLANES, 6 * D), lambda i: (0, 0, 0))


def _params():
    return pltpu.CompilerParams(vmem_limit_bytes=VMEM_LIMIT)


MOD_TN = 1536


def _mod_kernel(cond_ref, w_ref, b_ref, o_ref):
    s = _silu(cond_ref[...])
    o_ref[0] = _dot(s, w_ref[0]) + b_ref[0]


def _modulation(cond8, w_ada, b_ada):
    return pl.pallas_call(
        _mod_kernel,
        grid=(DEPTH, 6 * D // MOD_TN),
        in_specs=[
            pl.BlockSpec((SUBLANES, D), lambda l, n: (0, 0)),
            pl.BlockSpec((1, D, MOD_TN), lambda l, n: (l, 0, n)),
            pl.BlockSpec((1, 1, MOD_TN), lambda l, n: (l, 0, n)),
        ],
        out_specs=pl.BlockSpec((1, SUBLANES, MOD_TN), lambda l, n: (l, 0, n)),
        out_shape=jax.ShapeDtypeStruct((DEPTH, SUBLANES, 6 * D), F32),
        compiler_params=_params(),
        name="modulation",
    )(cond8, w_ada, b_ada.reshape(DEPTH, 1, 6 * D))


def _rope_chunk(xc, cos, sin):
    lane = lax.broadcasted_iota(I32, xc.shape, 1)
    first = (lane % ROPE_AXIS_DIM) < (ROPE_AXIS_DIM // 2)
    rot = jnp.where(first, -pltpu.roll(xc, LANES - ROPE_AXIS_DIM // 2, 1), pltpu.roll(xc, ROPE_AXIS_DIM // 2, 1))
    return xc * cos + rot * sin


def _in_even_kernel(l, xc_ref, xl_ref, mod_ref, g_ref, w_ref, cos_ref, sin_ref,
                    a_ref, q_ref, kc_ref, kl_ref, vc_ref, vl_ref):
    i = pl.program_id(0)
    r = _mod_row(i)

    def body(pop):
        x = (xc_ref, xl_ref)[pop][...]
        k_ref, v_ref = (kc_ref, kl_ref)[pop], (vc_ref, vl_ref)[pop]
        shift = mod_ref[l, pl.ds(r, 1), 0:D]
        scale = mod_ref[l, pl.ds(r, 1), D:2 * D]
        h = _rms_mod(x, g_ref[l:l + 1, :], scale, shift)
        proj = _dot(h, w_ref[...])
        a_ref[...] = proj[:, :POOL_DIM]
        v_ref[...] = proj[:, POOL_DIM + Q_DIM + KV_DIM:]
        if pop == 0:
            q_ref[...] = proj[:, POOL_DIM:POOL_DIM + Q_DIM].astype(BF16)
            k_ref[...] = proj[:, POOL_DIM + Q_DIM:POOL_DIM + Q_DIM + KV_DIM]
        else:
            off, _, _, _ = _seq_info(i)
            off = pl.multiple_of(off, TM)
            cos = cos_ref[pl.ds(off, TM), :]
            sin = sin_ref[pl.ds(off, TM), :]
            for c in range(Q_DIM // LANES):
                lo = POOL_DIM + c * LANES
                q_ref[:, c * LANES:(c + 1) * LANES] = _rope_chunk(proj[:, lo:lo + LANES], cos, sin).astype(BF16)
            lo = POOL_DIM + Q_DIM
            k_ref[...] = _rope_chunk(proj[:, lo:lo + LANES], cos, sin)

    _per_population(i, body)


def _in_even(l, e, xc, xl, mod, g, w_in, cos_t, sin_t):
    return pl.pallas_call(
        functools.partial(_in_even_kernel, l),
        grid=(NT,),
        in_specs=_pop_specs(TM, D) + [_MOD_SPEC, _full_spec((DEPTH, D)), _layer_spec((D, IN_AB), e),
                  _full_spec((LAT_LEN, LANES)), _full_spec((LAT_LEN, LANES))],
        out_specs=[_tile_spec(POOL_DIM), _tile_spec(Q_DIM)] + _pop_specs(TM, KV_DIM) + _pop_specs(TM, KV_DIM),
        out_shape=[jax.ShapeDtypeStruct((T_ALL, POOL_DIM), F32), jax.ShapeDtypeStruct((T_ALL, Q_DIM), BF16)]
                  + [jax.ShapeDtypeStruct((t, KV_DIM), F32) for t in (T_CTX, T_LAT, T_CTX, T_LAT)],
        compiler_params=_params(),
        name="in_even",
    )(xc, xl, mod, g, w_in, cos_t, sin_t)


def _attn_ctx_kernel(e, sink_ref, q_ref, k_ref, v_ref, o_ref):
    q = q_ref[...]
    k = k_ref[...]
    v = v_ref[...]
    for h in range(N_KV_HEADS):
        kh = k[:, h * HEAD_DIM:(h + 1) * HEAD_DIM]
        vh = v[:, h * HEAD_DIM:(h + 1) * HEAD_DIM]
        for g in range(Q_PER_KV):
            j = h * Q_PER_KV + g
            s = _dot_nt(q[:, j * HEAD_DIM:(j + 1) * HEAD_DIM], kh) * ATTN_SCALE
            sk = sink_ref[e, j]
            m = jnp.maximum(jnp.max(s, axis=-1, keepdims=True), sk)
            p = jnp.exp(s - m)
            denom = jnp.sum(p, axis=-1, keepdims=True) + jnp.exp(sk - m)
            o_ref[:, j * HEAD_DIM:(j + 1) * HEAD_DIM] = (_dot(p, vh) / denom).astype(o_ref.dtype)


def _attn_ctx(e, sink, q, k, v):
    tile = lambda w: pl.BlockSpec((TM, w), lambda b: (b, 0))
    return pl.pallas_call(
        functools.partial(_attn_ctx_kernel, e),
        grid=(N_CTX_SEQ,),
        in_specs=[pl.BlockSpec(memory_space=pltpu.SMEM), tile(Q_DIM), tile(KV_DIM), tile(KV_DIM)],
        out_specs=tile(Q_DIM),
        out_shape=jax.ShapeDtypeStruct((T_CTX, Q_DIM), BF16),
        compiler_params=_params(),
        name="attn_ctx",
    )(sink, q, k, v)


LAT_BLOCKS = LAT_LEN // ATTN_BLOCK
BAND = 3 * ATTN_BLOCK


def _attn_lat_kernel(e, sink_ref, q_ref, k_ref, v_ref, ck_ref, cv_ref, o_ref):
    n = pl.program_id(1)
    start = jnp.clip((n - 1) * ATTN_BLOCK, 0, LAT_LEN - BAND)
    start = pl.multiple_of(start, ATTN_BLOCK)
    q = q_ref[...]
    kw = k_ref[pl.ds(start, BAND), :]
    vw = v_ref[pl.ds(start, BAND), :]
    ck = ck_ref[0]
    cv = cv_ref[0]
    qpos = n * ATTN_BLOCK + lax.broadcasted_iota(I32, (ATTN_BLOCK, BAND), 0)
    kpos = start + lax.broadcasted_iota(I32, (ATTN_BLOCK, BAND), 1)
    valid = jnp.abs(qpos - kpos) <= ATTN_WINDOW
    for h in range(N_KV_HEADS):
        hs = slice(h * HEAD_DIM, (h + 1) * HEAD_DIM)
        for g in range(Q_PER_KV):
            j = h * Q_PER_KV + g
            qj = q[:, j * HEAD_DIM:(j + 1) * HEAD_DIM]
            s_loc = jnp.where(valid, _dot_nt(qj, kw[:, hs]) * ATTN_SCALE, NEG_BIG)
            s_ctx = _dot_nt(qj, ck[:, hs]) * ATTN_SCALE
            sk = sink_ref[e, j]
            m = jnp.maximum(jnp.maximum(jnp.max(s_loc, axis=-1, keepdims=True),
                                        jnp.max(s_ctx, axis=-1, keepdims=True)), sk)
            p_loc = jnp.exp(s_loc - m)
            p_ctx = jnp.exp(s_ctx - m)
            denom = (jnp.sum(p_loc, axis=-1, keepdims=True) + jnp.sum(p_ctx, axis=-1, keepdims=True)
                     + jnp.exp(sk - m))
            o = _dot(p_loc, vw[:, hs]) + _dot(p_ctx, cv[:, hs])
            o_ref[:, j * HEAD_DIM:(j + 1) * HEAD_DIM] = (o / denom).astype(o_ref.dtype)


def _attn_lat(e, sink, q, k, v, ck, cv):
    past = ck.shape[1]
    q_blk0 = T_CTX // ATTN_BLOCK
    q_spec = pl.BlockSpec((ATTN_BLOCK, Q_DIM), lambda b, n: (q_blk0 + b * LAT_BLOCKS + n, 0))
    return pl.pallas_call(
        functools.partial(_attn_lat_kernel, e),
        grid=(N_LAT_SEQ, LAT_BLOCKS),
        in_specs=[pl.BlockSpec(memory_space=pltpu.SMEM),
                  q_spec,
                  pl.BlockSpec((LAT_LEN, KV_DIM), lambda b, n: (b, 0)),
                  pl.BlockSpec((LAT_LEN, KV_DIM), lambda b, n: (b, 0)),
                  pl.BlockSpec((1, past, KV_DIM), lambda b, n: (b, 0, 0)),
                  pl.BlockSpec((1, past, KV_DIM), lambda b, n: (b, 0, 0))],
        out_specs=pl.BlockSpec((ATTN_BLOCK, Q_DIM), lambda b, n: (b * LAT_BLOCKS + n, 0)),
        out_shape=jax.ShapeDtypeStruct((T_LAT, Q_DIM), BF16),
        compiler_params=_params(),
        name="attn_lat",
    )(sink, q, k, v, ck, cv)


def _router(h2, wr_ref):
    hi = h2.astype(BF16)
    lo = (h2 - hi.astype(F32)).astype(BF16)
    whi = wr_ref[0]
    wlo = wr_ref[1]
    dot = functools.partial(jnp.dot, preferred_element_type=F32)
    logits_t = (dot(hi, whi) + (dot(lo, whi) + dot(hi, wlo))).T
    row = lambda k: logits_t[k:k + 1, :]

    def first_max(vals):
        best = functools.reduce(jnp.maximum, vals)
        idx = jnp.full(best.shape, len(vals) - 1, F32)
        for k in reversed(range(len(vals) - 1)):
            idx = jnp.where(vals[k] == best, float(k), idx)
        return best, idx

    grp = [row(k) for k in range(N_GROUPS)]
    gmax, g_idx = first_max(grp)
    g_w = 1.0 / sum(jnp.exp(gk - gmax) for gk in grp)
    el = []
    for e in range(EPG):
        v = row(N_GROUPS + (N_GROUPS - 1) * EPG + e)
        for k in reversed(range(N_GROUPS - 1)):
            v = jnp.where(g_idx == k, row(N_GROUPS + k * EPG + e), v)
        el.append(v)
    t1, i1 = first_max(el)
    t2, i2 = first_max([jnp.where(i1 == e, -jnp.inf, el[e]) for e in range(EPG)])
    d = jnp.exp(t2 - t1)
    w1 = g_w / (1.0 + d)
    w2 = g_w * d / (1.0 + d)
    return EPG * g_idx + i1, EPG * g_idx + i2, w1, w2


LIST_SPAN_LOG2 = 5
LIST_SPAN = 1 << LIST_SPAN_LOG2
LIST_ROWS = LANES
LIST_COLS = N_EXPERTS * LIST_SPAN
LIST_BLOCKS = LIST_COLS // LANES
LIST_PER_BLOCK_LOG2 = 2


N_TAIL_OUT = 7
N_TAIL_SCRATCH = 3


def _mixer_tail(l, x, y, mod_ref, r, gffn_ref, wr_ref, outs, scratch):
    x1_ref, h2_ref, cnt_ref, lc_ref, gc_ref, ll_ref, gl_ref = outs
    run_ref, acc_ref, gacc_ref = scratch
    i = pl.program_id(0)
    g1 = mod_ref[l, pl.ds(r, 1), 2 * D:3 * D]
    shift2 = mod_ref[l, pl.ds(r, 1), 3 * D:4 * D]
    scale2 = mod_ref[l, pl.ds(r, 1), 4 * D:5 * D]
    x1 = x + g1 * y
    x1_ref[...] = x1
    h2 = _rms_mod(x1, gffn_ref[l:l + 1, :], scale2, shift2)
    for s in range(TOK_ROWS):
        h2_ref[pl.ds(s, TM, stride=TOK_ROWS), :] = h2[:, s * LANES:(s + 1) * LANES]
    e1, e2, w1_row, w2_row = _router(h2, wr_ref)

    @pl.when(jnp.logical_or(i == 0, i == MOE_SPLIT))
    def _():
        run_ref[...] = jnp.zeros(run_ref.shape, F32)
        acc_ref[...] = jnp.zeros(acc_ref.shape, F32)
        gacc_ref[...] = jnp.zeros(gacc_ref.shape, F32)

    @pl.when(i == 0)
    def _():
        cnt_ref[...] = jnp.zeros(cnt_ref.shape, I32)

    sub_i = lax.broadcasted_iota(I32, (LANES, TM), 0)
    sub = sub_i.astype(F32)
    member = jnp.where(jnp.logical_or(sub == e1, sub == e2), 1.0, 0.0)
    tri = jnp.where(lax.broadcasted_iota(I32, (TM, TM), 0) <= lax.broadcasted_iota(I32, (TM, TM), 1), 1.0, 0.0)
    csum = jnp.dot(member.astype(BF16), tri.astype(BF16), preferred_element_type=F32)
    run = run_ref[:, 0:1]
    before = csum - member + run
    run_new = run + csum[:, TM - 1:TM]
    run_ref[...] = jnp.broadcast_to(run_new, run_ref.shape)

    col_i = lax.broadcasted_iota(I32, (TM, LIST_COLS), 1)
    rows_oh, col_rows = [], []
    for e in (e1, e2):
        rank = jnp.zeros((1, TM), F32)
        for k in range(N_EXPERTS):
            rank = rank + jnp.where(e == k, before[k:k + 1, :], 0.0)
        rank_i = rank.astype(I32)
        rows_oh.append(jnp.where(sub_i == (rank_i >> LIST_SPAN_LOG2), 1.0, 0.0))
        col_rows.append(e.astype(I32) * LIST_SPAN + (rank_i & (LIST_SPAN - 1)))
    rows_t = jnp.concatenate(rows_oh, axis=1)
    col_cols = jnp.concatenate(col_rows + [jnp.zeros((LANES - 2, TM), I32)], axis=0).T
    cols = jnp.concatenate([jnp.where(col_i == col_cols[:, k:k + 1], 1.0, 0.0).astype(BF16) for k in range(2)],
                           axis=0)
    tok = (lax.broadcasted_iota(I32, (1, 2 * TM), 1) % TM).astype(F32)
    rest = jnp.concatenate([w1_row, w2_row], axis=1)
    values = [None, tok]
    for _ in range(3):
        piece = rest.astype(BF16).astype(F32)
        values.append(piece)
        rest = rest - piece
    planes = jnp.concatenate([rows_t if v is None else rows_t * v for v in values], axis=0).astype(BF16)
    out = jnp.dot(planes, cols, preferred_element_type=F32)
    hits, tok_sum, gp0, gp1, gp2 = [out[k * LANES:(k + 1) * LANES] for k in range(len(values))]
    tile_in_pop = jnp.where(i < MOE_SPLIT, i, i - MOE_SPLIT).astype(F32)
    acc = acc_ref[...] + TOK_ROWS * (tok_sum + TM * tile_in_pop * hits)
    acc_ref[...] = acc
    gacc = gacc_ref[...] + gp0 + gp1 + gp2
    gacc_ref[...] = gacc

    def counts_row():
        return jnp.broadcast_to(run_new, (LANES, LANES)).T[0:1, :].astype(I32)

    def put_rows(dst_ref):
        for b in range(LIST_BLOCKS):
            dst_ref[b * LIST_ROWS:(b + 1) * LIST_ROWS, :] = acc[:, b * LANES:(b + 1) * LANES].astype(I32)

    @pl.when(i == MOE_SPLIT - 1)
    def _():
        cnt_ref[0:1, :] = counts_row()
        put_rows(lc_ref)
        gc_ref[...] = gacc

    @pl.when(i == NT - 1)
    def _():
        cnt_ref[1:2, :] = counts_row()
        put_rows(ll_ref)
        gl_ref[...] = gacc


_LIST_OUT_SHAPES = [(LIST_BLOCKS * LIST_ROWS, LANES), (LIST_ROWS, LIST_COLS)] * 2
_TAIL_OUT_SHAPES = [jax.ShapeDtypeStruct((T_ALL, D), F32),
                    jax.ShapeDtypeStruct((T_ALL * TOK_ROWS, LANES), F32),
                    jax.ShapeDtypeStruct((SUBLANES, LANES), I32)] + [
                        jax.ShapeDtypeStruct(s, dt) for s, dt in zip(_LIST_OUT_SHAPES, (I32, F32, I32, F32))]


def _tail_scratch():
    return [pltpu.VMEM((LANES, LANES), F32), pltpu.VMEM((LIST_ROWS, LIST_COLS), F32),
            pltpu.VMEM((LIST_ROWS, LIST_COLS), F32)]


def _tail_out_specs():
    return [_tile_spec(D),
            pl.BlockSpec((TM * TOK_ROWS, LANES), lambda i: (i, 0)),
            _full_spec((SUBLANES, LANES))] + [_full_spec(s) for s in _LIST_OUT_SHAPES]


def _halo_specs(halo, width):
    per = TM // halo
    last = T_ALL // halo - 1
    prev = pl.BlockSpec((halo, width), lambda i: (jnp.maximum(i * per - 1, 0), 0))
    nxt = pl.BlockSpec((halo, width), lambda i: (jnp.minimum((i + 1) * per, last), 0))
    return prev, nxt


def _mid_even_kernel(l, a_ref, ap_ref, an_ref, ybc_ref, ybl_ref, xc_ref, xl_ref, mod_ref, wp_ref, ps_ref, wo_ref,
                     gffn_ref, wr_ref, *refs):
    outs, (pad_ref, pool_ref), scratch = refs[:N_TAIL_OUT], refs[N_TAIL_OUT:-N_TAIL_SCRATCH], refs[-N_TAIL_SCRATCH:]
    i = pl.program_id(0)
    r = _mod_row(i)
    off, n, first, last = _seq_info(i)
    a = a_ref[...]
    ap = jnp.where(first, 0.0, ap_ref[...])
    an = jnp.where(last, 0.0, an_ref[...])
    half_rows = TM // 2
    pos = [(off + 2 * lax.broadcasted_iota(I32, (half_rows, 1), 0) + parity).astype(F32) for parity in (0, 1)]
    last_pos = (n - 1).astype(F32)
    mixed = []
    for g, w in enumerate(POOL_WINDOWS):
        lo = w // 2
        hi = w - lo - 1
        cols = slice(g * POOL_GROUP_DIM, (g + 1) * POOL_GROUP_DIM)
        pad_ref[g, 0:POOL_HALO, :] = ap[:, cols]
        pad_ref[g, POOL_HALO:POOL_HALO + TM, :] = a[:, cols]
        pad_ref[g, POOL_HALO + TM:, :] = an[:, cols]
        loads = {s: pad_ref[g, pl.ds(s, half_rows, stride=2), :]
                 for s in range(POOL_HALO - lo, POOL_HALO + hi + 2)}
        for parity in (0, 1):
            total = functools.reduce(lambda x, y: x + y,
                                     [loads[POOL_HALO + parity + j] for j in range(-lo, hi + 1)])
            count = jnp.minimum(pos[parity] + hi, last_pos) - jnp.maximum(pos[parity] - lo, 0.0) + 1.0
            pool_ref[g, pl.ds(parity, half_rows, stride=2), :] = total / count - loads[POOL_HALO + parity]
        mixed.append(_dot(pool_ref[g], wp_ref[g]))
    ya = jnp.concatenate(mixed, axis=1) * ps_ref[...]
    y_pool = _dot(ya, wo_ref[0:POOL_DIM, :])

    def body(pop):
        y = y_pool + _dot((ybc_ref, ybl_ref)[pop][...], wo_ref[POOL_DIM:, :])
        _mixer_tail(l, (xc_ref, xl_ref)[pop][...], y, mod_ref, r, gffn_ref, wr_ref, outs, scratch)

    _per_population(i, body)


def _mid_even(l, e, a, yb_c, yb_l, xc, xl, mod, w_pool, pool_scale, w_out, gffn, wr):
    prev, nxt = _halo_specs(POOL_HALO, POOL_DIM)
    return pl.pallas_call(
        functools.partial(_mid_even_kernel, l),
        grid=(NT,),
        in_specs=[_tile_spec(POOL_DIM), prev, nxt] + _pop_specs(TM, Q_DIM) + _pop_specs(TM, D) + [
                  _MOD_SPEC, _layer_spec((len(POOL_WINDOWS), POOL_GROUP_DIM, POOL_GROUP_DIM), e),
                  _layer_spec((1, POOL_DIM), e), _layer_spec((D, D), e), _full_spec((DEPTH, D)),
                  _layer_spec((2, D, LANES), l)],
        out_specs=_tail_out_specs(),
        out_shape=_TAIL_OUT_SHAPES,
        scratch_shapes=[pltpu.VMEM((len(POOL_WINDOWS), TM + 2 * POOL_HALO, POOL_GROUP_DIM), F32),
                        pltpu.VMEM((len(POOL_WINDOWS), TM, POOL_GROUP_DIM), F32)] + _tail_scratch(),
        compiler_params=_params(),
        name="mid_even",
    )(a, a, a, yb_c, yb_l, xc, xl, mod, w_pool, pool_scale, w_out, gffn, wr)


def _after_moe(l, r, x1_ref, moe_ref, mod_ref):
    return x1_ref[...] + mod_ref[l, pl.ds(r, 1), 5 * D:6 * D] * _from_token_tiles(moe_ref)


def _in_odd_kernel(l, x1_ref, moec_ref, moel_ref, mod_ref, g_ref, w1_ref, b1_ref, u_ref):
    i = pl.program_id(0)
    r = _mod_row(i)

    def body(pop):
        x2 = _after_moe(l - 1, r, x1_ref, (moec_ref, moel_ref)[pop], mod_ref)
        shift = mod_ref[l, pl.ds(r, 1), 0:D]
        scale = mod_ref[l, pl.ds(r, 1), D:2 * D]
        h = _rms_mod(x2, g_ref[l:l + 1, :], scale, shift)
        u = _dot(h, w1_ref[...]) + b1_ref[...]
        u_ref[...] = u[:, :D] * (1.0 / (1.0 + jnp.exp(-u[:, D:])))

    _per_half(i, body)


def _in_odd(l, o, x1, moe_c, moe_l, mod, g, w1, b1):
    return pl.pallas_call(
        functools.partial(_in_odd_kernel, l),
        grid=(NT,),
        in_specs=[_tile_spec(D)] + _half_specs(TM * TOK_ROWS, LANES) + [
                  _MOD_SPEC, _full_spec((DEPTH, D)), _layer_spec((D, 2 * D), o), _layer_spec((1, 2 * D), o)],
        out_specs=_tile_spec(D),
        out_shape=jax.ShapeDtypeStruct((T_ALL, D), F32),
        compiler_params=_params(),
        name="in_odd",
    )(x1, moe_c, moe_l, mod, g, w1, b1)


CONV_PAD_ROWS = TM + 2 * CONV_HALO
CONV_CHUNKS = D // LANES
CONV_BLOCK = 64


def _mid_odd_kernel(l, u_ref, up_ref, un_ref, x1_ref, moec_ref, moel_ref, mod_ref, dw_ref, dwb_ref, lng_ref, lnb_ref,
                    w2_ref, b2_ref, gffn_ref, wr_ref, *refs):
    outs, scratch = refs[:N_TAIL_OUT], refs[-N_TAIL_SCRATCH:]
    pad_ref, conv_ref, x_ref = refs[N_TAIL_OUT:-N_TAIL_SCRATCH]
    i = pl.program_id(0)

    def residual(pop):
        x_ref[...] = _after_moe(l - 1, _mod_row(i), x1_ref, (moec_ref, moel_ref)[pop], mod_ref)

    _per_half(i, residual)
    r = _mod_row(i)
    _, _, first, last = _seq_info(i)
    u = u_ref[...]
    up = jnp.where(first, 0.0, up_ref[...])
    un = jnp.where(last, 0.0, un_ref[...])
    half = CONV_WIDTH // 2
    first_row = CONV_HALO - half
    for c in range(CONV_CHUNKS):
        cols = slice(c * LANES, (c + 1) * LANES)
        pad_ref[c, 0:CONV_HALO, :] = up[:, cols]
        pad_ref[c, CONV_HALO:CONV_HALO + TM, :] = u[:, cols]
        pad_ref[c, CONV_HALO + TM:, :] = un[:, cols]
        bias = jnp.broadcast_to(dwb_ref[:, cols], (CONV_BLOCK, LANES))
        for r0 in range(0, TM, 2 * CONV_BLOCK):
            acc_even, acc_odd = bias, bias
            for s in range(first_row, first_row + CONV_WIDTH + 1):
                win = pad_ref[c, pl.ds(r0 + s, CONV_BLOCK, stride=2), :]
                j = s - first_row
                if j < CONV_WIDTH:
                    acc_even = acc_even + win * dw_ref[j:j + 1, cols]
                if j >= 1:
                    acc_odd = acc_odd + win * dw_ref[j - 1:j, cols]
            conv_ref[c, pl.ds(r0, CONV_BLOCK, stride=2), :] = acc_even
            conv_ref[c, pl.ds(r0 + 1, CONV_BLOCK, stride=2), :] = acc_odd
    acc = jnp.concatenate([conv_ref[c] for c in range(CONV_CHUNKS)], axis=1)
    mu = jnp.mean(acc, axis=-1, keepdims=True)
    cen = acc - mu
    var = jnp.mean(cen * cen, axis=-1, keepdims=True)
    v = _silu(cen * lax.rsqrt(var + EPS) * lng_ref[...] + lnb_ref[...])
    y = _dot(v, w2_ref[...]) + b2_ref[...]
    _mixer_tail(l, x_ref[...], y, mod_ref, r, gffn_ref, wr_ref, outs, scratch)


def _mid_odd(l, o, u, x1, moe_c, moe_l, mod, dw, dwb, lng, lnb, w2, b2, gffn, wr):
    prev, nxt = _halo_specs(CONV_HALO, D)
    return pl.pallas_call(
        functools.partial(_mid_odd_kernel, l),
        grid=(NT,),
        in_specs=[_tile_spec(D), prev, nxt, _tile_spec(D)] + _half_specs(TM * TOK_ROWS, LANES) + [
                  _MOD_SPEC, _layer_spec((CONV_WIDTH, D), o),
                  _layer_spec((1, D), o), _layer_spec((1, D), o), _layer_spec((1, D), o),
                  _layer_spec((D, D), o), _layer_spec((1, D), o), _full_spec((DEPTH, D)),
                  _layer_spec((2, D, LANES), l)],
        out_specs=_tail_out_specs(),
        out_shape=_TAIL_OUT_SHAPES,
        scratch_shapes=[pltpu.VMEM((CONV_CHUNKS, CONV_PAD_ROWS, LANES), F32),
                        pltpu.VMEM((CONV_CHUNKS, TM, LANES), F32), pltpu.VMEM((TM, D), F32)] + _tail_scratch(),
        compiler_params=_params(),
        name="mid_odd",
    )(u, u, u, x1, moe_c, moe_l, mod, dw, dwb, lng, lnb, w2, b2, gffn, wr)


def _tile_tables_kernel(cnt_ref, te_c, tc_c, tw_c, te_l, tc_l, tw_l, nact_ref, second_ref, first_ref, wplan_ref):
    for p, (te_ref, tc_ref, tw_ref) in enumerate(((te_c, tc_c, tw_c), (te_l, tc_l, tw_l))):
        run = jnp.int32(0)
        ends = []
        for e in range(N_EXPERTS):
            first_ref[p * N_EXPERTS + e] = run
            run = run + (cnt_ref[p, e] + MOE_TM - 1) // MOE_TM
            ends.append(run)
        nact_ref[p] = run

        nxt1, nxt2 = jnp.int32(0), jnp.int32(0)
        plans = []
        for e in reversed(range(N_EXPERTS)):
            plans.append(nxt2)
            used = cnt_ref[p, e] > 0
            nxt2 = jnp.where(used, nxt1, nxt2)
            nxt1 = jnp.where(used, e + 1, nxt1)
        plans.reverse()
        second_ref[p] = nxt2
        order = jnp.int32(0)
        for e in range(N_EXPERTS):
            wplan_ref[p * N_EXPERTS + e] = order + (plans[e] << W_AHEAD_SHIFT)
            order = jnp.where(cnt_ref[p, e] > 0, jnp.where(order == W_BUFFERS - 1, 0, order + 1), order)

        def expert_of(tile, ends=ends):
            e = jnp.int32(0)
            for k in range(N_EXPERTS):
                e = e + jnp.where(tile >= ends[k], 1, 0)
            return e

        last_e = expert_of(run - 1)

        def tile_body(j, c, te_ref=te_ref, tc_ref=tc_ref, tw_ref=tw_ref, p=p, run=run, expert_of=expert_of,
                      last_e=last_e):
            e = jnp.minimum(expert_of(j), last_e)
            te_ref[j] = e
            chunk = jnp.where(j < run, j - first_ref[p * N_EXPERTS + e], 0)
            tc_ref[j] = chunk
            starts_expert = jnp.logical_and(j < run, chunk == 0)
            tw_ref[j] = wplan_ref[p * N_EXPERTS + e] + jnp.where(starts_expert, W_FIRST_UNIT, 0)
            return c

        lax.fori_loop(0, te_ref.shape[0], tile_body, 0)


def _tile_tables(cnt):
    smem = pl.BlockSpec(memory_space=pltpu.SMEM)
    shapes = [(MOE_TILES,)] * 6 + [(2,), (2,)]
    return pl.pallas_call(
        _tile_tables_kernel,
        in_specs=[smem],
        out_specs=[smem] * len(shapes),
        out_shape=[jax.ShapeDtypeStruct(s, I32) for s in shapes],
        scratch_shapes=[pltpu.SMEM((2 * N_EXPERTS,), I32), pltpu.SMEM((2 * N_EXPERTS,), I32)],
        name="tile_tables",
    )(cnt)


def _moe_kernel(l, pop, te_ref, tc_ref, tw_ref, nact_ref, second_ref, list_ref, h_ref, gate_ref,
                w1_hbm, w3_hbm, w2_hbm, o_ref, xs_ref, ys_ref, w1_buf, w3_buf, w2_buf, w_sem):
    j = pl.program_id(0)
    nact = nact_ref[pop]
    last_tile = te_ref.shape[0] - 1
    tile_rows = MOE_TM * TOK_ROWS

    def weight_copies(expert, buf):
        return [pltpu.make_async_copy(hbm.at[l, expert], vmem.at[buf], w_sem.at[buf, k])
                for k, (hbm, vmem) in enumerate(((w1_hbm, w1_buf), (w3_hbm, w3_buf), (w2_hbm, w2_buf)))]

    def list_base(tile):
        expert = te_ref[tile]
        block = expert >> LIST_PER_BLOCK_LOG2
        in_block = expert & ((1 << LIST_PER_BLOCK_LOG2) - 1)
        return (block * LIST_ROWS + tc_ref[tile] * (MOE_TM // LIST_SPAN)) * LANES + in_block * LIST_SPAN

    def tok_rows(base, r):
        first = list_ref[base + (r // LIST_SPAN) * LANES + r % LIST_SPAN]
        return pl.ds(pl.multiple_of(first, TOK_ROWS), TOK_ROWS)

    def buf_rows(buf, r):
        return pl.ds(pl.multiple_of(buf * tile_rows + r * TOK_ROWS, TOK_ROWS), TOK_ROWS)

    def gather(tile, buf):
        base = list_base(tile)
        for r in range(MOE_TM):
            xs_ref[buf_rows(buf, r), :] = h_ref[tok_rows(base, r), :]

    tile = jnp.minimum(j, last_tile)
    plan = tw_ref[tile]
    wbuf = plan & (W_FIRST_UNIT - 1)
    starts_expert = jnp.logical_and(plan & W_FIRST_UNIT != 0, j <= last_tile)
    ahead_expert = (plan >> W_AHEAD_SHIFT) - 1
    ahead_buf = jnp.where(wbuf == 0, W_BUFFERS - 1, wbuf - 1)

    @pl.when(j == 0)
    def _():
        for cp in weight_copies(te_ref[0], 0):
            cp.start()

        @pl.when(second_ref[pop] > 0)
        def _():
            for cp in weight_copies(second_ref[pop] - 1, 1):
                cp.start()

        o_ref[...] = jnp.zeros(o_ref.shape, F32)
        ys_ref[...] = jnp.zeros(ys_ref.shape, F32)
        gather(0, 0)

    @pl.when(starts_expert)
    def _():
        for cp in weight_copies(te_ref[tile], wbuf):
            cp.wait()

        @pl.when(ahead_expert >= 0)
        def _():
            for cp in weight_copies(ahead_expert, ahead_buf):
                cp.start()

    @pl.when(j <= nact)
    def _():
        cur = j & 1
        w1_ref, w3_ref, w2_ref = w1_buf.at[wbuf], w3_buf.at[wbuf], w2_buf.at[wbuf]
        x = jnp.concatenate([xs_ref[pl.ds(cur * tile_rows + s, MOE_TM, stride=TOK_ROWS), :]
                             for s in range(TOK_ROWS)], axis=1).astype(BF16)
        gather(jnp.minimum(j + 1, last_tile), 1 - cur)

        prev_base = list_base(jnp.maximum(j - 1, 0))
        for r0 in range(0, MOE_TM, MOVE_BATCH):
            dst = [tok_rows(prev_base, r0 + k) for k in range(MOVE_BATCH)]
            vals = [o_ref[dst[k], :] + ys_ref[buf_rows(1 - cur, r0 + k), :] for k in range(MOVE_BATCH)]
            for k in reversed(range(MOVE_BATCH)):
                o_ref[dst[k], :] = vals[k]

        hg = jnp.dot(x, w1_ref[...].astype(BF16), preferred_element_type=F32)
        hu = jnp.dot(x, w3_ref[...].astype(BF16), preferred_element_type=F32)
        y = _dot(_silu(hg) * hu, w2_ref[...])

        first_row = pl.multiple_of(tc_ref[tile] * (MOE_TM // LIST_SPAN), MOE_TM // LIST_SPAN)
        spread = jnp.concatenate(
            [jnp.broadcast_to(gate_ref[pl.ds(first_row + k, 1), :], (LIST_SPAN, LIST_COLS))
             for k in range(MOE_TM // LIST_SPAN)], axis=0)
        want = te_ref[tile] * LIST_SPAN + (lax.broadcasted_iota(I32, (MOE_TM, 1), 0) & (LIST_SPAN - 1))
        hit = jnp.logical_and(lax.broadcasted_iota(I32, (MOE_TM, LIST_COLS), 1) == want, j < nact)
        y = y * jnp.sum(jnp.where(hit, spread, 0.0), axis=1, keepdims=True)
        for s in range(TOK_ROWS):
            ys_ref[pl.ds(cur * tile_rows + s, MOE_TM, stride=TOK_ROWS), :] = y[:, s * LANES:(s + 1) * LANES]


def _moe(l, pop, h_tt, te, tc, tw, nact, second, rows, gate, w1, w3, w2):
    tp = T_HALF
    blk = pop
    n_tiles = te.shape[0]
    hbm = pl.BlockSpec(memory_space=pl.ANY)
    grid_spec = pltpu.PrefetchScalarGridSpec(
        num_scalar_prefetch=6,
        grid=(n_tiles + 1,),
        in_specs=[
            pl.BlockSpec((tp * TOK_ROWS, LANES), lambda j, *_: (blk, 0), pipeline_mode=pl.Buffered(1)),
            pl.BlockSpec((LIST_ROWS, LIST_COLS), lambda j, *_: (0, 0), pipeline_mode=pl.Buffered(1)),
            hbm, hbm, hbm,
        ],
        out_specs=pl.BlockSpec((tp * TOK_ROWS, LANES), lambda j, *_: (0, 0), pipeline_mode=pl.Buffered(1)),
        scratch_shapes=[pltpu.VMEM((2 * MOE_TM * TOK_ROWS, LANES), F32),
                        pltpu.VMEM((2 * MOE_TM * TOK_ROWS, LANES), F32),
                        pltpu.VMEM((W_BUFFERS, D, EXPERT_HIDDEN), F32),
                        pltpu.VMEM((W_BUFFERS, D, EXPERT_HIDDEN), F32),
                        pltpu.VMEM((W_BUFFERS, EXPERT_HIDDEN, D), F32),
                        pltpu.SemaphoreType.DMA((W_BUFFERS, 3))],
    )
    return pl.pallas_call(
        functools.partial(_moe_kernel, l, pop),
        grid_spec=grid_spec,
        out_shape=jax.ShapeDtypeStruct((tp * TOK_ROWS, LANES), F32),
        compiler_params=_params(),
        name="moe",
    )(te, tc, tw, nact, second, rows, h_tt, gate, w1, w3, w2)


def _moe_all(l, h_tt, cnt, rows_c, gate_c, rows_l, gate_l, w1, w3, w2):
    te_c, tc_c, tw_c, te_l, tc_l, tw_l, nact, second = _tile_tables(cnt)
    out_c = _moe(l, 0, h_tt, te_c, tc_c, tw_c, nact, second, rows_c.reshape(-1), gate_c, w1, w3, w2)
    out_l = _moe(l, 1, h_tt, te_l, tc_l, tw_l, nact, second, rows_l.reshape(-1), gate_l, w1, w3, w2)
    return out_c, out_l


def _final_kernel(l, x1_ref, moec_ref, moel_ref, mod_ref, g_ref, oc_ref, ol_ref):
    i = pl.program_id(0)
    r = _mod_row(i)

    def body(half, pop):
        x = _after_moe(l, r, x1_ref, (moec_ref, moel_ref)[half], mod_ref)
        ms = jnp.mean(x * x, axis=-1, keepdims=True)
        (oc_ref, ol_ref)[pop][...] = x * lax.rsqrt(ms + EPS) * g_ref[...]

    pl.when(i < MOE_SPLIT)(functools.partial(body, 0, 0))
    pl.when(jnp.logical_and(i >= MOE_SPLIT, i < NT_CTX))(functools.partial(body, 1, 0))
    pl.when(i >= NT_CTX)(functools.partial(body, 1, 1))


def _final(l, x1, moe_c, moe_l, mod, g):
    return pl.pallas_call(
        functools.partial(_final_kernel, l),
        grid=(NT,),
        in_specs=[_tile_spec(D)] + _half_specs(TM * TOK_ROWS, LANES) + [_MOD_SPEC, _full_spec((1, D))],
        out_specs=_pop_specs(TM, D),
        out_shape=[jax.ShapeDtypeStruct((T_CTX, D), F32), jax.ShapeDtypeStruct((T_LAT, D), F32)],
        compiler_params=_params(),
        name="final_norm",
    )(x1, moe_c, moe_l, mod, g)


def _rope_tables():
    t = jnp.arange(LAT_LEN)
    row = (t // GRID_W).astype(F32)
    col = (t % GRID_W).astype(F32)
    inv = ROPE_BASE ** (-jnp.arange(0, ROPE_AXIS_DIM, 2, dtype=F32) / ROPE_AXIS_DIM)

    def table(p):
        ang = p[:, None] * inv[None, :]
        ang = jnp.concatenate([ang, ang], axis=-1)
        return jnp.cos(ang), jnp.sin(ang)

    (cr, sr), (cc, sc) = table(row), table(col)
    cos = jnp.concatenate([cr, cc, cr, cc], axis=-1)
    sin = jnp.concatenate([sr, sc, sr, sc], axis=-1)
    return cos, sin


def _router_weights(w_grp, w_exp):
    we = jnp.transpose(w_exp, (0, 2, 1, 3)).reshape(DEPTH, D, N_GROUPS * EPG)
    pad = jnp.zeros((DEPTH, D, LANES - N_GROUPS - N_GROUPS * EPG), F32)
    wr = jnp.concatenate([w_grp, we, pad], axis=-1)
    hi = lax.bitcast_convert_type(lax.bitcast_convert_type(wr, jnp.uint32) & jnp.uint32(0xFFFF0000), F32)
    return jnp.stack([hi.astype(BF16), (wr - hi).astype(BF16)], axis=1)


def kernel(x_prompt, x_sample, cache_k, cache_v, c, c_ctx, w_ada, b_ada, norm_mix_g, norm_ffn_g, w_in_ab, pool_w, pool_scale, attn_sink, w_out_ab, conv_w1, conv_b1, conv_dw, conv_dw_b, conv_ln_g, conv_ln_b, conv_w2, conv_b2, router_grp, router_exp, moe_w1, moe_w3, moe_w2, final_g):
    xc = x_prompt.reshape(T_CTX, D)
    xl = x_sample.reshape(T_LAT, D)
    cond8 = jnp.concatenate([c_ctx[None, :], c, jnp.zeros((SUBLANES - 1 - N_LAT_SEQ, D), F32)], axis=0)
    mod = _modulation(cond8, w_ada, b_ada)
    cos_t, sin_t = _rope_tables()
    wr = _router_weights(router_grp, router_exp)
    n_even, n_odd = w_in_ab.shape[0], conv_w1.shape[0]
    past = cache_k.shape[2]

    a, q, k_c, k_l, v_c, v_l = _in_even(0, 0, xc, xl, mod, norm_mix_g, w_in_ab, cos_t, sin_t)
    yb_c = _attn_ctx(0, attn_sink, q, k_c, v_c)
    yb_l = _attn_lat(0, attn_sink, q, k_l, v_l, cache_k[:, 0].reshape(N_LAT_SEQ, past, KV_DIM),
                     cache_v[:, 0].reshape(N_LAT_SEQ, past, KV_DIM))
    x1, h2, *routing = _mid_even(0, 0, a, yb_c, yb_l, xc, xl, mod, pool_w,
                                 pool_scale.reshape(n_even, 1, POOL_DIM), w_out_ab, norm_ffn_g, wr)
    moe0 = _moe_all(0, h2, *routing, moe_w1, moe_w3, moe_w2)

    vec = lambda p: p.reshape(n_odd, 1, -1)
    u = _in_odd(1, 0, x1, *moe0, mod, norm_mix_g, conv_w1, vec(conv_b1))
    x3, h2b, *routing = _mid_odd(1, 0, u, x1, *moe0, mod, conv_dw, vec(conv_dw_b), vec(conv_ln_g),
                                 vec(conv_ln_b), conv_w2, vec(conv_b2), norm_ffn_g, wr)
    moe1 = _moe_all(1, h2b, *routing, moe_w1, moe_w3, moe_w2)
    y_c, y_l = _final(1, x3, *moe1, mod, final_g.reshape(1, D))

    y_prompt = y_c.reshape(N_CTX_SEQ, CTX_LEN, D)
    y_sample = y_l.reshape(N_LAT_SEQ, LAT_LEN, D)
    state_k = k_c.reshape(N_CTX_SEQ, 1, CTX_LEN, N_KV_HEADS, HEAD_DIM)
    state_v = v_c.reshape(N_CTX_SEQ, 1, CTX_LEN, N_KV_HEADS, HEAD_DIM)
    return (y_prompt, y_sample, state_k, state_v)
```

```python
import functools

import jax
import jax.numpy as jnp
import numpy as np
from jax import lax
from jax.experimental import pallas as pl
from jax.experimental.pallas import tpu as pltpu

F32 = jnp.float32
BF16 = jnp.bfloat16
I32 = jnp.int32

D = 1024
N_CTX_SEQ = 16
CTX_LEN = 256
N_LAT_SEQ = 2
LAT_LEN = 1024
T_CTX = N_CTX_SEQ * CTX_LEN
T_LAT = N_LAT_SEQ * LAT_LEN
T_ALL = T_CTX + T_LAT
TM = 256
NT = T_ALL // TM
NT_CTX = T_CTX // TM
LAT_TILES_PER_SEQ = LAT_LEN // TM
GRID_W = 64
DEPTH = 2

POOL_WINDOWS = (2, 4, 8, 16)
POOL_GROUP_DIM = 128
POOL_DIM = 512
HEAD_DIM = 64
N_Q_HEADS = 8
N_KV_HEADS = 2
Q_PER_KV = 4
Q_DIM = 512
KV_DIM = 128
IN_AB = 1280
ATTN_WINDOW = 128
ATTN_BLOCK = 128
ATTN_SCALE = HEAD_DIM ** -0.5
ROPE_BASE = 10000.0
ROPE_AXIS_DIM = 32
CONV_WIDTH = 31
CONV_HALO = 16
POOL_HALO = 8
N_GROUPS = 4
EPG = 4
N_EXPERTS = 16
EXPERT_HIDDEN = 512
EPS = 1e-6
NEG_BIG = -1e30

SUBLANES = 8
LANES = 128
TOK_ROWS = D // LANES

MOE_TM = 256
MOE_SPLIT = NT // 2
T_HALF = MOE_SPLIT * TM
MOE_TILES = 2 * T_HALF // MOE_TM + N_EXPERTS
W_BUFFERS = 3
W_FIRST_UNIT = 4
W_AHEAD_SHIFT = 3
MOVE_BATCH_LOG2 = 2
MOVE_BATCH = 1 << MOVE_BATCH_LOG2
VMEM_LIMIT = 56 * 1024 * 1024


def _silu(x):
    return x * (1.0 / (1.0 + jnp.exp(-x)))


def _mod_row(i):
    return jnp.where(i < NT_CTX, 0, 1 + (i - NT_CTX) // LAT_TILES_PER_SEQ)


def _seq_info(i):
    is_ctx = i < NT_CTX
    k = (i - NT_CTX) % LAT_TILES_PER_SEQ
    off = jnp.where(is_ctx, 0, k * TM)
    n = jnp.where(is_ctx, CTX_LEN, LAT_LEN)
    first = jnp.logical_or(is_ctx, k == 0)
    last = jnp.logical_or(is_ctx, k == LAT_TILES_PER_SEQ - 1)
    return off, n, first, last


def _rms_mod(x, g, scale, shift):
    ms = jnp.mean(x * x, axis=-1, keepdims=True)
    return (x * lax.rsqrt(ms + EPS) * g) * (1.0 + scale) + shift


def _dot(a, b):
    return jnp.dot(a.astype(BF16), b.astype(BF16), preferred_element_type=F32)


def _dot_nt(a, b):
    return lax.dot_general(a.astype(BF16), b.astype(BF16), (((1,), (1,)), ((), ())),
                           preferred_element_type=F32)


def _from_token_tiles(ref):
    return jnp.concatenate([ref[pl.ds(s, TM, stride=TOK_ROWS), :] for s in range(TOK_ROWS)], axis=1)


def _tile_spec(width):
    return pl.BlockSpec((TM, width), lambda i: (i, 0))


def _full_spec(shape):
    return pl.BlockSpec(shape, lambda i: (0,) * len(shape))


def _layer_spec(shape, l):
    return pl.BlockSpec((None,) + tuple(shape), lambda i: (l,) + (0,) * len(shape))


def _pop_specs(rows, width):
    return [pl.BlockSpec((rows, width), lambda i: (jnp.minimum(i, NT_CTX - 1), 0)),
            pl.BlockSpec((rows, width), lambda i: (jnp.maximum(i - NT_CTX, 0), 0))]


def _per_population(i, body):
    pl.when(i < NT_CTX)(functools.partial(body, 0))
    pl.when(i >= NT_CTX)(functools.partial(body, 1))


def _half_specs(rows, width):
    return [pl.BlockSpec((rows, width), lambda i: (jnp.minimum(i, MOE_SPLIT - 1), 0)),
            pl.BlockSpec((rows, width), lambda i: (jnp.maximum(i - MOE_SPLIT, 0), 0))]


def _per_half(i, body):
    pl.when(i < MOE_SPLIT)(functools.partial(body, 0))
    pl.when(i >= MOE_SPLIT)(functools.partial(body, 1))


_MOD_SPEC = pl.BlockSpec((DEPTH, SUBLANES, 6 * D), lambda i: (0, 0, 0))


def _params():
    return pltpu.CompilerParams(vmem_limit_bytes=VMEM_LIMIT)


MOD_TN = 1536


def _mod_kernel(cond_ref, w_ref, b_ref, o_ref):
    s = _silu(cond_ref[...])
    o_ref[0] = _dot(s, w_ref[0]) + b_ref[0]


def _modulation(cond8, w_ada, b_ada):
    return pl.pallas_call(
        _mod_kernel,
        grid=(DEPTH, 6 * D // MOD_TN),
        in_specs=[
            pl.BlockSpec((SUBLANES, D), lambda l, n: (0, 0)),
            pl.BlockSpec((1, D, MOD_TN), lambda l, n: (l, 0, n)),
            pl.BlockSpec((1, 1, MOD_TN), lambda l, n: (l, 0, n)),
        ],
        out_specs=pl.BlockSpec((1, SUBLANES, MOD_TN), lambda l, n: (l, 0, n)),
        out_shape=jax.ShapeDtypeStruct((DEPTH, SUBLANES, 6 * D), F32),
        compiler_params=_params(),
        name="modulation",
    )(cond8, w_ada, b_ada.reshape(DEPTH, 1, 6 * D))


def _rope_chunk(xc, cos, sin):
    lane = lax.broadcasted_iota(I32, xc.shape, 1)
    first = (lane % ROPE_AXIS_DIM) < (ROPE_AXIS_DIM // 2)
    rot = jnp.where(first, -pltpu.roll(xc, LANES - ROPE_AXIS_DIM // 2, 1), pltpu.roll(xc, ROPE_AXIS_DIM // 2, 1))
    return xc * cos + rot * sin


def _in_even_kernel(l, xc_ref, xl_ref, mod_ref, g_ref, w_ref, cos_ref, sin_ref,
                    a_ref, q_ref, kc_ref, kl_ref, vc_ref, vl_ref):
    i = pl.program_id(0)
    r = _mod_row(i)

    def body(pop):
        x = (xc_ref, xl_ref)[pop][...]
        k_ref, v_ref = (kc_ref, kl_ref)[pop], (vc_ref, vl_ref)[pop]
        shift = mod_ref[l, pl.ds(r, 1), 0:D]
        scale = mod_ref[l, pl.ds(r, 1), D:2 * D]
        h = _rms_mod(x, g_ref[l:l + 1, :], scale, shift)
        proj = _dot(h, w_ref[...])
        a_ref[...] = proj[:, :POOL_DIM]
        v_ref[...] = proj[:, POOL_DIM + Q_DIM + KV_DIM:]
        if pop == 0:
            q_ref[...] = proj[:, POOL_DIM:POOL_DIM + Q_DIM].astype(BF16)
            k_ref[...] = proj[:, POOL_DIM + Q_DIM:POOL_DIM + Q_DIM + KV_DIM]
        else:
            off, _, _, _ = _seq_info(i)
            off = pl.multiple_of(off, TM)
            cos = cos_ref[pl.ds(off, TM), :]
            sin = sin_ref[pl.ds(off, TM), :]
            for c in range(Q_DIM // LANES):
                lo = POOL_DIM + c * LANES
                q_ref[:, c * LANES:(c + 1) * LANES] = _rope_chunk(proj[:, lo:lo + LANES], cos, sin).astype(BF16)
            lo = POOL_DIM + Q_DIM
            k_ref[...] = _rope_chunk(proj[:, lo:lo + LANES], cos, sin)

    _per_population(i, body)


def _in_even(l, e, xc, xl, mod, g, w_in, cos_t, sin_t):
    return pl.pallas_call(
        functools.partial(_in_even_kernel, l),
        grid=(NT,),
        in_specs=_pop_specs(TM, D) + [_MOD_SPEC, _full_spec((DEPTH, D)), _layer_spec((D, IN_AB), e),
                  _full_spec((LAT_LEN, LANES)), _full_spec((LAT_LEN, LANES))],
        out_specs=[_tile_spec(POOL_DIM), _tile_spec(Q_DIM)] + _pop_specs(TM, KV_DIM) + _pop_specs(TM, KV_DIM),
        out_shape=[jax.ShapeDtypeStruct((T_ALL, POOL_DIM), F32), jax.ShapeDtypeStruct((T_ALL, Q_DIM), BF16)]
                  + [jax.ShapeDtypeStruct((t, KV_DIM), F32) for t in (T_CTX, T_LAT, T_CTX, T_LAT)],
        compiler_params=_params(),
        name="in_even",
    )(xc, xl, mod, g, w_in, cos_t, sin_t)


def _attn_ctx_kernel(e, sink_ref, q_ref, k_ref, v_ref, o_ref):
    q = q_ref[...]
    k = k_ref[...]
    v = v_ref[...]
    for h in range(N_KV_HEADS):
        kh = k[:, h * HEAD_DIM:(h + 1) * HEAD_DIM]
        vh = v[:, h * HEAD_DIM:(h + 1) * HEAD_DIM]
        for g in range(Q_PER_KV):
            j = h * Q_PER_KV + g
            s = _dot_nt(q[:, j * HEAD_DIM:(j + 1) * HEAD_DIM], kh) * ATTN_SCALE
            sk = sink_ref[e, j]
            m = jnp.maximum(jnp.max(s, axis=-1, keepdims=True), sk)
            p = jnp.exp(s - m)
            denom = jnp.sum(p, axis=-1, keepdims=True) + jnp.exp(sk - m)
            o_ref[:, j * HEAD_DIM:(j + 1) * HEAD_DIM] = (_dot(p, vh) / denom).astype(o_ref.dtype)


def _attn_ctx(e, sink, q, k, v):
    tile = lambda w: pl.BlockSpec((TM, w), lambda b: (b, 0))
    return pl.pallas_call(
        functools.partial(_attn_ctx_kernel, e),
        grid=(N_CTX_SEQ,),
        in_specs=[pl.BlockSpec(memory_space=pltpu.SMEM), tile(Q_DIM), tile(KV_DIM), tile(KV_DIM)],
        out_specs=tile(Q_DIM),
        out_shape=jax.ShapeDtypeStruct((T_CTX, Q_DIM), BF16),
        compiler_params=_params(),
        name="attn_ctx",
    )(sink, q, k, v)


LAT_BLOCKS = LAT_LEN // ATTN_BLOCK
BAND = 3 * ATTN_BLOCK


def _attn_lat_kernel(e, sink_ref, q_ref, k_ref, v_ref, ck_ref, cv_ref, o_ref):
    n = pl.program_id(1)
    start = jnp.clip((n - 1) * ATTN_BLOCK, 0, LAT_LEN - BAND)
    start = pl.multiple_of(start, ATTN_BLOCK)
    q = q_ref[...]
    kw = k_ref[pl.ds(start, BAND), :]
    vw = v_ref[pl.ds(start, BAND), :]
    qpos = n * ATTN_BLOCK + lax.broadcasted_iota(I32, (ATTN_BLOCK, BAND), 0)
    kpos = start + lax.broadcasted_iota(I32, (ATTN_BLOCK, BAND), 1)
    valid = jnp.abs(qpos - kpos) <= ATTN_WINDOW
    for h in range(N_KV_HEADS):
        hs = slice(h * HEAD_DIM, (h + 1) * HEAD_DIM)
        ck = ck_ref[0, 0, :, h, :]
        cv = cv_ref[0, 0, :, h, :]
        for g in range(Q_PER_KV):
            j = h * Q_PER_KV + g
            qj = q[:, j * HEAD_DIM:(j + 1) * HEAD_DIM]
            s_loc = jnp.where(valid, _dot_nt(qj, kw[:, hs]) * ATTN_SCALE, NEG_BIG)
            s_ctx = _dot_nt(qj, ck) * ATTN_SCALE
            sk = sink_ref[e, j]
            m = jnp.maximum(jnp.maximum(jnp.max(s_loc, axis=-1, keepdims=True),
                                        jnp.max(s_ctx, axis=-1, keepdims=True)), sk)
            p_loc = jnp.exp(s_loc - m)
            p_ctx = jnp.exp(s_ctx - m)
            denom = (jnp.sum(p_loc, axis=-1, keepdims=True) + jnp.sum(p_ctx, axis=-1, keepdims=True)
                     + jnp.exp(sk - m))
            o = _dot(p_loc, vw[:, hs]) + _dot(p_ctx, cv)
            o_ref[:, j * HEAD_DIM:(j + 1) * HEAD_DIM] = (o / denom).astype(o_ref.dtype)


def _attn_lat(e, sink, q, k, v, ck, cv):
    past = ck.shape[2]
    cache_spec = pl.BlockSpec((1, 1, past, N_KV_HEADS, HEAD_DIM), lambda b, n: (b, e, 0, 0, 0))
    q_blk0 = T_CTX // ATTN_BLOCK
    q_spec = pl.BlockSpec((ATTN_BLOCK, Q_DIM), lambda b, n: (q_blk0 + b * LAT_BLOCKS + n, 0))
    return pl.pallas_call(
        functools.partial(_attn_lat_kernel, e),
        grid=(N_LAT_SEQ, LAT_BLOCKS),
        in_specs=[pl.BlockSpec(memory_space=pltpu.SMEM),
                  q_spec,
                  pl.BlockSpec((LAT_LEN, KV_DIM), lambda b, n: (b, 0)),
                  pl.BlockSpec((LAT_LEN, KV_DIM), lambda b, n: (b, 0)),
                  cache_spec, cache_spec],
        out_specs=pl.BlockSpec((ATTN_BLOCK, Q_DIM), lambda b, n: (b * LAT_BLOCKS + n, 0)),
        out_shape=jax.ShapeDtypeStruct((T_LAT, Q_DIM), BF16),
        compiler_params=_params(),
        name="attn_lat",
    )(sink, q, k, v, ck, cv)


def _router(h2, wr_ref):
    hi = h2.astype(BF16)
    lo = (h2 - hi.astype(F32)).astype(BF16)
    whi = wr_ref[0]
    wlo = wr_ref[1]
    dot = functools.partial(jnp.dot, preferred_element_type=F32)
    logits_t = (dot(hi, whi) + (dot(lo, whi) + dot(hi, wlo))).T
    row = lambda k: logits_t[k:k + 1, :]

    def first_max(vals):
        best = functools.reduce(jnp.maximum, vals)
        idx = jnp.full(best.shape, len(vals) - 1, F32)
        for k in reversed(range(len(vals) - 1)):
            idx = jnp.where(vals[k] == best, float(k), idx)
        return best, idx

    grp = [row(k) for k in range(N_GROUPS)]
    gmax, g_idx = first_max(grp)
    g_w = 1.0 / sum(jnp.exp(gk - gmax) for gk in grp)
    el = []
    for e in range(EPG):
        v = row(N_GROUPS + (N_GROUPS - 1) * EPG + e)
        for k in reversed(range(N_GROUPS - 1)):
            v = jnp.where(g_idx == k, row(N_GROUPS + k * EPG + e), v)
        el.append(v)
    t1, i1 = first_max(el)
    t2, i2 = first_max([jnp.where(i1 == e, -jnp.inf, el[e]) for e in range(EPG)])
    d = jnp.exp(t2 - t1)
    w1 = g_w / (1.0 + d)
    w2 = g_w * d / (1.0 + d)
    return EPG * g_idx + i1, EPG * g_idx + i2, w1, w2


LIST_SPAN_LOG2 = 5
LIST_SPAN = 1 << LIST_SPAN_LOG2
LIST_ROWS = LANES
LIST_COLS = N_EXPERTS * LIST_SPAN
LIST_BLOCKS = LIST_COLS // LANES
LIST_PER_BLOCK_LOG2 = 2


N_TAIL_OUT = 7
N_TAIL_SCRATCH = 3


def _mixer_tail(l, x, y, mod_ref, r, gffn_ref, wr_ref, outs, scratch):
    x1_ref, h2_ref, cnt_ref, lc_ref, gc_ref, ll_ref, gl_ref = outs
    run_ref, acc_ref, gacc_ref = scratch
    i = pl.program_id(0)
    g1 = mod_ref[l, pl.ds(r, 1), 2 * D:3 * D]
    shift2 = mod_ref[l, pl.ds(r, 1), 3 * D:4 * D]
    scale2 = mod_ref[l, pl.ds(r, 1), 4 * D:5 * D]
    x1 = x + g1 * y
    x1_ref[...] = x1
    h2 = _rms_mod(x1, gffn_ref[l:l + 1, :], scale2, shift2)
    for s in range(TOK_ROWS):
        h2_ref[pl.ds(s, TM, stride=TOK_ROWS), :] = h2[:, s * LANES:(s + 1) * LANES]
    e1, e2, w1_row, w2_row = _router(h2, wr_ref)

    @pl.when(jnp.logical_or(i == 0, i == MOE_SPLIT))
    def _():
        run_ref[...] = jnp.zeros(run_ref.shape, F32)
        acc_ref[...] = jnp.zeros(acc_ref.shape, F32)
        gacc_ref[...] = jnp.zeros(gacc_ref.shape, F32)

    @pl.when(i == 0)
    def _():
        cnt_ref[...] = jnp.zeros(cnt_ref.shape, I32)

    sub_i = lax.broadcasted_iota(I32, (LANES, TM), 0)
    sub = sub_i.astype(F32)
    member = jnp.where(jnp.logical_or(sub == e1, sub == e2), 1.0, 0.0)
    tri = jnp.where(lax.broadcasted_iota(I32, (TM, TM), 0) <= lax.broadcasted_iota(I32, (TM, TM), 1), 1.0, 0.0)
    csum = jnp.dot(member.astype(BF16), tri.astype(BF16), preferred_element_type=F32)
    run = run_ref[:, 0:1]
    before = csum - member + run
    run_new = run + csum[:, TM - 1:TM]
    run_ref[...] = jnp.broadcast_to(run_new, run_ref.shape)

    col_i = lax.broadcasted_iota(I32, (TM, LIST_COLS), 1)
    rows_oh, col_rows = [], []
    for e in (e1, e2):
        rank = jnp.zeros((1, TM), F32)
        for k in range(N_EXPERTS):
            rank = rank + jnp.where(e == k, before[k:k + 1, :], 0.0)
        rank_i = rank.astype(I32)
        rows_oh.append(jnp.where(sub_i == (rank_i >> LIST_SPAN_LOG2), 1.0, 0.0))
        col_rows.append(e.astype(I32) * LIST_SPAN + (rank_i & (LIST_SPAN - 1)))
    rows_t = jnp.concatenate(rows_oh, axis=1)
    col_cols = jnp.concatenate(col_rows + [jnp.zeros((LANES - 2, TM), I32)], axis=0).T
    cols = jnp.concatenate([jnp.where(col_i == col_cols[:, k:k + 1], 1.0, 0.0).astype(BF16) for k in range(2)],
                           axis=0)
    tok = (lax.broadcasted_iota(I32, (1, 2 * TM), 1) % TM).astype(F32)
    rest = jnp.concatenate([w1_row, w2_row], axis=1)
    values = [None, tok]
    for _ in range(3):
        piece = rest.astype(BF16).astype(F32)
        values.append(piece)
        rest = rest - piece
    planes = jnp.concatenate([rows_t if v is None else rows_t * v for v in values], axis=0).astype(BF16)
    out = jnp.dot(planes, cols, preferred_element_type=F32)
    hits, tok_sum, gp0, gp1, gp2 = [out[k * LANES:(k + 1) * LANES] for k in range(len(values))]
    tile_in_pop = jnp.where(i < MOE_SPLIT, i, i - MOE_SPLIT).astype(F32)
    acc = acc_ref[...] + TOK_ROWS * (tok_sum + TM * tile_in_pop * hits)
    acc_ref[...] = acc
    gacc = gacc_ref[...] + gp0 + gp1 + gp2
    gacc_ref[...] = gacc

    def counts_row():
        return jnp.broadcast_to(run_new, (LANES, LANES)).T[0:1, :].astype(I32)

    def put_rows(dst_ref):
        for b in range(LIST_BLOCKS):
            dst_ref[b * LIST_ROWS:(b + 1) * LIST_ROWS, :] = acc[:, b * LANES:(b + 1) * LANES].astype(I32)

    @pl.when(i == MOE_SPLIT - 1)
    def _():
        cnt_ref[0:1, :] = counts_row()
        put_rows(lc_ref)
        gc_ref[...] = gacc

    @pl.when(i == NT - 1)
    def _():
        cnt_ref[1:2, :] = counts_row()
        put_rows(ll_ref)
        gl_ref[...] = gacc


_LIST_OUT_SHAPES = [(LIST_BLOCKS * LIST_ROWS, LANES), (LIST_ROWS, LIST_COLS)] * 2
_TAIL_OUT_SHAPES = [jax.ShapeDtypeStruct((T_ALL, D), F32),
                    jax.ShapeDtypeStruct((T_ALL * TOK_ROWS, LANES), F32),
                    jax.ShapeDtypeStruct((SUBLANES, LANES), I32)] + [
                        jax.ShapeDtypeStruct(s, dt) for s, dt in zip(_LIST_OUT_SHAPES, (I32, F32, I32, F32))]


def _tail_scratch():
    return [pltpu.VMEM((LANES, LANES), F32), pltpu.VMEM((LIST_ROWS, LIST_COLS), F32),
            pltpu.VMEM((LIST_ROWS, LIST_COLS), F32)]


def _tail_out_specs():
    return [_tile_spec(D),
            pl.BlockSpec((TM * TOK_ROWS, LANES), lambda i: (i, 0)),
            _full_spec((SUBLANES, LANES))] + [_full_spec(s) for s in _LIST_OUT_SHAPES]


def _halo_specs(halo, width):
    per = TM // halo
    last = T_ALL // halo - 1
    prev = pl.BlockSpec((halo, width), lambda i: (jnp.maximum(i * per - 1, 0), 0))
    nxt = pl.BlockSpec((halo, width), lambda i: (jnp.minimum((i + 1) * per, last), 0))
    return prev, nxt


def _mid_even_kernel(l, a_ref, ap_ref, an_ref, ybc_ref, ybl_ref, xc_ref, xl_ref, mod_ref, wp_ref, ps_ref, wo_ref,
                     gffn_ref, wr_ref, *refs):
    outs, (pad_ref, pool_ref), scratch = refs[:N_TAIL_OUT], refs[N_TAIL_OUT:-N_TAIL_SCRATCH], refs[-N_TAIL_SCRATCH:]
    i = pl.program_id(0)
    r = _mod_row(i)
    off, n, first, last = _seq_info(i)
    a = a_ref[...]
    ap = jnp.where(first, 0.0, ap_ref[...])
    an = jnp.where(last, 0.0, an_ref[...])
    half_rows = TM // 2
    pos = [(off + 2 * lax.broadcasted_iota(I32, (half_rows, 1), 0) + parity).astype(F32) for parity in (0, 1)]
    last_pos = (n - 1).astype(F32)
    mixed = []
    for g, w in enumerate(POOL_WINDOWS):
        lo = w // 2
        hi = w - lo - 1
        cols = slice(g * POOL_GROUP_DIM, (g + 1) * POOL_GROUP_DIM)
        pad_ref[g, 0:POOL_HALO, :] = ap[:, cols]
        pad_ref[g, POOL_HALO:POOL_HALO + TM, :] = a[:, cols]
        pad_ref[g, POOL_HALO + TM:, :] = an[:, cols]
        loads = {s: pad_ref[g, pl.ds(s, half_rows, stride=2), :]
                 for s in range(POOL_HALO - lo, POOL_HALO + hi + 2)}
        for parity in (0, 1):
            total = functools.reduce(lambda x, y: x + y,
                                     [loads[POOL_HALO + parity + j] for j in range(-lo, hi + 1)])
            count = jnp.minimum(pos[parity] + hi, last_pos) - jnp.maximum(pos[parity] - lo, 0.0) + 1.0
            pool_ref[g, pl.ds(parity, half_rows, stride=2), :] = total / count - loads[POOL_HALO + parity]
        mixed.append(_dot(pool_ref[g], wp_ref[g]))
    ya = jnp.concatenate(mixed, axis=1) * ps_ref[...]
    y_pool = _dot(ya, wo_ref[0:POOL_DIM, :])

    def body(pop):
        y = y_pool + _dot((ybc_ref, ybl_ref)[pop][...], wo_ref[POOL_DIM:, :])
        _mixer_tail(l, (xc_ref, xl_ref)[pop][...], y, mod_ref, r, gffn_ref, wr_ref, outs, scratch)

    _per_population(i, body)


def _mid_even(l, e, a, yb_c, yb_l, xc, xl, mod, w_pool, pool_scale, w_out, gffn, wr):
    prev, nxt = _halo_specs(POOL_HALO, POOL_DIM)
    return pl.pallas_call(
        functools.partial(_mid_even_kernel, l),
        grid=(NT,),
        in_specs=[_tile_spec(POOL_DIM), prev, nxt] + _pop_specs(TM, Q_DIM) + _pop_specs(TM, D) + [
                  _MOD_SPEC, _layer_spec((len(POOL_WINDOWS), POOL_GROUP_DIM, POOL_GROUP_DIM), e),
                  _layer_spec((1, POOL_DIM), e), _layer_spec((D, D), e), _full_spec((DEPTH, D)),
                  _layer_spec((2, D, LANES), l)],
        out_specs=_tail_out_specs(),
        out_shape=_TAIL_OUT_SHAPES,
        scratch_shapes=[pltpu.VMEM((len(POOL_WINDOWS), TM + 2 * POOL_HALO, POOL_GROUP_DIM), F32),
                        pltpu.VMEM((len(POOL_WINDOWS), TM, POOL_GROUP_DIM), F32)] + _tail_scratch(),
        compiler_params=_params(),
        name="mid_even",
    )(a, a, a, yb_c, yb_l, xc, xl, mod, w_pool, pool_scale, w_out, gffn, wr)


def _after_moe(l, r, x1_ref, moe_ref, mod_ref):
    return x1_ref[...] + mod_ref[l, pl.ds(r, 1), 5 * D:6 * D] * _from_token_tiles(moe_ref)


def _in_odd_kernel(l, x1_ref, moec_ref, moel_ref, mod_ref, g_ref, w1_ref, b1_ref, u_ref):
    i = pl.program_id(0)
    r = _mod_row(i)

    def body(pop):
        x2 = _after_moe(l - 1, r, x1_ref, (moec_ref, moel_ref)[pop], mod_ref)
        shift = mod_ref[l, pl.ds(r, 1), 0:D]
        scale = mod_ref[l, pl.ds(r, 1), D:2 * D]
        h = _rms_mod(x2, g_ref[l:l + 1, :], scale, shift)
        u = _dot(h, w1_ref[...]) + b1_ref[...]
        u_ref[...] = u[:, :D] * (1.0 / (1.0 + jnp.exp(-u[:, D:])))

    _per_half(i, body)


def _in_odd(l, o, x1, moe_c, moe_l, mod, g, w1, b1):
    return pl.pallas_call(
        functools.partial(_in_odd_kernel, l),
        grid=(NT,),
        in_specs=[_tile_spec(D)] + _half_specs(TM * TOK_ROWS, LANES) + [
                  _MOD_SPEC, _full_spec((DEPTH, D)), _layer_spec((D, 2 * D), o), _layer_spec((1, 2 * D), o)],
        out_specs=_tile_spec(D),
        out_shape=jax.ShapeDtypeStruct((T_ALL, D), F32),
        compiler_params=_params(),
        name="in_odd",
    )(x1, moe_c, moe_l, mod, g, w1, b1)


CONV_PAD_ROWS = TM + 2 * CONV_HALO
CONV_CHUNKS = D // LANES
CONV_BLOCK = 64


def _mid_odd_kernel(l, u_ref, up_ref, un_ref, x1_ref, moec_ref, moel_ref, mod_ref, dw_ref, dwb_ref, lng_ref, lnb_ref,
                    w2_ref, b2_ref, gffn_ref, wr_ref, *refs):
    outs, scratch = refs[:N_TAIL_OUT], refs[-N_TAIL_SCRATCH:]
    pad_ref, conv_ref, x_ref = refs[N_TAIL_OUT:-N_TAIL_SCRATCH]
    i = pl.program_id(0)

    def residual(pop):
        x_ref[...] = _after_moe(l - 1, _mod_row(i), x1_ref, (moec_ref, moel_ref)[pop], mod_ref)

    _per_half(i, residual)
    r = _mod_row(i)
    _, _, first, last = _seq_info(i)
    u = u_ref[...]
    up = jnp.where(first, 0.0, up_ref[...])
    un = jnp.where(last, 0.0, un_ref[...])
    half = CONV_WIDTH // 2
    first_row = CONV_HALO - half
    for c in range(CONV_CHUNKS):
        cols = slice(c * LANES, (c + 1) * LANES)
        pad_ref[c, 0:CONV_HALO, :] = up[:, cols]
        pad_ref[c, CONV_HALO:CONV_HALO + TM, :] = u[:, cols]
        pad_ref[c, CONV_HALO + TM:, :] = un[:, cols]
        bias = jnp.broadcast_to(dwb_ref[:, cols], (CONV_BLOCK, LANES))
        for r0 in range(0, TM, 2 * CONV_BLOCK):
            acc_even, acc_odd = bias, bias
            for s in range(first_row, first_row + CONV_WIDTH + 1):
                win = pad_ref[c, pl.ds(r0 + s, CONV_BLOCK, stride=2), :]
                j = s - first_row
                if j < CONV_WIDTH:
                    acc_even = acc_even + win * dw_ref[j:j + 1, cols]
                if j >= 1:
                    acc_odd = acc_odd + win * dw_ref[j - 1:j, cols]
            conv_ref[c, pl.ds(r0, CONV_BLOCK, stride=2), :] = acc_even
            conv_ref[c, pl.ds(r0 + 1, CONV_BLOCK, stride=2), :] = acc_odd
    acc = jnp.concatenate([conv_ref[c] for c in range(CONV_CHUNKS)], axis=1)
    mu = jnp.mean(acc, axis=-1, keepdims=True)
    cen = acc - mu
    var = jnp.mean(cen * cen, axis=-1, keepdims=True)
    v = _silu(cen * lax.rsqrt(var + EPS) * lng_ref[...] + lnb_ref[...])
    y = _dot(v, w2_ref[...]) + b2_ref[...]
    _mixer_tail(l, x_ref[...], y, mod_ref, r, gffn_ref, wr_ref, outs, scratch)


def _mid_odd(l, o, u, x1, moe_c, moe_l, mod, dw, dwb, lng, lnb, w2, b2, gffn, wr):
    prev, nxt = _halo_specs(CONV_HALO, D)
    return pl.pallas_call(
        functools.partial(_mid_odd_kernel, l),
        grid=(NT,),
        in_specs=[_tile_spec(D), prev, nxt, _tile_spec(D)] + _half_specs(TM * TOK_ROWS, LANES) + [
                  _MOD_SPEC, _layer_spec((CONV_WIDTH, D), o),
                  _layer_spec((1, D), o), _layer_spec((1, D), o), _layer_spec((1, D), o),
                  _layer_spec((D, D), o), _layer_spec((1, D), o), _full_spec((DEPTH, D)),
                  _layer_spec((2, D, LANES), l)],
        out_specs=_tail_out_specs(),
        out_shape=_TAIL_OUT_SHAPES,
        scratch_shapes=[pltpu.VMEM((CONV_CHUNKS, CONV_PAD_ROWS, LANES), F32),
                        pltpu.VMEM((CONV_CHUNKS, TM, LANES), F32), pltpu.VMEM((TM, D), F32)] + _tail_scratch(),
        compiler_params=_params(),
        name="mid_odd",
    )(u, u, u, x1, moe_c, moe_l, mod, dw, dwb, lng, lnb, w2, b2, gffn, wr)


def _tile_tables_kernel(cnt_ref, te_c, tc_c, tw_c, te_l, tc_l, tw_l, nact_ref, second_ref, first_ref, wplan_ref):
    for p, (te_ref, tc_ref, tw_ref) in enumerate(((te_c, tc_c, tw_c), (te_l, tc_l, tw_l))):
        run = jnp.int32(0)
        ends = []
        for e in range(N_EXPERTS):
            first_ref[p * N_EXPERTS + e] = run
            run = run + (cnt_ref[p, e] + MOE_TM - 1) // MOE_TM
            ends.append(run)
        nact_ref[p] = run

        nxt1, nxt2 = jnp.int32(0), jnp.int32(0)
        plans = []
        for e in reversed(range(N_EXPERTS)):
            plans.append(nxt2)
            used = cnt_ref[p, e] > 0
            nxt2 = jnp.where(used, nxt1, nxt2)
            nxt1 = jnp.where(used, e + 1, nxt1)
        plans.reverse()
        second_ref[p] = nxt2
        order = jnp.int32(0)
        for e in range(N_EXPERTS):
            wplan_ref[p * N_EXPERTS + e] = order + (plans[e] << W_AHEAD_SHIFT)
            order = jnp.where(cnt_ref[p, e] > 0, jnp.where(order == W_BUFFERS - 1, 0, order + 1), order)

        def expert_of(tile, ends=ends):
            e = jnp.int32(0)
            for k in range(N_EXPERTS):
                e = e + jnp.where(tile >= ends[k], 1, 0)
            return e

        last_e = expert_of(run - 1)

        def tile_body(j, c, te_ref=te_ref, tc_ref=tc_ref, tw_ref=tw_ref, p=p, run=run, expert_of=expert_of,
                      last_e=last_e):
            e = jnp.minimum(expert_of(j), last_e)
            te_ref[j] = e
            chunk = jnp.where(j < run, j - first_ref[p * N_EXPERTS + e], 0)
            tc_ref[j] = chunk
            starts_expert = jnp.logical_and(j < run, chunk == 0)
            tw_ref[j] = wplan_ref[p * N_EXPERTS + e] + jnp.where(starts_expert, W_FIRST_UNIT, 0)
            return c

        lax.fori_loop(0, te_ref.shape[0], tile_body, 0)


def _tile_tables(cnt):
    smem = pl.BlockSpec(memory_space=pltpu.SMEM)
    shapes = [(MOE_TILES,)] * 6 + [(2,), (2,)]
    return pl.pallas_call(
        _tile_tables_kernel,
        in_specs=[smem],
        out_specs=[smem] * len(shapes),
        out_shape=[jax.ShapeDtypeStruct(s, I32) for s in shapes],
        scratch_shapes=[pltpu.SMEM((2 * N_EXPERTS,), I32), pltpu.SMEM((2 * N_EXPERTS,), I32)],
        name="tile_tables",
    )(cnt)


def _moe_kernel(l, pop, te_ref, tc_ref, tw_ref, nact_ref, second_ref, list_ref, h_ref, gate_ref,
                w1_hbm, w3_hbm, w2_hbm, o_ref, xs_ref, ys_ref, w1_buf, w3_buf, w2_buf, w_sem):
    j = pl.program_id(0)
    nact = nact_ref[pop]
    last_tile = te_ref.shape[0] - 1
    tile_rows = MOE_TM * TOK_ROWS

    def weight_copies(expert, buf):
        return [pltpu.make_async_copy(hbm.at[l, expert], vmem.at[buf], w_sem.at[buf, k])
                for k, (hbm, vmem) in enumerate(((w1_hbm, w1_buf), (w3_hbm, w3_buf), (w2_hbm, w2_buf)))]

    def list_base(tile):
        expert = te_ref[tile]
        block = expert >> LIST_PER_BLOCK_LOG2
        in_block = expert & ((1 << LIST_PER_BLOCK_LOG2) - 1)
        return (block * LIST_ROWS + tc_ref[tile] * (MOE_TM // LIST_SPAN)) * LANES + in_block * LIST_SPAN

    def tok_rows(base, r):
        first = list_ref[base + (r // LIST_SPAN) * LANES + r % LIST_SPAN]
        return pl.ds(pl.multiple_of(first, TOK_ROWS), TOK_ROWS)

    def buf_rows(buf, r):
        return pl.ds(pl.multiple_of(buf * tile_rows + r * TOK_ROWS, TOK_ROWS), TOK_ROWS)

    def gather(tile, buf):
        base = list_base(tile)
        for r in range(MOE_TM):
            xs_ref[buf_rows(buf, r), :] = h_ref[tok_rows(base, r), :]

    tile = jnp.minimum(j, last_tile)
    plan = tw_ref[tile]
    wbuf = plan & (W_FIRST_UNIT - 1)
    starts_expert = jnp.logical_and(plan & W_FIRST_UNIT != 0, j <= last_tile)
    ahead_expert = (plan >> W_AHEAD_SHIFT) - 1
    ahead_buf = jnp.where(wbuf == 0, W_BUFFERS - 1, wbuf - 1)

    @pl.when(j == 0)
    def _():
        for cp in weight_copies(te_ref[0], 0):
            cp.start()

        @pl.when(second_ref[pop] > 0)
        def _():
            for cp in weight_copies(second_ref[pop] - 1, 1):
                cp.start()

        o_ref[...] = jnp.zeros(o_ref.shape, F32)
        ys_ref[...] = jnp.zeros(ys_ref.shape, F32)
        gather(0, 0)

    @pl.when(starts_expert)
    def _():
        for cp in weight_copies(te_ref[tile], wbuf):
            cp.wait()

        @pl.when(ahead_expert >= 0)
        def _():
            for cp in weight_copies(ahead_expert, ahead_buf):
                cp.start()

    @pl.when(j <= nact)
    def _():
        cur = j & 1
        w1_ref, w3_ref, w2_ref = w1_buf.at[wbuf], w3_buf.at[wbuf], w2_buf.at[wbuf]
        x = jnp.concatenate([xs_ref[pl.ds(cur * tile_rows + s, MOE_TM, stride=TOK_ROWS), :]
                             for s in range(TOK_ROWS)], axis=1).astype(BF16)
        gather(jnp.minimum(j + 1, last_tile), 1 - cur)

        prev_base = list_base(jnp.maximum(j - 1, 0))
        for r0 in range(0, MOE_TM, MOVE_BATCH):
            dst = [tok_rows(prev_base, r0 + k) for k in range(MOVE_BATCH)]
            vals = [o_ref[dst[k], :] + ys_ref[buf_rows(1 - cur, r0 + k), :] for k in range(MOVE_BATCH)]
            for k in reversed(range(MOVE_BATCH)):
                o_ref[dst[k], :] = vals[k]

        hg = jnp.dot(x, w1_ref[...].astype(BF16), preferred_element_type=F32)
        hu = jnp.dot(x, w3_ref[...].astype(BF16), preferred_element_type=F32)
        y = _dot(_silu(hg) * hu, w2_ref[...])

        first_row = pl.multiple_of(tc_ref[tile] * (MOE_TM // LIST_SPAN), MOE_TM // LIST_SPAN)
        spread = jnp.concatenate(
            [jnp.broadcast_to(gate_ref[pl.ds(first_row + k, 1), :], (LIST_SPAN, LIST_COLS))
             for k in range(MOE_TM // LIST_SPAN)], axis=0)
        want = te_ref[tile] * LIST_SPAN + (lax.broadcasted_iota(I32, (MOE_TM, 1), 0) & (LIST_SPAN - 1))
        hit = jnp.logical_and(lax.broadcasted_iota(I32, (MOE_TM, LIST_COLS), 1) == want, j < nact)
        y = y * jnp.sum(jnp.where(hit, spread, 0.0), axis=1, keepdims=True)
        for s in range(TOK_ROWS):
            ys_ref[pl.ds(cur * tile_rows + s, MOE_TM, stride=TOK_ROWS), :] = y[:, s * LANES:(s + 1) * LANES]


def _moe(l, pop, h_tt, te, tc, tw, nact, second, rows, gate, w1, w3, w2):
    tp = T_HALF
    blk = pop
    n_tiles = te.shape[0]
    hbm = pl.BlockSpec(memory_space=pl.ANY)
    grid_spec = pltpu.PrefetchScalarGridSpec(
        num_scalar_prefetch=6,
        grid=(n_tiles + 1,),
        in_specs=[
            pl.BlockSpec((tp * TOK_ROWS, LANES), lambda j, *_: (blk, 0), pipeline_mode=pl.Buffered(1)),
            pl.BlockSpec((LIST_ROWS, LIST_COLS), lambda j, *_: (0, 0), pipeline_mode=pl.Buffered(1)),
            hbm, hbm, hbm,
        ],
        out_specs=pl.BlockSpec((tp * TOK_ROWS, LANES), lambda j, *_: (0, 0), pipeline_mode=pl.Buffered(1)),
        scratch_shapes=[pltpu.VMEM((2 * MOE_TM * TOK_ROWS, LANES), F32),
                        pltpu.VMEM((2 * MOE_TM * TOK_ROWS, LANES), F32),
                        pltpu.VMEM((W_BUFFERS, D, EXPERT_HIDDEN), F32),
                        pltpu.VMEM((W_BUFFERS, D, EXPERT_HIDDEN), F32),
                        pltpu.VMEM((W_BUFFERS, EXPERT_HIDDEN, D), F32),
                        pltpu.SemaphoreType.DMA((W_BUFFERS, 3))],
    )
    return pl.pallas_call(
        functools.partial(_moe_kernel, l, pop),
        grid_spec=grid_spec,
        out_shape=jax.ShapeDtypeStruct((tp * TOK_ROWS, LANES), F32),
        compiler_params=_params(),
        name="moe",
    )(te, tc, tw, nact, second, rows, h_tt, gate, w1, w3, w2)


def _moe_all(l, h_tt, cnt, rows_c, gate_c, rows_l, gate_l, w1, w3, w2):
    te_c, tc_c, tw_c, te_l, tc_l, tw_l, nact, second = _tile_tables(cnt)
    out_c = _moe(l, 0, h_tt, te_c, tc_c, tw_c, nact, second, rows_c.reshape(-1), gate_c, w1, w3, w2)
    out_l = _moe(l, 1, h_tt, te_l, tc_l, tw_l, nact, second, rows_l.reshape(-1), gate_l, w1, w3, w2)
    return out_c, out_l


def _final_kernel(l, x1_ref, moec_ref, moel_ref, mod_ref, g_ref, oc_ref, ol_ref):
    i = pl.program_id(0)
    r = _mod_row(i)

    def body(half, pop):
        x = _after_moe(l, r, x1_ref, (moec_ref, moel_ref)[half], mod_ref)
        ms = jnp.mean(x * x, axis=-1, keepdims=True)
        (oc_ref, ol_ref)[pop][...] = x * lax.rsqrt(ms + EPS) * g_ref[...]

    pl.when(i < MOE_SPLIT)(functools.partial(body, 0, 0))
    pl.when(jnp.logical_and(i >= MOE_SPLIT, i < NT_CTX))(functools.partial(body, 1, 0))
    pl.when(i >= NT_CTX)(functools.partial(body, 1, 1))


def _final(l, x1, moe_c, moe_l, mod, g):
    return pl.pallas_call(
        functools.partial(_final_kernel, l),
        grid=(NT,),
        in_specs=[_tile_spec(D)] + _half_specs(TM * TOK_ROWS, LANES) + [_MOD_SPEC, _full_spec((1, D))],
        out_specs=_pop_specs(TM, D),
        out_shape=[jax.ShapeDtypeStruct((T_CTX, D), F32), jax.ShapeDtypeStruct((T_LAT, D), F32)],
        compiler_params=_params(),
        name="final_norm",
    )(x1, moe_c, moe_l, mod, g)


def _rope_tables():
    t = np.arange(LAT_LEN)
    row = (t // GRID_W).astype(np.float32)
    col = (t % GRID_W).astype(np.float32)
    inv = np.float32(ROPE_BASE) ** (-np.arange(0, ROPE_AXIS_DIM, 2, dtype=np.float32) / np.float32(ROPE_AXIS_DIM))

    def table(p):
        ang = (p[:, None] * inv[None, :]).astype(np.float32)
        ang = np.concatenate([ang, ang], axis=-1)
        return np.cos(ang).astype(np.float32), np.sin(ang).astype(np.float32)

    (cr, sr), (cc, sc) = table(row), table(col)
    cos = np.concatenate([cr, cc, cr, cc], axis=-1)
    sin = np.concatenate([sr, sc, sr, sc], axis=-1)
    return jnp.asarray(cos), jnp.asarray(sin)


def _router_weights(w_grp, w_exp):
    we = jnp.transpose(w_exp, (0, 2, 1, 3)).reshape(DEPTH, D, N_GROUPS * EPG)
    pad = jnp.zeros((DEPTH, D, LANES - N_GROUPS - N_GROUPS * EPG), F32)
    wr = jnp.concatenate([w_grp, we, pad], axis=-1)
    hi = lax.bitcast_convert_type(lax.bitcast_convert_type(wr, jnp.uint32) & jnp.uint32(0xFFFF0000), F32)
    return jnp.stack([hi.astype(BF16), (wr - hi).astype(BF16)], axis=1)


def kernel(x_prompt, x_sample, cache_k, cache_v, c, c_ctx, w_ada, b_ada, norm_mix_g, norm_ffn_g, w_in_ab, pool_w, pool_scale, attn_sink, w_out_ab, conv_w1, conv_b1, conv_dw, conv_dw_b, conv_ln_g, conv_ln_b, conv_w2, conv_b2, router_grp, router_exp, moe_w1, moe_w3, moe_w2, final_g):
    xc = x_prompt.reshape(T_CTX, D)
    xl = x_sample.reshape(T_LAT, D)
    cond8 = jnp.concatenate([c_ctx[None, :], c, jnp.zeros((SUBLANES - 1 - N_LAT_SEQ, D), F32)], axis=0)
    mod = _modulation(cond8, w_ada, b_ada)
    cos_t, sin_t = _rope_tables()
    wr = _router_weights(router_grp, router_exp)
    n_even, n_odd = w_in_ab.shape[0], conv_w1.shape[0]

    a, q, k_c, k_l, v_c, v_l = _in_even(0, 0, xc, xl, mod, norm_mix_g, w_in_ab, cos_t, sin_t)
    yb_c = _attn_ctx(0, attn_sink, q, k_c, v_c)
    yb_l = _attn_lat(0, attn_sink, q, k_l, v_l, cache_k, cache_v)
    x1, h2, *routing = _mid_even(0, 0, a, yb_c, yb_l, xc, xl, mod, pool_w,
                                 pool_scale.reshape(n_even, 1, POOL_DIM), w_out_ab, norm_ffn_g, wr)
    moe0 = _moe_all(0, h2, *routing, moe_w1, moe_w3, moe_w2)

    vec = lambda p: p.reshape(n_odd, 1, -1)
    u = _in_odd(1, 0, x1, *moe0, mod, norm_mix_g, conv_w1, vec(conv_b1))
    x3, h2b, *routing = _mid_odd(1, 0, u, x1, *moe0, mod, conv_dw, vec(conv_dw_b), vec(conv_ln_g),
                                 vec(conv_ln_b), conv_w2, vec(conv_b2), norm_ffn_g, wr)
    moe1 = _moe_all(1, h2b, *routing, moe_w1, moe_w3, moe_w2)
    y_c, y_l = _final(1, x3, *moe1, mod, final_g.reshape(1, D))

    y_prompt = y_c.reshape(N_CTX_SEQ, CTX_LEN, D)
    y_sample = y_l.reshape(N_LAT_SEQ, LAT_LEN, D)
    state_k = k_c.reshape(N_CTX_SEQ, 1, CTX_LEN, N_KV_HEADS, HEAD_DIM)
    state_v = v_c.reshape(N_CTX_SEQ, 1, CTX_LEN, N_KV_HEADS, HEAD_DIM)
    return (y_prompt, y_sample, state_k, state_v)
```

```python
import functools

import jax
import jax.numpy as jnp
import numpy as np
from jax import lax
from jax.experimental import pallas as pl
from jax.experimental.pallas import tpu as pltpu

F32 = jnp.float32
BF16 = jnp.bfloat16
I32 = jnp.int32

D = 1024
N_CTX_SEQ = 16
CTX_LEN = 256
N_LAT_SEQ = 2
LAT_LEN = 1024
T_CTX = N_CTX_SEQ * CTX_LEN
T_LAT = N_LAT_SEQ * LAT_LEN
T_ALL = T_CTX + T_LAT
TM = 256
NT = T_ALL // TM
NT_CTX = T_CTX // TM
LAT_TILES_PER_SEQ = LAT_LEN // TM
GRID_W = 64
DEPTH = 2

POOL_WINDOWS = (2, 4, 8, 16)
POOL_GROUP_DIM = 128
POOL_DIM = 512
HEAD_DIM = 64
N_Q_HEADS = 8
N_KV_HEADS = 2
Q_PER_KV = 4
Q_DIM = 512
KV_DIM = 128
IN_AB = 1280
ATTN_WINDOW = 128
ATTN_BLOCK = 128
ATTN_SCALE = HEAD_DIM ** -0.5
ROPE_BASE = 10000.0
ROPE_AXIS_DIM = 32
CONV_WIDTH = 31
CONV_HALO = 16
POOL_HALO = 8
N_GROUPS = 4
EPG = 4
N_EXPERTS = 16
EXPERT_HIDDEN = 512
EPS = 1e-6
NEG_BIG = -1e30

SUBLANES = 8
LANES = 128
TOK_ROWS = D // LANES

MOE_TM = 256
MOE_SPLIT = NT // 2
T_HALF = MOE_SPLIT * TM
MOE_TILES = 2 * T_HALF // MOE_TM + N_EXPERTS
W_BUFFERS = 3
W_FIRST_UNIT = 4
W_AHEAD_SHIFT = 3
MOVE_BATCH_LOG2 = 2
MOVE_BATCH = 1 << MOVE_BATCH_LOG2
VMEM_LIMIT = 56 * 1024 * 1024


def _silu(x):
    return x * (1.0 / (1.0 + jnp.exp(-x)))


def _mod_row(i):
    return jnp.where(i < NT_CTX, 0, 1 + (i - NT_CTX) // LAT_TILES_PER_SEQ)


def _seq_info(i):
    is_ctx = i < NT_CTX
    k = (i - NT_CTX) % LAT_TILES_PER_SEQ
    off = jnp.where(is_ctx, 0, k * TM)
    n = jnp.where(is_ctx, CTX_LEN, LAT_LEN)
    first = jnp.logical_or(is_ctx, k == 0)
    last = jnp.logical_or(is_ctx, k == LAT_TILES_PER_SEQ - 1)
    return off, n, first, last


def _rms_mod(x, g, scale, shift):
    ms = jnp.mean(x * x, axis=-1, keepdims=True)
    return (x * lax.rsqrt(ms + EPS) * g) * (1.0 + scale) + shift


def _dot(a, b):
    return jnp.dot(a.astype(BF16), b.astype(BF16), preferred_element_type=F32)


def _dot_nt(a, b):
    return lax.dot_general(a.astype(BF16), b.astype(BF16), (((1,), (1,)), ((), ())),
                           preferred_element_type=F32)


def _from_token_tiles(ref):
    return jnp.concatenate([ref[pl.ds(s, TM, stride=TOK_ROWS), :] for s in range(TOK_ROWS)], axis=1)


def _tile_spec(width):
    return pl.BlockSpec((TM, width), lambda i: (i, 0))


def _full_spec(shape):
    return pl.BlockSpec(shape, lambda i: (0,) * len(shape))


def _layer_spec(shape, l):
    return pl.BlockSpec((None,) + tuple(shape), lambda i: (l,) + (0,) * len(shape))


def _pop_specs(rows, width):
    return [pl.BlockSpec((rows, width), lambda i: (jnp.minimum(i, NT_CTX - 1), 0)),
            pl.BlockSpec((rows, width), lambda i: (jnp.maximum(i - NT_CTX, 0), 0))]


def _per_population(i, body):
    pl.when(i < NT_CTX)(functools.partial(body, 0))
    pl.when(i >= NT_CTX)(functools.partial(body, 1))


def _half_specs(rows, width):
    return [pl.BlockSpec((rows, width), lambda i: (jnp.minimum(i, MOE_SPLIT - 1), 0)),
            pl.BlockSpec((rows, width), lambda i: (jnp.maximum(i - MOE_SPLIT, 0), 0))]


def _per_half(i, body):
    pl.when(i < MOE_SPLIT)(functools.partial(body, 0))
    pl.when(i >= MOE_SPLIT)(functools.partial(body, 1))


_MOD_SPEC = pl.BlockSpec((DEPTH, SUBLANES, 6 * D), lambda i: (0, 0, 0))


def _params():
    return pltpu.CompilerParams(vmem_limit_bytes=VMEM_LIMIT)


MOD_TN = 1536


def _mod_kernel(cond_ref, w_ref, b_ref, o_ref):
    s = _silu(cond_ref[...])
    o_ref[0] = _dot(s, w_ref[0]) + b_ref[0]


def _modulation(cond8, w_ada, b_ada):
    return pl.pallas_call(
        _mod_kernel,
        grid=(DEPTH, 6 * D // MOD_TN),
        in_specs=[
            pl.BlockSpec((SUBLANES, D), lambda l, n: (0, 0)),
            pl.BlockSpec((1, D, MOD_TN), lambda l, n: (l, 0, n)),
            pl.BlockSpec((1, 1, MOD_TN), lambda l, n: (l, 0, n)),
        ],
        out_specs=pl.BlockSpec((1, SUBLANES, MOD_TN), lambda l, n: (l, 0, n)),
        out_shape=jax.ShapeDtypeStruct((DEPTH, SUBLANES, 6 * D), F32),
        compiler_params=_params(),
        name="modulation",
    )(cond8, w_ada, b_ada.reshape(DEPTH, 1, 6 * D))


def _rope_chunk(xc, cos, sin):
    lane = lax.broadcasted_iota(I32, xc.shape, 1)
    first = (lane % ROPE_AXIS_DIM) < (ROPE_AXIS_DIM // 2)
    rot = jnp.where(first, -pltpu.roll(xc, LANES - ROPE_AXIS_DIM // 2, 1), pltpu.roll(xc, ROPE_AXIS_DIM // 2, 1))
    return xc * cos + rot * sin


def _in_even_kernel(l, xc_ref, xl_ref, mod_ref, g_ref, w_ref, cos_ref, sin_ref,
                    a_ref, q_ref, kc_ref, kl_ref, vc_ref, vl_ref, sk_ref, sv_ref):
    i = pl.program_id(0)
    r = _mod_row(i)

    def put_state(s_ref, val):
        for hd in range(N_KV_HEADS):
            s_ref[0, 0, :, hd, :] = val[:, hd * HEAD_DIM:(hd + 1) * HEAD_DIM]

    def body(pop):
        x = (xc_ref, xl_ref)[pop][...]
        k_ref, v_ref = (kc_ref, kl_ref)[pop], (vc_ref, vl_ref)[pop]
        shift = mod_ref[l, pl.ds(r, 1), 0:D]
        scale = mod_ref[l, pl.ds(r, 1), D:2 * D]
        h = _rms_mod(x, g_ref[l:l + 1, :], scale, shift)
        proj = _dot(h, w_ref[...])
        a_ref[...] = proj[:, :POOL_DIM]
        v_ref[...] = proj[:, POOL_DIM + Q_DIM + KV_DIM:]
        if pop == 0:
            q_ref[...] = proj[:, POOL_DIM:POOL_DIM + Q_DIM].astype(BF16)
            k_ref[...] = proj[:, POOL_DIM + Q_DIM:POOL_DIM + Q_DIM + KV_DIM]
            put_state(sk_ref, proj[:, POOL_DIM + Q_DIM:POOL_DIM + Q_DIM + KV_DIM])
            put_state(sv_ref, proj[:, POOL_DIM + Q_DIM + KV_DIM:])
        else:
            off, _, _, _ = _seq_info(i)
            off = pl.multiple_of(off, TM)
            cos = cos_ref[pl.ds(off, TM), :]
            sin = sin_ref[pl.ds(off, TM), :]
            for c in range(Q_DIM // LANES):
                lo = POOL_DIM + c * LANES
                q_ref[:, c * LANES:(c + 1) * LANES] = _rope_chunk(proj[:, lo:lo + LANES], cos, sin).astype(BF16)
            lo = POOL_DIM + Q_DIM
            k_ref[...] = _rope_chunk(proj[:, lo:lo + LANES], cos, sin)

    _per_population(i, body)


def _in_even(l, e, xc, xl, mod, g, w_in, cos_t, sin_t):
    assert CTX_LEN == TM
    state_shape = (N_CTX_SEQ, 1, CTX_LEN, N_KV_HEADS, HEAD_DIM)
    state_spec = pl.BlockSpec((1, 1, CTX_LEN, N_KV_HEADS, HEAD_DIM),
                              lambda i: (jnp.minimum(i, NT_CTX - 1), 0, 0, 0, 0))
    return pl.pallas_call(
        functools.partial(_in_even_kernel, l),
        grid=(NT,),
        in_specs=_pop_specs(TM, D) + [_MOD_SPEC, _full_spec((DEPTH, D)), _layer_spec((D, IN_AB), e),
                  _full_spec((LAT_LEN, LANES)), _full_spec((LAT_LEN, LANES))],
        out_specs=([_tile_spec(POOL_DIM), _tile_spec(Q_DIM)] + _pop_specs(TM, KV_DIM) + _pop_specs(TM, KV_DIM)
                   + [state_spec, state_spec]),
        out_shape=[jax.ShapeDtypeStruct((T_ALL, POOL_DIM), F32), jax.ShapeDtypeStruct((T_ALL, Q_DIM), BF16)]
                  + [jax.ShapeDtypeStruct((t, KV_DIM), F32) for t in (T_CTX, T_LAT, T_CTX, T_LAT)]
                  + [jax.ShapeDtypeStruct(state_shape, F32)] * 2,
        compiler_params=_params(),
        name="in_even",
    )(xc, xl, mod, g, w_in, cos_t, sin_t)


def _attn_ctx_kernel(e, sink_ref, q_ref, k_ref, v_ref, o_ref):
    q = q_ref[...]
    k = k_ref[...]
    v = v_ref[...]
    for h in range(N_KV_HEADS):
        kh = k[:, h * HEAD_DIM:(h + 1) * HEAD_DIM]
        vh = v[:, h * HEAD_DIM:(h + 1) * HEAD_DIM]
        for g in range(Q_PER_KV):
            j = h * Q_PER_KV + g
            s = _dot_nt(q[:, j * HEAD_DIM:(j + 1) * HEAD_DIM], kh) * ATTN_SCALE
            sk = sink_ref[e, j]
            m = jnp.maximum(jnp.max(s, axis=-1, keepdims=True), sk)
            p = jnp.exp(s - m)
            denom = jnp.sum(p, axis=-1, keepdims=True) + jnp.exp(sk - m)
            o_ref[:, j * HEAD_DIM:(j + 1) * HEAD_DIM] = (_dot(p, vh) / denom).astype(o_ref.dtype)


def _attn_ctx(e, sink, q, k, v):
    tile = lambda w: pl.BlockSpec((TM, w), lambda b: (b, 0))
    return pl.pallas_call(
        functools.partial(_attn_ctx_kernel, e),
        grid=(N_CTX_SEQ,),
        in_specs=[pl.BlockSpec(memory_space=pltpu.SMEM), tile(Q_DIM), tile(KV_DIM), tile(KV_DIM)],
        out_specs=tile(Q_DIM),
        out_shape=jax.ShapeDtypeStruct((T_CTX, Q_DIM), BF16),
        compiler_params=_params(),
        name="attn_ctx",
    )(sink, q, k, v)


LAT_BLOCKS = LAT_LEN // ATTN_BLOCK
BAND = 3 * ATTN_BLOCK


def _attn_lat_kernel(e, sink_ref, q_ref, k_ref, v_ref, ck_ref, cv_ref, o_ref):
    n = pl.program_id(1)
    start = jnp.clip((n - 1) * ATTN_BLOCK, 0, LAT_LEN - BAND)
    start = pl.multiple_of(start, ATTN_BLOCK)
    q = q_ref[...]
    kw = k_ref[pl.ds(start, BAND), :]
    vw = v_ref[pl.ds(start, BAND), :]
    qpos = n * ATTN_BLOCK + lax.broadcasted_iota(I32, (ATTN_BLOCK, BAND), 0)
    kpos = start + lax.broadcasted_iota(I32, (ATTN_BLOCK, BAND), 1)
    valid = jnp.abs(qpos - kpos) <= ATTN_WINDOW
    for h in range(N_KV_HEADS):
        hs = slice(h * HEAD_DIM, (h + 1) * HEAD_DIM)
        ck = ck_ref[0, 0, :, h, :]
        cv = cv_ref[0, 0, :, h, :]
        for g in range(Q_PER_KV):
            j = h * Q_PER_KV + g
            qj = q[:, j * HEAD_DIM:(j + 1) * HEAD_DIM]
            s_loc = jnp.where(valid, _dot_nt(qj, kw[:, hs]) * ATTN_SCALE, NEG_BIG)
            s_ctx = _dot_nt(qj, ck) * ATTN_SCALE
            sk = sink_ref[e, j]
            m = jnp.maximum(jnp.maximum(jnp.max(s_loc, axis=-1, keepdims=True),
                                        jnp.max(s_ctx, axis=-1, keepdims=True)), sk)
            p_loc = jnp.exp(s_loc - m)
            p_ctx = jnp.exp(s_ctx - m)
            denom = (jnp.sum(p_loc, axis=-1, keepdims=True) + jnp.sum(p_ctx, axis=-1, keepdims=True)
                     + jnp.exp(sk - m))
            o = _dot(p_loc, vw[:, hs]) + _dot(p_ctx, cv)
            o_ref[:, j * HEAD_DIM:(j + 1) * HEAD_DIM] = (o / denom).astype(o_ref.dtype)


def _attn_lat(e, sink, q, k, v, ck, cv):
    past = ck.shape[2]
    cache_spec = pl.BlockSpec((1, 1, past, N_KV_HEADS, HEAD_DIM), lambda b, n: (b, e, 0, 0, 0))
    q_blk0 = T_CTX // ATTN_BLOCK
    q_spec = pl.BlockSpec((ATTN_BLOCK, Q_DIM), lambda b, n: (q_blk0 + b * LAT_BLOCKS + n, 0))
    return pl.pallas_call(
        functools.partial(_attn_lat_kernel, e),
        grid=(N_LAT_SEQ, LAT_BLOCKS),
        in_specs=[pl.BlockSpec(memory_space=pltpu.SMEM),
                  q_spec,
                  pl.BlockSpec((LAT_LEN, KV_DIM), lambda b, n: (b, 0)),
                  pl.BlockSpec((LAT_LEN, KV_DIM), lambda b, n: (b, 0)),
                  cache_spec, cache_spec],
        out_specs=pl.BlockSpec((ATTN_BLOCK, Q_DIM), lambda b, n: (b * LAT_BLOCKS + n, 0)),
        out_shape=jax.ShapeDtypeStruct((T_LAT, Q_DIM), BF16),
        compiler_params=_params(),
        name="attn_lat",
    )(sink, q, k, v, ck, cv)


def _router(h2, wr_ref):
    hi = h2.astype(BF16)
    lo = (h2 - hi.astype(F32)).astype(BF16)
    whi = wr_ref[0]
    wlo = wr_ref[1]
    dot = functools.partial(jnp.dot, preferred_element_type=F32)
    logits_t = (dot(hi, whi) + (dot(lo, whi) + dot(hi, wlo))).T
    row = lambda k: logits_t[k:k + 1, :]

    def first_max(vals):
        best = functools.reduce(jnp.maximum, vals)
        idx = jnp.full(best.shape, len(vals) - 1, F32)
        for k in reversed(range(len(vals) - 1)):
            idx = jnp.where(vals[k] == best, float(k), idx)
        return best, idx

    grp = [row(k) for k in range(N_GROUPS)]
    gmax, g_idx = first_max(grp)
    g_w = 1.0 / sum(jnp.exp(gk - gmax) for gk in grp)
    el = []
    for e in range(EPG):
        v = row(N_GROUPS + (N_GROUPS - 1) * EPG + e)
        for k in reversed(range(N_GROUPS - 1)):
            v = jnp.where(g_idx == k, row(N_GROUPS + k * EPG + e), v)
        el.append(v)
    t1, i1 = first_max(el)
    t2, i2 = first_max([jnp.where(i1 == e, -jnp.inf, el[e]) for e in range(EPG)])
    d = jnp.exp(t2 - t1)
    w1 = g_w / (1.0 + d)
    w2 = g_w * d / (1.0 + d)
    return EPG * g_idx + i1, EPG * g_idx + i2, w1, w2


LIST_SPAN_LOG2 = 5
LIST_SPAN = 1 << LIST_SPAN_LOG2
LIST_ROWS = LANES
LIST_COLS = N_EXPERTS * LIST_SPAN
LIST_BLOCKS = LIST_COLS // LANES
LIST_PER_BLOCK_LOG2 = 2


N_TAIL_OUT = 7
N_TAIL_SCRATCH = 3


def _mixer_tail(l, x, y, mod_ref, r, gffn_ref, wr_ref, outs, scratch):
    x1_ref, h2_ref, cnt_ref, lc_ref, gc_ref, ll_ref, gl_ref = outs
    run_ref, acc_ref, gacc_ref = scratch
    i = pl.program_id(0)
    g1 = mod_ref[l, pl.ds(r, 1), 2 * D:3 * D]
    shift2 = mod_ref[l, pl.ds(r, 1), 3 * D:4 * D]
    scale2 = mod_ref[l, pl.ds(r, 1), 4 * D:5 * D]
    x1 = x + g1 * y
    x1_ref[...] = x1
    h2 = _rms_mod(x1, gffn_ref[l:l + 1, :], scale2, shift2)
    for s in range(TOK_ROWS):
        h2_ref[pl.ds(s, TM, stride=TOK_ROWS), :] = h2[:, s * LANES:(s + 1) * LANES]
    e1, e2, w1_row, w2_row = _router(h2, wr_ref)

    @pl.when(jnp.logical_or(i == 0, i == MOE_SPLIT))
    def _():
        run_ref[...] = jnp.zeros(run_ref.shape, F32)
        acc_ref[...] = jnp.zeros(acc_ref.shape, F32)
        gacc_ref[...] = jnp.zeros(gacc_ref.shape, F32)

    @pl.when(i == 0)
    def _():
        cnt_ref[...] = jnp.zeros(cnt_ref.shape, I32)

    sub_i = lax.broadcasted_iota(I32, (LANES, TM), 0)
    sub = sub_i.astype(F32)
    member = jnp.where(jnp.logical_or(sub == e1, sub == e2), 1.0, 0.0)
    tri = jnp.where(lax.broadcasted_iota(I32, (TM, TM), 0) <= lax.broadcasted_iota(I32, (TM, TM), 1), 1.0, 0.0)
    csum = jnp.dot(member.astype(BF16), tri.astype(BF16), preferred_element_type=F32)
    run = run_ref[:, 0:1]
    before = csum - member + run
    run_new = run + csum[:, TM - 1:TM]
    run_ref[...] = jnp.broadcast_to(run_new, run_ref.shape)

    col_i = lax.broadcasted_iota(I32, (TM, LIST_COLS), 1)
    rows_oh, col_rows = [], []
    for e in (e1, e2):
        rank = jnp.zeros((1, TM), F32)
        for k in range(N_EXPERTS):
            rank = rank + jnp.where(e == k, before[k:k + 1, :], 0.0)
        rank_i = rank.astype(I32)
        rows_oh.append(jnp.where(sub_i == (rank_i >> LIST_SPAN_LOG2), 1.0, 0.0))
        col_rows.append(e.astype(I32) * LIST_SPAN + (rank_i & (LIST_SPAN - 1)))
    rows_t = jnp.concatenate(rows_oh, axis=1)
    col_cols = jnp.concatenate(col_rows + [jnp.zeros((LANES - 2, TM), I32)], axis=0).T
    cols = jnp.concatenate([jnp.where(col_i == col_cols[:, k:k + 1], 1.0, 0.0).astype(BF16) for k in range(2)],
                           axis=0)
    tok = (lax.broadcasted_iota(I32, (1, 2 * TM), 1) % TM).astype(F32)
    rest = jnp.concatenate([w1_row, w2_row], axis=1)
    values = [None, tok]
    for _ in range(3):
        piece = rest.astype(BF16).astype(F32)
        values.append(piece)
        rest = rest - piece
    planes = jnp.concatenate([rows_t if v is None else rows_t * v for v in values], axis=0).astype(BF16)
    out = jnp.dot(planes, cols, preferred_element_type=F32)
    hits, tok_sum, gp0, gp1, gp2 = [out[k * LANES:(k + 1) * LANES] for k in range(len(values))]
    tile_in_pop = jnp.where(i < MOE_SPLIT, i, i - MOE_SPLIT).astype(F32)
    acc = acc_ref[...] + TOK_ROWS * (tok_sum + TM * tile_in_pop * hits)
    acc_ref[...] = acc
    gacc = gacc_ref[...] + gp0 + gp1 + gp2
    gacc_ref[...] = gacc

    def counts_row():
        return jnp.broadcast_to(run_new, (LANES, LANES)).T[0:1, :].astype(I32)

    def put_rows(dst_ref):
        for b in range(LIST_BLOCKS):
            dst_ref[b * LIST_ROWS:(b + 1) * LIST_ROWS, :] = acc[:, b * LANES:(b + 1) * LANES].astype(I32)

    @pl.when(i == MOE_SPLIT - 1)
    def _():
        cnt_ref[0:1, :] = counts_row()
        put_rows(lc_ref)
        gc_ref[...] = gacc

    @pl.when(i == NT - 1)
    def _():
        cnt_ref[1:2, :] = counts_row()
        put_rows(ll_ref)
        gl_ref[...] = gacc


_LIST_OUT_SHAPES = [(LIST_BLOCKS * LIST_ROWS, LANES), (LIST_ROWS, LIST_COLS)] * 2
_TAIL_OUT_SHAPES = [jax.ShapeDtypeStruct((T_ALL, D), F32),
                    jax.ShapeDtypeStruct((T_ALL * TOK_ROWS, LANES), F32),
                    jax.ShapeDtypeStruct((SUBLANES, LANES), I32)] + [
                        jax.ShapeDtypeStruct(s, dt) for s, dt in zip(_LIST_OUT_SHAPES, (I32, F32, I32, F32))]


def _tail_scratch():
    return [pltpu.VMEM((LANES, LANES), F32), pltpu.VMEM((LIST_ROWS, LIST_COLS), F32),
            pltpu.VMEM((LIST_ROWS, LIST_COLS), F32)]


def _tail_out_specs():
    return [_tile_spec(D),
            pl.BlockSpec((TM * TOK_ROWS, LANES), lambda i: (i, 0)),
            _full_spec((SUBLANES, LANES))] + [_full_spec(s) for s in _LIST_OUT_SHAPES]


def _halo_specs(halo, width):
    per = TM // halo
    last = T_ALL // halo - 1
    prev = pl.BlockSpec((halo, width), lambda i: (jnp.maximum(i * per - 1, 0), 0))
    nxt = pl.BlockSpec((halo, width), lambda i: (jnp.minimum((i + 1) * per, last), 0))
    return prev, nxt


def _mid_even_kernel(l, a_ref, ap_ref, an_ref, ybc_ref, ybl_ref, xc_ref, xl_ref, mod_ref, wp_ref, ps_ref, wo_ref,
                     gffn_ref, wr_ref, *refs):
    outs, (pad_ref, pool_ref), scratch = refs[:N_TAIL_OUT], refs[N_TAIL_OUT:-N_TAIL_SCRATCH], refs[-N_TAIL_SCRATCH:]
    i = pl.program_id(0)
    r = _mod_row(i)
    off, n, first, last = _seq_info(i)
    a = a_ref[...]
    ap = jnp.where(first, 0.0, ap_ref[...])
    an = jnp.where(last, 0.0, an_ref[...])
    half_rows = TM // 2
    pos = [(off + 2 * lax.broadcasted_iota(I32, (half_rows, 1), 0) + parity).astype(F32) for parity in (0, 1)]
    last_pos = (n - 1).astype(F32)
    mixed = []
    for g, w in enumerate(POOL_WINDOWS):
        lo = w // 2
        hi = w - lo - 1
        cols = slice(g * POOL_GROUP_DIM, (g + 1) * POOL_GROUP_DIM)
        pad_ref[g, 0:POOL_HALO, :] = ap[:, cols]
        pad_ref[g, POOL_HALO:POOL_HALO + TM, :] = a[:, cols]
        pad_ref[g, POOL_HALO + TM:, :] = an[:, cols]
        loads = {s: pad_ref[g, pl.ds(s, half_rows, stride=2), :]
                 for s in range(POOL_HALO - lo, POOL_HALO + hi + 2)}
        for parity in (0, 1):
            total = functools.reduce(lambda x, y: x + y,
                                     [loads[POOL_HALO + parity + j] for j in range(-lo, hi + 1)])
            count = jnp.minimum(pos[parity] + hi, last_pos) - jnp.maximum(pos[parity] - lo, 0.0) + 1.0
            pool_ref[g, pl.ds(parity, half_rows, stride=2), :] = total / count - loads[POOL_HALO + parity]
        mixed.append(_dot(pool_ref[g], wp_ref[g]))
    ya = jnp.concatenate(mixed, axis=1) * ps_ref[...]
    y_pool = _dot(ya, wo_ref[0:POOL_DIM, :])

    def body(pop):
        y = y_pool + _dot((ybc_ref, ybl_ref)[pop][...], wo_ref[POOL_DIM:, :])
        _mixer_tail(l, (xc_ref, xl_ref)[pop][...], y, mod_ref, r, gffn_ref, wr_ref, outs, scratch)

    _per_population(i, body)


def _mid_even(l, e, a, yb_c, yb_l, xc, xl, mod, w_pool, pool_scale, w_out, gffn, wr):
    prev, nxt = _halo_specs(POOL_HALO, POOL_DIM)
    return pl.pallas_call(
        functools.partial(_mid_even_kernel, l),
        grid=(NT,),
        in_specs=[_tile_spec(POOL_DIM), prev, nxt] + _pop_specs(TM, Q_DIM) + _pop_specs(TM, D) + [
                  _MOD_SPEC, _layer_spec((len(POOL_WINDOWS), POOL_GROUP_DIM, POOL_GROUP_DIM), e),
                  _layer_spec((1, POOL_DIM), e), _layer_spec((D, D), e), _full_spec((DEPTH, D)),
                  _layer_spec((2, D, LANES), l)],
        out_specs=_tail_out_specs(),
        out_shape=_TAIL_OUT_SHAPES,
        scratch_shapes=[pltpu.VMEM((len(POOL_WINDOWS), TM + 2 * POOL_HALO, POOL_GROUP_DIM), F32),
                        pltpu.VMEM((len(POOL_WINDOWS), TM, POOL_GROUP_DIM), F32)] + _tail_scratch(),
        compiler_params=_params(),
        name="mid_even",
    )(a, a, a, yb_c, yb_l, xc, xl, mod, w_pool, pool_scale, w_out, gffn, wr)


def _after_moe(l, r, x1_ref, moe_ref, mod_ref):
    return x1_ref[...] + mod_ref[l, pl.ds(r, 1), 5 * D:6 * D] * _from_token_tiles(moe_ref)


def _in_odd_kernel(l, x1_ref, moec_ref, moel_ref, mod_ref, g_ref, w1_ref, b1_ref, u_ref):
    i = pl.program_id(0)
    r = _mod_row(i)

    def body(pop):
        x2 = _after_moe(l - 1, r, x1_ref, (moec_ref, moel_ref)[pop], mod_ref)
        shift = mod_ref[l, pl.ds(r, 1), 0:D]
        scale = mod_ref[l, pl.ds(r, 1), D:2 * D]
        h = _rms_mod(x2, g_ref[l:l + 1, :], scale, shift)
        u = _dot(h, w1_ref[...]) + b1_ref[...]
        u_ref[...] = u[:, :D] * (1.0 / (1.0 + jnp.exp(-u[:, D:])))

    _per_half(i, body)


def _in_odd(l, o, x1, moe_c, moe_l, mod, g, w1, b1):
    return pl.pallas_call(
        functools.partial(_in_odd_kernel, l),
        grid=(NT,),
        in_specs=[_tile_spec(D)] + _half_specs(TM * TOK_ROWS, LANES) + [
                  _MOD_SPEC, _full_spec((DEPTH, D)), _layer_spec((D, 2 * D), o), _layer_spec((1, 2 * D), o)],
        out_specs=_tile_spec(D),
        out_shape=jax.ShapeDtypeStruct((T_ALL, D), F32),
        compiler_params=_params(),
        name="in_odd",
    )(x1, moe_c, moe_l, mod, g, w1, b1)


CONV_PAD_ROWS = TM + 2 * CONV_HALO
CONV_CHUNKS = D // LANES
CONV_BLOCK = 64


def _mid_odd_kernel(l, u_ref, up_ref, un_ref, x1_ref, moec_ref, moel_ref, mod_ref, dw_ref, dwb_ref, lng_ref, lnb_ref,
                    w2_ref, b2_ref, gffn_ref, wr_ref, *refs):
    outs, scratch = refs[:N_TAIL_OUT], refs[-N_TAIL_SCRATCH:]
    pad_ref, conv_ref, x_ref = refs[N_TAIL_OUT:-N_TAIL_SCRATCH]
    i = pl.program_id(0)

    def residual(pop):
        x_ref[...] = _after_moe(l - 1, _mod_row(i), x1_ref, (moec_ref, moel_ref)[pop], mod_ref)

    _per_half(i, residual)
    r = _mod_row(i)
    _, _, first, last = _seq_info(i)
    u = u_ref[...]
    up = jnp.where(first, 0.0, up_ref[...])
    un = jnp.where(last, 0.0, un_ref[...])
    half = CONV_WIDTH // 2
    first_row = CONV_HALO - half
    for c in range(CONV_CHUNKS):
        cols = slice(c * LANES, (c + 1) * LANES)
        pad_ref[c, 0:CONV_HALO, :] = up[:, cols]
        pad_ref[c, CONV_HALO:CONV_HALO + TM, :] = u[:, cols]
        pad_ref[c, CONV_HALO + TM:, :] = un[:, cols]
        bias = jnp.broadcast_to(dwb_ref[:, cols], (CONV_BLOCK, LANES))
        for r0 in range(0, TM, 2 * CONV_BLOCK):
            acc_even, acc_odd = bias, bias
            for s in range(first_row, first_row + CONV_WIDTH + 1):
                win = pad_ref[c, pl.ds(r0 + s, CONV_BLOCK, stride=2), :]
                j = s - first_row
                if j < CONV_WIDTH:
                    acc_even = acc_even + win * dw_ref[j:j + 1, cols]
                if j >= 1:
                    acc_odd = acc_odd + win * dw_ref[j - 1:j, cols]
            conv_ref[c, pl.ds(r0, CONV_BLOCK, stride=2), :] = acc_even
            conv_ref[c, pl.ds(r0 + 1, CONV_BLOCK, stride=2), :] = acc_odd
    acc = jnp.concatenate([conv_ref[c] for c in range(CONV_CHUNKS)], axis=1)
    mu = jnp.mean(acc, axis=-1, keepdims=True)
    cen = acc - mu
    var = jnp.mean(cen * cen, axis=-1, keepdims=True)
    v = _silu(cen * lax.rsqrt(var + EPS) * lng_ref[...] + lnb_ref[...])
    y = _dot(v, w2_ref[...]) + b2_ref[...]
    _mixer_tail(l, x_ref[...], y, mod_ref, r, gffn_ref, wr_ref, outs, scratch)


def _mid_odd(l, o, u, x1, moe_c, moe_l, mod, dw, dwb, lng, lnb, w2, b2, gffn, wr):
    prev, nxt = _halo_specs(CONV_HALO, D)
    return pl.pallas_call(
        functools.partial(_mid_odd_kernel, l),
        grid=(NT,),
        in_specs=[_tile_spec(D), prev, nxt, _tile_spec(D)] + _half_specs(TM * TOK_ROWS, LANES) + [
                  _MOD_SPEC, _layer_spec((CONV_WIDTH, D), o),
                  _layer_spec((1, D), o), _layer_spec((1, D), o), _layer_spec((1, D), o),
                  _layer_spec((D, D), o), _layer_spec((1, D), o), _full_spec((DEPTH, D)),
                  _layer_spec((2, D, LANES), l)],
        out_specs=_tail_out_specs(),
        out_shape=_TAIL_OUT_SHAPES,
        scratch_shapes=[pltpu.VMEM((CONV_CHUNKS, CONV_PAD_ROWS, LANES), F32),
                        pltpu.VMEM((CONV_CHUNKS, TM, LANES), F32), pltpu.VMEM((TM, D), F32)] + _tail_scratch(),
        compiler_params=_params(),
        name="mid_odd",
    )(u, u, u, x1, moe_c, moe_l, mod, dw, dwb, lng, lnb, w2, b2, gffn, wr)


def _tile_tables_kernel(cnt_ref, te_c, tc_c, tw_c, te_l, tc_l, tw_l, nact_ref, second_ref, first_ref, wplan_ref):
    for p, (te_ref, tc_ref, tw_ref) in enumerate(((te_c, tc_c, tw_c), (te_l, tc_l, tw_l))):
        run = jnp.int32(0)
        ends = []
        for e in range(N_EXPERTS):
            first_ref[p * N_EXPERTS + e] = run
            run = run + (cnt_ref[p, e] + MOE_TM - 1) // MOE_TM
            ends.append(run)
        nact_ref[p] = run

        nxt1, nxt2 = jnp.int32(0), jnp.int32(0)
        plans = []
        for e in reversed(range(N_EXPERTS)):
            plans.append(nxt2)
            used = cnt_ref[p, e] > 0
            nxt2 = jnp.where(used, nxt1, nxt2)
            nxt1 = jnp.where(used, e + 1, nxt1)
        plans.reverse()
        second_ref[p] = nxt2
        order = jnp.int32(0)
        for e in range(N_EXPERTS):
            wplan_ref[p * N_EXPERTS + e] = order + (plans[e] << W_AHEAD_SHIFT)
            order = jnp.where(cnt_ref[p, e] > 0, jnp.where(order == W_BUFFERS - 1, 0, order + 1), order)

        def expert_of(tile, ends=ends):
            e = jnp.int32(0)
            for k in range(N_EXPERTS):
                e = e + jnp.where(tile >= ends[k], 1, 0)
            return e

        last_e = expert_of(run - 1)

        def tile_body(j, c, te_ref=te_ref, tc_ref=tc_ref, tw_ref=tw_ref, p=p, run=run, expert_of=expert_of,
                      last_e=last_e):
            e = jnp.minimum(expert_of(j), last_e)
            te_ref[j] = e
            chunk = jnp.where(j < run, j - first_ref[p * N_EXPERTS + e], 0)
            tc_ref[j] = chunk
            starts_expert = jnp.logical_and(j < run, chunk == 0)
            tw_ref[j] = wplan_ref[p * N_EXPERTS + e] + jnp.where(starts_expert, W_FIRST_UNIT, 0)
            return c

        lax.fori_loop(0, te_ref.shape[0], tile_body, 0)


def _tile_tables(cnt):
    smem = pl.BlockSpec(memory_space=pltpu.SMEM)
    shapes = [(MOE_TILES,)] * 6 + [(2,), (2,)]
    return pl.pallas_call(
        _tile_tables_kernel,
        in_specs=[smem],
        out_specs=[smem] * len(shapes),
        out_shape=[jax.ShapeDtypeStruct(s, I32) for s in shapes],
        scratch_shapes=[pltpu.SMEM((2 * N_EXPERTS,), I32), pltpu.SMEM((2 * N_EXPERTS,), I32)],
        name="tile_tables",
    )(cnt)


def _moe_kernel(l, pop, te_ref, tc_ref, tw_ref, nact_ref, second_ref, list_ref, h_ref, gate_ref,
                w1_hbm, w3_hbm, w2_hbm, o_ref, xs_ref, ys_ref, w1_buf, w3_buf, w2_buf, w_sem):
    j = pl.program_id(0)
    nact = nact_ref[pop]
    last_tile = te_ref.shape[0] - 1
    tile_rows = MOE_TM * TOK_ROWS

    def weight_copies(expert, buf):
        return [pltpu.make_async_copy(hbm.at[l, expert], vmem.at[buf], w_sem.at[buf, k])
                for k, (hbm, vmem) in enumerate(((w1_hbm, w1_buf), (w3_hbm, w3_buf), (w2_hbm, w2_buf)))]

    def list_base(tile):
        expert = te_ref[tile]
        block = expert >> LIST_PER_BLOCK_LOG2
        in_block = expert & ((1 << LIST_PER_BLOCK_LOG2) - 1)
        return (block * LIST_ROWS + tc_ref[tile] * (MOE_TM // LIST_SPAN)) * LANES + in_block * LIST_SPAN

    def tok_rows(base, r):
        first = list_ref[base + (r // LIST_SPAN) * LANES + r % LIST_SPAN]
        return pl.ds(pl.multiple_of(first, TOK_ROWS), TOK_ROWS)

    def buf_rows(buf, r):
        return pl.ds(pl.multiple_of(buf * tile_rows + r * TOK_ROWS, TOK_ROWS), TOK_ROWS)

    def gather(tile, buf):
        base = list_base(tile)
        for r in range(MOE_TM):
            xs_ref[buf_rows(buf, r), :] = h_ref[tok_rows(base, r), :]

    tile = jnp.minimum(j, last_tile)
    plan = tw_ref[tile]
    wbuf = plan & (W_FIRST_UNIT - 1)
    starts_expert = jnp.logical_and(plan & W_FIRST_UNIT != 0, j <= last_tile)
    ahead_expert = (plan >> W_AHEAD_SHIFT) - 1
    ahead_buf = jnp.where(wbuf == 0, W_BUFFERS - 1, wbuf - 1)

    @pl.when(j == 0)
    def _():
        for cp in weight_copies(te_ref[0], 0):
            cp.start()

        @pl.when(second_ref[pop] > 0)
        def _():
            for cp in weight_copies(second_ref[pop] - 1, 1):
                cp.start()

        o_ref[...] = jnp.zeros(o_ref.shape, F32)
        ys_ref[...] = jnp.zeros(ys_ref.shape, F32)
        gather(0, 0)

    @pl.when(starts_expert)
    def _():
        for cp in weight_copies(te_ref[tile], wbuf):
            cp.wait()

        @pl.when(ahead_expert >= 0)
        def _():
            for cp in weight_copies(ahead_expert, ahead_buf):
                cp.start()

    @pl.when(j <= nact)
    def _():
        cur = j & 1
        w1_ref, w3_ref, w2_ref = w1_buf.at[wbuf], w3_buf.at[wbuf], w2_buf.at[wbuf]
        x = jnp.concatenate([xs_ref[pl.ds(cur * tile_rows + s, MOE_TM, stride=TOK_ROWS), :]
                             for s in range(TOK_ROWS)], axis=1).astype(BF16)
        gather(jnp.minimum(j + 1, last_tile), 1 - cur)

        prev_base = list_base(jnp.maximum(j - 1, 0))
        for r0 in range(0, MOE_TM, MOVE_BATCH):
            dst = [tok_rows(prev_base, r0 + k) for k in range(MOVE_BATCH)]
            vals = [o_ref[dst[k], :] + ys_ref[buf_rows(1 - cur, r0 + k), :] for k in range(MOVE_BATCH)]
            for k in reversed(range(MOVE_BATCH)):
                o_ref[dst[k], :] = vals[k]

        hg = jnp.dot(x, w1_ref[...].astype(BF16), preferred_element_type=F32)
        hu = jnp.dot(x, w3_ref[...].astype(BF16), preferred_element_type=F32)
        y = _dot(_silu(hg) * hu, w2_ref[...])

        first_row = pl.multiple_of(tc_ref[tile] * (MOE_TM // LIST_SPAN), MOE_TM // LIST_SPAN)
        spread = jnp.concatenate(
            [jnp.broadcast_to(gate_ref[pl.ds(first_row + k, 1), :], (LIST_SPAN, LIST_COLS))
             for k in range(MOE_TM // LIST_SPAN)], axis=0)
        want = te_ref[tile] * LIST_SPAN + (lax.broadcasted_iota(I32, (MOE_TM, 1), 0) & (LIST_SPAN - 1))
        hit = jnp.logical_and(lax.broadcasted_iota(I32, (MOE_TM, LIST_COLS), 1) == want, j < nact)
        y = y * jnp.sum(jnp.where(hit, spread, 0.0), axis=1, keepdims=True)
        for s in range(TOK_ROWS):
            ys_ref[pl.ds(cur * tile_rows + s, MOE_TM, stride=TOK_ROWS), :] = y[:, s * LANES:(s + 1) * LANES]


def _moe(l, pop, h_tt, te, tc, tw, nact, second, rows, gate, w1, w3, w2):
    tp = T_HALF
    blk = pop
    n_tiles = te.shape[0]
    hbm = pl.BlockSpec(memory_space=pl.ANY)
    grid_spec = pltpu.PrefetchScalarGridSpec(
        num_scalar_prefetch=6,
        grid=(n_tiles + 1,),
        in_specs=[
            pl.BlockSpec((tp * TOK_ROWS, LANES), lambda j, *_: (blk, 0), pipeline_mode=pl.Buffered(1)),
            pl.BlockSpec((LIST_ROWS, LIST_COLS), lambda j, *_: (0, 0), pipeline_mode=pl.Buffered(1)),
            hbm, hbm, hbm,
        ],
        out_specs=pl.BlockSpec((tp * TOK_ROWS, LANES), lambda j, *_: (0, 0), pipeline_mode=pl.Buffered(1)),
        scratch_shapes=[pltpu.VMEM((2 * MOE_TM * TOK_ROWS, LANES), F32),
                        pltpu.VMEM((2 * MOE_TM * TOK_ROWS, LANES), F32),
                        pltpu.VMEM((W_BUFFERS, D, EXPERT_HIDDEN), F32),
                        pltpu.VMEM((W_BUFFERS, D, EXPERT_HIDDEN), F32),
                        pltpu.VMEM((W_BUFFERS, EXPERT_HIDDEN, D), F32),
                        pltpu.SemaphoreType.DMA((W_BUFFERS, 3))],
    )
    return pl.pallas_call(
        functools.partial(_moe_kernel, l, pop),
        grid_spec=grid_spec,
        out_shape=jax.ShapeDtypeStruct((tp * TOK_ROWS, LANES), F32),
        compiler_params=_params(),
        name="moe",
    )(te, tc, tw, nact, second, rows, h_tt, gate, w1, w3, w2)


def _moe_all(l, h_tt, cnt, rows_c, gate_c, rows_l, gate_l, w1, w3, w2):
    te_c, tc_c, tw_c, te_l, tc_l, tw_l, nact, second = _tile_tables(cnt)
    out_c = _moe(l, 0, h_tt, te_c, tc_c, tw_c, nact, second, rows_c.reshape(-1), gate_c, w1, w3, w2)
    out_l = _moe(l, 1, h_tt, te_l, tc_l, tw_l, nact, second, rows_l.reshape(-1), gate_l, w1, w3, w2)
    return out_c, out_l


def _final_kernel(l, x1_ref, moec_ref, moel_ref, mod_ref, g_ref, oc_ref, ol_ref):
    i = pl.program_id(0)
    r = _mod_row(i)

    def body(half, pop):
        x = _after_moe(l, r, x1_ref, (moec_ref, moel_ref)[half], mod_ref)
        ms = jnp.mean(x * x, axis=-1, keepdims=True)
        (oc_ref, ol_ref)[pop][...] = x * lax.rsqrt(ms + EPS) * g_ref[...]

    pl.when(i < MOE_SPLIT)(functools.partial(body, 0, 0))
    pl.when(jnp.logical_and(i >= MOE_SPLIT, i < NT_CTX))(functools.partial(body, 1, 0))
    pl.when(i >= NT_CTX)(functools.partial(body, 1, 1))


def _final(l, x1, moe_c, moe_l, mod, g):
    return pl.pallas_call(
        functools.partial(_final_kernel, l),
        grid=(NT,),
        in_specs=[_tile_spec(D)] + _half_specs(TM * TOK_ROWS, LANES) + [_MOD_SPEC, _full_spec((1, D))],
        out_specs=_pop_specs(TM, D),
        out_shape=[jax.ShapeDtypeStruct((T_CTX, D), F32), jax.ShapeDtypeStruct((T_LAT, D), F32)],
        compiler_params=_params(),
        name="final_norm",
    )(x1, moe_c, moe_l, mod, g)


def _rope_tables():
    t = np.arange(LAT_LEN)
    row = (t // GRID_W).astype(np.float32)
    col = (t % GRID_W).astype(np.float32)
    inv = np.float32(ROPE_BASE) ** (-np.arange(0, ROPE_AXIS_DIM, 2, dtype=np.float32) / np.float32(ROPE_AXIS_DIM))

    def table(p):
        ang = (p[:, None] * inv[None, :]).astype(np.float32)
        ang = np.concatenate([ang, ang], axis=-1)
        return np.cos(ang).astype(np.float32), np.sin(ang).astype(np.float32)

    (cr, sr), (cc, sc) = table(row), table(col)
    cos = np.concatenate([cr, cc, cr, cc], axis=-1)
    sin = np.concatenate([sr, sc, sr, sc], axis=-1)
    return jnp.asarray(cos), jnp.asarray(sin)


def _router_weights(w_grp, w_exp):
    we = jnp.transpose(w_exp, (0, 2, 1, 3)).reshape(DEPTH, D, N_GROUPS * EPG)
    pad = jnp.zeros((DEPTH, D, LANES - N_GROUPS - N_GROUPS * EPG), F32)
    wr = jnp.concatenate([w_grp, we, pad], axis=-1)
    hi = lax.bitcast_convert_type(lax.bitcast_convert_type(wr, jnp.uint32) & jnp.uint32(0xFFFF0000), F32)
    return jnp.stack([hi.astype(BF16), (wr - hi).astype(BF16)], axis=1)


def kernel(x_prompt, x_sample, cache_k, cache_v, c, c_ctx, w_ada, b_ada, norm_mix_g, norm_ffn_g, w_in_ab, pool_w, pool_scale, attn_sink, w_out_ab, conv_w1, conv_b1, conv_dw, conv_dw_b, conv_ln_g, conv_ln_b, conv_w2, conv_b2, router_grp, router_exp, moe_w1, moe_w3, moe_w2, final_g):
    xc = x_prompt.reshape(T_CTX, D)
    xl = x_sample.reshape(T_LAT, D)
    cond8 = jnp.concatenate([c_ctx[None, :], c, jnp.zeros((SUBLANES - 1 - N_LAT_SEQ, D), F32)], axis=0)
    mod = _modulation(cond8, w_ada, b_ada)
    cos_t, sin_t = _rope_tables()
    wr = _router_weights(router_grp, router_exp)
    n_even, n_odd = w_in_ab.shape[0], conv_w1.shape[0]

    a, q, k_c, k_l, v_c, v_l, state_k, state_v = _in_even(0, 0, xc, xl, mod, norm_mix_g, w_in_ab, cos_t, sin_t)
    yb_c = _attn_ctx(0, attn_sink, q, k_c, v_c)
    yb_l = _attn_lat(0, attn_sink, q, k_l, v_l, cache_k, cache_v)
    x1, h2, *routing = _mid_even(0, 0, a, yb_c, yb_l, xc, xl, mod, pool_w,
                                 pool_scale.reshape(n_even, 1, POOL_DIM), w_out_ab, norm_ffn_g, wr)
    moe0 = _moe_all(0, h2, *routing, moe_w1, moe_w3, moe_w2)

    vec = lambda p: p.reshape(n_odd, 1, -1)
    u = _in_odd(1, 0, x1, *moe0, mod, norm_mix_g, conv_w1, vec(conv_b1))
    x3, h2b, *routing = _mid_odd(1, 0, u, x1, *moe0, mod, conv_dw, vec(conv_dw_b), vec(conv_ln_g),
                                 vec(conv_ln_b), conv_w2, vec(conv_b2), norm_ffn_g, wr)
    moe1 = _moe_all(1, h2b, *routing, moe_w1, moe_w3, moe_w2)
    y_c, y_l = _final(1, x3, *moe1, mod, final_g.reshape(1, D))

    y_prompt = y_c.reshape(N_CTX_SEQ, CTX_LEN, D)
    y_sample = y_l.reshape(N_LAT_SEQ, LAT_LEN, D)
    return (y_prompt, y_sample, state_k, state_v)
```

```python
import functools

import jax
import jax.numpy as jnp
import numpy as np
from jax import lax
from jax.experimental import pallas as pl
from jax.experimental.pallas import tpu as pltpu

F32 = jnp.float32
BF16 = jnp.bfloat16
I32 = jnp.int32

D = 1024
N_CTX_SEQ = 16
CTX_LEN = 256
N_LAT_SEQ = 2
LAT_LEN = 1024
T_CTX = N_CTX_SEQ * CTX_LEN
T_LAT = N_LAT_SEQ * LAT_LEN
T_ALL = T_CTX + T_LAT
TM = 256
NT = T_ALL // TM
NT_CTX = T_CTX // TM
LAT_TILES_PER_SEQ = LAT_LEN // TM
GRID_W = 64
DEPTH = 2

POOL_WINDOWS = (2, 4, 8, 16)
POOL_GROUP_DIM = 128
POOL_DIM = 512
HEAD_DIM = 64
N_Q_HEADS = 8
N_KV_HEADS = 2
Q_PER_KV = 4
Q_DIM = 512
KV_DIM = 128
IN_AB = 1280
ATTN_WINDOW = 128
ATTN_BLOCK = 128
ATTN_SCALE = HEAD_DIM ** -0.5
ROPE_BASE = 10000.0
ROPE_AXIS_DIM = 32
CONV_WIDTH = 31
CONV_HALO = 16
POOL_HALO = 8
N_GROUPS = 4
EPG = 4
N_EXPERTS = 16
EXPERT_HIDDEN = 512
EPS = 1e-6
NEG_BIG = -1e30

SUBLANES = 8
LANES = 128
TOK_ROWS = D // LANES

MOE_TM = 256
MOE_SPLIT = NT // 2
T_HALF = MOE_SPLIT * TM
MOE_TILES = 2 * T_HALF // MOE_TM + N_EXPERTS
W_BUFFERS = 3
W_FIRST_UNIT = 4
W_AHEAD_SHIFT = 3
MOVE_BATCH_LOG2 = 2
MOVE_BATCH = 1 << MOVE_BATCH_LOG2
VMEM_LIMIT = 56 * 1024 * 1024


def _silu(x):
    return x * (1.0 / (1.0 + jnp.exp(-x)))


def _mod_row(i):
    return jnp.where(i < NT_CTX, 0, 1 + (i - NT_CTX) // LAT_TILES_PER_SEQ)


def _seq_info(i):
    is_ctx = i < NT_CTX
    k = (i - NT_CTX) % LAT_TILES_PER_SEQ
    off = jnp.where(is_ctx, 0, k * TM)
    n = jnp.where(is_ctx, CTX_LEN, LAT_LEN)
    first = jnp.logical_or(is_ctx, k == 0)
    last = jnp.logical_or(is_ctx, k == LAT_TILES_PER_SEQ - 1)
    return off, n, first, last


def _rms_mod(x, g, scale, shift):
    ms = jnp.mean(x * x, axis=-1, keepdims=True)
    return (x * lax.rsqrt(ms + EPS) * g) * (1.0 + scale) + shift


def _dot(a, b):
    return jnp.dot(a.astype(BF16), b.astype(BF16), preferred_element_type=F32)


def _dot_nt(a, b):
    return lax.dot_general(a.astype(BF16), b.astype(BF16), (((1,), (1,)), ((), ())),
                           preferred_element_type=F32)


def _from_token_tiles(ref):
    rows = ref.shape[0] // TOK_ROWS
    return jnp.concatenate([ref[pl.ds(s, rows, stride=TOK_ROWS), :] for s in range(TOK_ROWS)], axis=1)


def _tile_spec(width):
    return pl.BlockSpec((TM, width), lambda i: (i, 0))


def _full_spec(shape):
    return pl.BlockSpec(shape, lambda i: (0,) * len(shape))


def _layer_spec(shape, l):
    return pl.BlockSpec((None,) + tuple(shape), lambda i: (l,) + (0,) * len(shape))


def _pop_specs(rows, width):
    return [pl.BlockSpec((rows, width), lambda i: (jnp.minimum(i, NT_CTX - 1), 0)),
            pl.BlockSpec((rows, width), lambda i: (jnp.maximum(i - NT_CTX, 0), 0))]


def _per_population(i, body):
    pl.when(i < NT_CTX)(functools.partial(body, 0))
    pl.when(i >= NT_CTX)(functools.partial(body, 1))


def _half_specs(rows, width):
    return [pl.BlockSpec((rows, width), lambda i: (jnp.minimum(i, MOE_SPLIT - 1), 0)),
            pl.BlockSpec((rows, width), lambda i: (jnp.maximum(i - MOE_SPLIT, 0), 0))]


def _per_half(i, body):
    pl.when(i < MOE_SPLIT)(functools.partial(body, 0))
    pl.when(i >= MOE_SPLIT)(functools.partial(body, 1))


PAIR = 2
assert LAT_TILES_PER_SEQ % PAIR == 0 and MOE_SPLIT % PAIR == 0 and NT_CTX % PAIR == 0


def _pair_spec(width):
    return pl.BlockSpec((PAIR * TM, width), lambda i: (i, 0))


def _pair_half_specs():
    split = MOE_SPLIT // PAIR
    block = (PAIR * TM * TOK_ROWS, LANES)
    return [pl.BlockSpec(block, lambda i: (jnp.minimum(i, split - 1), 0)),
            pl.BlockSpec(block, lambda i: (jnp.maximum(i - split, 0), 0))]


def _pair_pop_specs(width):
    n_ctx = NT_CTX // PAIR
    return [pl.BlockSpec((PAIR * TM, width), lambda i: (jnp.minimum(i, n_ctx - 1), 0)),
            pl.BlockSpec((PAIR * TM, width), lambda i: (jnp.maximum(i - n_ctx, 0), 0))]


_MOD_SPEC = pl.BlockSpec((DEPTH, SUBLANES, 6 * D), lambda i: (0, 0, 0))


def _params():
    return pltpu.CompilerParams(vmem_limit_bytes=VMEM_LIMIT)


MOD_TN = 1536


def _mod_kernel(cond_ref, w_ref, b_ref, o_ref):
    s = _silu(cond_ref[...])
    o_ref[0] = _dot(s, w_ref[0]) + b_ref[0]


def _modulation(cond8, w_ada, b_ada):
    return pl.pallas_call(
        _mod_kernel,
        grid=(DEPTH, 6 * D // MOD_TN),
        in_specs=[
            pl.BlockSpec((SUBLANES, D), lambda l, n: (0, 0)),
            pl.BlockSpec((1, D, MOD_TN), lambda l, n: (l, 0, n)),
            pl.BlockSpec((1, 1, MOD_TN), lambda l, n: (l, 0, n)),
        ],
        out_specs=pl.BlockSpec((1, SUBLANES, MOD_TN), lambda l, n: (l, 0, n)),
        out_shape=jax.ShapeDtypeStruct((DEPTH, SUBLANES, 6 * D), F32),
        compiler_params=_params(),
        name="modulation",
    )(cond8, w_ada, b_ada.reshape(DEPTH, 1, 6 * D))


def _rope_chunk(xc, cos, sin):
    lane = lax.broadcasted_iota(I32, xc.shape, 1)
    first = (lane % ROPE_AXIS_DIM) < (ROPE_AXIS_DIM // 2)
    rot = jnp.where(first, -pltpu.roll(xc, LANES - ROPE_AXIS_DIM // 2, 1), pltpu.roll(xc, ROPE_AXIS_DIM // 2, 1))
    return xc * cos + rot * sin


def _in_even_kernel(l, xc_ref, xl_ref, mod_ref, g_ref, w_ref, cos_ref, sin_ref,
                    a_ref, q_ref, kc_ref, kl_ref, vc_ref, vl_ref, sk_ref, sv_ref):
    i = pl.program_id(0)
    r = _mod_row(i)

    def put_state(s_ref, val):
        for hd in range(N_KV_HEADS):
            s_ref[0, 0, :, hd, :] = val[:, hd * HEAD_DIM:(hd + 1) * HEAD_DIM]

    def body(pop):
        x = (xc_ref, xl_ref)[pop][...]
        k_ref, v_ref = (kc_ref, kl_ref)[pop], (vc_ref, vl_ref)[pop]
        shift = mod_ref[l, pl.ds(r, 1), 0:D]
        scale = mod_ref[l, pl.ds(r, 1), D:2 * D]
        h = _rms_mod(x, g_ref[l:l + 1, :], scale, shift)
        proj = _dot(h, w_ref[...])
        a_ref[...] = proj[:, :POOL_DIM]
        v_ref[...] = proj[:, POOL_DIM + Q_DIM + KV_DIM:]
        if pop == 0:
            q_ref[...] = proj[:, POOL_DIM:POOL_DIM + Q_DIM].astype(BF16)
            k_ref[...] = proj[:, POOL_DIM + Q_DIM:POOL_DIM + Q_DIM + KV_DIM]
            put_state(sk_ref, proj[:, POOL_DIM + Q_DIM:POOL_DIM + Q_DIM + KV_DIM])
            put_state(sv_ref, proj[:, POOL_DIM + Q_DIM + KV_DIM:])
        else:
            off, _, _, _ = _seq_info(i)
            off = pl.multiple_of(off, TM)
            cos = cos_ref[pl.ds(off, TM), :]
            sin = sin_ref[pl.ds(off, TM), :]
            for c in range(Q_DIM // LANES):
                lo = POOL_DIM + c * LANES
                q_ref[:, c * LANES:(c + 1) * LANES] = _rope_chunk(proj[:, lo:lo + LANES], cos, sin).astype(BF16)
            lo = POOL_DIM + Q_DIM
            k_ref[...] = _rope_chunk(proj[:, lo:lo + LANES], cos, sin)

    _per_population(i, body)


def _in_even(l, e, xc, xl, mod, g, w_in, cos_t, sin_t):
    assert CTX_LEN == TM
    state_shape = (N_CTX_SEQ, 1, CTX_LEN, N_KV_HEADS, HEAD_DIM)
    state_spec = pl.BlockSpec((1, 1, CTX_LEN, N_KV_HEADS, HEAD_DIM),
                              lambda i: (jnp.minimum(i, NT_CTX - 1), 0, 0, 0, 0))
    return pl.pallas_call(
        functools.partial(_in_even_kernel, l),
        grid=(NT,),
        in_specs=_pop_specs(TM, D) + [_MOD_SPEC, _full_spec((DEPTH, D)), _layer_spec((D, IN_AB), e),
                  _full_spec((LAT_LEN, LANES)), _full_spec((LAT_LEN, LANES))],
        out_specs=([_tile_spec(POOL_DIM), _tile_spec(Q_DIM)] + _pop_specs(TM, KV_DIM) + _pop_specs(TM, KV_DIM)
                   + [state_spec, state_spec]),
        out_shape=[jax.ShapeDtypeStruct((T_ALL, POOL_DIM), F32), jax.ShapeDtypeStruct((T_ALL, Q_DIM), BF16)]
                  + [jax.ShapeDtypeStruct((t, KV_DIM), F32) for t in (T_CTX, T_LAT, T_CTX, T_LAT)]
                  + [jax.ShapeDtypeStruct(state_shape, F32)] * 2,
        compiler_params=_params(),
        name="in_even",
    )(xc, xl, mod, g, w_in, cos_t, sin_t)


def _attn_ctx_kernel(e, sink_ref, q_ref, k_ref, v_ref, o_ref):
    q = q_ref[...]
    k = k_ref[...]
    v = v_ref[...]
    for h in range(N_KV_HEADS):
        kh = k[:, h * HEAD_DIM:(h + 1) * HEAD_DIM]
        vh = v[:, h * HEAD_DIM:(h + 1) * HEAD_DIM]
        for g in range(Q_PER_KV):
            j = h * Q_PER_KV + g
            s = _dot_nt(q[:, j * HEAD_DIM:(j + 1) * HEAD_DIM], kh) * ATTN_SCALE
            sk = sink_ref[e, j]
            m = jnp.maximum(jnp.max(s, axis=-1, keepdims=True), sk)
            p = jnp.exp(s - m)
            denom = jnp.sum(p, axis=-1, keepdims=True) + jnp.exp(sk - m)
            o_ref[:, j * HEAD_DIM:(j + 1) * HEAD_DIM] = (_dot(p, vh) / denom).astype(o_ref.dtype)


def _attn_ctx(e, sink, q, k, v):
    tile = lambda w: pl.BlockSpec((TM, w), lambda b: (b, 0))
    return pl.pallas_call(
        functools.partial(_attn_ctx_kernel, e),
        grid=(N_CTX_SEQ,),
        in_specs=[pl.BlockSpec(memory_space=pltpu.SMEM), tile(Q_DIM), tile(KV_DIM), tile(KV_DIM)],
        out_specs=tile(Q_DIM),
        out_shape=jax.ShapeDtypeStruct((T_CTX, Q_DIM), BF16),
        compiler_params=_params(),
        name="attn_ctx",
    )(sink, q, k, v)


LAT_BLOCKS = LAT_LEN // ATTN_BLOCK
BAND = 3 * ATTN_BLOCK


def _attn_lat_kernel(e, sink_ref, q_ref, k_ref, v_ref, ck_ref, cv_ref, o_ref):
    n = pl.program_id(1)
    start = jnp.clip((n - 1) * ATTN_BLOCK, 0, LAT_LEN - BAND)
    start = pl.multiple_of(start, ATTN_BLOCK)
    q = q_ref[...]
    kw = k_ref[pl.ds(start, BAND), :]
    vw = v_ref[pl.ds(start, BAND), :]
    qpos = n * ATTN_BLOCK + lax.broadcasted_iota(I32, (ATTN_BLOCK, BAND), 0)
    kpos = start + lax.broadcasted_iota(I32, (ATTN_BLOCK, BAND), 1)
    valid = jnp.abs(qpos - kpos) <= ATTN_WINDOW
    for h in range(N_KV_HEADS):
        hs = slice(h * HEAD_DIM, (h + 1) * HEAD_DIM)
        ck = ck_ref[0, 0, :, h, :]
        cv = cv_ref[0, 0, :, h, :]
        for g in range(Q_PER_KV):
            j = h * Q_PER_KV + g
            qj = q[:, j * HEAD_DIM:(j + 1) * HEAD_DIM]
            s_loc = jnp.where(valid, _dot_nt(qj, kw[:, hs]) * ATTN_SCALE, NEG_BIG)
            s_ctx = _dot_nt(qj, ck) * ATTN_SCALE
            sk = sink_ref[e, j]
            m = jnp.maximum(jnp.maximum(jnp.max(s_loc, axis=-1, keepdims=True),
                                        jnp.max(s_ctx, axis=-1, keepdims=True)), sk)
            p_loc = jnp.exp(s_loc - m)
            p_ctx = jnp.exp(s_ctx - m)
            denom = (jnp.sum(p_loc, axis=-1, keepdims=True) + jnp.sum(p_ctx, axis=-1, keepdims=True)
                     + jnp.exp(sk - m))
            o = _dot(p_loc, vw[:, hs]) + _dot(p_ctx, cv)
            o_ref[:, j * HEAD_DIM:(j + 1) * HEAD_DIM] = (o / denom).astype(o_ref.dtype)


def _attn_lat(e, sink, q, k, v, ck, cv):
    past = ck.shape[2]
    cache_spec = pl.BlockSpec((1, 1, past, N_KV_HEADS, HEAD_DIM), lambda b, n: (b, e, 0, 0, 0))
    q_blk0 = T_CTX // ATTN_BLOCK
    q_spec = pl.BlockSpec((ATTN_BLOCK, Q_DIM), lambda b, n: (q_blk0 + b * LAT_BLOCKS + n, 0))
    return pl.pallas_call(
        functools.partial(_attn_lat_kernel, e),
        grid=(N_LAT_SEQ, LAT_BLOCKS),
        in_specs=[pl.BlockSpec(memory_space=pltpu.SMEM),
                  q_spec,
                  pl.BlockSpec((LAT_LEN, KV_DIM), lambda b, n: (b, 0)),
                  pl.BlockSpec((LAT_LEN, KV_DIM), lambda b, n: (b, 0)),
                  cache_spec, cache_spec],
        out_specs=pl.BlockSpec((ATTN_BLOCK, Q_DIM), lambda b, n: (b * LAT_BLOCKS + n, 0)),
        out_shape=jax.ShapeDtypeStruct((T_LAT, Q_DIM), BF16),
        compiler_params=_params(),
        name="attn_lat",
    )(sink, q, k, v, ck, cv)


def _router(h2, wr_ref):
    hi = h2.astype(BF16)
    lo = (h2 - hi.astype(F32)).astype(BF16)
    whi = wr_ref[0]
    wlo = wr_ref[1]
    dot = functools.partial(jnp.dot, preferred_element_type=F32)
    logits_t = (dot(hi, whi) + (dot(lo, whi) + dot(hi, wlo))).T
    row = lambda k: logits_t[k:k + 1, :]

    def first_max(vals):
        best = functools.reduce(jnp.maximum, vals)
        idx = jnp.full(best.shape, len(vals) - 1, F32)
        for k in reversed(range(len(vals) - 1)):
            idx = jnp.where(vals[k] == best, float(k), idx)
        return best, idx

    grp = [row(k) for k in range(N_GROUPS)]
    gmax, g_idx = first_max(grp)
    g_w = 1.0 / sum(jnp.exp(gk - gmax) for gk in grp)
    el = []
    for e in range(EPG):
        v = row(N_GROUPS + (N_GROUPS - 1) * EPG + e)
        for k in reversed(range(N_GROUPS - 1)):
            v = jnp.where(g_idx == k, row(N_GROUPS + k * EPG + e), v)
        el.append(v)
    t1, i1 = first_max(el)
    t2, i2 = first_max([jnp.where(i1 == e, -jnp.inf, el[e]) for e in range(EPG)])
    d = jnp.exp(t2 - t1)
    w1 = g_w / (1.0 + d)
    w2 = g_w * d / (1.0 + d)
    return EPG * g_idx + i1, EPG * g_idx + i2, w1, w2


LIST_SPAN_LOG2 = 5
LIST_SPAN = 1 << LIST_SPAN_LOG2
LIST_ROWS = LANES
LIST_COLS = N_EXPERTS * LIST_SPAN
LIST_BLOCKS = LIST_COLS // LANES
LIST_PER_BLOCK_LOG2 = 2


N_TAIL_OUT = 7
N_TAIL_SCRATCH = 3


def _mixer_tail(l, x, y, mod_ref, r, gffn_ref, wr_ref, outs, scratch):
    x1_ref, h2_ref, cnt_ref, lc_ref, gc_ref, ll_ref, gl_ref = outs
    run_ref, acc_ref, gacc_ref = scratch
    i = pl.program_id(0)
    g1 = mod_ref[l, pl.ds(r, 1), 2 * D:3 * D]
    shift2 = mod_ref[l, pl.ds(r, 1), 3 * D:4 * D]
    scale2 = mod_ref[l, pl.ds(r, 1), 4 * D:5 * D]
    x1 = x + g1 * y
    x1_ref[...] = x1
    h2 = _rms_mod(x1, gffn_ref[l:l + 1, :], scale2, shift2)
    for s in range(TOK_ROWS):
        h2_ref[pl.ds(s, TM, stride=TOK_ROWS), :] = h2[:, s * LANES:(s + 1) * LANES]
    e1, e2, w1_row, w2_row = _router(h2, wr_ref)

    @pl.when(jnp.logical_or(i == 0, i == MOE_SPLIT))
    def _():
        run_ref[...] = jnp.zeros(run_ref.shape, F32)
        acc_ref[...] = jnp.zeros(acc_ref.shape, F32)
        gacc_ref[...] = jnp.zeros(gacc_ref.shape, F32)

    @pl.when(i == 0)
    def _():
        cnt_ref[...] = jnp.zeros(cnt_ref.shape, I32)

    sub_i = lax.broadcasted_iota(I32, (LANES, TM), 0)
    sub = sub_i.astype(F32)
    member = jnp.where(jnp.logical_or(sub == e1, sub == e2), 1.0, 0.0)
    tri = jnp.where(lax.broadcasted_iota(I32, (TM, TM), 0) <= lax.broadcasted_iota(I32, (TM, TM), 1), 1.0, 0.0)
    csum = jnp.dot(member.astype(BF16), tri.astype(BF16), preferred_element_type=F32)
    run = run_ref[:, 0:1]
    before = csum - member + run
    run_new = run + csum[:, TM - 1:TM]
    run_ref[...] = jnp.broadcast_to(run_new, run_ref.shape)

    col_i = lax.broadcasted_iota(I32, (TM, LIST_COLS), 1)
    rows_oh, col_rows = [], []
    for e in (e1, e2):
        rank = jnp.zeros((1, TM), F32)
        for k in range(N_EXPERTS):
            rank = rank + jnp.where(e == k, before[k:k + 1, :], 0.0)
        rank_i = rank.astype(I32)
        rows_oh.append(jnp.where(sub_i == (rank_i >> LIST_SPAN_LOG2), 1.0, 0.0))
        col_rows.append(e.astype(I32) * LIST_SPAN + (rank_i & (LIST_SPAN - 1)))
    rows_t = jnp.concatenate(rows_oh, axis=1)
    col_cols = jnp.concatenate(col_rows + [jnp.zeros((LANES - 2, TM), I32)], axis=0).T
    cols = jnp.concatenate([jnp.where(col_i == col_cols[:, k:k + 1], 1.0, 0.0).astype(BF16) for k in range(2)],
                           axis=0)
    tok = (lax.broadcasted_iota(I32, (1, 2 * TM), 1) % TM).astype(F32)
    rest = jnp.concatenate([w1_row, w2_row], axis=1)
    values = [None, tok]
    for _ in range(3):
        piece = rest.astype(BF16).astype(F32)
        values.append(piece)
        rest = rest - piece
    planes = jnp.concatenate([rows_t if v is None else rows_t * v for v in values], axis=0).astype(BF16)
    out = jnp.dot(planes, cols, preferred_element_type=F32)
    hits, tok_sum, gp0, gp1, gp2 = [out[k * LANES:(k + 1) * LANES] for k in range(len(values))]
    tile_in_pop = jnp.where(i < MOE_SPLIT, i, i - MOE_SPLIT).astype(F32)
    acc = acc_ref[...] + TOK_ROWS * (tok_sum + TM * tile_in_pop * hits)
    acc_ref[...] = acc
    gacc = gacc_ref[...] + gp0 + gp1 + gp2
    gacc_ref[...] = gacc

    def counts_row():
        return jnp.broadcast_to(run_new, (LANES, LANES)).T[0:1, :].astype(I32)

    def put_rows(dst_ref):
        for b in range(LIST_BLOCKS):
            dst_ref[b * LIST_ROWS:(b + 1) * LIST_ROWS, :] = acc[:, b * LANES:(b + 1) * LANES].astype(I32)

    @pl.when(i == MOE_SPLIT - 1)
    def _():
        cnt_ref[0:1, :] = counts_row()
        put_rows(lc_ref)
        gc_ref[...] = gacc

    @pl.when(i == NT - 1)
    def _():
        cnt_ref[1:2, :] = counts_row()
        put_rows(ll_ref)
        gl_ref[...] = gacc


_LIST_OUT_SHAPES = [(LIST_BLOCKS * LIST_ROWS, LANES), (LIST_ROWS, LIST_COLS)] * 2
_TAIL_OUT_SHAPES = [jax.ShapeDtypeStruct((T_ALL, D), F32),
                    jax.ShapeDtypeStruct((T_ALL * TOK_ROWS, LANES), F32),
                    jax.ShapeDtypeStruct((SUBLANES, LANES), I32)] + [
                        jax.ShapeDtypeStruct(s, dt) for s, dt in zip(_LIST_OUT_SHAPES, (I32, F32, I32, F32))]


def _tail_scratch():
    return [pltpu.VMEM((LANES, LANES), F32), pltpu.VMEM((LIST_ROWS, LIST_COLS), F32),
            pltpu.VMEM((LIST_ROWS, LIST_COLS), F32)]


def _tail_out_specs():
    return [_tile_spec(D),
            pl.BlockSpec((TM * TOK_ROWS, LANES), lambda i: (i, 0)),
            _full_spec((SUBLANES, LANES))] + [_full_spec(s) for s in _LIST_OUT_SHAPES]


def _halo_specs(halo, width):
    per = TM // halo
    last = T_ALL // halo - 1
    prev = pl.BlockSpec((halo, width), lambda i: (jnp.maximum(i * per - 1, 0), 0))
    nxt = pl.BlockSpec((halo, width), lambda i: (jnp.minimum((i + 1) * per, last), 0))
    return prev, nxt


def _mid_even_kernel(l, a_ref, ap_ref, an_ref, ybc_ref, ybl_ref, xc_ref, xl_ref, mod_ref, wp_ref, ps_ref, wo_ref,
                     gffn_ref, wr_ref, *refs):
    outs, (pad_ref, pool_ref), scratch = refs[:N_TAIL_OUT], refs[N_TAIL_OUT:-N_TAIL_SCRATCH], refs[-N_TAIL_SCRATCH:]
    i = pl.program_id(0)
    r = _mod_row(i)
    off, n, first, last = _seq_info(i)
    a = a_ref[...]
    ap = jnp.where(first, 0.0, ap_ref[...])
    an = jnp.where(last, 0.0, an_ref[...])
    half_rows = TM // 2
    pos = [(off + 2 * lax.broadcasted_iota(I32, (half_rows, 1), 0) + parity).astype(F32) for parity in (0, 1)]
    last_pos = (n - 1).astype(F32)
    mixed = []
    for g, w in enumerate(POOL_WINDOWS):
        lo = w // 2
        hi = w - lo - 1
        cols = slice(g * POOL_GROUP_DIM, (g + 1) * POOL_GROUP_DIM)
        pad_ref[g, 0:POOL_HALO, :] = ap[:, cols]
        pad_ref[g, POOL_HALO:POOL_HALO + TM, :] = a[:, cols]
        pad_ref[g, POOL_HALO + TM:, :] = an[:, cols]
        loads = {s: pad_ref[g, pl.ds(s, half_rows, stride=2), :]
                 for s in range(POOL_HALO - lo, POOL_HALO + hi + 2)}
        for parity in (0, 1):
            total = functools.reduce(lambda x, y: x + y,
                                     [loads[POOL_HALO + parity + j] for j in range(-lo, hi + 1)])
            count = jnp.minimum(pos[parity] + hi, last_pos) - jnp.maximum(pos[parity] - lo, 0.0) + 1.0
            pool_ref[g, pl.ds(parity, half_rows, stride=2), :] = total / count - loads[POOL_HALO + parity]
        mixed.append(_dot(pool_ref[g], wp_ref[g]))
    ya = jnp.concatenate(mixed, axis=1) * ps_ref[...]
    y_pool = _dot(ya, wo_ref[0:POOL_DIM, :])

    def body(pop):
        y = y_pool + _dot((ybc_ref, ybl_ref)[pop][...], wo_ref[POOL_DIM:, :])
        _mixer_tail(l, (xc_ref, xl_ref)[pop][...], y, mod_ref, r, gffn_ref, wr_ref, outs, scratch)

    _per_population(i, body)


def _mid_even(l, e, a, yb_c, yb_l, xc, xl, mod, w_pool, pool_scale, w_out, gffn, wr):
    prev, nxt = _halo_specs(POOL_HALO, POOL_DIM)
    return pl.pallas_call(
        functools.partial(_mid_even_kernel, l),
        grid=(NT,),
        in_specs=[_tile_spec(POOL_DIM), prev, nxt] + _pop_specs(TM, Q_DIM) + _pop_specs(TM, D) + [
                  _MOD_SPEC, _layer_spec((len(POOL_WINDOWS), POOL_GROUP_DIM, POOL_GROUP_DIM), e),
                  _layer_spec((1, POOL_DIM), e), _layer_spec((D, D), e), _full_spec((DEPTH, D)),
                  _layer_spec((2, D, LANES), l)],
        out_specs=_tail_out_specs(),
        out_shape=_TAIL_OUT_SHAPES,
        scratch_shapes=[pltpu.VMEM((len(POOL_WINDOWS), TM + 2 * POOL_HALO, POOL_GROUP_DIM), F32),
                        pltpu.VMEM((len(POOL_WINDOWS), TM, POOL_GROUP_DIM), F32)] + _tail_scratch(),
        compiler_params=_params(),
        name="mid_even",
    )(a, a, a, yb_c, yb_l, xc, xl, mod, w_pool, pool_scale, w_out, gffn, wr)


def _after_moe(l, r, x1_ref, moe_ref, mod_ref):
    return x1_ref[...] + mod_ref[l, pl.ds(r, 1), 5 * D:6 * D] * _from_token_tiles(moe_ref)


def _in_odd_kernel(l, x1_ref, moec_ref, moel_ref, mod_ref, g_ref, w1_ref, b1_ref, u_ref):
    i = pl.program_id(0) * PAIR
    r = _mod_row(i)

    def body(pop):
        x2 = _after_moe(l - 1, r, x1_ref, (moec_ref, moel_ref)[pop], mod_ref)
        shift = mod_ref[l, pl.ds(r, 1), 0:D]
        scale = mod_ref[l, pl.ds(r, 1), D:2 * D]
        h = _rms_mod(x2, g_ref[l:l + 1, :], scale, shift)
        u = _dot(h, w1_ref[...]) + b1_ref[...]
        u_ref[...] = u[:, :D] * (1.0 / (1.0 + jnp.exp(-u[:, D:])))

    _per_half(i, body)


def _in_odd(l, o, x1, moe_c, moe_l, mod, g, w1, b1):
    return pl.pallas_call(
        functools.partial(_in_odd_kernel, l),
        grid=(NT // PAIR,),
        in_specs=[_pair_spec(D)] + _pair_half_specs() + [
                  _MOD_SPEC, _full_spec((DEPTH, D)), _layer_spec((D, 2 * D), o), _layer_spec((1, 2 * D), o)],
        out_specs=_pair_spec(D),
        out_shape=jax.ShapeDtypeStruct((T_ALL, D), F32),
        compiler_params=_params(),
        name="in_odd",
    )(x1, moe_c, moe_l, mod, g, w1, b1)


CONV_PAD_ROWS = TM + 2 * CONV_HALO
CONV_CHUNKS = D // LANES
CONV_BLOCK = 64


def _mid_odd_kernel(l, u_ref, up_ref, un_ref, x1_ref, moec_ref, moel_ref, mod_ref, dw_ref, dwb_ref, lng_ref, lnb_ref,
                    w2_ref, b2_ref, gffn_ref, wr_ref, *refs):
    outs, scratch = refs[:N_TAIL_OUT], refs[-N_TAIL_SCRATCH:]
    pad_ref, conv_ref, x_ref = refs[N_TAIL_OUT:-N_TAIL_SCRATCH]
    i = pl.program_id(0)

    def residual(pop):
        x_ref[...] = _after_moe(l - 1, _mod_row(i), x1_ref, (moec_ref, moel_ref)[pop], mod_ref)

    _per_half(i, residual)
    r = _mod_row(i)
    _, _, first, last = _seq_info(i)
    u = u_ref[...]
    up = jnp.where(first, 0.0, up_ref[...])
    un = jnp.where(last, 0.0, un_ref[...])
    half = CONV_WIDTH // 2
    first_row = CONV_HALO - half
    for c in range(CONV_CHUNKS):
        cols = slice(c * LANES, (c + 1) * LANES)
        pad_ref[c, 0:CONV_HALO, :] = up[:, cols]
        pad_ref[c, CONV_HALO:CONV_HALO + TM, :] = u[:, cols]
        pad_ref[c, CONV_HALO + TM:, :] = un[:, cols]
        bias = jnp.broadcast_to(dwb_ref[:, cols], (CONV_BLOCK, LANES))
        for r0 in range(0, TM, 2 * CONV_BLOCK):
            acc_even, acc_odd = bias, bias
            for s in range(first_row, first_row + CONV_WIDTH + 1):
                win = pad_ref[c, pl.ds(r0 + s, CONV_BLOCK, stride=2), :]
                j = s - first_row
                if j < CONV_WIDTH:
                    acc_even = acc_even + win * dw_ref[j:j + 1, cols]
                if j >= 1:
                    acc_odd = acc_odd + win * dw_ref[j - 1:j, cols]
            conv_ref[c, pl.ds(r0, CONV_BLOCK, stride=2), :] = acc_even
            conv_ref[c, pl.ds(r0 + 1, CONV_BLOCK, stride=2), :] = acc_odd
    acc = jnp.concatenate([conv_ref[c] for c in range(CONV_CHUNKS)], axis=1)
    mu = jnp.mean(acc, axis=-1, keepdims=True)
    cen = acc - mu
    var = jnp.mean(cen * cen, axis=-1, keepdims=True)
    v = _silu(cen * lax.rsqrt(var + EPS) * lng_ref[...] + lnb_ref[...])
    y = _dot(v, w2_ref[...]) + b2_ref[...]
    _mixer_tail(l, x_ref[...], y, mod_ref, r, gffn_ref, wr_ref, outs, scratch)


def _mid_odd(l, o, u, x1, moe_c, moe_l, mod, dw, dwb, lng, lnb, w2, b2, gffn, wr):
    prev, nxt = _halo_specs(CONV_HALO, D)
    return pl.pallas_call(
        functools.partial(_mid_odd_kernel, l),
        grid=(NT,),
        in_specs=[_tile_spec(D), prev, nxt, _tile_spec(D)] + _half_specs(TM * TOK_ROWS, LANES) + [
                  _MOD_SPEC, _layer_spec((CONV_WIDTH, D), o),
                  _layer_spec((1, D), o), _layer_spec((1, D), o), _layer_spec((1, D), o),
                  _layer_spec((D, D), o), _layer_spec((1, D), o), _full_spec((DEPTH, D)),
                  _layer_spec((2, D, LANES), l)],
        out_specs=_tail_out_specs(),
        out_shape=_TAIL_OUT_SHAPES,
        scratch_shapes=[pltpu.VMEM((CONV_CHUNKS, CONV_PAD_ROWS, LANES), F32),
                        pltpu.VMEM((CONV_CHUNKS, TM, LANES), F32), pltpu.VMEM((TM, D), F32)] + _tail_scratch(),
        compiler_params=_params(),
        name="mid_odd",
    )(u, u, u, x1, moe_c, moe_l, mod, dw, dwb, lng, lnb, w2, b2, gffn, wr)


def _tile_tables_kernel(cnt_ref, te_c, tc_c, tw_c, te_l, tc_l, tw_l, nact_ref, second_ref, first_ref, wplan_ref):
    for p, (te_ref, tc_ref, tw_ref) in enumerate(((te_c, tc_c, tw_c), (te_l, tc_l, tw_l))):
        run = jnp.int32(0)
        ends = []
        for e in range(N_EXPERTS):
            first_ref[p * N_EXPERTS + e] = run
            run = run + (cnt_ref[p, e] + MOE_TM - 1) // MOE_TM
            ends.append(run)
        nact_ref[p] = run

        nxt1, nxt2 = jnp.int32(0), jnp.int32(0)
        plans = []
        for e in reversed(range(N_EXPERTS)):
            plans.append(nxt2)
            used = cnt_ref[p, e] > 0
            nxt2 = jnp.where(used, nxt1, nxt2)
            nxt1 = jnp.where(used, e + 1, nxt1)
        plans.reverse()
        second_ref[p] = nxt2
        order = jnp.int32(0)
        for e in range(N_EXPERTS):
            wplan_ref[p * N_EXPERTS + e] = order + (plans[e] << W_AHEAD_SHIFT)
            order = jnp.where(cnt_ref[p, e] > 0, jnp.where(order == W_BUFFERS - 1, 0, order + 1), order)

        def expert_of(tile, ends=ends):
            e = jnp.int32(0)
            for k in range(N_EXPERTS):
                e = e + jnp.where(tile >= ends[k], 1, 0)
            return e

        last_e = expert_of(run - 1)

        def tile_body(j, c, te_ref=te_ref, tc_ref=tc_ref, tw_ref=tw_ref, p=p, run=run, expert_of=expert_of,
                      last_e=last_e):
            e = jnp.minimum(expert_of(j), last_e)
            te_ref[j] = e
            chunk = jnp.where(j < run, j - first_ref[p * N_EXPERTS + e], 0)
            tc_ref[j] = chunk
            starts_expert = jnp.logical_and(j < run, chunk == 0)
            tw_ref[j] = wplan_ref[p * N_EXPERTS + e] + jnp.where(starts_expert, W_FIRST_UNIT, 0)
            return c

        lax.fori_loop(0, te_ref.shape[0], tile_body, 0)


def _tile_tables(cnt):
    smem = pl.BlockSpec(memory_space=pltpu.SMEM)
    shapes = [(MOE_TILES,)] * 6 + [(2,), (2,)]
    return pl.pallas_call(
        _tile_tables_kernel,
        in_specs=[smem],
        out_specs=[smem] * len(shapes),
        out_shape=[jax.ShapeDtypeStruct(s, I32) for s in shapes],
        scratch_shapes=[pltpu.SMEM((2 * N_EXPERTS,), I32), pltpu.SMEM((2 * N_EXPERTS,), I32)],
        name="tile_tables",
    )(cnt)


def _moe_kernel(l, pop, te_ref, tc_ref, tw_ref, nact_ref, second_ref, list_ref, h_ref, gate_ref,
                w1_hbm, w3_hbm, w2_hbm, o_ref, xs_ref, ys_ref, w1_buf, w3_buf, w2_buf, w_sem):
    j = pl.program_id(0)
    nact = nact_ref[pop]
    last_tile = te_ref.shape[0] - 1
    tile_rows = MOE_TM * TOK_ROWS

    def weight_copies(expert, buf):
        return [pltpu.make_async_copy(hbm.at[l, expert], vmem.at[buf], w_sem.at[buf, k])
                for k, (hbm, vmem) in enumerate(((w1_hbm, w1_buf), (w3_hbm, w3_buf), (w2_hbm, w2_buf)))]

    def list_base(tile):
        expert = te_ref[tile]
        block = expert >> LIST_PER_BLOCK_LOG2
        in_block = expert & ((1 << LIST_PER_BLOCK_LOG2) - 1)
        return (block * LIST_ROWS + tc_ref[tile] * (MOE_TM // LIST_SPAN)) * LANES + in_block * LIST_SPAN

    def tok_rows(base, r):
        first = list_ref[base + (r // LIST_SPAN) * LANES + r % LIST_SPAN]
        return pl.ds(pl.multiple_of(first, TOK_ROWS), TOK_ROWS)

    def buf_rows(buf, r):
        return pl.ds(pl.multiple_of(buf * tile_rows + r * TOK_ROWS, TOK_ROWS), TOK_ROWS)

    def gather(tile, buf):
        base = list_base(tile)
        for r in range(MOE_TM):
            xs_ref[buf_rows(buf, r), :] = h_ref[tok_rows(base, r), :]

    tile = jnp.minimum(j, last_tile)
    plan = tw_ref[tile]
    wbuf = plan & (W_FIRST_UNIT - 1)
    starts_expert = jnp.logical_and(plan & W_FIRST_UNIT != 0, j <= last_tile)
    ahead_expert = (plan >> W_AHEAD_SHIFT) - 1
    ahead_buf = jnp.where(wbuf == 0, W_BUFFERS - 1, wbuf - 1)

    @pl.when(j == 0)
    def _():
        for cp in weight_copies(te_ref[0], 0):
            cp.start()

        @pl.when(second_ref[pop] > 0)
        def _():
            for cp in weight_copies(second_ref[pop] - 1, 1):
                cp.start()

        o_ref[...] = jnp.zeros(o_ref.shape, F32)
        ys_ref[...] = jnp.zeros(ys_ref.shape, F32)
        gather(0, 0)

    @pl.when(starts_expert)
    def _():
        for cp in weight_copies(te_ref[tile], wbuf):
            cp.wait()

        @pl.when(ahead_expert >= 0)
        def _():
            for cp in weight_copies(ahead_expert, ahead_buf):
                cp.start()

    @pl.when(j <= nact)
    def _():
        cur = j & 1
        w1_ref, w3_ref, w2_ref = w1_buf.at[wbuf], w3_buf.at[wbuf], w2_buf.at[wbuf]
        x = jnp.concatenate([xs_ref[pl.ds(cur * tile_rows + s, MOE_TM, stride=TOK_ROWS), :]
                             for s in range(TOK_ROWS)], axis=1).astype(BF16)
        gather(jnp.minimum(j + 1, last_tile), 1 - cur)

        prev_base = list_base(jnp.maximum(j - 1, 0))
        for r0 in range(0, MOE_TM, MOVE_BATCH):
            dst = [tok_rows(prev_base, r0 + k) for k in range(MOVE_BATCH)]
            vals = [o_ref[dst[k], :] + ys_ref[buf_rows(1 - cur, r0 + k), :] for k in range(MOVE_BATCH)]
            for k in reversed(range(MOVE_BATCH)):
                o_ref[dst[k], :] = vals[k]

        hg = jnp.dot(x, w1_ref[...].astype(BF16), preferred_element_type=F32)
        hu = jnp.dot(x, w3_ref[...].astype(BF16), preferred_element_type=F32)
        y = _dot(_silu(hg) * hu, w2_ref[...])

        first_row = pl.multiple_of(tc_ref[tile] * (MOE_TM // LIST_SPAN), MOE_TM // LIST_SPAN)
        spread = jnp.concatenate(
            [jnp.broadcast_to(gate_ref[pl.ds(first_row + k, 1), :], (LIST_SPAN, LIST_COLS))
             for k in range(MOE_TM // LIST_SPAN)], axis=0)
        want = te_ref[tile] * LIST_SPAN + (lax.broadcasted_iota(I32, (MOE_TM, 1), 0) & (LIST_SPAN - 1))
        hit = jnp.logical_and(lax.broadcasted_iota(I32, (MOE_TM, LIST_COLS), 1) == want, j < nact)
        y = y * jnp.sum(jnp.where(hit, spread, 0.0), axis=1, keepdims=True)
        for s in range(TOK_ROWS):
            ys_ref[pl.ds(cur * tile_rows + s, MOE_TM, stride=TOK_ROWS), :] = y[:, s * LANES:(s + 1) * LANES]


def _moe(l, pop, h_tt, te, tc, tw, nact, second, rows, gate, w1, w3, w2):
    tp = T_HALF
    blk = pop
    n_tiles = te.shape[0]
    hbm = pl.BlockSpec(memory_space=pl.ANY)
    grid_spec = pltpu.PrefetchScalarGridSpec(
        num_scalar_prefetch=6,
        grid=(n_tiles + 1,),
        in_specs=[
            pl.BlockSpec((tp * TOK_ROWS, LANES), lambda j, *_: (blk, 0), pipeline_mode=pl.Buffered(1)),
            pl.BlockSpec((LIST_ROWS, LIST_COLS), lambda j, *_: (0, 0), pipeline_mode=pl.Buffered(1)),
            hbm, hbm, hbm,
        ],
        out_specs=pl.BlockSpec((tp * TOK_ROWS, LANES), lambda j, *_: (0, 0), pipeline_mode=pl.Buffered(1)),
        scratch_shapes=[pltpu.VMEM((2 * MOE_TM * TOK_ROWS, LANES), F32),
                        pltpu.VMEM((2 * MOE_TM * TOK_ROWS, LANES), F32),
                        pltpu.VMEM((W_BUFFERS, D, EXPERT_HIDDEN), F32),
                        pltpu.VMEM((W_BUFFERS, D, EXPERT_HIDDEN), F32),
                        pltpu.VMEM((W_BUFFERS, EXPERT_HIDDEN, D), F32),
                        pltpu.SemaphoreType.DMA((W_BUFFERS, 3))],
    )
    return pl.pallas_call(
        functools.partial(_moe_kernel, l, pop),
        grid_spec=grid_spec,
        out_shape=jax.ShapeDtypeStruct((tp * TOK_ROWS, LANES), F32),
        compiler_params=_params(),
        name="moe",
    )(te, tc, tw, nact, second, rows, h_tt, gate, w1, w3, w2)


def _moe_all(l, h_tt, cnt, rows_c, gate_c, rows_l, gate_l, w1, w3, w2):
    te_c, tc_c, tw_c, te_l, tc_l, tw_l, nact, second = _tile_tables(cnt)
    out_c = _moe(l, 0, h_tt, te_c, tc_c, tw_c, nact, second, rows_c.reshape(-1), gate_c, w1, w3, w2)
    out_l = _moe(l, 1, h_tt, te_l, tc_l, tw_l, nact, second, rows_l.reshape(-1), gate_l, w1, w3, w2)
    return out_c, out_l


def _final_kernel(l, x1_ref, moec_ref, moel_ref, mod_ref, g_ref, oc_ref, ol_ref):
    i = pl.program_id(0) * PAIR
    r = _mod_row(i)

    def body(half, pop):
        x = _after_moe(l, r, x1_ref, (moec_ref, moel_ref)[half], mod_ref)
        ms = jnp.mean(x * x, axis=-1, keepdims=True)
        (oc_ref, ol_ref)[pop][...] = x * lax.rsqrt(ms + EPS) * g_ref[...]

    pl.when(i < MOE_SPLIT)(functools.partial(body, 0, 0))
    pl.when(jnp.logical_and(i >= MOE_SPLIT, i < NT_CTX))(functools.partial(body, 1, 0))
    pl.when(i >= NT_CTX)(functools.partial(body, 1, 1))


def _final(l, x1, moe_c, moe_l, mod, g):
    return pl.pallas_call(
        functools.partial(_final_kernel, l),
        grid=(NT // PAIR,),
        in_specs=[_pair_spec(D)] + _pair_half_specs() + [_MOD_SPEC, _full_spec((1, D))],
        out_specs=_pair_pop_specs(D),
        out_shape=[jax.ShapeDtypeStruct((T_CTX, D), F32), jax.ShapeDtypeStruct((T_LAT, D), F32)],
        compiler_params=_params(),
        name="final_norm",
    )(x1, moe_c, moe_l, mod, g)


def _rope_tables():
    t = np.arange(LAT_LEN)
    row = (t // GRID_W).astype(np.float32)
    col = (t % GRID_W).astype(np.float32)
    inv = np.float32(ROPE_BASE) ** (-np.arange(0, ROPE_AXIS_DIM, 2, dtype=np.float32) / np.float32(ROPE_AXIS_DIM))

    def table(p):
        ang = (p[:, None] * inv[None, :]).astype(np.float32)
        ang = np.concatenate([ang, ang], axis=-1)
        return np.cos(ang).astype(np.float32), np.sin(ang).astype(np.float32)

    (cr, sr), (cc, sc) = table(row), table(col)
    cos = np.concatenate([cr, cc, cr, cc], axis=-1)
    sin = np.concatenate([sr, sc, sr, sc], axis=-1)
    return jnp.asarray(cos), jnp.asarray(sin)


def _router_weights(w_grp, w_exp):
    we = jnp.transpose(w_exp, (0, 2, 1, 3)).reshape(DEPTH, D, N_GROUPS * EPG)
    pad = jnp.zeros((DEPTH, D, LANES - N_GROUPS - N_GROUPS * EPG), F32)
    wr = jnp.concatenate([w_grp, we, pad], axis=-1)
    hi = lax.bitcast_convert_type(lax.bitcast_convert_type(wr, jnp.uint32) & jnp.uint32(0xFFFF0000), F32)
    return jnp.stack([hi.astype(BF16), (wr - hi).astype(BF16)], axis=1)


def kernel(x_prompt, x_sample, cache_k, cache_v, c, c_ctx, w_ada, b_ada, norm_mix_g, norm_ffn_g, w_in_ab, pool_w, pool_scale, attn_sink, w_out_ab, conv_w1, conv_b1, conv_dw, conv_dw_b, conv_ln_g, conv_ln_b, conv_w2, conv_b2, router_grp, router_exp, moe_w1, moe_w3, moe_w2, final_g):
    xc = x_prompt.reshape(T_CTX, D)
    xl = x_sample.reshape(T_LAT, D)
    cond8 = jnp.concatenate([c_ctx[None, :], c, jnp.zeros((SUBLANES - 1 - N_LAT_SEQ, D), F32)], axis=0)
    mod = _modulation(cond8, w_ada, b_ada)
    cos_t, sin_t = _rope_tables()
    wr = _router_weights(router_grp, router_exp)
    n_even, n_odd = w_in_ab.shape[0], conv_w1.shape[0]

    a, q, k_c, k_l, v_c, v_l, state_k, state_v = _in_even(0, 0, xc, xl, mod, norm_mix_g, w_in_ab, cos_t, sin_t)
    yb_c = _attn_ctx(0, attn_sink, q, k_c, v_c)
    yb_l = _attn_lat(0, attn_sink, q, k_l, v_l, cache_k, cache_v)
    x1, h2, *routing = _mid_even(0, 0, a, yb_c, yb_l, xc, xl, mod, pool_w,
                                 pool_scale.reshape(n_even, 1, POOL_DIM), w_out_ab, norm_ffn_g, wr)
    moe0 = _moe_all(0, h2, *routing, moe_w1, moe_w3, moe_w2)

    vec = lambda p: p.reshape(n_odd, 1, -1)
    u = _in_odd(1, 0, x1, *moe0, mod, norm_mix_g, conv_w1, vec(conv_b1))
    x3, h2b, *routing = _mid_odd(1, 0, u, x1, *moe0, mod, conv_dw, vec(conv_dw_b), vec(conv_ln_g),
                                 vec(conv_ln_b), conv_w2, vec(conv_b2), norm_ffn_g, wr)
    moe1 = _moe_all(1, h2b, *routing, moe_w1, moe_w3, moe_w2)
    y_c, y_l = _final(1, x3, *moe1, mod, final_g.reshape(1, D))

    y_prompt = y_c.reshape(N_CTX_SEQ, CTX_LEN, D)
    y_sample = y_l.reshape(N_LAT_SEQ, LAT_LEN, D)
    return (y_prompt, y_sample, state_k, state_v)
```

```python
import functools

import jax
import jax.numpy as jnp
import numpy as np
from jax import lax
from jax.experimental import pallas as pl
from jax.experimental.pallas import tpu as pltpu

F32 = jnp.float32
BF16 = jnp.bfloat16
I32 = jnp.int32

D = 1024
N_CTX_SEQ = 16
CTX_LEN = 256
N_LAT_SEQ = 2
LAT_LEN = 1024
T_CTX = N_CTX_SEQ * CTX_LEN
T_LAT = N_LAT_SEQ * LAT_LEN
T_ALL = T_CTX + T_LAT
TM = 256
NT = T_ALL // TM
NT_CTX = T_CTX // TM
LAT_TILES_PER_SEQ = LAT_LEN // TM
GRID_W = 64
DEPTH = 2

POOL_WINDOWS = (2, 4, 8, 16)
POOL_GROUP_DIM = 128
POOL_DIM = 512
HEAD_DIM = 64
N_Q_HEADS = 8
N_KV_HEADS = 2
Q_PER_KV = 4
Q_DIM = 512
KV_DIM = 128
IN_AB = 1280
ATTN_WINDOW = 128
ATTN_BLOCK = 128
ATTN_SCALE = HEAD_DIM ** -0.5
ROPE_BASE = 10000.0
ROPE_AXIS_DIM = 32
CONV_WIDTH = 31
CONV_HALO = 16
POOL_HALO = 8
N_GROUPS = 4
EPG = 4
N_EXPERTS = 16
EXPERT_HIDDEN = 512
EPS = 1e-6
NEG_BIG = -1e30

SUBLANES = 8
LANES = 128
TOK_ROWS = D // LANES

MOE_TM = 256
MOE_SPLIT = NT // 2
T_HALF = MOE_SPLIT * TM
MOE_TILES = 2 * T_HALF // MOE_TM + N_EXPERTS
W_BUFFERS = 3
W_FIRST_UNIT = 4
W_AHEAD_SHIFT = 3
MOVE_BATCH_LOG2 = 2
MOVE_BATCH = 1 << MOVE_BATCH_LOG2
VMEM_LIMIT = 56 * 1024 * 1024


def _silu(x):
    return x * (1.0 / (1.0 + jnp.exp(-x)))


def _mod_row(i):
    return jnp.where(i < NT_CTX, 0, 1 + (i - NT_CTX) // LAT_TILES_PER_SEQ)


def _seq_info(i):
    is_ctx = i < NT_CTX
    k = (i - NT_CTX) % LAT_TILES_PER_SEQ
    off = jnp.where(is_ctx, 0, k * TM)
    n = jnp.where(is_ctx, CTX_LEN, LAT_LEN)
    first = jnp.logical_or(is_ctx, k == 0)
    last = jnp.logical_or(is_ctx, k == LAT_TILES_PER_SEQ - 1)
    return off, n, first, last


def _rms_mod(x, g, scale, shift):
    ms = jnp.mean(x * x, axis=-1, keepdims=True)
    return (x * lax.rsqrt(ms + EPS) * g) * (1.0 + scale) + shift


def _dot(a, b):
    return jnp.dot(a.astype(BF16), b.astype(BF16), preferred_element_type=F32)


def _dot_nt(a, b):
    return lax.dot_general(a.astype(BF16), b.astype(BF16), (((1,), (1,)), ((), ())),
                           preferred_element_type=F32)


def _from_token_tiles(ref):
    rows = ref.shape[0] // TOK_ROWS
    return jnp.concatenate([ref[pl.ds(s, rows, stride=TOK_ROWS), :] for s in range(TOK_ROWS)], axis=1)


def _tile_spec(width):
    return pl.BlockSpec((TM, width), lambda i: (i, 0))


def _full_spec(shape):
    return pl.BlockSpec(shape, lambda i: (0,) * len(shape))


def _layer_spec(shape, l):
    return pl.BlockSpec((None,) + tuple(shape), lambda i: (l,) + (0,) * len(shape))


def _pop_specs(rows, width):
    return [pl.BlockSpec((rows, width), lambda i: (jnp.minimum(i, NT_CTX - 1), 0)),
            pl.BlockSpec((rows, width), lambda i: (jnp.maximum(i - NT_CTX, 0), 0))]


def _per_population(i, body):
    pl.when(i < NT_CTX)(functools.partial(body, 0))
    pl.when(i >= NT_CTX)(functools.partial(body, 1))


def _half_specs(rows, width):
    return [pl.BlockSpec((rows, width), lambda i: (jnp.minimum(i, MOE_SPLIT - 1), 0)),
            pl.BlockSpec((rows, width), lambda i: (jnp.maximum(i - MOE_SPLIT, 0), 0))]


def _per_half(i, body):
    pl.when(i < MOE_SPLIT)(functools.partial(body, 0))
    pl.when(i >= MOE_SPLIT)(functools.partial(body, 1))


PAIR = 2
assert LAT_TILES_PER_SEQ % PAIR == 0 and MOE_SPLIT % PAIR == 0 and NT_CTX % PAIR == 0


def _pair_spec(width):
    return pl.BlockSpec((PAIR * TM, width), lambda i: (i, 0))


def _pair_half_specs():
    split = MOE_SPLIT // PAIR
    block = (PAIR * TM * TOK_ROWS, LANES)
    return [pl.BlockSpec(block, lambda i: (jnp.minimum(i, split - 1), 0)),
            pl.BlockSpec(block, lambda i: (jnp.maximum(i - split, 0), 0))]


def _pair_pop_specs(width):
    n_ctx = NT_CTX // PAIR
    return [pl.BlockSpec((PAIR * TM, width), lambda i: (jnp.minimum(i, n_ctx - 1), 0)),
            pl.BlockSpec((PAIR * TM, width), lambda i: (jnp.maximum(i - n_ctx, 0), 0))]


_MOD_SPEC = pl.BlockSpec((DEPTH, SUBLANES, 6 * D), lambda i: (0, 0, 0))


def _params():
    return pltpu.CompilerParams(vmem_limit_bytes=VMEM_LIMIT)


MOD_TN = 1536


def _mod_kernel(cond_ref, w_ref, b_ref, o_ref):
    s = _silu(cond_ref[...])
    o_ref[0] = _dot(s, w_ref[0]) + b_ref[0]


def _modulation(cond8, w_ada, b_ada):
    return pl.pallas_call(
        _mod_kernel,
        grid=(DEPTH, 6 * D // MOD_TN),
        in_specs=[
            pl.BlockSpec((SUBLANES, D), lambda l, n: (0, 0)),
            pl.BlockSpec((1, D, MOD_TN), lambda l, n: (l, 0, n)),
            pl.BlockSpec((1, 1, MOD_TN), lambda l, n: (l, 0, n)),
        ],
        out_specs=pl.BlockSpec((1, SUBLANES, MOD_TN), lambda l, n: (l, 0, n)),
        out_shape=jax.ShapeDtypeStruct((DEPTH, SUBLANES, 6 * D), F32),
        compiler_params=_params(),
        name="modulation",
    )(cond8, w_ada, b_ada.reshape(DEPTH, 1, 6 * D))


def _rope_chunk(xc, cos, sin):
    lane = lax.broadcasted_iota(I32, xc.shape, 1)
    first = (lane % ROPE_AXIS_DIM) < (ROPE_AXIS_DIM // 2)
    rot = jnp.where(first, -pltpu.roll(xc, LANES - ROPE_AXIS_DIM // 2, 1), pltpu.roll(xc, ROPE_AXIS_DIM // 2, 1))
    return xc * cos + rot * sin


def _in_even_kernel(l, xc_ref, xl_ref, mod_ref, g_ref, w_ref, cos_ref, sin_ref,
                    a_ref, q_ref, kc_ref, kl_ref, vc_ref, vl_ref, sk_ref, sv_ref):
    i = pl.program_id(0) * PAIR
    r = _mod_row(i)

    def put_state(s_ref, val):
        for sq in range(PAIR):
            for hd in range(N_KV_HEADS):
                s_ref[sq, 0, :, hd, :] = val[sq * TM:(sq + 1) * TM, hd * HEAD_DIM:(hd + 1) * HEAD_DIM]

    def body(pop):
        x = (xc_ref, xl_ref)[pop][...]
        k_ref, v_ref = (kc_ref, kl_ref)[pop], (vc_ref, vl_ref)[pop]
        shift = mod_ref[l, pl.ds(r, 1), 0:D]
        scale = mod_ref[l, pl.ds(r, 1), D:2 * D]
        h = _rms_mod(x, g_ref[l:l + 1, :], scale, shift)
        proj = _dot(h, w_ref[...])
        a_ref[...] = proj[:, :POOL_DIM]
        v_ref[...] = proj[:, POOL_DIM + Q_DIM + KV_DIM:]
        if pop == 0:
            q_ref[...] = proj[:, POOL_DIM:POOL_DIM + Q_DIM].astype(BF16)
            k_ref[...] = proj[:, POOL_DIM + Q_DIM:POOL_DIM + Q_DIM + KV_DIM]
            put_state(sk_ref, proj[:, POOL_DIM + Q_DIM:POOL_DIM + Q_DIM + KV_DIM])
            put_state(sv_ref, proj[:, POOL_DIM + Q_DIM + KV_DIM:])
        else:
            off, _, _, _ = _seq_info(i)
            off = pl.multiple_of(off, TM)
            cos = cos_ref[pl.ds(off, PAIR * TM), :]
            sin = sin_ref[pl.ds(off, PAIR * TM), :]
            for c in range(Q_DIM // LANES):
                lo = POOL_DIM + c * LANES
                q_ref[:, c * LANES:(c + 1) * LANES] = _rope_chunk(proj[:, lo:lo + LANES], cos, sin).astype(BF16)
            lo = POOL_DIM + Q_DIM
            k_ref[...] = _rope_chunk(proj[:, lo:lo + LANES], cos, sin)

    _per_population(i, body)


def _in_even(l, e, xc, xl, mod, g, w_in, cos_t, sin_t):
    assert CTX_LEN == TM
    state_shape = (N_CTX_SEQ, 1, CTX_LEN, N_KV_HEADS, HEAD_DIM)
    state_spec = pl.BlockSpec((PAIR, 1, CTX_LEN, N_KV_HEADS, HEAD_DIM),
                              lambda i: (jnp.minimum(i, NT_CTX // PAIR - 1), 0, 0, 0, 0))
    return pl.pallas_call(
        functools.partial(_in_even_kernel, l),
        grid=(NT // PAIR,),
        in_specs=_pair_pop_specs(D) + [_MOD_SPEC, _full_spec((DEPTH, D)), _layer_spec((D, IN_AB), e),
                  _full_spec((LAT_LEN, LANES)), _full_spec((LAT_LEN, LANES))],
        out_specs=([_pair_spec(POOL_DIM), _pair_spec(Q_DIM)] + _pair_pop_specs(KV_DIM) + _pair_pop_specs(KV_DIM)
                   + [state_spec, state_spec]),
        out_shape=[jax.ShapeDtypeStruct((T_ALL, POOL_DIM), F32), jax.ShapeDtypeStruct((T_ALL, Q_DIM), BF16)]
                  + [jax.ShapeDtypeStruct((t, KV_DIM), F32) for t in (T_CTX, T_LAT, T_CTX, T_LAT)]
                  + [jax.ShapeDtypeStruct(state_shape, F32)] * 2,
        compiler_params=_params(),
        name="in_even",
    )(xc, xl, mod, g, w_in, cos_t, sin_t)


def _attn_ctx_kernel(e, sink_ref, q_ref, k_ref, v_ref, o_ref):
    q = q_ref[...]
    k = k_ref[...]
    v = v_ref[...]
    for h in range(N_KV_HEADS):
        kh = k[:, h * HEAD_DIM:(h + 1) * HEAD_DIM]
        vh = v[:, h * HEAD_DIM:(h + 1) * HEAD_DIM]
        for g in range(Q_PER_KV):
            j = h * Q_PER_KV + g
            s = _dot_nt(q[:, j * HEAD_DIM:(j + 1) * HEAD_DIM], kh) * ATTN_SCALE
            sk = sink_ref[e, j]
            m = jnp.maximum(jnp.max(s, axis=-1, keepdims=True), sk)
            p = jnp.exp(s - m)
            denom = jnp.sum(p, axis=-1, keepdims=True) + jnp.exp(sk - m)
            o_ref[:, j * HEAD_DIM:(j + 1) * HEAD_DIM] = (_dot(p, vh) / denom).astype(o_ref.dtype)


def _attn_ctx(e, sink, q, k, v):
    tile = lambda w: pl.BlockSpec((TM, w), lambda b: (b, 0))
    return pl.pallas_call(
        functools.partial(_attn_ctx_kernel, e),
        grid=(N_CTX_SEQ,),
        in_specs=[pl.BlockSpec(memory_space=pltpu.SMEM), tile(Q_DIM), tile(KV_DIM), tile(KV_DIM)],
        out_specs=tile(Q_DIM),
        out_shape=jax.ShapeDtypeStruct((T_CTX, Q_DIM), BF16),
        compiler_params=_params(),
        name="attn_ctx",
    )(sink, q, k, v)


LAT_BLOCKS = LAT_LEN // ATTN_BLOCK
BAND = 3 * ATTN_BLOCK


def _attn_lat_kernel(e, sink_ref, q_ref, k_ref, v_ref, ck_ref, cv_ref, o_ref):
    n = pl.program_id(1)
    start = jnp.clip((n - 1) * ATTN_BLOCK, 0, LAT_LEN - BAND)
    start = pl.multiple_of(start, ATTN_BLOCK)
    q = q_ref[...]
    kw = k_ref[pl.ds(start, BAND), :]
    vw = v_ref[pl.ds(start, BAND), :]
    qpos = n * ATTN_BLOCK + lax.broadcasted_iota(I32, (ATTN_BLOCK, BAND), 0)
    kpos = start + lax.broadcasted_iota(I32, (ATTN_BLOCK, BAND), 1)
    valid = jnp.abs(qpos - kpos) <= ATTN_WINDOW
    for h in range(N_KV_HEADS):
        hs = slice(h * HEAD_DIM, (h + 1) * HEAD_DIM)
        ck = ck_ref[0, 0, :, h, :]
        cv = cv_ref[0, 0, :, h, :]
        for g in range(Q_PER_KV):
            j = h * Q_PER_KV + g
            qj = q[:, j * HEAD_DIM:(j + 1) * HEAD_DIM]
            s_loc = jnp.where(valid, _dot_nt(qj, kw[:, hs]) * ATTN_SCALE, NEG_BIG)
            s_ctx = _dot_nt(qj, ck) * ATTN_SCALE
            sk = sink_ref[e, j]
            m = jnp.maximum(jnp.maximum(jnp.max(s_loc, axis=-1, keepdims=True),
                                        jnp.max(s_ctx, axis=-1, keepdims=True)), sk)
            p_loc = jnp.exp(s_loc - m)
            p_ctx = jnp.exp(s_ctx - m)
            denom = (jnp.sum(p_loc, axis=-1, keepdims=True) + jnp.sum(p_ctx, axis=-1, keepdims=True)
                     + jnp.exp(sk - m))
            o = _dot(p_loc, vw[:, hs]) + _dot(p_ctx, cv)
            o_ref[:, j * HEAD_DIM:(j + 1) * HEAD_DIM] = (o / denom).astype(o_ref.dtype)


def _attn_lat(e, sink, q, k, v, ck, cv):
    past = ck.shape[2]
    cache_spec = pl.BlockSpec((1, 1, past, N_KV_HEADS, HEAD_DIM), lambda b, n: (b, e, 0, 0, 0))
    q_blk0 = T_CTX // ATTN_BLOCK
    q_spec = pl.BlockSpec((ATTN_BLOCK, Q_DIM), lambda b, n: (q_blk0 + b * LAT_BLOCKS + n, 0))
    return pl.pallas_call(
        functools.partial(_attn_lat_kernel, e),
        grid=(N_LAT_SEQ, LAT_BLOCKS),
        in_specs=[pl.BlockSpec(memory_space=pltpu.SMEM),
                  q_spec,
                  pl.BlockSpec((LAT_LEN, KV_DIM), lambda b, n: (b, 0)),
                  pl.BlockSpec((LAT_LEN, KV_DIM), lambda b, n: (b, 0)),
                  cache_spec, cache_spec],
        out_specs=pl.BlockSpec((ATTN_BLOCK, Q_DIM), lambda b, n: (b * LAT_BLOCKS + n, 0)),
        out_shape=jax.ShapeDtypeStruct((T_LAT, Q_DIM), BF16),
        compiler_params=_params(),
        name="attn_lat",
    )(sink, q, k, v, ck, cv)


def _router(h2, wr_ref):
    hi = h2.astype(BF16)
    lo = (h2 - hi.astype(F32)).astype(BF16)
    whi = wr_ref[0]
    wlo = wr_ref[1]
    dot = functools.partial(jnp.dot, preferred_element_type=F32)
    logits_t = (dot(hi, whi) + (dot(lo, whi) + dot(hi, wlo))).T
    row = lambda k: logits_t[k:k + 1, :]

    def first_max(vals):
        best = functools.reduce(jnp.maximum, vals)
        idx = jnp.full(best.shape, len(vals) - 1, F32)
        for k in reversed(range(len(vals) - 1)):
            idx = jnp.where(vals[k] == best, float(k), idx)
        return best, idx

    grp = [row(k) for k in range(N_GROUPS)]
    gmax, g_idx = first_max(grp)
    g_w = 1.0 / sum(jnp.exp(gk - gmax) for gk in grp)
    el = []
    for e in range(EPG):
        v = row(N_GROUPS + (N_GROUPS - 1) * EPG + e)
        for k in reversed(range(N_GROUPS - 1)):
            v = jnp.where(g_idx == k, row(N_GROUPS + k * EPG + e), v)
        el.append(v)
    t1, i1 = first_max(el)
    t2, i2 = first_max([jnp.where(i1 == e, -jnp.inf, el[e]) for e in range(EPG)])
    d = jnp.exp(t2 - t1)
    w1 = g_w / (1.0 + d)
    w2 = g_w * d / (1.0 + d)
    return EPG * g_idx + i1, EPG * g_idx + i2, w1, w2


LIST_SPAN_LOG2 = 5
LIST_SPAN = 1 << LIST_SPAN_LOG2
LIST_ROWS = LANES
LIST_COLS = N_EXPERTS * LIST_SPAN
LIST_BLOCKS = LIST_COLS // LANES
LIST_PER_BLOCK_LOG2 = 2


N_TAIL_OUT = 7
N_TAIL_SCRATCH = 3


def _mixer_tail(l, x, y, mod_ref, r, gffn_ref, wr_ref, outs, scratch):
    x1_ref, h2_ref, cnt_ref, lc_ref, gc_ref, ll_ref, gl_ref = outs
    run_ref, acc_ref, gacc_ref = scratch
    i = pl.program_id(0)
    g1 = mod_ref[l, pl.ds(r, 1), 2 * D:3 * D]
    shift2 = mod_ref[l, pl.ds(r, 1), 3 * D:4 * D]
    scale2 = mod_ref[l, pl.ds(r, 1), 4 * D:5 * D]
    x1 = x + g1 * y
    x1_ref[...] = x1
    h2 = _rms_mod(x1, gffn_ref[l:l + 1, :], scale2, shift2)
    for s in range(TOK_ROWS):
        h2_ref[pl.ds(s, TM, stride=TOK_ROWS), :] = h2[:, s * LANES:(s + 1) * LANES]
    e1, e2, w1_row, w2_row = _router(h2, wr_ref)

    @pl.when(jnp.logical_or(i == 0, i == MOE_SPLIT))
    def _():
        run_ref[...] = jnp.zeros(run_ref.shape, F32)
        acc_ref[...] = jnp.zeros(acc_ref.shape, F32)
        gacc_ref[...] = jnp.zeros(gacc_ref.shape, F32)

    @pl.when(i == 0)
    def _():
        cnt_ref[...] = jnp.zeros(cnt_ref.shape, I32)

    sub_i = lax.broadcasted_iota(I32, (LANES, TM), 0)
    sub = sub_i.astype(F32)
    member = jnp.where(jnp.logical_or(sub == e1, sub == e2), 1.0, 0.0)
    tri = jnp.where(lax.broadcasted_iota(I32, (TM, TM), 0) <= lax.broadcasted_iota(I32, (TM, TM), 1), 1.0, 0.0)
    csum = jnp.dot(member.astype(BF16), tri.astype(BF16), preferred_element_type=F32)
    run = run_ref[:, 0:1]
    before = csum - member + run
    run_new = run + csum[:, TM - 1:TM]
    run_ref[...] = jnp.broadcast_to(run_new, run_ref.shape)

    col_i = lax.broadcasted_iota(I32, (TM, LIST_COLS), 1)
    rows_oh, col_rows = [], []
    for e in (e1, e2):
        rank = jnp.zeros((1, TM), F32)
        for k in range(N_EXPERTS):
            rank = rank + jnp.where(e == k, before[k:k + 1, :], 0.0)
        rank_i = rank.astype(I32)
        rows_oh.append(jnp.where(sub_i == (rank_i >> LIST_SPAN_LOG2), 1.0, 0.0))
        col_rows.append(e.astype(I32) * LIST_SPAN + (rank_i & (LIST_SPAN - 1)))
    rows_t = jnp.concatenate(rows_oh, axis=1)
    col_cols = jnp.concatenate(col_rows + [jnp.zeros((LANES - 2, TM), I32)], axis=0).T
    cols = jnp.concatenate([jnp.where(col_i == col_cols[:, k:k + 1], 1.0, 0.0).astype(BF16) for k in range(2)],
                           axis=0)
    tok = (lax.broadcasted_iota(I32, (1, 2 * TM), 1) % TM).astype(F32)
    rest = jnp.concatenate([w1_row, w2_row], axis=1)
    values = [None, tok]
    for _ in range(3):
        piece = rest.astype(BF16).astype(F32)
        values.append(piece)
        rest = rest - piece
    planes = jnp.concatenate([rows_t if v is None else rows_t * v for v in values], axis=0).astype(BF16)
    out = jnp.dot(planes, cols, preferred_element_type=F32)
    hits, tok_sum, gp0, gp1, gp2 = [out[k * LANES:(k + 1) * LANES] for k in range(len(values))]
    tile_in_pop = jnp.where(i < MOE_SPLIT, i, i - MOE_SPLIT).astype(F32)
    acc = acc_ref[...] + TOK_ROWS * (tok_sum + TM * tile_in_pop * hits)
    acc_ref[...] = acc
    gacc = gacc_ref[...] + gp0 + gp1 + gp2
    gacc_ref[...] = gacc

    def counts_row():
        return jnp.broadcast_to(run_new, (LANES, LANES)).T[0:1, :].astype(I32)

    def put_rows(dst_ref):
        for b in range(LIST_BLOCKS):
            dst_ref[b * LIST_ROWS:(b + 1) * LIST_ROWS, :] = acc[:, b * LANES:(b + 1) * LANES].astype(I32)

    @pl.when(i == MOE_SPLIT - 1)
    def _():
        cnt_ref[0:1, :] = counts_row()
        put_rows(lc_ref)
        gc_ref[...] = gacc

    @pl.when(i == NT - 1)
    def _():
        cnt_ref[1:2, :] = counts_row()
        put_rows(ll_ref)
        gl_ref[...] = gacc


_LIST_OUT_SHAPES = [(LIST_BLOCKS * LIST_ROWS, LANES), (LIST_ROWS, LIST_COLS)] * 2
_TAIL_OUT_SHAPES = [jax.ShapeDtypeStruct((T_ALL, D), F32),
                    jax.ShapeDtypeStruct((T_ALL * TOK_ROWS, LANES), F32),
                    jax.ShapeDtypeStruct((SUBLANES, LANES), I32)] + [
                        jax.ShapeDtypeStruct(s, dt) for s, dt in zip(_LIST_OUT_SHAPES, (I32, F32, I32, F32))]


def _tail_scratch():
    return [pltpu.VMEM((LANES, LANES), F32), pltpu.VMEM((LIST_ROWS, LIST_COLS), F32),
            pltpu.VMEM((LIST_ROWS, LIST_COLS), F32)]


def _tail_out_specs():
    return [_tile_spec(D),
            pl.BlockSpec((TM * TOK_ROWS, LANES), lambda i: (i, 0)),
            _full_spec((SUBLANES, LANES))] + [_full_spec(s) for s in _LIST_OUT_SHAPES]


def _halo_specs(halo, width):
    per = TM // halo
    last = T_ALL // halo - 1
    prev = pl.BlockSpec((halo, width), lambda i: (jnp.maximum(i * per - 1, 0), 0))
    nxt = pl.BlockSpec((halo, width), lambda i: (jnp.minimum((i + 1) * per, last), 0))
    return prev, nxt


def _mid_even_kernel(l, a_ref, ap_ref, an_ref, ybc_ref, ybl_ref, xc_ref, xl_ref, mod_ref, wp_ref, ps_ref, wo_ref,
                     gffn_ref, wr_ref, *refs):
    outs, (pad_ref, pool_ref), scratch = refs[:N_TAIL_OUT], refs[N_TAIL_OUT:-N_TAIL_SCRATCH], refs[-N_TAIL_SCRATCH:]
    i = pl.program_id(0)
    r = _mod_row(i)
    off, n, first, last = _seq_info(i)
    a = a_ref[...]
    ap = jnp.where(first, 0.0, ap_ref[...])
    an = jnp.where(last, 0.0, an_ref[...])
    half_rows = TM // 2
    pos = [(off + 2 * lax.broadcasted_iota(I32, (half_rows, 1), 0) + parity).astype(F32) for parity in (0, 1)]
    last_pos = (n - 1).astype(F32)
    mixed = []
    for g, w in enumerate(POOL_WINDOWS):
        lo = w // 2
        hi = w - lo - 1
        cols = slice(g * POOL_GROUP_DIM, (g + 1) * POOL_GROUP_DIM)
        pad_ref[g, 0:POOL_HALO, :] = ap[:, cols]
        pad_ref[g, POOL_HALO:POOL_HALO + TM, :] = a[:, cols]
        pad_ref[g, POOL_HALO + TM:, :] = an[:, cols]
        loads = {s: pad_ref[g, pl.ds(s, half_rows, stride=2), :]
                 for s in range(POOL_HALO - lo, POOL_HALO + hi + 2)}
        for parity in (0, 1):
            total = functools.reduce(lambda x, y: x + y,
                                     [loads[POOL_HALO + parity + j] for j in range(-lo, hi + 1)])
            count = jnp.minimum(pos[parity] + hi, last_pos) - jnp.maximum(pos[parity] - lo, 0.0) + 1.0
            pool_ref[g, pl.ds(parity, half_rows, stride=2), :] = total / count - loads[POOL_HALO + parity]
        mixed.append(_dot(pool_ref[g], wp_ref[g]))
    ya = jnp.concatenate(mixed, axis=1) * ps_ref[...]
    y_pool = _dot(ya, wo_ref[0:POOL_DIM, :])

    def body(pop):
        y = y_pool + _dot((ybc_ref, ybl_ref)[pop][...], wo_ref[POOL_DIM:, :])
        _mixer_tail(l, (xc_ref, xl_ref)[pop][...], y, mod_ref, r, gffn_ref, wr_ref, outs, scratch)

    _per_population(i, body)


def _mid_even(l, e, a, yb_c, yb_l, xc, xl, mod, w_pool, pool_scale, w_out, gffn, wr):
    prev, nxt = _halo_specs(POOL_HALO, POOL_DIM)
    return pl.pallas_call(
        functools.partial(_mid_even_kernel, l),
        grid=(NT,),
        in_specs=[_tile_spec(POOL_DIM), prev, nxt] + _pop_specs(TM, Q_DIM) + _pop_specs(TM, D) + [
                  _MOD_SPEC, _layer_spec((len(POOL_WINDOWS), POOL_GROUP_DIM, POOL_GROUP_DIM), e),
                  _layer_spec((1, POOL_DIM), e), _layer_spec((D, D), e), _full_spec((DEPTH, D)),
                  _layer_spec((2, D, LANES), l)],
        out_specs=_tail_out_specs(),
        out_shape=_TAIL_OUT_SHAPES,
        scratch_shapes=[pltpu.VMEM((len(POOL_WINDOWS), TM + 2 * POOL_HALO, POOL_GROUP_DIM), F32),
                        pltpu.VMEM((len(POOL_WINDOWS), TM, POOL_GROUP_DIM), F32)] + _tail_scratch(),
        compiler_params=_params(),
        name="mid_even",
    )(a, a, a, yb_c, yb_l, xc, xl, mod, w_pool, pool_scale, w_out, gffn, wr)


def _after_moe(l, r, x1_ref, moe_ref, mod_ref):
    return x1_ref[...] + mod_ref[l, pl.ds(r, 1), 5 * D:6 * D] * _from_token_tiles(moe_ref)


def _in_odd_kernel(l, x1_ref, moec_ref, moel_ref, mod_ref, g_ref, w1_ref, b1_ref, u_ref):
    i = pl.program_id(0) * PAIR
    r = _mod_row(i)

    def body(pop):
        x2 = _after_moe(l - 1, r, x1_ref, (moec_ref, moel_ref)[pop], mod_ref)
        shift = mod_ref[l, pl.ds(r, 1), 0:D]
        scale = mod_ref[l, pl.ds(r, 1), D:2 * D]
        h = _rms_mod(x2, g_ref[l:l + 1, :], scale, shift)
        u = _dot(h, w1_ref[...]) + b1_ref[...]
        u_ref[...] = u[:, :D] * (1.0 / (1.0 + jnp.exp(-u[:, D:])))

    _per_half(i, body)


def _in_odd(l, o, x1, moe_c, moe_l, mod, g, w1, b1):
    return pl.pallas_call(
        functools.partial(_in_odd_kernel, l),
        grid=(NT // PAIR,),
        in_specs=[_pair_spec(D)] + _pair_half_specs() + [
                  _MOD_SPEC, _full_spec((DEPTH, D)), _layer_spec((D, 2 * D), o), _layer_spec((1, 2 * D), o)],
        out_specs=_pair_spec(D),
        out_shape=jax.ShapeDtypeStruct((T_ALL, D), F32),
        compiler_params=_params(),
        name="in_odd",
    )(x1, moe_c, moe_l, mod, g, w1, b1)


CONV_PAD_ROWS = TM + 2 * CONV_HALO
CONV_CHUNKS = D // LANES
CONV_BLOCK = 64


def _mid_odd_kernel(l, u_ref, up_ref, un_ref, x1_ref, moec_ref, moel_ref, mod_ref, dw_ref, dwb_ref, lng_ref, lnb_ref,
                    w2_ref, b2_ref, gffn_ref, wr_ref, *refs):
    outs, scratch = refs[:N_TAIL_OUT], refs[-N_TAIL_SCRATCH:]
    pad_ref, conv_ref, x_ref = refs[N_TAIL_OUT:-N_TAIL_SCRATCH]
    i = pl.program_id(0)

    def residual(pop):
        x_ref[...] = _after_moe(l - 1, _mod_row(i), x1_ref, (moec_ref, moel_ref)[pop], mod_ref)

    _per_half(i, residual)
    r = _mod_row(i)
    _, _, first, last = _seq_info(i)
    u = u_ref[...]
    up = jnp.where(first, 0.0, up_ref[...])
    un = jnp.where(last, 0.0, un_ref[...])
    half = CONV_WIDTH // 2
    first_row = CONV_HALO - half
    for c in range(CONV_CHUNKS):
        cols = slice(c * LANES, (c + 1) * LANES)
        pad_ref[c, 0:CONV_HALO, :] = up[:, cols]
        pad_ref[c, CONV_HALO:CONV_HALO + TM, :] = u[:, cols]
        pad_ref[c, CONV_HALO + TM:, :] = un[:, cols]
        bias = jnp.broadcast_to(dwb_ref[:, cols], (CONV_BLOCK, LANES))
        for r0 in range(0, TM, 2 * CONV_BLOCK):
            acc_even, acc_odd = bias, bias
            for s in range(first_row, first_row + CONV_WIDTH + 1):
                win = pad_ref[c, pl.ds(r0 + s, CONV_BLOCK, stride=2), :]
                j = s - first_row
                if j < CONV_WIDTH:
                    acc_even = acc_even + win * dw_ref[j:j + 1, cols]
                if j >= 1:
                    acc_odd = acc_odd + win * dw_ref[j - 1:j, cols]
            conv_ref[c, pl.ds(r0, CONV_BLOCK, stride=2), :] = acc_even
            conv_ref[c, pl.ds(r0 + 1, CONV_BLOCK, stride=2), :] = acc_odd
    acc = jnp.concatenate([conv_ref[c] for c in range(CONV_CHUNKS)], axis=1)
    mu = jnp.mean(acc, axis=-1, keepdims=True)
    cen = acc - mu
    var = jnp.mean(cen * cen, axis=-1, keepdims=True)
    v = _silu(cen * lax.rsqrt(var + EPS) * lng_ref[...] + lnb_ref[...])
    y = _dot(v, w2_ref[...]) + b2_ref[...]
    _mixer_tail(l, x_ref[...], y, mod_ref, r, gffn_ref, wr_ref, outs, scratch)


def _mid_odd(l, o, u, x1, moe_c, moe_l, mod, dw, dwb, lng, lnb, w2, b2, gffn, wr):
    prev, nxt = _halo_specs(CONV_HALO, D)
    return pl.pallas_call(
        functools.partial(_mid_odd_kernel, l),
        grid=(NT,),
        in_specs=[_tile_spec(D), prev, nxt, _tile_spec(D)] + _half_specs(TM * TOK_ROWS, LANES) + [
                  _MOD_SPEC, _layer_spec((CONV_WIDTH, D), o),
                  _layer_spec((1, D), o), _layer_spec((1, D), o), _layer_spec((1, D), o),
                  _layer_spec((D, D), o), _layer_spec((1, D), o), _full_spec((DEPTH, D)),
                  _layer_spec((2, D, LANES), l)],
        out_specs=_tail_out_specs(),
        out_shape=_TAIL_OUT_SHAPES,
        scratch_shapes=[pltpu.VMEM((CONV_CHUNKS, CONV_PAD_ROWS, LANES), F32),
                        pltpu.VMEM((CONV_CHUNKS, TM, LANES), F32), pltpu.VMEM((TM, D), F32)] + _tail_scratch(),
        compiler_params=_params(),
        name="mid_odd",
    )(u, u, u, x1, moe_c, moe_l, mod, dw, dwb, lng, lnb, w2, b2, gffn, wr)


def _tile_tables_kernel(cnt_ref, te_c, tc_c, tw_c, te_l, tc_l, tw_l, nact_ref, second_ref, first_ref, wplan_ref):
    for p, (te_ref, tc_ref, tw_ref) in enumerate(((te_c, tc_c, tw_c), (te_l, tc_l, tw_l))):
        run = jnp.int32(0)
        ends = []
        for e in range(N_EXPERTS):
            first_ref[p * N_EXPERTS + e] = run
            run = run + (cnt_ref[p, e] + MOE_TM - 1) // MOE_TM
            ends.append(run)
        nact_ref[p] = run

        nxt1, nxt2 = jnp.int32(0), jnp.int32(0)
        plans = []
        for e in reversed(range(N_EXPERTS)):
            plans.append(nxt2)
            used = cnt_ref[p, e] > 0
            nxt2 = jnp.where(used, nxt1, nxt2)
            nxt1 = jnp.where(used, e + 1, nxt1)
        plans.reverse()
        second_ref[p] = nxt2
        order = jnp.int32(0)
        for e in range(N_EXPERTS):
            wplan_ref[p * N_EXPERTS + e] = order + (plans[e] << W_AHEAD_SHIFT)
            order = jnp.where(cnt_ref[p, e] > 0, jnp.where(order == W_BUFFERS - 1, 0, order + 1), order)

        def expert_of(tile, ends=ends):
            e = jnp.int32(0)
            for k in range(N_EXPERTS):
                e = e + jnp.where(tile >= ends[k], 1, 0)
            return e

        last_e = expert_of(run - 1)

        def tile_body(j, c, te_ref=te_ref, tc_ref=tc_ref, tw_ref=tw_ref, p=p, run=run, expert_of=expert_of,
                      last_e=last_e):
            e = jnp.minimum(expert_of(j), last_e)
            te_ref[j] = e
            chunk = jnp.where(j < run, j - first_ref[p * N_EXPERTS + e], 0)
            tc_ref[j] = chunk
            starts_expert = jnp.logical_and(j < run, chunk == 0)
            tw_ref[j] = wplan_ref[p * N_EXPERTS + e] + jnp.where(starts_expert, W_FIRST_UNIT, 0)
            return c

        lax.fori_loop(0, te_ref.shape[0], tile_body, 0)


def _tile_tables(cnt):
    smem = pl.BlockSpec(memory_space=pltpu.SMEM)
    shapes = [(MOE_TILES,)] * 6 + [(2,), (2,)]
    return pl.pallas_call(
        _tile_tables_kernel,
        in_specs=[smem],
        out_specs=[smem] * len(shapes),
        out_shape=[jax.ShapeDtypeStruct(s, I32) for s in shapes],
        scratch_shapes=[pltpu.SMEM((2 * N_EXPERTS,), I32), pltpu.SMEM((2 * N_EXPERTS,), I32)],
        name="tile_tables",
    )(cnt)


def _moe_kernel(l, pop, te_ref, tc_ref, tw_ref, nact_ref, second_ref, list_ref, h_ref, gate_ref,
                w1_hbm, w3_hbm, w2_hbm, o_ref, xs_ref, ys_ref, w1_buf, w3_buf, w2_buf, w_sem):
    j = pl.program_id(0)
    nact = nact_ref[pop]
    last_tile = te_ref.shape[0] - 1
    tile_rows = MOE_TM * TOK_ROWS

    def weight_copies(expert, buf):
        return [pltpu.make_async_copy(hbm.at[l, expert], vmem.at[buf], w_sem.at[buf, k])
                for k, (hbm, vmem) in enumerate(((w1_hbm, w1_buf), (w3_hbm, w3_buf), (w2_hbm, w2_buf)))]

    def list_base(tile):
        expert = te_ref[tile]
        block = expert >> LIST_PER_BLOCK_LOG2
        in_block = expert & ((1 << LIST_PER_BLOCK_LOG2) - 1)
        return (block * LIST_ROWS + tc_ref[tile] * (MOE_TM // LIST_SPAN)) * LANES + in_block * LIST_SPAN

    def tok_rows(base, r):
        first = list_ref[base + (r // LIST_SPAN) * LANES + r % LIST_SPAN]
        return pl.ds(pl.multiple_of(first, TOK_ROWS), TOK_ROWS)

    def buf_rows(buf, r):
        return pl.ds(pl.multiple_of(buf * tile_rows + r * TOK_ROWS, TOK_ROWS), TOK_ROWS)

    def gather(tile, buf):
        base = list_base(tile)
        for r in range(MOE_TM):
            xs_ref[buf_rows(buf, r), :] = h_ref[tok_rows(base, r), :]

    tile = jnp.minimum(j, last_tile)
    plan = tw_ref[tile]
    wbuf = plan & (W_FIRST_UNIT - 1)
    starts_expert = jnp.logical_and(plan & W_FIRST_UNIT != 0, j <= last_tile)
    ahead_expert = (plan >> W_AHEAD_SHIFT) - 1
    ahead_buf = jnp.where(wbuf == 0, W_BUFFERS - 1, wbuf - 1)

    @pl.when(j == 0)
    def _():
        for cp in weight_copies(te_ref[0], 0):
            cp.start()

        @pl.when(second_ref[pop] > 0)
        def _():
            for cp in weight_copies(second_ref[pop] - 1, 1):
                cp.start()

        o_ref[...] = jnp.zeros(o_ref.shape, F32)
        ys_ref[...] = jnp.zeros(ys_ref.shape, F32)
        gather(0, 0)

    @pl.when(starts_expert)
    def _():
        for cp in weight_copies(te_ref[tile], wbuf):
            cp.wait()

        @pl.when(ahead_expert >= 0)
        def _():
            for cp in weight_copies(ahead_expert, ahead_buf):
                cp.start()

    @pl.when(j <= nact)
    def _():
        cur = j & 1
        w1_ref, w3_ref, w2_ref = w1_buf.at[wbuf], w3_buf.at[wbuf], w2_buf.at[wbuf]
        x = jnp.concatenate([xs_ref[pl.ds(cur * tile_rows + s, MOE_TM, stride=TOK_ROWS), :]
                             for s in range(TOK_ROWS)], axis=1).astype(BF16)
        gather(jnp.minimum(j + 1, last_tile), 1 - cur)

        prev_base = list_base(jnp.maximum(j - 1, 0))
        for r0 in range(0, MOE_TM, MOVE_BATCH):
            dst = [tok_rows(prev_base, r0 + k) for k in range(MOVE_BATCH)]
            vals = [o_ref[dst[k], :] + ys_ref[buf_rows(1 - cur, r0 + k), :] for k in range(MOVE_BATCH)]
            for k in reversed(range(MOVE_BATCH)):
                o_ref[dst[k], :] = vals[k]

        hg = jnp.dot(x, w1_ref[...].astype(BF16), preferred_element_type=F32)
        hu = jnp.dot(x, w3_ref[...].astype(BF16), preferred_element_type=F32)
        y = _dot(_silu(hg) * hu, w2_ref[...])

        first_row = pl.multiple_of(tc_ref[tile] * (MOE_TM // LIST_SPAN), MOE_TM // LIST_SPAN)
        spread = jnp.concatenate(
            [jnp.broadcast_to(gate_ref[pl.ds(first_row + k, 1), :], (LIST_SPAN, LIST_COLS))
             for k in range(MOE_TM // LIST_SPAN)], axis=0)
        want = te_ref[tile] * LIST_SPAN + (lax.broadcasted_iota(I32, (MOE_TM, 1), 0) & (LIST_SPAN - 1))
        hit = jnp.logical_and(lax.broadcasted_iota(I32, (MOE_TM, LIST_COLS), 1) == want, j < nact)
        y = y * jnp.sum(jnp.where(hit, spread, 0.0), axis=1, keepdims=True)
        for s in range(TOK_ROWS):
            ys_ref[pl.ds(cur * tile_rows + s, MOE_TM, stride=TOK_ROWS), :] = y[:, s * LANES:(s + 1) * LANES]


def _moe(l, pop, h_tt, te, tc, tw, nact, second, rows, gate, w1, w3, w2):
    tp = T_HALF
    blk = pop
    n_tiles = te.shape[0]
    hbm = pl.BlockSpec(memory_space=pl.ANY)
    grid_spec = pltpu.PrefetchScalarGridSpec(
        num_scalar_prefetch=6,
        grid=(n_tiles + 1,),
        in_specs=[
            pl.BlockSpec((tp * TOK_ROWS, LANES), lambda j, *_: (blk, 0), pipeline_mode=pl.Buffered(1)),
            pl.BlockSpec((LIST_ROWS, LIST_COLS), lambda j, *_: (0, 0), pipeline_mode=pl.Buffered(1)),
            hbm, hbm, hbm,
        ],
        out_specs=pl.BlockSpec((tp * TOK_ROWS, LANES), lambda j, *_: (0, 0), pipeline_mode=pl.Buffered(1)),
        scratch_shapes=[pltpu.VMEM((2 * MOE_TM * TOK_ROWS, LANES), F32),
                        pltpu.VMEM((2 * MOE_TM * TOK_ROWS, LANES), F32),
                        pltpu.VMEM((W_BUFFERS, D, EXPERT_HIDDEN), F32),
                        pltpu.VMEM((W_BUFFERS, D, EXPERT_HIDDEN), F32),
                        pltpu.VMEM((W_BUFFERS, EXPERT_HIDDEN, D), F32),
                        pltpu.SemaphoreType.DMA((W_BUFFERS, 3))],
    )
    return pl.pallas_call(
        functools.partial(_moe_kernel, l, pop),
        grid_spec=grid_spec,
        out_shape=jax.ShapeDtypeStruct((tp * TOK_ROWS, LANES), F32),
        compiler_params=_params(),
        name="moe",
    )(te, tc, tw, nact, second, rows, h_tt, gate, w1, w3, w2)


def _moe_all(l, h_tt, cnt, rows_c, gate_c, rows_l, gate_l, w1, w3, w2):
    te_c, tc_c, tw_c, te_l, tc_l, tw_l, nact, second = _tile_tables(cnt)
    out_c = _moe(l, 0, h_tt, te_c, tc_c, tw_c, nact, second, rows_c.reshape(-1), gate_c, w1, w3, w2)
    out_l = _moe(l, 1, h_tt, te_l, tc_l, tw_l, nact, second, rows_l.reshape(-1), gate_l, w1, w3, w2)
    return out_c, out_l


def _final_kernel(l, x1_ref, moec_ref, moel_ref, mod_ref, g_ref, oc_ref, ol_ref):
    i = pl.program_id(0) * PAIR
    r = _mod_row(i)

    def body(half, pop):
        x = _after_moe(l, r, x1_ref, (moec_ref, moel_ref)[half], mod_ref)
        ms = jnp.mean(x * x, axis=-1, keepdims=True)
        (oc_ref, ol_ref)[pop][...] = x * lax.rsqrt(ms + EPS) * g_ref[...]

    pl.when(i < MOE_SPLIT)(functools.partial(body, 0, 0))
    pl.when(jnp.logical_and(i >= MOE_SPLIT, i < NT_CTX))(functools.partial(body, 1, 0))
    pl.when(i >= NT_CTX)(functools.partial(body, 1, 1))


def _final(l, x1, moe_c, moe_l, mod, g):
    return pl.pallas_call(
        functools.partial(_final_kernel, l),
        grid=(NT // PAIR,),
        in_specs=[_pair_spec(D)] + _pair_half_specs() + [_MOD_SPEC, _full_spec((1, D))],
        out_specs=_pair_pop_specs(D),
        out_shape=[jax.ShapeDtypeStruct((T_CTX, D), F32), jax.ShapeDtypeStruct((T_LAT, D), F32)],
        compiler_params=_params(),
        name="final_norm",
    )(x1, moe_c, moe_l, mod, g)


def _rope_tables():
    t = np.arange(LAT_LEN)
    row = (t // GRID_W).astype(np.float32)
    col = (t % GRID_W).astype(np.float32)
    inv = np.float32(ROPE_BASE) ** (-np.arange(0, ROPE_AXIS_DIM, 2, dtype=np.float32) / np.float32(ROPE_AXIS_DIM))

    def table(p):
        ang = (p[:, None] * inv[None, :]).astype(np.float32)
        ang = np.concatenate([ang, ang], axis=-1)
        return np.cos(ang).astype(np.float32), np.sin(ang).astype(np.float32)

    (cr, sr), (cc, sc) = table(row), table(col)
    cos = np.concatenate([cr, cc, cr, cc], axis=-1)
    sin = np.concatenate([sr, sc, sr, sc], axis=-1)
    return jnp.asarray(cos), jnp.asarray(sin)


def _router_weights(w_grp, w_exp):
    we = jnp.transpose(w_exp, (0, 2, 1, 3)).reshape(DEPTH, D, N_GROUPS * EPG)
    pad = jnp.zeros((DEPTH, D, LANES - N_GROUPS - N_GROUPS * EPG), F32)
    wr = jnp.concatenate([w_grp, we, pad], axis=-1)
    hi = lax.bitcast_convert_type(lax.bitcast_convert_type(wr, jnp.uint32) & jnp.uint32(0xFFFF0000), F32)
    return jnp.stack([hi.astype(BF16), (wr - hi).astype(BF16)], axis=1)


def kernel(x_prompt, x_sample, cache_k, cache_v, c, c_ctx, w_ada, b_ada, norm_mix_g, norm_ffn_g, w_in_ab, pool_w, pool_scale, attn_sink, w_out_ab, conv_w1, conv_b1, conv_dw, conv_dw_b, conv_ln_g, conv_ln_b, conv_w2, conv_b2, router_grp, router_exp, moe_w1, moe_w3, moe_w2, final_g):
    xc = x_prompt.reshape(T_CTX, D)
    xl = x_sample.reshape(T_LAT, D)
    cond8 = jnp.concatenate([c_ctx[None, :], c, jnp.zeros((SUBLANES - 1 - N_LAT_SEQ, D), F32)], axis=0)
    mod = _modulation(cond8, w_ada, b_ada)
    cos_t, sin_t = _rope_tables()
    wr = _router_weights(router_grp, router_exp)
    n_even, n_odd = w_in_ab.shape[0], conv_w1.shape[0]

    a, q, k_c, k_l, v_c, v_l, state_k, state_v = _in_even(0, 0, xc, xl, mod, norm_mix_g, w_in_ab, cos_t, sin_t)
    yb_c = _attn_ctx(0, attn_sink, q, k_c, v_c)
    yb_l = _attn_lat(0, attn_sink, q, k_l, v_l, cache_k, cache_v)
    x1, h2, *routing = _mid_even(0, 0, a, yb_c, yb_l, xc, xl, mod, pool_w,
                                 pool_scale.reshape(n_even, 1, POOL_DIM), w_out_ab, norm_ffn_g, wr)
    moe0 = _moe_all(0, h2, *routing, moe_w1, moe_w3, moe_w2)

    vec = lambda p: p.reshape(n_odd, 1, -1)
    u = _in_odd(1, 0, x1, *moe0, mod, norm_mix_g, conv_w1, vec(conv_b1))
    x3, h2b, *routing = _mid_odd(1, 0, u, x1, *moe0, mod, conv_dw, vec(conv_dw_b), vec(conv_ln_g),
                                 vec(conv_ln_b), conv_w2, vec(conv_b2), norm_ffn_g, wr)
    moe1 = _moe_all(1, h2b, *routing, moe_w1, moe_w3, moe_w2)
    y_c, y_l = _final(1, x3, *moe1, mod, final_g.reshape(1, D))

    y_prompt = y_c.reshape(N_CTX_SEQ, CTX_LEN, D)
    y_sample = y_l.reshape(N_LAT_SEQ, LAT_LEN, D)
    return (y_prompt, y_sample, state_k, state_v)
```

```python
import functools

import jax
import jax.numpy as jnp
import numpy as np
from jax import lax
from jax.experimental import pallas as pl
from jax.experimental.pallas import tpu as pltpu

F32 = jnp.float32
BF16 = jnp.bfloat16
I32 = jnp.int32

D = 1024
N_CTX_SEQ = 16
CTX_LEN = 256
N_LAT_SEQ = 2
LAT_LEN = 1024
T_CTX = N_CTX_SEQ * CTX_LEN
T_LAT = N_LAT_SEQ * LAT_LEN
T_ALL = T_CTX + T_LAT
TM = 256
NT = T_ALL // TM
NT_CTX = T_CTX // TM
LAT_TILES_PER_SEQ = LAT_LEN // TM
GRID_W = 64
DEPTH = 2

POOL_WINDOWS = (2, 4, 8, 16)
POOL_GROUP_DIM = 128
POOL_DIM = 512
HEAD_DIM = 64
N_Q_HEADS = 8
N_KV_HEADS = 2
Q_PER_KV = 4
Q_DIM = 512
KV_DIM = 128
IN_AB = 1280
ATTN_WINDOW = 128
ATTN_BLOCK = 128
ATTN_SCALE = HEAD_DIM ** -0.5
ROPE_BASE = 10000.0
ROPE_AXIS_DIM = 32
CONV_WIDTH = 31
CONV_HALO = 16
POOL_HALO = 8
N_GROUPS = 4
EPG = 4
N_EXPERTS = 16
EXPERT_HIDDEN = 512
EPS = 1e-6
NEG_BIG = -1e30

SUBLANES = 8
LANES = 128
TOK_ROWS = D // LANES

MOE_TM = 256
MOE_SPLIT = NT // 2
T_HALF = MOE_SPLIT * TM
MOE_TILES = 2 * T_HALF // MOE_TM + N_EXPERTS
W_BUFFERS = 3
W_FIRST_UNIT = 4
W_AHEAD_SHIFT = 3
MOVE_BATCH_LOG2 = 2
MOVE_BATCH = 1 << MOVE_BATCH_LOG2
VMEM_LIMIT = 56 * 1024 * 1024


def _silu(x):
    return x * (1.0 / (1.0 + jnp.exp(-x)))


def _mod_row(i):
    return jnp.where(i < NT_CTX, 0, 1 + (i - NT_CTX) // LAT_TILES_PER_SEQ)


def _seq_info(i):
    is_ctx = i < NT_CTX
    k = (i - NT_CTX) % LAT_TILES_PER_SEQ
    off = jnp.where(is_ctx, 0, k * TM)
    n = jnp.where(is_ctx, CTX_LEN, LAT_LEN)
    first = jnp.logical_or(is_ctx, k == 0)
    last = jnp.logical_or(is_ctx, k == LAT_TILES_PER_SEQ - 1)
    return off, n, first, last


def _rms_mod(x, g, scale, shift):
    ms = jnp.mean(x * x, axis=-1, keepdims=True)
    return (x * lax.rsqrt(ms + EPS) * g) * (1.0 + scale) + shift


def _dot(a, b):
    return jnp.dot(a.astype(BF16), b.astype(BF16), preferred_element_type=F32)


def _dot_nt(a, b):
    return lax.dot_general(a.astype(BF16), b.astype(BF16), (((1,), (1,)), ((), ())),
                           preferred_element_type=F32)


def _from_token_tiles(ref):
    rows = ref.shape[0] // TOK_ROWS
    return jnp.concatenate([ref[pl.ds(s, rows, stride=TOK_ROWS), :] for s in range(TOK_ROWS)], axis=1)


def _tile_spec(width):
    return pl.BlockSpec((TM, width), lambda i: (i, 0))


def _full_spec(shape):
    return pl.BlockSpec(shape, lambda i: (0,) * len(shape))


def _layer_spec(shape, l):
    return pl.BlockSpec((None,) + tuple(shape), lambda i: (l,) + (0,) * len(shape))


def _pop_specs(rows, width):
    return [pl.BlockSpec((rows, width), lambda i: (jnp.minimum(i, NT_CTX - 1), 0)),
            pl.BlockSpec((rows, width), lambda i: (jnp.maximum(i - NT_CTX, 0), 0))]


def _per_population(i, body):
    pl.when(i < NT_CTX)(functools.partial(body, 0))
    pl.when(i >= NT_CTX)(functools.partial(body, 1))


def _half_specs(rows, width):
    return [pl.BlockSpec((rows, width), lambda i: (jnp.minimum(i, MOE_SPLIT - 1), 0)),
            pl.BlockSpec((rows, width), lambda i: (jnp.maximum(i - MOE_SPLIT, 0), 0))]


def _per_half(i, body):
    pl.when(i < MOE_SPLIT)(functools.partial(body, 0))
    pl.when(i >= MOE_SPLIT)(functools.partial(body, 1))


PAIR = 2
FINAL_GROUP = 4
for _n in (PAIR, FINAL_GROUP):
    assert LAT_TILES_PER_SEQ % _n == 0 and MOE_SPLIT % _n == 0 and NT_CTX % _n == 0


def _pair_spec(width, n=PAIR):
    return pl.BlockSpec((n * TM, width), lambda i: (i, 0))


def _pair_half_specs(n=PAIR):
    split = MOE_SPLIT // n
    block = (n * TM * TOK_ROWS, LANES)
    return [pl.BlockSpec(block, lambda i: (jnp.minimum(i, split - 1), 0)),
            pl.BlockSpec(block, lambda i: (jnp.maximum(i - split, 0), 0))]


def _pair_pop_specs(width, n=PAIR):
    n_ctx = NT_CTX // n
    return [pl.BlockSpec((n * TM, width), lambda i: (jnp.minimum(i, n_ctx - 1), 0)),
            pl.BlockSpec((n * TM, width), lambda i: (jnp.maximum(i - n_ctx, 0), 0))]


_MOD_SPEC = pl.BlockSpec((DEPTH, SUBLANES, 6 * D), lambda i: (0, 0, 0))


def _params():
    return pltpu.CompilerParams(vmem_limit_bytes=VMEM_LIMIT)


MOD_TN = 1536


def _mod_kernel(cond_ref, w_ref, b_ref, o_ref):
    s = _silu(cond_ref[...])
    o_ref[0] = _dot(s, w_ref[0]) + b_ref[0]


def _modulation(cond8, w_ada, b_ada):
    return pl.pallas_call(
        _mod_kernel,
        grid=(DEPTH, 6 * D // MOD_TN),
        in_specs=[
            pl.BlockSpec((SUBLANES, D), lambda l, n: (0, 0)),
            pl.BlockSpec((1, D, MOD_TN), lambda l, n: (l, 0, n)),
            pl.BlockSpec((1, 1, MOD_TN), lambda l, n: (l, 0, n)),
        ],
        out_specs=pl.BlockSpec((1, SUBLANES, MOD_TN), lambda l, n: (l, 0, n)),
        out_shape=jax.ShapeDtypeStruct((DEPTH, SUBLANES, 6 * D), F32),
        compiler_params=_params(),
        name="modulation",
    )(cond8, w_ada, b_ada.reshape(DEPTH, 1, 6 * D))


def _rope_chunk(xc, cos, sin):
    lane = lax.broadcasted_iota(I32, xc.shape, 1)
    first = (lane % ROPE_AXIS_DIM) < (ROPE_AXIS_DIM // 2)
    rot = jnp.where(first, -pltpu.roll(xc, LANES - ROPE_AXIS_DIM // 2, 1), pltpu.roll(xc, ROPE_AXIS_DIM // 2, 1))
    return xc * cos + rot * sin


def _in_even_kernel(l, xc_ref, xl_ref, mod_ref, g_ref, w_ref, cos_ref, sin_ref,
                    a_ref, q_ref, kc_ref, kl_ref, vc_ref, vl_ref, sk_ref, sv_ref):
    i = pl.program_id(0) * PAIR
    r = _mod_row(i)

    def put_state(s_ref, val):
        for sq in range(PAIR):
            for hd in range(N_KV_HEADS):
                s_ref[sq, 0, :, hd, :] = val[sq * TM:(sq + 1) * TM, hd * HEAD_DIM:(hd + 1) * HEAD_DIM]

    def body(pop):
        x = (xc_ref, xl_ref)[pop][...]
        k_ref, v_ref = (kc_ref, kl_ref)[pop], (vc_ref, vl_ref)[pop]
        shift = mod_ref[l, pl.ds(r, 1), 0:D]
        scale = mod_ref[l, pl.ds(r, 1), D:2 * D]
        h = _rms_mod(x, g_ref[l:l + 1, :], scale, shift)
        proj = _dot(h, w_ref[...])
        a_ref[...] = proj[:, :POOL_DIM]
        v_ref[...] = proj[:, POOL_DIM + Q_DIM + KV_DIM:]
        if pop == 0:
            q_ref[...] = proj[:, POOL_DIM:POOL_DIM + Q_DIM].astype(BF16)
            k_ref[...] = proj[:, POOL_DIM + Q_DIM:POOL_DIM + Q_DIM + KV_DIM]
            put_state(sk_ref, proj[:, POOL_DIM + Q_DIM:POOL_DIM + Q_DIM + KV_DIM])
            put_state(sv_ref, proj[:, POOL_DIM + Q_DIM + KV_DIM:])
        else:
            off, _, _, _ = _seq_info(i)
            off = pl.multiple_of(off, TM)
            cos = cos_ref[pl.ds(off, PAIR * TM), :]
            sin = sin_ref[pl.ds(off, PAIR * TM), :]
            for c in range(Q_DIM // LANES):
                lo = POOL_DIM + c * LANES
                q_ref[:, c * LANES:(c + 1) * LANES] = _rope_chunk(proj[:, lo:lo + LANES], cos, sin).astype(BF16)
            lo = POOL_DIM + Q_DIM
            k_ref[...] = _rope_chunk(proj[:, lo:lo + LANES], cos, sin)

    _per_population(i, body)


def _in_even(l, e, xc, xl, mod, g, w_in, cos_t, sin_t):
    assert CTX_LEN == TM
    state_shape = (N_CTX_SEQ, 1, CTX_LEN, N_KV_HEADS, HEAD_DIM)
    state_spec = pl.BlockSpec((PAIR, 1, CTX_LEN, N_KV_HEADS, HEAD_DIM),
                              lambda i: (jnp.minimum(i, NT_CTX // PAIR - 1), 0, 0, 0, 0))
    return pl.pallas_call(
        functools.partial(_in_even_kernel, l),
        grid=(NT // PAIR,),
        in_specs=_pair_pop_specs(D) + [_MOD_SPEC, _full_spec((DEPTH, D)), _layer_spec((D, IN_AB), e),
                  _full_spec((LAT_LEN, LANES)), _full_spec((LAT_LEN, LANES))],
        out_specs=([_pair_spec(POOL_DIM), _pair_spec(Q_DIM)] + _pair_pop_specs(KV_DIM) + _pair_pop_specs(KV_DIM)
                   + [state_spec, state_spec]),
        out_shape=[jax.ShapeDtypeStruct((T_ALL, POOL_DIM), F32), jax.ShapeDtypeStruct((T_ALL, Q_DIM), BF16)]
                  + [jax.ShapeDtypeStruct((t, KV_DIM), F32) for t in (T_CTX, T_LAT, T_CTX, T_LAT)]
                  + [jax.ShapeDtypeStruct(state_shape, F32)] * 2,
        compiler_params=_params(),
        name="in_even",
    )(xc, xl, mod, g, w_in, cos_t, sin_t)


def _attn_ctx_kernel(e, sink_ref, q_ref, k_ref, v_ref, o_ref):
    for sq in range(PAIR):
        rows = slice(sq * TM, (sq + 1) * TM)
        q = q_ref[rows, :]
        k = k_ref[rows, :]
        v = v_ref[rows, :]
        for h in range(N_KV_HEADS):
            kh = k[:, h * HEAD_DIM:(h + 1) * HEAD_DIM]
            vh = v[:, h * HEAD_DIM:(h + 1) * HEAD_DIM]
            for g in range(Q_PER_KV):
                j = h * Q_PER_KV + g
                s = _dot_nt(q[:, j * HEAD_DIM:(j + 1) * HEAD_DIM], kh) * ATTN_SCALE
                sk = sink_ref[e, j]
                m = jnp.maximum(jnp.max(s, axis=-1, keepdims=True), sk)
                p = jnp.exp(s - m)
                denom = jnp.sum(p, axis=-1, keepdims=True) + jnp.exp(sk - m)
                o_ref[rows, j * HEAD_DIM:(j + 1) * HEAD_DIM] = (_dot(p, vh) / denom).astype(o_ref.dtype)


def _attn_ctx(e, sink, q, k, v):
    tile = lambda w: pl.BlockSpec((PAIR * TM, w), lambda b: (b, 0))
    return pl.pallas_call(
        functools.partial(_attn_ctx_kernel, e),
        grid=(N_CTX_SEQ // PAIR,),
        in_specs=[pl.BlockSpec(memory_space=pltpu.SMEM), tile(Q_DIM), tile(KV_DIM), tile(KV_DIM)],
        out_specs=tile(Q_DIM),
        out_shape=jax.ShapeDtypeStruct((T_CTX, Q_DIM), BF16),
        compiler_params=_params(),
        name="attn_ctx",
    )(sink, q, k, v)


LAT_BLOCKS = LAT_LEN // ATTN_BLOCK
BAND = 3 * ATTN_BLOCK


def _attn_lat_kernel(e, sink_ref, q_ref, k_ref, v_ref, ck_ref, cv_ref, o_ref):
    n = pl.program_id(1)
    start = jnp.clip((n - 1) * ATTN_BLOCK, 0, LAT_LEN - BAND)
    start = pl.multiple_of(start, ATTN_BLOCK)
    q = q_ref[...]
    kw = k_ref[pl.ds(start, BAND), :]
    vw = v_ref[pl.ds(start, BAND), :]
    qpos = n * ATTN_BLOCK + lax.broadcasted_iota(I32, (ATTN_BLOCK, BAND), 0)
    kpos = start + lax.broadcasted_iota(I32, (ATTN_BLOCK, BAND), 1)
    valid = jnp.abs(qpos - kpos) <= ATTN_WINDOW
    for h in range(N_KV_HEADS):
        hs = slice(h * HEAD_DIM, (h + 1) * HEAD_DIM)
        ck = ck_ref[0, 0, :, h, :]
        cv = cv_ref[0, 0, :, h, :]
        for g in range(Q_PER_KV):
            j = h * Q_PER_KV + g
            qj = q[:, j * HEAD_DIM:(j + 1) * HEAD_DIM]
            s_loc = jnp.where(valid, _dot_nt(qj, kw[:, hs]) * ATTN_SCALE, NEG_BIG)
            s_ctx = _dot_nt(qj, ck) * ATTN_SCALE
            sk = sink_ref[e, j]
            m = jnp.maximum(jnp.maximum(jnp.max(s_loc, axis=-1, keepdims=True),
                                        jnp.max(s_ctx, axis=-1, keepdims=True)), sk)
            p_loc = jnp.exp(s_loc - m)
            p_ctx = jnp.exp(s_ctx - m)
            denom = (jnp.sum(p_loc, axis=-1, keepdims=True) + jnp.sum(p_ctx, axis=-1, keepdims=True)
                     + jnp.exp(sk - m))
            o = _dot(p_loc, vw[:, hs]) + _dot(p_ctx, cv)
            o_ref[:, j * HEAD_DIM:(j + 1) * HEAD_DIM] = (o / denom).astype(o_ref.dtype)


def _attn_lat(e, sink, q, k, v, ck, cv):
    past = ck.shape[2]
    cache_spec = pl.BlockSpec((1, 1, past, N_KV_HEADS, HEAD_DIM), lambda b, n: (b, e, 0, 0, 0))
    q_blk0 = T_CTX // ATTN_BLOCK
    q_spec = pl.BlockSpec((ATTN_BLOCK, Q_DIM), lambda b, n: (q_blk0 + b * LAT_BLOCKS + n, 0))
    return pl.pallas_call(
        functools.partial(_attn_lat_kernel, e),
        grid=(N_LAT_SEQ, LAT_BLOCKS),
        in_specs=[pl.BlockSpec(memory_space=pltpu.SMEM),
                  q_spec,
                  pl.BlockSpec((LAT_LEN, KV_DIM), lambda b, n: (b, 0)),
                  pl.BlockSpec((LAT_LEN, KV_DIM), lambda b, n: (b, 0)),
                  cache_spec, cache_spec],
        out_specs=pl.BlockSpec((ATTN_BLOCK, Q_DIM), lambda b, n: (b * LAT_BLOCKS + n, 0)),
        out_shape=jax.ShapeDtypeStruct((T_LAT, Q_DIM), BF16),
        compiler_params=_params(),
        name="attn_lat",
    )(sink, q, k, v, ck, cv)


def _router(h2, wr_ref):
    hi = h2.astype(BF16)
    lo = (h2 - hi.astype(F32)).astype(BF16)
    whi = wr_ref[0]
    wlo = wr_ref[1]
    dot = functools.partial(jnp.dot, preferred_element_type=F32)
    logits_t = (dot(hi, whi) + (dot(lo, whi) + dot(hi, wlo))).T
    row = lambda k: logits_t[k:k + 1, :]

    def first_max(vals):
        best = functools.reduce(jnp.maximum, vals)
        idx = jnp.full(best.shape, len(vals) - 1, F32)
        for k in reversed(range(len(vals) - 1)):
            idx = jnp.where(vals[k] == best, float(k), idx)
        return best, idx

    grp = [row(k) for k in range(N_GROUPS)]
    gmax, g_idx = first_max(grp)
    g_w = 1.0 / sum(jnp.exp(gk - gmax) for gk in grp)
    el = []
    for e in range(EPG):
        v = row(N_GROUPS + (N_GROUPS - 1) * EPG + e)
        for k in reversed(range(N_GROUPS - 1)):
            v = jnp.where(g_idx == k, row(N_GROUPS + k * EPG + e), v)
        el.append(v)
    t1, i1 = first_max(el)
    t2, i2 = first_max([jnp.where(i1 == e, -jnp.inf, el[e]) for e in range(EPG)])
    d = jnp.exp(t2 - t1)
    w1 = g_w / (1.0 + d)
    w2 = g_w * d / (1.0 + d)
    return EPG * g_idx + i1, EPG * g_idx + i2, w1, w2


LIST_SPAN_LOG2 = 5
LIST_SPAN = 1 << LIST_SPAN_LOG2
LIST_ROWS = LANES
LIST_COLS = N_EXPERTS * LIST_SPAN
LIST_BLOCKS = LIST_COLS // LANES
LIST_PER_BLOCK_LOG2 = 2


N_TAIL_OUT = 7
N_TAIL_SCRATCH = 3


def _mixer_tail(l, x, y, mod_ref, r, gffn_ref, wr_ref, outs, scratch):
    x1_ref, h2_ref, cnt_ref, lc_ref, gc_ref, ll_ref, gl_ref = outs
    run_ref, acc_ref, gacc_ref = scratch
    i = pl.program_id(0)
    g1 = mod_ref[l, pl.ds(r, 1), 2 * D:3 * D]
    shift2 = mod_ref[l, pl.ds(r, 1), 3 * D:4 * D]
    scale2 = mod_ref[l, pl.ds(r, 1), 4 * D:5 * D]
    x1 = x + g1 * y
    x1_ref[...] = x1
    h2 = _rms_mod(x1, gffn_ref[l:l + 1, :], scale2, shift2)
    for s in range(TOK_ROWS):
        h2_ref[pl.ds(s, TM, stride=TOK_ROWS), :] = h2[:, s * LANES:(s + 1) * LANES]
    e1, e2, w1_row, w2_row = _router(h2, wr_ref)

    @pl.when(jnp.logical_or(i == 0, i == MOE_SPLIT))
    def _():
        run_ref[...] = jnp.zeros(run_ref.shape, F32)
        acc_ref[...] = jnp.zeros(acc_ref.shape, F32)
        gacc_ref[...] = jnp.zeros(gacc_ref.shape, F32)

    @pl.when(i == 0)
    def _():
        cnt_ref[...] = jnp.zeros(cnt_ref.shape, I32)

    sub_i = lax.broadcasted_iota(I32, (LANES, TM), 0)
    sub = sub_i.astype(F32)
    member = jnp.where(jnp.logical_or(sub == e1, sub == e2), 1.0, 0.0)
    tri = jnp.where(lax.broadcasted_iota(I32, (TM, TM), 0) <= lax.broadcasted_iota(I32, (TM, TM), 1), 1.0, 0.0)
    csum = jnp.dot(member.astype(BF16), tri.astype(BF16), preferred_element_type=F32)
    run = run_ref[:, 0:1]
    before = csum - member + run
    run_new = run + csum[:, TM - 1:TM]
    run_ref[...] = jnp.broadcast_to(run_new, run_ref.shape)

    col_i = lax.broadcasted_iota(I32, (TM, LIST_COLS), 1)
    rows_oh, col_rows = [], []
    for e in (e1, e2):
        rank = jnp.zeros((1, TM), F32)
        for k in range(N_EXPERTS):
            rank = rank + jnp.where(e == k, before[k:k + 1, :], 0.0)
        rank_i = rank.astype(I32)
        rows_oh.append(jnp.where(sub_i == (rank_i >> LIST_SPAN_LOG2), 1.0, 0.0))
        col_rows.append(e.astype(I32) * LIST_SPAN + (rank_i & (LIST_SPAN - 1)))
    rows_t = jnp.concatenate(rows_oh, axis=1)
    col_cols = jnp.concatenate(col_rows + [jnp.zeros((LANES - 2, TM), I32)], axis=0).T
    cols = jnp.concatenate([jnp.where(col_i == col_cols[:, k:k + 1], 1.0, 0.0).astype(BF16) for k in range(2)],
                           axis=0)
    tok = (lax.broadcasted_iota(I32, (1, 2 * TM), 1) % TM).astype(F32)
    rest = jnp.concatenate([w1_row, w2_row], axis=1)
    values = [None, tok]
    for _ in range(3):
        piece = rest.astype(BF16).astype(F32)
        values.append(piece)
        rest = rest - piece
    planes = jnp.concatenate([rows_t if v is None else rows_t * v for v in values], axis=0).astype(BF16)
    out = jnp.dot(planes, cols, preferred_element_type=F32)
    hits, tok_sum, gp0, gp1, gp2 = [out[k * LANES:(k + 1) * LANES] for k in range(len(values))]
    tile_in_pop = jnp.where(i < MOE_SPLIT, i, i - MOE_SPLIT).astype(F32)
    acc = acc_ref[...] + TOK_ROWS * (tok_sum + TM * tile_in_pop * hits)
    acc_ref[...] = acc
    gacc = gacc_ref[...] + gp0 + gp1 + gp2
    gacc_ref[...] = gacc

    def counts_row():
        return jnp.broadcast_to(run_new, (LANES, LANES)).T[0:1, :].astype(I32)

    def put_rows(dst_ref):
        for b in range(LIST_BLOCKS):
            dst_ref[b * LIST_ROWS:(b + 1) * LIST_ROWS, :] = acc[:, b * LANES:(b + 1) * LANES].astype(I32)

    @pl.when(i == MOE_SPLIT - 1)
    def _():
        cnt_ref[0:1, :] = counts_row()
        put_rows(lc_ref)
        gc_ref[...] = gacc

    @pl.when(i == NT - 1)
    def _():
        cnt_ref[1:2, :] = counts_row()
        put_rows(ll_ref)
        gl_ref[...] = gacc


_LIST_OUT_SHAPES = [(LIST_BLOCKS * LIST_ROWS, LANES), (LIST_ROWS, LIST_COLS)] * 2
_TAIL_OUT_SHAPES = [jax.ShapeDtypeStruct((T_ALL, D), F32),
                    jax.ShapeDtypeStruct((T_ALL * TOK_ROWS, LANES), F32),
                    jax.ShapeDtypeStruct((SUBLANES, LANES), I32)] + [
                        jax.ShapeDtypeStruct(s, dt) for s, dt in zip(_LIST_OUT_SHAPES, (I32, F32, I32, F32))]


def _tail_scratch():
    return [pltpu.VMEM((LANES, LANES), F32), pltpu.VMEM((LIST_ROWS, LIST_COLS), F32),
            pltpu.VMEM((LIST_ROWS, LIST_COLS), F32)]


def _tail_out_specs():
    return [_tile_spec(D),
            pl.BlockSpec((TM * TOK_ROWS, LANES), lambda i: (i, 0)),
            _full_spec((SUBLANES, LANES))] + [_full_spec(s) for s in _LIST_OUT_SHAPES]


def _halo_specs(halo, width):
    per = TM // halo
    last = T_ALL // halo - 1
    prev = pl.BlockSpec((halo, width), lambda i: (jnp.maximum(i * per - 1, 0), 0))
    nxt = pl.BlockSpec((halo, width), lambda i: (jnp.minimum((i + 1) * per, last), 0))
    return prev, nxt


def _mid_even_kernel(l, a_ref, ap_ref, an_ref, ybc_ref, ybl_ref, xc_ref, xl_ref, mod_ref, wp_ref, ps_ref, wo_ref,
                     gffn_ref, wr_ref, *refs):
    outs, (pad_ref, pool_ref), scratch = refs[:N_TAIL_OUT], refs[N_TAIL_OUT:-N_TAIL_SCRATCH], refs[-N_TAIL_SCRATCH:]
    i = pl.program_id(0)
    r = _mod_row(i)
    off, n, first, last = _seq_info(i)
    a = a_ref[...]
    ap = jnp.where(first, 0.0, ap_ref[...])
    an = jnp.where(last, 0.0, an_ref[...])
    half_rows = TM // 2
    pos = [(off + 2 * lax.broadcasted_iota(I32, (half_rows, 1), 0) + parity).astype(F32) for parity in (0, 1)]
    last_pos = (n - 1).astype(F32)
    mixed = []
    for g, w in enumerate(POOL_WINDOWS):
        lo = w // 2
        hi = w - lo - 1
        cols = slice(g * POOL_GROUP_DIM, (g + 1) * POOL_GROUP_DIM)
        pad_ref[g, 0:POOL_HALO, :] = ap[:, cols]
        pad_ref[g, POOL_HALO:POOL_HALO + TM, :] = a[:, cols]
        pad_ref[g, POOL_HALO + TM:, :] = an[:, cols]
        loads = {s: pad_ref[g, pl.ds(s, half_rows, stride=2), :]
                 for s in range(POOL_HALO - lo, POOL_HALO + hi + 2)}
        for parity in (0, 1):
            total = functools.reduce(lambda x, y: x + y,
                                     [loads[POOL_HALO + parity + j] for j in range(-lo, hi + 1)])
            count = jnp.minimum(pos[parity] + hi, last_pos) - jnp.maximum(pos[parity] - lo, 0.0) + 1.0
            pool_ref[g, pl.ds(parity, half_rows, stride=2), :] = total / count - loads[POOL_HALO + parity]
        mixed.append(_dot(pool_ref[g], wp_ref[g]))
    ya = jnp.concatenate(mixed, axis=1) * ps_ref[...]
    y_pool = _dot(ya, wo_ref[0:POOL_DIM, :])

    def body(pop):
        y = y_pool + _dot((ybc_ref, ybl_ref)[pop][...], wo_ref[POOL_DIM:, :])
        _mixer_tail(l, (xc_ref, xl_ref)[pop][...], y, mod_ref, r, gffn_ref, wr_ref, outs, scratch)

    _per_population(i, body)


def _mid_even(l, e, a, yb_c, yb_l, xc, xl, mod, w_pool, pool_scale, w_out, gffn, wr):
    prev, nxt = _halo_specs(POOL_HALO, POOL_DIM)
    return pl.pallas_call(
        functools.partial(_mid_even_kernel, l),
        grid=(NT,),
        in_specs=[_tile_spec(POOL_DIM), prev, nxt] + _pop_specs(TM, Q_DIM) + _pop_specs(TM, D) + [
                  _MOD_SPEC, _layer_spec((len(POOL_WINDOWS), POOL_GROUP_DIM, POOL_GROUP_DIM), e),
                  _layer_spec((1, POOL_DIM), e), _layer_spec((D, D), e), _full_spec((DEPTH, D)),
                  _layer_spec((2, D, LANES), l)],
        out_specs=_tail_out_specs(),
        out_shape=_TAIL_OUT_SHAPES,
        scratch_shapes=[pltpu.VMEM((len(POOL_WINDOWS), TM + 2 * POOL_HALO, POOL_GROUP_DIM), F32),
                        pltpu.VMEM((len(POOL_WINDOWS), TM, POOL_GROUP_DIM), F32)] + _tail_scratch(),
        compiler_params=_params(),
        name="mid_even",
    )(a, a, a, yb_c, yb_l, xc, xl, mod, w_pool, pool_scale, w_out, gffn, wr)


def _after_moe(l, r, x1_ref, moe_ref, mod_ref):
    return x1_ref[...] + mod_ref[l, pl.ds(r, 1), 5 * D:6 * D] * _from_token_tiles(moe_ref)


def _in_odd_kernel(l, x1_ref, moec_ref, moel_ref, mod_ref, g_ref, w1_ref, b1_ref, u_ref):
    i = pl.program_id(0) * PAIR
    r = _mod_row(i)

    def body(pop):
        x2 = _after_moe(l - 1, r, x1_ref, (moec_ref, moel_ref)[pop], mod_ref)
        shift = mod_ref[l, pl.ds(r, 1), 0:D]
        scale = mod_ref[l, pl.ds(r, 1), D:2 * D]
        h = _rms_mod(x2, g_ref[l:l + 1, :], scale, shift)
        u = _dot(h, w1_ref[...]) + b1_ref[...]
        u_ref[...] = u[:, :D] * (1.0 / (1.0 + jnp.exp(-u[:, D:])))

    _per_half(i, body)


def _in_odd(l, o, x1, moe_c, moe_l, mod, g, w1, b1):
    return pl.pallas_call(
        functools.partial(_in_odd_kernel, l),
        grid=(NT // PAIR,),
        in_specs=[_pair_spec(D)] + _pair_half_specs() + [
                  _MOD_SPEC, _full_spec((DEPTH, D)), _layer_spec((D, 2 * D), o), _layer_spec((1, 2 * D), o)],
        out_specs=_pair_spec(D),
        out_shape=jax.ShapeDtypeStruct((T_ALL, D), F32),
        compiler_params=_params(),
        name="in_odd",
    )(x1, moe_c, moe_l, mod, g, w1, b1)


CONV_PAD_ROWS = TM + 2 * CONV_HALO
CONV_CHUNKS = D // LANES
CONV_BLOCK = 64


def _mid_odd_kernel(l, u_ref, up_ref, un_ref, x1_ref, moec_ref, moel_ref, mod_ref, dw_ref, dwb_ref, lng_ref, lnb_ref,
                    w2_ref, b2_ref, gffn_ref, wr_ref, *refs):
    outs, scratch = refs[:N_TAIL_OUT], refs[-N_TAIL_SCRATCH:]
    pad_ref, conv_ref, x_ref = refs[N_TAIL_OUT:-N_TAIL_SCRATCH]
    i = pl.program_id(0)

    def residual(pop):
        x_ref[...] = _after_moe(l - 1, _mod_row(i), x1_ref, (moec_ref, moel_ref)[pop], mod_ref)

    _per_half(i, residual)
    r = _mod_row(i)
    _, _, first, last = _seq_info(i)
    u = u_ref[...]
    up = jnp.where(first, 0.0, up_ref[...])
    un = jnp.where(last, 0.0, un_ref[...])
    half = CONV_WIDTH // 2
    first_row = CONV_HALO - half
    for c in range(CONV_CHUNKS):
        cols = slice(c * LANES, (c + 1) * LANES)
        pad_ref[c, 0:CONV_HALO, :] = up[:, cols]
        pad_ref[c, CONV_HALO:CONV_HALO + TM, :] = u[:, cols]
        pad_ref[c, CONV_HALO + TM:, :] = un[:, cols]
        bias = jnp.broadcast_to(dwb_ref[:, cols], (CONV_BLOCK, LANES))
        for r0 in range(0, TM, 2 * CONV_BLOCK):
            acc_even, acc_odd = bias, bias
            for s in range(first_row, first_row + CONV_WIDTH + 1):
                win = pad_ref[c, pl.ds(r0 + s, CONV_BLOCK, stride=2), :]
                j = s - first_row
                if j < CONV_WIDTH:
                    acc_even = acc_even + win * dw_ref[j:j + 1, cols]
                if j >= 1:
                    acc_odd = acc_odd + win * dw_ref[j - 1:j, cols]
            conv_ref[c, pl.ds(r0, CONV_BLOCK, stride=2), :] = acc_even
            conv_ref[c, pl.ds(r0 + 1, CONV_BLOCK, stride=2), :] = acc_odd
    acc = jnp.concatenate([conv_ref[c] for c in range(CONV_CHUNKS)], axis=1)
    mu = jnp.mean(acc, axis=-1, keepdims=True)
    cen = acc - mu
    var = jnp.mean(cen * cen, axis=-1, keepdims=True)
    v = _silu(cen * lax.rsqrt(var + EPS) * lng_ref[...] + lnb_ref[...])
    y = _dot(v, w2_ref[...]) + b2_ref[...]
    _mixer_tail(l, x_ref[...], y, mod_ref, r, gffn_ref, wr_ref, outs, scratch)


def _mid_odd(l, o, u, x1, moe_c, moe_l, mod, dw, dwb, lng, lnb, w2, b2, gffn, wr):
    prev, nxt = _halo_specs(CONV_HALO, D)
    return pl.pallas_call(
        functools.partial(_mid_odd_kernel, l),
        grid=(NT,),
        in_specs=[_tile_spec(D), prev, nxt, _tile_spec(D)] + _half_specs(TM * TOK_ROWS, LANES) + [
                  _MOD_SPEC, _layer_spec((CONV_WIDTH, D), o),
                  _layer_spec((1, D), o), _layer_spec((1, D), o), _layer_spec((1, D), o),
                  _layer_spec((D, D), o), _layer_spec((1, D), o), _full_spec((DEPTH, D)),
                  _layer_spec((2, D, LANES), l)],
        out_specs=_tail_out_specs(),
        out_shape=_TAIL_OUT_SHAPES,
        scratch_shapes=[pltpu.VMEM((CONV_CHUNKS, CONV_PAD_ROWS, LANES), F32),
                        pltpu.VMEM((CONV_CHUNKS, TM, LANES), F32), pltpu.VMEM((TM, D), F32)] + _tail_scratch(),
        compiler_params=_params(),
        name="mid_odd",
    )(u, u, u, x1, moe_c, moe_l, mod, dw, dwb, lng, lnb, w2, b2, gffn, wr)


def _tile_tables_kernel(cnt_ref, te_c, tc_c, tw_c, te_l, tc_l, tw_l, nact_ref, second_ref, first_ref, wplan_ref):
    for p, (te_ref, tc_ref, tw_ref) in enumerate(((te_c, tc_c, tw_c), (te_l, tc_l, tw_l))):
        run = jnp.int32(0)
        ends = []
        for e in range(N_EXPERTS):
            first_ref[p * N_EXPERTS + e] = run
            run = run + (cnt_ref[p, e] + MOE_TM - 1) // MOE_TM
            ends.append(run)
        nact_ref[p] = run

        nxt1, nxt2 = jnp.int32(0), jnp.int32(0)
        plans = []
        for e in reversed(range(N_EXPERTS)):
            plans.append(nxt2)
            used = cnt_ref[p, e] > 0
            nxt2 = jnp.where(used, nxt1, nxt2)
            nxt1 = jnp.where(used, e + 1, nxt1)
        plans.reverse()
        second_ref[p] = nxt2
        order = jnp.int32(0)
        for e in range(N_EXPERTS):
            wplan_ref[p * N_EXPERTS + e] = order + (plans[e] << W_AHEAD_SHIFT)
            order = jnp.where(cnt_ref[p, e] > 0, jnp.where(order == W_BUFFERS - 1, 0, order + 1), order)

        def expert_of(tile, ends=ends):
            e = jnp.int32(0)
            for k in range(N_EXPERTS):
                e = e + jnp.where(tile >= ends[k], 1, 0)
            return e

        last_e = expert_of(run - 1)

        def tile_body(j, c, te_ref=te_ref, tc_ref=tc_ref, tw_ref=tw_ref, p=p, run=run, expert_of=expert_of,
                      last_e=last_e):
            e = jnp.minimum(expert_of(j), last_e)
            te_ref[j] = e
            chunk = jnp.where(j < run, j - first_ref[p * N_EXPERTS + e], 0)
            tc_ref[j] = chunk
            starts_expert = jnp.logical_and(j < run, chunk == 0)
            tw_ref[j] = wplan_ref[p * N_EXPERTS + e] + jnp.where(starts_expert, W_FIRST_UNIT, 0)
            return c

        lax.fori_loop(0, te_ref.shape[0], tile_body, 0)


def _tile_tables(cnt):
    smem = pl.BlockSpec(memory_space=pltpu.SMEM)
    shapes = [(MOE_TILES,)] * 6 + [(2,), (2,)]
    return pl.pallas_call(
        _tile_tables_kernel,
        in_specs=[smem],
        out_specs=[smem] * len(shapes),
        out_shape=[jax.ShapeDtypeStruct(s, I32) for s in shapes],
        scratch_shapes=[pltpu.SMEM((2 * N_EXPERTS,), I32), pltpu.SMEM((2 * N_EXPERTS,), I32)],
        name="tile_tables",
    )(cnt)


def _moe_kernel(l, pop, te_ref, tc_ref, tw_ref, nact_ref, second_ref, list_ref, h_ref, gate_ref,
                w1_hbm, w3_hbm, w2_hbm, o_ref, xs_ref, ys_ref, w1_buf, w3_buf, w2_buf, w_sem):
    j = pl.program_id(0)
    nact = nact_ref[pop]
    last_tile = te_ref.shape[0] - 1
    tile_rows = MOE_TM * TOK_ROWS

    def weight_copies(expert, buf):
        return [pltpu.make_async_copy(hbm.at[l, expert], vmem.at[buf], w_sem.at[buf, k])
                for k, (hbm, vmem) in enumerate(((w1_hbm, w1_buf), (w3_hbm, w3_buf), (w2_hbm, w2_buf)))]

    def list_base(tile):
        expert = te_ref[tile]
        block = expert >> LIST_PER_BLOCK_LOG2
        in_block = expert & ((1 << LIST_PER_BLOCK_LOG2) - 1)
        return (block * LIST_ROWS + tc_ref[tile] * (MOE_TM // LIST_SPAN)) * LANES + in_block * LIST_SPAN

    def tok_rows(base, r):
        first = list_ref[base + (r // LIST_SPAN) * LANES + r % LIST_SPAN]
        return pl.ds(pl.multiple_of(first, TOK_ROWS), TOK_ROWS)

    def buf_rows(buf, r):
        return pl.ds(pl.multiple_of(buf * tile_rows + r * TOK_ROWS, TOK_ROWS), TOK_ROWS)

    def gather(tile, buf):
        base = list_base(tile)
        for r in range(MOE_TM):
            xs_ref[buf_rows(buf, r), :] = h_ref[tok_rows(base, r), :]

    tile = jnp.minimum(j, last_tile)
    plan = tw_ref[tile]
    wbuf = plan & (W_FIRST_UNIT - 1)
    starts_expert = jnp.logical_and(plan & W_FIRST_UNIT != 0, j <= last_tile)
    ahead_expert = (plan >> W_AHEAD_SHIFT) - 1
    ahead_buf = jnp.where(wbuf == 0, W_BUFFERS - 1, wbuf - 1)

    @pl.when(j == 0)
    def _():
        for cp in weight_copies(te_ref[0], 0):
            cp.start()

        @pl.when(second_ref[pop] > 0)
        def _():
            for cp in weight_copies(second_ref[pop] - 1, 1):
                cp.start()

        o_ref[...] = jnp.zeros(o_ref.shape, F32)
        ys_ref[...] = jnp.zeros(ys_ref.shape, F32)
        gather(0, 0)

    @pl.when(starts_expert)
    def _():
        for cp in weight_copies(te_ref[tile], wbuf):
            cp.wait()

        @pl.when(ahead_expert >= 0)
        def _():
            for cp in weight_copies(ahead_expert, ahead_buf):
                cp.start()

    @pl.when(j <= nact)
    def _():
        cur = j & 1
        w1_ref, w3_ref, w2_ref = w1_buf.at[wbuf], w3_buf.at[wbuf], w2_buf.at[wbuf]
        x = jnp.concatenate([xs_ref[pl.ds(cur * tile_rows + s, MOE_TM, stride=TOK_ROWS), :]
                             for s in range(TOK_ROWS)], axis=1).astype(BF16)
        gather(jnp.minimum(j + 1, last_tile), 1 - cur)

        prev_base = list_base(jnp.maximum(j - 1, 0))
        for r0 in range(0, MOE_TM, MOVE_BATCH):
            dst = [tok_rows(prev_base, r0 + k) for k in range(MOVE_BATCH)]
            vals = [o_ref[dst[k], :] + ys_ref[buf_rows(1 - cur, r0 + k), :] for k in range(MOVE_BATCH)]
            for k in reversed(range(MOVE_BATCH)):
                o_ref[dst[k], :] = vals[k]

        hg = jnp.dot(x, w1_ref[...].astype(BF16), preferred_element_type=F32)
        hu = jnp.dot(x, w3_ref[...].astype(BF16), preferred_element_type=F32)
        y = _dot(_silu(hg) * hu, w2_ref[...])

        first_row = pl.multiple_of(tc_ref[tile] * (MOE_TM // LIST_SPAN), MOE_TM // LIST_SPAN)
        spread = jnp.concatenate(
            [jnp.broadcast_to(gate_ref[pl.ds(first_row + k, 1), :], (LIST_SPAN, LIST_COLS))
             for k in range(MOE_TM // LIST_SPAN)], axis=0)
        want = te_ref[tile] * LIST_SPAN + (lax.broadcasted_iota(I32, (MOE_TM, 1), 0) & (LIST_SPAN - 1))
        hit = jnp.logical_and(lax.broadcasted_iota(I32, (MOE_TM, LIST_COLS), 1) == want, j < nact)
        y = y * jnp.sum(jnp.where(hit, spread, 0.0), axis=1, keepdims=True)
        for s in range(TOK_ROWS):
            ys_ref[pl.ds(cur * tile_rows + s, MOE_TM, stride=TOK_ROWS), :] = y[:, s * LANES:(s + 1) * LANES]


def _moe(l, pop, h_tt, te, tc, tw, nact, second, rows, gate, w1, w3, w2):
    tp = T_HALF
    blk = pop
    n_tiles = te.shape[0]
    hbm = pl.BlockSpec(memory_space=pl.ANY)
    grid_spec = pltpu.PrefetchScalarGridSpec(
        num_scalar_prefetch=6,
        grid=(n_tiles + 1,),
        in_specs=[
            pl.BlockSpec((tp * TOK_ROWS, LANES), lambda j, *_: (blk, 0), pipeline_mode=pl.Buffered(1)),
            pl.BlockSpec((LIST_ROWS, LIST_COLS), lambda j, *_: (0, 0), pipeline_mode=pl.Buffered(1)),
            hbm, hbm, hbm,
        ],
        out_specs=pl.BlockSpec((tp * TOK_ROWS, LANES), lambda j, *_: (0, 0), pipeline_mode=pl.Buffered(1)),
        scratch_shapes=[pltpu.VMEM((2 * MOE_TM * TOK_ROWS, LANES), F32),
                        pltpu.VMEM((2 * MOE_TM * TOK_ROWS, LANES), F32),
                        pltpu.VMEM((W_BUFFERS, D, EXPERT_HIDDEN), F32),
                        pltpu.VMEM((W_BUFFERS, D, EXPERT_HIDDEN), F32),
                        pltpu.VMEM((W_BUFFERS, EXPERT_HIDDEN, D), F32),
                        pltpu.SemaphoreType.DMA((W_BUFFERS, 3))],
    )
    return pl.pallas_call(
        functools.partial(_moe_kernel, l, pop),
        grid_spec=grid_spec,
        out_shape=jax.ShapeDtypeStruct((tp * TOK_ROWS, LANES), F32),
        compiler_params=_params(),
        name="moe",
    )(te, tc, tw, nact, second, rows, h_tt, gate, w1, w3, w2)


def _moe_all(l, h_tt, cnt, rows_c, gate_c, rows_l, gate_l, w1, w3, w2):
    te_c, tc_c, tw_c, te_l, tc_l, tw_l, nact, second = _tile_tables(cnt)
    out_c = _moe(l, 0, h_tt, te_c, tc_c, tw_c, nact, second, rows_c.reshape(-1), gate_c, w1, w3, w2)
    out_l = _moe(l, 1, h_tt, te_l, tc_l, tw_l, nact, second, rows_l.reshape(-1), gate_l, w1, w3, w2)
    return out_c, out_l


def _final_kernel(l, x1_ref, moec_ref, moel_ref, mod_ref, g_ref, oc_ref, ol_ref):
    i = pl.program_id(0) * FINAL_GROUP
    r = _mod_row(i)

    def body(half, pop):
        x = _after_moe(l, r, x1_ref, (moec_ref, moel_ref)[half], mod_ref)
        ms = jnp.mean(x * x, axis=-1, keepdims=True)
        (oc_ref, ol_ref)[pop][...] = x * lax.rsqrt(ms + EPS) * g_ref[...]

    pl.when(i < MOE_SPLIT)(functools.partial(body, 0, 0))
    pl.when(jnp.logical_and(i >= MOE_SPLIT, i < NT_CTX))(functools.partial(body, 1, 0))
    pl.when(i >= NT_CTX)(functools.partial(body, 1, 1))


def _final(l, x1, moe_c, moe_l, mod, g):
    return pl.pallas_call(
        functools.partial(_final_kernel, l),
        grid=(NT // FINAL_GROUP,),
        in_specs=[_pair_spec(D, FINAL_GROUP)] + _pair_half_specs(FINAL_GROUP) + [_MOD_SPEC, _full_spec((1, D))],
        out_specs=_pair_pop_specs(D, FINAL_GROUP),
        out_shape=[jax.ShapeDtypeStruct((T_CTX, D), F32), jax.ShapeDtypeStruct((T_LAT, D), F32)],
        compiler_params=_params(),
        name="final_norm",
    )(x1, moe_c, moe_l, mod, g)


def _rope_tables():
    t = np.arange(LAT_LEN)
    row = (t // GRID_W).astype(np.float32)
    col = (t % GRID_W).astype(np.float32)
    inv = np.float32(ROPE_BASE) ** (-np.arange(0, ROPE_AXIS_DIM, 2, dtype=np.float32) / np.float32(ROPE_AXIS_DIM))

    def table(p):
        ang = (p[:, None] * inv[None, :]).astype(np.float32)
        ang = np.concatenate([ang, ang], axis=-1)
        return np.cos(ang).astype(np.float32), np.sin(ang).astype(np.float32)

    (cr, sr), (cc, sc) = table(row), table(col)
    cos = np.concatenate([cr, cc, cr, cc], axis=-1)
    sin = np.concatenate([sr, sc, sr, sc], axis=-1)
    return jnp.asarray(cos), jnp.asarray(sin)


def _router_weights(w_grp, w_exp):
    we = jnp.transpose(w_exp, (0, 2, 1, 3)).reshape(DEPTH, D, N_GROUPS * EPG)
    pad = jnp.zeros((DEPTH, D, LANES - N_GROUPS - N_GROUPS * EPG), F32)
    wr = jnp.concatenate([w_grp, we, pad], axis=-1)
    hi = lax.bitcast_convert_type(lax.bitcast_convert_type(wr, jnp.uint32) & jnp.uint32(0xFFFF0000), F32)
    return jnp.stack([hi.astype(BF16), (wr - hi).astype(BF16)], axis=1)


def kernel(x_prompt, x_sample, cache_k, cache_v, c, c_ctx, w_ada, b_ada, norm_mix_g, norm_ffn_g, w_in_ab, pool_w, pool_scale, attn_sink, w_out_ab, conv_w1, conv_b1, conv_dw, conv_dw_b, conv_ln_g, conv_ln_b, conv_w2, conv_b2, router_grp, router_exp, moe_w1, moe_w3, moe_w2, final_g):
    xc = x_prompt.reshape(T_CTX, D)
    xl = x_sample.reshape(T_LAT, D)
    cond8 = jnp.concatenate([c_ctx[None, :], c, jnp.zeros((SUBLANES - 1 - N_LAT_SEQ, D), F32)], axis=0)
    mod = _modulation(cond8, w_ada, b_ada)
    cos_t, sin_t = _rope_tables()
    wr = _router_weights(router_grp, router_exp)
    n_even, n_odd = w_in_ab.shape[0], conv_w1.shape[0]

    a, q, k_c, k_l, v_c, v_l, state_k, state_v = _in_even(0, 0, xc, xl, mod, norm_mix_g, w_in_ab, cos_t, sin_t)
    yb_c = _attn_ctx(0, attn_sink, q, k_c, v_c)
    yb_l = _attn_lat(0, attn_sink, q, k_l, v_l, cache_k, cache_v)
    x1, h2, *routing = _mid_even(0, 0, a, yb_c, yb_l, xc, xl, mod, pool_w,
                                 pool_scale.reshape(n_even, 1, POOL_DIM), w_out_ab, norm_ffn_g, wr)
    moe0 = _moe_all(0, h2, *routing, moe_w1, moe_w3, moe_w2)

    vec = lambda p: p.reshape(n_odd, 1, -1)
    u = _in_odd(1, 0, x1, *moe0, mod, norm_mix_g, conv_w1, vec(conv_b1))
    x3, h2b, *routing = _mid_odd(1, 0, u, x1, *moe0, mod, conv_dw, vec(conv_dw_b), vec(conv_ln_g),
                                 vec(conv_ln_b), conv_w2, vec(conv_b2), norm_ffn_g, wr)
    moe1 = _moe_all(1, h2b, *routing, moe_w1, moe_w3, moe_w2)
    y_c, y_l = _final(1, x3, *moe1, mod, final_g.reshape(1, D))

    y_prompt = y_c.reshape(N_CTX_SEQ, CTX_LEN, D)
    y_sample = y_l.reshape(N_LAT_SEQ, LAT_LEN, D)
    return (y_prompt, y_sample, state_k, state_v)
```

```python
import functools

import jax
import jax.numpy as jnp
import numpy as np
from jax import lax
from jax.experimental import pallas as pl
from jax.experimental.pallas import tpu as pltpu

F32 = jnp.float32
BF16 = jnp.bfloat16
I32 = jnp.int32

D = 1024
N_CTX_SEQ = 16
CTX_LEN = 256
N_LAT_SEQ = 2
LAT_LEN = 1024
T_CTX = N_CTX_SEQ * CTX_LEN
T_LAT = N_LAT_SEQ * LAT_LEN
T_ALL = T_CTX + T_LAT
TM = 256
NT = T_ALL // TM
NT_CTX = T_CTX // TM
LAT_TILES_PER_SEQ = LAT_LEN // TM
GRID_W = 64
DEPTH = 2

POOL_WINDOWS = (2, 4, 8, 16)
POOL_GROUP_DIM = 128
POOL_DIM = 512
HEAD_DIM = 64
N_Q_HEADS = 8
N_KV_HEADS = 2
Q_PER_KV = 4
Q_DIM = 512
KV_DIM = 128
IN_AB = 1280
ATTN_WINDOW = 128
ATTN_BLOCK = 128
ATTN_SCALE = HEAD_DIM ** -0.5
ROPE_BASE = 10000.0
ROPE_AXIS_DIM = 32
CONV_WIDTH = 31
CONV_HALO = 16
POOL_HALO = 8
N_GROUPS = 4
EPG = 4
N_EXPERTS = 16
EXPERT_HIDDEN = 512
EPS = 1e-6
NEG_BIG = -1e30

SUBLANES = 8
LANES = 128
TOK_ROWS = D // LANES

MOE_TM = 256
MOE_SPLIT = NT // 2
T_HALF = MOE_SPLIT * TM
MOE_TILES = 2 * T_HALF // MOE_TM + N_EXPERTS
W_BUFFERS = 3
W_FIRST_UNIT = 4
W_AHEAD_SHIFT = 3
MOVE_BATCH_LOG2 = 2
MOVE_BATCH = 1 << MOVE_BATCH_LOG2
VMEM_LIMIT = 56 * 1024 * 1024


def _silu(x):
    return x * (1.0 / (1.0 + jnp.exp(-x)))


def _mod_row(i):
    return jnp.where(i < NT_CTX, 0, 1 + (i - NT_CTX) // LAT_TILES_PER_SEQ)


def _seq_info(i):
    is_ctx = i < NT_CTX
    k = (i - NT_CTX) % LAT_TILES_PER_SEQ
    off = jnp.where(is_ctx, 0, k * TM)
    n = jnp.where(is_ctx, CTX_LEN, LAT_LEN)
    first = jnp.logical_or(is_ctx, k == 0)
    last = jnp.logical_or(is_ctx, k == LAT_TILES_PER_SEQ - 1)
    return off, n, first, last


def _rms_mod(x, g, scale, shift):
    ms = jnp.mean(x * x, axis=-1, keepdims=True)
    return (x * lax.rsqrt(ms + EPS) * g) * (1.0 + scale) + shift


def _dot(a, b):
    return jnp.dot(a.astype(BF16), b.astype(BF16), preferred_element_type=F32)


def _dot_nt(a, b):
    return lax.dot_general(a.astype(BF16), b.astype(BF16), (((1,), (1,)), ((), ())),
                           preferred_element_type=F32)


def _from_token_tiles(ref):
    rows = ref.shape[0] // TOK_ROWS
    return jnp.concatenate([ref[pl.ds(s, rows, stride=TOK_ROWS), :] for s in range(TOK_ROWS)], axis=1)


def _tile_spec(width):
    return pl.BlockSpec((TM, width), lambda i: (i, 0))


def _full_spec(shape):
    return pl.BlockSpec(shape, lambda i: (0,) * len(shape))


def _layer_spec(shape, l):
    return pl.BlockSpec((None,) + tuple(shape), lambda i: (l,) + (0,) * len(shape))


def _pop_specs(rows, width):
    return [pl.BlockSpec((rows, width), lambda i: (jnp.minimum(i, NT_CTX - 1), 0)),
            pl.BlockSpec((rows, width), lambda i: (jnp.maximum(i - NT_CTX, 0), 0))]


def _per_population(i, body):
    pl.when(i < NT_CTX)(functools.partial(body, 0))
    pl.when(i >= NT_CTX)(functools.partial(body, 1))


def _half_specs(rows, width):
    return [pl.BlockSpec((rows, width), lambda i: (jnp.minimum(i, MOE_SPLIT - 1), 0)),
            pl.BlockSpec((rows, width), lambda i: (jnp.maximum(i - MOE_SPLIT, 0), 0))]


def _per_half(i, body):
    pl.when(i < MOE_SPLIT)(functools.partial(body, 0))
    pl.when(i >= MOE_SPLIT)(functools.partial(body, 1))


PAIR = 2
assert LAT_TILES_PER_SEQ % PAIR == 0 and MOE_SPLIT % PAIR == 0 and NT_CTX % PAIR == 0


def _pair_spec(width):
    return pl.BlockSpec((PAIR * TM, width), lambda i: (i, 0))


def _pair_half_specs():
    split = MOE_SPLIT // PAIR
    block = (PAIR * TM * TOK_ROWS, LANES)
    return [pl.BlockSpec(block, lambda i: (jnp.minimum(i, split - 1), 0)),
            pl.BlockSpec(block, lambda i: (jnp.maximum(i - split, 0), 0))]


def _pair_pop_specs(width):
    n_ctx = NT_CTX // PAIR
    return [pl.BlockSpec((PAIR * TM, width), lambda i: (jnp.minimum(i, n_ctx - 1), 0)),
            pl.BlockSpec((PAIR * TM, width), lambda i: (jnp.maximum(i - n_ctx, 0), 0))]


_MOD_SPEC = pl.BlockSpec((DEPTH, SUBLANES, 6 * D), lambda i: (0, 0, 0))


def _params():
    return pltpu.CompilerParams(vmem_limit_bytes=VMEM_LIMIT)


MOD_TN = 1536


def _mod_kernel(cond_ref, w_ref, b_ref, o_ref):
    s = _silu(cond_ref[...])
    o_ref[0] = _dot(s, w_ref[0]) + b_ref[0]


def _modulation(cond8, w_ada, b_ada):
    return pl.pallas_call(
        _mod_kernel,
        grid=(DEPTH, 6 * D // MOD_TN),
        in_specs=[
            pl.BlockSpec((SUBLANES, D), lambda l, n: (0, 0)),
            pl.BlockSpec((1, D, MOD_TN), lambda l, n: (l, 0, n)),
            pl.BlockSpec((1, 1, MOD_TN), lambda l, n: (l, 0, n)),
        ],
        out_specs=pl.BlockSpec((1, SUBLANES, MOD_TN), lambda l, n: (l, 0, n)),
        out_shape=jax.ShapeDtypeStruct((DEPTH, SUBLANES, 6 * D), F32),
        compiler_params=_params(),
        name="modulation",
    )(cond8, w_ada, b_ada.reshape(DEPTH, 1, 6 * D))


def _rope_chunk(xc, cos, sin):
    lane = lax.broadcasted_iota(I32, xc.shape, 1)
    first = (lane % ROPE_AXIS_DIM) < (ROPE_AXIS_DIM // 2)
    rot = jnp.where(first, -pltpu.roll(xc, LANES - ROPE_AXIS_DIM // 2, 1), pltpu.roll(xc, ROPE_AXIS_DIM // 2, 1))
    return xc * cos + rot * sin


def _in_even_kernel(l, xc_ref, xl_ref, mod_ref, g_ref, w_ref, cos_ref, sin_ref,
                    a_ref, q_ref, kc_ref, kl_ref, vc_ref, vl_ref, sk_ref, sv_ref):
    i = pl.program_id(0) * PAIR
    r = _mod_row(i)

    def put_state(s_ref, val):
        for sq in range(PAIR):
            for hd in range(N_KV_HEADS):
                s_ref[sq, 0, :, hd, :] = val[sq * TM:(sq + 1) * TM, hd * HEAD_DIM:(hd + 1) * HEAD_DIM]

    def body(pop):
        x = (xc_ref, xl_ref)[pop][...]
        k_ref, v_ref = (kc_ref, kl_ref)[pop], (vc_ref, vl_ref)[pop]
        shift = mod_ref[l, pl.ds(r, 1), 0:D]
        scale = mod_ref[l, pl.ds(r, 1), D:2 * D]
        h = _rms_mod(x, g_ref[l:l + 1, :], scale, shift)
        proj = _dot(h, w_ref[...])
        a_ref[...] = proj[:, :POOL_DIM]
        v_ref[...] = proj[:, POOL_DIM + Q_DIM + KV_DIM:]
        if pop == 0:
            q_ref[...] = proj[:, POOL_DIM:POOL_DIM + Q_DIM].astype(BF16)
            k_ref[...] = proj[:, POOL_DIM + Q_DIM:POOL_DIM + Q_DIM + KV_DIM]
            put_state(sk_ref, proj[:, POOL_DIM + Q_DIM:POOL_DIM + Q_DIM + KV_DIM])
            put_state(sv_ref, proj[:, POOL_DIM + Q_DIM + KV_DIM:])
        else:
            off, _, _, _ = _seq_info(i)
            off = pl.multiple_of(off, TM)
            cos = cos_ref[pl.ds(off, PAIR * TM), :]
            sin = sin_ref[pl.ds(off, PAIR * TM), :]
            for c in range(Q_DIM // LANES):
                lo = POOL_DIM + c * LANES
                q_ref[:, c * LANES:(c + 1) * LANES] = _rope_chunk(proj[:, lo:lo + LANES], cos, sin).astype(BF16)
            lo = POOL_DIM + Q_DIM
            k_ref[...] = _rope_chunk(proj[:, lo:lo + LANES], cos, sin)

    _per_population(i, body)


def _in_even(l, e, xc, xl, mod, g, w_in, cos_t, sin_t):
    assert CTX_LEN == TM
    state_shape = (N_CTX_SEQ, 1, CTX_LEN, N_KV_HEADS, HEAD_DIM)
    state_spec = pl.BlockSpec((PAIR, 1, CTX_LEN, N_KV_HEADS, HEAD_DIM),
                              lambda i: (jnp.minimum(i, NT_CTX // PAIR - 1), 0, 0, 0, 0))
    return pl.pallas_call(
        functools.partial(_in_even_kernel, l),
        grid=(NT // PAIR,),
        in_specs=_pair_pop_specs(D) + [_MOD_SPEC, _full_spec((DEPTH, D)), _layer_spec((D, IN_AB), e),
                  _full_spec((LAT_LEN, LANES)), _full_spec((LAT_LEN, LANES))],
        out_specs=([_pair_spec(POOL_DIM), _pair_spec(Q_DIM)] + _pair_pop_specs(KV_DIM) + _pair_pop_specs(KV_DIM)
                   + [state_spec, state_spec]),
        out_shape=[jax.ShapeDtypeStruct((T_ALL, POOL_DIM), F32), jax.ShapeDtypeStruct((T_ALL, Q_DIM), BF16)]
                  + [jax.ShapeDtypeStruct((t, KV_DIM), F32) for t in (T_CTX, T_LAT, T_CTX, T_LAT)]
                  + [jax.ShapeDtypeStruct(state_shape, F32)] * 2,
        compiler_params=_params(),
        name="in_even",
    )(xc, xl, mod, g, w_in, cos_t, sin_t)


def _attn_ctx_kernel(e, sink_ref, q_ref, k_ref, v_ref, o_ref):
    q = q_ref[...]
    k = k_ref[...]
    v = v_ref[...]
    for h in range(N_KV_HEADS):
        kh = k[:, h * HEAD_DIM:(h + 1) * HEAD_DIM]
        vh = v[:, h * HEAD_DIM:(h + 1) * HEAD_DIM]
        for g in range(Q_PER_KV):
            j = h * Q_PER_KV + g
            s = _dot_nt(q[:, j * HEAD_DIM:(j + 1) * HEAD_DIM], kh) * ATTN_SCALE
            sk = sink_ref[e, j]
            m = jnp.maximum(jnp.max(s, axis=-1, keepdims=True), sk)
            p = jnp.exp(s - m)
            denom = jnp.sum(p, axis=-1, keepdims=True) + jnp.exp(sk - m)
            o_ref[:, j * HEAD_DIM:(j + 1) * HEAD_DIM] = (_dot(p, vh) / denom).astype(o_ref.dtype)


def _attn_ctx(e, sink, q, k, v):
    tile = lambda w: pl.BlockSpec((TM, w), lambda b: (b, 0))
    return pl.pallas_call(
        functools.partial(_attn_ctx_kernel, e),
        grid=(N_CTX_SEQ,),
        in_specs=[pl.BlockSpec(memory_space=pltpu.SMEM), tile(Q_DIM), tile(KV_DIM), tile(KV_DIM)],
        out_specs=tile(Q_DIM),
        out_shape=jax.ShapeDtypeStruct((T_CTX, Q_DIM), BF16),
        compiler_params=_params(),
        name="attn_ctx",
    )(sink, q, k, v)


LAT_BLOCKS = LAT_LEN // ATTN_BLOCK
BAND = 3 * ATTN_BLOCK


def _attn_lat_kernel(e, sink_ref, q_ref, k_ref, v_ref, ck_ref, cv_ref, o_ref):
    n = pl.program_id(1)
    start = jnp.clip((n - 1) * ATTN_BLOCK, 0, LAT_LEN - BAND)
    start = pl.multiple_of(start, ATTN_BLOCK)
    q = q_ref[...]
    kw = k_ref[pl.ds(start, BAND), :]
    vw = v_ref[pl.ds(start, BAND), :]
    qpos = n * ATTN_BLOCK + lax.broadcasted_iota(I32, (ATTN_BLOCK, BAND), 0)
    kpos = start + lax.broadcasted_iota(I32, (ATTN_BLOCK, BAND), 1)
    valid = jnp.abs(qpos - kpos) <= ATTN_WINDOW
    for h in range(N_KV_HEADS):
        hs = slice(h * HEAD_DIM, (h + 1) * HEAD_DIM)
        ck = ck_ref[0, 0, :, h, :]
        cv = cv_ref[0, 0, :, h, :]
        for g in range(Q_PER_KV):
            j = h * Q_PER_KV + g
            qj = q[:, j * HEAD_DIM:(j + 1) * HEAD_DIM]
            s_loc = jnp.where(valid, _dot_nt(qj, kw[:, hs]) * ATTN_SCALE, NEG_BIG)
            s_ctx = _dot_nt(qj, ck) * ATTN_SCALE
            sk = sink_ref[e, j]
            m = jnp.maximum(jnp.maximum(jnp.max(s_loc, axis=-1, keepdims=True),
                                        jnp.max(s_ctx, axis=-1, keepdims=True)), sk)
            p_loc = jnp.exp(s_loc - m)
            p_ctx = jnp.exp(s_ctx - m)
            denom = (jnp.sum(p_loc, axis=-1, keepdims=True) + jnp.sum(p_ctx, axis=-1, keepdims=True)
                     + jnp.exp(sk - m))
            o = _dot(p_loc, vw[:, hs]) + _dot(p_ctx, cv)
            o_ref[:, j * HEAD_DIM:(j + 1) * HEAD_DIM] = (o / denom).astype(o_ref.dtype)


def _attn_lat(e, sink, q, k, v, ck, cv):
    past = ck.shape[2]
    cache_spec = pl.BlockSpec((1, 1, past, N_KV_HEADS, HEAD_DIM), lambda b, n: (b, e, 0, 0, 0))
    q_blk0 = T_CTX // ATTN_BLOCK
    q_spec = pl.BlockSpec((ATTN_BLOCK, Q_DIM), lambda b, n: (q_blk0 + b * LAT_BLOCKS + n, 0))
    return pl.pallas_call(
        functools.partial(_attn_lat_kernel, e),
        grid=(N_LAT_SEQ, LAT_BLOCKS),
        in_specs=[pl.BlockSpec(memory_space=pltpu.SMEM),
                  q_spec,
                  pl.BlockSpec((LAT_LEN, KV_DIM), lambda b, n: (b, 0)),
                  pl.BlockSpec((LAT_LEN, KV_DIM), lambda b, n: (b, 0)),
                  cache_spec, cache_spec],
        out_specs=pl.BlockSpec((ATTN_BLOCK, Q_DIM), lambda b, n: (b * LAT_BLOCKS + n, 0)),
        out_shape=jax.ShapeDtypeStruct((T_LAT, Q_DIM), BF16),
        compiler_params=_params(),
        name="attn_lat",
    )(sink, q, k, v, ck, cv)


def _router(h2, wr_ref):
    hi = h2.astype(BF16)
    lo = (h2 - hi.astype(F32)).astype(BF16)
    whi = wr_ref[0]
    wlo = wr_ref[1]
    dot = functools.partial(jnp.dot, preferred_element_type=F32)
    logits_t = (dot(hi, whi) + (dot(lo, whi) + dot(hi, wlo))).T
    row = lambda k: logits_t[k:k + 1, :]

    def first_max(vals):
        best = functools.reduce(jnp.maximum, vals)
        idx = jnp.full(best.shape, len(vals) - 1, F32)
        for k in reversed(range(len(vals) - 1)):
            idx = jnp.where(vals[k] == best, float(k), idx)
        return best, idx

    grp = [row(k) for k in range(N_GROUPS)]
    gmax, g_idx = first_max(grp)
    g_w = 1.0 / sum(jnp.exp(gk - gmax) for gk in grp)
    el = []
    for e in range(EPG):
        v = row(N_GROUPS + (N_GROUPS - 1) * EPG + e)
        for k in reversed(range(N_GROUPS - 1)):
            v = jnp.where(g_idx == k, row(N_GROUPS + k * EPG + e), v)
        el.append(v)
    t1, i1 = first_max(el)
    t2, i2 = first_max([jnp.where(i1 == e, -jnp.inf, el[e]) for e in range(EPG)])
    d = jnp.exp(t2 - t1)
    w1 = g_w / (1.0 + d)
    w2 = g_w * d / (1.0 + d)
    return EPG * g_idx + i1, EPG * g_idx + i2, w1, w2


LIST_SPAN_LOG2 = 5
LIST_SPAN = 1 << LIST_SPAN_LOG2
LIST_ROWS = LANES
LIST_COLS = N_EXPERTS * LIST_SPAN
LIST_BLOCKS = LIST_COLS // LANES
LIST_PER_BLOCK_LOG2 = 2


N_TAIL_OUT = 7
N_TAIL_SCRATCH = 3


def _mixer_tail(l, x, y, mod_ref, r, gffn_ref, wr_ref, outs, scratch):
    x1_ref, h2_ref, cnt_ref, lc_ref, gc_ref, ll_ref, gl_ref = outs
    run_ref, acc_ref, gacc_ref = scratch
    i = pl.program_id(0)
    g1 = mod_ref[l, pl.ds(r, 1), 2 * D:3 * D]
    shift2 = mod_ref[l, pl.ds(r, 1), 3 * D:4 * D]
    scale2 = mod_ref[l, pl.ds(r, 1), 4 * D:5 * D]
    x1 = x + g1 * y
    x1_ref[...] = x1
    h2 = _rms_mod(x1, gffn_ref[l:l + 1, :], scale2, shift2)
    for s in range(TOK_ROWS):
        h2_ref[pl.ds(s, TM, stride=TOK_ROWS), :] = h2[:, s * LANES:(s + 1) * LANES]
    e1, e2, w1_row, w2_row = _router(h2, wr_ref)

    @pl.when(jnp.logical_or(i == 0, i == MOE_SPLIT))
    def _():
        run_ref[...] = jnp.zeros(run_ref.shape, F32)
        acc_ref[...] = jnp.zeros(acc_ref.shape, F32)
        gacc_ref[...] = jnp.zeros(gacc_ref.shape, F32)

    @pl.when(i == 0)
    def _():
        cnt_ref[...] = jnp.zeros(cnt_ref.shape, I32)

    sub_i = lax.broadcasted_iota(I32, (LANES, TM), 0)
    sub = sub_i.astype(F32)
    member = jnp.where(jnp.logical_or(sub == e1, sub == e2), 1.0, 0.0)
    tri = jnp.where(lax.broadcasted_iota(I32, (TM, TM), 0) <= lax.broadcasted_iota(I32, (TM, TM), 1), 1.0, 0.0)
    csum = jnp.dot(member.astype(BF16), tri.astype(BF16), preferred_element_type=F32)
    run = run_ref[:, 0:1]
    before = csum - member + run
    run_new = run + csum[:, TM - 1:TM]
    run_ref[...] = jnp.broadcast_to(run_new, run_ref.shape)

    col_i = lax.broadcasted_iota(I32, (TM, LIST_COLS), 1)
    rows_oh, col_rows = [], []
    for e in (e1, e2):
        rank = jnp.zeros((1, TM), F32)
        for k in range(N_EXPERTS):
            rank = rank + jnp.where(e == k, before[k:k + 1, :], 0.0)
        rank_i = rank.astype(I32)
        rows_oh.append(jnp.where(sub_i == (rank_i >> LIST_SPAN_LOG2), 1.0, 0.0))
        col_rows.append(e.astype(I32) * LIST_SPAN + (rank_i & (LIST_SPAN - 1)))
    rows_t = jnp.concatenate(rows_oh, axis=1)
    col_cols = jnp.concatenate(col_rows + [jnp.zeros((LANES - 2, TM), I32)], axis=0).T
    cols = jnp.concatenate([jnp.where(col_i == col_cols[:, k:k + 1], 1.0, 0.0).astype(BF16) for k in range(2)],
                           axis=0)
    tok = (lax.broadcasted_iota(I32, (1, 2 * TM), 1) % TM).astype(F32)
    rest = jnp.concatenate([w1_row, w2_row], axis=1)
    values = [None, tok]
    for _ in range(3):
        piece = rest.astype(BF16).astype(F32)
        values.append(piece)
        rest = rest - piece
    planes = jnp.concatenate([rows_t if v is None else rows_t * v for v in values], axis=0).astype(BF16)
    out = jnp.dot(planes, cols, preferred_element_type=F32)
    hits, tok_sum, gp0, gp1, gp2 = [out[k * LANES:(k + 1) * LANES] for k in range(len(values))]
    tile_in_pop = jnp.where(i < MOE_SPLIT, i, i - MOE_SPLIT).astype(F32)
    acc = acc_ref[...] + TOK_ROWS * (tok_sum + TM * tile_in_pop * hits)
    acc_ref[...] = acc
    gacc = gacc_ref[...] + gp0 + gp1 + gp2
    gacc_ref[...] = gacc

    def counts_row():
        return jnp.broadcast_to(run_new, (LANES, LANES)).T[0:1, :].astype(I32)

    def put_rows(dst_ref):
        for b in range(LIST_BLOCKS):
            dst_ref[b * LIST_ROWS:(b + 1) * LIST_ROWS, :] = acc[:, b * LANES:(b + 1) * LANES].astype(I32)

    @pl.when(i == MOE_SPLIT - 1)
    def _():
        cnt_ref[0:1, :] = counts_row()
        put_rows(lc_ref)
        gc_ref[...] = gacc

    @pl.when(i == NT - 1)
    def _():
        cnt_ref[1:2, :] = counts_row()
        put_rows(ll_ref)
        gl_ref[...] = gacc


_LIST_OUT_SHAPES = [(LIST_BLOCKS * LIST_ROWS, LANES), (LIST_ROWS, LIST_COLS)] * 2
_TAIL_OUT_SHAPES = [jax.ShapeDtypeStruct((T_ALL, D), F32),
                    jax.ShapeDtypeStruct((T_ALL * TOK_ROWS, LANES), F32),
                    jax.ShapeDtypeStruct((SUBLANES, LANES), I32)] + [
                        jax.ShapeDtypeStruct(s, dt) for s, dt in zip(_LIST_OUT_SHAPES, (I32, F32, I32, F32))]


def _tail_scratch():
    return [pltpu.VMEM((LANES, LANES), F32), pltpu.VMEM((LIST_ROWS, LIST_COLS), F32),
            pltpu.VMEM((LIST_ROWS, LIST_COLS), F32)]


def _tail_out_specs():
    return [_tile_spec(D),
            pl.BlockSpec((TM * TOK_ROWS, LANES), lambda i: (i, 0)),
            _full_spec((SUBLANES, LANES))] + [_full_spec(s) for s in _LIST_OUT_SHAPES]


def _halo_specs(halo, width):
    per = TM // halo
    last = T_ALL // halo - 1
    prev = pl.BlockSpec((halo, width), lambda i: (jnp.maximum(i * per - 1, 0), 0))
    nxt = pl.BlockSpec((halo, width), lambda i: (jnp.minimum((i + 1) * per, last), 0))
    return prev, nxt


def _mid_even_kernel(l, a_ref, ap_ref, an_ref, ybc_ref, ybl_ref, xc_ref, xl_ref, mod_ref, wp_ref, ps_ref, wo_ref,
                     gffn_ref, wr_ref, *refs):
    outs, (pad_ref, pool_ref), scratch = refs[:N_TAIL_OUT], refs[N_TAIL_OUT:-N_TAIL_SCRATCH], refs[-N_TAIL_SCRATCH:]
    i = pl.program_id(0)
    r = _mod_row(i)
    off, n, first, last = _seq_info(i)
    a = a_ref[...]
    ap = jnp.where(first, 0.0, ap_ref[...])
    an = jnp.where(last, 0.0, an_ref[...])
    half_rows = TM // 2
    pos = [(off + 2 * lax.broadcasted_iota(I32, (half_rows, 1), 0) + parity).astype(F32) for parity in (0, 1)]
    last_pos = (n - 1).astype(F32)
    mixed = []
    for g, w in enumerate(POOL_WINDOWS):
        lo = w // 2
        hi = w - lo - 1
        cols = slice(g * POOL_GROUP_DIM, (g + 1) * POOL_GROUP_DIM)
        pad_ref[g, 0:POOL_HALO, :] = ap[:, cols]
        pad_ref[g, POOL_HALO:POOL_HALO + TM, :] = a[:, cols]
        pad_ref[g, POOL_HALO + TM:, :] = an[:, cols]
        loads = {s: pad_ref[g, pl.ds(s, half_rows, stride=2), :]
                 for s in range(POOL_HALO - lo, POOL_HALO + hi + 2)}
        for parity in (0, 1):
            total = functools.reduce(lambda x, y: x + y,
                                     [loads[POOL_HALO + parity + j] for j in range(-lo, hi + 1)])
            count = jnp.minimum(pos[parity] + hi, last_pos) - jnp.maximum(pos[parity] - lo, 0.0) + 1.0
            pool_ref[g, pl.ds(parity, half_rows, stride=2), :] = total / count - loads[POOL_HALO + parity]
        mixed.append(_dot(pool_ref[g], wp_ref[g]))
    ya = jnp.concatenate(mixed, axis=1) * ps_ref[...]
    y_pool = _dot(ya, wo_ref[0:POOL_DIM, :])

    def body(pop):
        y = y_pool + _dot((ybc_ref, ybl_ref)[pop][...], wo_ref[POOL_DIM:, :])
        _mixer_tail(l, (xc_ref, xl_ref)[pop][...], y, mod_ref, r, gffn_ref, wr_ref, outs, scratch)

    _per_population(i, body)


def _mid_even(l, e, a, yb_c, yb_l, xc, xl, mod, w_pool, pool_scale, w_out, gffn, wr):
    prev, nxt = _halo_specs(POOL_HALO, POOL_DIM)
    return pl.pallas_call(
        functools.partial(_mid_even_kernel, l),
        grid=(NT,),
        in_specs=[_tile_spec(POOL_DIM), prev, nxt] + _pop_specs(TM, Q_DIM) + _pop_specs(TM, D) + [
                  _MOD_SPEC, _layer_spec((len(POOL_WINDOWS), POOL_GROUP_DIM, POOL_GROUP_DIM), e),
                  _layer_spec((1, POOL_DIM), e), _layer_spec((D, D), e), _full_spec((DEPTH, D)),
                  _layer_spec((2, D, LANES), l)],
        out_specs=_tail_out_specs(),
        out_shape=_TAIL_OUT_SHAPES,
        scratch_shapes=[pltpu.VMEM((len(POOL_WINDOWS), TM + 2 * POOL_HALO, POOL_GROUP_DIM), F32),
                        pltpu.VMEM((len(POOL_WINDOWS), TM, POOL_GROUP_DIM), F32)] + _tail_scratch(),
        compiler_params=_params(),
        name="mid_even",
    )(a, a, a, yb_c, yb_l, xc, xl, mod, w_pool, pool_scale, w_out, gffn, wr)


def _after_moe(l, r, x1_ref, moe_ref, mod_ref):
    return x1_ref[...] + mod_ref[l, pl.ds(r, 1), 5 * D:6 * D] * _from_token_tiles(moe_ref)


def _in_odd_kernel(l, x1_ref, moec_ref, moel_ref, mod_ref, g_ref, w1_ref, b1_ref, u_ref):
    i = pl.program_id(0) * PAIR
    r = _mod_row(i)

    def body(pop):
        x2 = _after_moe(l - 1, r, x1_ref, (moec_ref, moel_ref)[pop], mod_ref)
        shift = mod_ref[l, pl.ds(r, 1), 0:D]
        scale = mod_ref[l, pl.ds(r, 1), D:2 * D]
        h = _rms_mod(x2, g_ref[l:l + 1, :], scale, shift)
        u = _dot(h, w1_ref[...]) + b1_ref[...]
        u_ref[...] = u[:, :D] * (1.0 / (1.0 + jnp.exp(-u[:, D:])))

    _per_half(i, body)


def _in_odd(l, o, x1, moe_c, moe_l, mod, g, w1, b1):
    return pl.pallas_call(
        functools.partial(_in_odd_kernel, l),
        grid=(NT // PAIR,),
        in_specs=[_pair_spec(D)] + _pair_half_specs() + [
                  _MOD_SPEC, _full_spec((DEPTH, D)), _layer_spec((D, 2 * D), o), _layer_spec((1, 2 * D), o)],
        out_specs=_pair_spec(D),
        out_shape=jax.ShapeDtypeStruct((T_ALL, D), F32),
        compiler_params=_params(),
        name="in_odd",
    )(x1, moe_c, moe_l, mod, g, w1, b1)


CONV_PAD_ROWS = TM + 2 * CONV_HALO
CONV_CHUNKS = D // LANES
CONV_BLOCK = 64


def _mid_odd_kernel(l, u_ref, up_ref, un_ref, x1_ref, moec_ref, moel_ref, mod_ref, dw_ref, dwb_ref, lng_ref, lnb_ref,
                    w2_ref, b2_ref, gffn_ref, wr_ref, *refs):
    outs, scratch = refs[:N_TAIL_OUT], refs[-N_TAIL_SCRATCH:]
    pad_ref, conv_ref, x_ref = refs[N_TAIL_OUT:-N_TAIL_SCRATCH]
    i = pl.program_id(0)

    def residual(pop):
        x_ref[...] = _after_moe(l - 1, _mod_row(i), x1_ref, (moec_ref, moel_ref)[pop], mod_ref)

    _per_half(i, residual)
    r = _mod_row(i)
    _, _, first, last = _seq_info(i)
    u = u_ref[...]
    up = jnp.where(first, 0.0, up_ref[...])
    un = jnp.where(last, 0.0, un_ref[...])
    half = CONV_WIDTH // 2
    first_row = CONV_HALO - half
    for c in range(CONV_CHUNKS):
        cols = slice(c * LANES, (c + 1) * LANES)
        pad_ref[c, 0:CONV_HALO, :] = up[:, cols]
        pad_ref[c, CONV_HALO:CONV_HALO + TM, :] = u[:, cols]
        pad_ref[c, CONV_HALO + TM:, :] = un[:, cols]
        bias = jnp.broadcast_to(dwb_ref[:, cols], (CONV_BLOCK, LANES))
        for r0 in range(0, TM, 2 * CONV_BLOCK):
            acc_even, acc_odd = bias, bias
            for s in range(first_row, first_row + CONV_WIDTH + 1):
                win = pad_ref[c, pl.ds(r0 + s, CONV_BLOCK, stride=2), :]
                j = s - first_row
                if j < CONV_WIDTH:
                    acc_even = acc_even + win * dw_ref[j:j + 1, cols]
                if j >= 1:
                    acc_odd = acc_odd + win * dw_ref[j - 1:j, cols]
            conv_ref[c, pl.ds(r0, CONV_BLOCK, stride=2), :] = acc_even
            conv_ref[c, pl.ds(r0 + 1, CONV_BLOCK, stride=2), :] = acc_odd
    acc = jnp.concatenate([conv_ref[c] for c in range(CONV_CHUNKS)], axis=1)
    mu = jnp.mean(acc, axis=-1, keepdims=True)
    cen = acc - mu
    var = jnp.mean(cen * cen, axis=-1, keepdims=True)
    v = _silu(cen * lax.rsqrt(var + EPS) * lng_ref[...] + lnb_ref[...])
    y = _dot(v, w2_ref[...]) + b2_ref[...]
    _mixer_tail(l, x_ref[...], y, mod_ref, r, gffn_ref, wr_ref, outs, scratch)


def _mid_odd(l, o, u, x1, moe_c, moe_l, mod, dw, dwb, lng, lnb, w2, b2, gffn, wr):
    prev, nxt = _halo_specs(CONV_HALO, D)
    return pl.pallas_call(
        functools.partial(_mid_odd_kernel, l),
        grid=(NT,),
        in_specs=[_tile_spec(D), prev, nxt, _tile_spec(D)] + _half_specs(TM * TOK_ROWS, LANES) + [
                  _MOD_SPEC, _layer_spec((CONV_WIDTH, D), o),
                  _layer_spec((1, D), o), _layer_spec((1, D), o), _layer_spec((1, D), o),
                  _layer_spec((D, D), o), _layer_spec((1, D), o), _full_spec((DEPTH, D)),
                  _layer_spec((2, D, LANES), l)],
        out_specs=_tail_out_specs(),
        out_shape=_TAIL_OUT_SHAPES,
        scratch_shapes=[pltpu.VMEM((CONV_CHUNKS, CONV_PAD_ROWS, LANES), F32),
                        pltpu.VMEM((CONV_CHUNKS, TM, LANES), F32), pltpu.VMEM((TM, D), F32)] + _tail_scratch(),
        compiler_params=_params(),
        name="mid_odd",
    )(u, u, u, x1, moe_c, moe_l, mod, dw, dwb, lng, lnb, w2, b2, gffn, wr)


def _tile_tables_kernel(cnt_ref, te_c, tc_c, tw_c, te_l, tc_l, tw_l, nact_ref, second_ref, first_ref, wplan_ref):
    for p, (te_ref, tc_ref, tw_ref) in enumerate(((te_c, tc_c, tw_c), (te_l, tc_l, tw_l))):
        run = jnp.int32(0)
        ends = []
        for e in range(N_EXPERTS):
            first_ref[p * N_EXPERTS + e] = run
            run = run + (cnt_ref[p, e] + MOE_TM - 1) // MOE_TM
            ends.append(run)
        nact_ref[p] = run

        nxt1, nxt2 = jnp.int32(0), jnp.int32(0)
        plans = []
        for e in reversed(range(N_EXPERTS)):
            plans.append(nxt2)
            used = cnt_ref[p, e] > 0
            nxt2 = jnp.where(used, nxt1, nxt2)
            nxt1 = jnp.where(used, e + 1, nxt1)
        plans.reverse()
        second_ref[p] = nxt2
        order = jnp.int32(0)
        for e in range(N_EXPERTS):
            wplan_ref[p * N_EXPERTS + e] = order + (plans[e] << W_AHEAD_SHIFT)
            order = jnp.where(cnt_ref[p, e] > 0, jnp.where(order == W_BUFFERS - 1, 0, order + 1), order)

        def expert_of(tile, ends=ends):
            e = jnp.int32(0)
            for k in range(N_EXPERTS):
                e = e + jnp.where(tile >= ends[k], 1, 0)
            return e

        last_e = expert_of(run - 1)

        def tile_body(j, c, te_ref=te_ref, tc_ref=tc_ref, tw_ref=tw_ref, p=p, run=run, expert_of=expert_of,
                      last_e=last_e):
            e = jnp.minimum(expert_of(j), last_e)
            te_ref[j] = e
            chunk = jnp.where(j < run, j - first_ref[p * N_EXPERTS + e], 0)
            tc_ref[j] = chunk
            starts_expert = jnp.logical_and(j < run, chunk == 0)
            tw_ref[j] = wplan_ref[p * N_EXPERTS + e] + jnp.where(starts_expert, W_FIRST_UNIT, 0)
            return c

        lax.fori_loop(0, te_ref.shape[0], tile_body, 0)


def _tile_tables(cnt):
    smem = pl.BlockSpec(memory_space=pltpu.SMEM)
    shapes = [(MOE_TILES,)] * 6 + [(2,), (2,)]
    return pl.pallas_call(
        _tile_tables_kernel,
        in_specs=[smem],
        out_specs=[smem] * len(shapes),
        out_shape=[jax.ShapeDtypeStruct(s, I32) for s in shapes],
        scratch_shapes=[pltpu.SMEM((2 * N_EXPERTS,), I32), pltpu.SMEM((2 * N_EXPERTS,), I32)],
        name="tile_tables",
    )(cnt)


def _moe_kernel(l, pop, te_ref, tc_ref, tw_ref, nact_ref, second_ref, list_ref, h_ref, gate_ref,
                w1_hbm, w3_hbm, w2_hbm, o_ref, xs_ref, ys_ref, w1_buf, w3_buf, w2_buf, w_sem):
    j = pl.program_id(0)
    nact = nact_ref[pop]
    last_tile = te_ref.shape[0] - 1
    tile_rows = MOE_TM * TOK_ROWS

    def weight_copies(expert, buf):
        return [pltpu.make_async_copy(hbm.at[l, expert], vmem.at[buf], w_sem.at[buf, k])
                for k, (hbm, vmem) in enumerate(((w1_hbm, w1_buf), (w3_hbm, w3_buf), (w2_hbm, w2_buf)))]

    def list_base(tile):
        expert = te_ref[tile]
        block = expert >> LIST_PER_BLOCK_LOG2
        in_block = expert & ((1 << LIST_PER_BLOCK_LOG2) - 1)
        return (block * LIST_ROWS + tc_ref[tile] * (MOE_TM // LIST_SPAN)) * LANES + in_block * LIST_SPAN

    def tok_rows(base, r):
        first = list_ref[base + (r // LIST_SPAN) * LANES + r % LIST_SPAN]
        return pl.ds(pl.multiple_of(first, TOK_ROWS), TOK_ROWS)

    def buf_rows(buf, r):
        return pl.ds(pl.multiple_of(buf * tile_rows + r * TOK_ROWS, TOK_ROWS), TOK_ROWS)

    def gather(tile, buf):
        base = list_base(tile)
        for r in range(MOE_TM):
            xs_ref[buf_rows(buf, r), :] = h_ref[tok_rows(base, r), :]

    tile = jnp.minimum(j, last_tile)
    plan = tw_ref[tile]
    wbuf = plan & (W_FIRST_UNIT - 1)
    starts_expert = jnp.logical_and(plan & W_FIRST_UNIT != 0, j <= last_tile)
    ahead_expert = (plan >> W_AHEAD_SHIFT) - 1
    ahead_buf = jnp.where(wbuf == 0, W_BUFFERS - 1, wbuf - 1)

    @pl.when(j == 0)
    def _():
        for cp in weight_copies(te_ref[0], 0):
            cp.start()

        @pl.when(second_ref[pop] > 0)
        def _():
            for cp in weight_copies(second_ref[pop] - 1, 1):
                cp.start()

        o_ref[...] = jnp.zeros(o_ref.shape, F32)
        ys_ref[...] = jnp.zeros(ys_ref.shape, F32)
        gather(0, 0)

    @pl.when(starts_expert)
    def _():
        for cp in weight_copies(te_ref[tile], wbuf):
            cp.wait()

        @pl.when(ahead_expert >= 0)
        def _():
            for cp in weight_copies(ahead_expert, ahead_buf):
                cp.start()

    @pl.when(j <= nact)
    def _():
        cur = j & 1
        w1_ref, w3_ref, w2_ref = w1_buf.at[wbuf], w3_buf.at[wbuf], w2_buf.at[wbuf]
        x = jnp.concatenate([xs_ref[pl.ds(cur * tile_rows + s, MOE_TM, stride=TOK_ROWS), :]
                             for s in range(TOK_ROWS)], axis=1).astype(BF16)
        gather(jnp.minimum(j + 1, last_tile), 1 - cur)

        prev_base = list_base(jnp.maximum(j - 1, 0))
        for r0 in range(0, MOE_TM, MOVE_BATCH):
            dst = [tok_rows(prev_base, r0 + k) for k in range(MOVE_BATCH)]
            vals = [o_ref[dst[k], :] + ys_ref[buf_rows(1 - cur, r0 + k), :] for k in range(MOVE_BATCH)]
            for k in reversed(range(MOVE_BATCH)):
                o_ref[dst[k], :] = vals[k]

        hg = jnp.dot(x, w1_ref[...].astype(BF16), preferred_element_type=F32)
        hu = jnp.dot(x, w3_ref[...].astype(BF16), preferred_element_type=F32)
        y = _dot(_silu(hg) * hu, w2_ref[...])

        first_row = pl.multiple_of(tc_ref[tile] * (MOE_TM // LIST_SPAN), MOE_TM // LIST_SPAN)
        spread = jnp.concatenate(
            [jnp.broadcast_to(gate_ref[pl.ds(first_row + k, 1), :], (LIST_SPAN, LIST_COLS))
             for k in range(MOE_TM // LIST_SPAN)], axis=0)
        want = te_ref[tile] * LIST_SPAN + (lax.broadcasted_iota(I32, (MOE_TM, 1), 0) & (LIST_SPAN - 1))
        hit = jnp.logical_and(lax.broadcasted_iota(I32, (MOE_TM, LIST_COLS), 1) == want, j < nact)
        y = y * jnp.sum(jnp.where(hit, spread, 0.0), axis=1, keepdims=True)
        for s in range(TOK_ROWS):
            ys_ref[pl.ds(cur * tile_rows + s, MOE_TM, stride=TOK_ROWS), :] = y[:, s * LANES:(s + 1) * LANES]


def _moe(l, pop, h_tt, te, tc, tw, nact, second, rows, gate, w1, w3, w2):
    tp = T_HALF
    blk = pop
    n_tiles = te.shape[0]
    hbm = pl.BlockSpec(memory_space=pl.ANY)
    grid_spec = pltpu.PrefetchScalarGridSpec(
        num_scalar_prefetch=6,
        grid=(nact[pop] + 1,),
        in_specs=[
            pl.BlockSpec((tp * TOK_ROWS, LANES), lambda j, *_: (blk, 0), pipeline_mode=pl.Buffered(1)),
            pl.BlockSpec((LIST_ROWS, LIST_COLS), lambda j, *_: (0, 0), pipeline_mode=pl.Buffered(1)),
            hbm, hbm, hbm,
        ],
        out_specs=pl.BlockSpec((tp * TOK_ROWS, LANES), lambda j, *_: (0, 0), pipeline_mode=pl.Buffered(1)),
        scratch_shapes=[pltpu.VMEM((2 * MOE_TM * TOK_ROWS, LANES), F32),
                        pltpu.VMEM((2 * MOE_TM * TOK_ROWS, LANES), F32),
                        pltpu.VMEM((W_BUFFERS, D, EXPERT_HIDDEN), F32),
                        pltpu.VMEM((W_BUFFERS, D, EXPERT_HIDDEN), F32),
                        pltpu.VMEM((W_BUFFERS, EXPERT_HIDDEN, D), F32),
                        pltpu.SemaphoreType.DMA((W_BUFFERS, 3))],
    )
    return pl.pallas_call(
        functools.partial(_moe_kernel, l, pop),
        grid_spec=grid_spec,
        out_shape=jax.ShapeDtypeStruct((tp * TOK_ROWS, LANES), F32),
        compiler_params=_params(),
        name="moe",
    )(te, tc, tw, nact, second, rows, h_tt, gate, w1, w3, w2)


def _moe_all(l, h_tt, cnt, rows_c, gate_c, rows_l, gate_l, w1, w3, w2):
    te_c, tc_c, tw_c, te_l, tc_l, tw_l, nact, second = _tile_tables(cnt)
    out_c = _moe(l, 0, h_tt, te_c, tc_c, tw_c, nact, second, rows_c.reshape(-1), gate_c, w1, w3, w2)
    out_l = _moe(l, 1, h_tt, te_l, tc_l, tw_l, nact, second, rows_l.reshape(-1), gate_l, w1, w3, w2)
    return out_c, out_l


def _final_kernel(l, x1_ref, moec_ref, moel_ref, mod_ref, g_ref, oc_ref, ol_ref):
    i = pl.program_id(0) * PAIR
    r = _mod_row(i)

    def body(half, pop):
        x = _after_moe(l, r, x1_ref, (moec_ref, moel_ref)[half], mod_ref)
        ms = jnp.mean(x * x, axis=-1, keepdims=True)
        (oc_ref, ol_ref)[pop][...] = x * lax.rsqrt(ms + EPS) * g_ref[...]

    pl.when(i < MOE_SPLIT)(functools.partial(body, 0, 0))
    pl.when(jnp.logical_and(i >= MOE_SPLIT, i < NT_CTX))(functools.partial(body, 1, 0))
    pl.when(i >= NT_CTX)(functools.partial(body, 1, 1))


def _final(l, x1, moe_c, moe_l, mod, g):
    return pl.pallas_call(
        functools.partial(_final_kernel, l),
        grid=(NT // PAIR,),
        in_specs=[_pair_spec(D)] + _pair_half_specs() + [_MOD_SPEC, _full_spec((1, D))],
        out_specs=_pair_pop_specs(D),
        out_shape=[jax.ShapeDtypeStruct((T_CTX, D), F32), jax.ShapeDtypeStruct((T_LAT, D), F32)],
        compiler_params=_params(),
        name="final_norm",
    )(x1, moe_c, moe_l, mod, g)


def _rope_tables():
    t = np.arange(LAT_LEN)
    row = (t // GRID_W).astype(np.float32)
    col = (t % GRID_W).astype(np.float32)
    inv = np.float32(ROPE_BASE) ** (-np.arange(0, ROPE_AXIS_DIM, 2, dtype=np.float32) / np.float32(ROPE_AXIS_DIM))

    def table(p):
        ang = (p[:, None] * inv[None, :]).astype(np.float32)
        ang = np.concatenate([ang, ang], axis=-1)
        return np.cos(ang).astype(np.float32), np.sin(ang).astype(np.float32)

    (cr, sr), (cc, sc) = table(row), table(col)
    cos = np.concatenate([cr, cc, cr, cc], axis=-1)
    sin = np.concatenate([sr, sc, sr, sc], axis=-1)
    return jnp.asarray(cos), jnp.asarray(sin)


def _router_weights(w_grp, w_exp):
    we = jnp.transpose(w_exp, (0, 2, 1, 3)).reshape(DEPTH, D, N_GROUPS * EPG)
    pad = jnp.zeros((DEPTH, D, LANES - N_GROUPS - N_GROUPS * EPG), F32)
    wr = jnp.concatenate([w_grp, we, pad], axis=-1)
    hi = lax.bitcast_convert_type(lax.bitcast_convert_type(wr, jnp.uint32) & jnp.uint32(0xFFFF0000), F32)
    return jnp.stack([hi.astype(BF16), (wr - hi).astype(BF16)], axis=1)


def kernel(x_prompt, x_sample, cache_k, cache_v, c, c_ctx, w_ada, b_ada, norm_mix_g, norm_ffn_g, w_in_ab, pool_w, pool_scale, attn_sink, w_out_ab, conv_w1, conv_b1, conv_dw, conv_dw_b, conv_ln_g, conv_ln_b, conv_w2, conv_b2, router_grp, router_exp, moe_w1, moe_w3, moe_w2, final_g):
    xc = x_prompt.reshape(T_CTX, D)
    xl = x_sample.reshape(T_LAT, D)
    cond8 = jnp.concatenate([c_ctx[None, :], c, jnp.zeros((SUBLANES - 1 - N_LAT_SEQ, D), F32)], axis=0)
    mod = _modulation(cond8, w_ada, b_ada)
    cos_t, sin_t = _rope_tables()
    wr = _router_weights(router_grp, router_exp)
    n_even, n_odd = w_in_ab.shape[0], conv_w1.shape[0]

    a, q, k_c, k_l, v_c, v_l, state_k, state_v = _in_even(0, 0, xc, xl, mod, norm_mix_g, w_in_ab, cos_t, sin_t)
    yb_c = _attn_ctx(0, attn_sink, q, k_c, v_c)
    yb_l = _attn_lat(0, attn_sink, q, k_l, v_l, cache_k, cache_v)
    x1, h2, *routing = _mid_even(0, 0, a, yb_c, yb_l, xc, xl, mod, pool_w,
                                 pool_scale.reshape(n_even, 1, POOL_DIM), w_out_ab, norm_ffn_g, wr)
    moe0 = _moe_all(0, h2, *routing, moe_w1, moe_w3, moe_w2)

    vec = lambda p: p.reshape(n_odd, 1, -1)
    u = _in_odd(1, 0, x1, *moe0, mod, norm_mix_g, conv_w1, vec(conv_b1))
    x3, h2b, *routing = _mid_odd(1, 0, u, x1, *moe0, mod, conv_dw, vec(conv_dw_b), vec(conv_ln_g),
                                 vec(conv_ln_b), conv_w2, vec(conv_b2), norm_ffn_g, wr)
    moe1 = _moe_all(1, h2b, *routing, moe_w1, moe_w3, moe_w2)
    y_c, y_l = _final(1, x3, *moe1, mod, final_g.reshape(1, D))

    y_prompt = y_c.reshape(N_CTX_SEQ, CTX_LEN, D)
    y_sample = y_l.reshape(N_LAT_SEQ, LAT_LEN, D)
    return (y_prompt, y_sample, state_k, state_v)
```
